```python
import math
import jax, jax.numpy as jnp
from jax import lax
import numpy as np

D_MODEL = 1024
BATCH = 8
SEQ = 16384
DEPTH = 4

N_META = 16
N_HEADS = 16
N_KV_HEADS = 4
HEAD_DIM = 64
Q_GROUP = N_HEADS // N_KV_HEADS
WINDOW = 128
BLOCK = 128
SSM_WIDTH = D_MODEL // 2
SSM_GROUP_CH = 16
SSM_GROUPS = SSM_WIDTH // SSM_GROUP_CH
SSM_STATE = 64
N_DIR = 2
DT_MIN = 1e-3
DT_MAX = 1e-1
D_FF = 2816
EPS = 1e-6
NEG = -1e30

ATTN_Q = N_HEADS * HEAD_DIM
ATTN_KV = N_KV_HEADS * HEAD_DIM
SPLITS = [ATTN_Q, ATTN_Q + ATTN_KV, ATTN_Q + 2 * ATTN_KV,
          ATTN_Q + 2 * ATTN_KV + SSM_WIDTH,
          ATTN_Q + 2 * ATTN_KV + SSM_WIDTH + D_MODEL]
IN_COLS = ATTN_Q + 2 * ATTN_KV + SSM_WIDTH + 2 * D_MODEL

kernel_name = "hybrid_s5_swa_macaron_encoder"


def alibi_slopes():
    s = 2.0 ** (-8.0 * np.arange(1, N_HEADS + 1) / N_HEADS)
    return jnp.asarray(s.reshape(N_KV_HEADS, Q_GROUP), dtype=jnp.float32)


def rmsnorm(x, g):
    xf = x.astype(jnp.float32)
    y = xf * lax.rsqrt(jnp.mean(xf * xf, axis=-1, keepdims=True) + EPS)
    return (y * g.astype(jnp.float32)).astype(x.dtype)


def swiglu(h, w_gate, w_up, w_down):
    return (jax.nn.silu(h @ w_gate) * (h @ w_up)) @ w_down


def s5_direction(ug, lam_re, lam_im, log_dt, b_re, b_im, c_re, c_im, reverse):
    f32 = jnp.float32
    lam = lax.complex(lam_re.astype(f32), lam_im.astype(f32))
    dt = jnp.exp(log_dt.astype(f32))[:, None]
    lam_bar = jnp.exp(lam * dt)
    b_bar = ((lam_bar - 1.0) / lam)[:, :, None] * lax.complex(b_re.astype(f32), b_im.astype(f32))
    c_mat = lax.complex(c_re.astype(f32), c_im.astype(f32))
    bu = jnp.einsum('blgc,gpc->blgp', ug, b_bar)
    a = jnp.broadcast_to(lam_bar, bu.shape)

    def combine(e1, e2):
        a1, b1 = e1
        a2, b2 = e2
        return a1 * a2, a2 * b1 + b2

    _, states = lax.associative_scan(combine, (a, bu), reverse=reverse, axis=1)
    return jnp.einsum('blgp,gcp->blgc', states, c_mat).real


def ssm_branch(u, lam_re, lam_im, log_dt, b_re, b_im, c_re, c_im, d, w_glu):
    B, L, _ = u.shape
    uf = u.astype(jnp.float32)
    ug = uf.reshape(B, L, SSM_GROUPS, SSM_GROUP_CH)
    y = jnp.zeros_like(ug)
    for dr in range(N_DIR):
        y = y + s5_direction(ug, lam_re[dr], lam_im[dr], log_dt[dr], b_re[dr], b_im[dr],
                             c_re[dr], c_im[dr], reverse=(dr == 1))
    y = y.reshape(B, L, SSM_WIDTH) + d.astype(jnp.float32) * uf
    z = jax.nn.gelu(y).astype(u.dtype)
    return z * jax.nn.sigmoid(z @ w_glu)


def windowed_gqa(q, k, v, sink):
    B, L = q.shape[:2]
    pad = BLOCK - N_META
    Lp = L + pad
    nb = Lp // BLOCK
    k_meta, v_meta = k[:, :N_META], v[:, :N_META]

    def to_blocks(t):
        tp = jnp.pad(t, ((0, 0), (pad, 0), (0, 0), (0, 0)))
        return tp.reshape(B, nb, BLOCK, *t.shape[2:])

    def band(t):
        tb = jnp.pad(to_blocks(t), ((0, 0), (1, 1), (0, 0), (0, 0), (0, 0)))
        return jnp.concatenate([tb[:, :-2], tb[:, 1:-1], tb[:, 2:]], axis=2)

    qb = to_blocks(q).reshape(B, nb, BLOCK, N_KV_HEADS, Q_GROUP, HEAD_DIM)
    kband, vband = band(k), band(v)
    scale = HEAD_DIM ** -0.5

    qi = jnp.arange(BLOCK)
    sj = jnp.arange(3 * BLOCK)
    dist = jnp.abs(qi[:, None] + BLOCK - sj[None, :])
    kpos = (jnp.arange(nb)[:, None] - 1) * BLOCK + sj[None, :]
    kvalid = (kpos >= BLOCK) & (kpos < Lp)
    valid = (dist <= WINDOW)[None] & kvalid[:, None, :]
    slopes = alibi_slopes()

    s_band = jnp.einsum('bnqkgd,bnskd->bnkgqs', qb, kband,
                        preferred_element_type=jnp.float32) * scale
    s_band = s_band - slopes[:, :, None, None] * dist.astype(jnp.float32)
    s_band = jnp.where(valid[None, :, None, None], s_band, NEG)
    s_meta = jnp.einsum('bnqkgd,bmkd->bnkgqm', qb, k_meta,
                        preferred_element_type=jnp.float32) * scale
    s_sink = jnp.broadcast_to(sink.astype(jnp.float32).reshape(N_KV_HEADS, Q_GROUP)[None, None, :, :, None, None],
                              (B, nb, N_KV_HEADS, Q_GROUP, BLOCK, 1))
    p = jax.nn.softmax(jnp.concatenate([s_band, s_meta, s_sink], axis=-1), axis=-1)
    p_band = p[..., :3 * BLOCK].astype(v.dtype)
    p_meta = p[..., 3 * BLOCK:3 * BLOCK + N_META].astype(v.dtype)
    out = (jnp.einsum('bnkgqs,bnskd->bnqkgd', p_band, vband)
           + jnp.einsum('bnkgqm,bmkd->bnqkgd', p_meta, v_meta))
    return out.reshape(B, Lp, N_HEADS * HEAD_DIM)[:, pad:]


def mixer(h, w_in, lam_re, lam_im, log_dt, b_re, b_im, c_re, c_im, d, w_glu, sink,
          w_branch_ssm, w_branch_attn, w_out):
    B, L, _ = h.shape
    proj = h @ w_in
    q, k, v, u, g_s, g_a = jnp.split(proj, SPLITS, axis=-1)
    y_attn = windowed_gqa(q.reshape(B, L, N_HEADS, HEAD_DIM),
                          k.reshape(B, L, N_KV_HEADS, HEAD_DIM),
                          v.reshape(B, L, N_KV_HEADS, HEAD_DIM), sink)
    y_ssm = ssm_branch(u, lam_re, lam_im, log_dt, b_re, b_im, c_re, c_im, d, w_glu)
    merged = (jax.nn.sigmoid(g_s) * (y_ssm @ w_branch_ssm)
              + jax.nn.sigmoid(g_a) * (y_attn @ w_branch_attn))
    return merged @ w_out


def _fwd_setup_inputs(seed: int = 0) -> dict:
    key = jax.random.key(seed)
    ks = jax.random.split(key, 32)
    f32 = jnp.float32

    def nrm(k, shape, fan_in):
        return jax.random.normal(k, shape, f32) * fan_in ** -0.5

    def gain(k, shape):
        return 1.0 + 0.05 * jax.random.normal(k, shape, f32)

    G, P, CH = SSM_GROUPS, SSM_STATE, SSM_GROUP_CH
    n = jnp.arange(P, dtype=f32)
    return {
        "x": jax.random.normal(ks[0], (BATCH, SEQ, D_MODEL), f32),
        "meta_tokens": jax.random.normal(ks[1], (N_META, D_MODEL), f32),
        "ffn1_norm": gain(ks[2], (DEPTH, D_MODEL)),
        "ffn1_w_gate": nrm(ks[3], (DEPTH, D_MODEL, D_FF), D_MODEL),
        "ffn1_w_up": nrm(ks[4], (DEPTH, D_MODEL, D_FF), D_MODEL),
        "ffn1_w_down": nrm(ks[5], (DEPTH, D_FF, D_MODEL), D_FF),
        "mix_norm": gain(ks[6], (DEPTH, D_MODEL)),
        "w_in": nrm(ks[7], (DEPTH, D_MODEL, IN_COLS), D_MODEL),
        "ssm_lam_re": -0.5 + 0.01 * jax.random.normal(ks[8], (DEPTH, N_DIR, G, P), f32),
        "ssm_lam_im": math.pi * n + 0.01 * jax.random.normal(ks[9], (DEPTH, N_DIR, G, P), f32),
        "ssm_log_dt": jax.random.uniform(ks[10], (DEPTH, N_DIR, G), f32,
                                         minval=math.log(DT_MIN), maxval=math.log(DT_MAX)),
        "ssm_b_re": nrm(ks[11], (DEPTH, N_DIR, G, P, CH), 2 * CH),
        "ssm_b_im": nrm(ks[12], (DEPTH, N_DIR, G, P, CH), 2 * CH),
        "ssm_c_re": nrm(ks[13], (DEPTH, N_DIR, G, CH, P), 2 * P),
        "ssm_c_im": nrm(ks[14], (DEPTH, N_DIR, G, CH, P), 2 * P),
        "ssm_d": jax.random.normal(ks[15], (DEPTH, SSM_WIDTH), f32),
        "ssm_w_glu": nrm(ks[16], (DEPTH, SSM_WIDTH, SSM_WIDTH), SSM_WIDTH),
        "attn_sink": 0.5 * jax.random.normal(ks[17], (DEPTH, N_HEADS), f32),
        "w_branch_ssm": nrm(ks[18], (DEPTH, SSM_WIDTH, D_MODEL), SSM_WIDTH),
        "w_branch_attn": nrm(ks[19], (DEPTH, ATTN_Q, D_MODEL), ATTN_Q),
        "w_out": nrm(ks[20], (DEPTH, D_MODEL, D_MODEL), D_MODEL),
        "ffn2_norm": gain(ks[21], (DEPTH, D_MODEL)),
        "ffn2_w_gate": nrm(ks[22], (DEPTH, D_MODEL, D_FF), D_MODEL),
        "ffn2_w_up": nrm(ks[23], (DEPTH, D_MODEL, D_FF), D_MODEL),
        "ffn2_w_down": nrm(ks[24], (DEPTH, D_FF, D_MODEL), D_FF),
        "final_norm": gain(ks[25], (D_MODEL,)),
    }


def _fwd_reference(x, meta_tokens, ffn1_norm, ffn1_w_gate, ffn1_w_up, ffn1_w_down,
              mix_norm, w_in, ssm_lam_re, ssm_lam_im, ssm_log_dt, ssm_b_re, ssm_b_im,
              ssm_c_re, ssm_c_im, ssm_d, ssm_w_glu, attn_sink, w_branch_ssm,
              w_branch_attn, w_out, ffn2_norm, ffn2_w_gate, ffn2_w_up, ffn2_w_down,
              final_norm):
    B = x.shape[0]
    meta = jnp.broadcast_to(meta_tokens.astype(x.dtype)[None], (B, N_META, D_MODEL))
    h = jnp.concatenate([meta, x], axis=1)
    for l in range(DEPTH):
        h = h + 0.5 * swiglu(rmsnorm(h, ffn1_norm[l]), ffn1_w_gate[l], ffn1_w_up[l], ffn1_w_down[l])
        h = h + mixer(rmsnorm(h, mix_norm[l]), w_in[l], ssm_lam_re[l], ssm_lam_im[l],
                      ssm_log_dt[l], ssm_b_re[l], ssm_b_im[l], ssm_c_re[l], ssm_c_im[l],
                      ssm_d[l], ssm_w_glu[l], attn_sink[l], w_branch_ssm[l],
                      w_branch_attn[l], w_out[l])
        h = h + 0.5 * swiglu(rmsnorm(h, ffn2_norm[l]), ffn2_w_gate[l], ffn2_w_up[l], ffn2_w_down[l])
    h = rmsnorm(h, final_norm)
    return h[:, N_META:]


import jax as _jax
import jax.numpy as _jnp

TWIN_FORMAT = 'train_step'
FWD_PARAMS = ['x', 'meta_tokens', 'ffn1_norm', 'ffn1_w_gate', 'ffn1_w_up', 'ffn1_w_down', 'mix_norm', 'w_in', 'ssm_lam_re', 'ssm_lam_im', 'ssm_log_dt', 'ssm_b_re', 'ssm_b_im', 'ssm_c_re', 'ssm_c_im', 'ssm_d', 'ssm_w_glu', 'attn_sink', 'w_branch_ssm', 'w_branch_attn', 'w_out', 'ffn2_norm', 'ffn2_w_gate', 'ffn2_w_up', 'ffn2_w_down', 'final_norm']
TWIN_WEIGHTS = ['meta_tokens', 'ffn1_norm', 'ffn1_w_gate', 'ffn1_w_up', 'ffn1_w_down', 'mix_norm', 'w_in', 'ssm_lam_re', 'ssm_lam_im', 'ssm_log_dt', 'ssm_b_re', 'ssm_b_im', 'ssm_c_re', 'ssm_c_im', 'ssm_d', 'ssm_w_glu', 'attn_sink', 'w_branch_ssm', 'w_branch_attn', 'w_out', 'ffn2_norm', 'ffn2_w_gate', 'ffn2_w_up', 'ffn2_w_down', 'final_norm']
TWIN_DIFF_INPUT = 'x'
TWIN_INPUTS = ['x', 'meta_tokens', 'ffn1_norm', 'ffn1_w_gate', 'ffn1_w_up', 'ffn1_w_down', 'mix_norm', 'w_in', 'ssm_lam_re', 'ssm_lam_im', 'ssm_log_dt', 'ssm_b_re', 'ssm_b_im', 'ssm_c_re', 'ssm_c_im', 'ssm_d', 'ssm_w_glu', 'attn_sink', 'w_branch_ssm', 'w_branch_attn', 'w_out', 'ffn2_norm', 'ffn2_w_gate', 'ffn2_w_up', 'ffn2_w_down', 'final_norm', 'loss_target', 'm_meta_tokens', 'm_ffn1_norm', 'm_ffn1_w_gate', 'm_ffn1_w_up', 'm_ffn1_w_down', 'm_mix_norm', 'm_w_in', 'm_ssm_lam_re', 'm_ssm_lam_im', 'm_ssm_log_dt', 'm_ssm_b_re', 'm_ssm_b_im', 'm_ssm_c_re', 'm_ssm_c_im', 'm_ssm_d', 'm_ssm_w_glu', 'm_attn_sink', 'm_w_branch_ssm', 'm_w_branch_attn', 'm_w_out', 'm_ffn2_norm', 'm_ffn2_w_gate', 'm_ffn2_w_up', 'm_ffn2_w_down', 'm_final_norm', 'v_meta_tokens', 'v_ffn1_norm', 'v_ffn1_w_gate', 'v_ffn1_w_up', 'v_ffn1_w_down', 'v_mix_norm', 'v_w_in', 'v_ssm_lam_re', 'v_ssm_lam_im', 'v_ssm_log_dt', 'v_ssm_b_re', 'v_ssm_b_im', 'v_ssm_c_re', 'v_ssm_c_im', 'v_ssm_d', 'v_ssm_w_glu', 'v_attn_sink', 'v_w_branch_ssm', 'v_w_branch_attn', 'v_w_out', 'v_ffn2_norm', 'v_ffn2_w_gate', 'v_ffn2_w_up', 'v_ffn2_w_down', 'v_final_norm']
TWIN_OUTPUTS = ['loss', 'grad_x', 'grad_meta_tokens', 'grad_ffn1_norm', 'grad_ffn1_w_gate', 'grad_ffn1_w_up', 'grad_ffn1_w_down', 'grad_mix_norm', 'grad_w_in', 'grad_ssm_lam_re', 'grad_ssm_lam_im', 'grad_ssm_log_dt', 'grad_ssm_b_re', 'grad_ssm_b_im', 'grad_ssm_c_re', 'grad_ssm_c_im', 'grad_ssm_d', 'grad_ssm_w_glu', 'grad_attn_sink', 'grad_w_branch_ssm', 'grad_w_branch_attn', 'grad_w_out', 'grad_ffn2_norm', 'grad_ffn2_w_gate', 'grad_ffn2_w_up', 'grad_ffn2_w_down', 'grad_final_norm', 'delta_meta_tokens', 'delta_ffn1_norm', 'delta_ffn1_w_gate', 'delta_ffn1_w_up', 'delta_ffn1_w_down', 'delta_mix_norm', 'delta_w_in', 'delta_ssm_lam_re', 'delta_ssm_lam_im', 'delta_ssm_log_dt', 'delta_ssm_b_re', 'delta_ssm_b_im', 'delta_ssm_c_re', 'delta_ssm_c_im', 'delta_ssm_d', 'delta_ssm_w_glu', 'delta_attn_sink', 'delta_w_branch_ssm', 'delta_w_branch_attn', 'delta_w_out', 'delta_ffn2_norm', 'delta_ffn2_w_gate', 'delta_ffn2_w_up', 'delta_ffn2_w_down', 'delta_final_norm', 'new_m_meta_tokens', 'new_m_ffn1_norm', 'new_m_ffn1_w_gate', 'new_m_ffn1_w_up', 'new_m_ffn1_w_down', 'new_m_mix_norm', 'new_m_w_in', 'new_m_ssm_lam_re', 'new_m_ssm_lam_im', 'new_m_ssm_log_dt', 'new_m_ssm_b_re', 'new_m_ssm_b_im', 'new_m_ssm_c_re', 'new_m_ssm_c_im', 'new_m_ssm_d', 'new_m_ssm_w_glu', 'new_m_attn_sink', 'new_m_w_branch_ssm', 'new_m_w_branch_attn', 'new_m_w_out', 'new_m_ffn2_norm', 'new_m_ffn2_w_gate', 'new_m_ffn2_w_up', 'new_m_ffn2_w_down', 'new_m_final_norm', 'new_v_meta_tokens', 'new_v_ffn1_norm', 'new_v_ffn1_w_gate', 'new_v_ffn1_w_up', 'new_v_ffn1_w_down', 'new_v_mix_norm', 'new_v_w_in', 'new_v_ssm_lam_re', 'new_v_ssm_lam_im', 'new_v_ssm_log_dt', 'new_v_ssm_b_re', 'new_v_ssm_b_im', 'new_v_ssm_c_re', 'new_v_ssm_c_im', 'new_v_ssm_d', 'new_v_ssm_w_glu', 'new_v_attn_sink', 'new_v_w_branch_ssm', 'new_v_w_branch_attn', 'new_v_w_out', 'new_v_ffn2_norm', 'new_v_ffn2_w_gate', 'new_v_ffn2_w_up', 'new_v_ffn2_w_down', 'new_v_final_norm']
TWIN_LEAF_KINDS = {'loss': 'loss', 'grad_x': 'grad_x', 'grad_meta_tokens': 'grad_w', 'grad_ffn1_norm': 'grad_w', 'grad_ffn1_w_gate': 'grad_w', 'grad_ffn1_w_up': 'grad_w', 'grad_ffn1_w_down': 'grad_w', 'grad_mix_norm': 'grad_w', 'grad_w_in': 'grad_w', 'grad_ssm_lam_re': 'grad_w', 'grad_ssm_lam_im': 'grad_w', 'grad_ssm_log_dt': 'grad_w', 'grad_ssm_b_re': 'grad_w', 'grad_ssm_b_im': 'grad_w', 'grad_ssm_c_re': 'grad_w', 'grad_ssm_c_im': 'grad_w', 'grad_ssm_d': 'grad_w', 'grad_ssm_w_glu': 'grad_w', 'grad_attn_sink': 'grad_w', 'grad_w_branch_ssm': 'grad_w', 'grad_w_branch_attn': 'grad_w', 'grad_w_out': 'grad_w', 'grad_ffn2_norm': 'grad_w', 'grad_ffn2_w_gate': 'grad_w', 'grad_ffn2_w_up': 'grad_w', 'grad_ffn2_w_down': 'grad_w', 'grad_final_norm': 'grad_w', 'delta_meta_tokens': 'delta_w', 'delta_ffn1_norm': 'delta_w', 'delta_ffn1_w_gate': 'delta_w', 'delta_ffn1_w_up': 'delta_w', 'delta_ffn1_w_down': 'delta_w', 'delta_mix_norm': 'delta_w', 'delta_w_in': 'delta_w', 'delta_ssm_lam_re': 'delta_w', 'delta_ssm_lam_im': 'delta_w', 'delta_ssm_log_dt': 'delta_w', 'delta_ssm_b_re': 'delta_w', 'delta_ssm_b_im': 'delta_w', 'delta_ssm_c_re': 'delta_w', 'delta_ssm_c_im': 'delta_w', 'delta_ssm_d': 'delta_w', 'delta_ssm_w_glu': 'delta_w', 'delta_attn_sink': 'delta_w', 'delta_w_branch_ssm': 'delta_w', 'delta_w_branch_attn': 'delta_w', 'delta_w_out': 'delta_w', 'delta_ffn2_norm': 'delta_w', 'delta_ffn2_w_gate': 'delta_w', 'delta_ffn2_w_up': 'delta_w', 'delta_ffn2_w_down': 'delta_w', 'delta_final_norm': 'delta_w', 'new_m_meta_tokens': 'new_m', 'new_m_ffn1_norm': 'new_m', 'new_m_ffn1_w_gate': 'new_m', 'new_m_ffn1_w_up': 'new_m', 'new_m_ffn1_w_down': 'new_m', 'new_m_mix_norm': 'new_m', 'new_m_w_in': 'new_m', 'new_m_ssm_lam_re': 'new_m', 'new_m_ssm_lam_im': 'new_m', 'new_m_ssm_log_dt': 'new_m', 'new_m_ssm_b_re': 'new_m', 'new_m_ssm_b_im': 'new_m', 'new_m_ssm_c_re': 'new_m', 'new_m_ssm_c_im': 'new_m', 'new_m_ssm_d': 'new_m', 'new_m_ssm_w_glu': 'new_m', 'new_m_attn_sink': 'new_m', 'new_m_w_branch_ssm': 'new_m', 'new_m_w_branch_attn': 'new_m', 'new_m_w_out': 'new_m', 'new_m_ffn2_norm': 'new_m', 'new_m_ffn2_w_gate': 'new_m', 'new_m_ffn2_w_up': 'new_m', 'new_m_ffn2_w_down': 'new_m', 'new_m_final_norm': 'new_m', 'new_v_meta_tokens': 'new_v', 'new_v_ffn1_norm': 'new_v', 'new_v_ffn1_w_gate': 'new_v', 'new_v_ffn1_w_up': 'new_v', 'new_v_ffn1_w_down': 'new_v', 'new_v_mix_norm': 'new_v', 'new_v_w_in': 'new_v', 'new_v_ssm_lam_re': 'new_v', 'new_v_ssm_lam_im': 'new_v', 'new_v_ssm_log_dt': 'new_v', 'new_v_ssm_b_re': 'new_v', 'new_v_ssm_b_im': 'new_v', 'new_v_ssm_c_re': 'new_v', 'new_v_ssm_c_im': 'new_v', 'new_v_ssm_d': 'new_v', 'new_v_ssm_w_glu': 'new_v', 'new_v_attn_sink': 'new_v', 'new_v_w_branch_ssm': 'new_v', 'new_v_w_branch_attn': 'new_v', 'new_v_w_out': 'new_v', 'new_v_ffn2_norm': 'new_v', 'new_v_ffn2_w_gate': 'new_v', 'new_v_ffn2_w_up': 'new_v', 'new_v_ffn2_w_down': 'new_v', 'new_v_final_norm': 'new_v'}


def _forward(args):
    return _fwd_reference(*[args[k] for k in FWD_PARAMS])


def _output_shape():
    def fwd():
        inp = _fwd_setup_inputs(0)
        return _fwd_reference(*[inp[k] for k in FWD_PARAMS])
    out = _jax.eval_shape(fwd)
    return out.shape, out.dtype

N_MICROBATCH = 1
ADAM_LR = 0.001
ADAM_B1 = 0.9
ADAM_B2 = 0.999
ADAM_EPS = 1e-08
ADAM_WD = 0.01
ADAM_STEP = 10
PER_EXAMPLE_BATCH_AXIS = {'x': 0, 'loss_target': 0}
SHARED_INPUTS = []
_WEIGHT_DTYPES = {'meta_tokens': _jnp.float32, 'ffn1_norm': _jnp.float32, 'ffn1_w_gate': _jnp.float32, 'ffn1_w_up': _jnp.float32, 'ffn1_w_down': _jnp.float32, 'mix_norm': _jnp.float32, 'w_in': _jnp.float32, 'ssm_lam_re': _jnp.float32, 'ssm_lam_im': _jnp.float32, 'ssm_log_dt': _jnp.float32, 'ssm_b_re': _jnp.float32, 'ssm_b_im': _jnp.float32, 'ssm_c_re': _jnp.float32, 'ssm_c_im': _jnp.float32, 'ssm_d': _jnp.float32, 'ssm_w_glu': _jnp.float32, 'attn_sink': _jnp.float32, 'w_branch_ssm': _jnp.float32, 'w_branch_attn': _jnp.float32, 'w_out': _jnp.float32, 'ffn2_norm': _jnp.float32, 'ffn2_w_gate': _jnp.float32, 'ffn2_w_up': _jnp.float32, 'ffn2_w_down': _jnp.float32, 'final_norm': _jnp.float32}
MOMENT_SCALE = {'meta_tokens': 4.110977e-02, 'ffn1_norm': 1.557123e-01, 'ffn1_w_gate': 6.727148e-02, 'ffn1_w_up': 6.560586e-02, 'ffn1_w_down': 1.090144e-01, 'mix_norm': 1.249193e-01, 'w_in': 6.071800e-02, 'ssm_lam_re': 6.462192e-03, 'ssm_lam_im': 6.440493e-03, 'ssm_log_dt': 3.960290e+00, 'ssm_b_re': 3.721103e-03, 'ssm_b_im': 3.641175e-03, 'ssm_c_re': 7.322765e-03, 'ssm_c_im': 7.370692e-03, 'ssm_d': 1.464977e-01, 'ssm_w_glu': 3.311566e-02, 'attn_sink': 1.401026e-02, 'w_branch_ssm': 1.042521e-01, 'w_branch_attn': 8.755712e-02, 'w_out': 1.393123e-01, 'ffn2_norm': 1.434159e-01, 'ffn2_w_gate': 6.099468e-02, 'ffn2_w_up': 5.995700e-02, 'ffn2_w_down': 9.965840e-02, 'final_norm': 1.280611e+02}


def _to_microbatches(a, axis):
    t = _jnp.moveaxis(a, axis, 0)
    t = t.reshape((N_MICROBATCH, t.shape[0] // N_MICROBATCH) + t.shape[1:])
    return _jnp.moveaxis(t, 1, axis + 1)


def setup_inputs(seed: int = 0) -> dict:
    inp = _fwd_setup_inputs(seed)
    key = _jax.random.fold_in(_jax.random.key(seed), 7919)
    shape, _ = _output_shape()
    out = dict(inp)
    out["loss_target"] = _jax.random.normal(_jax.random.fold_in(key, 0), shape, _jnp.float32)
    for i, name in enumerate(TWIN_WEIGHTS):
        w = inp[name].astype(_jnp.float32)
        if MOMENT_SCALE is None:
            s = _jnp.sqrt(_jnp.mean(_jnp.square(w)) + 1e-30)
        else:
            s = MOMENT_SCALE[name]
        km, kv = _jax.random.split(_jax.random.fold_in(key, i + 1))
        out[name] = w
        out["m_" + name] = s * _jax.random.normal(km, w.shape, _jnp.float32)
        out["v_" + name] = (s * s) * _jax.random.uniform(kv, w.shape, _jnp.float32, 0.5, 1.5)
    if N_MICROBATCH > 1:
        for name, axis in PER_EXAMPLE_BATCH_AXIS.items():
            out[name] = _to_microbatches(out[name], axis)
    return {'x': out['x'], 'meta_tokens': out['meta_tokens'], 'ffn1_norm': out['ffn1_norm'], 'ffn1_w_gate': out['ffn1_w_gate'], 'ffn1_w_up': out['ffn1_w_up'], 'ffn1_w_down': out['ffn1_w_down'], 'mix_norm': out['mix_norm'], 'w_in': out['w_in'], 'ssm_lam_re': out['ssm_lam_re'], 'ssm_lam_im': out['ssm_lam_im'], 'ssm_log_dt': out['ssm_log_dt'], 'ssm_b_re': out['ssm_b_re'], 'ssm_b_im': out['ssm_b_im'], 'ssm_c_re': out['ssm_c_re'], 'ssm_c_im': out['ssm_c_im'], 'ssm_d': out['ssm_d'], 'ssm_w_glu': out['ssm_w_glu'], 'attn_sink': out['attn_sink'], 'w_branch_ssm': out['w_branch_ssm'], 'w_branch_attn': out['w_branch_attn'], 'w_out': out['w_out'], 'ffn2_norm': out['ffn2_norm'], 'ffn2_w_gate': out['ffn2_w_gate'], 'ffn2_w_up': out['ffn2_w_up'], 'ffn2_w_down': out['ffn2_w_down'], 'final_norm': out['final_norm'], 'loss_target': out['loss_target'], 'm_meta_tokens': out['m_meta_tokens'], 'm_ffn1_norm': out['m_ffn1_norm'], 'm_ffn1_w_gate': out['m_ffn1_w_gate'], 'm_ffn1_w_up': out['m_ffn1_w_up'], 'm_ffn1_w_down': out['m_ffn1_w_down'], 'm_mix_norm': out['m_mix_norm'], 'm_w_in': out['m_w_in'], 'm_ssm_lam_re': out['m_ssm_lam_re'], 'm_ssm_lam_im': out['m_ssm_lam_im'], 'm_ssm_log_dt': out['m_ssm_log_dt'], 'm_ssm_b_re': out['m_ssm_b_re'], 'm_ssm_b_im': out['m_ssm_b_im'], 'm_ssm_c_re': out['m_ssm_c_re'], 'm_ssm_c_im': out['m_ssm_c_im'], 'm_ssm_d': out['m_ssm_d'], 'm_ssm_w_glu': out['m_ssm_w_glu'], 'm_attn_sink': out['m_attn_sink'], 'm_w_branch_ssm': out['m_w_branch_ssm'], 'm_w_branch_attn': out['m_w_branch_attn'], 'm_w_out': out['m_w_out'], 'm_ffn2_norm': out['m_ffn2_norm'], 'm_ffn2_w_gate': out['m_ffn2_w_gate'], 'm_ffn2_w_up': out['m_ffn2_w_up'], 'm_ffn2_w_down': out['m_ffn2_w_down'], 'm_final_norm': out['m_final_norm'], 'v_meta_tokens': out['v_meta_tokens'], 'v_ffn1_norm': out['v_ffn1_norm'], 'v_ffn1_w_gate': out['v_ffn1_w_gate'], 'v_ffn1_w_up': out['v_ffn1_w_up'], 'v_ffn1_w_down': out['v_ffn1_w_down'], 'v_mix_norm': out['v_mix_norm'], 'v_w_in': out['v_w_in'], 'v_ssm_lam_re': out['v_ssm_lam_re'], 'v_ssm_lam_im': out['v_ssm_lam_im'], 'v_ssm_log_dt': out['v_ssm_log_dt'], 'v_ssm_b_re': out['v_ssm_b_re'], 'v_ssm_b_im': out['v_ssm_b_im'], 'v_ssm_c_re': out['v_ssm_c_re'], 'v_ssm_c_im': out['v_ssm_c_im'], 'v_ssm_d': out['v_ssm_d'], 'v_ssm_w_glu': out['v_ssm_w_glu'], 'v_attn_sink': out['v_attn_sink'], 'v_w_branch_ssm': out['v_w_branch_ssm'], 'v_w_branch_attn': out['v_w_branch_attn'], 'v_w_out': out['v_w_out'], 'v_ffn2_norm': out['v_ffn2_norm'], 'v_ffn2_w_gate': out['v_ffn2_w_gate'], 'v_ffn2_w_up': out['v_ffn2_w_up'], 'v_ffn2_w_down': out['v_ffn2_w_down'], 'v_final_norm': out['v_final_norm']}


def _loss(weights, diff, rest, loss_target):
    with _jax.named_scope("forward"):
        args = {**rest, TWIN_DIFF_INPUT: diff, **{k: w.astype(_WEIGHT_DTYPES[k]) for k, w in weights.items()}}
        y = _forward(args)
    with _jax.named_scope("loss_head"):
        err = _jnp.square(y.astype(_jnp.float32) - loss_target)
        return 0.5 * _jnp.sum(_jnp.mean(err, axis=-1)) if err.ndim else 0.5 * err


def _adamw(w, g, m, v):
    m = ADAM_B1 * m + (1.0 - ADAM_B1) * g
    v = ADAM_B2 * v + (1.0 - ADAM_B2) * _jnp.square(g)
    m_hat = m / (1.0 - ADAM_B1 ** ADAM_STEP)
    v_hat = v / (1.0 - ADAM_B2 ** ADAM_STEP)
    delta = -ADAM_LR * (m_hat / (_jnp.sqrt(v_hat) + ADAM_EPS) + ADAM_WD * w)
    return delta, m, v


def reference(x, meta_tokens, ffn1_norm, ffn1_w_gate, ffn1_w_up, ffn1_w_down, mix_norm, w_in, ssm_lam_re, ssm_lam_im, ssm_log_dt, ssm_b_re, ssm_b_im, ssm_c_re, ssm_c_im, ssm_d, ssm_w_glu, attn_sink, w_branch_ssm, w_branch_attn, w_out, ffn2_norm, ffn2_w_gate, ffn2_w_up, ffn2_w_down, final_norm, loss_target, m_meta_tokens, m_ffn1_norm, m_ffn1_w_gate, m_ffn1_w_up, m_ffn1_w_down, m_mix_norm, m_w_in, m_ssm_lam_re, m_ssm_lam_im, m_ssm_log_dt, m_ssm_b_re, m_ssm_b_im, m_ssm_c_re, m_ssm_c_im, m_ssm_d, m_ssm_w_glu, m_attn_sink, m_w_branch_ssm, m_w_branch_attn, m_w_out, m_ffn2_norm, m_ffn2_w_gate, m_ffn2_w_up, m_ffn2_w_down, m_final_norm, v_meta_tokens, v_ffn1_norm, v_ffn1_w_gate, v_ffn1_w_up, v_ffn1_w_down, v_mix_norm, v_w_in, v_ssm_lam_re, v_ssm_lam_im, v_ssm_log_dt, v_ssm_b_re, v_ssm_b_im, v_ssm_c_re, v_ssm_c_im, v_ssm_d, v_ssm_w_glu, v_attn_sink, v_w_branch_ssm, v_w_branch_attn, v_w_out, v_ffn2_norm, v_ffn2_w_gate, v_ffn2_w_up, v_ffn2_w_down, v_final_norm):
    given = dict(x=x, meta_tokens=meta_tokens, ffn1_norm=ffn1_norm, ffn1_w_gate=ffn1_w_gate, ffn1_w_up=ffn1_w_up, ffn1_w_down=ffn1_w_down, mix_norm=mix_norm, w_in=w_in, ssm_lam_re=ssm_lam_re, ssm_lam_im=ssm_lam_im, ssm_log_dt=ssm_log_dt, ssm_b_re=ssm_b_re, ssm_b_im=ssm_b_im, ssm_c_re=ssm_c_re, ssm_c_im=ssm_c_im, ssm_d=ssm_d, ssm_w_glu=ssm_w_glu, attn_sink=attn_sink, w_branch_ssm=w_branch_ssm, w_branch_attn=w_branch_attn, w_out=w_out, ffn2_norm=ffn2_norm, ffn2_w_gate=ffn2_w_gate, ffn2_w_up=ffn2_w_up, ffn2_w_down=ffn2_w_down, final_norm=final_norm, loss_target=loss_target, m_meta_tokens=m_meta_tokens, m_ffn1_norm=m_ffn1_norm, m_ffn1_w_gate=m_ffn1_w_gate, m_ffn1_w_up=m_ffn1_w_up, m_ffn1_w_down=m_ffn1_w_down, m_mix_norm=m_mix_norm, m_w_in=m_w_in, m_ssm_lam_re=m_ssm_lam_re, m_ssm_lam_im=m_ssm_lam_im, m_ssm_log_dt=m_ssm_log_dt, m_ssm_b_re=m_ssm_b_re, m_ssm_b_im=m_ssm_b_im, m_ssm_c_re=m_ssm_c_re, m_ssm_c_im=m_ssm_c_im, m_ssm_d=m_ssm_d, m_ssm_w_glu=m_ssm_w_glu, m_attn_sink=m_attn_sink, m_w_branch_ssm=m_w_branch_ssm, m_w_branch_attn=m_w_branch_attn, m_w_out=m_w_out, m_ffn2_norm=m_ffn2_norm, m_ffn2_w_gate=m_ffn2_w_gate, m_ffn2_w_up=m_ffn2_w_up, m_ffn2_w_down=m_ffn2_w_down, m_final_norm=m_final_norm, v_meta_tokens=v_meta_tokens, v_ffn1_norm=v_ffn1_norm, v_ffn1_w_gate=v_ffn1_w_gate, v_ffn1_w_up=v_ffn1_w_up, v_ffn1_w_down=v_ffn1_w_down, v_mix_norm=v_mix_norm, v_w_in=v_w_in, v_ssm_lam_re=v_ssm_lam_re, v_ssm_lam_im=v_ssm_lam_im, v_ssm_log_dt=v_ssm_log_dt, v_ssm_b_re=v_ssm_b_re, v_ssm_b_im=v_ssm_b_im, v_ssm_c_re=v_ssm_c_re, v_ssm_c_im=v_ssm_c_im, v_ssm_d=v_ssm_d, v_ssm_w_glu=v_ssm_w_glu, v_attn_sink=v_attn_sink, v_w_branch_ssm=v_w_branch_ssm, v_w_branch_attn=v_w_branch_attn, v_w_out=v_w_out, v_ffn2_norm=v_ffn2_norm, v_ffn2_w_gate=v_ffn2_w_gate, v_ffn2_w_up=v_ffn2_w_up, v_ffn2_w_down=v_ffn2_w_down, v_final_norm=v_final_norm)
    weights = {n: given[n] for n in TWIN_WEIGHTS}
    shared = {n: given[n] for n in SHARED_INPUTS}
    per_example = {n: given[n] for n in ['x']}
    grad_fn = _jax.value_and_grad(_loss, argnums=(0, 1))

    def one_microbatch(ex, loss_target):
        ex = dict(ex)
        diff = ex.pop(TWIN_DIFF_INPUT)
        return grad_fn(weights, diff, {**shared, **ex}, loss_target)

    if N_MICROBATCH == 1:
        loss, (grad_w, grad_x) = one_microbatch(per_example, given["loss_target"])
    else:
        def body(carry, xs):
            loss_sum, grad_sum = carry
            l_k, (gw_k, gx_k) = one_microbatch(xs[0], xs[1])
            with _jax.named_scope("update"):
                return (loss_sum + l_k, _jax.tree.map(_jnp.add, grad_sum, gw_k)), gx_k

        init = (_jnp.zeros((), _jnp.float32), _jax.tree.map(_jnp.zeros_like, weights))
        (loss, grad_w), grad_x = _jax.lax.scan(body, init, (per_example, given["loss_target"]))
    with _jax.named_scope("update"):
        delta_w, new_m, new_v = {}, {}, {}
        for n in TWIN_WEIGHTS:
            delta_w[n], new_m[n], new_v[n] = _adamw(weights[n], grad_w[n], given["m_" + n], given["v_" + n])
    return (loss, grad_x, *[grad_w[n] for n in TWIN_WEIGHTS], *[delta_w[n] for n in TWIN_WEIGHTS],
            *[new_m[n] for n in TWIN_WEIGHTS], *[new_v[n] for n in TWIN_WEIGHTS])
```

```python
import functools
import math

import jax
import jax.numpy as jnp
from jax import lax
from jax.experimental import pallas as pl
from jax.experimental.pallas import tpu as pltpu

F32 = jnp.float32
BF16 = jnp.bfloat16

D = 1024
DFF = 2816
N_META = 16
N_HEADS = 16
N_KV = 4
HD = 64
QG = 4
WIN = 128
BLK = 128
PAD = BLK - N_META
SW = 512
SGRP = 32
SCH = 16
SP = 64
NST = SGRP * SP
EPS = 1e-6
NEG = -1e30
SCALE = HD ** -0.5
NDEV = 8
DEPTH = 4
MESH_AXES = ("x", "y", "c")
MESH = pl.DeviceIdType.MESH

ADAM_LR = 0.001
ADAM_B1 = 0.9
ADAM_B2 = 0.999
ADAM_EPS = 1e-08
ADAM_WD = 0.01
ADAM_STEP = 10

VMEM_LIMIT = 56 * 1024 * 1024


def _params(*sem):
    return pltpu.CompilerParams(dimension_semantics=sem, vmem_limit_bytes=VMEM_LIMIT)


def _nn(a, b):
    return lax.dot_general(a, b, (((1,), (0,)), ((), ())), preferred_element_type=F32)


def _nt(a, b):
    return lax.dot_general(a, b, (((1,), (1,)), ((), ())), preferred_element_type=F32)


def _tn(a, b):
    return lax.dot_general(a, b, (((0,), (0,)), ((), ())), preferred_element_type=F32)


def _sig(x):
    return 1.0 / (1.0 + jnp.exp(-x))


def _rms_fwd(h, g):
    r = lax.rsqrt(jnp.mean(h * h, axis=-1, keepdims=True) + EPS)
    hh = h * r
    return hh, r, hh * g


def _rms_bwd(hh, r, g, dn):
    dhh = dn * g
    dx = r * (dhh - hh * jnp.mean(dhh * hh, axis=-1, keepdims=True))
    return dx, jnp.sum(dn * hh, axis=0, keepdims=True)


def _row_ok(i, tm):
    rows = i * tm + lax.broadcasted_iota(jnp.int32, (tm, 1), 0)
    return rows >= PAD


def _const_spec(shape):
    nd = len(shape)
    return pl.BlockSpec(shape, lambda *_: (0,) * nd)


def rowcall(name, body, rows, consts, outs, accs=(), *, tm):
    nrows = rows[0].shape[0]
    nt = nrows // tm
    assert nt * tm == nrows, (name, nrows, tm)
    nr, nc, no, na = len(rows), len(consts), len(outs), len(accs)
    in_specs = [pl.BlockSpec((tm, r.shape[1]), lambda i: (i, 0)) for r in rows]
    in_specs += [_const_spec(c.shape) for c in consts]
    out_shape = [jax.ShapeDtypeStruct((nrows, w), dt) for (w, dt) in outs]
    out_specs = [pl.BlockSpec((tm, w), lambda i: (i, 0)) for (w, dt) in outs]
    out_shape += [jax.ShapeDtypeStruct(s, F32) for s in accs]
    out_specs += [_const_spec(s) for s in accs]

    def kern(*refs):
        i = pl.program_id(0)
        row_vals = [r[...] for r in refs[:nr]]
        res = body(i, *row_vals, *refs[nr:nr + nc])
        out_refs = refs[nr + nc:nr + nc + no]
        acc_refs = refs[nr + nc + no:]
        for r, v in zip(out_refs, res[:no]):
            r[...] = v.astype(r.dtype)
        if na:
            @pl.when(i == 0)
            def _():
                for r in acc_refs:
                    r[...] = jnp.zeros_like(r)
            for r, v in zip(acc_refs, res[no:]):
                r[...] += v

    res = pl.pallas_call(
        kern, name=name, grid=(nt,), in_specs=in_specs, out_specs=out_specs, out_shape=out_shape,
        compiler_params=_params("arbitrary"),
    )(*rows, *consts)
    return res


def tn_matmul(name, lhs, rhs, scale=1.0):
    M, K = lhs.shape
    N = rhs.shape[1]
    tmw = M // 12
    assert tmw * 12 == M and tmw % 16 == 0
    tk = 1408 if (K % 1408 == 0) else K
    nk, nm = K // tk, M // tmw

    def kern(a_ref, b_ref, o_ref, acc):
        m = pl.program_id(1)

        @pl.when(m == 0)
        def _():
            acc[...] = jnp.zeros_like(acc)

        acc[...] += _tn(a_ref[...].astype(BF16), b_ref[...].astype(BF16))

        @pl.when(m == nm - 1)
        def _():
            o_ref[...] = (acc[...] * scale).astype(o_ref.dtype)

    return pl.pallas_call(
        kern, name=name, grid=(nk, nm),
        in_specs=[pl.BlockSpec((tmw, tk), lambda k, m: (m, k)), pl.BlockSpec((tmw, N), lambda k, m: (m, 0))],
        out_specs=pl.BlockSpec((tk, N), lambda k, m: (k, 0)),
        out_shape=jax.ShapeDtypeStruct((K, N), BF16),
        scratch_shapes=[pltpu.VMEM((tk, N), F32)],
        compiler_params=_params("arbitrary", "arbitrary"),
    )(lhs, rhs)


def _mesh_pos():
    x, y, c = lax.axis_index("x"), lax.axis_index("y"), lax.axis_index("c")
    return x, y, c


def all_gather_pieces(name, groups):
    ng = len(groups)
    packed = [g[0] for g in groups]
    pieces = [g[1] for g in groups]
    out_shape, out_map = [], []
    for gi, (p, pcs) in enumerate(groups):
        idx = []
        for (off, r) in pcs:
            idx.append(len(out_shape))
            out_shape.append(jax.ShapeDtypeStruct((NDEV * r, p.shape[1]), p.dtype))
        out_map.append(idx)
    nout = len(out_shape)

    def body(*refs):
        p_refs = refs[:ng]
        o_refs = refs[ng:ng + nout]
        send_sems, recv_sems, local_sems = refs[ng + nout:]
        x, y, c = _mesh_pos()
        me = (x, y, c)
        sibling = (x, y, 1 - c)
        chips = [(1 - x, y), (x, 1 - y), (1 - x, 1 - y)]

        def blk(px, py, pc):
            return 4 * px + 2 * py + pc

        def copies(gi, k, origin, to, from_out):
            cps = []
            for (off, r), oi in zip(pieces[gi], out_map[gi]):
                dst = o_refs[oi].at[pl.ds(origin * r, r), :]
                src = dst if from_out else p_refs[gi].at[pl.ds(off, r), :]
                cps.append(pltpu.make_async_remote_copy(
                    src_ref=src, dst_ref=dst, send_sem=send_sems.at[gi, k], recv_sem=recv_sems.at[gi, k],
                    device_id=to, device_id_type=MESH))
            return cps

        def whole(gi, k):
            return pltpu.make_async_remote_copy(
                src_ref=p_refs[gi], dst_ref=p_refs[gi], send_sem=send_sems.at[gi, k],
                recv_sem=recv_sems.at[gi, k], device_id=me, device_id_type=MESH)

        mine = []
        for gi in range(ng):
            for (off, r), oi in zip(pieces[gi], out_map[gi]):
                mine.append(pltpu.make_async_copy(
                    p_refs[gi].at[pl.ds(off, r), :], o_refs[oi].at[pl.ds(blk(*me) * r, r), :],
                    local_sems.at[gi]))
        for cp in mine:
            cp.start()
        for gi in range(ng):
            for cp in copies(gi, 0, blk(*me), sibling, False):
                cp.start()
            for j, chip in enumerate(chips):
                for cp in copies(gi, 1 + j, blk(*me), (*chip, c), False):
                    cp.start()
        for j, chip in enumerate(chips):
            for gi in range(ng):
                whole(gi, 1 + j).wait_recv()
                for cp in copies(gi, 4 + j, blk(*chip, c), sibling, True):
                    cp.start()
        for gi in range(ng):
            whole(gi, 0).wait_recv()
            for j in range(3):
                whole(gi, 4 + j).wait_recv()
        for gi in range(ng):
            for k in range(7):
                whole(gi, k).wait_send()
            pltpu.make_async_copy(p_refs[gi], p_refs[gi], local_sems.at[gi]).wait()

    any_spec = pl.BlockSpec(memory_space=pl.ANY)
    outs = pl.pallas_call(
        body, name=name, out_shape=out_shape,
        in_specs=[any_spec] * ng, out_specs=[any_spec] * nout,
        scratch_shapes=[pltpu.SemaphoreType.DMA((ng, 7)), pltpu.SemaphoreType.DMA((ng, 7)),
                        pltpu.SemaphoreType.DMA((ng,))],
    )(*packed)
    return [[outs[oi] for oi in idx] for idx in out_map]


def all_to_all_pieces(name, groups):
    ng = len(groups)
    flat, offs, land_shape = [], [], []
    for arrs in groups:
        o, off = [], 0
        for a in arrs:
            r = a.shape[0] // NDEV
            o.append((off, r))
            off += r
            flat.append(a)
        offs.append(o)
        land_shape.append(jax.ShapeDtypeStruct((NDEV, off, arrs[0].shape[1]), arrs[0].dtype))
    counts = [len(a) for a in groups]
    nin = len(flat)

    def body(*refs):
        in_refs = refs[:nin]
        land = refs[nin:nin + ng]
        send_sems, recv_sems, local_sems = refs[nin + ng:]
        x, y, c = _mesh_pos()
        me = (x, y, c)
        me_i = 4 * x + 2 * y + c
        peers = [(x, y, 1 - c), (1 - x, y, c), (x, 1 - y, c), (1 - x, 1 - y, c),
                 (1 - x, y, 1 - c), (x, 1 - y, 1 - c), (1 - x, 1 - y, 1 - c)]
        base = 0
        started = []
        for gi in range(ng):
            for ai, (off, r) in enumerate(offs[gi]):
                src_arr = in_refs[base + ai]
                loc = pltpu.make_async_copy(src_arr.at[pl.ds(me_i * r, r), :],
                                            land[gi].at[me_i, pl.ds(off, r), :], local_sems.at[gi])
                loc.start()
                for k, peer in enumerate(peers):
                    p_i = 4 * peer[0] + 2 * peer[1] + peer[2]
                    cp = pltpu.make_async_remote_copy(
                        src_ref=src_arr.at[pl.ds(p_i * r, r), :], dst_ref=land[gi].at[me_i, pl.ds(off, r), :],
                        send_sem=send_sems.at[gi, k], recv_sem=recv_sems.at[gi, k],
                        device_id=peer, device_id_type=MESH)
                    cp.start()
            base += counts[gi]
        for gi in range(ng):
            for k in range(7):
                w = pltpu.make_async_remote_copy(
                    src_ref=land[gi].at[0], dst_ref=land[gi].at[0], send_sem=send_sems.at[gi, k],
                    recv_sem=recv_sems.at[gi, k], device_id=me, device_id_type=MESH)
                w.wait_send()
                w.wait_recv()
            pltpu.make_async_copy(land[gi].at[0], land[gi].at[0], local_sems.at[gi]).wait()

    any_spec = pl.BlockSpec(memory_space=pl.ANY)
    return pl.pallas_call(
        body, name=name, out_shape=land_shape,
        in_specs=[any_spec] * nin, out_specs=[any_spec] * ng,
        scratch_shapes=[pltpu.SemaphoreType.DMA((ng, 7)), pltpu.SemaphoreType.DMA((ng, 7)),
                        pltpu.SemaphoreType.DMA((ng,))],
    )(*flat)


def _pick_tile(n, cap):
    best = None
    for t in range(8, min(n, cap) + 1, 8):
        if n % t == 0:
            best = t
    return best if best is not None else n


def sum_slots(name, land):
    _, R, W = land.shape
    tr = _pick_tile(R, 512)

    def kern(l_ref, o_ref):
        acc = l_ref[0].astype(F32)
        for s in range(1, NDEV):
            acc = acc + l_ref[s].astype(F32)
        o_ref[...] = acc

    return pl.pallas_call(
        kern, name=name, grid=(R // tr,),
        in_specs=[pl.BlockSpec((NDEV, tr, W), lambda i: (0, i, 0))],
        out_specs=pl.BlockSpec((tr, W), lambda i: (i, 0)),
        out_shape=jax.ShapeDtypeStruct((R, W), F32),
        compiler_params=_params("arbitrary"),
    )(land)


def adamw(name, w, g, m, v):
    shp = w.shape
    C = shp[-1]
    R = max(1, math.prod(shp[:-1]))
    tr = _pick_tile(R, 1024)
    w2, g2, m2, v2 = (a.reshape(R, C) for a in (w, g, m, v))

    def kern(w_ref, g_ref, m_ref, v_ref, d_ref, mo_ref, vo_ref):
        gg = g_ref[...]
        mn = ADAM_B1 * m_ref[...] + (1.0 - ADAM_B1) * gg
        vn = ADAM_B2 * v_ref[...] + (1.0 - ADAM_B2) * jnp.square(gg)
        m_hat = mn / (1.0 - ADAM_B1 ** ADAM_STEP)
        v_hat = vn / (1.0 - ADAM_B2 ** ADAM_STEP)
        d_ref[...] = -ADAM_LR * (m_hat / (jnp.sqrt(v_hat) + ADAM_EPS) + ADAM_WD * w_ref[...])
        mo_ref[...] = mn
        vo_ref[...] = vn

    spec = pl.BlockSpec((tr, C), lambda i: (i, 0))
    d, mo, vo = pl.pallas_call(
        kern, name=name, grid=(R // tr,), in_specs=[spec] * 4, out_specs=[spec] * 3,
        out_shape=[jax.ShapeDtypeStruct((R, C), F32)] * 3, compiler_params=_params("arbitrary"),
    )(w2, g2, m2, v2)
    return d.reshape(shp), mo.reshape(shp), vo.reshape(shp)


def build_h0(x2, blk0):
    L0 = x2.shape[0]
    nb = L0 // BLK + 1

    def kern(x_ref, b_ref, o_ref):
        i = pl.program_id(0)

        @pl.when(i == 0)
        def _():
            o_ref[...] = b_ref[...]

        @pl.when(i > 0)
        def _():
            o_ref[...] = x_ref[...]

    return pl.pallas_call(
        kern, name="build_h0", grid=(nb,),
        in_specs=[pl.BlockSpec((BLK, D), lambda i: (jnp.maximum(i - 1, 0), 0)), _const_spec((BLK, D))],
        out_specs=pl.BlockSpec((BLK, D), lambda i: (i, 0)),
        out_shape=jax.ShapeDtypeStruct((L0 + BLK, D), F32), compiler_params=_params("arbitrary"),
    )(x2, blk0)


def final_loss(h, tgt, gf):
    LP = h.shape[0]
    nb = LP // BLK

    def kern(h_ref, t_ref, g_ref, dh_ref, loss_ref, dg_ref):
        i = pl.program_id(0)

        @pl.when(i == 0)
        def _():
            loss_ref[...] = jnp.zeros_like(loss_ref)
            dg_ref[...] = jnp.zeros_like(dg_ref)

        g = g_ref[...]
        hh, r, yv = _rms_fwd(h_ref[...], g)
        valid = (i > 0).astype(F32)
        err = (yv - t_ref[...]) * valid
        loss_ref[...] += 0.5 * jnp.sum(jnp.sum(err * err, axis=1, keepdims=True), axis=0, keepdims=True) / D
        dy = err / D
        dx, dg = _rms_bwd(hh, r, g, dy)
        dh_ref[...] = dx
        dg_ref[...] += dg

    return pl.pallas_call(
        kern, name="final_loss", grid=(nb,),
        in_specs=[pl.BlockSpec((BLK, D), lambda i: (i, 0)),
                  pl.BlockSpec((BLK, D), lambda i: (jnp.maximum(i - 1, 0), 0)), _const_spec((1, D))],
        out_specs=[pl.BlockSpec((BLK, D), lambda i: (i, 0)), _const_spec((8, 128)), _const_spec((1, D))],
        out_shape=[jax.ShapeDtypeStruct((LP, D), F32), jax.ShapeDtypeStruct((8, 128), F32),
                   jax.ShapeDtypeStruct((1, D), F32)],
        compiler_params=_params("arbitrary"),
    )(h, tgt, gf)


def ffn_forward(tag, h, g, wgT, wuT, wd, tm):
    def f1(i, hv, g_ref, wg_ref, wu_ref):
        _, _, n = _rms_fwd(hv, g_ref[...])
        nb = n.astype(BF16)
        G = _nt(nb, wg_ref[...])
        U = _nt(nb, wu_ref[...])
        A = G * _sig(G) * U
        return nb, G, U, A

    n, G, U, A = rowcall(tag + "_up", f1, [h], [g, wgT, wuT],
                         [(D, BF16), (DFF, BF16), (DFF, BF16), (DFF, BF16)], tm=tm)

    def f2(i, av, hv, wd_ref):
        return (hv + 0.5 * _nn(av, wd_ref[...]),)

    (h2,) = rowcall(tag + "_down", f2, [A, h], [wd], [(D, F32)], tm=tm)
    return h2, (h, n, G, U, A)


def ffn_backward(tag, dh, saved, g, wgT, wuT, wd, tm):
    h, n, G, U, A = saved

    def b1(i, dhv, Gv, Uv, wd_ref):
        dA = 0.5 * _nt(dhv.astype(BF16), wd_ref[...])
        Gf = Gv.astype(F32)
        Uf = Uv.astype(F32)
        sg = _sig(Gf)
        dG = dA * Uf * sg * (1.0 + Gf * (1.0 - sg))
        dU = dA * Gf * sg
        return dG, dU

    dG, dU = rowcall(tag + "_bwd_act", b1, [dh, G, U], [wd], [(DFF, BF16), (DFF, BF16)], tm=tm)

    def b2(i, dGv, dUv, hv, dhv, g_ref, wg_ref, wu_ref):
        dn = _nn(dGv, wg_ref[...]) + _nn(dUv, wu_ref[...])
        gv = g_ref[...]
        hh, r, _ = _rms_fwd(hv, gv)
        dx, dg = _rms_bwd(hh, r, gv, dn)
        dx = jnp.where(_row_ok(i, tm), dx, 0.0)
        return dhv + dx, dg

    dh2, dg = rowcall(tag + "_bwd_in", b2, [dG, dU, h, dh], [g, wgT, wuT], [(D, F32)], [(1, D)], tm=tm)
    dwd = tn_matmul(tag + "_dwd", A, dh, scale=0.5)
    dwgT = tn_matmul(tag + "_dwg", dG, n)
    dwuT = tn_matmul(tag + "_dwu", dU, n)
    return dh2, dg, dwgT, dwuT, dwd


def _alibi_slope(head):
    return float(2.0 ** (-8.0 * (head + 1) / N_HEADS))


def _att_bias(n, nb):
    qi = lax.broadcasted_iota(jnp.int32, (BLK, 4 * BLK), 0)
    cj = lax.broadcasted_iota(jnp.int32, (BLK, 4 * BLK), 1)
    jb = cj - BLK
    dist = jnp.abs(qi + BLK - jb)
    kpos = (n - 1) * BLK + jb
    band_ok = (dist <= WIN) & (kpos >= BLK) & (kpos < nb * BLK)
    is_meta = cj < BLK
    ok = (is_meta & (cj >= PAD)) | (jnp.logical_not(is_meta) & band_ok)
    distf = jnp.where(is_meta, 0, dist).astype(F32)
    maskadd = jnp.where(ok, 0.0, NEG).astype(F32)
    distf4 = jnp.concatenate([distf] * QG, axis=0)
    mask4 = jnp.concatenate([maskadd] * QG, axis=0)
    return distf4, mask4


def _group_col(vals):
    rg = lax.broadcasted_iota(jnp.int32, (QG * BLK, 1), 0) // BLK
    col = jnp.full((QG * BLK, 1), vals[QG - 1], F32)
    for gq in range(QG - 2, -1, -1):
        col = jnp.where(rg == gq, vals[gq], col)
    return col


def _stack_heads(ref_or_val, kh):
    return jnp.concatenate(
        [ref_or_val[:, (kh * QG + gq) * HD:(kh * QG + gq + 1) * HD] for gq in range(QG)], axis=0)


def _stack_keys(km, kp, kc, kn, kh):
    sl = slice(kh * HD, (kh + 1) * HD)
    return jnp.concatenate([km[:, sl], kp[:, sl], kc[:, sl], kn[:, sl]], axis=0)


def _att_probs(qs, kb, distf4, mask4, kh, sink_ref):
    slope_col = _group_col([_alibi_slope(kh * QG + gq) for gq in range(QG)])
    sink_col = _group_col([sink_ref[kh * QG + gq] for gq in range(QG)])
    s = _nt(qs, kb) * SCALE + (mask4 - slope_col * distf4)
    m = jnp.maximum(jnp.max(s, axis=1, keepdims=True), sink_col)
    e = jnp.exp(s - m)
    es = jnp.exp(sink_col - m)
    inv = 1.0 / (jnp.sum(e, axis=1, keepdims=True) + es)
    return e * inv, es * inv


def attention_forward(tag, q, k, v, sink):
    LP = q.shape[0]
    nb = LP // BLK

    def kern(sink_ref, q_ref, km_ref, kp_ref, kc_ref, kn_ref, vm_ref, vp_ref, vc_ref, vn_ref, o_ref):
        n = pl.program_id(0)
        distf4, mask4 = _att_bias(n, nb)
        qv = q_ref[...]
        km, kp, kc, kn = km_ref[...], kp_ref[...], kc_ref[...], kn_ref[...]
        vm, vp, vc, vn = vm_ref[...], vp_ref[...], vc_ref[...], vn_ref[...]
        for kh in range(N_KV):
            qs = _stack_heads(qv, kh)
            kb = _stack_keys(km, kp, kc, kn, kh)
            vb = _stack_keys(vm, vp, vc, vn, kh)
            p, _ = _att_probs(qs, kb, distf4, mask4, kh, sink_ref)
            o = _nn(p.astype(BF16), vb)
            for gq in range(QG):
                hcol = (kh * QG + gq) * HD
                o_ref[:, hcol:hcol + HD] = o[gq * BLK:(gq + 1) * BLK].astype(o_ref.dtype)

    def kvspec(dn):
        return pl.BlockSpec((BLK, N_KV * HD), lambda n: (jnp.clip(n + dn, 0, nb - 1), 0))

    meta_spec = pl.BlockSpec((BLK, N_KV * HD), lambda n: (0, 0))
    return pl.pallas_call(
        kern, name=tag + "_att_fwd", grid=(nb,),
        in_specs=[pl.BlockSpec(memory_space=pltpu.SMEM), pl.BlockSpec((BLK, D), lambda n: (n, 0)),
                  meta_spec, kvspec(-1), kvspec(0), kvspec(1), meta_spec, kvspec(-1), kvspec(0), kvspec(1)],
        out_specs=pl.BlockSpec((BLK, D), lambda n: (n, 0)),
        out_shape=jax.ShapeDtypeStruct((LP, D), BF16), compiler_params=_params("arbitrary"),
    )(sink, q, k, k, k, k, v, v, v, v)


def attention_backward(tag, q, k, v, do, sink):
    LP = q.shape[0]
    nb = LP // BLK
    KW = N_KV * HD

    def kern(sink_ref, q_ref, do_ref, km_ref, kp_ref, kc_ref, kn_ref, vm_ref, vp_ref, vc_ref, vn_ref,
             dq_ref, dkp_ref, dvp_ref, dkm_ref, dvm_ref, dsink_ref):
        n = pl.program_id(0)

        @pl.when(n == 0)
        def _():
            dkm_ref[...] = jnp.zeros_like(dkm_ref)
            dvm_ref[...] = jnp.zeros_like(dvm_ref)
            dsink_ref[...] = jnp.zeros_like(dsink_ref)

        distf4, mask4 = _att_bias(n, nb)
        qv, dov = q_ref[...], do_ref[...]
        km, kp, kc, kn = km_ref[...], kp_ref[...], kc_ref[...], kn_ref[...]
        vm, vp, vc, vn = vm_ref[...], vp_ref[...], vc_ref[...], vn_ref[...]
        lane = lax.broadcasted_iota(jnp.int32, (8, 128), 1)
        dsink = jnp.zeros((8, 128), F32)
        for kh in range(N_KV):
            qs = _stack_heads(qv, kh)
            dos = _stack_heads(dov, kh)
            kb = _stack_keys(km, kp, kc, kn, kh)
            vb = _stack_keys(vm, vp, vc, vn, kh)
            p, ps = _att_probs(qs, kb, distf4, mask4, kh, sink_ref)
            dp = _nt(dos, vb)
            delta = jnp.sum(p * dp, axis=1, keepdims=True)
            ds = (p * (dp - delta)).astype(BF16)
            dqs = _nn(ds, kb) * SCALE
            dkb = _tn(ds, qs) * SCALE
            dvb = _tn(p.astype(BF16), dos)
            dsk = -(ps * delta)
            sl = slice(kh * HD, (kh + 1) * HD)
            for gq in range(QG):
                hcol = (kh * QG + gq) * HD
                dq_ref[:, hcol:hcol + HD] = dqs[gq * BLK:(gq + 1) * BLK].astype(dq_ref.dtype)
                tot = jnp.sum(dsk[gq * BLK:(gq + 1) * BLK], axis=0, keepdims=True)
                dsink = dsink + jnp.where(lane == kh * QG + gq, tot, 0.0)
            dkm_ref[:, sl] += dkb[0:BLK]
            dvm_ref[:, sl] += dvb[0:BLK]
            for slot in range(3):
                dkp_ref[0, slot, :, sl] = dkb[(slot + 1) * BLK:(slot + 2) * BLK]
                dvp_ref[0, slot, :, sl] = dvb[(slot + 1) * BLK:(slot + 2) * BLK]
        dsink_ref[...] += dsink

    def kvspec(dn):
        return pl.BlockSpec((BLK, KW), lambda n: (jnp.clip(n + dn, 0, nb - 1), 0))

    meta_spec = pl.BlockSpec((BLK, KW), lambda n: (0, 0))
    rowspec = pl.BlockSpec((BLK, D), lambda n: (n, 0))
    part_spec = pl.BlockSpec((1, 3, BLK, KW), lambda n: (n, 0, 0, 0))
    dq, dkp, dvp, dkm, dvm, dsink = pl.pallas_call(
        kern, name=tag + "_att_bwd", grid=(nb,),
        in_specs=[pl.BlockSpec(memory_space=pltpu.SMEM), rowspec, rowspec,
                  meta_spec, kvspec(-1), kvspec(0), kvspec(1), meta_spec, kvspec(-1), kvspec(0), kvspec(1)],
        out_specs=[rowspec, part_spec, part_spec, _const_spec((BLK, KW)), _const_spec((BLK, KW)),
                   _const_spec((8, 128))],
        out_shape=[jax.ShapeDtypeStruct((LP, D), BF16), jax.ShapeDtypeStruct((nb, 3, BLK, KW), F32),
                   jax.ShapeDtypeStruct((nb, 3, BLK, KW), F32), jax.ShapeDtypeStruct((BLK, KW), F32),
                   jax.ShapeDtypeStruct((BLK, KW), F32), jax.ShapeDtypeStruct((8, 128), F32)],
        compiler_params=_params("arbitrary"),
    )(sink, q, do, k, k, k, k, v, v, v, v)

    def comb(a_ref, b_ref, c_ref, m_ref, a2_ref, b2_ref, c2_ref, m2_ref, dk_ref, dv_ref):
        mblk = pl.program_id(0)
        has_prev = (mblk > 0).astype(F32)
        has_next = (mblk < nb - 1).astype(F32)
        is0 = (mblk == 0).astype(F32)
        dk_ref[...] = (a_ref[0, 0] * has_prev + b_ref[0, 0] + c_ref[0, 0] * has_next
                       + m_ref[...] * is0).astype(dk_ref.dtype)
        dv_ref[...] = (a2_ref[0, 0] * has_prev + b2_ref[0, 0] + c2_ref[0, 0] * has_next
                       + m2_ref[...] * is0).astype(dv_ref.dtype)

    def pspec(dn, slot):
        return pl.BlockSpec((1, 1, BLK, KW), lambda m: (jnp.clip(m + dn, 0, nb - 1), slot, 0, 0))

    kvout = pl.BlockSpec((BLK, KW), lambda m: (m, 0))
    dk, dv = pl.pallas_call(
        comb, name=tag + "_att_dkv", grid=(nb,),
        in_specs=[pspec(-1, 2), pspec(0, 1), pspec(1, 0), _const_spec((BLK, KW)),
                  pspec(-1, 2), pspec(0, 1), pspec(1, 0), _const_spec((BLK, KW))],
        out_specs=[kvout, kvout],
        out_shape=[jax.ShapeDtypeStruct((LP, KW), BF16)] * 2, compiler_params=_params("arbitrary"),
    )(dkp, dkp, dkp, dkm, dvp, dvp, dvp, dvm)
    return dq, dk, dv, dsink


def _scan_tile(xr, xi, cr, ci, tab, nblk, reverse):
    def blk(j, carry):
        b = (nblk - 1 - j) if reverse else j
        r0 = pl.multiple_of(b * 8, 8)
        vr = xr[pl.ds(r0, 8), :]
        vi = xi[pl.ds(r0, 8), :]
        for t, s in enumerate((1, 2, 4)):
            sh = (8 - s) if reverse else s
            sr = pltpu.roll(vr, sh, 0)
            si = pltpu.roll(vi, sh, 0)
            tr = tab[2 * t]
            ti = tab[2 * t + 1]
            vr, vi = vr + tr * sr - ti * si, vi + tr * si + ti * sr
        pr = tab[6]
        pi = tab[7]
        c_r = cr[...]
        c_i = ci[...]
        vr, vi = vr + pr * c_r - pi * c_i, vi + pr * c_i + pi * c_r
        xr[pl.ds(r0, 8), :] = vr
        xi[pl.ds(r0, 8), :] = vi
        row = 0 if reverse else 7
        cr[...] = jnp.broadcast_to(vr[row:row + 1, :], (8, NST))
        ci[...] = jnp.broadcast_to(vi[row:row + 1, :], (8, NST))
        return carry

    lax.fori_loop(0, nblk, blk, 0)


ST_T = 4 * SP * 2
CH_T = 128


def ssm_dir_forward(tag, u, bpr, bpi, cpr, cpi, tab, reverse, tm):
    LP = u.shape[0]
    nt = LP // tm

    def rix(i):
        return (nt - 1 - i) if reverse else i

    def kern(u_ref, bpr_ref, bpi_ref, cpr_ref, cpi_ref, tab_ref, xre_ref, xim_ref, y_ref, xr, xi, cr, ci):
        i = pl.program_id(0)

        @pl.when(i == 0)
        def _():
            cr[...] = jnp.zeros_like(cr)
            ci[...] = jnp.zeros_like(ci)

        ub = u_ref[...].astype(BF16)
        for ct in range(4):
            uc = ub[:, ct * CH_T:(ct + 1) * CH_T]
            xr[:, ct * ST_T:(ct + 1) * ST_T] = _nn(uc, bpr_ref[ct * CH_T:(ct + 1) * CH_T, :])
            xi[:, ct * ST_T:(ct + 1) * ST_T] = _nn(uc, bpi_ref[ct * CH_T:(ct + 1) * CH_T, :])
        _scan_tile(xr, xi, cr, ci, tab_ref, tm // 8, reverse)
        xrb = xr[...].astype(BF16)
        xib = xi[...].astype(BF16)
        xre_ref[...] = xrb
        xim_ref[...] = xib
        for ct in range(4):
            ss = slice(ct * ST_T, (ct + 1) * ST_T)
            y_ref[:, ct * CH_T:(ct + 1) * CH_T] = (_nn(xrb[:, ss], cpr_ref[ss, :]) - _nn(xib[:, ss], cpi_ref[ss, :]))

    row = lambda w: pl.BlockSpec((tm, w), lambda i: (rix(i), 0))
    return pl.pallas_call(
        kern, name=tag, grid=(nt,),
        in_specs=[row(SW), _const_spec(bpr.shape), _const_spec(bpi.shape), _const_spec(cpr.shape),
                  _const_spec(cpi.shape), _const_spec(tab.shape)],
        out_specs=[row(NST), row(NST), row(SW)],
        out_shape=[jax.ShapeDtypeStruct((LP, NST), BF16), jax.ShapeDtypeStruct((LP, NST), BF16),
                   jax.ShapeDtypeStruct((LP, SW), F32)],
        scratch_shapes=[pltpu.VMEM((tm, NST), F32), pltpu.VMEM((tm, NST), F32),
                        pltpu.VMEM((8, NST), F32), pltpu.VMEM((8, NST), F32)],
        compiler_params=_params("arbitrary"),
    )(u, bpr, bpi, cpr, cpi, tab)


def ssm_dir_backward(tag, dy, xre, xim, u, bpr, bpi, cpr, cpi, tab_adj, reverse, tm):
    LP = u.shape[0]
    nt = LP // tm

    def rix(i):
        return (nt - 1 - i) if reverse else i

    def kern(dy_ref, xre_ref, xim_ref, u_ref, bpr_ref, bpi_ref, cpr_ref, cpi_ref, tab_ref,
             du_ref, gbr_ref, gbi_ref, gcr_ref, gci_ref, sr_ref, si_ref, lr, li, gr, gi, cr, ci):
        i = pl.program_id(0)

        @pl.when(i == 0)
        def _():
            cr[...] = jnp.zeros_like(cr)
            ci[...] = jnp.zeros_like(ci)
            for r in (gbr_ref, gbi_ref, gcr_ref, gci_ref, sr_ref, si_ref):
                r[...] = jnp.zeros_like(r)

        dyb = dy_ref[...]
        ub = u_ref[...].astype(BF16)
        for ct in range(4):
            ss = slice(ct * ST_T, (ct + 1) * ST_T)
            dc = dyb[:, ct * CH_T:(ct + 1) * CH_T]
            g_re = _nt(dc, cpr_ref[ss, :])
            g_im = -_nt(dc, cpi_ref[ss, :])
            lr[:, ss] = g_re
            li[:, ss] = g_im
            gr[:, ss] = g_re
            gi[:, ss] = g_im
        _scan_tile(lr, li, cr, ci, tab_ref, tm // 8, reverse)
        lam_r = lr[...]
        lam_i = li[...]
        wr = lam_r - gr[...]
        wi = lam_i - gi[...]
        xr = xre_ref[...].astype(F32)
        xi = xim_ref[...].astype(F32)
        sr_ref[...] += jnp.sum(wr * xr + wi * xi, axis=0, keepdims=True)
        si_ref[...] += jnp.sum(wi * xr - wr * xi, axis=0, keepdims=True)
        lrb = lam_r.astype(BF16)
        lib = lam_i.astype(BF16)
        xrb = xre_ref[...]
        xib = xim_ref[...]
        for ct in range(4):
            ss = slice(ct * ST_T, (ct + 1) * ST_T)
            cs = slice(ct * CH_T, (ct + 1) * CH_T)
            du_ref[:, cs] = _nt(lrb[:, ss], bpr_ref[cs, :]) + _nt(lib[:, ss], bpi_ref[cs, :])
            gbr_ref[ss, :] += _tn(lrb[:, ss], ub[:, cs])
            gbi_ref[ss, :] += _tn(lib[:, ss], ub[:, cs])
            gcr_ref[ss, :] += _tn(xrb[:, ss], dyb[:, cs])
            gci_ref[ss, :] -= _tn(xib[:, ss], dyb[:, cs])

    row = lambda w: pl.BlockSpec((tm, w), lambda i: (rix(i), 0))
    acc = _const_spec((NST, CH_T))
    vec = _const_spec((1, NST))
    return pl.pallas_call(
        kern, name=tag, grid=(nt,),
        in_specs=[row(SW), row(NST), row(NST), row(SW), _const_spec(bpr.shape), _const_spec(bpi.shape),
                  _const_spec(cpr.shape), _const_spec(cpi.shape), _const_spec(tab_adj.shape)],
        out_specs=[row(SW), acc, acc, acc, acc, vec, vec],
        out_shape=[jax.ShapeDtypeStruct((LP, SW), F32)] + [jax.ShapeDtypeStruct((NST, CH_T), F32)] * 4
        + [jax.ShapeDtypeStruct((1, NST), F32)] * 2,
        scratch_shapes=[pltpu.VMEM((tm, NST), F32)] * 4 + [pltpu.VMEM((8, NST), F32)] * 2,
        compiler_params=_params("arbitrary"),
    )(dy, xre, xim, u, bpr, bpi, cpr, cpi, tab_adj)


def _ssm_disc(lam_re, lam_im, log_dt, b_re, b_im):
    dt = jnp.exp(log_dt)[:, None]
    mag = jnp.exp(lam_re * dt)
    a_re = mag * jnp.cos(lam_im * dt)
    a_im = mag * jnp.sin(lam_im * dt)
    den = lam_re * lam_re + lam_im * lam_im
    f_re = ((a_re - 1.0) * lam_re + a_im * lam_im) / den
    f_im = (a_im * lam_re - (a_re - 1.0) * lam_im) / den
    bb_re = f_re[:, :, None] * b_re - f_im[:, :, None] * b_im
    bb_im = f_re[:, :, None] * b_im + f_im[:, :, None] * b_re
    return a_re, a_im, bb_re, bb_im


def _scan_tables(lam_re, lam_im, log_dt, conj, reverse):
    dt = jnp.exp(log_dt)[:, None]
    lr = (lam_re * dt).reshape(1, NST)
    li = (lam_im * dt).reshape(1, NST) * (-1.0 if conj else 1.0)
    t = jnp.arange(8, dtype=F32)[:, None]

    def power(kk):
        mag = jnp.exp(kk * lr)
        return mag * jnp.cos(kk * li), mag * jnp.sin(kk * li)

    tabs = []
    for s in (1, 2, 4):
        mask = (t <= 7 - s) if reverse else (t >= s)
        pr, pi = power(jnp.full((8, 1), float(s), F32))
        tabs += [jnp.where(mask, pr, 0.0), jnp.where(mask, pi, 0.0)]
    kk = (8.0 - t) if reverse else (t + 1.0)
    pr, pi = power(kk)
    tabs += [pr, pi]
    return jnp.stack(tabs).astype(F32)


def _pack_b(bb):
    t = bb.transpose(0, 2, 1).reshape(4, 8, SCH, SP)
    eye = jnp.eye(8, dtype=bb.dtype)
    return jnp.einsum('tgcp,gh->tgchp', t, eye).reshape(SW, ST_T)


def _pack_c(cc):
    t = cc.transpose(0, 2, 1).reshape(4, 8, SP, SCH)
    eye = jnp.eye(8, dtype=cc.dtype)
    return jnp.einsum('tgpc,gh->tgphc', t, eye).reshape(NST, CH_T)


def _unpack_diag(acc):
    t = acc.reshape(4, 8, SP, 8, SCH)
    eye = jnp.eye(8, dtype=acc.dtype)
    return jnp.einsum('tgphc,gh->tgpc', t, eye).reshape(SGRP, SP, SCH)


def _gelu(y):
    k0 = math.sqrt(2.0 / math.pi)
    inner = k0 * (y + 0.044715 * y * y * y)
    th = jnp.tanh(inner)
    z = 0.5 * y * (1.0 + th)
    dz = 0.5 * (1.0 + th) + 0.5 * y * (1.0 - th * th) * k0 * (1.0 + 3.0 * 0.044715 * y * y)
    return z, dz


Q0, K0, V0, U0, GS0, GA0, IN_COLS = 0, 1024, 1280, 1536, 2048, 3072, 4096


def mixer_forward(tag, h, p, tm):
    g, winT, wglu, wbsT, wba, wout = p["g"], p["winT"], p["wglu"], p["wbsT"], p["wba"], p["wout"]

    def proj(i, hv, g_ref, w_ref):
        _, _, n = _rms_fwd(hv, g_ref[...])
        nb = n.astype(BF16)
        return (nb, _nt(nb, w_ref[Q0:K0, :]), _nt(nb, w_ref[K0:V0, :]), _nt(nb, w_ref[V0:U0, :]),
                _nt(nb, w_ref[U0:GS0, :]), _nt(nb, w_ref[GS0:GA0, :]), _nt(nb, w_ref[GA0:IN_COLS, :]))

    n, q, k, v, u, gs, ga = rowcall(
        tag + "_proj", proj, [h], [g, winT],
        [(D, BF16), (D, BF16), (N_KV * HD, BF16), (N_KV * HD, BF16), (SW, F32), (D, F32), (D, F32)], tm=tm)

    ya = attention_forward(tag, q, k, v, p["sink"])

    states, ydir = [], []
    for dr in range(2):
        s = p["ssm"][dr]
        xre, xim, yd = ssm_dir_forward(f"{tag}_ssm_fwd{dr}", u, s["bpr"], s["bpi"], s["cpr"], s["cpi"],
                                       s["tab"], dr == 1, tm)
        states.append((xre, xim))
        ydir.append(yd)

    def glu(i, y0, y1, uv, d_ref, w_ref):
        ypre = y0 + y1 + d_ref[...] * uv
        z, _ = _gelu(ypre)
        zb = z.astype(BF16)
        t = _nn(zb, w_ref[...])
        return ypre, zb, t, z * _sig(t)

    ypre, zb, t, ys = rowcall(tag + "_glu", glu, [ydir[0], ydir[1], u], [p["d"], wglu],
                              [(SW, F32), (SW, BF16), (SW, F32), (SW, BF16)], tm=tm)

    def merge(i, ysv, yav, gsv, gav, wbs_ref, wba_ref):
        bs = _nt(ysv, wbs_ref[...])
        ba = _nn(yav, wba_ref[...])
        mg = _sig(gsv) * bs + _sig(gav) * ba
        mg = jnp.where(_row_ok(i, tm), mg, 0.0)
        return bs, ba, mg

    bs, ba, mg = rowcall(tag + "_merge", merge, [ys, ya, gs, ga], [wbsT, wba],
                         [(D, BF16), (D, BF16), (D, BF16)], tm=tm)

    def outp(i, mv, hv, w_ref):
        return (hv + _nn(mv, w_ref[...]),)

    (h2,) = rowcall(tag + "_out", outp, [mg, h], [wout], [(D, F32)], tm=tm)
    saved = dict(h=h, n=n, q=q, k=k, v=v, u=u, gs=gs, ga=ga, ya=ya, states=states, ypre=ypre, zb=zb, t=t,
                 ys=ys, bs=bs, ba=ba, mg=mg)
    return h2, saved


def mixer_backward(tag, dh, sv, p, tm):
    g, winT, wglu, wbsT, wba, wout = p["g"], p["winT"], p["wglu"], p["wbsT"], p["wba"], p["wout"]

    def y1(i, dhv, bsv, bav, gsv, gav, w_ref):
        dmg = _nt(dhv.astype(BF16), w_ref[...])
        dmg = jnp.where(_row_ok(i, tm), dmg, 0.0)
        sgs = _sig(gsv)
        sga = _sig(gav)
        return (dmg * sgs, dmg * sga, dmg * bsv.astype(F32) * sgs * (1.0 - sgs),
                dmg * bav.astype(F32) * sga * (1.0 - sga))

    dbs, dba, dgs, dga = rowcall(tag + "_bwd_merge", y1, [dh, sv["bs"], sv["ba"], sv["gs"], sv["ga"]], [wout],
                                 [(D, BF16)] * 4, tm=tm)
    dwout = tn_matmul(tag + "_dwout", sv["mg"], dh)

    def y2(i, dbsv, dbav, wbs_ref, wba_ref):
        return _nn(dbsv, wbs_ref[...]), _nt(dbav, wba_ref[...])

    dys, dya = rowcall(tag + "_bwd_branch", y2, [dbs, dba], [wbsT, wba], [(SW, F32), (D, BF16)], tm=tm)
    dwbsT = tn_matmul(tag + "_dwbs", dbs, sv["ys"])
    dwba = tn_matmul(tag + "_dwba", sv["ya"], dba)

    def s2b(i, dysv, ypv, tv, uv, d_ref, w_ref):
        z, dz_dy = _gelu(ypv)
        st = _sig(tv)
        dt_ = dysv * z * st * (1.0 - st)
        dz = dysv * st + _nt(dt_.astype(BF16), w_ref[...])
        dyp = dz * dz_dy
        return dyp, dyp * d_ref[...], dt_, jnp.sum(dyp * uv, axis=0, keepdims=True)

    dypb, du0, dtb, dd = rowcall(tag + "_bwd_glu", s2b, [dys, sv["ypre"], sv["t"], sv["u"]], [p["d"], wglu],
                                 [(SW, BF16), (SW, F32), (SW, BF16)], [(1, SW)], tm=tm)
    dwglu = tn_matmul(tag + "_dwglu", sv["zb"], dtb)

    du_dirs, ssm_sums = [], []
    for dr in range(2):
        s = p["ssm"][dr]
        xre, xim = sv["states"][dr]
        res = ssm_dir_backward(f"{tag}_ssm_bwd{dr}", dypb, xre, xim, sv["u"], s["bpr"], s["bpi"], s["cpr"],
                               s["cpi"], s["tab_adj"], dr == 0, tm)
        du_dirs.append(res[0])
        ssm_sums.append(res[1:])

    dq, dk, dv, dsink = attention_backward(tag, sv["q"], sv["k"], sv["v"], dya, p["sink"])

    def x1b(i, dqv, dkv, dvv, du0v, du1v, du2v, dgsv, dgav, hv, dhv, g_ref, w_ref):
        dub = (du0v + du1v + du2v).astype(BF16)
        dn = (_nn(dqv, w_ref[Q0:K0, :]) + _nn(dkv, w_ref[K0:V0, :]) + _nn(dvv, w_ref[V0:U0, :])
              + _nn(dub, w_ref[U0:GS0, :]) + _nn(dgsv, w_ref[GS0:GA0, :]) + _nn(dgav, w_ref[GA0:IN_COLS, :]))
        gv = g_ref[...]
        hh, r, _ = _rms_fwd(hv, gv)
        dx, dg = _rms_bwd(hh, r, gv, dn)
        dx = jnp.where(_row_ok(i, tm), dx, 0.0)
        return dhv + dx, dub, dg

    dh2, dub, dg = rowcall(tag + "_bwd_in", x1b,
                           [dq, dk, dv, du0, du_dirs[0], du_dirs[1], dgs, dga, sv["h"], dh], [g, winT],
                           [(D, F32), (SW, BF16)], [(1, D)], tm=tm)
    n = sv["n"]
    dwinT = jnp.concatenate([tn_matmul(f"{tag}_dwin{j}", piece, n)
                             for j, piece in enumerate((dq, dk, dv, dub, dgs, dga))], axis=0)
    grads = dict(g=dg, d=dd, sink=dsink, ssm=ssm_sums, winT=dwinT, wglu=dwglu, wbsT=dwbsT, wba=dwba, wout=dwout)
    return dh2, grads


W1024 = ("f1_wgT", "f1_wuT", "f1_wd", "winT", "wba", "wout", "f2_wgT", "f2_wuT", "f2_wd")
W512 = ("wglu", "wbsT")
SMALL = ("ffn1_norm", "mix_norm", "ffn2_norm", "final_norm", "ssm_lam_re", "ssm_lam_im", "ssm_log_dt",
         "ssm_b_re", "ssm_b_im", "ssm_c_re", "ssm_c_im", "ssm_d", "attn_sink")


def kernel(x, meta_tokens, ffn1_norm, ffn1_w_gate, ffn1_w_up, ffn1_w_down, mix_norm, w_in, ssm_lam_re, ssm_lam_im, ssm_log_dt, ssm_b_re, ssm_b_im, ssm_c_re, ssm_c_im, ssm_d, ssm_w_glu, attn_sink, w_branch_ssm, w_branch_attn, w_out, ffn2_norm, ffn2_w_gate, ffn2_w_up, ffn2_w_down, final_norm, loss_target, m_meta_tokens, m_ffn1_norm, m_ffn1_w_gate, m_ffn1_w_up, m_ffn1_w_down, m_mix_norm, m_w_in, m_ssm_lam_re, m_ssm_lam_im, m_ssm_log_dt, m_ssm_b_re, m_ssm_b_im, m_ssm_c_re, m_ssm_c_im, m_ssm_d, m_ssm_w_glu, m_attn_sink, m_w_branch_ssm, m_w_branch_attn, m_w_out, m_ffn2_norm, m_ffn2_w_gate, m_ffn2_w_up, m_ffn2_w_down, m_final_norm, v_meta_tokens, v_ffn1_norm, v_ffn1_w_gate, v_ffn1_w_up, v_ffn1_w_down, v_mix_norm, v_w_in, v_ssm_lam_re, v_ssm_lam_im, v_ssm_log_dt, v_ssm_b_re, v_ssm_b_im, v_ssm_c_re, v_ssm_c_im, v_ssm_d, v_ssm_w_glu, v_attn_sink, v_w_branch_ssm, v_w_branch_attn, v_w_out, v_ffn2_norm, v_ffn2_w_gate, v_ffn2_w_up, v_ffn2_w_down, v_final_norm):
    weights = dict(meta_tokens=meta_tokens, ffn1_norm=ffn1_norm, ffn1_w_gate=ffn1_w_gate, ffn1_w_up=ffn1_w_up, ffn1_w_down=ffn1_w_down, mix_norm=mix_norm, w_in=w_in, ssm_lam_re=ssm_lam_re, ssm_lam_im=ssm_lam_im, ssm_log_dt=ssm_log_dt, ssm_b_re=ssm_b_re, ssm_b_im=ssm_b_im, ssm_c_re=ssm_c_re, ssm_c_im=ssm_c_im, ssm_d=ssm_d, ssm_w_glu=ssm_w_glu, attn_sink=attn_sink, w_branch_ssm=w_branch_ssm, w_branch_attn=w_branch_attn, w_out=w_out, ffn2_norm=ffn2_norm, ffn2_w_gate=ffn2_w_gate, ffn2_w_up=ffn2_w_up, ffn2_w_down=ffn2_w_down, final_norm=final_norm)
    mom_m = dict(meta_tokens=m_meta_tokens, ffn1_norm=m_ffn1_norm, ffn1_w_gate=m_ffn1_w_gate, ffn1_w_up=m_ffn1_w_up, ffn1_w_down=m_ffn1_w_down, mix_norm=m_mix_norm, w_in=m_w_in, ssm_lam_re=m_ssm_lam_re, ssm_lam_im=m_ssm_lam_im, ssm_log_dt=m_ssm_log_dt, ssm_b_re=m_ssm_b_re, ssm_b_im=m_ssm_b_im, ssm_c_re=m_ssm_c_re, ssm_c_im=m_ssm_c_im, ssm_d=m_ssm_d, ssm_w_glu=m_ssm_w_glu, attn_sink=m_attn_sink, w_branch_ssm=m_w_branch_ssm, w_branch_attn=m_w_branch_attn, w_out=m_w_out, ffn2_norm=m_ffn2_norm, ffn2_w_gate=m_ffn2_w_gate, ffn2_w_up=m_ffn2_w_up, ffn2_w_down=m_ffn2_w_down, final_norm=m_final_norm)
    mom_v = dict(meta_tokens=v_meta_tokens, ffn1_norm=v_ffn1_norm, ffn1_w_gate=v_ffn1_w_gate, ffn1_w_up=v_ffn1_w_up, ffn1_w_down=v_ffn1_w_down, mix_norm=v_mix_norm, w_in=v_w_in, ssm_lam_re=v_ssm_lam_re, ssm_lam_im=v_ssm_lam_im, ssm_log_dt=v_ssm_log_dt, ssm_b_re=v_ssm_b_re, ssm_b_im=v_ssm_b_im, ssm_c_re=v_ssm_c_re, ssm_c_im=v_ssm_c_im, ssm_d=v_ssm_d, ssm_w_glu=v_ssm_w_glu, attn_sink=v_attn_sink, w_branch_ssm=v_w_branch_ssm, w_branch_attn=v_w_branch_attn, w_out=v_w_out, ffn2_norm=v_ffn2_norm, ffn2_w_gate=v_ffn2_w_gate, ffn2_w_up=v_ffn2_w_up, ffn2_w_down=v_ffn2_w_down, final_norm=v_final_norm)
    names = list(weights)

    L0 = x.shape[1]
    LP = L0 + BLK
    tm = 384 if LP % 384 == 0 else BLK
    x_i, y_i, c_i = lax.axis_index("x"), lax.axis_index("y"), lax.axis_index("c")
    me = 4 * x_i + 2 * y_i + c_i

    def canon(l):
        return dict(
            f1_wgT=ffn1_w_gate[l].T, f1_wuT=ffn1_w_up[l].T, f1_wd=ffn1_w_down[l],
            winT=w_in[l].T, wba=w_branch_attn[l], wout=w_out[l],
            f2_wgT=ffn2_w_gate[l].T, f2_wuT=ffn2_w_up[l].T, f2_wd=ffn2_w_down[l],
            wglu=ssm_w_glu[l], wbsT=w_branch_ssm[l].T)

    shards = [canon(l) for l in range(DEPTH)]
    p1_parts, p1_pieces, p2_parts, p2_pieces = [], [], [], []
    off = 0
    for l in range(DEPTH):
        for nm in W1024:
            a = shards[l][nm].astype(BF16)
            p1_parts.append(a)
            p1_pieces.append((off, a.shape[0]))
            off += a.shape[0]
    off = 0
    for l in range(DEPTH):
        for nm in W512:
            a = shards[l][nm].astype(BF16)
            p2_parts.append(a)
            p2_pieces.append((off, a.shape[0]))
            off += a.shape[0]
    p1 = jnp.concatenate(p1_parts, axis=0)
    p2 = jnp.concatenate(p2_parts, axis=0)
    g1, g2, gm = all_gather_pieces(
        "gather_weights", [(p1, p1_pieces), (p2, p2_pieces), (meta_tokens, [(0, N_META)])])
    full = []
    for l in range(DEPTH):
        d = {nm: g1[l * len(W1024) + j] for j, nm in enumerate(W1024)}
        d.update({nm: g2[l * len(W512) + j] for j, nm in enumerate(W512)})
        full.append(d)
    meta_full = gm[0].reshape(NDEV, N_META, D // NDEV).transpose(1, 0, 2).reshape(N_META, D)

    def disc_all(lre, lim, ldt, bre, bim):
        return _ssm_disc(lre, lim, ldt, bre, bim)

    ssm_p, ssm_vjp = [], []
    for l in range(DEPTH):
        row, vrow = [], []
        for dr in range(2):
            args = (ssm_lam_re[l, dr], ssm_lam_im[l, dr], ssm_log_dt[l, dr], ssm_b_re[l, dr], ssm_b_im[l, dr])
            (a_re, a_im, bb_re, bb_im), vjp = jax.vjp(disc_all, *args)
            row.append(dict(
                bpr=_pack_b(bb_re).astype(BF16), bpi=_pack_b(bb_im).astype(BF16),
                cpr=_pack_c(ssm_c_re[l, dr]).astype(BF16), cpi=_pack_c(ssm_c_im[l, dr]).astype(BF16),
                tab=_scan_tables(args[0], args[1], args[2], False, dr == 1),
                tab_adj=_scan_tables(args[0], args[1], args[2], True, dr == 0),
                a_re=a_re, a_im=a_im))
            vrow.append(vjp)
        ssm_p.append(row)
        ssm_vjp.append(vrow)

    blk0 = jnp.concatenate([jnp.zeros((PAD, D), F32), meta_full.astype(F32)], axis=0)
    h = build_h0(x[0], blk0)
    saved = []
    for l in range(DEPTH):
        w = full[l]
        g1n, g2n = ffn1_norm[l][None, :], ffn2_norm[l][None, :]
        mp = dict(g=mix_norm[l][None, :], winT=w["winT"], wglu=w["wglu"], wbsT=w["wbsT"], wba=w["wba"],
                  wout=w["wout"], d=ssm_d[l][None, :], sink=attn_sink[l], ssm=ssm_p[l])
        h, s1 = ffn_forward(f"l{l}_ffn1", h, g1n, w["f1_wgT"], w["f1_wuT"], w["f1_wd"], tm)
        h, s2 = mixer_forward(f"l{l}_mix", h, mp, tm)
        h, s3 = ffn_forward(f"l{l}_ffn2", h, g2n, w["f2_wgT"], w["f2_wuT"], w["f2_wd"], tm)
        saved.append((s1, s2, s3, mp, g1n, g2n))

    dh, loss_acc, dgf = final_loss(h, loss_target[0], final_norm[None, :])
    loss = lax.psum(loss_acc[0, 0], MESH_AXES)

    big_grads = [None] * DEPTH
    small = {nm: [None] * DEPTH for nm in SMALL if nm != "final_norm"}
    for l in reversed(range(DEPTH)):
        s1, s2, s3, mp, g1n, g2n = saved[l]
        w = full[l]
        dh, dg2, f2g, f2u, f2d = ffn_backward(f"l{l}_ffn2", dh, s3, g2n, w["f2_wgT"], w["f2_wuT"], w["f2_wd"], tm)
        dh, mg = mixer_backward(f"l{l}_mix", dh, s2, mp, tm)
        dh, dg1, f1g, f1u, f1d = ffn_backward(f"l{l}_ffn1", dh, s1, g1n, w["f1_wgT"], w["f1_wuT"], w["f1_wd"], tm)
        big_grads[l] = dict(f1_wgT=f1g, f1_wuT=f1u, f1_wd=f1d, winT=mg["winT"], wba=mg["wba"], wout=mg["wout"],
                            f2_wgT=f2g, f2_wuT=f2u, f2_wd=f2d, wglu=mg["wglu"], wbsT=mg["wbsT"])
        small["ffn1_norm"][l] = dg1[0]
        small["mix_norm"][l] = mg["g"][0]
        small["ffn2_norm"][l] = dg2[0]
        small["ssm_d"][l] = mg["d"][0]
        small["attn_sink"][l] = mg["sink"][0, :N_HEADS]
        per_dir = {k: [] for k in ("ssm_lam_re", "ssm_lam_im", "ssm_log_dt", "ssm_b_re", "ssm_b_im",
                                   "ssm_c_re", "ssm_c_im")}
        for dr in range(2):
            gbr, gbi, gcr, gci, s_re, s_im = mg["ssm"][dr]
            a_re, a_im = ssm_p[l][dr]["a_re"], ssm_p[l][dr]["a_im"]
            s_re = s_re.reshape(SGRP, SP)
            s_im = s_im.reshape(SGRP, SP)
            den = a_re * a_re + a_im * a_im
            ga_re = (s_re * a_re - s_im * a_im) / den
            ga_im = (s_re * a_im + s_im * a_re) / den
            glr, gli, gld, gbre, gbim = ssm_vjp[l][dr]((ga_re, ga_im, _unpack_diag(gbr), _unpack_diag(gbi)))
            per_dir["ssm_lam_re"].append(glr)
            per_dir["ssm_lam_im"].append(gli)
            per_dir["ssm_log_dt"].append(gld)
            per_dir["ssm_b_re"].append(gbre)
            per_dir["ssm_b_im"].append(gbim)
            per_dir["ssm_c_re"].append(_unpack_diag(gcr).transpose(0, 2, 1))
            per_dir["ssm_c_im"].append(_unpack_diag(gci).transpose(0, 2, 1))
        for k, vlist in per_dir.items():
            small[k][l] = jnp.stack(vlist)

    grad_x = dh[BLK:][None]
    dmeta_part = dh[PAD:BLK]

    small_part = {k: jnp.stack(vv) for k, vv in small.items()}
    small_part["final_norm"] = dgf[0]
    pieces = [small_part[k].reshape(-1) for k in SMALL] + [dmeta_part.reshape(-1)]
    sizes = [p_.shape[0] for p_ in pieces]
    total = sum(sizes)
    rows_s = -(-total // (8 * D)) * 8
    flat = jnp.concatenate(pieces + [jnp.zeros((rows_s * D - total,), F32)]).reshape(rows_s, D)
    ((gathered,),) = all_gather_pieces("gather_small_grads", [(flat, [(0, rows_s)])])
    small_sum = sum_slots("sum_small_grads", gathered.reshape(NDEV, rows_s, D)).reshape(-1)
    grads = {}
    o = 0
    for k, sz in zip(SMALL, sizes[:-1]):
        grads[k] = small_sum[o:o + sz].reshape(weights[k].shape)
        o += sz
    dmeta_full = small_sum[o:o + N_META * D].reshape(N_META, D)
    grads["meta_tokens"] = lax.dynamic_slice_in_dim(dmeta_full, me * (D // NDEV), D // NDEV, axis=1)

    land1, land2 = all_to_all_pieces(
        "scatter_weight_grads",
        [[big_grads[l][nm] for l in range(DEPTH) for nm in W1024],
         [big_grads[l][nm] for l in range(DEPTH) for nm in W512]])
    own1 = sum_slots("sum_weight_grads_1024", land1)
    own2 = sum_slots("sum_weight_grads_512", land2)
    own = [dict() for _ in range(DEPTH)]
    for (off_, r), (l, nm) in zip(p1_pieces, [(l, nm) for l in range(DEPTH) for nm in W1024]):
        own[l][nm] = own1[off_:off_ + r]
    for (off_, r), (l, nm) in zip(p2_pieces, [(l, nm) for l in range(DEPTH) for nm in W512]):
        own[l][nm] = own2[off_:off_ + r]

    def stack(fn):
        return jnp.stack([fn(own[l]) for l in range(DEPTH)])

    grads["ffn1_w_gate"] = stack(lambda d: d["f1_wgT"].T)
    grads["ffn1_w_up"] = stack(lambda d: d["f1_wuT"].T)
    grads["ffn1_w_down"] = stack(lambda d: d["f1_wd"])
    grads["w_in"] = stack(lambda d: d["winT"].T)
    grads["ssm_w_glu"] = stack(lambda d: d["wglu"])
    grads["w_branch_ssm"] = stack(lambda d: d["wbsT"].T)
    grads["w_branch_attn"] = stack(lambda d: d["wba"])
    grads["w_out"] = stack(lambda d: d["wout"])
    grads["ffn2_w_gate"] = stack(lambda d: d["f2_wgT"].T)
    grads["ffn2_w_up"] = stack(lambda d: d["f2_wuT"].T)
    grads["ffn2_w_down"] = stack(lambda d: d["f2_wd"])

    deltas, new_m, new_v = {}, {}, {}
    for nm in names:
        deltas[nm], new_m[nm], new_v[nm] = adamw("adamw_" + nm, weights[nm], grads[nm], mom_m[nm], mom_v[nm])

    return (loss, grad_x, *[grads[n] for n in names], *[deltas[n] for n in names],
            *[new_m[n] for n in names], *[new_v[n] for n in names])
```

```python
import functools
import math

import jax
import jax.numpy as jnp
from jax import lax
from jax.experimental import pallas as pl
from jax.experimental.pallas import tpu as pltpu

F32 = jnp.float32
BF16 = jnp.bfloat16

D = 1024
DFF = 2816
N_META = 16
N_HEADS = 16
N_KV = 4
HD = 64
QG = 4
WIN = 128
BLK = 128
PAD = BLK - N_META
SW = 512
SGRP = 32
SCH = 16
SP = 64
NST = SGRP * SP
EPS = 1e-6
NEG = -1e30
SCALE = HD ** -0.5
NDEV = 8
DEPTH = 4
MESH_AXES = ("x", "y", "c")
MESH = pl.DeviceIdType.MESH

ADAM_LR = 0.001
ADAM_B1 = 0.9
ADAM_B2 = 0.999
ADAM_EPS = 1e-08
ADAM_WD = 0.01
ADAM_STEP = 10

VMEM_LIMIT = 56 * 1024 * 1024


def _params(*sem):
    return pltpu.CompilerParams(dimension_semantics=sem, vmem_limit_bytes=VMEM_LIMIT)


def _nn(a, b):
    return lax.dot_general(a, b, (((1,), (0,)), ((), ())), preferred_element_type=F32)


def _nt(a, b):
    return lax.dot_general(a, b, (((1,), (1,)), ((), ())), preferred_element_type=F32)


def _tn(a, b):
    return lax.dot_general(a, b, (((0,), (0,)), ((), ())), preferred_element_type=F32)


def _sig(x):
    return 1.0 / (1.0 + jnp.exp(-x))


def _rms_fwd(h, g):
    r = lax.rsqrt(jnp.mean(h * h, axis=-1, keepdims=True) + EPS)
    hh = h * r
    return hh, r, hh * g


def _rms_bwd(hh, r, g, dn):
    dhh = dn * g
    dx = r * (dhh - hh * jnp.mean(dhh * hh, axis=-1, keepdims=True))
    return dx, jnp.sum(dn * hh, axis=0, keepdims=True)


def _row_ok(i, tm):
    rows = i * tm + lax.broadcasted_iota(jnp.int32, (tm, 1), 0)
    return rows >= PAD


def _const_spec(shape):
    nd = len(shape)
    return pl.BlockSpec(shape, lambda *_: (0,) * nd)


def rowcall(name, body, rows, consts, outs, accs=(), *, tm):
    nrows = rows[0].shape[0]
    nt = nrows // tm
    assert nt * tm == nrows, (name, nrows, tm)
    nr, nc, no, na = len(rows), len(consts), len(outs), len(accs)
    in_specs = [pl.BlockSpec((tm, r.shape[1]), lambda i: (i, 0)) for r in rows]
    in_specs += [_const_spec(c.shape) for c in consts]
    out_shape = [jax.ShapeDtypeStruct((nrows, w), dt) for (w, dt) in outs]
    out_specs = [pl.BlockSpec((tm, w), lambda i: (i, 0)) for (w, dt) in outs]
    out_shape += [jax.ShapeDtypeStruct(s, F32) for s in accs]
    out_specs += [_const_spec(s) for s in accs]

    def kern(*refs):
        i = pl.program_id(0)
        row_vals = [r[...] for r in refs[:nr]]
        res = body(i, *row_vals, *refs[nr:nr + nc])
        out_refs = refs[nr + nc:nr + nc + no]
        acc_refs = refs[nr + nc + no:]
        for r, v in zip(out_refs, res[:no]):
            r[...] = v.astype(r.dtype)
        if na:
            @pl.when(i == 0)
            def _():
                for r in acc_refs:
                    r[...] = jnp.zeros_like(r)
            for r, v in zip(acc_refs, res[no:]):
                r[...] += v

    res = pl.pallas_call(
        kern, name=name, grid=(nt,), in_specs=in_specs, out_specs=out_specs, out_shape=out_shape,
        compiler_params=_params("arbitrary"),
    )(*rows, *consts)
    return res


def tn_matmul(name, lhs, rhs, scale=1.0):
    M, K = lhs.shape
    N = rhs.shape[1]
    tmw = M // 12
    assert tmw * 12 == M and tmw % 16 == 0
    tk = 1408 if (K % 1408 == 0) else K
    nk, nm = K // tk, M // tmw

    def kern(a_ref, b_ref, o_ref, acc):
        m = pl.program_id(1)

        @pl.when(m == 0)
        def _():
            acc[...] = jnp.zeros_like(acc)

        acc[...] += _tn(a_ref[...].astype(BF16), b_ref[...].astype(BF16))

        @pl.when(m == nm - 1)
        def _():
            o_ref[...] = (acc[...] * scale).astype(o_ref.dtype)

    return pl.pallas_call(
        kern, name=name, grid=(nk, nm),
        in_specs=[pl.BlockSpec((tmw, tk), lambda k, m: (m, k)), pl.BlockSpec((tmw, N), lambda k, m: (m, 0))],
        out_specs=pl.BlockSpec((tk, N), lambda k, m: (k, 0)),
        out_shape=jax.ShapeDtypeStruct((K, N), BF16),
        scratch_shapes=[pltpu.VMEM((tk, N), F32)],
        compiler_params=_params("arbitrary", "arbitrary"),
    )(lhs, rhs)


def _mesh_pos():
    x, y, c = lax.axis_index("x"), lax.axis_index("y"), lax.axis_index("c")
    return x, y, c


def all_gather_pieces(name, groups):
    ng = len(groups)
    packed = [g[0] for g in groups]
    pieces = [g[1] for g in groups]
    out_shape, out_map = [], []
    for gi, (p, pcs) in enumerate(groups):
        idx = []
        for (off, r) in pcs:
            idx.append(len(out_shape))
            out_shape.append(jax.ShapeDtypeStruct((NDEV * r, p.shape[1]), p.dtype))
        out_map.append(idx)
    nout = len(out_shape)

    def body(*refs):
        p_refs = refs[:ng]
        o_refs = refs[ng:ng + nout]
        send_sems, recv_sems, local_sems = refs[ng + nout:]
        x, y, c = _mesh_pos()
        me = (x, y, c)
        sibling = (x, y, 1 - c)
        chips = [(1 - x, y), (x, 1 - y), (1 - x, 1 - y)]

        def blk(px, py, pc):
            return 4 * px + 2 * py + pc

        def copies(gi, k, origin, to, from_out):
            cps = []
            for (off, r), oi in zip(pieces[gi], out_map[gi]):
                dst = o_refs[oi].at[pl.ds(origin * r, r), :]
                src = dst if from_out else p_refs[gi].at[pl.ds(off, r), :]
                cps.append(pltpu.make_async_remote_copy(
                    src_ref=src, dst_ref=dst, send_sem=send_sems.at[gi, k], recv_sem=recv_sems.at[gi, k],
                    device_id=to, device_id_type=MESH))
            return cps

        def whole(gi, k):
            return pltpu.make_async_remote_copy(
                src_ref=p_refs[gi], dst_ref=p_refs[gi], send_sem=send_sems.at[gi, k],
                recv_sem=recv_sems.at[gi, k], device_id=me, device_id_type=MESH)

        mine = []
        for gi in range(ng):
            for (off, r), oi in zip(pieces[gi], out_map[gi]):
                mine.append(pltpu.make_async_copy(
                    p_refs[gi].at[pl.ds(off, r), :], o_refs[oi].at[pl.ds(blk(*me) * r, r), :],
                    local_sems.at[gi]))
        for cp in mine:
            cp.start()
        for gi in range(ng):
            for cp in copies(gi, 0, blk(*me), sibling, False):
                cp.start()
            for j, chip in enumerate(chips):
                for cp in copies(gi, 1 + j, blk(*me), (*chip, c), False):
                    cp.start()
        for j, chip in enumerate(chips):
            for gi in range(ng):
                whole(gi, 1 + j).wait_recv()
                for cp in copies(gi, 4 + j, blk(*chip, c), sibling, True):
                    cp.start()
        for gi in range(ng):
            whole(gi, 0).wait_recv()
            for j in range(3):
                whole(gi, 4 + j).wait_recv()
        for gi in range(ng):
            for k in range(7):
                whole(gi, k).wait_send()
            pltpu.make_async_copy(p_refs[gi], p_refs[gi], local_sems.at[gi]).wait()

    any_spec = pl.BlockSpec(memory_space=pl.ANY)
    outs = pl.pallas_call(
        body, name=name, out_shape=out_shape,
        in_specs=[any_spec] * ng, out_specs=[any_spec] * nout,
        scratch_shapes=[pltpu.SemaphoreType.DMA((ng, 7)), pltpu.SemaphoreType.DMA((ng, 7)),
                        pltpu.SemaphoreType.DMA((ng,))],
    )(*packed)
    return [[outs[oi] for oi in idx] for idx in out_map]


HBM_SPEC = pl.BlockSpec(memory_space=pltpu.HBM)
SEM_SPEC = pl.BlockSpec(memory_space=pltpu.SEMAPHORE)
DATAFLOW = pltpu.SideEffectType.DATAFLOW_SIDE_EFFECTING


def _peers(x, y, c):
    return [(x, y, 1 - c), (1 - x, y, c), (x, 1 - y, c), (1 - x, 1 - y, c),
            (1 - x, y, 1 - c), (x, 1 - y, 1 - c), (1 - x, 1 - y, 1 - c)]


def exchange_start(name, arrays, ng, plan):
    n = len(arrays)
    ns = ng * 7

    def body(*refs):
        in_refs = refs[:n]
        send_sems, recv_sems = refs[n:n + ns], refs[n + ns:n + 2 * ns]
        token = refs[-1]
        x, y, c = _mesh_pos()
        me_i = 4 * x + 2 * y + c
        for k, peer in enumerate(_peers(x, y, c)):
            p_i = 4 * peer[0] + 2 * peer[1] + peer[2]
            for src, dst, gi in plan(in_refs, me_i, p_i):
                pltpu.make_async_remote_copy(
                    src_ref=src, dst_ref=dst, send_sem=send_sems[gi * 7 + k], recv_sem=recv_sems[gi * 7 + k],
                    device_id=peer, device_id_type=MESH).start()
        token[...] = jnp.zeros_like(token)

    res = pl.pallas_call(
        body, name=name,
        out_shape=(*[pltpu.SemaphoreType.DMA(())] * (2 * ns),
                   *[pltpu.HBM(a.shape, a.dtype) for a in arrays], jax.ShapeDtypeStruct((8, 128), F32)),
        in_specs=[HBM_SPEC] * n,
        out_specs=(*[SEM_SPEC] * (2 * ns), *[HBM_SPEC] * n, pl.BlockSpec(memory_space=pltpu.VMEM)),
        input_output_aliases={i: 2 * ns + i for i in range(n)},
        compiler_params=pltpu.CompilerParams(has_side_effects=DATAFLOW),
    )(*[pltpu.with_memory_space_constraint(a, pltpu.HBM) for a in arrays])
    return list(res[:ns]), list(res[ns:2 * ns]), list(res[2 * ns:2 * ns + n]), res[-1]


def exchange_wait(name, send_sems, recv_sems, arrays, ng, sized, after):
    n = len(arrays)
    ns = ng * 7

    def body(*refs):
        in_refs = refs[:n]
        s_sems, r_sems = refs[n:n + ns], refs[n + ns:n + 2 * ns]
        x, y, c = _mesh_pos()
        for gi in range(ng):
            view = sized(in_refs, gi)
            for k in range(7):
                w = pltpu.make_async_remote_copy(
                    src_ref=view, dst_ref=view, send_sem=s_sems[gi * 7 + k], recv_sem=r_sems[gi * 7 + k],
                    device_id=(x, y, c), device_id_type=MESH)
                w.wait_send()
                w.wait_recv()

    res = pl.pallas_call(
        body, name=name, out_shape=tuple(pltpu.HBM(a.shape, a.dtype) for a in arrays),
        in_specs=[HBM_SPEC] * n + [SEM_SPEC] * (2 * ns) + [pl.BlockSpec(memory_space=pl.ANY)],
        out_specs=tuple([HBM_SPEC] * n), input_output_aliases={i: i for i in range(n)},
        compiler_params=pltpu.CompilerParams(has_side_effects=DATAFLOW),
    )(*arrays, *send_sems, *recv_sems, after)
    return list(res)


def gather_layer_start(name, packed, pieces):
    ng = len(packed)
    dests = [lax.empty((NDEV * r, p.shape[1]), p.dtype) for p, pcs in zip(packed, pieces) for (_, r) in pcs]

    def plan(refs, me_i, p_i):
        out, di = [], ng
        for gi in range(ng):
            for (off, r) in pieces[gi]:
                out.append((refs[gi].at[pl.ds(off, r), :], refs[di].at[pl.ds(me_i * r, r), :], gi))
                di += 1
        return out

    return exchange_start(name, list(packed) + dests, ng, plan)


def gather_layer_wait(name, handle, ng, after):
    send_sems, recv_sems, arrays, _ = handle
    out = exchange_wait(name, send_sems, recv_sems, arrays, ng, lambda refs, gi: refs[gi], after)
    return out[ng:]


def scatter_layer_start(name, groups):
    ng = len(groups)
    flat = [a for arrs in groups for a in arrs]
    offs, lands = [], []
    for arrs in groups:
        o, off = [], 0
        for a in arrs:
            r = a.shape[0] // NDEV
            o.append((off, r))
            off += r
        offs.append(o)
        lands.append(jnp.zeros((NDEV, off, arrs[0].shape[1]), arrs[0].dtype))
    nin = len(flat)

    def plan(refs, me_i, p_i):
        out, ai = [], 0
        for gi in range(ng):
            for (off, r) in offs[gi]:
                out.append((refs[ai].at[pl.ds(p_i * r, r), :], refs[nin + gi].at[me_i, pl.ds(off, r), :], gi))
                ai += 1
        return out

    return exchange_start(name, flat + lands, ng, plan), nin


def scatter_layer_wait(name, handle, nin, ng, after):
    send_sems, recv_sems, arrays, _ = handle
    out = exchange_wait(name, send_sems, recv_sems, arrays, ng, lambda refs, gi: refs[nin + gi].at[0], after)
    return out[nin:]


def _pick_tile(n, cap):
    best = None
    for t in range(8, min(n, cap) + 1, 8):
        if n % t == 0:
            best = t
    return best if best is not None else n


def sum_slots(name, land):
    _, R, W = land.shape
    tr = _pick_tile(R, 512)

    def kern(l_ref, o_ref):
        acc = l_ref[0].astype(F32)
        for s in range(1, NDEV):
            acc = acc + l_ref[s].astype(F32)
        o_ref[...] = acc

    return pl.pallas_call(
        kern, name=name, grid=(R // tr,),
        in_specs=[pl.BlockSpec((NDEV, tr, W), lambda i: (0, i, 0))],
        out_specs=pl.BlockSpec((tr, W), lambda i: (i, 0)),
        out_shape=jax.ShapeDtypeStruct((R, W), F32),
        compiler_params=_params("arbitrary"),
    )(land)


def adamw(name, w, g, m, v):
    shp = w.shape
    C = shp[-1]
    R = max(1, math.prod(shp[:-1]))
    tr = _pick_tile(R, 1024)
    w2, g2, m2, v2 = (a.reshape(R, C) for a in (w, g, m, v))

    def kern(w_ref, g_ref, m_ref, v_ref, d_ref, mo_ref, vo_ref):
        gg = g_ref[...]
        mn = ADAM_B1 * m_ref[...] + (1.0 - ADAM_B1) * gg
        vn = ADAM_B2 * v_ref[...] + (1.0 - ADAM_B2) * jnp.square(gg)
        m_hat = mn / (1.0 - ADAM_B1 ** ADAM_STEP)
        v_hat = vn / (1.0 - ADAM_B2 ** ADAM_STEP)
        d_ref[...] = -ADAM_LR * (m_hat / (jnp.sqrt(v_hat) + ADAM_EPS) + ADAM_WD * w_ref[...])
        mo_ref[...] = mn
        vo_ref[...] = vn

    spec = pl.BlockSpec((tr, C), lambda i: (i, 0))
    d, mo, vo = pl.pallas_call(
        kern, name=name, grid=(R // tr,), in_specs=[spec] * 4, out_specs=[spec] * 3,
        out_shape=[jax.ShapeDtypeStruct((R, C), F32)] * 3, compiler_params=_params("arbitrary"),
    )(w2, g2, m2, v2)
    return d.reshape(shp), mo.reshape(shp), vo.reshape(shp)


def build_h0(x2, blk0):
    L0 = x2.shape[0]
    nb = L0 // BLK + 1

    def kern(x_ref, b_ref, o_ref):
        i = pl.program_id(0)

        @pl.when(i == 0)
        def _():
            o_ref[...] = b_ref[...]

        @pl.when(i > 0)
        def _():
            o_ref[...] = x_ref[...]

    return pl.pallas_call(
        kern, name="build_h0", grid=(nb,),
        in_specs=[pl.BlockSpec((BLK, D), lambda i: (jnp.maximum(i - 1, 0), 0)), _const_spec((BLK, D))],
        out_specs=pl.BlockSpec((BLK, D), lambda i: (i, 0)),
        out_shape=jax.ShapeDtypeStruct((L0 + BLK, D), F32), compiler_params=_params("arbitrary"),
    )(x2, blk0)


def final_loss(h, tgt, gf):
    LP = h.shape[0]
    nb = LP // BLK

    def kern(h_ref, t_ref, g_ref, dh_ref, loss_ref, dg_ref):
        i = pl.program_id(0)

        @pl.when(i == 0)
        def _():
            loss_ref[...] = jnp.zeros_like(loss_ref)
            dg_ref[...] = jnp.zeros_like(dg_ref)

        g = g_ref[...]
        hh, r, yv = _rms_fwd(h_ref[...], g)
        valid = (i > 0).astype(F32)
        err = (yv - t_ref[...]) * valid
        loss_ref[...] += 0.5 * jnp.sum(jnp.sum(err * err, axis=1, keepdims=True), axis=0, keepdims=True) / D
        dy = err / D
        dx, dg = _rms_bwd(hh, r, g, dy)
        dh_ref[...] = dx
        dg_ref[...] += dg

    return pl.pallas_call(
        kern, name="final_loss", grid=(nb,),
        in_specs=[pl.BlockSpec((BLK, D), lambda i: (i, 0)),
                  pl.BlockSpec((BLK, D), lambda i: (jnp.maximum(i - 1, 0), 0)), _const_spec((1, D))],
        out_specs=[pl.BlockSpec((BLK, D), lambda i: (i, 0)), _const_spec((8, 128)), _const_spec((1, D))],
        out_shape=[jax.ShapeDtypeStruct((LP, D), F32), jax.ShapeDtypeStruct((8, 128), F32),
                   jax.ShapeDtypeStruct((1, D), F32)],
        compiler_params=_params("arbitrary"),
    )(h, tgt, gf)


def ffn_forward(tag, h, g, wgT, wuT, wd, tm):
    def f1(i, hv, g_ref, wg_ref, wu_ref):
        _, _, n = _rms_fwd(hv, g_ref[...])
        nb = n.astype(BF16)
        G = _nt(nb, wg_ref[...])
        U = _nt(nb, wu_ref[...])
        A = G * _sig(G) * U
        return nb, G, U, A

    n, G, U, A = rowcall(tag + "_up", f1, [h], [g, wgT, wuT],
                         [(D, BF16), (DFF, BF16), (DFF, BF16), (DFF, BF16)], tm=tm)

    def f2(i, av, hv, wd_ref):
        return (hv + 0.5 * _nn(av, wd_ref[...]),)

    (h2,) = rowcall(tag + "_down", f2, [A, h], [wd], [(D, F32)], tm=tm)
    return h2, (h, n, G, U, A)


def ffn_backward(tag, dh, saved, g, wgT, wuT, wd, tm):
    h, n, G, U, A = saved

    def b1(i, dhv, Gv, Uv, wd_ref):
        dA = 0.5 * _nt(dhv.astype(BF16), wd_ref[...])
        Gf = Gv.astype(F32)
        Uf = Uv.astype(F32)
        sg = _sig(Gf)
        dG = dA * Uf * sg * (1.0 + Gf * (1.0 - sg))
        dU = dA * Gf * sg
        return dG, dU

    dG, dU = rowcall(tag + "_bwd_act", b1, [dh, G, U], [wd], [(DFF, BF16), (DFF, BF16)], tm=tm)

    def b2(i, dGv, dUv, hv, dhv, g_ref, wg_ref, wu_ref):
        dn = _nn(dGv, wg_ref[...]) + _nn(dUv, wu_ref[...])
        gv = g_ref[...]
        hh, r, _ = _rms_fwd(hv, gv)
        dx, dg = _rms_bwd(hh, r, gv, dn)
        dx = jnp.where(_row_ok(i, tm), dx, 0.0)
        return dhv + dx, dg

    dh2, dg = rowcall(tag + "_bwd_in", b2, [dG, dU, h, dh], [g, wgT, wuT], [(D, F32)], [(1, D)], tm=tm)
    dwd = tn_matmul(tag + "_dwd", A, dh, scale=0.5)
    dwgT = tn_matmul(tag + "_dwg", dG, n)
    dwuT = tn_matmul(tag + "_dwu", dU, n)
    return dh2, dg, dwgT, dwuT, dwd


def _alibi_slope(head):
    return float(2.0 ** (-8.0 * (head + 1) / N_HEADS))


def _att_bias(n, nb):
    qi = lax.broadcasted_iota(jnp.int32, (BLK, 4 * BLK), 0)
    cj = lax.broadcasted_iota(jnp.int32, (BLK, 4 * BLK), 1)
    jb = cj - BLK
    dist = jnp.abs(qi + BLK - jb)
    kpos = (n - 1) * BLK + jb
    band_ok = (dist <= WIN) & (kpos >= BLK) & (kpos < nb * BLK)
    is_meta = cj < BLK
    ok = (is_meta & (cj >= PAD)) | (jnp.logical_not(is_meta) & band_ok)
    distf = jnp.where(is_meta, 0, dist).astype(F32)
    maskadd = jnp.where(ok, 0.0, NEG).astype(F32)
    distf4 = jnp.concatenate([distf] * QG, axis=0)
    mask4 = jnp.concatenate([maskadd] * QG, axis=0)
    return distf4, mask4


def _group_col(vals):
    rg = lax.broadcasted_iota(jnp.int32, (QG * BLK, 1), 0) // BLK
    col = jnp.full((QG * BLK, 1), vals[QG - 1], F32)
    for gq in range(QG - 2, -1, -1):
        col = jnp.where(rg == gq, vals[gq], col)
    return col


def _stack_heads(ref_or_val, kh):
    return jnp.concatenate(
        [ref_or_val[:, (kh * QG + gq) * HD:(kh * QG + gq + 1) * HD] for gq in range(QG)], axis=0)


def _stack_keys(km, kp, kc, kn, kh):
    sl = slice(kh * HD, (kh + 1) * HD)
    return jnp.concatenate([km[:, sl], kp[:, sl], kc[:, sl], kn[:, sl]], axis=0)


def _att_probs(qs, kb, distf4, mask4, kh, sink_ref):
    slope_col = _group_col([_alibi_slope(kh * QG + gq) for gq in range(QG)])
    sink_col = _group_col([sink_ref[kh * QG + gq] for gq in range(QG)])
    s = _nt(qs, kb) * SCALE + (mask4 - slope_col * distf4)
    m = jnp.maximum(jnp.max(s, axis=1, keepdims=True), sink_col)
    e = jnp.exp(s - m)
    es = jnp.exp(sink_col - m)
    inv = 1.0 / (jnp.sum(e, axis=1, keepdims=True) + es)
    return e * inv, es * inv


def attention_forward(tag, q, k, v, sink):
    LP = q.shape[0]
    nb = LP // BLK

    def kern(sink_ref, q_ref, km_ref, kp_ref, kc_ref, kn_ref, vm_ref, vp_ref, vc_ref, vn_ref, o_ref):
        n = pl.program_id(0)
        distf4, mask4 = _att_bias(n, nb)
        qv = q_ref[...]
        km, kp, kc, kn = km_ref[...], kp_ref[...], kc_ref[...], kn_ref[...]
        vm, vp, vc, vn = vm_ref[...], vp_ref[...], vc_ref[...], vn_ref[...]
        for kh in range(N_KV):
            qs = _stack_heads(qv, kh)
            kb = _stack_keys(km, kp, kc, kn, kh)
            vb = _stack_keys(vm, vp, vc, vn, kh)
            p, _ = _att_probs(qs, kb, distf4, mask4, kh, sink_ref)
            o = _nn(p.astype(BF16), vb)
            for gq in range(QG):
                hcol = (kh * QG + gq) * HD
                o_ref[:, hcol:hcol + HD] = o[gq * BLK:(gq + 1) * BLK].astype(o_ref.dtype)

    def kvspec(dn):
        return pl.BlockSpec((BLK, N_KV * HD), lambda n: (jnp.clip(n + dn, 0, nb - 1), 0))

    meta_spec = pl.BlockSpec((BLK, N_KV * HD), lambda n: (0, 0))
    return pl.pallas_call(
        kern, name=tag + "_att_fwd", grid=(nb,),
        in_specs=[pl.BlockSpec(memory_space=pltpu.SMEM), pl.BlockSpec((BLK, D), lambda n: (n, 0)),
                  meta_spec, kvspec(-1), kvspec(0), kvspec(1), meta_spec, kvspec(-1), kvspec(0), kvspec(1)],
        out_specs=pl.BlockSpec((BLK, D), lambda n: (n, 0)),
        out_shape=jax.ShapeDtypeStruct((LP, D), BF16), compiler_params=_params("arbitrary"),
    )(sink, q, k, k, k, k, v, v, v, v)


def attention_backward(tag, q, k, v, do, sink):
    LP = q.shape[0]
    nb = LP // BLK
    KW = N_KV * HD

    def kern(sink_ref, q_ref, do_ref, km_ref, kp_ref, kc_ref, kn_ref, vm_ref, vp_ref, vc_ref, vn_ref,
             dq_ref, dkp_ref, dvp_ref, dkm_ref, dvm_ref, dsink_ref):
        n = pl.program_id(0)

        @pl.when(n == 0)
        def _():
            dkm_ref[...] = jnp.zeros_like(dkm_ref)
            dvm_ref[...] = jnp.zeros_like(dvm_ref)
            dsink_ref[...] = jnp.zeros_like(dsink_ref)

        distf4, mask4 = _att_bias(n, nb)
        qv, dov = q_ref[...], do_ref[...]
        km, kp, kc, kn = km_ref[...], kp_ref[...], kc_ref[...], kn_ref[...]
        vm, vp, vc, vn = vm_ref[...], vp_ref[...], vc_ref[...], vn_ref[...]
        lane = lax.broadcasted_iota(jnp.int32, (8, 128), 1)
        dsink = jnp.zeros((8, 128), F32)
        for kh in range(N_KV):
            qs = _stack_heads(qv, kh)
            dos = _stack_heads(dov, kh)
            kb = _stack_keys(km, kp, kc, kn, kh)
            vb = _stack_keys(vm, vp, vc, vn, kh)
            p, ps = _att_probs(qs, kb, distf4, mask4, kh, sink_ref)
            dp = _nt(dos, vb)
            delta = jnp.sum(p * dp, axis=1, keepdims=True)
            ds = (p * (dp - delta)).astype(BF16)
            dqs = _nn(ds, kb) * SCALE
            dkb = _tn(ds, qs) * SCALE
            dvb = _tn(p.astype(BF16), dos)
            dsk = -(ps * delta)
            sl = slice(kh * HD, (kh + 1) * HD)
            for gq in range(QG):
                hcol = (kh * QG + gq) * HD
                dq_ref[:, hcol:hcol + HD] = dqs[gq * BLK:(gq + 1) * BLK].astype(dq_ref.dtype)
                tot = jnp.sum(dsk[gq * BLK:(gq + 1) * BLK], axis=0, keepdims=True)
                dsink = dsink + jnp.where(lane == kh * QG + gq, tot, 0.0)
            dkm_ref[:, sl] += dkb[0:BLK]
            dvm_ref[:, sl] += dvb[0:BLK]
            for slot in range(3):
                dkp_ref[0, slot, :, sl] = dkb[(slot + 1) * BLK:(slot + 2) * BLK]
                dvp_ref[0, slot, :, sl] = dvb[(slot + 1) * BLK:(slot + 2) * BLK]
        dsink_ref[...] += dsink

    def kvspec(dn):
        return pl.BlockSpec((BLK, KW), lambda n: (jnp.clip(n + dn, 0, nb - 1), 0))

    meta_spec = pl.BlockSpec((BLK, KW), lambda n: (0, 0))
    rowspec = pl.BlockSpec((BLK, D), lambda n: (n, 0))
    part_spec = pl.BlockSpec((1, 3, BLK, KW), lambda n: (n, 0, 0, 0))
    dq, dkp, dvp, dkm, dvm, dsink = pl.pallas_call(
        kern, name=tag + "_att_bwd", grid=(nb,),
        in_specs=[pl.BlockSpec(memory_space=pltpu.SMEM), rowspec, rowspec,
                  meta_spec, kvspec(-1), kvspec(0), kvspec(1), meta_spec, kvspec(-1), kvspec(0), kvspec(1)],
        out_specs=[rowspec, part_spec, part_spec, _const_spec((BLK, KW)), _const_spec((BLK, KW)),
                   _const_spec((8, 128))],
        out_shape=[jax.ShapeDtypeStruct((LP, D), BF16), jax.ShapeDtypeStruct((nb, 3, BLK, KW), F32),
                   jax.ShapeDtypeStruct((nb, 3, BLK, KW), F32), jax.ShapeDtypeStruct((BLK, KW), F32),
                   jax.ShapeDtypeStruct((BLK, KW), F32), jax.ShapeDtypeStruct((8, 128), F32)],
        compiler_params=_params("arbitrary"),
    )(sink, q, do, k, k, k, k, v, v, v, v)

    def comb(a_ref, b_ref, c_ref, m_ref, a2_ref, b2_ref, c2_ref, m2_ref, dk_ref, dv_ref):
        mblk = pl.program_id(0)
        has_prev = (mblk > 0).astype(F32)
        has_next = (mblk < nb - 1).astype(F32)
        is0 = (mblk == 0).astype(F32)
        dk_ref[...] = (a_ref[0, 0] * has_prev + b_ref[0, 0] + c_ref[0, 0] * has_next
                       + m_ref[...] * is0).astype(dk_ref.dtype)
        dv_ref[...] = (a2_ref[0, 0] * has_prev + b2_ref[0, 0] + c2_ref[0, 0] * has_next
                       + m2_ref[...] * is0).astype(dv_ref.dtype)

    def pspec(dn, slot):
        return pl.BlockSpec((1, 1, BLK, KW), lambda m: (jnp.clip(m + dn, 0, nb - 1), slot, 0, 0))

    kvout = pl.BlockSpec((BLK, KW), lambda m: (m, 0))
    dk, dv = pl.pallas_call(
        comb, name=tag + "_att_dkv", grid=(nb,),
        in_specs=[pspec(-1, 2), pspec(0, 1), pspec(1, 0), _const_spec((BLK, KW)),
                  pspec(-1, 2), pspec(0, 1), pspec(1, 0), _const_spec((BLK, KW))],
        out_specs=[kvout, kvout],
        out_shape=[jax.ShapeDtypeStruct((LP, KW), BF16)] * 2, compiler_params=_params("arbitrary"),
    )(dkp, dkp, dkp, dkm, dvp, dvp, dvp, dvm)
    return dq, dk, dv, dsink


SCAN_LANES = 1024


def _scan_tile(xr, xi, cr, ci, a8, tab, seg, reverse):
    sub = lax.broadcasted_iota(jnp.int32, (8, SCAN_LANES), 0)
    for c0 in range(0, NST, SCAN_LANES):
        cs = pl.ds(c0, SCAN_LANES)
        ar = a8[0, :, cs]
        ai = a8[1, :, cs]

        def rows(j):
            jj = (seg - 1 - j) if reverse else j
            return pl.ds(pl.multiple_of(jj * 8, 8), 8)

        def step1(j, carry):
            vr, vi = carry
            rs = rows(j)
            nr = ar * vr - ai * vi + xr[rs, cs]
            ni = ar * vi + ai * vr + xi[rs, cs]
            xr[rs, cs] = nr
            xi[rs, cs] = ni
            return nr, ni

        zero = jnp.zeros((8, SCAN_LANES), F32)
        vr, vi = lax.fori_loop(0, seg, step1, (zero, zero), unroll=2)
        for t, s in enumerate((1, 2, 4)):
            sh = (8 - s) if reverse else s
            sr = pltpu.roll(vr, sh, 0)
            si = pltpu.roll(vi, sh, 0)
            tr = tab[2 * t, :, cs]
            ti = tab[2 * t + 1, :, cs]
            vr, vi = vr + tr * sr - ti * si, vi + tr * si + ti * sr
        pr = tab[6, :, cs]
        pi = tab[7, :, cs]
        c_r = cr[:, cs]
        c_i = ci[:, cs]
        vr, vi = vr + pr * c_r - pi * c_i, vi + pr * c_i + pi * c_r
        edge = 7 if reverse else 0
        last = 0 if reverse else 7
        sh = 7 if reverse else 1
        in_r = jnp.where(sub == edge, c_r, pltpu.roll(vr, sh, 0))
        in_i = jnp.where(sub == edge, c_i, pltpu.roll(vi, sh, 0))
        cr[:, cs] = jnp.broadcast_to(vr[last:last + 1, :], (8, SCAN_LANES))
        ci[:, cs] = jnp.broadcast_to(vi[last:last + 1, :], (8, SCAN_LANES))

        def step2(j, carry):
            dr, di = carry
            rs = rows(j)
            ndr = ar * dr - ai * di
            ndi = ar * di + ai * dr
            xr[rs, cs] += ndr
            xi[rs, cs] += ndi
            return ndr, ndi

        lax.fori_loop(0, seg, step2, (in_r, in_i), unroll=2)


ST_T = 4 * SP * 2
CH_T = 128


def _load_segmented(ref, scr, seg):
    out = []
    for ct in range(4):
        scr[ct] = ref[:, ct * CH_T:(ct + 1) * CH_T]
        out.append(jnp.concatenate([scr[ct, pl.ds(j, 8, stride=seg), :] for j in range(seg)], axis=0))
    return out


def _store_segmented(ref, scr, vals, seg):
    for ct in range(4):
        for j in range(seg):
            scr[ct, pl.ds(j, 8, stride=seg), :] = vals[ct][8 * j:8 * j + 8]
        ref[:, ct * CH_T:(ct + 1) * CH_T] = scr[ct]


def ssm_dir_forward(tag, u, bpr, bpi, cpr, cpi, a8, tab, reverse, tm):
    LP = u.shape[0]
    nt = LP // tm
    seg = tm // 8

    def rix(i):
        return (nt - 1 - i) if reverse else i

    def kern(u_ref, bpr_ref, bpi_ref, cpr_ref, cpi_ref, a8_ref, tab_ref, xre_ref, xim_ref, y_ref,
             xr, xi, ys, cr, ci):
        i = pl.program_id(0)

        @pl.when(i == 0)
        def _():
            cr[...] = jnp.zeros_like(cr)
            ci[...] = jnp.zeros_like(ci)

        ub = _load_segmented(u_ref, ys, seg)
        for ct in range(4):
            uc = ub[ct].astype(BF16)
            xr[:, ct * ST_T:(ct + 1) * ST_T] = _nn(uc, bpr_ref[ct * CH_T:(ct + 1) * CH_T, :])
            xi[:, ct * ST_T:(ct + 1) * ST_T] = _nn(uc, bpi_ref[ct * CH_T:(ct + 1) * CH_T, :])
        _scan_tile(xr, xi, cr, ci, a8_ref, tab_ref, seg, reverse)
        xrb = xr[...].astype(BF16)
        xib = xi[...].astype(BF16)
        xre_ref[...] = xrb
        xim_ref[...] = xib
        yv = []
        for ct in range(4):
            ss = slice(ct * ST_T, (ct + 1) * ST_T)
            yv.append(_nn(xrb[:, ss], cpr_ref[ss, :]) - _nn(xib[:, ss], cpi_ref[ss, :]))
        _store_segmented(y_ref, ys, yv, seg)

    row = lambda w: pl.BlockSpec((tm, w), lambda i: (rix(i), 0))
    return pl.pallas_call(
        kern, name=tag, grid=(nt,),
        in_specs=[row(SW), _const_spec(bpr.shape), _const_spec(bpi.shape), _const_spec(cpr.shape),
                  _const_spec(cpi.shape), _const_spec(a8.shape), _const_spec(tab.shape)],
        out_specs=[row(NST), row(NST), row(SW)],
        out_shape=[jax.ShapeDtypeStruct((LP, NST), BF16), jax.ShapeDtypeStruct((LP, NST), BF16),
                   jax.ShapeDtypeStruct((LP, SW), F32)],
        scratch_shapes=[pltpu.VMEM((tm, NST), F32), pltpu.VMEM((tm, NST), F32), pltpu.VMEM((4, tm, CH_T), F32),
                        pltpu.VMEM((8, NST), F32), pltpu.VMEM((8, NST), F32)],
        compiler_params=_params("arbitrary"),
    )(u, bpr, bpi, cpr, cpi, a8, tab)


def ssm_dir_backward(tag, dy, xre, xim, u, bpr, bpi, cpr, cpi, a8_adj, tab_adj, reverse, tm):
    LP = u.shape[0]
    nt = LP // tm
    seg = tm // 8

    def rix(i):
        return (nt - 1 - i) if reverse else i

    def kern(dy_ref, xre_ref, xim_ref, u_ref, bpr_ref, bpi_ref, cpr_ref, cpi_ref, a8_ref, tab_ref,
             du_ref, gbr_ref, gbi_ref, gcr_ref, gci_ref, sr_ref, si_ref, lr, li, gr, gi, dus, cr, ci):
        i = pl.program_id(0)

        @pl.when(i == 0)
        def _():
            cr[...] = jnp.zeros_like(cr)
            ci[...] = jnp.zeros_like(ci)
            for r in (gbr_ref, gbi_ref, gcr_ref, gci_ref, sr_ref, si_ref):
                r[...] = jnp.zeros_like(r)

        dyb = [v.astype(BF16) for v in _load_segmented(dy_ref, dus, seg)]
        ub = [v.astype(BF16) for v in _load_segmented(u_ref, dus, seg)]
        for ct in range(4):
            ss = slice(ct * ST_T, (ct + 1) * ST_T)
            dc = dyb[ct]
            g_re = _nt(dc, cpr_ref[ss, :])
            g_im = -_nt(dc, cpi_ref[ss, :])
            lr[:, ss] = g_re
            li[:, ss] = g_im
            gr[:, ss] = g_re
            gi[:, ss] = g_im
        _scan_tile(lr, li, cr, ci, a8_ref, tab_ref, seg, reverse)
        lam_r = lr[...]
        lam_i = li[...]
        wr = lam_r - gr[...]
        wi = lam_i - gi[...]
        xr = xre_ref[...].astype(F32)
        xi = xim_ref[...].astype(F32)
        sr_ref[...] += jnp.sum(wr * xr + wi * xi, axis=0, keepdims=True)
        si_ref[...] += jnp.sum(wi * xr - wr * xi, axis=0, keepdims=True)
        lrb = lam_r.astype(BF16)
        lib = lam_i.astype(BF16)
        xrb = xre_ref[...]
        xib = xim_ref[...]
        duv = []
        for ct in range(4):
            ss = slice(ct * ST_T, (ct + 1) * ST_T)
            cs = slice(ct * CH_T, (ct + 1) * CH_T)
            duv.append(_nt(lrb[:, ss], bpr_ref[cs, :]) + _nt(lib[:, ss], bpi_ref[cs, :]))
            gbr_ref[ss, :] += _tn(lrb[:, ss], ub[ct])
            gbi_ref[ss, :] += _tn(lib[:, ss], ub[ct])
            gcr_ref[ss, :] += _tn(xrb[:, ss], dyb[ct])
            gci_ref[ss, :] -= _tn(xib[:, ss], dyb[ct])
        _store_segmented(du_ref, dus, duv, seg)

    row = lambda w: pl.BlockSpec((tm, w), lambda i: (rix(i), 0))
    acc = _const_spec((NST, CH_T))
    vec = _const_spec((1, NST))
    return pl.pallas_call(
        kern, name=tag, grid=(nt,),
        in_specs=[row(SW), row(NST), row(NST), row(SW), _const_spec(bpr.shape), _const_spec(bpi.shape),
                  _const_spec(cpr.shape), _const_spec(cpi.shape), _const_spec(a8_adj.shape),
                  _const_spec(tab_adj.shape)],
        out_specs=[row(SW), acc, acc, acc, acc, vec, vec],
        out_shape=[jax.ShapeDtypeStruct((LP, SW), F32)] + [jax.ShapeDtypeStruct((NST, CH_T), F32)] * 4
        + [jax.ShapeDtypeStruct((1, NST), F32)] * 2,
        scratch_shapes=[pltpu.VMEM((tm, NST), F32)] * 4 + [pltpu.VMEM((4, tm, CH_T), F32)]
        + [pltpu.VMEM((8, NST), F32)] * 2,
        compiler_params=_params("arbitrary"),
    )(dy, xre, xim, u, bpr, bpi, cpr, cpi, a8_adj, tab_adj)


def _ssm_disc(lam_re, lam_im, log_dt, b_re, b_im):
    dt = jnp.exp(log_dt)[:, None]
    mag = jnp.exp(lam_re * dt)
    a_re = mag * jnp.cos(lam_im * dt)
    a_im = mag * jnp.sin(lam_im * dt)
    den = lam_re * lam_re + lam_im * lam_im
    f_re = ((a_re - 1.0) * lam_re + a_im * lam_im) / den
    f_im = (a_im * lam_re - (a_re - 1.0) * lam_im) / den
    bb_re = f_re[:, :, None] * b_re - f_im[:, :, None] * b_im
    bb_im = f_re[:, :, None] * b_im + f_im[:, :, None] * b_re
    return a_re, a_im, bb_re, bb_im


def _scan_tables(lam_re, lam_im, log_dt, conj, reverse, seg):
    dt = jnp.exp(log_dt)[:, None]
    lr = (lam_re * dt).reshape(1, NST)
    li = (lam_im * dt).reshape(1, NST) * (-1.0 if conj else 1.0)
    t = jnp.arange(8, dtype=F32)[:, None]

    def power(kk):
        mag = jnp.exp(kk * lr)
        return mag * jnp.cos(kk * li), mag * jnp.sin(kk * li)

    ones = jnp.ones((8, 1), F32)
    a8 = jnp.stack(power(ones)).astype(F32)
    tabs = []
    for s in (1, 2, 4):
        mask = (t <= 7 - s) if reverse else (t >= s)
        pr, pi = power(float(s * seg) * ones)
        tabs += [jnp.where(mask, pr, 0.0), jnp.where(mask, pi, 0.0)]
    kk = ((8.0 - t) if reverse else (t + 1.0)) * float(seg)
    pr, pi = power(kk)
    tabs += [pr, pi]
    return a8, jnp.stack(tabs).astype(F32)


def _pack_b(bb):
    t = bb.transpose(0, 2, 1).reshape(4, 8, SCH, SP)
    eye = jnp.eye(8, dtype=bb.dtype)
    return jnp.einsum('tgcp,gh->tgchp', t, eye).reshape(SW, ST_T)


def _pack_c(cc):
    t = cc.transpose(0, 2, 1).reshape(4, 8, SP, SCH)
    eye = jnp.eye(8, dtype=cc.dtype)
    return jnp.einsum('tgpc,gh->tgphc', t, eye).reshape(NST, CH_T)


def _unpack_diag(acc):
    t = acc.reshape(4, 8, SP, 8, SCH)
    eye = jnp.eye(8, dtype=acc.dtype)
    return jnp.einsum('tgphc,gh->tgpc', t, eye).reshape(SGRP, SP, SCH)


def _gelu(y):
    k0 = math.sqrt(2.0 / math.pi)
    inner = k0 * (y + 0.044715 * y * y * y)
    th = jnp.tanh(inner)
    z = 0.5 * y * (1.0 + th)
    dz = 0.5 * (1.0 + th) + 0.5 * y * (1.0 - th * th) * k0 * (1.0 + 3.0 * 0.044715 * y * y)
    return z, dz


Q0, K0, V0, U0, GS0, GA0, IN_COLS = 0, 1024, 1280, 1536, 2048, 3072, 4096


def mixer_forward(tag, h, p, tm):
    g, winT, wglu, wbsT, wba, wout = p["g"], p["winT"], p["wglu"], p["wbsT"], p["wba"], p["wout"]

    def proj(i, hv, g_ref, w_ref):
        _, _, n = _rms_fwd(hv, g_ref[...])
        nb = n.astype(BF16)
        return (nb, _nt(nb, w_ref[Q0:K0, :]), _nt(nb, w_ref[K0:V0, :]), _nt(nb, w_ref[V0:U0, :]),
                _nt(nb, w_ref[U0:GS0, :]), _nt(nb, w_ref[GS0:GA0, :]), _nt(nb, w_ref[GA0:IN_COLS, :]))

    n, q, k, v, u, gs, ga = rowcall(
        tag + "_proj", proj, [h], [g, winT],
        [(D, BF16), (D, BF16), (N_KV * HD, BF16), (N_KV * HD, BF16), (SW, F32), (D, F32), (D, F32)], tm=tm)

    ya = attention_forward(tag, q, k, v, p["sink"])

    states, ydir = [], []
    for dr in range(2):
        s = p["ssm"][dr]
        xre, xim, yd = ssm_dir_forward(f"{tag}_ssm_fwd{dr}", u, s["bpr"], s["bpi"], s["cpr"], s["cpi"],
                                       s["a8"], s["tab"], dr == 1, tm)
        states.append((xre, xim))
        ydir.append(yd)

    def glu(i, y0, y1, uv, d_ref, w_ref):
        ypre = y0 + y1 + d_ref[...] * uv
        z, _ = _gelu(ypre)
        zb = z.astype(BF16)
        t = _nn(zb, w_ref[...])
        return ypre, zb, t, z * _sig(t)

    ypre, zb, t, ys = rowcall(tag + "_glu", glu, [ydir[0], ydir[1], u], [p["d"], wglu],
                              [(SW, F32), (SW, BF16), (SW, F32), (SW, BF16)], tm=tm)

    def merge(i, ysv, yav, gsv, gav, wbs_ref, wba_ref):
        bs = _nt(ysv, wbs_ref[...])
        ba = _nn(yav, wba_ref[...])
        mg = _sig(gsv) * bs + _sig(gav) * ba
        mg = jnp.where(_row_ok(i, tm), mg, 0.0)
        return bs, ba, mg

    bs, ba, mg = rowcall(tag + "_merge", merge, [ys, ya, gs, ga], [wbsT, wba],
                         [(D, BF16), (D, BF16), (D, BF16)], tm=tm)

    def outp(i, mv, hv, w_ref):
        return (hv + _nn(mv, w_ref[...]),)

    (h2,) = rowcall(tag + "_out", outp, [mg, h], [wout], [(D, F32)], tm=tm)
    saved = dict(h=h, n=n, q=q, k=k, v=v, u=u, gs=gs, ga=ga, ya=ya, states=states, ypre=ypre, zb=zb, t=t,
                 ys=ys, bs=bs, ba=ba, mg=mg)
    return h2, saved


def mixer_backward(tag, dh, sv, p, tm):
    g, winT, wglu, wbsT, wba, wout = p["g"], p["winT"], p["wglu"], p["wbsT"], p["wba"], p["wout"]

    def y1(i, dhv, bsv, bav, gsv, gav, w_ref):
        dmg = _nt(dhv.astype(BF16), w_ref[...])
        dmg = jnp.where(_row_ok(i, tm), dmg, 0.0)
        sgs = _sig(gsv)
        sga = _sig(gav)
        return (dmg * sgs, dmg * sga, dmg * bsv.astype(F32) * sgs * (1.0 - sgs),
                dmg * bav.astype(F32) * sga * (1.0 - sga))

    dbs, dba, dgs, dga = rowcall(tag + "_bwd_merge", y1, [dh, sv["bs"], sv["ba"], sv["gs"], sv["ga"]], [wout],
                                 [(D, BF16)] * 4, tm=tm)
    dwout = tn_matmul(tag + "_dwout", sv["mg"], dh)

    def y2(i, dbsv, dbav, wbs_ref, wba_ref):
        return _nn(dbsv, wbs_ref[...]), _nt(dbav, wba_ref[...])

    dys, dya = rowcall(tag + "_bwd_branch", y2, [dbs, dba], [wbsT, wba], [(SW, F32), (D, BF16)], tm=tm)
    dwbsT = tn_matmul(tag + "_dwbs", dbs, sv["ys"])
    dwba = tn_matmul(tag + "_dwba", sv["ya"], dba)

    def s2b(i, dysv, ypv, tv, uv, d_ref, w_ref):
        z, dz_dy = _gelu(ypv)
        st = _sig(tv)
        dt_ = dysv * z * st * (1.0 - st)
        dz = dysv * st + _nt(dt_.astype(BF16), w_ref[...])
        dyp = dz * dz_dy
        return dyp, dyp * d_ref[...], dt_, jnp.sum(dyp * uv, axis=0, keepdims=True)

    dypb, du0, dtb, dd = rowcall(tag + "_bwd_glu", s2b, [dys, sv["ypre"], sv["t"], sv["u"]], [p["d"], wglu],
                                 [(SW, F32), (SW, F32), (SW, BF16)], [(1, SW)], tm=tm)
    dwglu = tn_matmul(tag + "_dwglu", sv["zb"], dtb)

    du_dirs, ssm_sums = [], []
    for dr in range(2):
        s = p["ssm"][dr]
        xre, xim = sv["states"][dr]
        res = ssm_dir_backward(f"{tag}_ssm_bwd{dr}", dypb, xre, xim, sv["u"], s["bpr"], s["bpi"], s["cpr"],
                               s["cpi"], s["a8_adj"], s["tab_adj"], dr == 0, tm)
        du_dirs.append(res[0])
        ssm_sums.append(res[1:])

    dq, dk, dv, dsink = attention_backward(tag, sv["q"], sv["k"], sv["v"], dya, p["sink"])

    def x1b(i, dqv, dkv, dvv, du0v, du1v, du2v, dgsv, dgav, hv, dhv, g_ref, w_ref):
        dub = (du0v + du1v + du2v).astype(BF16)
        dn = (_nn(dqv, w_ref[Q0:K0, :]) + _nn(dkv, w_ref[K0:V0, :]) + _nn(dvv, w_ref[V0:U0, :])
              + _nn(dub, w_ref[U0:GS0, :]) + _nn(dgsv, w_ref[GS0:GA0, :]) + _nn(dgav, w_ref[GA0:IN_COLS, :]))
        gv = g_ref[...]
        hh, r, _ = _rms_fwd(hv, gv)
        dx, dg = _rms_bwd(hh, r, gv, dn)
        dx = jnp.where(_row_ok(i, tm), dx, 0.0)
        return dhv + dx, dub, dg

    dh2, dub, dg = rowcall(tag + "_bwd_in", x1b,
                           [dq, dk, dv, du0, du_dirs[0], du_dirs[1], dgs, dga, sv["h"], dh], [g, winT],
                           [(D, F32), (SW, BF16)], [(1, D)], tm=tm)
    n = sv["n"]
    dwinT = jnp.concatenate([tn_matmul(f"{tag}_dwin{j}", piece, n)
                             for j, piece in enumerate((dq, dk, dv, dub, dgs, dga))], axis=0)
    grads = dict(g=dg, d=dd, sink=dsink, ssm=ssm_sums, winT=dwinT, wglu=dwglu, wbsT=dwbsT, wba=dwba, wout=dwout)
    return dh2, grads


W1024 = ("f1_wgT", "f1_wuT", "f1_wd", "winT", "wba", "wout", "f2_wgT", "f2_wuT", "f2_wd")
W512 = ("wglu", "wbsT")
SMALL = ("ffn1_norm", "mix_norm", "ffn2_norm", "final_norm", "ssm_lam_re", "ssm_lam_im", "ssm_log_dt",
         "ssm_b_re", "ssm_b_im", "ssm_c_re", "ssm_c_im", "ssm_d", "attn_sink")


def kernel(x, meta_tokens, ffn1_norm, ffn1_w_gate, ffn1_w_up, ffn1_w_down, mix_norm, w_in, ssm_lam_re, ssm_lam_im, ssm_log_dt, ssm_b_re, ssm_b_im, ssm_c_re, ssm_c_im, ssm_d, ssm_w_glu, attn_sink, w_branch_ssm, w_branch_attn, w_out, ffn2_norm, ffn2_w_gate, ffn2_w_up, ffn2_w_down, final_norm, loss_target, m_meta_tokens, m_ffn1_norm, m_ffn1_w_gate, m_ffn1_w_up, m_ffn1_w_down, m_mix_norm, m_w_in, m_ssm_lam_re, m_ssm_lam_im, m_ssm_log_dt, m_ssm_b_re, m_ssm_b_im, m_ssm_c_re, m_ssm_c_im, m_ssm_d, m_ssm_w_glu, m_attn_sink, m_w_branch_ssm, m_w_branch_attn, m_w_out, m_ffn2_norm, m_ffn2_w_gate, m_ffn2_w_up, m_ffn2_w_down, m_final_norm, v_meta_tokens, v_ffn1_norm, v_ffn1_w_gate, v_ffn1_w_up, v_ffn1_w_down, v_mix_norm, v_w_in, v_ssm_lam_re, v_ssm_lam_im, v_ssm_log_dt, v_ssm_b_re, v_ssm_b_im, v_ssm_c_re, v_ssm_c_im, v_ssm_d, v_ssm_w_glu, v_attn_sink, v_w_branch_ssm, v_w_branch_attn, v_w_out, v_ffn2_norm, v_ffn2_w_gate, v_ffn2_w_up, v_ffn2_w_down, v_final_norm):
    weights = dict(meta_tokens=meta_tokens, ffn1_norm=ffn1_norm, ffn1_w_gate=ffn1_w_gate, ffn1_w_up=ffn1_w_up, ffn1_w_down=ffn1_w_down, mix_norm=mix_norm, w_in=w_in, ssm_lam_re=ssm_lam_re, ssm_lam_im=ssm_lam_im, ssm_log_dt=ssm_log_dt, ssm_b_re=ssm_b_re, ssm_b_im=ssm_b_im, ssm_c_re=ssm_c_re, ssm_c_im=ssm_c_im, ssm_d=ssm_d, ssm_w_glu=ssm_w_glu, attn_sink=attn_sink, w_branch_ssm=w_branch_ssm, w_branch_attn=w_branch_attn, w_out=w_out, ffn2_norm=ffn2_norm, ffn2_w_gate=ffn2_w_gate, ffn2_w_up=ffn2_w_up, ffn2_w_down=ffn2_w_down, final_norm=final_norm)
    mom_m = dict(meta_tokens=m_meta_tokens, ffn1_norm=m_ffn1_norm, ffn1_w_gate=m_ffn1_w_gate, ffn1_w_up=m_ffn1_w_up, ffn1_w_down=m_ffn1_w_down, mix_norm=m_mix_norm, w_in=m_w_in, ssm_lam_re=m_ssm_lam_re, ssm_lam_im=m_ssm_lam_im, ssm_log_dt=m_ssm_log_dt, ssm_b_re=m_ssm_b_re, ssm_b_im=m_ssm_b_im, ssm_c_re=m_ssm_c_re, ssm_c_im=m_ssm_c_im, ssm_d=m_ssm_d, ssm_w_glu=m_ssm_w_glu, attn_sink=m_attn_sink, w_branch_ssm=m_w_branch_ssm, w_branch_attn=m_w_branch_attn, w_out=m_w_out, ffn2_norm=m_ffn2_norm, ffn2_w_gate=m_ffn2_w_gate, ffn2_w_up=m_ffn2_w_up, ffn2_w_down=m_ffn2_w_down, final_norm=m_final_norm)
    mom_v = dict(meta_tokens=v_meta_tokens, ffn1_norm=v_ffn1_norm, ffn1_w_gate=v_ffn1_w_gate, ffn1_w_up=v_ffn1_w_up, ffn1_w_down=v_ffn1_w_down, mix_norm=v_mix_norm, w_in=v_w_in, ssm_lam_re=v_ssm_lam_re, ssm_lam_im=v_ssm_lam_im, ssm_log_dt=v_ssm_log_dt, ssm_b_re=v_ssm_b_re, ssm_b_im=v_ssm_b_im, ssm_c_re=v_ssm_c_re, ssm_c_im=v_ssm_c_im, ssm_d=v_ssm_d, ssm_w_glu=v_ssm_w_glu, attn_sink=v_attn_sink, w_branch_ssm=v_w_branch_ssm, w_branch_attn=v_w_branch_attn, w_out=v_w_out, ffn2_norm=v_ffn2_norm, ffn2_w_gate=v_ffn2_w_gate, ffn2_w_up=v_ffn2_w_up, ffn2_w_down=v_ffn2_w_down, final_norm=v_final_norm)
    names = list(weights)

    L0 = x.shape[1]
    LP = L0 + BLK
    tm = 384 if LP % 384 == 0 else BLK
    x_i, y_i, c_i = lax.axis_index("x"), lax.axis_index("y"), lax.axis_index("c")
    me = 4 * x_i + 2 * y_i + c_i

    def canon(l):
        return dict(
            f1_wgT=ffn1_w_gate[l].T, f1_wuT=ffn1_w_up[l].T, f1_wd=ffn1_w_down[l],
            winT=w_in[l].T, wba=w_branch_attn[l], wout=w_out[l],
            f2_wgT=ffn2_w_gate[l].T, f2_wuT=ffn2_w_up[l].T, f2_wd=ffn2_w_down[l],
            wglu=ssm_w_glu[l], wbsT=w_branch_ssm[l].T)

    shards = [{nm: a.astype(BF16) for nm, a in canon(l).items()} for l in range(DEPTH)]

    def pieces_of(names_):
        out, off = [], 0
        for nm in names_:
            r = shards[0][nm].shape[0]
            out.append((off, r))
            off += r
        return out

    p1_pieces, p2_pieces = pieces_of(W1024), pieces_of(W512)
    packed = [(jnp.concatenate([shards[l][nm] for nm in W1024], axis=0),
               jnp.concatenate([shards[l][nm] for nm in W512], axis=0)) for l in range(DEPTH)]
    g1, g2, gm = all_gather_pieces(
        "gather_weights_first", [(packed[0][0], p1_pieces), (packed[0][1], p2_pieces), (meta_tokens, [(0, N_META)])])
    full = [dict(zip(W1024 + W512, list(g1) + list(g2)))] + [None] * (DEPTH - 1)
    meta_full = gm[0].reshape(NDEV, N_META, D // NDEV).transpose(1, 0, 2).reshape(N_META, D)
    gather_handles = [None] + [gather_layer_start(f"gather_start_l{l}", list(packed[l]), [p1_pieces, p2_pieces])
                               for l in range(1, DEPTH)]
    started = sum(hd[3][0, 0] for hd in gather_handles[1:])

    def finish_gather(l, after):
        dests = gather_layer_wait(f"gather_wait_l{l}", gather_handles[l], 2, after)
        out = {}
        for nm, dest in zip(W1024 + W512, dests):
            sh = shards[l][nm]
            out[nm] = lax.dynamic_update_slice(dest, sh, (me * sh.shape[0], 0))
        return out

    def disc_all(lre, lim, ldt, bre, bim):
        return _ssm_disc(lre, lim, ldt, bre, bim)

    ssm_p, ssm_vjp = [], []
    for l in range(DEPTH):
        row, vrow = [], []
        for dr in range(2):
            args = (ssm_lam_re[l, dr], ssm_lam_im[l, dr], ssm_log_dt[l, dr], ssm_b_re[l, dr], ssm_b_im[l, dr])
            (a_re, a_im, bb_re, bb_im), vjp = jax.vjp(disc_all, *args)
            a8, tab = _scan_tables(args[0], args[1], args[2], False, dr == 1, tm // 8)
            a8_adj, tab_adj = _scan_tables(args[0], args[1], args[2], True, dr == 0, tm // 8)
            row.append(dict(
                bpr=_pack_b(bb_re).astype(BF16), bpi=_pack_b(bb_im).astype(BF16),
                cpr=_pack_c(ssm_c_re[l, dr]).astype(BF16), cpi=_pack_c(ssm_c_im[l, dr]).astype(BF16),
                a8=a8, tab=tab, a8_adj=a8_adj, tab_adj=tab_adj, a_re=a_re, a_im=a_im))
            vrow.append(vjp)
        ssm_p.append(row)
        ssm_vjp.append(vrow)

    blk0 = jnp.concatenate([jnp.zeros((PAD, D), F32), meta_full.astype(F32)], axis=0)
    h = build_h0(x[0], blk0)
    saved = []
    for l in range(DEPTH):
        if l > 0:
            full[l] = finish_gather(l, h)
        w = full[l]
        g1n, g2n = ffn1_norm[l][None, :], ffn2_norm[l][None, :]
        if l == 0:
            g1n = g1n + started
        mp = dict(g=mix_norm[l][None, :], winT=w["winT"], wglu=w["wglu"], wbsT=w["wbsT"], wba=w["wba"],
                  wout=w["wout"], d=ssm_d[l][None, :], sink=attn_sink[l], ssm=ssm_p[l])
        h, s1 = ffn_forward("ffn1", h, g1n, w["f1_wgT"], w["f1_wuT"], w["f1_wd"], tm)
        h, s2 = mixer_forward("mix", h, mp, tm)
        h, s3 = ffn_forward("ffn2", h, g2n, w["f2_wgT"], w["f2_wuT"], w["f2_wd"], tm)
        saved.append((s1, s2, s3, mp, g1n, g2n))

    dh, loss_acc, dgf = final_loss(h, loss_target[0], final_norm[None, :])
    loss = lax.psum(loss_acc[0, 0], MESH_AXES)

    big_grads = [None] * DEPTH
    small = {nm: [None] * DEPTH for nm in SMALL if nm != "final_norm"}
    scatter_handles = [None] * DEPTH
    own_rows = [None] * DEPTH
    sent = jnp.zeros((), F32)
    for l in reversed(range(DEPTH)):
        s1, s2, s3, mp, g1n, g2n = saved[l]
        w = full[l]
        dh, dg2, f2g, f2u, f2d = ffn_backward("ffn2", dh, s3, g2n + sent, w["f2_wgT"], w["f2_wuT"], w["f2_wd"], tm)
        dh, mg = mixer_backward("mix", dh, s2, mp, tm)
        dh, dg1, f1g, f1u, f1d = ffn_backward("ffn1", dh, s1, g1n, w["f1_wgT"], w["f1_wuT"], w["f1_wd"], tm)
        big_grads[l] = dict(f1_wgT=f1g, f1_wuT=f1u, f1_wd=f1d, winT=mg["winT"], wba=mg["wba"], wout=mg["wout"],
                            f2_wgT=f2g, f2_wuT=f2u, f2_wd=f2d, wglu=mg["wglu"], wbsT=mg["wbsT"])
        own_rows[l] = [jnp.concatenate(
            [lax.dynamic_slice_in_dim(big_grads[l][nm], me * r, r, axis=0) for nm, (_, r) in zip(names_, pcs)],
            axis=0) for names_, pcs in ((W1024, p1_pieces), (W512, p2_pieces))]
        scatter_handles[l] = scatter_layer_start(
            f"scatter_start_l{l}", [[big_grads[l][nm] for nm in W1024], [big_grads[l][nm] for nm in W512]])
        sent = scatter_handles[l][0][3][0, 0]
        small["ffn1_norm"][l] = dg1[0]
        small["mix_norm"][l] = mg["g"][0]
        small["ffn2_norm"][l] = dg2[0]
        small["ssm_d"][l] = mg["d"][0]
        small["attn_sink"][l] = mg["sink"][0, :N_HEADS]
        per_dir = {k: [] for k in ("ssm_lam_re", "ssm_lam_im", "ssm_log_dt", "ssm_b_re", "ssm_b_im",
                                   "ssm_c_re", "ssm_c_im")}
        for dr in range(2):
            gbr, gbi, gcr, gci, s_re, s_im = mg["ssm"][dr]
            a_re, a_im = ssm_p[l][dr]["a_re"], ssm_p[l][dr]["a_im"]
            s_re = s_re.reshape(SGRP, SP)
            s_im = s_im.reshape(SGRP, SP)
            den = a_re * a_re + a_im * a_im
            ga_re = (s_re * a_re - s_im * a_im) / den
            ga_im = (s_re * a_im + s_im * a_re) / den
            glr, gli, gld, gbre, gbim = ssm_vjp[l][dr]((ga_re, ga_im, _unpack_diag(gbr), _unpack_diag(gbi)))
            per_dir["ssm_lam_re"].append(glr)
            per_dir["ssm_lam_im"].append(gli)
            per_dir["ssm_log_dt"].append(gld)
            per_dir["ssm_b_re"].append(gbre)
            per_dir["ssm_b_im"].append(gbim)
            per_dir["ssm_c_re"].append(_unpack_diag(gcr).transpose(0, 2, 1))
            per_dir["ssm_c_im"].append(_unpack_diag(gci).transpose(0, 2, 1))
        for k, vlist in per_dir.items():
            small[k][l] = jnp.stack(vlist)

    grad_x = dh[BLK:][None]
    dmeta_part = dh[PAD:BLK]

    small_part = {k: jnp.stack(vv) for k, vv in small.items()}
    small_part["final_norm"] = dgf[0]
    pieces = [small_part[k].reshape(-1) for k in SMALL] + [dmeta_part.reshape(-1)]
    sizes = [p_.shape[0] for p_ in pieces]
    total = sum(sizes)
    rows_s = -(-total // (8 * D)) * 8
    flat = jnp.concatenate(pieces + [jnp.zeros((rows_s * D - total,), F32)]).reshape(rows_s, D)
    ((gathered,),) = all_gather_pieces("gather_small_grads", [(flat, [(0, rows_s)])])
    small_sum = sum_slots("sum_small_grads", gathered.reshape(NDEV, rows_s, D)).reshape(-1)
    grads = {}
    o = 0
    for k, sz in zip(SMALL, sizes[:-1]):
        grads[k] = small_sum[o:o + sz].reshape(weights[k].shape)
        o += sz
    dmeta_full = small_sum[o:o + N_META * D].reshape(N_META, D)
    grads["meta_tokens"] = lax.dynamic_slice_in_dim(dmeta_full, me * (D // NDEV), D // NDEV, axis=1)

    own = [dict() for _ in range(DEPTH)]
    for l in reversed(range(DEPTH)):
        handle, nin = scatter_handles[l]
        lands = scatter_layer_wait(f"scatter_wait_l{l}", handle, nin, 2, dh)
        for land, mine, names_, pcs, tag in ((lands[0], own_rows[l][0], W1024, p1_pieces, "1024"),
                                             (lands[1], own_rows[l][1], W512, p2_pieces, "512")):
            land = lax.dynamic_update_slice(land, mine[None], (me, 0, 0))
            tot = sum_slots("sum_weight_grads_" + tag, land)
            for nm, (off_, r) in zip(names_, pcs):
                own[l][nm] = tot[off_:off_ + r]

    def stack(fn):
        return jnp.stack([fn(own[l]) for l in range(DEPTH)])

    grads["ffn1_w_gate"] = stack(lambda d: d["f1_wgT"].T)
    grads["ffn1_w_up"] = stack(lambda d: d["f1_wuT"].T)
    grads["ffn1_w_down"] = stack(lambda d: d["f1_wd"])
    grads["w_in"] = stack(lambda d: d["winT"].T)
    grads["ssm_w_glu"] = stack(lambda d: d["wglu"])
    grads["w_branch_ssm"] = stack(lambda d: d["wbsT"].T)
    grads["w_branch_attn"] = stack(lambda d: d["wba"])
    grads["w_out"] = stack(lambda d: d["wout"])
    grads["ffn2_w_gate"] = stack(lambda d: d["f2_wgT"].T)
    grads["ffn2_w_up"] = stack(lambda d: d["f2_wuT"].T)
    grads["ffn2_w_down"] = stack(lambda d: d["f2_wd"])

    deltas, new_m, new_v = {}, {}, {}
    for nm in names:
        deltas[nm], new_m[nm], new_v[nm] = adamw("adamw_" + nm, weights[nm], grads[nm], mom_m[nm], mom_v[nm])

    return (loss, grad_x, *[grads[n] for n in names], *[deltas[n] for n in names],
            *[new_m[n] for n in names], *[new_v[n] for n in names])
```

```python
import functools
import math

import jax
import jax.numpy as jnp
from jax import lax
from jax.experimental import pallas as pl
from jax.experimental.pallas import tpu as pltpu

F32 = jnp.float32
BF16 = jnp.bfloat16

D = 1024
DFF = 2816
N_META = 16
N_HEADS = 16
N_KV = 4
HD = 64
QG = 4
WIN = 128
BLK = 128
PAD = BLK - N_META
SW = 512
SGRP = 32
SCH = 16
SP = 64
NST = SGRP * SP
EPS = 1e-6
NEG = -1e30
SCALE = HD ** -0.5
NDEV = 8
DEPTH = 4
MESH_AXES = ("x", "y", "c")
MESH = pl.DeviceIdType.MESH

ADAM_LR = 0.001
ADAM_B1 = 0.9
ADAM_B2 = 0.999
ADAM_EPS = 1e-08
ADAM_WD = 0.01
ADAM_STEP = 10

VMEM_LIMIT = 56 * 1024 * 1024


def _params(*sem):
    return pltpu.CompilerParams(dimension_semantics=sem, vmem_limit_bytes=VMEM_LIMIT)


def _nn(a, b):
    return lax.dot_general(a, b, (((1,), (0,)), ((), ())), preferred_element_type=F32)


def _nt(a, b):
    return lax.dot_general(a, b, (((1,), (1,)), ((), ())), preferred_element_type=F32)


def _tn(a, b):
    return lax.dot_general(a, b, (((0,), (0,)), ((), ())), preferred_element_type=F32)


def _sig(x):
    return 1.0 / (1.0 + jnp.exp(-x))


def _rms_fwd(h, g):
    r = lax.rsqrt(jnp.mean(h * h, axis=-1, keepdims=True) + EPS)
    hh = h * r
    return hh, r, hh * g


def _rms_bwd(hh, r, g, dn):
    dhh = dn * g
    dx = r * (dhh - hh * jnp.mean(dhh * hh, axis=-1, keepdims=True))
    return dx, jnp.sum(dn * hh, axis=0, keepdims=True)


def _row_ok(i, tm):
    rows = i * tm + lax.broadcasted_iota(jnp.int32, (tm, 1), 0)
    return rows >= PAD


def _const_spec(shape):
    nd = len(shape)
    return pl.BlockSpec(shape, lambda *_: (0,) * nd)


def rowcall(name, body, rows, consts, outs, accs=(), *, tm):
    nrows = rows[0].shape[0]
    nt = nrows // tm
    assert nt * tm == nrows, (name, nrows, tm)
    nr, nc, no, na = len(rows), len(consts), len(outs), len(accs)
    in_specs = [pl.BlockSpec((tm, r.shape[1]), lambda i: (i, 0)) for r in rows]
    in_specs += [_const_spec(c.shape) for c in consts]
    out_shape = [jax.ShapeDtypeStruct((nrows, w), dt) for (w, dt) in outs]
    out_specs = [pl.BlockSpec((tm, w), lambda i: (i, 0)) for (w, dt) in outs]
    out_shape += [jax.ShapeDtypeStruct(s, F32) for s in accs]
    out_specs += [_const_spec(s) for s in accs]

    def kern(*refs):
        i = pl.program_id(0)
        row_vals = [r[...] for r in refs[:nr]]
        res = body(i, *row_vals, *refs[nr:nr + nc])
        out_refs = refs[nr + nc:nr + nc + no]
        acc_refs = refs[nr + nc + no:]
        for r, v in zip(out_refs, res[:no]):
            r[...] = v.astype(r.dtype)
        if na:
            @pl.when(i == 0)
            def _():
                for r in acc_refs:
                    r[...] = jnp.zeros_like(r)
            for r, v in zip(acc_refs, res[no:]):
                r[...] += v

    res = pl.pallas_call(
        kern, name=name, grid=(nt,), in_specs=in_specs, out_specs=out_specs, out_shape=out_shape,
        compiler_params=_params("arbitrary"),
    )(*rows, *consts)
    return res


def tn_matmul(name, lhs, rhs, scale=1.0):
    M, K = lhs.shape
    N = rhs.shape[1]
    assert lhs.dtype == BF16 and rhs.dtype == BF16
    nm = 3
    tmw = M // nm
    assert tmw * nm == M and tmw % 16 == 0
    tk = 256 if K % 256 == 0 else K
    nk = K // tk

    def kern(a_ref, b_ref, o_ref, acc):
        m, k = pl.program_id(0), pl.program_id(1)
        rows = pl.ds(pl.multiple_of(k * tk, tk), tk)
        part = _tn(a_ref[...], b_ref[...])

        @pl.when(m == 0)
        def _():
            acc[rows, :] = part

        @pl.when((m > 0) & (m < nm - 1))
        def _():
            acc[rows, :] += part

        @pl.when(m == nm - 1)
        def _():
            o_ref[rows, :] = ((acc[rows, :] + part) * scale).astype(o_ref.dtype)

    return pl.pallas_call(
        kern, name=name, grid=(nm, nk),
        in_specs=[pl.BlockSpec((tmw, tk), lambda m, k: (m, k)), pl.BlockSpec((tmw, N), lambda m, k: (m, 0))],
        out_specs=_const_spec((K, N)),
        out_shape=jax.ShapeDtypeStruct((K, N), BF16),
        scratch_shapes=[pltpu.VMEM((K, N), F32)],
        compiler_params=_params("arbitrary", "arbitrary"),
    )(lhs, rhs)


def _mesh_pos():
    x, y, c = lax.axis_index("x"), lax.axis_index("y"), lax.axis_index("c")
    return x, y, c


def all_gather_pieces(name, groups):
    ng = len(groups)
    packed = [g[0] for g in groups]
    pieces = [g[1] for g in groups]
    out_shape, out_map = [], []
    for gi, (p, pcs) in enumerate(groups):
        idx = []
        for (off, r) in pcs:
            idx.append(len(out_shape))
            out_shape.append(jax.ShapeDtypeStruct((NDEV * r, p.shape[1]), p.dtype))
        out_map.append(idx)
    nout = len(out_shape)

    def body(*refs):
        p_refs = refs[:ng]
        o_refs = refs[ng:ng + nout]
        send_sems, recv_sems, local_sems = refs[ng + nout:]
        x, y, c = _mesh_pos()
        me = (x, y, c)
        sibling = (x, y, 1 - c)
        chips = [(1 - x, y), (x, 1 - y), (1 - x, 1 - y)]

        def blk(px, py, pc):
            return 4 * px + 2 * py + pc

        def copies(gi, k, origin, to, from_out):
            cps = []
            for (off, r), oi in zip(pieces[gi], out_map[gi]):
                dst = o_refs[oi].at[pl.ds(origin * r, r), :]
                src = dst if from_out else p_refs[gi].at[pl.ds(off, r), :]
                cps.append(pltpu.make_async_remote_copy(
                    src_ref=src, dst_ref=dst, send_sem=send_sems.at[gi, k], recv_sem=recv_sems.at[gi, k],
                    device_id=to, device_id_type=MESH))
            return cps

        def whole(gi, k):
            return pltpu.make_async_remote_copy(
                src_ref=p_refs[gi], dst_ref=p_refs[gi], send_sem=send_sems.at[gi, k],
                recv_sem=recv_sems.at[gi, k], device_id=me, device_id_type=MESH)

        mine = []
        for gi in range(ng):
            for (off, r), oi in zip(pieces[gi], out_map[gi]):
                mine.append(pltpu.make_async_copy(
                    p_refs[gi].at[pl.ds(off, r), :], o_refs[oi].at[pl.ds(blk(*me) * r, r), :],
                    local_sems.at[gi]))
        for cp in mine:
            cp.start()
        for gi in range(ng):
            for cp in copies(gi, 0, blk(*me), sibling, False):
                cp.start()
            for j, chip in enumerate(chips):
                for cp in copies(gi, 1 + j, blk(*me), (*chip, c), False):
                    cp.start()
        for j, chip in enumerate(chips):
            for gi in range(ng):
                whole(gi, 1 + j).wait_recv()
                for cp in copies(gi, 4 + j, blk(*chip, c), sibling, True):
                    cp.start()
        for gi in range(ng):
            whole(gi, 0).wait_recv()
            for j in range(3):
                whole(gi, 4 + j).wait_recv()
        for gi in range(ng):
            for k in range(7):
                whole(gi, k).wait_send()
            pltpu.make_async_copy(p_refs[gi], p_refs[gi], local_sems.at[gi]).wait()

    any_spec = pl.BlockSpec(memory_space=pl.ANY)
    outs = pl.pallas_call(
        body, name=name, out_shape=out_shape,
        in_specs=[any_spec] * ng, out_specs=[any_spec] * nout,
        scratch_shapes=[pltpu.SemaphoreType.DMA((ng, 7)), pltpu.SemaphoreType.DMA((ng, 7)),
                        pltpu.SemaphoreType.DMA((ng,))],
    )(*packed)
    return [[outs[oi] for oi in idx] for idx in out_map]


HBM_SPEC = pl.BlockSpec(memory_space=pltpu.HBM)
SEM_SPEC = pl.BlockSpec(memory_space=pltpu.SEMAPHORE)
DATAFLOW = pltpu.SideEffectType.DATAFLOW_SIDE_EFFECTING


def _peers(x, y, c):
    return [(x, y, 1 - c), (1 - x, y, c), (x, 1 - y, c), (1 - x, 1 - y, c),
            (1 - x, y, 1 - c), (x, 1 - y, 1 - c), (1 - x, 1 - y, 1 - c)]


def exchange_start(name, arrays, ng, plan):
    n = len(arrays)
    ns = ng * 7

    def body(*refs):
        in_refs = refs[:n]
        send_sems, recv_sems = refs[n:n + ns], refs[n + ns:n + 2 * ns]
        token = refs[-1]
        x, y, c = _mesh_pos()
        me_i = 4 * x + 2 * y + c
        for k, peer in enumerate(_peers(x, y, c)):
            p_i = 4 * peer[0] + 2 * peer[1] + peer[2]
            for src, dst, gi in plan(in_refs, me_i, p_i):
                pltpu.make_async_remote_copy(
                    src_ref=src, dst_ref=dst, send_sem=send_sems[gi * 7 + k], recv_sem=recv_sems[gi * 7 + k],
                    device_id=peer, device_id_type=MESH).start()
        token[...] = jnp.zeros_like(token)

    res = pl.pallas_call(
        body, name=name,
        out_shape=(*[pltpu.SemaphoreType.DMA(())] * (2 * ns),
                   *[pltpu.HBM(a.shape, a.dtype) for a in arrays], jax.ShapeDtypeStruct((8, 128), F32)),
        in_specs=[HBM_SPEC] * n,
        out_specs=(*[SEM_SPEC] * (2 * ns), *[HBM_SPEC] * n, pl.BlockSpec(memory_space=pltpu.VMEM)),
        input_output_aliases={i: 2 * ns + i for i in range(n)},
        compiler_params=pltpu.CompilerParams(has_side_effects=DATAFLOW),
    )(*[pltpu.with_memory_space_constraint(a, pltpu.HBM) for a in arrays])
    return list(res[:ns]), list(res[ns:2 * ns]), list(res[2 * ns:2 * ns + n]), res[-1]


def exchange_wait(name, send_sems, recv_sems, arrays, ng, sized, after):
    n = len(arrays)
    ns = ng * 7

    def body(*refs):
        in_refs = refs[:n]
        s_sems, r_sems = refs[n:n + ns], refs[n + ns:n + 2 * ns]
        x, y, c = _mesh_pos()
        for gi in range(ng):
            view = sized(in_refs, gi)
            for k in range(7):
                w = pltpu.make_async_remote_copy(
                    src_ref=view, dst_ref=view, send_sem=s_sems[gi * 7 + k], recv_sem=r_sems[gi * 7 + k],
                    device_id=(x, y, c), device_id_type=MESH)
                w.wait_send()
                w.wait_recv()

    res = pl.pallas_call(
        body, name=name, out_shape=tuple(pltpu.HBM(a.shape, a.dtype) for a in arrays),
        in_specs=[HBM_SPEC] * n + [SEM_SPEC] * (2 * ns) + [pl.BlockSpec(memory_space=pl.ANY)],
        out_specs=tuple([HBM_SPEC] * n), input_output_aliases={i: i for i in range(n)},
        compiler_params=pltpu.CompilerParams(has_side_effects=DATAFLOW),
    )(*arrays, *send_sems, *recv_sems, after)
    return list(res)


def gather_layer_start(name, packed, pieces):
    ng = len(packed)
    dests = [lax.empty((NDEV * r, p.shape[1]), p.dtype) for p, pcs in zip(packed, pieces) for (_, r) in pcs]

    def plan(refs, me_i, p_i):
        out, di = [], ng
        for gi in range(ng):
            for (off, r) in pieces[gi]:
                out.append((refs[gi].at[pl.ds(off, r), :], refs[di].at[pl.ds(me_i * r, r), :], gi))
                di += 1
        return out

    return exchange_start(name, list(packed) + dests, ng, plan)


def gather_layer_wait(name, handle, ng, after):
    send_sems, recv_sems, arrays, _ = handle
    out = exchange_wait(name, send_sems, recv_sems, arrays, ng, lambda refs, gi: refs[gi], after)
    return out[ng:]


def scatter_layer_start(name, groups):
    ng = len(groups)
    flat = [a for arrs in groups for a in arrs]
    offs, lands = [], []
    for arrs in groups:
        o, off = [], 0
        for a in arrs:
            r = a.shape[0] // NDEV
            o.append((off, r))
            off += r
        offs.append(o)
        lands.append(jnp.zeros((NDEV, off, arrs[0].shape[1]), arrs[0].dtype))
    nin = len(flat)

    def plan(refs, me_i, p_i):
        out, ai = [], 0
        for gi in range(ng):
            for (off, r) in offs[gi]:
                out.append((refs[ai].at[pl.ds(p_i * r, r), :], refs[nin + gi].at[me_i, pl.ds(off, r), :], gi))
                ai += 1
        return out

    return exchange_start(name, flat + lands, ng, plan), nin


def scatter_layer_wait(name, handle, nin, ng, after):
    send_sems, recv_sems, arrays, _ = handle
    out = exchange_wait(name, send_sems, recv_sems, arrays, ng, lambda refs, gi: refs[nin + gi].at[0], after)
    return out[nin:]


def _pick_tile(n, cap):
    best = None
    for t in range(8, min(n, cap) + 1, 8):
        if n % t == 0:
            best = t
    return best if best is not None else n


def sum_slots(name, land):
    _, R, W = land.shape
    tr = _pick_tile(R, 512)

    def kern(l_ref, o_ref):
        acc = l_ref[0].astype(F32)
        for s in range(1, NDEV):
            acc = acc + l_ref[s].astype(F32)
        o_ref[...] = acc

    return pl.pallas_call(
        kern, name=name, grid=(R // tr,),
        in_specs=[pl.BlockSpec((NDEV, tr, W), lambda i: (0, i, 0))],
        out_specs=pl.BlockSpec((tr, W), lambda i: (i, 0)),
        out_shape=jax.ShapeDtypeStruct((R, W), F32),
        compiler_params=_params("arbitrary"),
    )(land)


def adamw(name, w, g, m, v):
    shp = w.shape
    C = shp[-1]
    R = max(1, math.prod(shp[:-1]))
    tr = _pick_tile(R, 1024)
    w2, g2, m2, v2 = (a.reshape(R, C) for a in (w, g, m, v))

    def kern(w_ref, g_ref, m_ref, v_ref, d_ref, mo_ref, vo_ref):
        gg = g_ref[...]
        mn = ADAM_B1 * m_ref[...] + (1.0 - ADAM_B1) * gg
        vn = ADAM_B2 * v_ref[...] + (1.0 - ADAM_B2) * jnp.square(gg)
        m_hat = mn / (1.0 - ADAM_B1 ** ADAM_STEP)
        v_hat = vn / (1.0 - ADAM_B2 ** ADAM_STEP)
        d_ref[...] = -ADAM_LR * (m_hat / (jnp.sqrt(v_hat) + ADAM_EPS) + ADAM_WD * w_ref[...])
        mo_ref[...] = mn
        vo_ref[...] = vn

    spec = pl.BlockSpec((tr, C), lambda i: (i, 0))
    d, mo, vo = pl.pallas_call(
        kern, name=name, grid=(R // tr,), in_specs=[spec] * 4, out_specs=[spec] * 3,
        out_shape=[jax.ShapeDtypeStruct((R, C), F32)] * 3, compiler_params=_params("arbitrary"),
    )(w2, g2, m2, v2)
    return d.reshape(shp), mo.reshape(shp), vo.reshape(shp)


def build_h0(x2, blk0):
    L0 = x2.shape[0]
    nb = L0 // BLK + 1

    def kern(x_ref, b_ref, o_ref):
        i = pl.program_id(0)

        @pl.when(i == 0)
        def _():
            o_ref[...] = b_ref[...]

        @pl.when(i > 0)
        def _():
            o_ref[...] = x_ref[...]

    return pl.pallas_call(
        kern, name="build_h0", grid=(nb,),
        in_specs=[pl.BlockSpec((BLK, D), lambda i: (jnp.maximum(i - 1, 0), 0)), _const_spec((BLK, D))],
        out_specs=pl.BlockSpec((BLK, D), lambda i: (i, 0)),
        out_shape=jax.ShapeDtypeStruct((L0 + BLK, D), F32), compiler_params=_params("arbitrary"),
    )(x2, blk0)


def final_loss(h, tgt, gf):
    LP = h.shape[0]
    nb = LP // BLK

    def kern(h_ref, t_ref, g_ref, dh_ref, loss_ref, dg_ref):
        i = pl.program_id(0)

        @pl.when(i == 0)
        def _():
            loss_ref[...] = jnp.zeros_like(loss_ref)
            dg_ref[...] = jnp.zeros_like(dg_ref)

        g = g_ref[...]
        hh, r, yv = _rms_fwd(h_ref[...], g)
        valid = (i > 0).astype(F32)
        err = (yv - t_ref[...]) * valid
        loss_ref[...] += 0.5 * jnp.sum(jnp.sum(err * err, axis=1, keepdims=True), axis=0, keepdims=True) / D
        dy = err / D
        dx, dg = _rms_bwd(hh, r, g, dy)
        dh_ref[...] = dx
        dg_ref[...] += dg

    return pl.pallas_call(
        kern, name="final_loss", grid=(nb,),
        in_specs=[pl.BlockSpec((BLK, D), lambda i: (i, 0)),
                  pl.BlockSpec((BLK, D), lambda i: (jnp.maximum(i - 1, 0), 0)), _const_spec((1, D))],
        out_specs=[pl.BlockSpec((BLK, D), lambda i: (i, 0)), _const_spec((8, 128)), _const_spec((1, D))],
        out_shape=[jax.ShapeDtypeStruct((LP, D), F32), jax.ShapeDtypeStruct((8, 128), F32),
                   jax.ShapeDtypeStruct((1, D), F32)],
        compiler_params=_params("arbitrary"),
    )(h, tgt, gf)


def ffn_forward(tag, h, g, wgT, wuT, wd, tm):
    def f1(i, hv, g_ref, wg_ref, wu_ref):
        _, _, n = _rms_fwd(hv, g_ref[...])
        nb = n.astype(BF16)
        G = _nt(nb, wg_ref[...])
        U = _nt(nb, wu_ref[...])
        sg = _sig(G)
        S = G * sg
        DS = sg * (1.0 + G * (1.0 - sg))
        return nb, U, S * U, S, DS

    n, U, A, S, DS = rowcall(tag + "_up", f1, [h], [g, wgT, wuT],
                             [(D, BF16), (DFF, BF16), (DFF, BF16), (DFF, BF16), (DFF, BF16)], tm=tm)

    def f2(i, av, hv, wd_ref):
        return (hv + 0.5 * _nn(av, wd_ref[...]),)

    (h2,) = rowcall(tag + "_down", f2, [A, h], [wd], [(D, F32)], tm=tm)
    return h2, (h, n, U, A, S, DS)


def ffn_backward(tag, dh, saved, g, wgT, wuT, wd, tm):
    h, n, U, A, S, DS = saved

    def b1(i, dhv, Uv, Sv, DSv, wd_ref):
        dyb = (0.5 * dhv).astype(BF16)
        dA = _nt(dyb, wd_ref[...])
        dG = dA * Uv.astype(F32) * DSv.astype(F32)
        dU = dA * Sv.astype(F32)
        return dG, dU, dyb

    dG, dU, dyb = rowcall(tag + "_bwd_act", b1, [dh, U, S, DS], [wd], [(DFF, BF16), (DFF, BF16), (D, BF16)], tm=tm)

    def b2(i, dGv, dUv, hv, dhv, g_ref, wg_ref, wu_ref):
        dn = _nn(dGv, wg_ref[...]) + _nn(dUv, wu_ref[...])
        gv = g_ref[...]
        hh, r, _ = _rms_fwd(hv, gv)
        dx, dg = _rms_bwd(hh, r, gv, dn)
        dx = jnp.where(_row_ok(i, tm), dx, 0.0)
        return dhv + dx, dg

    dh2, dg = rowcall(tag + "_bwd_in", b2, [dG, dU, h, dh], [g, wgT, wuT], [(D, F32)], [(1, D)], tm=tm)
    dwd = tn_matmul(tag + "_dwd", A, dyb)
    dwgT = tn_matmul(tag + "_dwg", dG, n)
    dwuT = tn_matmul(tag + "_dwu", dU, n)
    return dh2, dg, dwgT, dwuT, dwd


def _alibi_slope(head):
    return float(2.0 ** (-8.0 * (head + 1) / N_HEADS))


def _att_bias(n, nb):
    qi = lax.broadcasted_iota(jnp.int32, (BLK, 4 * BLK), 0)
    cj = lax.broadcasted_iota(jnp.int32, (BLK, 4 * BLK), 1)
    jb = cj - BLK
    dist = jnp.abs(qi + BLK - jb)
    kpos = (n - 1) * BLK + jb
    band_ok = (dist <= WIN) & (kpos >= BLK) & (kpos < nb * BLK)
    is_meta = cj < BLK
    ok = (is_meta & (cj >= PAD)) | (jnp.logical_not(is_meta) & band_ok)
    distf = jnp.where(is_meta, 0, dist).astype(F32)
    maskadd = jnp.where(ok, 0.0, NEG).astype(F32)
    distf4 = jnp.concatenate([distf] * QG, axis=0)
    mask4 = jnp.concatenate([maskadd] * QG, axis=0)
    return distf4, mask4


def _group_col(vals):
    rg = lax.broadcasted_iota(jnp.int32, (QG * BLK, 1), 0) // BLK
    col = jnp.full((QG * BLK, 1), vals[QG - 1], F32)
    for gq in range(QG - 2, -1, -1):
        col = jnp.where(rg == gq, vals[gq], col)
    return col


def _stack_heads(ref_or_val, kh):
    return jnp.concatenate(
        [ref_or_val[:, (kh * QG + gq) * HD:(kh * QG + gq + 1) * HD] for gq in range(QG)], axis=0)


def _stack_keys(km, kp, kc, kn, kh):
    sl = slice(kh * HD, (kh + 1) * HD)
    return jnp.concatenate([km[:, sl], kp[:, sl], kc[:, sl], kn[:, sl]], axis=0)


LOG2E = 1.4426950408889634
LN2 = 0.6931471805599453
QSCALE = SCALE * LOG2E


def _att_update_bias(bias_ref, n, nb):
    @pl.when((n <= 2) | (n == nb - 1))
    def _():
        distf4, mask4 = _att_bias(n, nb)
        for kh in range(N_KV):
            slope_col = _group_col([_alibi_slope(kh * QG + gq) * LOG2E for gq in range(QG)])
            bias_ref[kh] = mask4 - slope_col * distf4


def _att_exp(qs, kb, bias_ref, kh, sink_ref):
    sink_col = _group_col([sink_ref[kh * QG + gq] for gq in range(QG)]) * LOG2E
    s = _nt(qs, kb) + bias_ref[kh]
    m = jnp.maximum(jnp.max(s, axis=1, keepdims=True), sink_col)
    e = jnp.exp2(s - m)
    es = jnp.exp2(sink_col - m)
    inv = 1.0 / (jnp.sum(e, axis=1, keepdims=True) + es)
    return e, es, inv


def attention_forward(tag, q, k, v, sink):
    LP = q.shape[0]
    nb = LP // BLK

    def kern(sink_ref, q_ref, km_ref, kp_ref, kc_ref, kn_ref, vm_ref, vp_ref, vc_ref, vn_ref, o_ref, bias_ref):
        n = pl.program_id(0)
        _att_update_bias(bias_ref, n, nb)
        qv = q_ref[...]
        km, kp, kc, kn = km_ref[...], kp_ref[...], kc_ref[...], kn_ref[...]
        vm, vp, vc, vn = vm_ref[...], vp_ref[...], vc_ref[...], vn_ref[...]
        for kh in range(N_KV):
            qs = _stack_heads(qv, kh)
            kb = _stack_keys(km, kp, kc, kn, kh)
            vb = _stack_keys(vm, vp, vc, vn, kh)
            e, _, inv = _att_exp(qs, kb, bias_ref, kh, sink_ref)
            o = _nn(e.astype(BF16), vb) * inv
            for gq in range(QG):
                hcol = (kh * QG + gq) * HD
                o_ref[:, hcol:hcol + HD] = o[gq * BLK:(gq + 1) * BLK].astype(o_ref.dtype)

    def kvspec(dn):
        return pl.BlockSpec((BLK, N_KV * HD), lambda n: (jnp.clip(n + dn, 0, nb - 1), 0))

    meta_spec = pl.BlockSpec((BLK, N_KV * HD), lambda n: (0, 0))
    return pl.pallas_call(
        kern, name=tag + "_att_fwd", grid=(nb,),
        in_specs=[pl.BlockSpec(memory_space=pltpu.SMEM), pl.BlockSpec((BLK, D), lambda n: (n, 0)),
                  meta_spec, kvspec(-1), kvspec(0), kvspec(1), meta_spec, kvspec(-1), kvspec(0), kvspec(1)],
        out_specs=pl.BlockSpec((BLK, D), lambda n: (n, 0)),
        out_shape=jax.ShapeDtypeStruct((LP, D), BF16),
        scratch_shapes=[pltpu.VMEM((N_KV, QG * BLK, 4 * BLK), F32)], compiler_params=_params("arbitrary"),
    )(sink, q, k, k, k, k, v, v, v, v)


def attention_backward(tag, q, k, v, do, sink):
    LP = q.shape[0]
    nb = LP // BLK
    KW = N_KV * HD

    def kern(sink_ref, q_ref, do_ref, km_ref, kp_ref, kc_ref, kn_ref, vm_ref, vp_ref, vc_ref, vn_ref,
             dq_ref, dkp_ref, dvp_ref, dkm_ref, dvm_ref, dsink_ref, bias_ref):
        n = pl.program_id(0)

        @pl.when(n == 0)
        def _():
            dkm_ref[...] = jnp.zeros_like(dkm_ref)
            dvm_ref[...] = jnp.zeros_like(dvm_ref)
            dsink_ref[...] = jnp.zeros_like(dsink_ref)

        _att_update_bias(bias_ref, n, nb)
        qv, dov = q_ref[...], do_ref[...]
        km, kp, kc, kn = km_ref[...], kp_ref[...], kc_ref[...], kn_ref[...]
        vm, vp, vc, vn = vm_ref[...], vp_ref[...], vc_ref[...], vn_ref[...]
        lane = lax.broadcasted_iota(jnp.int32, (8, 128), 1)
        dsink = jnp.zeros((8, 128), F32)
        for kh in range(N_KV):
            qs = _stack_heads(qv, kh)
            dos = _stack_heads(dov, kh)
            kb = _stack_keys(km, kp, kc, kn, kh)
            vb = _stack_keys(vm, vp, vc, vn, kh)
            e, es, inv = _att_exp(qs, kb, bias_ref, kh, sink_ref)
            dos_n = (dos.astype(F32) * inv).astype(BF16)
            dp = _nt(dos_n, vb)
            delta = jnp.sum(e * dp, axis=1, keepdims=True)
            ds = (e * (dp - inv * delta)).astype(BF16)
            dqs = _nn(ds, kb) * SCALE
            dkb = _tn(ds, qs) * LN2
            dvb = _tn(e.astype(BF16), dos_n)
            dsk = -(es * inv * delta)
            sl = slice(kh * HD, (kh + 1) * HD)
            for gq in range(QG):
                hcol = (kh * QG + gq) * HD
                dq_ref[:, hcol:hcol + HD] = dqs[gq * BLK:(gq + 1) * BLK].astype(dq_ref.dtype)
                tot = jnp.sum(dsk[gq * BLK:(gq + 1) * BLK], axis=0, keepdims=True)
                dsink = dsink + jnp.where(lane == kh * QG + gq, tot, 0.0)
            dkm_ref[:, sl] += dkb[0:BLK]
            dvm_ref[:, sl] += dvb[0:BLK]
            for slot in range(3):
                dkp_ref[0, slot, :, sl] = dkb[(slot + 1) * BLK:(slot + 2) * BLK]
                dvp_ref[0, slot, :, sl] = dvb[(slot + 1) * BLK:(slot + 2) * BLK]
        dsink_ref[...] += dsink

    def kvspec(dn):
        return pl.BlockSpec((BLK, KW), lambda n: (jnp.clip(n + dn, 0, nb - 1), 0))

    meta_spec = pl.BlockSpec((BLK, KW), lambda n: (0, 0))
    rowspec = pl.BlockSpec((BLK, D), lambda n: (n, 0))
    part_spec = pl.BlockSpec((1, 3, BLK, KW), lambda n: (n, 0, 0, 0))
    dq, dkp, dvp, dkm, dvm, dsink = pl.pallas_call(
        kern, name=tag + "_att_bwd", grid=(nb,),
        in_specs=[pl.BlockSpec(memory_space=pltpu.SMEM), rowspec, rowspec,
                  meta_spec, kvspec(-1), kvspec(0), kvspec(1), meta_spec, kvspec(-1), kvspec(0), kvspec(1)],
        out_specs=[rowspec, part_spec, part_spec, _const_spec((BLK, KW)), _const_spec((BLK, KW)),
                   _const_spec((8, 128))],
        out_shape=[jax.ShapeDtypeStruct((LP, D), BF16), jax.ShapeDtypeStruct((nb, 3, BLK, KW), F32),
                   jax.ShapeDtypeStruct((nb, 3, BLK, KW), F32), jax.ShapeDtypeStruct((BLK, KW), F32),
                   jax.ShapeDtypeStruct((BLK, KW), F32), jax.ShapeDtypeStruct((8, 128), F32)],
        scratch_shapes=[pltpu.VMEM((N_KV, QG * BLK, 4 * BLK), F32)], compiler_params=_params("arbitrary"),
    )(sink, q, do, k, k, k, k, v, v, v, v)

    def comb(a_ref, b_ref, c_ref, m_ref, a2_ref, b2_ref, c2_ref, m2_ref, dk_ref, dv_ref):
        mblk = pl.program_id(0)
        has_prev = (mblk > 0).astype(F32)
        has_next = (mblk < nb - 1).astype(F32)
        is0 = (mblk == 0).astype(F32)
        dk_ref[...] = (a_ref[0, 0] * has_prev + b_ref[0, 0] + c_ref[0, 0] * has_next
                       + m_ref[...] * is0).astype(dk_ref.dtype)
        dv_ref[...] = (a2_ref[0, 0] * has_prev + b2_ref[0, 0] + c2_ref[0, 0] * has_next
                       + m2_ref[...] * is0).astype(dv_ref.dtype)

    def pspec(dn, slot):
        return pl.BlockSpec((1, 1, BLK, KW), lambda m: (jnp.clip(m + dn, 0, nb - 1), slot, 0, 0))

    kvout = pl.BlockSpec((BLK, KW), lambda m: (m, 0))
    dk, dv = pl.pallas_call(
        comb, name=tag + "_att_dkv", grid=(nb,),
        in_specs=[pspec(-1, 2), pspec(0, 1), pspec(1, 0), _const_spec((BLK, KW)),
                  pspec(-1, 2), pspec(0, 1), pspec(1, 0), _const_spec((BLK, KW))],
        out_specs=[kvout, kvout],
        out_shape=[jax.ShapeDtypeStruct((LP, KW), BF16)] * 2, compiler_params=_params("arbitrary"),
    )(dkp, dkp, dkp, dkm, dvp, dvp, dvp, dvm)
    return dq, dk, dv, dsink


SCAN_LANES = 1024


def _scan_tile(xr, xi, cr, ci, a8, tab, seg, reverse):
    sub = lax.broadcasted_iota(jnp.int32, (8, SCAN_LANES), 0)
    for c0 in range(0, NST, SCAN_LANES):
        cs = pl.ds(c0, SCAN_LANES)
        ar = a8[0, :, cs]
        ai = a8[1, :, cs]

        def rows(j):
            jj = (seg - 1 - j) if reverse else j
            return pl.ds(pl.multiple_of(jj * 8, 8), 8)

        def step1(j, carry):
            vr, vi = carry
            rs = rows(j)
            nr = ar * vr - ai * vi + xr[rs, cs]
            ni = ar * vi + ai * vr + xi[rs, cs]
            xr[rs, cs] = nr
            xi[rs, cs] = ni
            return nr, ni

        zero = jnp.zeros((8, SCAN_LANES), F32)
        vr, vi = lax.fori_loop(0, seg, step1, (zero, zero), unroll=2)
        for t, s in enumerate((1, 2, 4)):
            sh = (8 - s) if reverse else s
            sr = pltpu.roll(vr, sh, 0)
            si = pltpu.roll(vi, sh, 0)
            tr = tab[2 * t, :, cs]
            ti = tab[2 * t + 1, :, cs]
            vr, vi = vr + tr * sr - ti * si, vi + tr * si + ti * sr
        pr = tab[6, :, cs]
        pi = tab[7, :, cs]
        c_r = cr[:, cs]
        c_i = ci[:, cs]
        vr, vi = vr + pr * c_r - pi * c_i, vi + pr * c_i + pi * c_r
        edge = 7 if reverse else 0
        last = 0 if reverse else 7
        sh = 7 if reverse else 1
        in_r = jnp.where(sub == edge, c_r, pltpu.roll(vr, sh, 0))
        in_i = jnp.where(sub == edge, c_i, pltpu.roll(vi, sh, 0))
        cr[:, cs] = jnp.broadcast_to(vr[last:last + 1, :], (8, SCAN_LANES))
        ci[:, cs] = jnp.broadcast_to(vi[last:last + 1, :], (8, SCAN_LANES))

        def step2(j, carry):
            dr, di = carry
            rs = rows(j)
            ndr = ar * dr - ai * di
            ndi = ar * di + ai * dr
            xr[rs, cs] += ndr
            xi[rs, cs] += ndi
            return ndr, ndi

        lax.fori_loop(0, seg, step2, (in_r, in_i), unroll=2)


ST_T = 4 * SP * 2
CH_T = 128


def _load_segmented(ref, scr, seg):
    out = []
    for ct in range(4):
        scr[ct] = ref[:, ct * CH_T:(ct + 1) * CH_T]
        out.append(jnp.concatenate([scr[ct, pl.ds(j, 8, stride=seg), :] for j in range(seg)], axis=0))
    return out


def _store_segmented(ref, scr, vals, seg):
    for ct in range(4):
        for j in range(seg):
            scr[ct, pl.ds(j, 8, stride=seg), :] = vals[ct][8 * j:8 * j + 8]
        ref[:, ct * CH_T:(ct + 1) * CH_T] = scr[ct]


def ssm_dir_forward(tag, u, bpr, bpi, cpr, cpi, a8, tab, reverse, tm):
    LP = u.shape[0]
    nt = LP // tm
    seg = tm // 8

    def rix(i):
        return (nt - 1 - i) if reverse else i

    def kern(u_ref, bpr_ref, bpi_ref, cpr_ref, cpi_ref, a8_ref, tab_ref, xre_ref, xim_ref, y_ref,
             xr, xi, ys, cr, ci):
        i = pl.program_id(0)

        @pl.when(i == 0)
        def _():
            cr[...] = jnp.zeros_like(cr)
            ci[...] = jnp.zeros_like(ci)

        ub = _load_segmented(u_ref, ys, seg)
        for ct in range(4):
            uc = ub[ct].astype(BF16)
            xr[:, ct * ST_T:(ct + 1) * ST_T] = _nn(uc, bpr_ref[ct * CH_T:(ct + 1) * CH_T, :])
            xi[:, ct * ST_T:(ct + 1) * ST_T] = _nn(uc, bpi_ref[ct * CH_T:(ct + 1) * CH_T, :])
        _scan_tile(xr, xi, cr, ci, a8_ref, tab_ref, seg, reverse)
        xrb = xr[...].astype(BF16)
        xib = xi[...].astype(BF16)
        xre_ref[...] = xrb
        xim_ref[...] = xib
        yv = []
        for ct in range(4):
            ss = slice(ct * ST_T, (ct + 1) * ST_T)
            yv.append(_nn(xrb[:, ss], cpr_ref[ss, :]) - _nn(xib[:, ss], cpi_ref[ss, :]))
        _store_segmented(y_ref, ys, yv, seg)

    row = lambda w: pl.BlockSpec((tm, w), lambda i: (rix(i), 0))
    return pl.pallas_call(
        kern, name=tag, grid=(nt,),
        in_specs=[row(SW), _const_spec(bpr.shape), _const_spec(bpi.shape), _const_spec(cpr.shape),
                  _const_spec(cpi.shape), _const_spec(a8.shape), _const_spec(tab.shape)],
        out_specs=[row(NST), row(NST), row(SW)],
        out_shape=[jax.ShapeDtypeStruct((LP, NST), BF16), jax.ShapeDtypeStruct((LP, NST), BF16),
                   jax.ShapeDtypeStruct((LP, SW), F32)],
        scratch_shapes=[pltpu.VMEM((tm, NST), F32), pltpu.VMEM((tm, NST), F32), pltpu.VMEM((4, tm, CH_T), F32),
                        pltpu.VMEM((8, NST), F32), pltpu.VMEM((8, NST), F32)],
        compiler_params=_params("arbitrary"),
    )(u, bpr, bpi, cpr, cpi, a8, tab)


def ssm_dir_backward(tag, dy, xre, xim, u, bpr, bpi, cpr, cpi, a8_adj, tab_adj, reverse, tm):
    LP = u.shape[0]
    nt = LP // tm
    seg = tm // 8

    def rix(i):
        return (nt - 1 - i) if reverse else i

    def kern(dy_ref, xre_ref, xim_ref, u_ref, bpr_ref, bpi_ref, cpr_ref, cpi_ref, a8_ref, tab_ref,
             du_ref, gbr_ref, gbi_ref, gcr_ref, gci_ref, sr_ref, si_ref, lr, li, gr, gi, dus, cr, ci):
        i = pl.program_id(0)

        @pl.when(i == 0)
        def _():
            cr[...] = jnp.zeros_like(cr)
            ci[...] = jnp.zeros_like(ci)
            for r in (gbr_ref, gbi_ref, gcr_ref, gci_ref, sr_ref, si_ref):
                r[...] = jnp.zeros_like(r)

        dyb = [v.astype(BF16) for v in _load_segmented(dy_ref, dus, seg)]
        ub = [v.astype(BF16) for v in _load_segmented(u_ref, dus, seg)]
        for ct in range(4):
            ss = slice(ct * ST_T, (ct + 1) * ST_T)
            dc = dyb[ct]
            g_re = _nt(dc, cpr_ref[ss, :])
            g_im = -_nt(dc, cpi_ref[ss, :])
            lr[:, ss] = g_re
            li[:, ss] = g_im
            gr[:, ss] = g_re
            gi[:, ss] = g_im
        _scan_tile(lr, li, cr, ci, a8_ref, tab_ref, seg, reverse)
        lam_r = lr[...]
        lam_i = li[...]
        wr = lam_r - gr[...]
        wi = lam_i - gi[...]
        xr = xre_ref[...].astype(F32)
        xi = xim_ref[...].astype(F32)
        sr_ref[...] += jnp.sum(wr * xr + wi * xi, axis=0, keepdims=True)
        si_ref[...] += jnp.sum(wi * xr - wr * xi, axis=0, keepdims=True)
        lrb = lam_r.astype(BF16)
        lib = lam_i.astype(BF16)
        xrb = xre_ref[...]
        xib = xim_ref[...]
        duv = []
        for ct in range(4):
            ss = slice(ct * ST_T, (ct + 1) * ST_T)
            cs = slice(ct * CH_T, (ct + 1) * CH_T)
            duv.append(_nt(lrb[:, ss], bpr_ref[cs, :]) + _nt(lib[:, ss], bpi_ref[cs, :]))
            gbr_ref[ss, :] += _tn(lrb[:, ss], ub[ct])
            gbi_ref[ss, :] += _tn(lib[:, ss], ub[ct])
            gcr_ref[ss, :] += _tn(xrb[:, ss], dyb[ct])
            gci_ref[ss, :] -= _tn(xib[:, ss], dyb[ct])
        _store_segmented(du_ref, dus, duv, seg)

    row = lambda w: pl.BlockSpec((tm, w), lambda i: (rix(i), 0))
    acc = _const_spec((NST, CH_T))
    vec = _const_spec((1, NST))
    return pl.pallas_call(
        kern, name=tag, grid=(nt,),
        in_specs=[row(SW), row(NST), row(NST), row(SW), _const_spec(bpr.shape), _const_spec(bpi.shape),
                  _const_spec(cpr.shape), _const_spec(cpi.shape), _const_spec(a8_adj.shape),
                  _const_spec(tab_adj.shape)],
        out_specs=[row(SW), acc, acc, acc, acc, vec, vec],
        out_shape=[jax.ShapeDtypeStruct((LP, SW), F32)] + [jax.ShapeDtypeStruct((NST, CH_T), F32)] * 4
        + [jax.ShapeDtypeStruct((1, NST), F32)] * 2,
        scratch_shapes=[pltpu.VMEM((tm, NST), F32)] * 4 + [pltpu.VMEM((4, tm, CH_T), F32)]
        + [pltpu.VMEM((8, NST), F32)] * 2,
        compiler_params=_params("arbitrary"),
    )(dy, xre, xim, u, bpr, bpi, cpr, cpi, a8_adj, tab_adj)


def _ssm_disc(lam_re, lam_im, log_dt, b_re, b_im):
    dt = jnp.exp(log_dt)[:, None]
    mag = jnp.exp(lam_re * dt)
    a_re = mag * jnp.cos(lam_im * dt)
    a_im = mag * jnp.sin(lam_im * dt)
    den = lam_re * lam_re + lam_im * lam_im
    f_re = ((a_re - 1.0) * lam_re + a_im * lam_im) / den
    f_im = (a_im * lam_re - (a_re - 1.0) * lam_im) / den
    bb_re = f_re[:, :, None] * b_re - f_im[:, :, None] * b_im
    bb_im = f_re[:, :, None] * b_im + f_im[:, :, None] * b_re
    return a_re, a_im, bb_re, bb_im


def _scan_tables(lam_re, lam_im, log_dt, conj, reverse, seg):
    dt = jnp.exp(log_dt)[:, None]
    lr = (lam_re * dt).reshape(1, NST)
    li = (lam_im * dt).reshape(1, NST) * (-1.0 if conj else 1.0)
    t = jnp.arange(8, dtype=F32)[:, None]

    def power(kk):
        mag = jnp.exp(kk * lr)
        return mag * jnp.cos(kk * li), mag * jnp.sin(kk * li)

    ones = jnp.ones((8, 1), F32)
    a8 = jnp.stack(power(ones)).astype(F32)
    tabs = []
    for s in (1, 2, 4):
        mask = (t <= 7 - s) if reverse else (t >= s)
        pr, pi = power(float(s * seg) * ones)
        tabs += [jnp.where(mask, pr, 0.0), jnp.where(mask, pi, 0.0)]
    kk = ((8.0 - t) if reverse else (t + 1.0)) * float(seg)
    pr, pi = power(kk)
    tabs += [pr, pi]
    return a8, jnp.stack(tabs).astype(F32)


def _pack_b(bb):
    t = bb.transpose(0, 2, 1).reshape(4, 8, SCH, SP)
    eye = jnp.eye(8, dtype=bb.dtype)
    return jnp.einsum('tgcp,gh->tgchp', t, eye).reshape(SW, ST_T)


def _pack_c(cc):
    t = cc.transpose(0, 2, 1).reshape(4, 8, SP, SCH)
    eye = jnp.eye(8, dtype=cc.dtype)
    return jnp.einsum('tgpc,gh->tgphc', t, eye).reshape(NST, CH_T)


def _unpack_diag(acc):
    t = acc.reshape(4, 8, SP, 8, SCH)
    eye = jnp.eye(8, dtype=acc.dtype)
    return jnp.einsum('tgphc,gh->tgpc', t, eye).reshape(SGRP, SP, SCH)


def _gelu(y):
    k0 = math.sqrt(2.0 / math.pi)
    inner = k0 * (y + 0.044715 * y * y * y)
    th = jnp.tanh(inner)
    z = 0.5 * y * (1.0 + th)
    dz = 0.5 * (1.0 + th) + 0.5 * y * (1.0 - th * th) * k0 * (1.0 + 3.0 * 0.044715 * y * y)
    return z, dz


Q0, K0, V0, U0, GS0, GA0, IN_COLS = 0, 1024, 1280, 1536, 2048, 3072, 4096


def mixer_forward(tag, h, p, tm):
    g, winT, wglu, wbsT, wba, wout = p["g"], p["winT"], p["wglu"], p["wbsT"], p["wba"], p["wout"]

    def proj(i, hv, g_ref, w_ref):
        _, _, n = _rms_fwd(hv, g_ref[...])
        nb = n.astype(BF16)
        return (nb, _nt(nb, w_ref[Q0:K0, :]) * QSCALE, _nt(nb, w_ref[K0:V0, :]), _nt(nb, w_ref[V0:U0, :]),
                _nt(nb, w_ref[U0:GS0, :]), _nt(nb, w_ref[GS0:GA0, :]), _nt(nb, w_ref[GA0:IN_COLS, :]))

    n, q, k, v, u, gs, ga = rowcall(
        tag + "_proj", proj, [h], [g, winT],
        [(D, BF16), (D, BF16), (N_KV * HD, BF16), (N_KV * HD, BF16), (SW, F32), (D, F32), (D, F32)], tm=tm)

    ya = attention_forward(tag, q, k, v, p["sink"])

    states, ydir = [], []
    for dr in range(2):
        s = p["ssm"][dr]
        xre, xim, yd = ssm_dir_forward(f"{tag}_ssm_fwd{dr}", u, s["bpr"], s["bpi"], s["cpr"], s["cpi"],
                                       s["a8"], s["tab"], dr == 1, tm)
        states.append((xre, xim))
        ydir.append(yd)

    def glu(i, y0, y1, uv, d_ref, w_ref):
        ypre = y0 + y1 + d_ref[...] * uv
        z, _ = _gelu(ypre)
        zb = z.astype(BF16)
        t = _nn(zb, w_ref[...])
        return ypre, zb, t, z * _sig(t)

    ypre, zb, t, ys = rowcall(tag + "_glu", glu, [ydir[0], ydir[1], u], [p["d"], wglu],
                              [(SW, F32), (SW, BF16), (SW, F32), (SW, BF16)], tm=tm)

    def merge(i, ysv, yav, gsv, gav, wbs_ref, wba_ref):
        bs = _nt(ysv, wbs_ref[...])
        ba = _nn(yav, wba_ref[...])
        mg = _sig(gsv) * bs + _sig(gav) * ba
        mg = jnp.where(_row_ok(i, tm), mg, 0.0)
        return bs, ba, mg

    bs, ba, mg = rowcall(tag + "_merge", merge, [ys, ya, gs, ga], [wbsT, wba],
                         [(D, BF16), (D, BF16), (D, BF16)], tm=tm)

    def outp(i, mv, hv, w_ref):
        return (hv + _nn(mv, w_ref[...]),)

    (h2,) = rowcall(tag + "_out", outp, [mg, h], [wout], [(D, F32)], tm=tm)
    saved = dict(h=h, n=n, q=q, k=k, v=v, u=u, gs=gs, ga=ga, ya=ya, states=states, ypre=ypre, zb=zb, t=t,
                 ys=ys, bs=bs, ba=ba, mg=mg)
    return h2, saved


def mixer_backward(tag, dh, sv, p, tm):
    g, winT, wglu, wbsT, wba, wout = p["g"], p["winT"], p["wglu"], p["wbsT"], p["wba"], p["wout"]

    def y1(i, dhv, bsv, bav, gsv, gav, w_ref):
        dhb = dhv.astype(BF16)
        dmg = _nt(dhb, w_ref[...])
        dmg = jnp.where(_row_ok(i, tm), dmg, 0.0)
        sgs = _sig(gsv)
        sga = _sig(gav)
        return (dmg * sgs, dmg * sga, dmg * bsv.astype(F32) * sgs * (1.0 - sgs),
                dmg * bav.astype(F32) * sga * (1.0 - sga), dhb)

    dbs, dba, dgs, dga, dhb = rowcall(tag + "_bwd_merge", y1, [dh, sv["bs"], sv["ba"], sv["gs"], sv["ga"]], [wout],
                                      [(D, BF16)] * 5, tm=tm)
    dwout = tn_matmul(tag + "_dwout", sv["mg"], dhb)

    def y2(i, dbsv, dbav, wbs_ref, wba_ref):
        return _nn(dbsv, wbs_ref[...]), _nt(dbav, wba_ref[...])

    dys, dya = rowcall(tag + "_bwd_branch", y2, [dbs, dba], [wbsT, wba], [(SW, F32), (D, BF16)], tm=tm)
    dwbsT = tn_matmul(tag + "_dwbs", dbs, sv["ys"])
    dwba = tn_matmul(tag + "_dwba", sv["ya"], dba)

    def s2b(i, dysv, ypv, tv, uv, d_ref, w_ref):
        z, dz_dy = _gelu(ypv)
        st = _sig(tv)
        dt_ = dysv * z * st * (1.0 - st)
        dz = dysv * st + _nt(dt_.astype(BF16), w_ref[...])
        dyp = dz * dz_dy
        return dyp, dyp * d_ref[...], dt_, jnp.sum(dyp * uv, axis=0, keepdims=True)

    dypb, du0, dtb, dd = rowcall(tag + "_bwd_glu", s2b, [dys, sv["ypre"], sv["t"], sv["u"]], [p["d"], wglu],
                                 [(SW, F32), (SW, F32), (SW, BF16)], [(1, SW)], tm=tm)
    dwglu = tn_matmul(tag + "_dwglu", sv["zb"], dtb)

    du_dirs, ssm_sums = [], []
    for dr in range(2):
        s = p["ssm"][dr]
        xre, xim = sv["states"][dr]
        res = ssm_dir_backward(f"{tag}_ssm_bwd{dr}", dypb, xre, xim, sv["u"], s["bpr"], s["bpi"], s["cpr"],
                               s["cpi"], s["a8_adj"], s["tab_adj"], dr == 0, tm)
        du_dirs.append(res[0])
        ssm_sums.append(res[1:])

    dq, dk, dv, dsink = attention_backward(tag, sv["q"], sv["k"], sv["v"], dya, p["sink"])

    def x1b(i, dqv, dkv, dvv, du0v, du1v, du2v, dgsv, dgav, hv, dhv, g_ref, w_ref):
        dub = (du0v + du1v + du2v).astype(BF16)
        dn = (_nn(dqv, w_ref[Q0:K0, :]) + _nn(dkv, w_ref[K0:V0, :]) + _nn(dvv, w_ref[V0:U0, :])
              + _nn(dub, w_ref[U0:GS0, :]) + _nn(dgsv, w_ref[GS0:GA0, :]) + _nn(dgav, w_ref[GA0:IN_COLS, :]))
        gv = g_ref[...]
        hh, r, _ = _rms_fwd(hv, gv)
        dx, dg = _rms_bwd(hh, r, gv, dn)
        dx = jnp.where(_row_ok(i, tm), dx, 0.0)
        return dhv + dx, dub, dg

    dh2, dub, dg = rowcall(tag + "_bwd_in", x1b,
                           [dq, dk, dv, du0, du_dirs[0], du_dirs[1], dgs, dga, sv["h"], dh], [g, winT],
                           [(D, F32), (SW, BF16)], [(1, D)], tm=tm)
    n = sv["n"]
    dwinT = jnp.concatenate([tn_matmul(f"{tag}_dwin{j}", piece, n)
                             for j, piece in enumerate((dq, dk, dv, dub, dgs, dga))], axis=0)
    grads = dict(g=dg, d=dd, sink=dsink, ssm=ssm_sums, winT=dwinT, wglu=dwglu, wbsT=dwbsT, wba=dwba, wout=dwout)
    return dh2, grads


W1024 = ("f1_wgT", "f1_wuT", "f1_wd", "winT", "wba", "wout", "f2_wgT", "f2_wuT", "f2_wd")
W512 = ("wglu", "wbsT")
SMALL = ("ffn1_norm", "mix_norm", "ffn2_norm", "final_norm", "ssm_lam_re", "ssm_lam_im", "ssm_log_dt",
         "ssm_b_re", "ssm_b_im", "ssm_c_re", "ssm_c_im", "ssm_d", "attn_sink")


def kernel(x, meta_tokens, ffn1_norm, ffn1_w_gate, ffn1_w_up, ffn1_w_down, mix_norm, w_in, ssm_lam_re, ssm_lam_im, ssm_log_dt, ssm_b_re, ssm_b_im, ssm_c_re, ssm_c_im, ssm_d, ssm_w_glu, attn_sink, w_branch_ssm, w_branch_attn, w_out, ffn2_norm, ffn2_w_gate, ffn2_w_up, ffn2_w_down, final_norm, loss_target, m_meta_tokens, m_ffn1_norm, m_ffn1_w_gate, m_ffn1_w_up, m_ffn1_w_down, m_mix_norm, m_w_in, m_ssm_lam_re, m_ssm_lam_im, m_ssm_log_dt, m_ssm_b_re, m_ssm_b_im, m_ssm_c_re, m_ssm_c_im, m_ssm_d, m_ssm_w_glu, m_attn_sink, m_w_branch_ssm, m_w_branch_attn, m_w_out, m_ffn2_norm, m_ffn2_w_gate, m_ffn2_w_up, m_ffn2_w_down, m_final_norm, v_meta_tokens, v_ffn1_norm, v_ffn1_w_gate, v_ffn1_w_up, v_ffn1_w_down, v_mix_norm, v_w_in, v_ssm_lam_re, v_ssm_lam_im, v_ssm_log_dt, v_ssm_b_re, v_ssm_b_im, v_ssm_c_re, v_ssm_c_im, v_ssm_d, v_ssm_w_glu, v_attn_sink, v_w_branch_ssm, v_w_branch_attn, v_w_out, v_ffn2_norm, v_ffn2_w_gate, v_ffn2_w_up, v_ffn2_w_down, v_final_norm):
    weights = dict(meta_tokens=meta_tokens, ffn1_norm=ffn1_norm, ffn1_w_gate=ffn1_w_gate, ffn1_w_up=ffn1_w_up, ffn1_w_down=ffn1_w_down, mix_norm=mix_norm, w_in=w_in, ssm_lam_re=ssm_lam_re, ssm_lam_im=ssm_lam_im, ssm_log_dt=ssm_log_dt, ssm_b_re=ssm_b_re, ssm_b_im=ssm_b_im, ssm_c_re=ssm_c_re, ssm_c_im=ssm_c_im, ssm_d=ssm_d, ssm_w_glu=ssm_w_glu, attn_sink=attn_sink, w_branch_ssm=w_branch_ssm, w_branch_attn=w_branch_attn, w_out=w_out, ffn2_norm=ffn2_norm, ffn2_w_gate=ffn2_w_gate, ffn2_w_up=ffn2_w_up, ffn2_w_down=ffn2_w_down, final_norm=final_norm)
    mom_m = dict(meta_tokens=m_meta_tokens, ffn1_norm=m_ffn1_norm, ffn1_w_gate=m_ffn1_w_gate, ffn1_w_up=m_ffn1_w_up, ffn1_w_down=m_ffn1_w_down, mix_norm=m_mix_norm, w_in=m_w_in, ssm_lam_re=m_ssm_lam_re, ssm_lam_im=m_ssm_lam_im, ssm_log_dt=m_ssm_log_dt, ssm_b_re=m_ssm_b_re, ssm_b_im=m_ssm_b_im, ssm_c_re=m_ssm_c_re, ssm_c_im=m_ssm_c_im, ssm_d=m_ssm_d, ssm_w_glu=m_ssm_w_glu, attn_sink=m_attn_sink, w_branch_ssm=m_w_branch_ssm, w_branch_attn=m_w_branch_attn, w_out=m_w_out, ffn2_norm=m_ffn2_norm, ffn2_w_gate=m_ffn2_w_gate, ffn2_w_up=m_ffn2_w_up, ffn2_w_down=m_ffn2_w_down, final_norm=m_final_norm)
    mom_v = dict(meta_tokens=v_meta_tokens, ffn1_norm=v_ffn1_norm, ffn1_w_gate=v_ffn1_w_gate, ffn1_w_up=v_ffn1_w_up, ffn1_w_down=v_ffn1_w_down, mix_norm=v_mix_norm, w_in=v_w_in, ssm_lam_re=v_ssm_lam_re, ssm_lam_im=v_ssm_lam_im, ssm_log_dt=v_ssm_log_dt, ssm_b_re=v_ssm_b_re, ssm_b_im=v_ssm_b_im, ssm_c_re=v_ssm_c_re, ssm_c_im=v_ssm_c_im, ssm_d=v_ssm_d, ssm_w_glu=v_ssm_w_glu, attn_sink=v_attn_sink, w_branch_ssm=v_w_branch_ssm, w_branch_attn=v_w_branch_attn, w_out=v_w_out, ffn2_norm=v_ffn2_norm, ffn2_w_gate=v_ffn2_w_gate, ffn2_w_up=v_ffn2_w_up, ffn2_w_down=v_ffn2_w_down, final_norm=v_final_norm)
    names = list(weights)

    L0 = x.shape[1]
    LP = L0 + BLK
    tm = 384 if LP % 384 == 0 else BLK
    x_i, y_i, c_i = lax.axis_index("x"), lax.axis_index("y"), lax.axis_index("c")
    me = 4 * x_i + 2 * y_i + c_i

    def canon(l):
        return dict(
            f1_wgT=ffn1_w_gate[l].T, f1_wuT=ffn1_w_up[l].T, f1_wd=ffn1_w_down[l],
            winT=w_in[l].T, wba=w_branch_attn[l], wout=w_out[l],
            f2_wgT=ffn2_w_gate[l].T, f2_wuT=ffn2_w_up[l].T, f2_wd=ffn2_w_down[l],
            wglu=ssm_w_glu[l], wbsT=w_branch_ssm[l].T)

    shards = [{nm: a.astype(BF16) for nm, a in canon(l).items()} for l in range(DEPTH)]

    def pieces_of(names_):
        out, off = [], 0
        for nm in names_:
            r = shards[0][nm].shape[0]
            out.append((off, r))
            off += r
        return out

    p1_pieces, p2_pieces = pieces_of(W1024), pieces_of(W512)
    packed = [(jnp.concatenate([shards[l][nm] for nm in W1024], axis=0),
               jnp.concatenate([shards[l][nm] for nm in W512], axis=0)) for l in range(DEPTH)]
    g1, g2, gm = all_gather_pieces(
        "gather_weights_first", [(packed[0][0], p1_pieces), (packed[0][1], p2_pieces), (meta_tokens, [(0, N_META)])])
    full = [dict(zip(W1024 + W512, list(g1) + list(g2)))] + [None] * (DEPTH - 1)
    meta_full = gm[0].reshape(NDEV, N_META, D // NDEV).transpose(1, 0, 2).reshape(N_META, D)
    gather_handles = [None] + [gather_layer_start(f"gather_start_l{l}", list(packed[l]), [p1_pieces, p2_pieces])
                               for l in range(1, DEPTH)]
    started = sum(hd[3][0, 0] for hd in gather_handles[1:])

    def finish_gather(l, after):
        dests = gather_layer_wait(f"gather_wait_l{l}", gather_handles[l], 2, after)
        out = {}
        for nm, dest in zip(W1024 + W512, dests):
            sh = shards[l][nm]
            out[nm] = lax.dynamic_update_slice(dest, sh, (me * sh.shape[0], 0))
        return out

    def disc_all(lre, lim, ldt, bre, bim):
        return _ssm_disc(lre, lim, ldt, bre, bim)

    ssm_p, ssm_vjp = [], []
    for l in range(DEPTH):
        row, vrow = [], []
        for dr in range(2):
            args = (ssm_lam_re[l, dr], ssm_lam_im[l, dr], ssm_log_dt[l, dr], ssm_b_re[l, dr], ssm_b_im[l, dr])
            (a_re, a_im, bb_re, bb_im), vjp = jax.vjp(disc_all, *args)
            a8, tab = _scan_tables(args[0], args[1], args[2], False, dr == 1, tm // 8)
            a8_adj, tab_adj = _scan_tables(args[0], args[1], args[2], True, dr == 0, tm // 8)
            row.append(dict(
                bpr=_pack_b(bb_re).astype(BF16), bpi=_pack_b(bb_im).astype(BF16),
                cpr=_pack_c(ssm_c_re[l, dr]).astype(BF16), cpi=_pack_c(ssm_c_im[l, dr]).astype(BF16),
                a8=a8, tab=tab, a8_adj=a8_adj, tab_adj=tab_adj, a_re=a_re, a_im=a_im))
            vrow.append(vjp)
        ssm_p.append(row)
        ssm_vjp.append(vrow)

    blk0 = jnp.concatenate([jnp.zeros((PAD, D), F32), meta_full.astype(F32)], axis=0)
    h = build_h0(x[0], blk0)
    saved = []
    for l in range(DEPTH):
        if l > 0:
            full[l] = finish_gather(l, h)
        w = full[l]
        g1n, g2n = ffn1_norm[l][None, :], ffn2_norm[l][None, :]
        if l == 0:
            g1n = g1n + started
        mp = dict(g=mix_norm[l][None, :], winT=w["winT"], wglu=w["wglu"], wbsT=w["wbsT"], wba=w["wba"],
                  wout=w["wout"], d=ssm_d[l][None, :], sink=attn_sink[l], ssm=ssm_p[l])
        h, s1 = ffn_forward("ffn1", h, g1n, w["f1_wgT"], w["f1_wuT"], w["f1_wd"], tm)
        h, s2 = mixer_forward("mix", h, mp, tm)
        h, s3 = ffn_forward("ffn2", h, g2n, w["f2_wgT"], w["f2_wuT"], w["f2_wd"], tm)
        saved.append((s1, s2, s3, mp, g1n, g2n))

    dh, loss_acc, dgf = final_loss(h, loss_target[0], final_norm[None, :])
    loss = lax.psum(loss_acc[0, 0], MESH_AXES)

    big_grads = [None] * DEPTH
    small = {nm: [None] * DEPTH for nm in SMALL if nm != "final_norm"}
    scatter_handles = [None] * DEPTH
    own_rows = [None] * DEPTH
    sent = jnp.zeros((), F32)
    for l in reversed(range(DEPTH)):
        s1, s2, s3, mp, g1n, g2n = saved[l]
        w = full[l]
        dh, dg2, f2g, f2u, f2d = ffn_backward("ffn2", dh, s3, g2n + sent, w["f2_wgT"], w["f2_wuT"], w["f2_wd"], tm)
        dh, mg = mixer_backward("mix", dh, s2, mp, tm)
        dh, dg1, f1g, f1u, f1d = ffn_backward("ffn1", dh, s1, g1n, w["f1_wgT"], w["f1_wuT"], w["f1_wd"], tm)
        big_grads[l] = dict(f1_wgT=f1g, f1_wuT=f1u, f1_wd=f1d, winT=mg["winT"], wba=mg["wba"], wout=mg["wout"],
                            f2_wgT=f2g, f2_wuT=f2u, f2_wd=f2d, wglu=mg["wglu"], wbsT=mg["wbsT"])
        own_rows[l] = [jnp.concatenate(
            [lax.dynamic_slice_in_dim(big_grads[l][nm], me * r, r, axis=0) for nm, (_, r) in zip(names_, pcs)],
            axis=0) for names_, pcs in ((W1024, p1_pieces), (W512, p2_pieces))]
        scatter_handles[l] = scatter_layer_start(
            f"scatter_start_l{l}", [[big_grads[l][nm] for nm in W1024], [big_grads[l][nm] for nm in W512]])
        sent = scatter_handles[l][0][3][0, 0]
        small["ffn1_norm"][l] = dg1[0]
        small["mix_norm"][l] = mg["g"][0]
        small["ffn2_norm"][l] = dg2[0]
        small["ssm_d"][l] = mg["d"][0]
        small["attn_sink"][l] = mg["sink"][0, :N_HEADS]
        per_dir = {k: [] for k in ("ssm_lam_re", "ssm_lam_im", "ssm_log_dt", "ssm_b_re", "ssm_b_im",
                                   "ssm_c_re", "ssm_c_im")}
        for dr in range(2):
            gbr, gbi, gcr, gci, s_re, s_im = mg["ssm"][dr]
            a_re, a_im = ssm_p[l][dr]["a_re"], ssm_p[l][dr]["a_im"]
            s_re = s_re.reshape(SGRP, SP)
            s_im = s_im.reshape(SGRP, SP)
            den = a_re * a_re + a_im * a_im
            ga_re = (s_re * a_re - s_im * a_im) / den
            ga_im = (s_re * a_im + s_im * a_re) / den
            glr, gli, gld, gbre, gbim = ssm_vjp[l][dr]((ga_re, ga_im, _unpack_diag(gbr), _unpack_diag(gbi)))
            per_dir["ssm_lam_re"].append(glr)
            per_dir["ssm_lam_im"].append(gli)
            per_dir["ssm_log_dt"].append(gld)
            per_dir["ssm_b_re"].append(gbre)
            per_dir["ssm_b_im"].append(gbim)
            per_dir["ssm_c_re"].append(_unpack_diag(gcr).transpose(0, 2, 1))
            per_dir["ssm_c_im"].append(_unpack_diag(gci).transpose(0, 2, 1))
        for k, vlist in per_dir.items():
            small[k][l] = jnp.stack(vlist)

    grad_x = dh[BLK:][None]
    dmeta_part = dh[PAD:BLK]

    small_part = {k: jnp.stack(vv) for k, vv in small.items()}
    small_part["final_norm"] = dgf[0]
    pieces = [small_part[k].reshape(-1) for k in SMALL] + [dmeta_part.reshape(-1)]
    sizes = [p_.shape[0] for p_ in pieces]
    total = sum(sizes)
    rows_s = -(-total // (8 * D)) * 8
    flat = jnp.concatenate(pieces + [jnp.zeros((rows_s * D - total,), F32)]).reshape(rows_s, D)
    ((gathered,),) = all_gather_pieces("gather_small_grads", [(flat, [(0, rows_s)])])
    small_sum = sum_slots("sum_small_grads", gathered.reshape(NDEV, rows_s, D)).reshape(-1)
    grads = {}
    o = 0
    for k, sz in zip(SMALL, sizes[:-1]):
        grads[k] = small_sum[o:o + sz].reshape(weights[k].shape)
        o += sz
    dmeta_full = small_sum[o:o + N_META * D].reshape(N_META, D)
    grads["meta_tokens"] = lax.dynamic_slice_in_dim(dmeta_full, me * (D // NDEV), D // NDEV, axis=1)

    own = [dict() for _ in range(DEPTH)]
    for l in reversed(range(DEPTH)):
        handle, nin = scatter_handles[l]
        lands = scatter_layer_wait(f"scatter_wait_l{l}", handle, nin, 2, dh)
        for land, mine, names_, pcs, tag in ((lands[0], own_rows[l][0], W1024, p1_pieces, "1024"),
                                             (lands[1], own_rows[l][1], W512, p2_pieces, "512")):
            land = lax.dynamic_update_slice(land, mine[None], (me, 0, 0))
            tot = sum_slots("sum_weight_grads_" + tag, land)
            for nm, (off_, r) in zip(names_, pcs):
                own[l][nm] = tot[off_:off_ + r]

    def stack(fn):
        return jnp.stack([fn(own[l]) for l in range(DEPTH)])

    grads["ffn1_w_gate"] = stack(lambda d: d["f1_wgT"].T)
    grads["ffn1_w_up"] = stack(lambda d: d["f1_wuT"].T)
    grads["ffn1_w_down"] = stack(lambda d: d["f1_wd"])
    grads["w_in"] = stack(lambda d: d["winT"].T)
    grads["ssm_w_glu"] = stack(lambda d: d["wglu"])
    grads["w_branch_ssm"] = stack(lambda d: d["wbsT"].T)
    grads["w_branch_attn"] = stack(lambda d: d["wba"])
    grads["w_out"] = stack(lambda d: d["wout"])
    grads["ffn2_w_gate"] = stack(lambda d: d["f2_wgT"].T)
    grads["ffn2_w_up"] = stack(lambda d: d["f2_wuT"].T)
    grads["ffn2_w_down"] = stack(lambda d: d["f2_wd"])

    deltas, new_m, new_v = {}, {}, {}
    for nm in names:
        deltas[nm], new_m[nm], new_v[nm] = adamw("adamw_" + nm, weights[nm], grads[nm], mom_m[nm], mom_v[nm])

    return (loss, grad_x, *[grads[n] for n in names], *[deltas[n] for n in names],
            *[new_m[n] for n in names], *[new_v[n] for n in names])
```

```python
import functools
import math

import jax
import jax.numpy as jnp
from jax import lax
from jax.experimental import pallas as pl
from jax.experimental.pallas import tpu as pltpu

F32 = jnp.float32
BF16 = jnp.bfloat16

D = 1024
DFF = 2816
N_META = 16
N_HEADS = 16
N_KV = 4
HD = 64
QG = 4
WIN = 128
BLK = 128
PAD = BLK - N_META
SW = 512
SGRP = 32
SCH = 16
SP = 64
NST = SGRP * SP
EPS = 1e-6
NEG = -1e30
SCALE = HD ** -0.5
NDEV = 8
DEPTH = 4
MESH_AXES = ("x", "y", "c")
MESH = pl.DeviceIdType.MESH

ADAM_LR = 0.001
ADAM_B1 = 0.9
ADAM_B2 = 0.999
ADAM_EPS = 1e-08
ADAM_WD = 0.01
ADAM_STEP = 10

VMEM_LIMIT = 56 * 1024 * 1024


def _params(*sem):
    return pltpu.CompilerParams(dimension_semantics=sem, vmem_limit_bytes=VMEM_LIMIT)


def _nn(a, b):
    return lax.dot_general(a, b, (((1,), (0,)), ((), ())), preferred_element_type=F32)


def _nt(a, b):
    return lax.dot_general(a, b, (((1,), (1,)), ((), ())), preferred_element_type=F32)


def _tn(a, b):
    return lax.dot_general(a, b, (((0,), (0,)), ((), ())), preferred_element_type=F32)


def _sig(x):
    return 1.0 / (1.0 + jnp.exp(-x))


def _rms_fwd(h, g):
    r = lax.rsqrt(jnp.mean(h * h, axis=-1, keepdims=True) + EPS)
    hh = h * r
    return hh, r, hh * g


def _rms_bwd(hh, r, g, dn):
    dhh = dn * g
    dx = r * (dhh - hh * jnp.mean(dhh * hh, axis=-1, keepdims=True))
    return dx, jnp.sum(dn * hh, axis=0, keepdims=True)


def _row_ok(i, tm):
    rows = i * tm + lax.broadcasted_iota(jnp.int32, (tm, 1), 0)
    return rows >= PAD


def _const_spec(shape):
    nd = len(shape)
    return pl.BlockSpec(shape, lambda *_: (0,) * nd)


def rowcall(name, body, rows, consts, outs, accs=(), *, tm):
    nrows = rows[0].shape[0]
    nt = nrows // tm
    assert nt * tm == nrows, (name, nrows, tm)
    nr, nc, no, na = len(rows), len(consts), len(outs), len(accs)
    in_specs = [pl.BlockSpec((tm, r.shape[1]), lambda i: (i, 0)) for r in rows]
    in_specs += [_const_spec(c.shape) for c in consts]
    out_shape = [jax.ShapeDtypeStruct((nrows, w), dt) for (w, dt) in outs]
    out_specs = [pl.BlockSpec((tm, w), lambda i: (i, 0)) for (w, dt) in outs]
    out_shape += [jax.ShapeDtypeStruct(s, F32) for s in accs]
    out_specs += [_const_spec(s) for s in accs]

    def kern(*refs):
        i = pl.program_id(0)
        row_vals = [r[...] for r in refs[:nr]]
        res = body(i, *row_vals, *refs[nr:nr + nc])
        out_refs = refs[nr + nc:nr + nc + no]
        acc_refs = refs[nr + nc + no:]
        for r, v in zip(out_refs, res[:no]):
            r[...] = v.astype(r.dtype)
        if na:
            @pl.when(i == 0)
            def _():
                for r in acc_refs:
                    r[...] = jnp.zeros_like(r)
            for r, v in zip(acc_refs, res[no:]):
                r[...] += v

    res = pl.pallas_call(
        kern, name=name, grid=(nt,), in_specs=in_specs, out_specs=out_specs, out_shape=out_shape,
        compiler_params=_params("arbitrary"),
    )(*rows, *consts)
    return res


def tn_matmul(name, lhs, rhs, scale=1.0):
    M, K = lhs.shape
    N = rhs.shape[1]
    assert lhs.dtype == BF16 and rhs.dtype == BF16
    nm = 6
    tmw = M // nm
    assert tmw * nm == M and tmw % 16 == 0
    tk = 1408 if (K % 1408 == 0) else K
    nk = K // tk

    def kern(a_ref, b_ref, o_ref, acc):
        m = pl.program_id(1)
        part = _tn(a_ref[...], b_ref[...])

        @pl.when(m == 0)
        def _():
            acc[...] = part

        @pl.when((m > 0) & (m < nm - 1))
        def _():
            acc[...] += part

        @pl.when(m == nm - 1)
        def _():
            o_ref[...] = ((acc[...] + part) * scale).astype(o_ref.dtype)

    return pl.pallas_call(
        kern, name=name, grid=(nk, nm),
        in_specs=[pl.BlockSpec((tmw, tk), lambda k, m: (m, k)), pl.BlockSpec((tmw, N), lambda k, m: (m, 0))],
        out_specs=pl.BlockSpec((tk, N), lambda k, m: (k, 0)),
        out_shape=jax.ShapeDtypeStruct((K, N), BF16),
        scratch_shapes=[pltpu.VMEM((tk, N), F32)],
        compiler_params=_params("arbitrary", "arbitrary"),
    )(lhs, rhs)


def _mesh_pos():
    x, y, c = lax.axis_index("x"), lax.axis_index("y"), lax.axis_index("c")
    return x, y, c


def all_gather_pieces(name, groups):
    ng = len(groups)
    packed = [g[0] for g in groups]
    pieces = [g[1] for g in groups]
    out_shape, out_map = [], []
    for gi, (p, pcs) in enumerate(groups):
        idx = []
        for (off, r) in pcs:
            idx.append(len(out_shape))
            out_shape.append(jax.ShapeDtypeStruct((NDEV * r, p.shape[1]), p.dtype))
        out_map.append(idx)
    nout = len(out_shape)

    def body(*refs):
        p_refs = refs[:ng]
        o_refs = refs[ng:ng + nout]
        send_sems, recv_sems, local_sems = refs[ng + nout:]
        x, y, c = _mesh_pos()
        me = (x, y, c)
        sibling = (x, y, 1 - c)
        chips = [(1 - x, y), (x, 1 - y), (1 - x, 1 - y)]

        def blk(px, py, pc):
            return 4 * px + 2 * py + pc

        def copies(gi, k, origin, to, from_out):
            cps = []
            for (off, r), oi in zip(pieces[gi], out_map[gi]):
                dst = o_refs[oi].at[pl.ds(origin * r, r), :]
                src = dst if from_out else p_refs[gi].at[pl.ds(off, r), :]
                cps.append(pltpu.make_async_remote_copy(
                    src_ref=src, dst_ref=dst, send_sem=send_sems.at[gi, k], recv_sem=recv_sems.at[gi, k],
                    device_id=to, device_id_type=MESH))
            return cps

        def whole(gi, k):
            return pltpu.make_async_remote_copy(
                src_ref=p_refs[gi], dst_ref=p_refs[gi], send_sem=send_sems.at[gi, k],
                recv_sem=recv_sems.at[gi, k], device_id=me, device_id_type=MESH)

        mine = []
        for gi in range(ng):
            for (off, r), oi in zip(pieces[gi], out_map[gi]):
                mine.append(pltpu.make_async_copy(
                    p_refs[gi].at[pl.ds(off, r), :], o_refs[oi].at[pl.ds(blk(*me) * r, r), :],
                    local_sems.at[gi]))
        for cp in mine:
            cp.start()
        for gi in range(ng):
            for cp in copies(gi, 0, blk(*me), sibling, False):
                cp.start()
            for j, chip in enumerate(chips):
                for cp in copies(gi, 1 + j, blk(*me), (*chip, c), False):
                    cp.start()
        for j, chip in enumerate(chips):
            for gi in range(ng):
                whole(gi, 1 + j).wait_recv()
                for cp in copies(gi, 4 + j, blk(*chip, c), sibling, True):
                    cp.start()
        for gi in range(ng):
            whole(gi, 0).wait_recv()
            for j in range(3):
                whole(gi, 4 + j).wait_recv()
        for gi in range(ng):
            for k in range(7):
                whole(gi, k).wait_send()
            pltpu.make_async_copy(p_refs[gi], p_refs[gi], local_sems.at[gi]).wait()

    any_spec = pl.BlockSpec(memory_space=pl.ANY)
    outs = pl.pallas_call(
        body, name=name, out_shape=out_shape,
        in_specs=[any_spec] * ng, out_specs=[any_spec] * nout,
        scratch_shapes=[pltpu.SemaphoreType.DMA((ng, 7)), pltpu.SemaphoreType.DMA((ng, 7)),
                        pltpu.SemaphoreType.DMA((ng,))],
    )(*packed)
    return [[outs[oi] for oi in idx] for idx in out_map]


HBM_SPEC = pl.BlockSpec(memory_space=pltpu.HBM)
SEM_SPEC = pl.BlockSpec(memory_space=pltpu.SEMAPHORE)
DATAFLOW = pltpu.SideEffectType.DATAFLOW_SIDE_EFFECTING


def _peers(x, y, c):
    return [(x, y, 1 - c), (1 - x, y, c), (x, 1 - y, c), (1 - x, 1 - y, c),
            (1 - x, y, 1 - c), (x, 1 - y, 1 - c), (1 - x, 1 - y, 1 - c)]


def exchange_start(name, arrays, ng, plan):
    n = len(arrays)
    ns = ng * 7

    def body(*refs):
        in_refs = refs[:n]
        send_sems, recv_sems = refs[n:n + ns], refs[n + ns:n + 2 * ns]
        token = refs[-1]
        x, y, c = _mesh_pos()
        me_i = 4 * x + 2 * y + c
        for k, peer in enumerate(_peers(x, y, c)):
            p_i = 4 * peer[0] + 2 * peer[1] + peer[2]
            for src, dst, gi in plan(in_refs, me_i, p_i):
                pltpu.make_async_remote_copy(
                    src_ref=src, dst_ref=dst, send_sem=send_sems[gi * 7 + k], recv_sem=recv_sems[gi * 7 + k],
                    device_id=peer, device_id_type=MESH).start()
        token[...] = jnp.zeros_like(token)

    res = pl.pallas_call(
        body, name=name,
        out_shape=(*[pltpu.SemaphoreType.DMA(())] * (2 * ns),
                   *[pltpu.HBM(a.shape, a.dtype) for a in arrays], jax.ShapeDtypeStruct((8, 128), F32)),
        in_specs=[HBM_SPEC] * n,
        out_specs=(*[SEM_SPEC] * (2 * ns), *[HBM_SPEC] * n, pl.BlockSpec(memory_space=pltpu.VMEM)),
        input_output_aliases={i: 2 * ns + i for i in range(n)},
        compiler_params=pltpu.CompilerParams(has_side_effects=DATAFLOW),
    )(*[pltpu.with_memory_space_constraint(a, pltpu.HBM) for a in arrays])
    return list(res[:ns]), list(res[ns:2 * ns]), list(res[2 * ns:2 * ns + n]), res[-1]


def exchange_wait(name, send_sems, recv_sems, arrays, ng, sized, after):
    n = len(arrays)
    ns = ng * 7

    def body(*refs):
        in_refs = refs[:n]
        s_sems, r_sems = refs[n:n + ns], refs[n + ns:n + 2 * ns]
        x, y, c = _mesh_pos()
        for gi in range(ng):
            view = sized(in_refs, gi)
            for k in range(7):
                w = pltpu.make_async_remote_copy(
                    src_ref=view, dst_ref=view, send_sem=s_sems[gi * 7 + k], recv_sem=r_sems[gi * 7 + k],
                    device_id=(x, y, c), device_id_type=MESH)
                w.wait_send()
                w.wait_recv()

    res = pl.pallas_call(
        body, name=name, out_shape=tuple(pltpu.HBM(a.shape, a.dtype) for a in arrays),
        in_specs=[HBM_SPEC] * n + [SEM_SPEC] * (2 * ns) + [pl.BlockSpec(memory_space=pl.ANY)],
        out_specs=tuple([HBM_SPEC] * n), input_output_aliases={i: i for i in range(n)},
        compiler_params=pltpu.CompilerParams(has_side_effects=DATAFLOW),
    )(*arrays, *send_sems, *recv_sems, after)
    return list(res)


def gather_layer_start(name, packed, pieces):
    ng = len(packed)
    dests = [lax.empty((NDEV * r, p.shape[1]), p.dtype) for p, pcs in zip(packed, pieces) for (_, r) in pcs]

    def plan(refs, me_i, p_i):
        out, di = [], ng
        for gi in range(ng):
            for (off, r) in pieces[gi]:
                out.append((refs[gi].at[pl.ds(off, r), :], refs[di].at[pl.ds(me_i * r, r), :], gi))
                di += 1
        return out

    return exchange_start(name, list(packed) + dests, ng, plan)


def gather_layer_wait(name, handle, ng, after):
    send_sems, recv_sems, arrays, _ = handle
    out = exchange_wait(name, send_sems, recv_sems, arrays, ng, lambda refs, gi: refs[gi], after)
    return out[ng:]


def scatter_layer_start(name, groups):
    ng = len(groups)
    flat = [a for arrs in groups for a in arrs]
    offs, lands = [], []
    for arrs in groups:
        o, off = [], 0
        for a in arrs:
            r = a.shape[0] // NDEV
            o.append((off, r))
            off += r
        offs.append(o)
        lands.append(jnp.zeros((NDEV, off, arrs[0].shape[1]), arrs[0].dtype))
    nin = len(flat)

    def plan(refs, me_i, p_i):
        out, ai = [], 0
        for gi in range(ng):
            for (off, r) in offs[gi]:
                out.append((refs[ai].at[pl.ds(p_i * r, r), :], refs[nin + gi].at[me_i, pl.ds(off, r), :], gi))
                ai += 1
        return out

    return exchange_start(name, flat + lands, ng, plan), nin


def scatter_layer_wait(name, handle, nin, ng, after):
    send_sems, recv_sems, arrays, _ = handle
    out = exchange_wait(name, send_sems, recv_sems, arrays, ng, lambda refs, gi: refs[nin + gi].at[0], after)
    return out[nin:]


def _pick_tile(n, cap):
    best = None
    for t in range(8, min(n, cap) + 1, 8):
        if n % t == 0:
            best = t
    return best if best is not None else n


def sum_slots(name, land):
    _, R, W = land.shape
    tr = _pick_tile(R, 512)

    def kern(l_ref, o_ref):
        acc = l_ref[0].astype(F32)
        for s in range(1, NDEV):
            acc = acc + l_ref[s].astype(F32)
        o_ref[...] = acc

    return pl.pallas_call(
        kern, name=name, grid=(R // tr,),
        in_specs=[pl.BlockSpec((NDEV, tr, W), lambda i: (0, i, 0))],
        out_specs=pl.BlockSpec((tr, W), lambda i: (i, 0)),
        out_shape=jax.ShapeDtypeStruct((R, W), F32),
        compiler_params=_params("arbitrary"),
    )(land)


def adamw(name, w, g, m, v):
    shp = w.shape
    C = shp[-1]
    R = max(1, math.prod(shp[:-1]))
    tr = _pick_tile(R, 1024)
    w2, g2, m2, v2 = (a.reshape(R, C) for a in (w, g, m, v))

    def kern(w_ref, g_ref, m_ref, v_ref, d_ref, mo_ref, vo_ref):
        gg = g_ref[...]
        mn = ADAM_B1 * m_ref[...] + (1.0 - ADAM_B1) * gg
        vn = ADAM_B2 * v_ref[...] + (1.0 - ADAM_B2) * jnp.square(gg)
        m_hat = mn / (1.0 - ADAM_B1 ** ADAM_STEP)
        v_hat = vn / (1.0 - ADAM_B2 ** ADAM_STEP)
        d_ref[...] = -ADAM_LR * (m_hat / (jnp.sqrt(v_hat) + ADAM_EPS) + ADAM_WD * w_ref[...])
        mo_ref[...] = mn
        vo_ref[...] = vn

    spec = pl.BlockSpec((tr, C), lambda i: (i, 0))
    d, mo, vo = pl.pallas_call(
        kern, name=name, grid=(R // tr,), in_specs=[spec] * 4, out_specs=[spec] * 3,
        out_shape=[jax.ShapeDtypeStruct((R, C), F32)] * 3, compiler_params=_params("arbitrary"),
    )(w2, g2, m2, v2)
    return d.reshape(shp), mo.reshape(shp), vo.reshape(shp)


def build_h0(x2, blk0):
    L0 = x2.shape[0]
    nb = L0 // BLK + 1

    def kern(x_ref, b_ref, o_ref):
        i = pl.program_id(0)

        @pl.when(i == 0)
        def _():
            o_ref[...] = b_ref[...]

        @pl.when(i > 0)
        def _():
            o_ref[...] = x_ref[...]

    return pl.pallas_call(
        kern, name="build_h0", grid=(nb,),
        in_specs=[pl.BlockSpec((BLK, D), lambda i: (jnp.maximum(i - 1, 0), 0)), _const_spec((BLK, D))],
        out_specs=pl.BlockSpec((BLK, D), lambda i: (i, 0)),
        out_shape=jax.ShapeDtypeStruct((L0 + BLK, D), F32), compiler_params=_params("arbitrary"),
    )(x2, blk0)


def final_loss(h, tgt, gf):
    LP = h.shape[0]
    nb = LP // BLK

    def kern(h_ref, t_ref, g_ref, dh_ref, loss_ref, dg_ref):
        i = pl.program_id(0)

        @pl.when(i == 0)
        def _():
            loss_ref[...] = jnp.zeros_like(loss_ref)
            dg_ref[...] = jnp.zeros_like(dg_ref)

        g = g_ref[...]
        hh, r, yv = _rms_fwd(h_ref[...], g)
        valid = (i > 0).astype(F32)
        err = (yv - t_ref[...]) * valid
        loss_ref[...] += 0.5 * jnp.sum(jnp.sum(err * err, axis=1, keepdims=True), axis=0, keepdims=True) / D
        dy = err / D
        dx, dg = _rms_bwd(hh, r, g, dy)
        dh_ref[...] = dx
        dg_ref[...] += dg

    return pl.pallas_call(
        kern, name="final_loss", grid=(nb,),
        in_specs=[pl.BlockSpec((BLK, D), lambda i: (i, 0)),
                  pl.BlockSpec((BLK, D), lambda i: (jnp.maximum(i - 1, 0), 0)), _const_spec((1, D))],
        out_specs=[pl.BlockSpec((BLK, D), lambda i: (i, 0)), _const_spec((8, 128)), _const_spec((1, D))],
        out_shape=[jax.ShapeDtypeStruct((LP, D), F32), jax.ShapeDtypeStruct((8, 128), F32),
                   jax.ShapeDtypeStruct((1, D), F32)],
        compiler_params=_params("arbitrary"),
    )(h, tgt, gf)


def ffn_forward(tag, h, g, wgT, wuT, wd, tm):
    def f1(i, hv, g_ref, wg_ref, wu_ref):
        _, _, n = _rms_fwd(hv, g_ref[...])
        nb = n.astype(BF16)
        G = _nt(nb, wg_ref[...])
        U = _nt(nb, wu_ref[...])
        A = G * _sig(G) * U
        return nb, G, U, A

    n, G, U, A = rowcall(tag + "_up", f1, [h], [g, wgT, wuT],
                         [(D, BF16), (DFF, BF16), (DFF, BF16), (DFF, BF16)], tm=tm)

    def f2(i, av, hv, wd_ref):
        return (hv + 0.5 * _nn(av, wd_ref[...]),)

    (h2,) = rowcall(tag + "_down", f2, [A, h], [wd], [(D, F32)], tm=tm)
    return h2, (h, n, G, U, A)


def ffn_backward(tag, dh, saved, g, wgT, wuT, wd, tm):
    h, n, G, U, A = saved

    def b1(i, dhv, Gv, Uv, wd_ref):
        dyb = (0.5 * dhv).astype(BF16)
        dA = _nt(dyb, wd_ref[...])
        Gf = Gv.astype(F32)
        sg = _sig(Gf)
        dG = dA * Uv.astype(F32) * (sg * (1.0 + Gf * (1.0 - sg)))
        dU = dA * (Gf * sg)
        return dG, dU, dyb

    dG, dU, dyb = rowcall(tag + "_bwd_act", b1, [dh, G, U], [wd], [(DFF, BF16), (DFF, BF16), (D, BF16)], tm=tm)

    def b2(i, dGv, dUv, hv, dhv, g_ref, wg_ref, wu_ref):
        dn = _nn(dGv, wg_ref[...]) + _nn(dUv, wu_ref[...])
        gv = g_ref[...]
        hh, r, _ = _rms_fwd(hv, gv)
        dx, dg = _rms_bwd(hh, r, gv, dn)
        dx = jnp.where(_row_ok(i, tm), dx, 0.0)
        return dhv + dx, dg

    dh2, dg = rowcall(tag + "_bwd_in", b2, [dG, dU, h, dh], [g, wgT, wuT], [(D, F32)], [(1, D)], tm=tm)
    dwd = tn_matmul(tag + "_dwd", A, dyb)
    dwgT = tn_matmul(tag + "_dwg", dG, n)
    dwuT = tn_matmul(tag + "_dwu", dU, n)
    return dh2, dg, dwgT, dwuT, dwd


def _alibi_slope(head):
    return float(2.0 ** (-8.0 * (head + 1) / N_HEADS))


def _att_bias(n, nb):
    qi = lax.broadcasted_iota(jnp.int32, (BLK, 4 * BLK), 0)
    cj = lax.broadcasted_iota(jnp.int32, (BLK, 4 * BLK), 1)
    jb = cj - BLK
    dist = jnp.abs(qi + BLK - jb)
    kpos = (n - 1) * BLK + jb
    band_ok = (dist <= WIN) & (kpos >= BLK) & (kpos < nb * BLK)
    is_meta = cj < BLK
    ok = (is_meta & (cj >= PAD)) | (jnp.logical_not(is_meta) & band_ok)
    distf = jnp.where(is_meta, 0, dist).astype(F32)
    maskadd = jnp.where(ok, 0.0, NEG).astype(F32)
    distf4 = jnp.concatenate([distf] * QG, axis=0)
    mask4 = jnp.concatenate([maskadd] * QG, axis=0)
    return distf4, mask4


def _group_col(vals):
    rg = lax.broadcasted_iota(jnp.int32, (QG * BLK, 1), 0) // BLK
    col = jnp.full((QG * BLK, 1), vals[QG - 1], F32)
    for gq in range(QG - 2, -1, -1):
        col = jnp.where(rg == gq, vals[gq], col)
    return col


def _stack_heads(ref_or_val, kh):
    return jnp.concatenate(
        [ref_or_val[:, (kh * QG + gq) * HD:(kh * QG + gq + 1) * HD] for gq in range(QG)], axis=0)


def _stack_keys(km, kp, kc, kn, kh):
    sl = slice(kh * HD, (kh + 1) * HD)
    return jnp.concatenate([km[:, sl], kp[:, sl], kc[:, sl], kn[:, sl]], axis=0)


LOG2E = 1.4426950408889634
LN2 = 0.6931471805599453
QSCALE = SCALE * LOG2E


def _att_update_bias(bias_ref, n, nb):
    @pl.when((n <= 2) | (n == nb - 1))
    def _():
        distf4, mask4 = _att_bias(n, nb)
        for kh in range(N_KV):
            slope_col = _group_col([_alibi_slope(kh * QG + gq) * LOG2E for gq in range(QG)])
            bias_ref[kh] = mask4 - slope_col * distf4


def _att_exp(qs, kb, bias_ref, kh, sink_ref):
    sink_col = _group_col([sink_ref[kh * QG + gq] for gq in range(QG)]) * LOG2E
    s = _nt(qs, kb) + bias_ref[kh]
    m = jnp.maximum(jnp.max(s, axis=1, keepdims=True), sink_col)
    e = jnp.exp2(s - m)
    es = jnp.exp2(sink_col - m)
    inv = 1.0 / (jnp.sum(e, axis=1, keepdims=True) + es)
    return e, es, inv


def attention_forward(tag, q, k, v, sink):
    LP = q.shape[0]
    nb = LP // BLK

    def kern(sink_ref, q_ref, km_ref, kp_ref, kc_ref, kn_ref, vm_ref, vp_ref, vc_ref, vn_ref, o_ref, bias_ref):
        n = pl.program_id(0)
        _att_update_bias(bias_ref, n, nb)
        qv = q_ref[...]
        km, kp, kc, kn = km_ref[...], kp_ref[...], kc_ref[...], kn_ref[...]
        vm, vp, vc, vn = vm_ref[...], vp_ref[...], vc_ref[...], vn_ref[...]
        for kh in range(N_KV):
            qs = _stack_heads(qv, kh)
            kb = _stack_keys(km, kp, kc, kn, kh)
            vb = _stack_keys(vm, vp, vc, vn, kh)
            e, _, inv = _att_exp(qs, kb, bias_ref, kh, sink_ref)
            o = _nn(e.astype(BF16), vb) * inv
            for gq in range(QG):
                hcol = (kh * QG + gq) * HD
                o_ref[:, hcol:hcol + HD] = o[gq * BLK:(gq + 1) * BLK].astype(o_ref.dtype)

    def kvspec(dn):
        return pl.BlockSpec((BLK, N_KV * HD), lambda n: (jnp.clip(n + dn, 0, nb - 1), 0))

    meta_spec = pl.BlockSpec((BLK, N_KV * HD), lambda n: (0, 0))
    return pl.pallas_call(
        kern, name=tag + "_att_fwd", grid=(nb,),
        in_specs=[pl.BlockSpec(memory_space=pltpu.SMEM), pl.BlockSpec((BLK, D), lambda n: (n, 0)),
                  meta_spec, kvspec(-1), kvspec(0), kvspec(1), meta_spec, kvspec(-1), kvspec(0), kvspec(1)],
        out_specs=pl.BlockSpec((BLK, D), lambda n: (n, 0)),
        out_shape=jax.ShapeDtypeStruct((LP, D), BF16),
        scratch_shapes=[pltpu.VMEM((N_KV, QG * BLK, 4 * BLK), F32)], compiler_params=_params("arbitrary"),
    )(sink, q, k, k, k, k, v, v, v, v)


def attention_backward(tag, q, k, v, do, sink):
    LP = q.shape[0]
    nb = LP // BLK
    KW = N_KV * HD

    def kern(sink_ref, q_ref, do_ref, km_ref, kp_ref, kc_ref, kn_ref, vm_ref, vp_ref, vc_ref, vn_ref,
             dq_ref, dkp_ref, dvp_ref, dkm_ref, dvm_ref, dsink_ref, bias_ref):
        n = pl.program_id(0)

        @pl.when(n == 0)
        def _():
            dkm_ref[...] = jnp.zeros_like(dkm_ref)
            dvm_ref[...] = jnp.zeros_like(dvm_ref)
            dsink_ref[...] = jnp.zeros_like(dsink_ref)

        _att_update_bias(bias_ref, n, nb)
        qv, dov = q_ref[...], do_ref[...]
        km, kp, kc, kn = km_ref[...], kp_ref[...], kc_ref[...], kn_ref[...]
        vm, vp, vc, vn = vm_ref[...], vp_ref[...], vc_ref[...], vn_ref[...]
        lane = lax.broadcasted_iota(jnp.int32, (8, 128), 1)
        dsink = jnp.zeros((8, 128), F32)
        for kh in range(N_KV):
            qs = _stack_heads(qv, kh)
            dos = _stack_heads(dov, kh)
            kb = _stack_keys(km, kp, kc, kn, kh)
            vb = _stack_keys(vm, vp, vc, vn, kh)
            dp = _nt(dos, vb)
            e, es, inv = _att_exp(qs, kb, bias_ref, kh, sink_ref)
            delta = inv * jnp.sum(e * dp, axis=1, keepdims=True)
            ds = (e * ((dp - delta) * inv)).astype(BF16)
            dqs = _nn(ds, kb) * SCALE
            dkb = _tn(ds, qs) * LN2
            dvb = _tn(e.astype(BF16), (dos.astype(F32) * inv).astype(BF16))
            dsk = -(es * inv * delta)
            sl = slice(kh * HD, (kh + 1) * HD)
            for gq in range(QG):
                hcol = (kh * QG + gq) * HD
                dq_ref[:, hcol:hcol + HD] = dqs[gq * BLK:(gq + 1) * BLK].astype(dq_ref.dtype)
                tot = jnp.sum(dsk[gq * BLK:(gq + 1) * BLK], axis=0, keepdims=True)
                dsink = dsink + jnp.where(lane == kh * QG + gq, tot, 0.0)
            dkm_ref[:, sl] += dkb[0:BLK]
            dvm_ref[:, sl] += dvb[0:BLK]
            for slot in range(3):
                dkp_ref[0, slot, :, sl] = dkb[(slot + 1) * BLK:(slot + 2) * BLK]
                dvp_ref[0, slot, :, sl] = dvb[(slot + 1) * BLK:(slot + 2) * BLK]
        dsink_ref[...] += dsink

    def kvspec(dn):
        return pl.BlockSpec((BLK, KW), lambda n: (jnp.clip(n + dn, 0, nb - 1), 0))

    meta_spec = pl.BlockSpec((BLK, KW), lambda n: (0, 0))
    rowspec = pl.BlockSpec((BLK, D), lambda n: (n, 0))
    part_spec = pl.BlockSpec((1, 3, BLK, KW), lambda n: (n, 0, 0, 0))
    dq, dkp, dvp, dkm, dvm, dsink = pl.pallas_call(
        kern, name=tag + "_att_bwd", grid=(nb,),
        in_specs=[pl.BlockSpec(memory_space=pltpu.SMEM), rowspec, rowspec,
                  meta_spec, kvspec(-1), kvspec(0), kvspec(1), meta_spec, kvspec(-1), kvspec(0), kvspec(1)],
        out_specs=[rowspec, part_spec, part_spec, _const_spec((BLK, KW)), _const_spec((BLK, KW)),
                   _const_spec((8, 128))],
        out_shape=[jax.ShapeDtypeStruct((LP, D), BF16), jax.ShapeDtypeStruct((nb, 3, BLK, KW), F32),
                   jax.ShapeDtypeStruct((nb, 3, BLK, KW), F32), jax.ShapeDtypeStruct((BLK, KW), F32),
                   jax.ShapeDtypeStruct((BLK, KW), F32), jax.ShapeDtypeStruct((8, 128), F32)],
        scratch_shapes=[pltpu.VMEM((N_KV, QG * BLK, 4 * BLK), F32)], compiler_params=_params("arbitrary"),
    )(sink, q, do, k, k, k, k, v, v, v, v)

    def comb(a_ref, b_ref, c_ref, m_ref, a2_ref, b2_ref, c2_ref, m2_ref, dk_ref, dv_ref):
        mblk = pl.program_id(0)
        has_prev = (mblk > 0).astype(F32)
        has_next = (mblk < nb - 1).astype(F32)
        is0 = (mblk == 0).astype(F32)
        dk_ref[...] = (a_ref[0, 0] * has_prev + b_ref[0, 0] + c_ref[0, 0] * has_next
                       + m_ref[...] * is0).astype(dk_ref.dtype)
        dv_ref[...] = (a2_ref[0, 0] * has_prev + b2_ref[0, 0] + c2_ref[0, 0] * has_next
                       + m2_ref[...] * is0).astype(dv_ref.dtype)

    def pspec(dn, slot):
        return pl.BlockSpec((1, 1, BLK, KW), lambda m: (jnp.clip(m + dn, 0, nb - 1), slot, 0, 0))

    kvout = pl.BlockSpec((BLK, KW), lambda m: (m, 0))
    dk, dv = pl.pallas_call(
        comb, name=tag + "_att_dkv", grid=(nb,),
        in_specs=[pspec(-1, 2), pspec(0, 1), pspec(1, 0), _const_spec((BLK, KW)),
                  pspec(-1, 2), pspec(0, 1), pspec(1, 0), _const_spec((BLK, KW))],
        out_specs=[kvout, kvout],
        out_shape=[jax.ShapeDtypeStruct((LP, KW), BF16)] * 2, compiler_params=_params("arbitrary"),
    )(dkp, dkp, dkp, dkm, dvp, dvp, dvp, dvm)
    return dq, dk, dv, dsink


SCAN_LANES = 1024


def _scan_tile(xr, xi, cr, ci, a8, tab, seg, reverse):
    sub = lax.broadcasted_iota(jnp.int32, (8, SCAN_LANES), 0)
    for c0 in range(0, NST, SCAN_LANES):
        cs = pl.ds(c0, SCAN_LANES)
        ar = a8[0, :, cs]
        ai = a8[1, :, cs]

        def rows(j):
            jj = (seg - 1 - j) if reverse else j
            return pl.ds(jj * 8, 8)

        def step1(j, carry):
            vr, vi = carry
            rs = rows(j)
            nr = ar * vr - ai * vi + xr[rs, cs]
            ni = ar * vi + ai * vr + xi[rs, cs]
            xr[rs, cs] = nr
            xi[rs, cs] = ni
            return nr, ni

        zero = jnp.zeros((8, SCAN_LANES), F32)
        vr, vi = zero, zero
        for j in range(seg):
            vr, vi = step1(j, (vr, vi))
        for t, s in enumerate((1, 2, 4)):
            sh = (8 - s) if reverse else s
            sr = pltpu.roll(vr, sh, 0)
            si = pltpu.roll(vi, sh, 0)
            tr = tab[2 * t, :, cs]
            ti = tab[2 * t + 1, :, cs]
            vr, vi = vr + tr * sr - ti * si, vi + tr * si + ti * sr
        pr = tab[6, :, cs]
        pi = tab[7, :, cs]
        c_r = cr[:, cs]
        c_i = ci[:, cs]
        vr, vi = vr + pr * c_r - pi * c_i, vi + pr * c_i + pi * c_r
        edge = 7 if reverse else 0
        last = 0 if reverse else 7
        sh = 7 if reverse else 1
        in_r = jnp.where(sub == edge, c_r, pltpu.roll(vr, sh, 0))
        in_i = jnp.where(sub == edge, c_i, pltpu.roll(vi, sh, 0))
        cr[:, cs] = jnp.broadcast_to(vr[last:last + 1, :], (8, SCAN_LANES))
        ci[:, cs] = jnp.broadcast_to(vi[last:last + 1, :], (8, SCAN_LANES))

        def step2(j, carry):
            dr, di = carry
            rs = rows(j)
            ndr = ar * dr - ai * di
            ndi = ar * di + ai * dr
            xr[rs, cs] += ndr
            xi[rs, cs] += ndi
            return ndr, ndi

        dr, di = in_r, in_i
        for j in range(seg):
            dr, di = step2(j, (dr, di))


ST_T = 4 * SP * 2
CH_T = 128


def _load_segmented(ref, scr, seg):
    out = []
    for ct in range(4):
        scr[ct] = ref[:, ct * CH_T:(ct + 1) * CH_T]
        out.append(jnp.concatenate([scr[ct, pl.ds(j, 8, stride=seg), :] for j in range(seg)], axis=0))
    return out


def _store_segmented(ref, scr, vals, seg):
    for ct in range(4):
        for j in range(seg):
            scr[ct, pl.ds(j, 8, stride=seg), :] = vals[ct][8 * j:8 * j + 8]
        ref[:, ct * CH_T:(ct + 1) * CH_T] = scr[ct]


def ssm_dir_forward(tag, u, bpr, bpi, cpr, cpi, a8, tab, reverse, tm):
    LP = u.shape[0]
    nt = LP // tm
    seg = tm // 8

    def rix(i):
        return (nt - 1 - i) if reverse else i

    def kern(u_ref, bpr_ref, bpi_ref, cpr_ref, cpi_ref, a8_ref, tab_ref, xre_ref, xim_ref, y_ref,
             xr, xi, ys, cr, ci):
        i = pl.program_id(0)

        @pl.when(i == 0)
        def _():
            cr[...] = jnp.zeros_like(cr)
            ci[...] = jnp.zeros_like(ci)

        ub = _load_segmented(u_ref, ys, seg)
        for ct in range(4):
            uc = ub[ct].astype(BF16)
            xr[:, ct * ST_T:(ct + 1) * ST_T] = _nn(uc, bpr_ref[ct * CH_T:(ct + 1) * CH_T, :])
            xi[:, ct * ST_T:(ct + 1) * ST_T] = _nn(uc, bpi_ref[ct * CH_T:(ct + 1) * CH_T, :])
        _scan_tile(xr, xi, cr, ci, a8_ref, tab_ref, seg, reverse)
        xrb = xr[...].astype(BF16)
        xib = xi[...].astype(BF16)
        xre_ref[...] = xrb
        xim_ref[...] = xib
        yv = []
        for ct in range(4):
            ss = slice(ct * ST_T, (ct + 1) * ST_T)
            yv.append(_nn(xrb[:, ss], cpr_ref[ss, :]) - _nn(xib[:, ss], cpi_ref[ss, :]))
        _store_segmented(y_ref, ys, yv, seg)

    row = lambda w: pl.BlockSpec((tm, w), lambda i: (rix(i), 0))
    return pl.pallas_call(
        kern, name=tag, grid=(nt,),
        in_specs=[row(SW), _const_spec(bpr.shape), _const_spec(bpi.shape), _const_spec(cpr.shape),
                  _const_spec(cpi.shape), _const_spec(a8.shape), _const_spec(tab.shape)],
        out_specs=[row(NST), row(NST), row(SW)],
        out_shape=[jax.ShapeDtypeStruct((LP, NST), BF16), jax.ShapeDtypeStruct((LP, NST), BF16),
                   jax.ShapeDtypeStruct((LP, SW), F32)],
        scratch_shapes=[pltpu.VMEM((tm, NST), F32), pltpu.VMEM((tm, NST), F32), pltpu.VMEM((4, tm, CH_T), F32),
                        pltpu.VMEM((8, NST), F32), pltpu.VMEM((8, NST), F32)],
        compiler_params=_params("arbitrary"),
    )(u, bpr, bpi, cpr, cpi, a8, tab)


def ssm_dir_backward(tag, dy, xre, xim, u, bpr, bpi, cpr, cpi, a8_adj, tab_adj, reverse, tm):
    LP = u.shape[0]
    nt = LP // tm
    seg = tm // 8

    def rix(i):
        return (nt - 1 - i) if reverse else i

    def kern(dy_ref, xre_ref, xim_ref, u_ref, bpr_ref, bpi_ref, cpr_ref, cpi_ref, a8_ref, tab_ref,
             du_ref, gbr_ref, gbi_ref, gcr_ref, gci_ref, sr_ref, si_ref, lr, li, gr, gi, dus, cr, ci):
        i = pl.program_id(0)

        @pl.when(i == 0)
        def _():
            cr[...] = jnp.zeros_like(cr)
            ci[...] = jnp.zeros_like(ci)
            for r in (gbr_ref, gbi_ref, gcr_ref, gci_ref, sr_ref, si_ref):
                r[...] = jnp.zeros_like(r)

        dyb = [v.astype(BF16) for v in _load_segmented(dy_ref, dus, seg)]
        ub = [v.astype(BF16) for v in _load_segmented(u_ref, dus, seg)]
        for ct in range(4):
            ss = slice(ct * ST_T, (ct + 1) * ST_T)
            dc = dyb[ct]
            g_re = _nt(dc, cpr_ref[ss, :])
            g_im = -_nt(dc, cpi_ref[ss, :])
            lr[:, ss] = g_re
            li[:, ss] = g_im
            gr[:, ss] = g_re
            gi[:, ss] = g_im
        _scan_tile(lr, li, cr, ci, a8_ref, tab_ref, seg, reverse)
        lam_r = lr[...]
        lam_i = li[...]
        wr = lam_r - gr[...]
        wi = lam_i - gi[...]
        xr = xre_ref[...].astype(F32)
        xi = xim_ref[...].astype(F32)
        sr_ref[...] += jnp.sum(wr * xr + wi * xi, axis=0, keepdims=True)
        si_ref[...] += jnp.sum(wi * xr - wr * xi, axis=0, keepdims=True)
        lrb = lam_r.astype(BF16)
        lib = lam_i.astype(BF16)
        xrb = xre_ref[...]
        xib = xim_ref[...]
        duv = []
        for ct in range(4):
            ss = slice(ct * ST_T, (ct + 1) * ST_T)
            cs = slice(ct * CH_T, (ct + 1) * CH_T)
            duv.append(_nt(lrb[:, ss], bpr_ref[cs, :]) + _nt(lib[:, ss], bpi_ref[cs, :]))
            gbr_ref[ss, :] += _tn(lrb[:, ss], ub[ct])
            gbi_ref[ss, :] += _tn(lib[:, ss], ub[ct])
            gcr_ref[ss, :] += _tn(xrb[:, ss], dyb[ct])
            gci_ref[ss, :] -= _tn(xib[:, ss], dyb[ct])
        _store_segmented(du_ref, dus, duv, seg)

    row = lambda w: pl.BlockSpec((tm, w), lambda i: (rix(i), 0))
    acc = _const_spec((NST, CH_T))
    vec = _const_spec((1, NST))
    return pl.pallas_call(
        kern, name=tag, grid=(nt,),
        in_specs=[row(SW), row(NST), row(NST), row(SW), _const_spec(bpr.shape), _const_spec(bpi.shape),
                  _const_spec(cpr.shape), _const_spec(cpi.shape), _const_spec(a8_adj.shape),
                  _const_spec(tab_adj.shape)],
        out_specs=[row(SW), acc, acc, acc, acc, vec, vec],
        out_shape=[jax.ShapeDtypeStruct((LP, SW), F32)] + [jax.ShapeDtypeStruct((NST, CH_T), F32)] * 4
        + [jax.ShapeDtypeStruct((1, NST), F32)] * 2,
        scratch_shapes=[pltpu.VMEM((tm, NST), F32)] * 4 + [pltpu.VMEM((4, tm, CH_T), F32)]
        + [pltpu.VMEM((8, NST), F32)] * 2,
        compiler_params=_params("arbitrary"),
    )(dy, xre, xim, u, bpr, bpi, cpr, cpi, a8_adj, tab_adj)


def _ssm_disc(lam_re, lam_im, log_dt, b_re, b_im):
    dt = jnp.exp(log_dt)[:, None]
    mag = jnp.exp(lam_re * dt)
    a_re = mag * jnp.cos(lam_im * dt)
    a_im = mag * jnp.sin(lam_im * dt)
    den = lam_re * lam_re + lam_im * lam_im
    f_re = ((a_re - 1.0) * lam_re + a_im * lam_im) / den
    f_im = (a_im * lam_re - (a_re - 1.0) * lam_im) / den
    bb_re = f_re[:, :, None] * b_re - f_im[:, :, None] * b_im
    bb_im = f_re[:, :, None] * b_im + f_im[:, :, None] * b_re
    return a_re, a_im, bb_re, bb_im


def _scan_tables(lam_re, lam_im, log_dt, conj, reverse, seg):
    dt = jnp.exp(log_dt)[:, None]
    lr = (lam_re * dt).reshape(1, NST)
    li = (lam_im * dt).reshape(1, NST) * (-1.0 if conj else 1.0)
    t = jnp.arange(8, dtype=F32)[:, None]

    def power(kk):
        mag = jnp.exp(kk * lr)
        return mag * jnp.cos(kk * li), mag * jnp.sin(kk * li)

    ones = jnp.ones((8, 1), F32)
    a8 = jnp.stack(power(ones)).astype(F32)
    tabs = []
    for s in (1, 2, 4):
        mask = (t <= 7 - s) if reverse else (t >= s)
        pr, pi = power(float(s * seg) * ones)
        tabs += [jnp.where(mask, pr, 0.0), jnp.where(mask, pi, 0.0)]
    kk = ((8.0 - t) if reverse else (t + 1.0)) * float(seg)
    pr, pi = power(kk)
    tabs += [pr, pi]
    return a8, jnp.stack(tabs).astype(F32)


def _pack_b(bb):
    t = bb.transpose(0, 2, 1).reshape(4, 8, SCH, SP)
    eye = jnp.eye(8, dtype=bb.dtype)
    return jnp.einsum('tgcp,gh->tgchp', t, eye).reshape(SW, ST_T)


def _pack_c(cc):
    t = cc.transpose(0, 2, 1).reshape(4, 8, SP, SCH)
    eye = jnp.eye(8, dtype=cc.dtype)
    return jnp.einsum('tgpc,gh->tgphc', t, eye).reshape(NST, CH_T)


def _unpack_diag(acc):
    t = acc.reshape(4, 8, SP, 8, SCH)
    eye = jnp.eye(8, dtype=acc.dtype)
    return jnp.einsum('tgphc,gh->tgpc', t, eye).reshape(SGRP, SP, SCH)


def _gelu(y):
    k0 = math.sqrt(2.0 / math.pi)
    inner = k0 * (y + 0.044715 * y * y * y)
    th = jnp.tanh(inner)
    z = 0.5 * y * (1.0 + th)
    dz = 0.5 * (1.0 + th) + 0.5 * y * (1.0 - th * th) * k0 * (1.0 + 3.0 * 0.044715 * y * y)
    return z, dz


Q0, K0, V0, U0, GS0, GA0, IN_COLS = 0, 1024, 1280, 1536, 2048, 3072, 4096


def mixer_forward(tag, h, p, tm):
    g, winT, wglu, wbsT, wba, wout = p["g"], p["winT"], p["wglu"], p["wbsT"], p["wba"], p["wout"]

    def proj(i, hv, g_ref, w_ref):
        _, _, n = _rms_fwd(hv, g_ref[...])
        nb = n.astype(BF16)
        return (nb, _nt(nb, w_ref[Q0:K0, :]) * QSCALE, _nt(nb, w_ref[K0:V0, :]), _nt(nb, w_ref[V0:U0, :]),
                _nt(nb, w_ref[U0:GS0, :]), _nt(nb, w_ref[GS0:GA0, :]), _nt(nb, w_ref[GA0:IN_COLS, :]))

    n, q, k, v, u, gs, ga = rowcall(
        tag + "_proj", proj, [h], [g, winT],
        [(D, BF16), (D, BF16), (N_KV * HD, BF16), (N_KV * HD, BF16), (SW, F32), (D, F32), (D, F32)], tm=tm)

    ya = attention_forward(tag, q, k, v, p["sink"])

    states, ydir = [], []
    for dr in range(2):
        s = p["ssm"][dr]
        xre, xim, yd = ssm_dir_forward(f"{tag}_ssm_fwd{dr}", u, s["bpr"], s["bpi"], s["cpr"], s["cpi"],
                                       s["a8"], s["tab"], dr == 1, tm)
        states.append((xre, xim))
        ydir.append(yd)

    def glu(i, y0, y1, uv, d_ref, w_ref):
        ypre = y0 + y1 + d_ref[...] * uv
        z, _ = _gelu(ypre)
        zb = z.astype(BF16)
        t = _nn(zb, w_ref[...])
        return ypre, zb, t, z * _sig(t)

    ypre, zb, t, ys = rowcall(tag + "_glu", glu, [ydir[0], ydir[1], u], [p["d"], wglu],
                              [(SW, F32), (SW, BF16), (SW, F32), (SW, BF16)], tm=tm)

    def merge(i, ysv, yav, gsv, gav, wbs_ref, wba_ref):
        bs = _nt(ysv, wbs_ref[...])
        ba = _nn(yav, wba_ref[...])
        mg = _sig(gsv) * bs + _sig(gav) * ba
        mg = jnp.where(_row_ok(i, tm), mg, 0.0)
        return bs, ba, mg

    bs, ba, mg = rowcall(tag + "_merge", merge, [ys, ya, gs, ga], [wbsT, wba],
                         [(D, BF16), (D, BF16), (D, BF16)], tm=tm)

    def outp(i, mv, hv, w_ref):
        return (hv + _nn(mv, w_ref[...]),)

    (h2,) = rowcall(tag + "_out", outp, [mg, h], [wout], [(D, F32)], tm=tm)
    saved = dict(h=h, n=n, q=q, k=k, v=v, u=u, gs=gs, ga=ga, ya=ya, states=states, ypre=ypre, zb=zb, t=t,
                 ys=ys, bs=bs, ba=ba, mg=mg)
    return h2, saved


def mixer_backward(tag, dh, sv, p, tm):
    g, winT, wglu, wbsT, wba, wout = p["g"], p["winT"], p["wglu"], p["wbsT"], p["wba"], p["wout"]

    def y1(i, dhv, bsv, bav, gsv, gav, w_ref):
        dhb = dhv.astype(BF16)
        dmg = _nt(dhb, w_ref[...])
        dmg = jnp.where(_row_ok(i, tm), dmg, 0.0)
        sgs = _sig(gsv)
        sga = _sig(gav)
        return (dmg * sgs, dmg * sga, dmg * bsv.astype(F32) * sgs * (1.0 - sgs),
                dmg * bav.astype(F32) * sga * (1.0 - sga), dhb)

    dbs, dba, dgs, dga, dhb = rowcall(tag + "_bwd_merge", y1, [dh, sv["bs"], sv["ba"], sv["gs"], sv["ga"]], [wout],
                                      [(D, BF16)] * 5, tm=tm)
    dwout = tn_matmul(tag + "_dwout", sv["mg"], dhb)

    def y2(i, dbsv, dbav, wbs_ref, wba_ref):
        return _nn(dbsv, wbs_ref[...]), _nt(dbav, wba_ref[...])

    dys, dya = rowcall(tag + "_bwd_branch", y2, [dbs, dba], [wbsT, wba], [(SW, F32), (D, BF16)], tm=tm)
    dwbsT = tn_matmul(tag + "_dwbs", dbs, sv["ys"])
    dwba = tn_matmul(tag + "_dwba", sv["ya"], dba)

    def s2b(i, dysv, ypv, tv, uv, d_ref, w_ref):
        z, dz_dy = _gelu(ypv)
        st = _sig(tv)
        dt_ = dysv * z * st * (1.0 - st)
        dz = dysv * st + _nt(dt_.astype(BF16), w_ref[...])
        dyp = dz * dz_dy
        return dyp, dyp * d_ref[...], dt_, jnp.sum(dyp * uv, axis=0, keepdims=True)

    dypb, du0, dtb, dd = rowcall(tag + "_bwd_glu", s2b, [dys, sv["ypre"], sv["t"], sv["u"]], [p["d"], wglu],
                                 [(SW, F32), (SW, F32), (SW, BF16)], [(1, SW)], tm=tm)
    dwglu = tn_matmul(tag + "_dwglu", sv["zb"], dtb)

    du_dirs, ssm_sums = [], []
    for dr in range(2):
        s = p["ssm"][dr]
        xre, xim = sv["states"][dr]
        res = ssm_dir_backward(f"{tag}_ssm_bwd{dr}", dypb, xre, xim, sv["u"], s["bpr"], s["bpi"], s["cpr"],
                               s["cpi"], s["a8_adj"], s["tab_adj"], dr == 0, tm)
        du_dirs.append(res[0])
        ssm_sums.append(res[1:])

    dq, dk, dv, dsink = attention_backward(tag, sv["q"], sv["k"], sv["v"], dya, p["sink"])

    def x1b(i, dqv, dkv, dvv, du0v, du1v, du2v, dgsv, dgav, hv, dhv, g_ref, w_ref):
        dub = (du0v + du1v + du2v).astype(BF16)
        dn = (_nn(dqv, w_ref[Q0:K0, :]) + _nn(dkv, w_ref[K0:V0, :]) + _nn(dvv, w_ref[V0:U0, :])
              + _nn(dub, w_ref[U0:GS0, :]) + _nn(dgsv, w_ref[GS0:GA0, :]) + _nn(dgav, w_ref[GA0:IN_COLS, :]))
        gv = g_ref[...]
        hh, r, _ = _rms_fwd(hv, gv)
        dx, dg = _rms_bwd(hh, r, gv, dn)
        dx = jnp.where(_row_ok(i, tm), dx, 0.0)
        return dhv + dx, dub, dg

    dh2, dub, dg = rowcall(tag + "_bwd_in", x1b,
                           [dq, dk, dv, du0, du_dirs[0], du_dirs[1], dgs, dga, sv["h"], dh], [g, winT],
                           [(D, F32), (SW, BF16)], [(1, D)], tm=tm)
    n = sv["n"]
    dwinT = jnp.concatenate([tn_matmul(f"{tag}_dwin{j}", piece, n)
                             for j, piece in enumerate((dq, dk, dv, dub, dgs, dga))], axis=0)
    grads = dict(g=dg, d=dd, sink=dsink, ssm=ssm_sums, winT=dwinT, wglu=dwglu, wbsT=dwbsT, wba=dwba, wout=dwout)
    return dh2, grads


W1024 = ("f1_wgT", "f1_wuT", "f1_wd", "winT", "wba", "wout", "f2_wgT", "f2_wuT", "f2_wd")
W512 = ("wglu", "wbsT")
SMALL = ("ffn1_norm", "mix_norm", "ffn2_norm", "final_norm", "ssm_lam_re", "ssm_lam_im", "ssm_log_dt",
         "ssm_b_re", "ssm_b_im", "ssm_c_re", "ssm_c_im", "ssm_d", "attn_sink")


def kernel(x, meta_tokens, ffn1_norm, ffn1_w_gate, ffn1_w_up, ffn1_w_down, mix_norm, w_in, ssm_lam_re, ssm_lam_im, ssm_log_dt, ssm_b_re, ssm_b_im, ssm_c_re, ssm_c_im, ssm_d, ssm_w_glu, attn_sink, w_branch_ssm, w_branch_attn, w_out, ffn2_norm, ffn2_w_gate, ffn2_w_up, ffn2_w_down, final_norm, loss_target, m_meta_tokens, m_ffn1_norm, m_ffn1_w_gate, m_ffn1_w_up, m_ffn1_w_down, m_mix_norm, m_w_in, m_ssm_lam_re, m_ssm_lam_im, m_ssm_log_dt, m_ssm_b_re, m_ssm_b_im, m_ssm_c_re, m_ssm_c_im, m_ssm_d, m_ssm_w_glu, m_attn_sink, m_w_branch_ssm, m_w_branch_attn, m_w_out, m_ffn2_norm, m_ffn2_w_gate, m_ffn2_w_up, m_ffn2_w_down, m_final_norm, v_meta_tokens, v_ffn1_norm, v_ffn1_w_gate, v_ffn1_w_up, v_ffn1_w_down, v_mix_norm, v_w_in, v_ssm_lam_re, v_ssm_lam_im, v_ssm_log_dt, v_ssm_b_re, v_ssm_b_im, v_ssm_c_re, v_ssm_c_im, v_ssm_d, v_ssm_w_glu, v_attn_sink, v_w_branch_ssm, v_w_branch_attn, v_w_out, v_ffn2_norm, v_ffn2_w_gate, v_ffn2_w_up, v_ffn2_w_down, v_final_norm):
    weights = dict(meta_tokens=meta_tokens, ffn1_norm=ffn1_norm, ffn1_w_gate=ffn1_w_gate, ffn1_w_up=ffn1_w_up, ffn1_w_down=ffn1_w_down, mix_norm=mix_norm, w_in=w_in, ssm_lam_re=ssm_lam_re, ssm_lam_im=ssm_lam_im, ssm_log_dt=ssm_log_dt, ssm_b_re=ssm_b_re, ssm_b_im=ssm_b_im, ssm_c_re=ssm_c_re, ssm_c_im=ssm_c_im, ssm_d=ssm_d, ssm_w_glu=ssm_w_glu, attn_sink=attn_sink, w_branch_ssm=w_branch_ssm, w_branch_attn=w_branch_attn, w_out=w_out, ffn2_norm=ffn2_norm, ffn2_w_gate=ffn2_w_gate, ffn2_w_up=ffn2_w_up, ffn2_w_down=ffn2_w_down, final_norm=final_norm)
    mom_m = dict(meta_tokens=m_meta_tokens, ffn1_norm=m_ffn1_norm, ffn1_w_gate=m_ffn1_w_gate, ffn1_w_up=m_ffn1_w_up, ffn1_w_down=m_ffn1_w_down, mix_norm=m_mix_norm, w_in=m_w_in, ssm_lam_re=m_ssm_lam_re, ssm_lam_im=m_ssm_lam_im, ssm_log_dt=m_ssm_log_dt, ssm_b_re=m_ssm_b_re, ssm_b_im=m_ssm_b_im, ssm_c_re=m_ssm_c_re, ssm_c_im=m_ssm_c_im, ssm_d=m_ssm_d, ssm_w_glu=m_ssm_w_glu, attn_sink=m_attn_sink, w_branch_ssm=m_w_branch_ssm, w_branch_attn=m_w_branch_attn, w_out=m_w_out, ffn2_norm=m_ffn2_norm, ffn2_w_gate=m_ffn2_w_gate, ffn2_w_up=m_ffn2_w_up, ffn2_w_down=m_ffn2_w_down, final_norm=m_final_norm)
    mom_v = dict(meta_tokens=v_meta_tokens, ffn1_norm=v_ffn1_norm, ffn1_w_gate=v_ffn1_w_gate, ffn1_w_up=v_ffn1_w_up, ffn1_w_down=v_ffn1_w_down, mix_norm=v_mix_norm, w_in=v_w_in, ssm_lam_re=v_ssm_lam_re, ssm_lam_im=v_ssm_lam_im, ssm_log_dt=v_ssm_log_dt, ssm_b_re=v_ssm_b_re, ssm_b_im=v_ssm_b_im, ssm_c_re=v_ssm_c_re, ssm_c_im=v_ssm_c_im, ssm_d=v_ssm_d, ssm_w_glu=v_ssm_w_glu, attn_sink=v_attn_sink, w_branch_ssm=v_w_branch_ssm, w_branch_attn=v_w_branch_attn, w_out=v_w_out, ffn2_norm=v_ffn2_norm, ffn2_w_gate=v_ffn2_w_gate, ffn2_w_up=v_ffn2_w_up, ffn2_w_down=v_ffn2_w_down, final_norm=v_final_norm)
    names = list(weights)

    L0 = x.shape[1]
    LP = L0 + BLK
    tm = 384 if LP % 384 == 0 else BLK
    x_i, y_i, c_i = lax.axis_index("x"), lax.axis_index("y"), lax.axis_index("c")
    me = 4 * x_i + 2 * y_i + c_i

    def canon(l):
        return dict(
            f1_wgT=ffn1_w_gate[l].T, f1_wuT=ffn1_w_up[l].T, f1_wd=ffn1_w_down[l],
            winT=w_in[l].T, wba=w_branch_attn[l], wout=w_out[l],
            f2_wgT=ffn2_w_gate[l].T, f2_wuT=ffn2_w_up[l].T, f2_wd=ffn2_w_down[l],
            wglu=ssm_w_glu[l], wbsT=w_branch_ssm[l].T)

    shards = [{nm: a.astype(BF16) for nm, a in canon(l).items()} for l in range(DEPTH)]

    def pieces_of(names_):
        out, off = [], 0
        for nm in names_:
            r = shards[0][nm].shape[0]
            out.append((off, r))
            off += r
        return out

    p1_pieces, p2_pieces = pieces_of(W1024), pieces_of(W512)
    packed = [(jnp.concatenate([shards[l][nm] for nm in W1024], axis=0),
               jnp.concatenate([shards[l][nm] for nm in W512], axis=0)) for l in range(DEPTH)]
    g1, g2, gm = all_gather_pieces(
        "gather_weights_first", [(packed[0][0], p1_pieces), (packed[0][1], p2_pieces), (meta_tokens, [(0, N_META)])])
    full = [dict(zip(W1024 + W512, list(g1) + list(g2)))] + [None] * (DEPTH - 1)
    meta_full = gm[0].reshape(NDEV, N_META, D // NDEV).transpose(1, 0, 2).reshape(N_META, D)
    gather_handles = [None] + [gather_layer_start(f"gather_start_l{l}", list(packed[l]), [p1_pieces, p2_pieces])
                               for l in range(1, DEPTH)]
    started = sum(hd[3][0, 0] for hd in gather_handles[1:])

    def finish_gather(l, after):
        dests = gather_layer_wait(f"gather_wait_l{l}", gather_handles[l], 2, after)
        out = {}
        for nm, dest in zip(W1024 + W512, dests):
            sh = shards[l][nm]
            out[nm] = lax.dynamic_update_slice(dest, sh, (me * sh.shape[0], 0))
        return out

    def disc_all(lre, lim, ldt, bre, bim):
        return _ssm_disc(lre, lim, ldt, bre, bim)

    ssm_p, ssm_vjp = [], []
    for l in range(DEPTH):
        row, vrow = [], []
        for dr in range(2):
            args = (ssm_lam_re[l, dr], ssm_lam_im[l, dr], ssm_log_dt[l, dr], ssm_b_re[l, dr], ssm_b_im[l, dr])
            (a_re, a_im, bb_re, bb_im), vjp = jax.vjp(disc_all, *args)
            a8, tab = _scan_tables(args[0], args[1], args[2], False, dr == 1, tm // 8)
            a8_adj, tab_adj = _scan_tables(args[0], args[1], args[2], True, dr == 0, tm // 8)
            row.append(dict(
                bpr=_pack_b(bb_re).astype(BF16), bpi=_pack_b(bb_im).astype(BF16),
                cpr=_pack_c(ssm_c_re[l, dr]).astype(BF16), cpi=_pack_c(ssm_c_im[l, dr]).astype(BF16),
                a8=a8, tab=tab, a8_adj=a8_adj, tab_adj=tab_adj, a_re=a_re, a_im=a_im))
            vrow.append(vjp)
        ssm_p.append(row)
        ssm_vjp.append(vrow)

    blk0 = jnp.concatenate([jnp.zeros((PAD, D), F32), meta_full.astype(F32)], axis=0)
    h = build_h0(x[0], blk0)
    saved = []
    for l in range(DEPTH):
        if l > 0:
            full[l] = finish_gather(l, h)
        w = full[l]
        g1n, g2n = ffn1_norm[l][None, :], ffn2_norm[l][None, :]
        if l == 0:
            g1n = g1n + started
        mp = dict(g=mix_norm[l][None, :], winT=w["winT"], wglu=w["wglu"], wbsT=w["wbsT"], wba=w["wba"],
                  wout=w["wout"], d=ssm_d[l][None, :], sink=attn_sink[l], ssm=ssm_p[l])
        h, s1 = ffn_forward("ffn1", h, g1n, w["f1_wgT"], w["f1_wuT"], w["f1_wd"], tm)
        h, s2 = mixer_forward("mix", h, mp, tm)
        h, s3 = ffn_forward("ffn2", h, g2n, w["f2_wgT"], w["f2_wuT"], w["f2_wd"], tm)
        saved.append((s1, s2, s3, mp, g1n, g2n))

    dh, loss_acc, dgf = final_loss(h, loss_target[0], final_norm[None, :])
    loss = lax.psum(loss_acc[0, 0], MESH_AXES)

    big_grads = [None] * DEPTH
    small = {nm: [None] * DEPTH for nm in SMALL if nm != "final_norm"}
    scatter_handles = [None] * DEPTH
    own_rows = [None] * DEPTH
    sent = jnp.zeros((), F32)
    for l in reversed(range(DEPTH)):
        s1, s2, s3, mp, g1n, g2n = saved[l]
        w = full[l]
        dh, dg2, f2g, f2u, f2d = ffn_backward("ffn2", dh, s3, g2n + sent, w["f2_wgT"], w["f2_wuT"], w["f2_wd"], tm)
        dh, mg = mixer_backward("mix", dh, s2, mp, tm)
        dh, dg1, f1g, f1u, f1d = ffn_backward("ffn1", dh, s1, g1n, w["f1_wgT"], w["f1_wuT"], w["f1_wd"], tm)
        big_grads[l] = dict(f1_wgT=f1g, f1_wuT=f1u, f1_wd=f1d, winT=mg["winT"], wba=mg["wba"], wout=mg["wout"],
                            f2_wgT=f2g, f2_wuT=f2u, f2_wd=f2d, wglu=mg["wglu"], wbsT=mg["wbsT"])
        own_rows[l] = [jnp.concatenate(
            [lax.dynamic_slice_in_dim(big_grads[l][nm], me * r, r, axis=0) for nm, (_, r) in zip(names_, pcs)],
            axis=0) for names_, pcs in ((W1024, p1_pieces), (W512, p2_pieces))]
        scatter_handles[l] = scatter_layer_start(
            f"scatter_start_l{l}", [[big_grads[l][nm] for nm in W1024], [big_grads[l][nm] for nm in W512]])
        sent = scatter_handles[l][0][3][0, 0]
        small["ffn1_norm"][l] = dg1[0]
        small["mix_norm"][l] = mg["g"][0]
        small["ffn2_norm"][l] = dg2[0]
        small["ssm_d"][l] = mg["d"][0]
        small["attn_sink"][l] = mg["sink"][0, :N_HEADS]
        per_dir = {k: [] for k in ("ssm_lam_re", "ssm_lam_im", "ssm_log_dt", "ssm_b_re", "ssm_b_im",
                                   "ssm_c_re", "ssm_c_im")}
        for dr in range(2):
            gbr, gbi, gcr, gci, s_re, s_im = mg["ssm"][dr]
            a_re, a_im = ssm_p[l][dr]["a_re"], ssm_p[l][dr]["a_im"]
            s_re = s_re.reshape(SGRP, SP)
            s_im = s_im.reshape(SGRP, SP)
            den = a_re * a_re + a_im * a_im
            ga_re = (s_re * a_re - s_im * a_im) / den
            ga_im = (s_re * a_im + s_im * a_re) / den
            glr, gli, gld, gbre, gbim = ssm_vjp[l][dr]((ga_re, ga_im, _unpack_diag(gbr), _unpack_diag(gbi)))
            per_dir["ssm_lam_re"].append(glr)
            per_dir["ssm_lam_im"].append(gli)
            per_dir["ssm_log_dt"].append(gld)
            per_dir["ssm_b_re"].append(gbre)
            per_dir["ssm_b_im"].append(gbim)
            per_dir["ssm_c_re"].append(_unpack_diag(gcr).transpose(0, 2, 1))
            per_dir["ssm_c_im"].append(_unpack_diag(gci).transpose(0, 2, 1))
        for k, vlist in per_dir.items():
            small[k][l] = jnp.stack(vlist)

    grad_x = dh[BLK:][None]
    dmeta_part = dh[PAD:BLK]

    small_part = {k: jnp.stack(vv) for k, vv in small.items()}
    small_part["final_norm"] = dgf[0]
    pieces = [small_part[k].reshape(-1) for k in SMALL] + [dmeta_part.reshape(-1)]
    sizes = [p_.shape[0] for p_ in pieces]
    total = sum(sizes)
    rows_s = -(-total // (8 * D)) * 8
    flat = jnp.concatenate(pieces + [jnp.zeros((rows_s * D - total,), F32)]).reshape(rows_s, D)
    ((gathered,),) = all_gather_pieces("gather_small_grads", [(flat, [(0, rows_s)])])
    small_sum = sum_slots("sum_small_grads", gathered.reshape(NDEV, rows_s, D)).reshape(-1)
    grads = {}
    o = 0
    for k, sz in zip(SMALL, sizes[:-1]):
        grads[k] = small_sum[o:o + sz].reshape(weights[k].shape)
        o += sz
    dmeta_full = small_sum[o:o + N_META * D].reshape(N_META, D)
    grads["meta_tokens"] = lax.dynamic_slice_in_dim(dmeta_full, me * (D // NDEV), D // NDEV, axis=1)

    own = [dict() for _ in range(DEPTH)]
    for l in reversed(range(DEPTH)):
        handle, nin = scatter_handles[l]
        lands = scatter_layer_wait(f"scatter_wait_l{l}", handle, nin, 2, dh)
        for land, mine, names_, pcs, tag in ((lands[0], own_rows[l][0], W1024, p1_pieces, "1024"),
                                             (lands[1], own_rows[l][1], W512, p2_pieces, "512")):
            land = lax.dynamic_update_slice(land, mine[None], (me, 0, 0))
            tot = sum_slots("sum_weight_grads_" + tag, land)
            for nm, (off_, r) in zip(names_, pcs):
                own[l][nm] = tot[off_:off_ + r]

    def stack(fn):
        return jnp.stack([fn(own[l]) for l in range(DEPTH)])

    grads["ffn1_w_gate"] = stack(lambda d: d["f1_wgT"].T)
    grads["ffn1_w_up"] = stack(lambda d: d["f1_wuT"].T)
    grads["ffn1_w_down"] = stack(lambda d: d["f1_wd"])
    grads["w_in"] = stack(lambda d: d["winT"].T)
    grads["ssm_w_glu"] = stack(lambda d: d["wglu"])
    grads["w_branch_ssm"] = stack(lambda d: d["wbsT"].T)
    grads["w_branch_attn"] = stack(lambda d: d["wba"])
    grads["w_out"] = stack(lambda d: d["wout"])
    grads["ffn2_w_gate"] = stack(lambda d: d["f2_wgT"].T)
    grads["ffn2_w_up"] = stack(lambda d: d["f2_wuT"].T)
    grads["ffn2_w_down"] = stack(lambda d: d["f2_wd"])

    deltas, new_m, new_v = {}, {}, {}
    for nm in names:
        deltas[nm], new_m[nm], new_v[nm] = adamw("adamw_" + nm, weights[nm], grads[nm], mom_m[nm], mom_v[nm])

    return (loss, grad_x, *[grads[n] for n in names], *[deltas[n] for n in names],
            *[new_m[n] for n in names], *[new_v[n] for n in names])
```

```python
import functools
import math

import jax
import jax.numpy as jnp
from jax import lax
from jax.experimental import pallas as pl
from jax.experimental.pallas import tpu as pltpu

F32 = jnp.float32
BF16 = jnp.bfloat16

D = 1024
DFF = 2816
N_META = 16
N_HEADS = 16
N_KV = 4
HD = 64
QG = 4
WIN = 128
BLK = 128
PAD = BLK - N_META
SW = 512
SGRP = 32
SCH = 16
SP = 64
NST = SGRP * SP
EPS = 1e-6
NEG = -1e30
SCALE = HD ** -0.5
NDEV = 8
DEPTH = 4
MESH_AXES = ("x", "y", "c")
MESH = pl.DeviceIdType.MESH

ADAM_LR = 0.001
ADAM_B1 = 0.9
ADAM_B2 = 0.999
ADAM_EPS = 1e-08
ADAM_WD = 0.01
ADAM_STEP = 10

VMEM_LIMIT = 56 * 1024 * 1024


def _params(*sem):
    return pltpu.CompilerParams(dimension_semantics=sem, vmem_limit_bytes=VMEM_LIMIT)


def _nn(a, b):
    return lax.dot_general(a, b, (((1,), (0,)), ((), ())), preferred_element_type=F32)


def _nt(a, b):
    return lax.dot_general(a, b, (((1,), (1,)), ((), ())), preferred_element_type=F32)


def _tn(a, b):
    return lax.dot_general(a, b, (((0,), (0,)), ((), ())), preferred_element_type=F32)


def _sig(x):
    return 1.0 / (1.0 + jnp.exp(-x))


def _rms_fwd(h, g):
    r = lax.rsqrt(jnp.mean(h * h, axis=-1, keepdims=True) + EPS)
    hh = h * r
    return hh, r, hh * g


def _rms_bwd(hh, r, g, dn):
    dhh = dn * g
    dx = r * (dhh - hh * jnp.mean(dhh * hh, axis=-1, keepdims=True))
    return dx, jnp.sum(dn * hh, axis=0, keepdims=True)


def _row_ok(i, tm):
    rows = i * tm + lax.broadcasted_iota(jnp.int32, (tm, 1), 0)
    return rows >= PAD


def _const_spec(shape):
    nd = len(shape)
    return pl.BlockSpec(shape, lambda *_: (0,) * nd)


def rowcall(name, body, rows, consts, outs, accs=(), *, tm):
    nrows = rows[0].shape[0]
    nt = nrows // tm
    assert nt * tm == nrows, (name, nrows, tm)
    nr, nc, no, na = len(rows), len(consts), len(outs), len(accs)
    in_specs = [pl.BlockSpec((tm, r.shape[1]), lambda i: (i, 0)) for r in rows]
    in_specs += [_const_spec(c.shape) for c in consts]
    out_shape = [jax.ShapeDtypeStruct((nrows, w), dt) for (w, dt) in outs]
    out_specs = [pl.BlockSpec((tm, w), lambda i: (i, 0)) for (w, dt) in outs]
    out_shape += [jax.ShapeDtypeStruct(s, F32) for s in accs]
    out_specs += [_const_spec(s) for s in accs]

    def kern(*refs):
        i = pl.program_id(0)
        row_vals = [r[...] for r in refs[:nr]]
        res = body(i, *row_vals, *refs[nr:nr + nc])
        out_refs = refs[nr + nc:nr + nc + no]
        acc_refs = refs[nr + nc + no:]
        for r, v in zip(out_refs, res[:no]):
            r[...] = v.astype(r.dtype)
        if na:
            @pl.when(i == 0)
            def _():
                for r in acc_refs:
                    r[...] = jnp.zeros_like(r)
            for r, v in zip(acc_refs, res[no:]):
                r[...] += v

    res = pl.pallas_call(
        kern, name=name, grid=(nt,), in_specs=in_specs, out_specs=out_specs, out_shape=out_shape,
        compiler_params=_params("arbitrary"),
    )(*rows, *consts)
    return res


def tn_matmul(name, lhs, rhs, scale=1.0):
    M, K = lhs.shape
    N = rhs.shape[1]
    assert lhs.dtype == BF16 and rhs.dtype == BF16
    nm = 6
    tmw = M // nm
    assert tmw * nm == M and tmw % 16 == 0
    tk = 1408 if (K % 1408 == 0) else K
    nk = K // tk

    def kern(a_ref, b_ref, o_ref, acc):
        m = pl.program_id(1)
        part = _tn(a_ref[...], b_ref[...])

        @pl.when(m == 0)
        def _():
            acc[...] = part

        @pl.when((m > 0) & (m < nm - 1))
        def _():
            acc[...] += part

        @pl.when(m == nm - 1)
        def _():
            o_ref[...] = ((acc[...] + part) * scale).astype(o_ref.dtype)

    return pl.pallas_call(
        kern, name=name, grid=(nk, nm),
        in_specs=[pl.BlockSpec((tmw, tk), lambda k, m: (m, k)), pl.BlockSpec((tmw, N), lambda k, m: (m, 0))],
        out_specs=pl.BlockSpec((tk, N), lambda k, m: (k, 0)),
        out_shape=jax.ShapeDtypeStruct((K, N), BF16),
        scratch_shapes=[pltpu.VMEM((tk, N), F32)],
        compiler_params=_params("arbitrary", "arbitrary"),
    )(lhs, rhs)


def _mesh_pos():
    x, y, c = lax.axis_index("x"), lax.axis_index("y"), lax.axis_index("c")
    return x, y, c


def all_gather_pieces(name, groups):
    ng = len(groups)
    packed = [g[0] for g in groups]
    pieces = [g[1] for g in groups]
    out_shape, out_map = [], []
    for gi, (p, pcs) in enumerate(groups):
        idx = []
        for (off, r) in pcs:
            idx.append(len(out_shape))
            out_shape.append(jax.ShapeDtypeStruct((NDEV * r, p.shape[1]), p.dtype))
        out_map.append(idx)
    nout = len(out_shape)

    def body(*refs):
        p_refs = refs[:ng]
        o_refs = refs[ng:ng + nout]
        send_sems, recv_sems, local_sems = refs[ng + nout:]
        x, y, c = _mesh_pos()
        me = (x, y, c)
        sibling = (x, y, 1 - c)
        chips = [(1 - x, y), (x, 1 - y), (1 - x, 1 - y)]

        def blk(px, py, pc):
            return 4 * px + 2 * py + pc

        def copies(gi, k, origin, to, from_out):
            cps = []
            for (off, r), oi in zip(pieces[gi], out_map[gi]):
                dst = o_refs[oi].at[pl.ds(origin * r, r), :]
                src = dst if from_out else p_refs[gi].at[pl.ds(off, r), :]
                cps.append(pltpu.make_async_remote_copy(
                    src_ref=src, dst_ref=dst, send_sem=send_sems.at[gi, k], recv_sem=recv_sems.at[gi, k],
                    device_id=to, device_id_type=MESH))
            return cps

        def whole(gi, k):
            return pltpu.make_async_remote_copy(
                src_ref=p_refs[gi], dst_ref=p_refs[gi], send_sem=send_sems.at[gi, k],
                recv_sem=recv_sems.at[gi, k], device_id=me, device_id_type=MESH)

        mine = []
        for gi in range(ng):
            for (off, r), oi in zip(pieces[gi], out_map[gi]):
                mine.append(pltpu.make_async_copy(
                    p_refs[gi].at[pl.ds(off, r), :], o_refs[oi].at[pl.ds(blk(*me) * r, r), :],
                    local_sems.at[gi]))
        for cp in mine:
            cp.start()
        for gi in range(ng):
            for cp in copies(gi, 0, blk(*me), sibling, False):
                cp.start()
            for j, chip in enumerate(chips):
                for cp in copies(gi, 1 + j, blk(*me), (*chip, c), False):
                    cp.start()
        for j, chip in enumerate(chips):
            for gi in range(ng):
                whole(gi, 1 + j).wait_recv()
                for cp in copies(gi, 4 + j, blk(*chip, c), sibling, True):
                    cp.start()
        for gi in range(ng):
            whole(gi, 0).wait_recv()
            for j in range(3):
                whole(gi, 4 + j).wait_recv()
        for gi in range(ng):
            for k in range(7):
                whole(gi, k).wait_send()
            pltpu.make_async_copy(p_refs[gi], p_refs[gi], local_sems.at[gi]).wait()

    any_spec = pl.BlockSpec(memory_space=pl.ANY)
    outs = pl.pallas_call(
        body, name=name, out_shape=out_shape,
        in_specs=[any_spec] * ng, out_specs=[any_spec] * nout,
        scratch_shapes=[pltpu.SemaphoreType.DMA((ng, 7)), pltpu.SemaphoreType.DMA((ng, 7)),
                        pltpu.SemaphoreType.DMA((ng,))],
    )(*packed)
    return [[outs[oi] for oi in idx] for idx in out_map]


HBM_SPEC = pl.BlockSpec(memory_space=pltpu.HBM)
SEM_SPEC = pl.BlockSpec(memory_space=pltpu.SEMAPHORE)
DATAFLOW = pltpu.SideEffectType.DATAFLOW_SIDE_EFFECTING


def _peers(x, y, c):
    return [(x, y, 1 - c), (1 - x, y, c), (x, 1 - y, c), (1 - x, 1 - y, c),
            (1 - x, y, 1 - c), (x, 1 - y, 1 - c), (1 - x, 1 - y, 1 - c)]


def exchange_start(name, arrays, ng, plan):
    n = len(arrays)
    ns = ng * 7

    def body(*refs):
        in_refs = refs[:n]
        send_sems, recv_sems = refs[n:n + ns], refs[n + ns:n + 2 * ns]
        token = refs[-1]
        x, y, c = _mesh_pos()
        me_i = 4 * x + 2 * y + c
        for k, peer in enumerate(_peers(x, y, c)):
            p_i = 4 * peer[0] + 2 * peer[1] + peer[2]
            for src, dst, gi in plan(in_refs, me_i, p_i):
                pltpu.make_async_remote_copy(
                    src_ref=src, dst_ref=dst, send_sem=send_sems[gi * 7 + k], recv_sem=recv_sems[gi * 7 + k],
                    device_id=peer, device_id_type=MESH).start()
        token[...] = jnp.zeros_like(token)

    res = pl.pallas_call(
        body, name=name,
        out_shape=(*[pltpu.SemaphoreType.DMA(())] * (2 * ns),
                   *[pltpu.HBM(a.shape, a.dtype) for a in arrays], jax.ShapeDtypeStruct((8, 128), F32)),
        in_specs=[HBM_SPEC] * n,
        out_specs=(*[SEM_SPEC] * (2 * ns), *[HBM_SPEC] * n, pl.BlockSpec(memory_space=pltpu.VMEM)),
        input_output_aliases={i: 2 * ns + i for i in range(n)},
        compiler_params=pltpu.CompilerParams(has_side_effects=DATAFLOW),
    )(*[pltpu.with_memory_space_constraint(a, pltpu.HBM) for a in arrays])
    return list(res[:ns]), list(res[ns:2 * ns]), list(res[2 * ns:2 * ns + n]), res[-1]


def exchange_wait(name, send_sems, recv_sems, arrays, ng, sized, after):
    n = len(arrays)
    ns = ng * 7

    def body(*refs):
        in_refs = refs[:n]
        s_sems, r_sems = refs[n:n + ns], refs[n + ns:n + 2 * ns]
        x, y, c = _mesh_pos()
        for gi in range(ng):
            view = sized(in_refs, gi)
            for k in range(7):
                w = pltpu.make_async_remote_copy(
                    src_ref=view, dst_ref=view, send_sem=s_sems[gi * 7 + k], recv_sem=r_sems[gi * 7 + k],
                    device_id=(x, y, c), device_id_type=MESH)
                w.wait_send()
                w.wait_recv()

    res = pl.pallas_call(
        body, name=name, out_shape=tuple(pltpu.HBM(a.shape, a.dtype) for a in arrays),
        in_specs=[HBM_SPEC] * n + [SEM_SPEC] * (2 * ns) + [pl.BlockSpec(memory_space=pl.ANY)],
        out_specs=tuple([HBM_SPEC] * n), input_output_aliases={i: i for i in range(n)},
        compiler_params=pltpu.CompilerParams(has_side_effects=DATAFLOW),
    )(*arrays, *send_sems, *recv_sems, after)
    return list(res)


def gather_layer_start(name, packed, pieces):
    ng = len(packed)
    dests = [lax.empty((NDEV * r, p.shape[1]), p.dtype) for p, pcs in zip(packed, pieces) for (_, r) in pcs]

    def plan(refs, me_i, p_i):
        out, di = [], ng
        for gi in range(ng):
            for (off, r) in pieces[gi]:
                out.append((refs[gi].at[pl.ds(off, r), :], refs[di].at[pl.ds(me_i * r, r), :], gi))
                di += 1
        return out

    return exchange_start(name, list(packed) + dests, ng, plan)


def gather_layer_wait(name, handle, ng, after):
    send_sems, recv_sems, arrays, _ = handle
    out = exchange_wait(name, send_sems, recv_sems, arrays, ng, lambda refs, gi: refs[gi], after)
    return out[ng:]


def scatter_layer_start(name, groups):
    ng = len(groups)
    flat = [a for arrs in groups for a in arrs]
    offs, lands = [], []
    for arrs in groups:
        o, off = [], 0
        for a in arrs:
            r = a.shape[0] // NDEV
            o.append((off, r))
            off += r
        offs.append(o)
        lands.append(jnp.zeros((NDEV, off, arrs[0].shape[1]), arrs[0].dtype))
    nin = len(flat)

    def plan(refs, me_i, p_i):
        out, ai = [], 0
        for gi in range(ng):
            for (off, r) in offs[gi]:
                out.append((refs[ai].at[pl.ds(p_i * r, r), :], refs[nin + gi].at[me_i, pl.ds(off, r), :], gi))
                ai += 1
        return out

    return exchange_start(name, flat + lands, ng, plan), nin


def scatter_layer_wait(name, handle, nin, ng, after):
    send_sems, recv_sems, arrays, _ = handle
    out = exchange_wait(name, send_sems, recv_sems, arrays, ng, lambda refs, gi: refs[nin + gi].at[0], after)
    return out[nin:]


def _pick_tile(n, cap):
    best = None
    for t in range(8, min(n, cap) + 1, 8):
        if n % t == 0:
            best = t
    return best if best is not None else n


def sum_slots(name, land):
    _, R, W = land.shape
    tr = _pick_tile(R, 512)

    def kern(l_ref, o_ref):
        acc = l_ref[0].astype(F32)
        for s in range(1, NDEV):
            acc = acc + l_ref[s].astype(F32)
        o_ref[...] = acc

    return pl.pallas_call(
        kern, name=name, grid=(R // tr,),
        in_specs=[pl.BlockSpec((NDEV, tr, W), lambda i: (0, i, 0))],
        out_specs=pl.BlockSpec((tr, W), lambda i: (i, 0)),
        out_shape=jax.ShapeDtypeStruct((R, W), F32),
        compiler_params=_params("arbitrary"),
    )(land)


def adamw(name, w, g, m, v):
    shp = w.shape
    C = shp[-1]
    R = max(1, math.prod(shp[:-1]))
    tr = _pick_tile(R, 1024)
    w2, g2, m2, v2 = (a.reshape(R, C) for a in (w, g, m, v))

    def kern(w_ref, g_ref, m_ref, v_ref, d_ref, mo_ref, vo_ref):
        gg = g_ref[...]
        mn = ADAM_B1 * m_ref[...] + (1.0 - ADAM_B1) * gg
        vn = ADAM_B2 * v_ref[...] + (1.0 - ADAM_B2) * jnp.square(gg)
        m_hat = mn / (1.0 - ADAM_B1 ** ADAM_STEP)
        v_hat = vn / (1.0 - ADAM_B2 ** ADAM_STEP)
        d_ref[...] = -ADAM_LR * (m_hat / (jnp.sqrt(v_hat) + ADAM_EPS) + ADAM_WD * w_ref[...])
        mo_ref[...] = mn
        vo_ref[...] = vn

    spec = pl.BlockSpec((tr, C), lambda i: (i, 0))
    d, mo, vo = pl.pallas_call(
        kern, name=name, grid=(R // tr,), in_specs=[spec] * 4, out_specs=[spec] * 3,
        out_shape=[jax.ShapeDtypeStruct((R, C), F32)] * 3, compiler_params=_params("arbitrary"),
    )(w2, g2, m2, v2)
    return d.reshape(shp), mo.reshape(shp), vo.reshape(shp)


def build_h0(x2, blk0):
    L0 = x2.shape[0]
    nb = L0 // BLK + 1

    def kern(x_ref, b_ref, o_ref):
        i = pl.program_id(0)

        @pl.when(i == 0)
        def _():
            o_ref[...] = b_ref[...]

        @pl.when(i > 0)
        def _():
            o_ref[...] = x_ref[...]

    return pl.pallas_call(
        kern, name="build_h0", grid=(nb,),
        in_specs=[pl.BlockSpec((BLK, D), lambda i: (jnp.maximum(i - 1, 0), 0)), _const_spec((BLK, D))],
        out_specs=pl.BlockSpec((BLK, D), lambda i: (i, 0)),
        out_shape=jax.ShapeDtypeStruct((L0 + BLK, D), F32), compiler_params=_params("arbitrary"),
    )(x2, blk0)


def final_loss(h, tgt, gf):
    LP = h.shape[0]
    nb = LP // BLK

    def kern(h_ref, t_ref, g_ref, dh_ref, loss_ref, dg_ref):
        i = pl.program_id(0)

        @pl.when(i == 0)
        def _():
            loss_ref[...] = jnp.zeros_like(loss_ref)
            dg_ref[...] = jnp.zeros_like(dg_ref)

        g = g_ref[...]
        hh, r, yv = _rms_fwd(h_ref[...], g)
        valid = (i > 0).astype(F32)
        err = (yv - t_ref[...]) * valid
        loss_ref[...] += 0.5 * jnp.sum(jnp.sum(err * err, axis=1, keepdims=True), axis=0, keepdims=True) / D
        dy = err / D
        dx, dg = _rms_bwd(hh, r, g, dy)
        dh_ref[...] = dx
        dg_ref[...] += dg

    return pl.pallas_call(
        kern, name="final_loss", grid=(nb,),
        in_specs=[pl.BlockSpec((BLK, D), lambda i: (i, 0)),
                  pl.BlockSpec((BLK, D), lambda i: (jnp.maximum(i - 1, 0), 0)), _const_spec((1, D))],
        out_specs=[pl.BlockSpec((BLK, D), lambda i: (i, 0)), _const_spec((8, 128)), _const_spec((1, D))],
        out_shape=[jax.ShapeDtypeStruct((LP, D), F32), jax.ShapeDtypeStruct((8, 128), F32),
                   jax.ShapeDtypeStruct((1, D), F32)],
        compiler_params=_params("arbitrary"),
    )(h, tgt, gf)


def ffn_forward(tag, h, g, wgT, wuT, wd, tm):
    def f1(i, hv, g_ref, wg_ref, wu_ref):
        _, _, n = _rms_fwd(hv, g_ref[...])
        nb = n.astype(BF16)
        G = _nt(nb, wg_ref[...])
        U = _nt(nb, wu_ref[...])
        A = G * _sig(G) * U
        return nb, G, U, A

    n, G, U, A = rowcall(tag + "_up", f1, [h], [g, wgT, wuT],
                         [(D, BF16), (DFF, BF16), (DFF, BF16), (DFF, BF16)], tm=tm)

    def f2(i, av, hv, wd_ref):
        return (hv + 0.5 * _nn(av, wd_ref[...]),)

    (h2,) = rowcall(tag + "_down", f2, [A, h], [wd], [(D, F32)], tm=tm)
    return h2, (h, n, G, U, A)


def ffn_backward(tag, dh, saved, g, wgT, wuT, wd, tm):
    h, n, G, U, A = saved

    def b1(i, dhv, Gv, Uv, wd_ref):
        dyb = (0.5 * dhv).astype(BF16)
        dA = _nt(dyb, wd_ref[...])
        Gf = Gv.astype(F32)
        sg = _sig(Gf)
        dG = dA * Uv.astype(F32) * (sg * (1.0 + Gf * (1.0 - sg)))
        dU = dA * (Gf * sg)
        return dG, dU, dyb

    dG, dU, dyb = rowcall(tag + "_bwd_act", b1, [dh, G, U], [wd], [(DFF, BF16), (DFF, BF16), (D, BF16)], tm=tm)

    def b2(i, dGv, dUv, hv, dhv, g_ref, wg_ref, wu_ref):
        dn = _nn(dGv, wg_ref[...]) + _nn(dUv, wu_ref[...])
        gv = g_ref[...]
        hh, r, _ = _rms_fwd(hv, gv)
        dx, dg = _rms_bwd(hh, r, gv, dn)
        dx = jnp.where(_row_ok(i, tm), dx, 0.0)
        return dhv + dx, dg

    dh2, dg = rowcall(tag + "_bwd_in", b2, [dG, dU, h, dh], [g, wgT, wuT], [(D, F32)], [(1, D)], tm=tm)
    dwd = tn_matmul(tag + "_dwd", A, dyb)
    dwgT = tn_matmul(tag + "_dwg", dG, n)
    dwuT = tn_matmul(tag + "_dwu", dU, n)
    return dh2, dg, dwgT, dwuT, dwd


def _alibi_slope(head):
    return float(2.0 ** (-8.0 * (head + 1) / N_HEADS))


def _att_bias(n, nb):
    qi = lax.broadcasted_iota(jnp.int32, (BLK, 4 * BLK), 0)
    cj = lax.broadcasted_iota(jnp.int32, (BLK, 4 * BLK), 1)
    jb = cj - BLK
    dist = jnp.abs(qi + BLK - jb)
    kpos = (n - 1) * BLK + jb
    band_ok = (dist <= WIN) & (kpos >= BLK) & (kpos < nb * BLK)
    is_meta = cj < BLK
    ok = (is_meta & (cj >= PAD)) | (jnp.logical_not(is_meta) & band_ok)
    distf = jnp.where(is_meta, 0, dist).astype(F32)
    maskadd = jnp.where(ok, 0.0, NEG).astype(F32)
    distf4 = jnp.concatenate([distf] * QG, axis=0)
    mask4 = jnp.concatenate([maskadd] * QG, axis=0)
    return distf4, mask4


def _group_col(vals):
    rg = lax.broadcasted_iota(jnp.int32, (QG * BLK, 1), 0) // BLK
    col = jnp.full((QG * BLK, 1), vals[QG - 1], F32)
    for gq in range(QG - 2, -1, -1):
        col = jnp.where(rg == gq, vals[gq], col)
    return col


def _stack_heads(ref_or_val, kh):
    return jnp.concatenate(
        [ref_or_val[:, (kh * QG + gq) * HD:(kh * QG + gq + 1) * HD] for gq in range(QG)], axis=0)


def _stack_keys(km, kp, kc, kn, kh):
    sl = slice(kh * HD, (kh + 1) * HD)
    return jnp.concatenate([km[:, sl], kp[:, sl], kc[:, sl], kn[:, sl]], axis=0)


LOG2E = 1.4426950408889634
LN2 = 0.6931471805599453
QSCALE = SCALE * LOG2E


def _att_update_bias(bias_ref, n, nb):
    @pl.when((n <= 2) | (n == nb - 1))
    def _():
        distf4, mask4 = _att_bias(n, nb)
        for kh in range(N_KV):
            slope_col = _group_col([_alibi_slope(kh * QG + gq) * LOG2E for gq in range(QG)])
            bias_ref[kh] = mask4 - slope_col * distf4


def _att_exp(qs, kb, bias_ref, kh, sink_ref):
    sink_col = _group_col([sink_ref[kh * QG + gq] for gq in range(QG)]) * LOG2E
    s = _nt(qs, kb) + bias_ref[kh]
    m = jnp.maximum(jnp.max(s, axis=1, keepdims=True), sink_col)
    e = jnp.exp2(s - m)
    es = jnp.exp2(sink_col - m)
    inv = 1.0 / (jnp.sum(e, axis=1, keepdims=True) + es)
    return e, es, inv


def attention_forward(tag, q, k, v, sink):
    LP = q.shape[0]
    nb = LP // BLK

    def kern(sink_ref, q_ref, km_ref, kp_ref, kc_ref, kn_ref, vm_ref, vp_ref, vc_ref, vn_ref, o_ref, bias_ref):
        n = pl.program_id(0)
        _att_update_bias(bias_ref, n, nb)
        qv = q_ref[...]
        km, kp, kc, kn = km_ref[...], kp_ref[...], kc_ref[...], kn_ref[...]
        vm, vp, vc, vn = vm_ref[...], vp_ref[...], vc_ref[...], vn_ref[...]
        for kh in range(N_KV):
            qs = _stack_heads(qv, kh)
            kb = _stack_keys(km, kp, kc, kn, kh)
            vb = _stack_keys(vm, vp, vc, vn, kh)
            e, _, inv = _att_exp(qs, kb, bias_ref, kh, sink_ref)
            o = _nn(e.astype(BF16), vb) * inv
            for gq in range(QG):
                hcol = (kh * QG + gq) * HD
                o_ref[:, hcol:hcol + HD] = o[gq * BLK:(gq + 1) * BLK].astype(o_ref.dtype)

    def kvspec(dn):
        return pl.BlockSpec((BLK, N_KV * HD), lambda n: (jnp.clip(n + dn, 0, nb - 1), 0))

    meta_spec = pl.BlockSpec((BLK, N_KV * HD), lambda n: (0, 0))
    return pl.pallas_call(
        kern, name=tag + "_att_fwd", grid=(nb,),
        in_specs=[pl.BlockSpec(memory_space=pltpu.SMEM), pl.BlockSpec((BLK, D), lambda n: (n, 0)),
                  meta_spec, kvspec(-1), kvspec(0), kvspec(1), meta_spec, kvspec(-1), kvspec(0), kvspec(1)],
        out_specs=pl.BlockSpec((BLK, D), lambda n: (n, 0)),
        out_shape=jax.ShapeDtypeStruct((LP, D), BF16),
        scratch_shapes=[pltpu.VMEM((N_KV, QG * BLK, 4 * BLK), F32)], compiler_params=_params("arbitrary"),
    )(sink, q, k, k, k, k, v, v, v, v)


def attention_backward(tag, q, k, v, do, sink):
    LP = q.shape[0]
    nb = LP // BLK
    KW = N_KV * HD

    def kern(sink_ref, q_ref, do_ref, km_ref, kp_ref, kc_ref, kn_ref, vm_ref, vp_ref, vc_ref, vn_ref,
             dq_ref, dkp_ref, dvp_ref, dkm_ref, dvm_ref, dsink_ref, bias_ref):
        n = pl.program_id(0)

        @pl.when(n == 0)
        def _():
            dkm_ref[...] = jnp.zeros_like(dkm_ref)
            dvm_ref[...] = jnp.zeros_like(dvm_ref)
            dsink_ref[...] = jnp.zeros_like(dsink_ref)

        _att_update_bias(bias_ref, n, nb)
        qv, dov = q_ref[...], do_ref[...]
        km, kp, kc, kn = km_ref[...], kp_ref[...], kc_ref[...], kn_ref[...]
        vm, vp, vc, vn = vm_ref[...], vp_ref[...], vc_ref[...], vn_ref[...]
        lane = lax.broadcasted_iota(jnp.int32, (8, 128), 1)
        dsink = jnp.zeros((8, 128), F32)
        for kh in range(N_KV):
            qs = _stack_heads(qv, kh)
            dos = _stack_heads(dov, kh)
            kb = _stack_keys(km, kp, kc, kn, kh)
            vb = _stack_keys(vm, vp, vc, vn, kh)
            dp = _nt(dos, vb)
            e, es, inv = _att_exp(qs, kb, bias_ref, kh, sink_ref)
            delta = inv * jnp.sum(e * dp, axis=1, keepdims=True)
            ds = (e * ((dp - delta) * inv)).astype(BF16)
            dqs = _nn(ds, kb) * SCALE
            dkt = _tn(qs, ds) * LN2
            dvt = _tn((dos.astype(F32) * inv).astype(BF16), e.astype(BF16))
            dsk = -(es * inv * delta)
            for gq in range(QG):
                hcol = (kh * QG + gq) * HD
                dq_ref[:, hcol:hcol + HD] = dqs[gq * BLK:(gq + 1) * BLK].astype(dq_ref.dtype)
                tot = jnp.sum(dsk[gq * BLK:(gq + 1) * BLK], axis=0, keepdims=True)
                dsink = dsink + jnp.where(lane == kh * QG + gq, tot, 0.0)
            hs = slice(kh * HD, (kh + 1) * HD)
            dkm_ref[hs, :] += dkt[:, 0:BLK]
            dvm_ref[hs, :] += dvt[:, 0:BLK]
            for slot in range(3):
                dkp_ref[0, slot, hs, :] = dkt[:, (slot + 1) * BLK:(slot + 2) * BLK]
                dvp_ref[0, slot, hs, :] = dvt[:, (slot + 1) * BLK:(slot + 2) * BLK]
        dsink_ref[...] += dsink

    def kvspec(dn):
        return pl.BlockSpec((BLK, KW), lambda n: (jnp.clip(n + dn, 0, nb - 1), 0))

    meta_spec = pl.BlockSpec((BLK, KW), lambda n: (0, 0))
    rowspec = pl.BlockSpec((BLK, D), lambda n: (n, 0))
    part_spec = pl.BlockSpec((1, 3, KW, BLK), lambda n: (n, 0, 0, 0))
    dq, dkp, dvp, dkm, dvm, dsink = pl.pallas_call(
        kern, name=tag + "_att_bwd", grid=(nb,),
        in_specs=[pl.BlockSpec(memory_space=pltpu.SMEM), rowspec, rowspec,
                  meta_spec, kvspec(-1), kvspec(0), kvspec(1), meta_spec, kvspec(-1), kvspec(0), kvspec(1)],
        out_specs=[rowspec, part_spec, part_spec, _const_spec((KW, BLK)), _const_spec((KW, BLK)),
                   _const_spec((8, 128))],
        out_shape=[jax.ShapeDtypeStruct((LP, D), BF16), jax.ShapeDtypeStruct((nb, 3, KW, BLK), F32),
                   jax.ShapeDtypeStruct((nb, 3, KW, BLK), F32), jax.ShapeDtypeStruct((KW, BLK), F32),
                   jax.ShapeDtypeStruct((KW, BLK), F32), jax.ShapeDtypeStruct((8, 128), F32)],
        scratch_shapes=[pltpu.VMEM((N_KV, QG * BLK, 4 * BLK), F32)], compiler_params=_params("arbitrary"),
    )(sink, q, do, k, k, k, k, v, v, v, v)

    def comb(a_ref, b_ref, c_ref, m_ref, a2_ref, b2_ref, c2_ref, m2_ref, dk_ref, dv_ref):
        mblk = pl.program_id(0)
        has_prev = (mblk > 0).astype(F32)
        has_next = (mblk < nb - 1).astype(F32)
        is0 = (mblk == 0).astype(F32)
        dkt = a_ref[0, 0] * has_prev + b_ref[0, 0] + c_ref[0, 0] * has_next + m_ref[...] * is0
        dvt = a2_ref[0, 0] * has_prev + b2_ref[0, 0] + c2_ref[0, 0] * has_next + m2_ref[...] * is0
        dk_ref[...] = dkt.T.astype(dk_ref.dtype)
        dv_ref[...] = dvt.T.astype(dv_ref.dtype)

    def pspec(dn, slot):
        return pl.BlockSpec((1, 1, KW, BLK), lambda m: (jnp.clip(m + dn, 0, nb - 1), slot, 0, 0))

    kvout = pl.BlockSpec((BLK, KW), lambda m: (m, 0))
    dk, dv = pl.pallas_call(
        comb, name=tag + "_att_dkv", grid=(nb,),
        in_specs=[pspec(-1, 2), pspec(0, 1), pspec(1, 0), _const_spec((KW, BLK)),
                  pspec(-1, 2), pspec(0, 1), pspec(1, 0), _const_spec((KW, BLK))],
        out_specs=[kvout, kvout],
        out_shape=[jax.ShapeDtypeStruct((LP, KW), BF16)] * 2, compiler_params=_params("arbitrary"),
    )(dkp, dkp, dkp, dkm, dvp, dvp, dvp, dvm)
    return dq, dk, dv, dsink


SCAN_LANES = 1024


def _scan_tile(xr, xi, cr, ci, a8, tab, seg, reverse):
    sub = lax.broadcasted_iota(jnp.int32, (8, SCAN_LANES), 0)
    for c0 in range(0, NST, SCAN_LANES):
        cs = pl.ds(c0, SCAN_LANES)
        ar = a8[0, :, cs]
        ai = a8[1, :, cs]

        def rows(j):
            jj = (seg - 1 - j) if reverse else j
            return pl.ds(jj * 8, 8)

        def step1(j, carry):
            vr, vi = carry
            rs = rows(j)
            nr = ar * vr - ai * vi + xr[rs, cs]
            ni = ar * vi + ai * vr + xi[rs, cs]
            xr[rs, cs] = nr
            xi[rs, cs] = ni
            return nr, ni

        zero = jnp.zeros((8, SCAN_LANES), F32)
        vr, vi = zero, zero
        for j in range(seg):
            vr, vi = step1(j, (vr, vi))
        for t, s in enumerate((1, 2, 4)):
            sh = (8 - s) if reverse else s
            sr = pltpu.roll(vr, sh, 0)
            si = pltpu.roll(vi, sh, 0)
            tr = tab[2 * t, :, cs]
            ti = tab[2 * t + 1, :, cs]
            vr, vi = vr + tr * sr - ti * si, vi + tr * si + ti * sr
        pr = tab[6, :, cs]
        pi = tab[7, :, cs]
        c_r = cr[:, cs]
        c_i = ci[:, cs]
        vr, vi = vr + pr * c_r - pi * c_i, vi + pr * c_i + pi * c_r
        edge = 7 if reverse else 0
        last = 0 if reverse else 7
        sh = 7 if reverse else 1
        in_r = jnp.where(sub == edge, c_r, pltpu.roll(vr, sh, 0))
        in_i = jnp.where(sub == edge, c_i, pltpu.roll(vi, sh, 0))
        cr[:, cs] = jnp.broadcast_to(vr[last:last + 1, :], (8, SCAN_LANES))
        ci[:, cs] = jnp.broadcast_to(vi[last:last + 1, :], (8, SCAN_LANES))

        def step2(j, carry):
            dr, di = carry
            rs = rows(j)
            ndr = ar * dr - ai * di
            ndi = ar * di + ai * dr
            xr[rs, cs] += ndr
            xi[rs, cs] += ndi
            return ndr, ndi

        dr, di = in_r, in_i
        for j in range(seg):
            dr, di = step2(j, (dr, di))


ST_T = 4 * SP * 2
CH_T = 128


def _load_segmented(ref, scr, seg):
    out = []
    for ct in range(4):
        scr[ct] = ref[:, ct * CH_T:(ct + 1) * CH_T]
        out.append(jnp.concatenate([scr[ct, pl.ds(j, 8, stride=seg), :] for j in range(seg)], axis=0))
    return out


def _store_segmented(ref, scr, vals, seg):
    for ct in range(4):
        for j in range(seg):
            scr[ct, pl.ds(j, 8, stride=seg), :] = vals[ct][8 * j:8 * j + 8]
        ref[:, ct * CH_T:(ct + 1) * CH_T] = scr[ct]


def ssm_dir_forward(tag, u, bpr, bpi, cpr, cpi, a8, tab, reverse, tm):
    LP = u.shape[0]
    nt = LP // tm
    seg = tm // 8

    def rix(i):
        return (nt - 1 - i) if reverse else i

    def kern(u_ref, bpr_ref, bpi_ref, cpr_ref, cpi_ref, a8_ref, tab_ref, xre_ref, xim_ref, y_ref,
             xr, xi, ys, cr, ci):
        i = pl.program_id(0)

        @pl.when(i == 0)
        def _():
            cr[...] = jnp.zeros_like(cr)
            ci[...] = jnp.zeros_like(ci)

        ub = _load_segmented(u_ref, ys, seg)
        for ct in range(4):
            uc = ub[ct].astype(BF16)
            xr[:, ct * ST_T:(ct + 1) * ST_T] = _nn(uc, bpr_ref[ct * CH_T:(ct + 1) * CH_T, :])
            xi[:, ct * ST_T:(ct + 1) * ST_T] = _nn(uc, bpi_ref[ct * CH_T:(ct + 1) * CH_T, :])
        _scan_tile(xr, xi, cr, ci, a8_ref, tab_ref, seg, reverse)
        xrb = xr[...].astype(BF16)
        xib = xi[...].astype(BF16)
        xre_ref[...] = xrb
        xim_ref[...] = xib
        yv = []
        for ct in range(4):
            ss = slice(ct * ST_T, (ct + 1) * ST_T)
            yv.append(_nn(xrb[:, ss], cpr_ref[ss, :]) - _nn(xib[:, ss], cpi_ref[ss, :]))
        _store_segmented(y_ref, ys, yv, seg)

    row = lambda w: pl.BlockSpec((tm, w), lambda i: (rix(i), 0))
    return pl.pallas_call(
        kern, name=tag, grid=(nt,),
        in_specs=[row(SW), _const_spec(bpr.shape), _const_spec(bpi.shape), _const_spec(cpr.shape),
                  _const_spec(cpi.shape), _const_spec(a8.shape), _const_spec(tab.shape)],
        out_specs=[row(NST), row(NST), row(SW)],
        out_shape=[jax.ShapeDtypeStruct((LP, NST), BF16), jax.ShapeDtypeStruct((LP, NST), BF16),
                   jax.ShapeDtypeStruct((LP, SW), F32)],
        scratch_shapes=[pltpu.VMEM((tm, NST), F32), pltpu.VMEM((tm, NST), F32), pltpu.VMEM((4, tm, CH_T), F32),
                        pltpu.VMEM((8, NST), F32), pltpu.VMEM((8, NST), F32)],
        compiler_params=_params("arbitrary"),
    )(u, bpr, bpi, cpr, cpi, a8, tab)


def ssm_dir_backward(tag, dy, xre, xim, u, bpr, bpi, cpr, cpi, a8_adj, tab_adj, reverse, tm):
    LP = u.shape[0]
    nt = LP // tm
    seg = tm // 8

    def rix(i):
        return (nt - 1 - i) if reverse else i

    def kern(dy_ref, xre_ref, xim_ref, u_ref, bpr_ref, bpi_ref, cpr_ref, cpi_ref, a8_ref, tab_ref,
             du_ref, gbr_ref, gbi_ref, gcr_ref, gci_ref, sr_ref, si_ref, lr, li, gr, gi, dus, cr, ci):
        i = pl.program_id(0)

        @pl.when(i == 0)
        def _():
            cr[...] = jnp.zeros_like(cr)
            ci[...] = jnp.zeros_like(ci)
            for r in (gbr_ref, gbi_ref, gcr_ref, gci_ref, sr_ref, si_ref):
                r[...] = jnp.zeros_like(r)

        dyb = [v.astype(BF16) for v in _load_segmented(dy_ref, dus, seg)]
        ub = [v.astype(BF16) for v in _load_segmented(u_ref, dus, seg)]
        for ct in range(4):
            ss = slice(ct * ST_T, (ct + 1) * ST_T)
            dc = dyb[ct]
            g_re = _nt(dc, cpr_ref[ss, :])
            g_im = -_nt(dc, cpi_ref[ss, :])
            lr[:, ss] = g_re
            li[:, ss] = g_im
            gr[:, ss] = g_re
            gi[:, ss] = g_im
        _scan_tile(lr, li, cr, ci, a8_ref, tab_ref, seg, reverse)
        lam_r = lr[...]
        lam_i = li[...]
        wr = lam_r - gr[...]
        wi = lam_i - gi[...]
        xr = xre_ref[...].astype(F32)
        xi = xim_ref[...].astype(F32)
        sr_ref[...] += jnp.sum(wr * xr + wi * xi, axis=0, keepdims=True)
        si_ref[...] += jnp.sum(wi * xr - wr * xi, axis=0, keepdims=True)
        lrb = lam_r.astype(BF16)
        lib = lam_i.astype(BF16)
        xrb = xre_ref[...]
        xib = xim_ref[...]
        duv = []
        for ct in range(4):
            ss = slice(ct * ST_T, (ct + 1) * ST_T)
            cs = slice(ct * CH_T, (ct + 1) * CH_T)
            duv.append(_nt(lrb[:, ss], bpr_ref[cs, :]) + _nt(lib[:, ss], bpi_ref[cs, :]))
            gbr_ref[ss, :] += _tn(lrb[:, ss], ub[ct])
            gbi_ref[ss, :] += _tn(lib[:, ss], ub[ct])
            gcr_ref[ss, :] += _tn(xrb[:, ss], dyb[ct])
            gci_ref[ss, :] -= _tn(xib[:, ss], dyb[ct])
        _store_segmented(du_ref, dus, duv, seg)

    row = lambda w: pl.BlockSpec((tm, w), lambda i: (rix(i), 0))
    acc = _const_spec((NST, CH_T))
    vec = _const_spec((1, NST))
    return pl.pallas_call(
        kern, name=tag, grid=(nt,),
        in_specs=[row(SW), row(NST), row(NST), row(SW), _const_spec(bpr.shape), _const_spec(bpi.shape),
                  _const_spec(cpr.shape), _const_spec(cpi.shape), _const_spec(a8_adj.shape),
                  _const_spec(tab_adj.shape)],
        out_specs=[row(SW), acc, acc, acc, acc, vec, vec],
        out_shape=[jax.ShapeDtypeStruct((LP, SW), F32)] + [jax.ShapeDtypeStruct((NST, CH_T), F32)] * 4
        + [jax.ShapeDtypeStruct((1, NST), F32)] * 2,
        scratch_shapes=[pltpu.VMEM((tm, NST), F32)] * 4 + [pltpu.VMEM((4, tm, CH_T), F32)]
        + [pltpu.VMEM((8, NST), F32)] * 2,
        compiler_params=_params("arbitrary"),
    )(dy, xre, xim, u, bpr, bpi, cpr, cpi, a8_adj, tab_adj)


def _ssm_disc(lam_re, lam_im, log_dt, b_re, b_im):
    dt = jnp.exp(log_dt)[:, None]
    mag = jnp.exp(lam_re * dt)
    a_re = mag * jnp.cos(lam_im * dt)
    a_im = mag * jnp.sin(lam_im * dt)
    den = lam_re * lam_re + lam_im * lam_im
    f_re = ((a_re - 1.0) * lam_re + a_im * lam_im) / den
    f_im = (a_im * lam_re - (a_re - 1.0) * lam_im) / den
    bb_re = f_re[:, :, None] * b_re - f_im[:, :, None] * b_im
    bb_im = f_re[:, :, None] * b_im + f_im[:, :, None] * b_re
    return a_re, a_im, bb_re, bb_im


def _scan_tables(lam_re, lam_im, log_dt, conj, reverse, seg):
    dt = jnp.exp(log_dt)[:, None]
    lr = (lam_re * dt).reshape(1, NST)
    li = (lam_im * dt).reshape(1, NST) * (-1.0 if conj else 1.0)
    t = jnp.arange(8, dtype=F32)[:, None]

    def power(kk):
        mag = jnp.exp(kk * lr)
        return mag * jnp.cos(kk * li), mag * jnp.sin(kk * li)

    ones = jnp.ones((8, 1), F32)
    a8 = jnp.stack(power(ones)).astype(F32)
    tabs = []
    for s in (1, 2, 4):
        mask = (t <= 7 - s) if reverse else (t >= s)
        pr, pi = power(float(s * seg) * ones)
        tabs += [jnp.where(mask, pr, 0.0), jnp.where(mask, pi, 0.0)]
    kk = ((8.0 - t) if reverse else (t + 1.0)) * float(seg)
    pr, pi = power(kk)
    tabs += [pr, pi]
    return a8, jnp.stack(tabs).astype(F32)


def _pack_b(bb):
    t = bb.transpose(0, 2, 1).reshape(4, 8, SCH, SP)
    eye = jnp.eye(8, dtype=bb.dtype)
    return jnp.einsum('tgcp,gh->tgchp', t, eye).reshape(SW, ST_T)


def _pack_c(cc):
    t = cc.transpose(0, 2, 1).reshape(4, 8, SP, SCH)
    eye = jnp.eye(8, dtype=cc.dtype)
    return jnp.einsum('tgpc,gh->tgphc', t, eye).reshape(NST, CH_T)


def _unpack_diag(acc):
    t = acc.reshape(4, 8, SP, 8, SCH)
    eye = jnp.eye(8, dtype=acc.dtype)
    return jnp.einsum('tgphc,gh->tgpc', t, eye).reshape(SGRP, SP, SCH)


def _gelu(y):
    k0 = math.sqrt(2.0 / math.pi)
    inner = k0 * (y + 0.044715 * y * y * y)
    th = jnp.tanh(inner)
    z = 0.5 * y * (1.0 + th)
    dz = 0.5 * (1.0 + th) + 0.5 * y * (1.0 - th * th) * k0 * (1.0 + 3.0 * 0.044715 * y * y)
    return z, dz


Q0, K0, V0, U0, GS0, GA0, IN_COLS = 0, 1024, 1280, 1536, 2048, 3072, 4096


def mixer_forward(tag, h, p, tm):
    g, winT, wglu, wbsT, wba, wout = p["g"], p["winT"], p["wglu"], p["wbsT"], p["wba"], p["wout"]

    def proj(i, hv, g_ref, w_ref):
        _, _, n = _rms_fwd(hv, g_ref[...])
        nb = n.astype(BF16)
        return (nb, _nt(nb, w_ref[Q0:K0, :]) * QSCALE, _nt(nb, w_ref[K0:V0, :]), _nt(nb, w_ref[V0:U0, :]),
                _nt(nb, w_ref[U0:GS0, :]), _nt(nb, w_ref[GS0:GA0, :]), _nt(nb, w_ref[GA0:IN_COLS, :]))

    n, q, k, v, u, gs, ga = rowcall(
        tag + "_proj", proj, [h], [g, winT],
        [(D, BF16), (D, BF16), (N_KV * HD, BF16), (N_KV * HD, BF16), (SW, F32), (D, F32), (D, F32)], tm=tm)

    ya = attention_forward(tag, q, k, v, p["sink"])

    states, ydir = [], []
    for dr in range(2):
        s = p["ssm"][dr]
        xre, xim, yd = ssm_dir_forward(f"{tag}_ssm_fwd{dr}", u, s["bpr"], s["bpi"], s["cpr"], s["cpi"],
                                       s["a8"], s["tab"], dr == 1, tm)
        states.append((xre, xim))
        ydir.append(yd)

    def glu(i, y0, y1, uv, d_ref, w_ref):
        ypre = y0 + y1 + d_ref[...] * uv
        z, _ = _gelu(ypre)
        zb = z.astype(BF16)
        t = _nn(zb, w_ref[...])
        return ypre, zb, t, z * _sig(t)

    ypre, zb, t, ys = rowcall(tag + "_glu", glu, [ydir[0], ydir[1], u], [p["d"], wglu],
                              [(SW, F32), (SW, BF16), (SW, F32), (SW, BF16)], tm=tm)

    def merge(i, ysv, yav, gsv, gav, wbs_ref, wba_ref):
        bs = _nt(ysv, wbs_ref[...])
        ba = _nn(yav, wba_ref[...])
        mg = _sig(gsv) * bs + _sig(gav) * ba
        mg = jnp.where(_row_ok(i, tm), mg, 0.0)
        return bs, ba, mg

    bs, ba, mg = rowcall(tag + "_merge", merge, [ys, ya, gs, ga], [wbsT, wba],
                         [(D, BF16), (D, BF16), (D, BF16)], tm=tm)

    def outp(i, mv, hv, w_ref):
        return (hv + _nn(mv, w_ref[...]),)

    (h2,) = rowcall(tag + "_out", outp, [mg, h], [wout], [(D, F32)], tm=tm)
    saved = dict(h=h, n=n, q=q, k=k, v=v, u=u, gs=gs, ga=ga, ya=ya, states=states, ypre=ypre, zb=zb, t=t,
                 ys=ys, bs=bs, ba=ba, mg=mg)
    return h2, saved


def mixer_backward(tag, dh, sv, p, tm):
    g, winT, wglu, wbsT, wba, wout = p["g"], p["winT"], p["wglu"], p["wbsT"], p["wba"], p["wout"]

    def y1(i, dhv, bsv, bav, gsv, gav, w_ref):
        dhb = dhv.astype(BF16)
        dmg = _nt(dhb, w_ref[...])
        dmg = jnp.where(_row_ok(i, tm), dmg, 0.0)
        sgs = _sig(gsv)
        sga = _sig(gav)
        return (dmg * sgs, dmg * sga, dmg * bsv.astype(F32) * sgs * (1.0 - sgs),
                dmg * bav.astype(F32) * sga * (1.0 - sga), dhb)

    dbs, dba, dgs, dga, dhb = rowcall(tag + "_bwd_merge", y1, [dh, sv["bs"], sv["ba"], sv["gs"], sv["ga"]], [wout],
                                      [(D, BF16)] * 5, tm=tm)
    dwout = tn_matmul(tag + "_dwout", sv["mg"], dhb)

    def y2(i, dbsv, dbav, wbs_ref, wba_ref):
        return _nn(dbsv, wbs_ref[...]), _nt(dbav, wba_ref[...])

    dys, dya = rowcall(tag + "_bwd_branch", y2, [dbs, dba], [wbsT, wba], [(SW, F32), (D, BF16)], tm=tm)
    dwbsT = tn_matmul(tag + "_dwbs", dbs, sv["ys"])
    dwba = tn_matmul(tag + "_dwba", sv["ya"], dba)

    def s2b(i, dysv, ypv, tv, uv, d_ref, w_ref):
        z, dz_dy = _gelu(ypv)
        st = _sig(tv)
        dt_ = dysv * z * st * (1.0 - st)
        dz = dysv * st + _nt(dt_.astype(BF16), w_ref[...])
        dyp = dz * dz_dy
        return dyp, dyp * d_ref[...], dt_, jnp.sum(dyp * uv, axis=0, keepdims=True)

    dypb, du0, dtb, dd = rowcall(tag + "_bwd_glu", s2b, [dys, sv["ypre"], sv["t"], sv["u"]], [p["d"], wglu],
                                 [(SW, F32), (SW, F32), (SW, BF16)], [(1, SW)], tm=tm)
    dwglu = tn_matmul(tag + "_dwglu", sv["zb"], dtb)

    du_dirs, ssm_sums = [], []
    for dr in range(2):
        s = p["ssm"][dr]
        xre, xim = sv["states"][dr]
        res = ssm_dir_backward(f"{tag}_ssm_bwd{dr}", dypb, xre, xim, sv["u"], s["bpr"], s["bpi"], s["cpr"],
                               s["cpi"], s["a8_adj"], s["tab_adj"], dr == 0, tm)
        du_dirs.append(res[0])
        ssm_sums.append(res[1:])

    dq, dk, dv, dsink = attention_backward(tag, sv["q"], sv["k"], sv["v"], dya, p["sink"])

    def x1b(i, dqv, dkv, dvv, du0v, du1v, du2v, dgsv, dgav, hv, dhv, g_ref, w_ref):
        dub = (du0v + du1v + du2v).astype(BF16)
        dn = (_nn(dqv, w_ref[Q0:K0, :]) + _nn(dkv, w_ref[K0:V0, :]) + _nn(dvv, w_ref[V0:U0, :])
              + _nn(dub, w_ref[U0:GS0, :]) + _nn(dgsv, w_ref[GS0:GA0, :]) + _nn(dgav, w_ref[GA0:IN_COLS, :]))
        gv = g_ref[...]
        hh, r, _ = _rms_fwd(hv, gv)
        dx, dg = _rms_bwd(hh, r, gv, dn)
        dx = jnp.where(_row_ok(i, tm), dx, 0.0)
        return dhv + dx, dub, dg

    dh2, dub, dg = rowcall(tag + "_bwd_in", x1b,
                           [dq, dk, dv, du0, du_dirs[0], du_dirs[1], dgs, dga, sv["h"], dh], [g, winT],
                           [(D, F32), (SW, BF16)], [(1, D)], tm=tm)
    n = sv["n"]
    dwinT = jnp.concatenate([tn_matmul(f"{tag}_dwin{j}", piece, n)
                             for j, piece in enumerate((dq, dk, dv, dub, dgs, dga))], axis=0)
    grads = dict(g=dg, d=dd, sink=dsink, ssm=ssm_sums, winT=dwinT, wglu=dwglu, wbsT=dwbsT, wba=dwba, wout=dwout)
    return dh2, grads


W1024 = ("f1_wgT", "f1_wuT", "f1_wd", "winT", "wba", "wout", "f2_wgT", "f2_wuT", "f2_wd")
W512 = ("wglu", "wbsT")
SMALL = ("ffn1_norm", "mix_norm", "ffn2_norm", "final_norm", "ssm_lam_re", "ssm_lam_im", "ssm_log_dt",
         "ssm_b_re", "ssm_b_im", "ssm_c_re", "ssm_c_im", "ssm_d", "attn_sink")


def kernel(x, meta_tokens, ffn1_norm, ffn1_w_gate, ffn1_w_up, ffn1_w_down, mix_norm, w_in, ssm_lam_re, ssm_lam_im, ssm_log_dt, ssm_b_re, ssm_b_im, ssm_c_re, ssm_c_im, ssm_d, ssm_w_glu, attn_sink, w_branch_ssm, w_branch_attn, w_out, ffn2_norm, ffn2_w_gate, ffn2_w_up, ffn2_w_down, final_norm, loss_target, m_meta_tokens, m_ffn1_norm, m_ffn1_w_gate, m_ffn1_w_up, m_ffn1_w_down, m_mix_norm, m_w_in, m_ssm_lam_re, m_ssm_lam_im, m_ssm_log_dt, m_ssm_b_re, m_ssm_b_im, m_ssm_c_re, m_ssm_c_im, m_ssm_d, m_ssm_w_glu, m_attn_sink, m_w_branch_ssm, m_w_branch_attn, m_w_out, m_ffn2_norm, m_ffn2_w_gate, m_ffn2_w_up, m_ffn2_w_down, m_final_norm, v_meta_tokens, v_ffn1_norm, v_ffn1_w_gate, v_ffn1_w_up, v_ffn1_w_down, v_mix_norm, v_w_in, v_ssm_lam_re, v_ssm_lam_im, v_ssm_log_dt, v_ssm_b_re, v_ssm_b_im, v_ssm_c_re, v_ssm_c_im, v_ssm_d, v_ssm_w_glu, v_attn_sink, v_w_branch_ssm, v_w_branch_attn, v_w_out, v_ffn2_norm, v_ffn2_w_gate, v_ffn2_w_up, v_ffn2_w_down, v_final_norm):
    weights = dict(meta_tokens=meta_tokens, ffn1_norm=ffn1_norm, ffn1_w_gate=ffn1_w_gate, ffn1_w_up=ffn1_w_up, ffn1_w_down=ffn1_w_down, mix_norm=mix_norm, w_in=w_in, ssm_lam_re=ssm_lam_re, ssm_lam_im=ssm_lam_im, ssm_log_dt=ssm_log_dt, ssm_b_re=ssm_b_re, ssm_b_im=ssm_b_im, ssm_c_re=ssm_c_re, ssm_c_im=ssm_c_im, ssm_d=ssm_d, ssm_w_glu=ssm_w_glu, attn_sink=attn_sink, w_branch_ssm=w_branch_ssm, w_branch_attn=w_branch_attn, w_out=w_out, ffn2_norm=ffn2_norm, ffn2_w_gate=ffn2_w_gate, ffn2_w_up=ffn2_w_up, ffn2_w_down=ffn2_w_down, final_norm=final_norm)
    mom_m = dict(meta_tokens=m_meta_tokens, ffn1_norm=m_ffn1_norm, ffn1_w_gate=m_ffn1_w_gate, ffn1_w_up=m_ffn1_w_up, ffn1_w_down=m_ffn1_w_down, mix_norm=m_mix_norm, w_in=m_w_in, ssm_lam_re=m_ssm_lam_re, ssm_lam_im=m_ssm_lam_im, ssm_log_dt=m_ssm_log_dt, ssm_b_re=m_ssm_b_re, ssm_b_im=m_ssm_b_im, ssm_c_re=m_ssm_c_re, ssm_c_im=m_ssm_c_im, ssm_d=m_ssm_d, ssm_w_glu=m_ssm_w_glu, attn_sink=m_attn_sink, w_branch_ssm=m_w_branch_ssm, w_branch_attn=m_w_branch_attn, w_out=m_w_out, ffn2_norm=m_ffn2_norm, ffn2_w_gate=m_ffn2_w_gate, ffn2_w_up=m_ffn2_w_up, ffn2_w_down=m_ffn2_w_down, final_norm=m_final_norm)
    mom_v = dict(meta_tokens=v_meta_tokens, ffn1_norm=v_ffn1_norm, ffn1_w_gate=v_ffn1_w_gate, ffn1_w_up=v_ffn1_w_up, ffn1_w_down=v_ffn1_w_down, mix_norm=v_mix_norm, w_in=v_w_in, ssm_lam_re=v_ssm_lam_re, ssm_lam_im=v_ssm_lam_im, ssm_log_dt=v_ssm_log_dt, ssm_b_re=v_ssm_b_re, ssm_b_im=v_ssm_b_im, ssm_c_re=v_ssm_c_re, ssm_c_im=v_ssm_c_im, ssm_d=v_ssm_d, ssm_w_glu=v_ssm_w_glu, attn_sink=v_attn_sink, w_branch_ssm=v_w_branch_ssm, w_branch_attn=v_w_branch_attn, w_out=v_w_out, ffn2_norm=v_ffn2_norm, ffn2_w_gate=v_ffn2_w_gate, ffn2_w_up=v_ffn2_w_up, ffn2_w_down=v_ffn2_w_down, final_norm=v_final_norm)
    names = list(weights)

    L0 = x.shape[1]
    LP = L0 + BLK
    tm = 384 if LP % 384 == 0 else BLK
    x_i, y_i, c_i = lax.axis_index("x"), lax.axis_index("y"), lax.axis_index("c")
    me = 4 * x_i + 2 * y_i + c_i

    def canon(l):
        return dict(
            f1_wgT=ffn1_w_gate[l].T, f1_wuT=ffn1_w_up[l].T, f1_wd=ffn1_w_down[l],
            winT=w_in[l].T, wba=w_branch_attn[l], wout=w_out[l],
            f2_wgT=ffn2_w_gate[l].T, f2_wuT=ffn2_w_up[l].T, f2_wd=ffn2_w_down[l],
            wglu=ssm_w_glu[l], wbsT=w_branch_ssm[l].T)

    shards = [{nm: a.astype(BF16) for nm, a in canon(l).items()} for l in range(DEPTH)]

    def pieces_of(names_):
        out, off = [], 0
        for nm in names_:
            r = shards[0][nm].shape[0]
            out.append((off, r))
            off += r
        return out

    p1_pieces, p2_pieces = pieces_of(W1024), pieces_of(W512)
    packed = [(jnp.concatenate([shards[l][nm] for nm in W1024], axis=0),
               jnp.concatenate([shards[l][nm] for nm in W512], axis=0)) for l in range(DEPTH)]
    g1, g2, gm = all_gather_pieces(
        "gather_weights_first", [(packed[0][0], p1_pieces), (packed[0][1], p2_pieces), (meta_tokens, [(0, N_META)])])
    full = [dict(zip(W1024 + W512, list(g1) + list(g2)))] + [None] * (DEPTH - 1)
    meta_full = gm[0].reshape(NDEV, N_META, D // NDEV).transpose(1, 0, 2).reshape(N_META, D)
    gather_handles = [None] + [gather_layer_start(f"gather_start_l{l}", list(packed[l]), [p1_pieces, p2_pieces])
                               for l in range(1, DEPTH)]
    started = sum(hd[3][0, 0] for hd in gather_handles[1:])

    def finish_gather(l, after):
        dests = gather_layer_wait(f"gather_wait_l{l}", gather_handles[l], 2, after)
        out = {}
        for nm, dest in zip(W1024 + W512, dests):
            sh = shards[l][nm]
            out[nm] = lax.dynamic_update_slice(dest, sh, (me * sh.shape[0], 0))
        return out

    def disc_all(lre, lim, ldt, bre, bim):
        return _ssm_disc(lre, lim, ldt, bre, bim)

    ssm_p, ssm_vjp = [], []
    for l in range(DEPTH):
        row, vrow = [], []
        for dr in range(2):
            args = (ssm_lam_re[l, dr], ssm_lam_im[l, dr], ssm_log_dt[l, dr], ssm_b_re[l, dr], ssm_b_im[l, dr])
            (a_re, a_im, bb_re, bb_im), vjp = jax.vjp(disc_all, *args)
            a8, tab = _scan_tables(args[0], args[1], args[2], False, dr == 1, tm // 8)
            a8_adj, tab_adj = _scan_tables(args[0], args[1], args[2], True, dr == 0, tm // 8)
            row.append(dict(
                bpr=_pack_b(bb_re).astype(BF16), bpi=_pack_b(bb_im).astype(BF16),
                cpr=_pack_c(ssm_c_re[l, dr]).astype(BF16), cpi=_pack_c(ssm_c_im[l, dr]).astype(BF16),
                a8=a8, tab=tab, a8_adj=a8_adj, tab_adj=tab_adj, a_re=a_re, a_im=a_im))
            vrow.append(vjp)
        ssm_p.append(row)
        ssm_vjp.append(vrow)

    blk0 = jnp.concatenate([jnp.zeros((PAD, D), F32), meta_full.astype(F32)], axis=0)
    h = build_h0(x[0], blk0)
    saved = []
    for l in range(DEPTH):
        if l > 0:
            full[l] = finish_gather(l, h)
        w = full[l]
        g1n, g2n = ffn1_norm[l][None, :], ffn2_norm[l][None, :]
        if l == 0:
            g1n = g1n + started
        mp = dict(g=mix_norm[l][None, :], winT=w["winT"], wglu=w["wglu"], wbsT=w["wbsT"], wba=w["wba"],
                  wout=w["wout"], d=ssm_d[l][None, :], sink=attn_sink[l], ssm=ssm_p[l])
        h, s1 = ffn_forward("ffn1", h, g1n, w["f1_wgT"], w["f1_wuT"], w["f1_wd"], tm)
        h, s2 = mixer_forward("mix", h, mp, tm)
        h, s3 = ffn_forward("ffn2", h, g2n, w["f2_wgT"], w["f2_wuT"], w["f2_wd"], tm)
        saved.append((s1, s2, s3, mp, g1n, g2n))

    dh, loss_acc, dgf = final_loss(h, loss_target[0], final_norm[None, :])
    loss = lax.psum(loss_acc[0, 0], MESH_AXES)

    big_grads = [None] * DEPTH
    small = {nm: [None] * DEPTH for nm in SMALL if nm != "final_norm"}
    scatter_handles = [None] * DEPTH
    own_rows = [None] * DEPTH
    sent = jnp.zeros((), F32)
    for l in reversed(range(DEPTH)):
        s1, s2, s3, mp, g1n, g2n = saved[l]
        w = full[l]
        dh, dg2, f2g, f2u, f2d = ffn_backward("ffn2", dh, s3, g2n + sent, w["f2_wgT"], w["f2_wuT"], w["f2_wd"], tm)
        dh, mg = mixer_backward("mix", dh, s2, mp, tm)
        dh, dg1, f1g, f1u, f1d = ffn_backward("ffn1", dh, s1, g1n, w["f1_wgT"], w["f1_wuT"], w["f1_wd"], tm)
        big_grads[l] = dict(f1_wgT=f1g, f1_wuT=f1u, f1_wd=f1d, winT=mg["winT"], wba=mg["wba"], wout=mg["wout"],
                            f2_wgT=f2g, f2_wuT=f2u, f2_wd=f2d, wglu=mg["wglu"], wbsT=mg["wbsT"])
        own_rows[l] = [jnp.concatenate(
            [lax.dynamic_slice_in_dim(big_grads[l][nm], me * r, r, axis=0) for nm, (_, r) in zip(names_, pcs)],
            axis=0) for names_, pcs in ((W1024, p1_pieces), (W512, p2_pieces))]
        scatter_handles[l] = scatter_layer_start(
            f"scatter_start_l{l}", [[big_grads[l][nm] for nm in W1024], [big_grads[l][nm] for nm in W512]])
        sent = scatter_handles[l][0][3][0, 0]
        small["ffn1_norm"][l] = dg1[0]
        small["mix_norm"][l] = mg["g"][0]
        small["ffn2_norm"][l] = dg2[0]
        small["ssm_d"][l] = mg["d"][0]
        small["attn_sink"][l] = mg["sink"][0, :N_HEADS]
        per_dir = {k: [] for k in ("ssm_lam_re", "ssm_lam_im", "ssm_log_dt", "ssm_b_re", "ssm_b_im",
                                   "ssm_c_re", "ssm_c_im")}
        for dr in range(2):
            gbr, gbi, gcr, gci, s_re, s_im = mg["ssm"][dr]
            a_re, a_im = ssm_p[l][dr]["a_re"], ssm_p[l][dr]["a_im"]
            s_re = s_re.reshape(SGRP, SP)
            s_im = s_im.reshape(SGRP, SP)
            den = a_re * a_re + a_im * a_im
            ga_re = (s_re * a_re - s_im * a_im) / den
            ga_im = (s_re * a_im + s_im * a_re) / den
            glr, gli, gld, gbre, gbim = ssm_vjp[l][dr]((ga_re, ga_im, _unpack_diag(gbr), _unpack_diag(gbi)))
            per_dir["ssm_lam_re"].append(glr)
            per_dir["ssm_lam_im"].append(gli)
            per_dir["ssm_log_dt"].append(gld)
            per_dir["ssm_b_re"].append(gbre)
            per_dir["ssm_b_im"].append(gbim)
            per_dir["ssm_c_re"].append(_unpack_diag(gcr).transpose(0, 2, 1))
            per_dir["ssm_c_im"].append(_unpack_diag(gci).transpose(0, 2, 1))
        for k, vlist in per_dir.items():
            small[k][l] = jnp.stack(vlist)

    grad_x = dh[BLK:][None]
    dmeta_part = dh[PAD:BLK]

    small_part = {k: jnp.stack(vv) for k, vv in small.items()}
    small_part["final_norm"] = dgf[0]
    pieces = [small_part[k].reshape(-1) for k in SMALL] + [dmeta_part.reshape(-1)]
    sizes = [p_.shape[0] for p_ in pieces]
    total = sum(sizes)
    rows_s = -(-total // (8 * D)) * 8
    flat = jnp.concatenate(pieces + [jnp.zeros((rows_s * D - total,), F32)]).reshape(rows_s, D)
    ((gathered,),) = all_gather_pieces("gather_small_grads", [(flat, [(0, rows_s)])])
    small_sum = sum_slots("sum_small_grads", gathered.reshape(NDEV, rows_s, D)).reshape(-1)
    grads = {}
    o = 0
    for k, sz in zip(SMALL, sizes[:-1]):
        grads[k] = small_sum[o:o + sz].reshape(weights[k].shape)
        o += sz
    dmeta_full = small_sum[o:o + N_META * D].reshape(N_META, D)
    grads["meta_tokens"] = lax.dynamic_slice_in_dim(dmeta_full, me * (D // NDEV), D // NDEV, axis=1)

    own = [dict() for _ in range(DEPTH)]
    for l in reversed(range(DEPTH)):
        handle, nin = scatter_handles[l]
        lands = scatter_layer_wait(f"scatter_wait_l{l}", handle, nin, 2, dh)
        for land, mine, names_, pcs, tag in ((lands[0], own_rows[l][0], W1024, p1_pieces, "1024"),
                                             (lands[1], own_rows[l][1], W512, p2_pieces, "512")):
            land = lax.dynamic_update_slice(land, mine[None], (me, 0, 0))
            tot = sum_slots("sum_weight_grads_" + tag, land)
            for nm, (off_, r) in zip(names_, pcs):
                own[l][nm] = tot[off_:off_ + r]

    def stack(fn):
        return jnp.stack([fn(own[l]) for l in range(DEPTH)])

    grads["ffn1_w_gate"] = stack(lambda d: d["f1_wgT"].T)
    grads["ffn1_w_up"] = stack(lambda d: d["f1_wuT"].T)
    grads["ffn1_w_down"] = stack(lambda d: d["f1_wd"])
    grads["w_in"] = stack(lambda d: d["winT"].T)
    grads["ssm_w_glu"] = stack(lambda d: d["wglu"])
    grads["w_branch_ssm"] = stack(lambda d: d["wbsT"].T)
    grads["w_branch_attn"] = stack(lambda d: d["wba"])
    grads["w_out"] = stack(lambda d: d["wout"])
    grads["ffn2_w_gate"] = stack(lambda d: d["f2_wgT"].T)
    grads["ffn2_w_up"] = stack(lambda d: d["f2_wuT"].T)
    grads["ffn2_w_down"] = stack(lambda d: d["f2_wd"])

    deltas, new_m, new_v = {}, {}, {}
    for nm in names:
        deltas[nm], new_m[nm], new_v[nm] = adamw("adamw_" + nm, weights[nm], grads[nm], mom_m[nm], mom_v[nm])

    return (loss, grad_x, *[grads[n] for n in names], *[deltas[n] for n in names],
            *[new_m[n] for n in names], *[new_v[n] for n in names])
```

```python
import functools
import math

import jax
import jax.numpy as jnp
from jax import lax
from jax.experimental import pallas as pl
from jax.experimental.pallas import tpu as pltpu

F32 = jnp.float32
BF16 = jnp.bfloat16

D = 1024
DFF = 2816
N_META = 16
N_HEADS = 16
N_KV = 4
HD = 64
QG = 4
WIN = 128
BLK = 128
PAD = BLK - N_META
SW = 512
SGRP = 32
SCH = 16
SP = 64
NST = SGRP * SP
EPS = 1e-6
NEG = -1e30
SCALE = HD ** -0.5
NDEV = 8
DEPTH = 4
MESH_AXES = ("x", "y", "c")
MESH = pl.DeviceIdType.MESH

ADAM_LR = 0.001
ADAM_B1 = 0.9
ADAM_B2 = 0.999
ADAM_EPS = 1e-08
ADAM_WD = 0.01
ADAM_STEP = 10

VMEM_LIMIT = 56 * 1024 * 1024


def _params(*sem):
    return pltpu.CompilerParams(dimension_semantics=sem, vmem_limit_bytes=VMEM_LIMIT)


def _nn(a, b):
    return lax.dot_general(a, b, (((1,), (0,)), ((), ())), preferred_element_type=F32)


def _nt(a, b):
    return lax.dot_general(a, b, (((1,), (1,)), ((), ())), preferred_element_type=F32)


def _tn(a, b):
    return lax.dot_general(a, b, (((0,), (0,)), ((), ())), preferred_element_type=F32)


def _sig(x):
    return 0.5 * jnp.tanh(0.5 * x) + 0.5


def _rms_fwd(h, g):
    r = lax.rsqrt(jnp.mean(h * h, axis=-1, keepdims=True) + EPS)
    hh = h * r
    return hh, r, hh * g


def _rms_bwd(hh, r, g, dn):
    dhh = dn * g
    dx = r * (dhh - hh * jnp.mean(dhh * hh, axis=-1, keepdims=True))
    return dx, jnp.sum(dn * hh, axis=0, keepdims=True)


def _row_ok(i, tm):
    rows = i * tm + lax.broadcasted_iota(jnp.int32, (tm, 1), 0)
    return rows >= PAD


def _const_spec(shape):
    nd = len(shape)
    return pl.BlockSpec(shape, lambda *_: (0,) * nd)


def rowcall(name, body, rows, consts, outs, accs=(), *, tm):
    nrows = rows[0].shape[0]
    nt = nrows // tm
    assert nt * tm == nrows, (name, nrows, tm)
    nr, nc, no, na = len(rows), len(consts), len(outs), len(accs)
    in_specs = [pl.BlockSpec((tm, r.shape[1]), lambda i: (i, 0)) for r in rows]
    in_specs += [_const_spec(c.shape) for c in consts]
    out_shape = [jax.ShapeDtypeStruct((nrows, w), dt) for (w, dt) in outs]
    out_specs = [pl.BlockSpec((tm, w), lambda i: (i, 0)) for (w, dt) in outs]
    out_shape += [jax.ShapeDtypeStruct(s, F32) for s in accs]
    out_specs += [_const_spec(s) for s in accs]

    def kern(*refs):
        i = pl.program_id(0)
        row_vals = [r[...] for r in refs[:nr]]
        res = body(i, *row_vals, *refs[nr:nr + nc])
        out_refs = refs[nr + nc:nr + nc + no]
        acc_refs = refs[nr + nc + no:]
        for r, v in zip(out_refs, res[:no]):
            r[...] = v.astype(r.dtype)
        if na:
            @pl.when(i == 0)
            def _():
                for r in acc_refs:
                    r[...] = jnp.zeros_like(r)
            for r, v in zip(acc_refs, res[no:]):
                r[...] += v

    res = pl.pallas_call(
        kern, name=name, grid=(nt,), in_specs=in_specs, out_specs=out_specs, out_shape=out_shape,
        compiler_params=_params("arbitrary"),
    )(*rows, *consts)
    return res


def tn_matmul(name, lhs, rhs, scale=1.0):
    M, K = lhs.shape
    N = rhs.shape[1]
    assert lhs.dtype == BF16 and rhs.dtype == BF16
    nm = 6
    tmw = M // nm
    assert tmw * nm == M and tmw % 16 == 0
    tk = 1408 if (K % 1408 == 0) else K
    nk = K // tk

    def kern(a_ref, b_ref, o_ref, acc):
        m = pl.program_id(1)
        part = _tn(a_ref[...], b_ref[...])

        @pl.when(m == 0)
        def _():
            acc[...] = part

        @pl.when((m > 0) & (m < nm - 1))
        def _():
            acc[...] += part

        @pl.when(m == nm - 1)
        def _():
            o_ref[...] = ((acc[...] + part) * scale).astype(o_ref.dtype)

    return pl.pallas_call(
        kern, name=name, grid=(nk, nm),
        in_specs=[pl.BlockSpec((tmw, tk), lambda k, m: (m, k)), pl.BlockSpec((tmw, N), lambda k, m: (m, 0))],
        out_specs=pl.BlockSpec((tk, N), lambda k, m: (k, 0)),
        out_shape=jax.ShapeDtypeStruct((K, N), BF16),
        scratch_shapes=[pltpu.VMEM((tk, N), F32)],
        compiler_params=_params("arbitrary", "arbitrary"),
    )(lhs, rhs)


def _mesh_pos():
    x, y, c = lax.axis_index("x"), lax.axis_index("y"), lax.axis_index("c")
    return x, y, c


def all_gather_pieces(name, groups):
    ng = len(groups)
    packed = [g[0] for g in groups]
    pieces = [g[1] for g in groups]
    out_shape, out_map = [], []
    for gi, (p, pcs) in enumerate(groups):
        idx = []
        for (off, r) in pcs:
            idx.append(len(out_shape))
            out_shape.append(jax.ShapeDtypeStruct((NDEV * r, p.shape[1]), p.dtype))
        out_map.append(idx)
    nout = len(out_shape)

    def body(*refs):
        p_refs = refs[:ng]
        o_refs = refs[ng:ng + nout]
        send_sems, recv_sems, local_sems = refs[ng + nout:]
        x, y, c = _mesh_pos()
        me = (x, y, c)
        sibling = (x, y, 1 - c)
        chips = [(1 - x, y), (x, 1 - y), (1 - x, 1 - y)]

        def blk(px, py, pc):
            return 4 * px + 2 * py + pc

        def copies(gi, k, origin, to, from_out):
            cps = []
            for (off, r), oi in zip(pieces[gi], out_map[gi]):
                dst = o_refs[oi].at[pl.ds(origin * r, r), :]
                src = dst if from_out else p_refs[gi].at[pl.ds(off, r), :]
                cps.append(pltpu.make_async_remote_copy(
                    src_ref=src, dst_ref=dst, send_sem=send_sems.at[gi, k], recv_sem=recv_sems.at[gi, k],
                    device_id=to, device_id_type=MESH))
            return cps

        def whole(gi, k):
            return pltpu.make_async_remote_copy(
                src_ref=p_refs[gi], dst_ref=p_refs[gi], send_sem=send_sems.at[gi, k],
                recv_sem=recv_sems.at[gi, k], device_id=me, device_id_type=MESH)

        mine = []
        for gi in range(ng):
            for (off, r), oi in zip(pieces[gi], out_map[gi]):
                mine.append(pltpu.make_async_copy(
                    p_refs[gi].at[pl.ds(off, r), :], o_refs[oi].at[pl.ds(blk(*me) * r, r), :],
                    local_sems.at[gi]))
        for cp in mine:
            cp.start()
        for gi in range(ng):
            for cp in copies(gi, 0, blk(*me), sibling, False):
                cp.start()
            for j, chip in enumerate(chips):
                for cp in copies(gi, 1 + j, blk(*me), (*chip, c), False):
                    cp.start()
        for j, chip in enumerate(chips):
            for gi in range(ng):
                whole(gi, 1 + j).wait_recv()
                for cp in copies(gi, 4 + j, blk(*chip, c), sibling, True):
                    cp.start()
        for gi in range(ng):
            whole(gi, 0).wait_recv()
            for j in range(3):
                whole(gi, 4 + j).wait_recv()
        for gi in range(ng):
            for k in range(7):
                whole(gi, k).wait_send()
            pltpu.make_async_copy(p_refs[gi], p_refs[gi], local_sems.at[gi]).wait()

    any_spec = pl.BlockSpec(memory_space=pl.ANY)
    outs = pl.pallas_call(
        body, name=name, out_shape=out_shape,
        in_specs=[any_spec] * ng, out_specs=[any_spec] * nout,
        scratch_shapes=[pltpu.SemaphoreType.DMA((ng, 7)), pltpu.SemaphoreType.DMA((ng, 7)),
                        pltpu.SemaphoreType.DMA((ng,))],
    )(*packed)
    return [[outs[oi] for oi in idx] for idx in out_map]


HBM_SPEC = pl.BlockSpec(memory_space=pltpu.HBM)
SEM_SPEC = pl.BlockSpec(memory_space=pltpu.SEMAPHORE)
DATAFLOW = pltpu.SideEffectType.DATAFLOW_SIDE_EFFECTING


def _peers(x, y, c):
    return [(x, y, 1 - c), (1 - x, y, c), (x, 1 - y, c), (1 - x, 1 - y, c),
            (1 - x, y, 1 - c), (x, 1 - y, 1 - c), (1 - x, 1 - y, 1 - c)]


def exchange_start(name, arrays, ng, plan):
    n = len(arrays)
    ns = ng * 7

    def body(*refs):
        in_refs = refs[:n]
        send_sems, recv_sems = refs[n:n + ns], refs[n + ns:n + 2 * ns]
        token = refs[-1]
        x, y, c = _mesh_pos()
        me_i = 4 * x + 2 * y + c
        for k, peer in enumerate(_peers(x, y, c)):
            p_i = 4 * peer[0] + 2 * peer[1] + peer[2]
            for src, dst, gi in plan(in_refs, me_i, p_i):
                pltpu.make_async_remote_copy(
                    src_ref=src, dst_ref=dst, send_sem=send_sems[gi * 7 + k], recv_sem=recv_sems[gi * 7 + k],
                    device_id=peer, device_id_type=MESH).start()
        token[...] = jnp.zeros_like(token)

    res = pl.pallas_call(
        body, name=name,
        out_shape=(*[pltpu.SemaphoreType.DMA(())] * (2 * ns),
                   *[pltpu.HBM(a.shape, a.dtype) for a in arrays], jax.ShapeDtypeStruct((8, 128), F32)),
        in_specs=[HBM_SPEC] * n,
        out_specs=(*[SEM_SPEC] * (2 * ns), *[HBM_SPEC] * n, pl.BlockSpec(memory_space=pltpu.VMEM)),
        input_output_aliases={i: 2 * ns + i for i in range(n)},
        compiler_params=pltpu.CompilerParams(has_side_effects=DATAFLOW),
    )(*[pltpu.with_memory_space_constraint(a, pltpu.HBM) for a in arrays])
    return list(res[:ns]), list(res[ns:2 * ns]), list(res[2 * ns:2 * ns + n]), res[-1]


def exchange_wait(name, send_sems, recv_sems, arrays, ng, sized, after):
    n = len(arrays)
    ns = ng * 7

    def body(*refs):
        in_refs = refs[:n]
        s_sems, r_sems = refs[n:n + ns], refs[n + ns:n + 2 * ns]
        x, y, c = _mesh_pos()
        for gi in range(ng):
            view = sized(in_refs, gi)
            for k in range(7):
                w = pltpu.make_async_remote_copy(
                    src_ref=view, dst_ref=view, send_sem=s_sems[gi * 7 + k], recv_sem=r_sems[gi * 7 + k],
                    device_id=(x, y, c), device_id_type=MESH)
                w.wait_send()
                w.wait_recv()

    res = pl.pallas_call(
        body, name=name, out_shape=tuple(pltpu.HBM(a.shape, a.dtype) for a in arrays),
        in_specs=[HBM_SPEC] * n + [SEM_SPEC] * (2 * ns) + [pl.BlockSpec(memory_space=pl.ANY)],
        out_specs=tuple([HBM_SPEC] * n), input_output_aliases={i: i for i in range(n)},
        compiler_params=pltpu.CompilerParams(has_side_effects=DATAFLOW),
    )(*arrays, *send_sems, *recv_sems, after)
    return list(res)


def gather_layer_start(name, packed, pieces):
    ng = len(packed)
    dests = [lax.empty((NDEV * r, p.shape[1]), p.dtype) for p, pcs in zip(packed, pieces) for (_, r) in pcs]

    def plan(refs, me_i, p_i):
        out, di = [], ng
        for gi in range(ng):
            for (off, r) in pieces[gi]:
                out.append((refs[gi].at[pl.ds(off, r), :], refs[di].at[pl.ds(me_i * r, r), :], gi))
                di += 1
        return out

    return exchange_start(name, list(packed) + dests, ng, plan)


def gather_layer_wait(name, handle, ng, after):
    send_sems, recv_sems, arrays, _ = handle
    out = exchange_wait(name, send_sems, recv_sems, arrays, ng, lambda refs, gi: refs[gi], after)
    return out[ng:]


def scatter_layer_start(name, groups):
    ng = len(groups)
    flat = [a for arrs in groups for a in arrs]
    offs, lands = [], []
    for arrs in groups:
        o, off = [], 0
        for a in arrs:
            r = a.shape[0] // NDEV
            o.append((off, r))
            off += r
        offs.append(o)
        lands.append(jnp.zeros((NDEV, off, arrs[0].shape[1]), arrs[0].dtype))
    nin = len(flat)

    def plan(refs, me_i, p_i):
        out, ai = [], 0
        for gi in range(ng):
            for (off, r) in offs[gi]:
                out.append((refs[ai].at[pl.ds(p_i * r, r), :], refs[nin + gi].at[me_i, pl.ds(off, r), :], gi))
                ai += 1
        return out

    return exchange_start(name, flat + lands, ng, plan), nin


def scatter_layer_wait(name, handle, nin, ng, after):
    send_sems, recv_sems, arrays, _ = handle
    out = exchange_wait(name, send_sems, recv_sems, arrays, ng, lambda refs, gi: refs[nin + gi].at[0], after)
    return out[nin:]


def _pick_tile(n, cap):
    best = None
    for t in range(8, min(n, cap) + 1, 8):
        if n % t == 0:
            best = t
    return best if best is not None else n


def sum_slots(name, land):
    _, R, W = land.shape
    tr = _pick_tile(R, 512)

    def kern(l_ref, o_ref):
        acc = l_ref[0].astype(F32)
        for s in range(1, NDEV):
            acc = acc + l_ref[s].astype(F32)
        o_ref[...] = acc

    return pl.pallas_call(
        kern, name=name, grid=(R // tr,),
        in_specs=[pl.BlockSpec((NDEV, tr, W), lambda i: (0, i, 0))],
        out_specs=pl.BlockSpec((tr, W), lambda i: (i, 0)),
        out_shape=jax.ShapeDtypeStruct((R, W), F32),
        compiler_params=_params("arbitrary"),
    )(land)


def adamw(name, w, g, m, v):
    shp = w.shape
    C = shp[-1]
    R = max(1, math.prod(shp[:-1]))
    tr = _pick_tile(R, 1024)
    w2, g2, m2, v2 = (a.reshape(R, C) for a in (w, g, m, v))

    def kern(w_ref, g_ref, m_ref, v_ref, d_ref, mo_ref, vo_ref):
        gg = g_ref[...]
        mn = ADAM_B1 * m_ref[...] + (1.0 - ADAM_B1) * gg
        vn = ADAM_B2 * v_ref[...] + (1.0 - ADAM_B2) * jnp.square(gg)
        m_hat = mn / (1.0 - ADAM_B1 ** ADAM_STEP)
        v_hat = vn / (1.0 - ADAM_B2 ** ADAM_STEP)
        d_ref[...] = -ADAM_LR * (m_hat / (jnp.sqrt(v_hat) + ADAM_EPS) + ADAM_WD * w_ref[...])
        mo_ref[...] = mn
        vo_ref[...] = vn

    spec = pl.BlockSpec((tr, C), lambda i: (i, 0))
    d, mo, vo = pl.pallas_call(
        kern, name=name, grid=(R // tr,), in_specs=[spec] * 4, out_specs=[spec] * 3,
        out_shape=[jax.ShapeDtypeStruct((R, C), F32)] * 3, compiler_params=_params("arbitrary"),
    )(w2, g2, m2, v2)
    return d.reshape(shp), mo.reshape(shp), vo.reshape(shp)


def build_h0(x2, blk0):
    L0 = x2.shape[0]
    nb = L0 // BLK + 1

    def kern(x_ref, b_ref, o_ref):
        i = pl.program_id(0)

        @pl.when(i == 0)
        def _():
            o_ref[...] = b_ref[...]

        @pl.when(i > 0)
        def _():
            o_ref[...] = x_ref[...]

    return pl.pallas_call(
        kern, name="build_h0", grid=(nb,),
        in_specs=[pl.BlockSpec((BLK, D), lambda i: (jnp.maximum(i - 1, 0), 0)), _const_spec((BLK, D))],
        out_specs=pl.BlockSpec((BLK, D), lambda i: (i, 0)),
        out_shape=jax.ShapeDtypeStruct((L0 + BLK, D), F32), compiler_params=_params("arbitrary"),
    )(x2, blk0)


def final_loss(h, tgt, gf):
    LP = h.shape[0]
    nb = LP // BLK

    def kern(h_ref, t_ref, g_ref, dh_ref, loss_ref, dg_ref):
        i = pl.program_id(0)

        @pl.when(i == 0)
        def _():
            loss_ref[...] = jnp.zeros_like(loss_ref)
            dg_ref[...] = jnp.zeros_like(dg_ref)

        g = g_ref[...]
        hh, r, yv = _rms_fwd(h_ref[...], g)
        valid = (i > 0).astype(F32)
        err = (yv - t_ref[...]) * valid
        loss_ref[...] += 0.5 * jnp.sum(jnp.sum(err * err, axis=1, keepdims=True), axis=0, keepdims=True) / D
        dy = err / D
        dx, dg = _rms_bwd(hh, r, g, dy)
        dh_ref[...] = dx
        dg_ref[...] += dg

    return pl.pallas_call(
        kern, name="final_loss", grid=(nb,),
        in_specs=[pl.BlockSpec((BLK, D), lambda i: (i, 0)),
                  pl.BlockSpec((BLK, D), lambda i: (jnp.maximum(i - 1, 0), 0)), _const_spec((1, D))],
        out_specs=[pl.BlockSpec((BLK, D), lambda i: (i, 0)), _const_spec((8, 128)), _const_spec((1, D))],
        out_shape=[jax.ShapeDtypeStruct((LP, D), F32), jax.ShapeDtypeStruct((8, 128), F32),
                   jax.ShapeDtypeStruct((1, D), F32)],
        compiler_params=_params("arbitrary"),
    )(h, tgt, gf)


def ffn_forward(tag, h, g, wgT, wuT, wd, tm):
    def f1(i, hv, g_ref, wg_ref, wu_ref):
        _, _, n = _rms_fwd(hv, g_ref[...])
        nb = n.astype(BF16)
        G = _nt(nb, wg_ref[...])
        U = _nt(nb, wu_ref[...])
        A = G * _sig(G) * U
        return nb, G, U, A

    n, G, U, A = rowcall(tag + "_up", f1, [h], [g, wgT, wuT],
                         [(D, BF16), (DFF, BF16), (DFF, BF16), (DFF, BF16)], tm=tm)

    def f2(i, av, hv, wd_ref):
        return (hv + 0.5 * _nn(av, wd_ref[...]),)

    (h2,) = rowcall(tag + "_down", f2, [A, h], [wd], [(D, F32)], tm=tm)
    return h2, (h, n, G, U, A)


def ffn_backward(tag, dh, saved, g, wgT, wuT, wd, tm):
    h, n, G, U, A = saved

    def b1(i, dhv, Gv, Uv, wd_ref):
        dyb = (0.5 * dhv).astype(BF16)
        dA = _nt(dyb, wd_ref[...])
        Gf = Gv.astype(F32)
        sg = _sig(Gf)
        dG = dA * Uv.astype(F32) * (sg * (1.0 + Gf * (1.0 - sg)))
        dU = dA * (Gf * sg)
        return dG, dU, dyb

    dG, dU, dyb = rowcall(tag + "_bwd_act", b1, [dh, G, U], [wd], [(DFF, BF16), (DFF, BF16), (D, BF16)], tm=tm)

    def b2(i, dGv, dUv, hv, dhv, g_ref, wg_ref, wu_ref):
        dn = _nn(dGv, wg_ref[...]) + _nn(dUv, wu_ref[...])
        gv = g_ref[...]
        hh, r, _ = _rms_fwd(hv, gv)
        dx, dg = _rms_bwd(hh, r, gv, dn)
        dx = jnp.where(_row_ok(i, tm), dx, 0.0)
        return dhv + dx, dg

    dh2, dg = rowcall(tag + "_bwd_in", b2, [dG, dU, h, dh], [g, wgT, wuT], [(D, F32)], [(1, D)], tm=tm)
    dwd = tn_matmul(tag + "_dwd", A, dyb)
    dwgT = tn_matmul(tag + "_dwg", dG, n)
    dwuT = tn_matmul(tag + "_dwu", dU, n)
    return dh2, dg, dwgT, dwuT, dwd


def _alibi_slope(head):
    return float(2.0 ** (-8.0 * (head + 1) / N_HEADS))


def _att_bias(n, nb):
    qi = lax.broadcasted_iota(jnp.int32, (BLK, 4 * BLK), 0)
    cj = lax.broadcasted_iota(jnp.int32, (BLK, 4 * BLK), 1)
    jb = cj - BLK
    dist = jnp.abs(qi + BLK - jb)
    kpos = (n - 1) * BLK + jb
    band_ok = (dist <= WIN) & (kpos >= BLK) & (kpos < nb * BLK)
    is_meta = cj < BLK
    ok = (is_meta & (cj >= PAD)) | (jnp.logical_not(is_meta) & band_ok)
    distf = jnp.where(is_meta, 0, dist).astype(F32)
    maskadd = jnp.where(ok, 0.0, NEG).astype(F32)
    distf4 = jnp.concatenate([distf] * QG, axis=0)
    mask4 = jnp.concatenate([maskadd] * QG, axis=0)
    return distf4, mask4


def _group_col(vals):
    rg = lax.broadcasted_iota(jnp.int32, (QG * BLK, 1), 0) // BLK
    col = jnp.full((QG * BLK, 1), vals[QG - 1], F32)
    for gq in range(QG - 2, -1, -1):
        col = jnp.where(rg == gq, vals[gq], col)
    return col


def _stack_heads(ref_or_val, kh):
    return jnp.concatenate(
        [ref_or_val[:, (kh * QG + gq) * HD:(kh * QG + gq + 1) * HD] for gq in range(QG)], axis=0)


def _stack_keys(km, kp, kc, kn, kh):
    sl = slice(kh * HD, (kh + 1) * HD)
    return jnp.concatenate([km[:, sl], kp[:, sl], kc[:, sl], kn[:, sl]], axis=0)


LOG2E = 1.4426950408889634
LN2 = 0.6931471805599453
QSCALE = SCALE * LOG2E


def _att_update_bias(bias_ref, n, nb):
    @pl.when((n <= 2) | (n == nb - 1))
    def _():
        distf4, mask4 = _att_bias(n, nb)
        for kh in range(N_KV):
            slope_col = _group_col([_alibi_slope(kh * QG + gq) * LOG2E for gq in range(QG)])
            bias_ref[kh] = mask4 - slope_col * distf4


def _att_exp(qs, kb, bias_ref, kh, sink_ref):
    sink_col = _group_col([sink_ref[kh * QG + gq] for gq in range(QG)]) * LOG2E
    s = _nt(qs, kb) + bias_ref[kh]
    m = jnp.maximum(jnp.max(s, axis=1, keepdims=True), sink_col)
    e = jnp.exp2(s - m)
    es = jnp.exp2(sink_col - m)
    inv = 1.0 / (jnp.sum(e, axis=1, keepdims=True) + es)
    return e, es, inv


def attention_forward(tag, q, k, v, sink):
    LP = q.shape[0]
    nb = LP // BLK

    def kern(sink_ref, q_ref, km_ref, kp_ref, kc_ref, kn_ref, vm_ref, vp_ref, vc_ref, vn_ref, o_ref, bias_ref):
        n = pl.program_id(0)
        _att_update_bias(bias_ref, n, nb)
        qv = q_ref[...]
        km, kp, kc, kn = km_ref[...], kp_ref[...], kc_ref[...], kn_ref[...]
        vm, vp, vc, vn = vm_ref[...], vp_ref[...], vc_ref[...], vn_ref[...]
        for kh in range(N_KV):
            qs = _stack_heads(qv, kh)
            kb = _stack_keys(km, kp, kc, kn, kh)
            vb = _stack_keys(vm, vp, vc, vn, kh)
            e, _, inv = _att_exp(qs, kb, bias_ref, kh, sink_ref)
            o = _nn(e.astype(BF16), vb) * inv
            for gq in range(QG):
                hcol = (kh * QG + gq) * HD
                o_ref[:, hcol:hcol + HD] = o[gq * BLK:(gq + 1) * BLK].astype(o_ref.dtype)

    def kvspec(dn):
        return pl.BlockSpec((BLK, N_KV * HD), lambda n: (jnp.clip(n + dn, 0, nb - 1), 0))

    meta_spec = pl.BlockSpec((BLK, N_KV * HD), lambda n: (0, 0))
    return pl.pallas_call(
        kern, name=tag + "_att_fwd", grid=(nb,),
        in_specs=[pl.BlockSpec(memory_space=pltpu.SMEM), pl.BlockSpec((BLK, D), lambda n: (n, 0)),
                  meta_spec, kvspec(-1), kvspec(0), kvspec(1), meta_spec, kvspec(-1), kvspec(0), kvspec(1)],
        out_specs=pl.BlockSpec((BLK, D), lambda n: (n, 0)),
        out_shape=jax.ShapeDtypeStruct((LP, D), BF16),
        scratch_shapes=[pltpu.VMEM((N_KV, QG * BLK, 4 * BLK), F32)], compiler_params=_params("arbitrary"),
    )(sink, q, k, k, k, k, v, v, v, v)


def attention_backward(tag, q, k, v, do, sink):
    LP = q.shape[0]
    nb = LP // BLK
    KW = N_KV * HD

    def kern(sink_ref, q_ref, do_ref, km_ref, kp_ref, kc_ref, kn_ref, vm_ref, vp_ref, vc_ref, vn_ref,
             dq_ref, dkp_ref, dvp_ref, dkm_ref, dvm_ref, dsink_ref, bias_ref):
        n = pl.program_id(0)

        @pl.when(n == 0)
        def _():
            dkm_ref[...] = jnp.zeros_like(dkm_ref)
            dvm_ref[...] = jnp.zeros_like(dvm_ref)
            dsink_ref[...] = jnp.zeros_like(dsink_ref)

        _att_update_bias(bias_ref, n, nb)
        qv, dov = q_ref[...], do_ref[...]
        km, kp, kc, kn = km_ref[...], kp_ref[...], kc_ref[...], kn_ref[...]
        vm, vp, vc, vn = vm_ref[...], vp_ref[...], vc_ref[...], vn_ref[...]
        lane = lax.broadcasted_iota(jnp.int32, (8, 128), 1)
        dsink = jnp.zeros((8, 128), F32)
        for kh in range(N_KV):
            qs = _stack_heads(qv, kh)
            dos = _stack_heads(dov, kh)
            kb = _stack_keys(km, kp, kc, kn, kh)
            vb = _stack_keys(vm, vp, vc, vn, kh)
            dp = _nt(dos, vb)
            e, es, inv = _att_exp(qs, kb, bias_ref, kh, sink_ref)
            delta = inv * jnp.sum(e * dp, axis=1, keepdims=True)
            ds = (e * ((dp - delta) * inv)).astype(BF16)
            dqs = _nn(ds, kb) * SCALE
            dkt = _tn(qs, ds) * LN2
            dvt = _tn((dos.astype(F32) * inv).astype(BF16), e.astype(BF16))
            dsk = -(es * inv * delta)
            for gq in range(QG):
                hcol = (kh * QG + gq) * HD
                dq_ref[:, hcol:hcol + HD] = dqs[gq * BLK:(gq + 1) * BLK].astype(dq_ref.dtype)
                tot = jnp.sum(dsk[gq * BLK:(gq + 1) * BLK], axis=0, keepdims=True)
                dsink = dsink + jnp.where(lane == kh * QG + gq, tot, 0.0)
            hs = slice(kh * HD, (kh + 1) * HD)
            dkm_ref[hs, :] += dkt[:, 0:BLK]
            dvm_ref[hs, :] += dvt[:, 0:BLK]
            for slot in range(3):
                dkp_ref[0, slot, hs, :] = dkt[:, (slot + 1) * BLK:(slot + 2) * BLK]
                dvp_ref[0, slot, hs, :] = dvt[:, (slot + 1) * BLK:(slot + 2) * BLK]
        dsink_ref[...] += dsink

    def kvspec(dn):
        return pl.BlockSpec((BLK, KW), lambda n: (jnp.clip(n + dn, 0, nb - 1), 0))

    meta_spec = pl.BlockSpec((BLK, KW), lambda n: (0, 0))
    rowspec = pl.BlockSpec((BLK, D), lambda n: (n, 0))
    part_spec = pl.BlockSpec((1, 3, KW, BLK), lambda n: (n, 0, 0, 0))
    dq, dkp, dvp, dkm, dvm, dsink = pl.pallas_call(
        kern, name=tag + "_att_bwd", grid=(nb,),
        in_specs=[pl.BlockSpec(memory_space=pltpu.SMEM), rowspec, rowspec,
                  meta_spec, kvspec(-1), kvspec(0), kvspec(1), meta_spec, kvspec(-1), kvspec(0), kvspec(1)],
        out_specs=[rowspec, part_spec, part_spec, _const_spec((KW, BLK)), _const_spec((KW, BLK)),
                   _const_spec((8, 128))],
        out_shape=[jax.ShapeDtypeStruct((LP, D), BF16), jax.ShapeDtypeStruct((nb, 3, KW, BLK), F32),
                   jax.ShapeDtypeStruct((nb, 3, KW, BLK), F32), jax.ShapeDtypeStruct((KW, BLK), F32),
                   jax.ShapeDtypeStruct((KW, BLK), F32), jax.ShapeDtypeStruct((8, 128), F32)],
        scratch_shapes=[pltpu.VMEM((N_KV, QG * BLK, 4 * BLK), F32)], compiler_params=_params("arbitrary"),
    )(sink, q, do, k, k, k, k, v, v, v, v)

    def comb(a_ref, b_ref, c_ref, m_ref, a2_ref, b2_ref, c2_ref, m2_ref, dk_ref, dv_ref):
        mblk = pl.program_id(0)
        has_prev = (mblk > 0).astype(F32)
        has_next = (mblk < nb - 1).astype(F32)
        is0 = (mblk == 0).astype(F32)
        dkt = a_ref[0, 0] * has_prev + b_ref[0, 0] + c_ref[0, 0] * has_next + m_ref[...] * is0
        dvt = a2_ref[0, 0] * has_prev + b2_ref[0, 0] + c2_ref[0, 0] * has_next + m2_ref[...] * is0
        dk_ref[...] = dkt.T.astype(dk_ref.dtype)
        dv_ref[...] = dvt.T.astype(dv_ref.dtype)

    def pspec(dn, slot):
        return pl.BlockSpec((1, 1, KW, BLK), lambda m: (jnp.clip(m + dn, 0, nb - 1), slot, 0, 0))

    kvout = pl.BlockSpec((BLK, KW), lambda m: (m, 0))
    dk, dv = pl.pallas_call(
        comb, name=tag + "_att_dkv", grid=(nb,),
        in_specs=[pspec(-1, 2), pspec(0, 1), pspec(1, 0), _const_spec((KW, BLK)),
                  pspec(-1, 2), pspec(0, 1), pspec(1, 0), _const_spec((KW, BLK))],
        out_specs=[kvout, kvout],
        out_shape=[jax.ShapeDtypeStruct((LP, KW), BF16)] * 2, compiler_params=_params("arbitrary"),
    )(dkp, dkp, dkp, dkm, dvp, dvp, dvp, dvm)
    return dq, dk, dv, dsink


SCAN_LANES = 1024


def _scan_tile(xr, xi, cr, ci, a8, tab, seg, reverse):
    sub = lax.broadcasted_iota(jnp.int32, (8, SCAN_LANES), 0)
    for c0 in range(0, NST, SCAN_LANES):
        cs = pl.ds(c0, SCAN_LANES)
        ar = a8[0, :, cs]
        ai = a8[1, :, cs]

        def rows(j):
            jj = (seg - 1 - j) if reverse else j
            return pl.ds(jj * 8, 8)

        def step1(j, carry):
            vr, vi = carry
            rs = rows(j)
            nr = ar * vr - ai * vi + xr[rs, cs]
            ni = ar * vi + ai * vr + xi[rs, cs]
            xr[rs, cs] = nr
            xi[rs, cs] = ni
            return nr, ni

        zero = jnp.zeros((8, SCAN_LANES), F32)
        vr, vi = zero, zero
        for j in range(seg):
            vr, vi = step1(j, (vr, vi))
        for t, s in enumerate((1, 2, 4)):
            sh = (8 - s) if reverse else s
            sr = pltpu.roll(vr, sh, 0)
            si = pltpu.roll(vi, sh, 0)
            tr = tab[2 * t, :, cs]
            ti = tab[2 * t + 1, :, cs]
            vr, vi = vr + tr * sr - ti * si, vi + tr * si + ti * sr
        pr = tab[6, :, cs]
        pi = tab[7, :, cs]
        c_r = cr[:, cs]
        c_i = ci[:, cs]
        vr, vi = vr + pr * c_r - pi * c_i, vi + pr * c_i + pi * c_r
        edge = 7 if reverse else 0
        last = 0 if reverse else 7
        sh = 7 if reverse else 1
        in_r = jnp.where(sub == edge, c_r, pltpu.roll(vr, sh, 0))
        in_i = jnp.where(sub == edge, c_i, pltpu.roll(vi, sh, 0))
        cr[:, cs] = jnp.broadcast_to(vr[last:last + 1, :], (8, SCAN_LANES))
        ci[:, cs] = jnp.broadcast_to(vi[last:last + 1, :], (8, SCAN_LANES))

        def step2(j, carry):
            dr, di = carry
            rs = rows(j)
            ndr = ar * dr - ai * di
            ndi = ar * di + ai * dr
            xr[rs, cs] += ndr
            xi[rs, cs] += ndi
            return ndr, ndi

        dr, di = in_r, in_i
        for j in range(seg):
            dr, di = step2(j, (dr, di))


ST_T = 4 * SP * 2
CH_T = 128


def _load_segmented(ref, scr, seg):
    out = []
    for ct in range(4):
        scr[ct] = ref[:, ct * CH_T:(ct + 1) * CH_T]
        out.append(jnp.concatenate([scr[ct, pl.ds(j, 8, stride=seg), :] for j in range(seg)], axis=0))
    return out


def _store_segmented(ref, scr, vals, seg):
    for ct in range(4):
        for j in range(seg):
            scr[ct, pl.ds(j, 8, stride=seg), :] = vals[ct][8 * j:8 * j + 8]
        ref[:, ct * CH_T:(ct + 1) * CH_T] = scr[ct]


def ssm_dir_forward(tag, u, bpr, bpi, cpr, cpi, a8, tab, reverse, tm):
    LP = u.shape[0]
    nt = LP // tm
    seg = tm // 8

    def rix(i):
        return (nt - 1 - i) if reverse else i

    def kern(u_ref, bpr_ref, bpi_ref, cpr_ref, cpi_ref, a8_ref, tab_ref, xre_ref, xim_ref, y_ref,
             xr, xi, ys, cr, ci):
        i = pl.program_id(0)

        @pl.when(i == 0)
        def _():
            cr[...] = jnp.zeros_like(cr)
            ci[...] = jnp.zeros_like(ci)

        ub = _load_segmented(u_ref, ys, seg)
        for ct in range(4):
            uc = ub[ct].astype(BF16)
            xr[:, ct * ST_T:(ct + 1) * ST_T] = _nn(uc, bpr_ref[ct * CH_T:(ct + 1) * CH_T, :])
            xi[:, ct * ST_T:(ct + 1) * ST_T] = _nn(uc, bpi_ref[ct * CH_T:(ct + 1) * CH_T, :])
        _scan_tile(xr, xi, cr, ci, a8_ref, tab_ref, seg, reverse)
        xrb = xr[...].astype(BF16)
        xib = xi[...].astype(BF16)
        xre_ref[...] = xrb
        xim_ref[...] = xib
        yv = []
        for ct in range(4):
            ss = slice(ct * ST_T, (ct + 1) * ST_T)
            yv.append(_nn(xrb[:, ss], cpr_ref[ss, :]) - _nn(xib[:, ss], cpi_ref[ss, :]))
        _store_segmented(y_ref, ys, yv, seg)

    row = lambda w: pl.BlockSpec((tm, w), lambda i: (rix(i), 0))
    return pl.pallas_call(
        kern, name=tag, grid=(nt,),
        in_specs=[row(SW), _const_spec(bpr.shape), _const_spec(bpi.shape), _const_spec(cpr.shape),
                  _const_spec(cpi.shape), _const_spec(a8.shape), _const_spec(tab.shape)],
        out_specs=[row(NST), row(NST), row(SW)],
        out_shape=[jax.ShapeDtypeStruct((LP, NST), BF16), jax.ShapeDtypeStruct((LP, NST), BF16),
                   jax.ShapeDtypeStruct((LP, SW), F32)],
        scratch_shapes=[pltpu.VMEM((tm, NST), F32), pltpu.VMEM((tm, NST), F32), pltpu.VMEM((4, tm, CH_T), F32),
                        pltpu.VMEM((8, NST), F32), pltpu.VMEM((8, NST), F32)],
        compiler_params=_params("arbitrary"),
    )(u, bpr, bpi, cpr, cpi, a8, tab)


def ssm_dir_backward(tag, dy, xre, xim, u, bpr, bpi, cpr, cpi, a8_adj, tab_adj, reverse, tm):
    LP = u.shape[0]
    nt = LP // tm
    seg = tm // 8

    def rix(i):
        return (nt - 1 - i) if reverse else i

    def kern(dy_ref, xre_ref, xim_ref, u_ref, bpr_ref, bpi_ref, cpr_ref, cpi_ref, a8_ref, tab_ref,
             du_ref, gbr_ref, gbi_ref, gcr_ref, gci_ref, sr_ref, si_ref, lr, li, gr, gi, dus, cr, ci):
        i = pl.program_id(0)

        @pl.when(i == 0)
        def _():
            cr[...] = jnp.zeros_like(cr)
            ci[...] = jnp.zeros_like(ci)
            for r in (gbr_ref, gbi_ref, gcr_ref, gci_ref, sr_ref, si_ref):
                r[...] = jnp.zeros_like(r)

        dyb = [v.astype(BF16) for v in _load_segmented(dy_ref, dus, seg)]
        ub = [v.astype(BF16) for v in _load_segmented(u_ref, dus, seg)]
        for ct in range(4):
            ss = slice(ct * ST_T, (ct + 1) * ST_T)
            dc = dyb[ct]
            g_re = _nt(dc, cpr_ref[ss, :])
            g_im = -_nt(dc, cpi_ref[ss, :])
            lr[:, ss] = g_re
            li[:, ss] = g_im
            gr[:, ss] = g_re
            gi[:, ss] = g_im
        _scan_tile(lr, li, cr, ci, a8_ref, tab_ref, seg, reverse)
        lam_r = lr[...]
        lam_i = li[...]
        wr = lam_r - gr[...]
        wi = lam_i - gi[...]
        xr = xre_ref[...].astype(F32)
        xi = xim_ref[...].astype(F32)
        sr_ref[...] += jnp.sum(wr * xr + wi * xi, axis=0, keepdims=True)
        si_ref[...] += jnp.sum(wi * xr - wr * xi, axis=0, keepdims=True)
        lrb = lam_r.astype(BF16)
        lib = lam_i.astype(BF16)
        xrb = xre_ref[...]
        xib = xim_ref[...]
        duv = []
        for ct in range(4):
            ss = slice(ct * ST_T, (ct + 1) * ST_T)
            cs = slice(ct * CH_T, (ct + 1) * CH_T)
            duv.append(_nt(lrb[:, ss], bpr_ref[cs, :]) + _nt(lib[:, ss], bpi_ref[cs, :]))
            gbr_ref[cs, :] += _tn(ub[ct], lrb[:, ss])
            gbi_ref[cs, :] += _tn(ub[ct], lib[:, ss])
            gcr_ref[cs, :] += _tn(dyb[ct], xrb[:, ss])
            gci_ref[cs, :] -= _tn(dyb[ct], xib[:, ss])
        _store_segmented(du_ref, dus, duv, seg)

    row = lambda w: pl.BlockSpec((tm, w), lambda i: (rix(i), 0))
    acc = _const_spec((SW, ST_T))
    vec = _const_spec((1, NST))
    return pl.pallas_call(
        kern, name=tag, grid=(nt,),
        in_specs=[row(SW), row(NST), row(NST), row(SW), _const_spec(bpr.shape), _const_spec(bpi.shape),
                  _const_spec(cpr.shape), _const_spec(cpi.shape), _const_spec(a8_adj.shape),
                  _const_spec(tab_adj.shape)],
        out_specs=[row(SW), acc, acc, acc, acc, vec, vec],
        out_shape=[jax.ShapeDtypeStruct((LP, SW), F32)] + [jax.ShapeDtypeStruct((SW, ST_T), F32)] * 4
        + [jax.ShapeDtypeStruct((1, NST), F32)] * 2,
        scratch_shapes=[pltpu.VMEM((tm, NST), F32)] * 4 + [pltpu.VMEM((4, tm, CH_T), F32)]
        + [pltpu.VMEM((8, NST), F32)] * 2,
        compiler_params=_params("arbitrary"),
    )(dy, xre, xim, u, bpr, bpi, cpr, cpi, a8_adj, tab_adj)


def _ssm_disc(lam_re, lam_im, log_dt, b_re, b_im):
    dt = jnp.exp(log_dt)[:, None]
    mag = jnp.exp(lam_re * dt)
    a_re = mag * jnp.cos(lam_im * dt)
    a_im = mag * jnp.sin(lam_im * dt)
    den = lam_re * lam_re + lam_im * lam_im
    f_re = ((a_re - 1.0) * lam_re + a_im * lam_im) / den
    f_im = (a_im * lam_re - (a_re - 1.0) * lam_im) / den
    bb_re = f_re[:, :, None] * b_re - f_im[:, :, None] * b_im
    bb_im = f_re[:, :, None] * b_im + f_im[:, :, None] * b_re
    return a_re, a_im, bb_re, bb_im


def _scan_tables(lam_re, lam_im, log_dt, conj, reverse, seg):
    dt = jnp.exp(log_dt)[:, None]
    lr = (lam_re * dt).reshape(1, NST)
    li = (lam_im * dt).reshape(1, NST) * (-1.0 if conj else 1.0)
    t = jnp.arange(8, dtype=F32)[:, None]

    def power(kk):
        mag = jnp.exp(kk * lr)
        return mag * jnp.cos(kk * li), mag * jnp.sin(kk * li)

    ones = jnp.ones((8, 1), F32)
    a8 = jnp.stack(power(ones)).astype(F32)
    tabs = []
    for s in (1, 2, 4):
        mask = (t <= 7 - s) if reverse else (t >= s)
        pr, pi = power(float(s * seg) * ones)
        tabs += [jnp.where(mask, pr, 0.0), jnp.where(mask, pi, 0.0)]
    kk = ((8.0 - t) if reverse else (t + 1.0)) * float(seg)
    pr, pi = power(kk)
    tabs += [pr, pi]
    return a8, jnp.stack(tabs).astype(F32)


def _pack_b(bb):
    t = bb.transpose(0, 2, 1).reshape(4, 8, SCH, SP)
    eye = jnp.eye(8, dtype=bb.dtype)
    return jnp.einsum('tgcp,gh->tgchp', t, eye).reshape(SW, ST_T)


def _pack_c(cc):
    t = cc.transpose(0, 2, 1).reshape(4, 8, SP, SCH)
    eye = jnp.eye(8, dtype=cc.dtype)
    return jnp.einsum('tgpc,gh->tgphc', t, eye).reshape(NST, CH_T)


def _unpack_diag(acc):
    t = acc.reshape(4, 8, SCH, 8, SP)
    eye = jnp.eye(8, dtype=acc.dtype)
    return jnp.einsum('tgchp,gh->tgcp', t, eye).reshape(SGRP, SCH, SP)


def _gelu(y):
    k0 = math.sqrt(2.0 / math.pi)
    inner = k0 * (y + 0.044715 * y * y * y)
    th = jnp.tanh(inner)
    z = 0.5 * y * (1.0 + th)
    dz = 0.5 * (1.0 + th) + 0.5 * y * (1.0 - th * th) * k0 * (1.0 + 3.0 * 0.044715 * y * y)
    return z, dz


Q0, K0, V0, U0, GS0, GA0, IN_COLS = 0, 1024, 1280, 1536, 2048, 3072, 4096


def mixer_forward(tag, h, p, tm):
    g, winT, wglu, wbsT, wba, wout = p["g"], p["winT"], p["wglu"], p["wbsT"], p["wba"], p["wout"]

    def proj(i, hv, g_ref, w_ref):
        _, _, n = _rms_fwd(hv, g_ref[...])
        nb = n.astype(BF16)
        return (nb, _nt(nb, w_ref[Q0:K0, :]) * QSCALE, _nt(nb, w_ref[K0:V0, :]), _nt(nb, w_ref[V0:U0, :]),
                _nt(nb, w_ref[U0:GS0, :]), _nt(nb, w_ref[GS0:GA0, :]), _nt(nb, w_ref[GA0:IN_COLS, :]))

    n, q, k, v, u, gs, ga = rowcall(
        tag + "_proj", proj, [h], [g, winT],
        [(D, BF16), (D, BF16), (N_KV * HD, BF16), (N_KV * HD, BF16), (SW, F32), (D, F32), (D, F32)], tm=tm)

    ya = attention_forward(tag, q, k, v, p["sink"])

    states, ydir = [], []
    for dr in range(2):
        s = p["ssm"][dr]
        xre, xim, yd = ssm_dir_forward(f"{tag}_ssm_fwd{dr}", u, s["bpr"], s["bpi"], s["cpr"], s["cpi"],
                                       s["a8"], s["tab"], dr == 1, tm)
        states.append((xre, xim))
        ydir.append(yd)

    def glu(i, y0, y1, uv, d_ref, w_ref):
        ypre = y0 + y1 + d_ref[...] * uv
        z, _ = _gelu(ypre)
        zb = z.astype(BF16)
        t = _nn(zb, w_ref[...])
        return ypre, zb, t, z * _sig(t)

    ypre, zb, t, ys = rowcall(tag + "_glu", glu, [ydir[0], ydir[1], u], [p["d"], wglu],
                              [(SW, F32), (SW, BF16), (SW, F32), (SW, BF16)], tm=tm)

    def merge(i, ysv, yav, gsv, gav, wbs_ref, wba_ref):
        bs = _nt(ysv, wbs_ref[...])
        ba = _nn(yav, wba_ref[...])
        mg = _sig(gsv) * bs + _sig(gav) * ba
        mg = jnp.where(_row_ok(i, tm), mg, 0.0)
        return bs, ba, mg

    bs, ba, mg = rowcall(tag + "_merge", merge, [ys, ya, gs, ga], [wbsT, wba],
                         [(D, BF16), (D, BF16), (D, BF16)], tm=tm)

    def outp(i, mv, hv, w_ref):
        return (hv + _nn(mv, w_ref[...]),)

    (h2,) = rowcall(tag + "_out", outp, [mg, h], [wout], [(D, F32)], tm=tm)
    saved = dict(h=h, n=n, q=q, k=k, v=v, u=u, gs=gs, ga=ga, ya=ya, states=states, ypre=ypre, zb=zb, t=t,
                 ys=ys, bs=bs, ba=ba, mg=mg)
    return h2, saved


def mixer_backward(tag, dh, sv, p, tm):
    g, winT, wglu, wbsT, wba, wout = p["g"], p["winT"], p["wglu"], p["wbsT"], p["wba"], p["wout"]

    def y1(i, dhv, bsv, bav, gsv, gav, w_ref):
        dhb = dhv.astype(BF16)
        dmg = _nt(dhb, w_ref[...])
        dmg = jnp.where(_row_ok(i, tm), dmg, 0.0)
        sgs = _sig(gsv)
        sga = _sig(gav)
        return (dmg * sgs, dmg * sga, dmg * bsv.astype(F32) * sgs * (1.0 - sgs),
                dmg * bav.astype(F32) * sga * (1.0 - sga), dhb)

    dbs, dba, dgs, dga, dhb = rowcall(tag + "_bwd_merge", y1, [dh, sv["bs"], sv["ba"], sv["gs"], sv["ga"]], [wout],
                                      [(D, BF16)] * 5, tm=tm)
    dwout = tn_matmul(tag + "_dwout", sv["mg"], dhb)

    def y2(i, dbsv, dbav, wbs_ref, wba_ref):
        return _nn(dbsv, wbs_ref[...]), _nt(dbav, wba_ref[...])

    dys, dya = rowcall(tag + "_bwd_branch", y2, [dbs, dba], [wbsT, wba], [(SW, F32), (D, BF16)], tm=tm)
    dwbsT = tn_matmul(tag + "_dwbs", dbs, sv["ys"])
    dwba = tn_matmul(tag + "_dwba", sv["ya"], dba)

    def s2b(i, dysv, ypv, tv, uv, d_ref, w_ref):
        z, dz_dy = _gelu(ypv)
        st = _sig(tv)
        dt_ = dysv * z * st * (1.0 - st)
        dz = dysv * st + _nt(dt_.astype(BF16), w_ref[...])
        dyp = dz * dz_dy
        return dyp, dyp * d_ref[...], dt_, jnp.sum(dyp * uv, axis=0, keepdims=True)

    dypb, du0, dtb, dd = rowcall(tag + "_bwd_glu", s2b, [dys, sv["ypre"], sv["t"], sv["u"]], [p["d"], wglu],
                                 [(SW, F32), (SW, F32), (SW, BF16)], [(1, SW)], tm=tm)
    dwglu = tn_matmul(tag + "_dwglu", sv["zb"], dtb)

    du_dirs, ssm_sums = [], []
    for dr in range(2):
        s = p["ssm"][dr]
        xre, xim = sv["states"][dr]
        res = ssm_dir_backward(f"{tag}_ssm_bwd{dr}", dypb, xre, xim, sv["u"], s["bpr"], s["bpi"], s["cpr"],
                               s["cpi"], s["a8_adj"], s["tab_adj"], dr == 0, tm)
        du_dirs.append(res[0])
        ssm_sums.append(res[1:])

    dq, dk, dv, dsink = attention_backward(tag, sv["q"], sv["k"], sv["v"], dya, p["sink"])

    def x1b(i, dqv, dkv, dvv, du0v, du1v, du2v, dgsv, dgav, hv, dhv, g_ref, w_ref):
        dub = (du0v + du1v + du2v).astype(BF16)
        dn = (_nn(dqv, w_ref[Q0:K0, :]) + _nn(dkv, w_ref[K0:V0, :]) + _nn(dvv, w_ref[V0:U0, :])
              + _nn(dub, w_ref[U0:GS0, :]) + _nn(dgsv, w_ref[GS0:GA0, :]) + _nn(dgav, w_ref[GA0:IN_COLS, :]))
        gv = g_ref[...]
        hh, r, _ = _rms_fwd(hv, gv)
        dx, dg = _rms_bwd(hh, r, gv, dn)
        dx = jnp.where(_row_ok(i, tm), dx, 0.0)
        return dhv + dx, dub, dg

    dh2, dub, dg = rowcall(tag + "_bwd_in", x1b,
                           [dq, dk, dv, du0, du_dirs[0], du_dirs[1], dgs, dga, sv["h"], dh], [g, winT],
                           [(D, F32), (SW, BF16)], [(1, D)], tm=tm)
    n = sv["n"]
    dwinT = jnp.concatenate([tn_matmul(f"{tag}_dwin{j}", piece, n)
                             for j, piece in enumerate((dq, dk, dv, dub, dgs, dga))], axis=0)
    grads = dict(g=dg, d=dd, sink=dsink, ssm=ssm_sums, winT=dwinT, wglu=dwglu, wbsT=dwbsT, wba=dwba, wout=dwout)
    return dh2, grads


W1024 = ("f1_wgT", "f1_wuT", "f1_wd", "winT", "wba", "wout", "f2_wgT", "f2_wuT", "f2_wd")
W512 = ("wglu", "wbsT")
PART_F1 = ("f1_wgT", "f1_wuT", "f1_wd")
PART_MIX = ("winT", "wba", "wout", "wglu", "wbsT")
PART_F2 = ("f2_wgT", "f2_wuT", "f2_wd")
PER_LAYER_SMALL = ("ffn1_norm", "mix_norm", "ffn2_norm", "ssm_lam_re", "ssm_lam_im", "ssm_log_dt",
                   "ssm_b_re", "ssm_b_im", "ssm_c_re", "ssm_c_im", "ssm_d", "attn_sink")
SMALL = ("ffn1_norm", "mix_norm", "ffn2_norm", "final_norm", "ssm_lam_re", "ssm_lam_im", "ssm_log_dt",
         "ssm_b_re", "ssm_b_im", "ssm_c_re", "ssm_c_im", "ssm_d", "attn_sink")


def kernel(x, meta_tokens, ffn1_norm, ffn1_w_gate, ffn1_w_up, ffn1_w_down, mix_norm, w_in, ssm_lam_re, ssm_lam_im, ssm_log_dt, ssm_b_re, ssm_b_im, ssm_c_re, ssm_c_im, ssm_d, ssm_w_glu, attn_sink, w_branch_ssm, w_branch_attn, w_out, ffn2_norm, ffn2_w_gate, ffn2_w_up, ffn2_w_down, final_norm, loss_target, m_meta_tokens, m_ffn1_norm, m_ffn1_w_gate, m_ffn1_w_up, m_ffn1_w_down, m_mix_norm, m_w_in, m_ssm_lam_re, m_ssm_lam_im, m_ssm_log_dt, m_ssm_b_re, m_ssm_b_im, m_ssm_c_re, m_ssm_c_im, m_ssm_d, m_ssm_w_glu, m_attn_sink, m_w_branch_ssm, m_w_branch_attn, m_w_out, m_ffn2_norm, m_ffn2_w_gate, m_ffn2_w_up, m_ffn2_w_down, m_final_norm, v_meta_tokens, v_ffn1_norm, v_ffn1_w_gate, v_ffn1_w_up, v_ffn1_w_down, v_mix_norm, v_w_in, v_ssm_lam_re, v_ssm_lam_im, v_ssm_log_dt, v_ssm_b_re, v_ssm_b_im, v_ssm_c_re, v_ssm_c_im, v_ssm_d, v_ssm_w_glu, v_attn_sink, v_w_branch_ssm, v_w_branch_attn, v_w_out, v_ffn2_norm, v_ffn2_w_gate, v_ffn2_w_up, v_ffn2_w_down, v_final_norm):
    weights = dict(meta_tokens=meta_tokens, ffn1_norm=ffn1_norm, ffn1_w_gate=ffn1_w_gate, ffn1_w_up=ffn1_w_up, ffn1_w_down=ffn1_w_down, mix_norm=mix_norm, w_in=w_in, ssm_lam_re=ssm_lam_re, ssm_lam_im=ssm_lam_im, ssm_log_dt=ssm_log_dt, ssm_b_re=ssm_b_re, ssm_b_im=ssm_b_im, ssm_c_re=ssm_c_re, ssm_c_im=ssm_c_im, ssm_d=ssm_d, ssm_w_glu=ssm_w_glu, attn_sink=attn_sink, w_branch_ssm=w_branch_ssm, w_branch_attn=w_branch_attn, w_out=w_out, ffn2_norm=ffn2_norm, ffn2_w_gate=ffn2_w_gate, ffn2_w_up=ffn2_w_up, ffn2_w_down=ffn2_w_down, final_norm=final_norm)
    mom_m = dict(meta_tokens=m_meta_tokens, ffn1_norm=m_ffn1_norm, ffn1_w_gate=m_ffn1_w_gate, ffn1_w_up=m_ffn1_w_up, ffn1_w_down=m_ffn1_w_down, mix_norm=m_mix_norm, w_in=m_w_in, ssm_lam_re=m_ssm_lam_re, ssm_lam_im=m_ssm_lam_im, ssm_log_dt=m_ssm_log_dt, ssm_b_re=m_ssm_b_re, ssm_b_im=m_ssm_b_im, ssm_c_re=m_ssm_c_re, ssm_c_im=m_ssm_c_im, ssm_d=m_ssm_d, ssm_w_glu=m_ssm_w_glu, attn_sink=m_attn_sink, w_branch_ssm=m_w_branch_ssm, w_branch_attn=m_w_branch_attn, w_out=m_w_out, ffn2_norm=m_ffn2_norm, ffn2_w_gate=m_ffn2_w_gate, ffn2_w_up=m_ffn2_w_up, ffn2_w_down=m_ffn2_w_down, final_norm=m_final_norm)
    mom_v = dict(meta_tokens=v_meta_tokens, ffn1_norm=v_ffn1_norm, ffn1_w_gate=v_ffn1_w_gate, ffn1_w_up=v_ffn1_w_up, ffn1_w_down=v_ffn1_w_down, mix_norm=v_mix_norm, w_in=v_w_in, ssm_lam_re=v_ssm_lam_re, ssm_lam_im=v_ssm_lam_im, ssm_log_dt=v_ssm_log_dt, ssm_b_re=v_ssm_b_re, ssm_b_im=v_ssm_b_im, ssm_c_re=v_ssm_c_re, ssm_c_im=v_ssm_c_im, ssm_d=v_ssm_d, ssm_w_glu=v_ssm_w_glu, attn_sink=v_attn_sink, w_branch_ssm=v_w_branch_ssm, w_branch_attn=v_w_branch_attn, w_out=v_w_out, ffn2_norm=v_ffn2_norm, ffn2_w_gate=v_ffn2_w_gate, ffn2_w_up=v_ffn2_w_up, ffn2_w_down=v_ffn2_w_down, final_norm=v_final_norm)
    names = list(weights)

    L0 = x.shape[1]
    LP = L0 + BLK
    tm = 384 if LP % 384 == 0 else BLK
    x_i, y_i, c_i = lax.axis_index("x"), lax.axis_index("y"), lax.axis_index("c")
    me = 4 * x_i + 2 * y_i + c_i

    def canon(l):
        return dict(
            f1_wgT=ffn1_w_gate[l].T, f1_wuT=ffn1_w_up[l].T, f1_wd=ffn1_w_down[l],
            winT=w_in[l].T, wba=w_branch_attn[l], wout=w_out[l],
            f2_wgT=ffn2_w_gate[l].T, f2_wuT=ffn2_w_up[l].T, f2_wd=ffn2_w_down[l],
            wglu=ssm_w_glu[l], wbsT=w_branch_ssm[l].T)

    shards = [{nm: a.astype(BF16) for nm, a in canon(l).items()} for l in range(DEPTH)]

    def rows_of(nm):
        return shards[0][nm].shape[0]

    def width_groups(names_):
        return [g for g in ([nm for nm in names_ if nm in W1024], [nm for nm in names_ if nm in W512]) if g]

    def pieces_for(group):
        out, off = [], 0
        for nm in group:
            out.append((off, rows_of(nm)))
            off += rows_of(nm)
        return out

    def start_gather(tag, l, names_):
        groups = width_groups(names_)
        packed = [jnp.concatenate([shards[l][nm] for nm in g], axis=0) for g in groups]
        return gather_layer_start(tag, packed, [pieces_for(g) for g in groups]), groups

    def finish_gather(tag, l, started_, after):
        handle, groups = started_
        dests = gather_layer_wait(tag, handle, len(groups), after)
        out = {}
        for nm, dest in zip([nm for g in groups for nm in g], dests):
            sh = shards[l][nm]
            out[nm] = lax.dynamic_update_slice(dest, sh, (me * sh.shape[0], 0))
        return out

    g1, gm = all_gather_pieces(
        "gather_weights_first",
        [(jnp.concatenate([shards[0][nm] for nm in PART_F1], axis=0), pieces_for(PART_F1)),
         (meta_tokens, [(0, N_META)])])
    first_weights = dict(zip(PART_F1, g1))
    meta_full = gm[0].reshape(NDEV, N_META, D // NDEV).transpose(1, 0, 2).reshape(N_META, D)
    gather_started = [start_gather("gather_start_l0", 0, PART_MIX + PART_F2)]
    gather_started += [start_gather(f"gather_start_l{l}", l, W1024 + W512) for l in range(1, DEPTH)]
    started = sum(st[0][3][0, 0] for st in gather_started)
    full = [None] * DEPTH

    def disc_all(lre, lim, ldt, bre, bim):
        return _ssm_disc(lre, lim, ldt, bre, bim)

    ssm_p, ssm_vjp = [], []
    for l in range(DEPTH):
        row, vrow = [], []
        for dr in range(2):
            args = (ssm_lam_re[l, dr], ssm_lam_im[l, dr], ssm_log_dt[l, dr], ssm_b_re[l, dr], ssm_b_im[l, dr])
            (a_re, a_im, bb_re, bb_im), vjp = jax.vjp(disc_all, *args)
            a8, tab = _scan_tables(args[0], args[1], args[2], False, dr == 1, tm // 8)
            a8_adj, tab_adj = _scan_tables(args[0], args[1], args[2], True, dr == 0, tm // 8)
            row.append(dict(
                bpr=_pack_b(bb_re).astype(BF16), bpi=_pack_b(bb_im).astype(BF16),
                cpr=_pack_c(ssm_c_re[l, dr]).astype(BF16), cpi=_pack_c(ssm_c_im[l, dr]).astype(BF16),
                a8=a8, tab=tab, a8_adj=a8_adj, tab_adj=tab_adj, a_re=a_re, a_im=a_im))
            vrow.append(vjp)
        ssm_p.append(row)
        ssm_vjp.append(vrow)

    blk0 = jnp.concatenate([jnp.zeros((PAD, D), F32), meta_full.astype(F32)], axis=0)
    h = build_h0(x[0], blk0)
    saved = []
    for l in range(DEPTH):
        w = dict(first_weights) if l == 0 else finish_gather(f"gather_wait_l{l}", l, gather_started[l], h)
        full[l] = w
        g1n, g2n = ffn1_norm[l][None, :], ffn2_norm[l][None, :]
        if l == 0:
            g1n = g1n + started
        h, s1 = ffn_forward("ffn1", h, g1n, w["f1_wgT"], w["f1_wuT"], w["f1_wd"], tm)
        if l == 0:
            w.update(finish_gather("gather_wait_l0", 0, gather_started[0], h))
        mp = dict(g=mix_norm[l][None, :], winT=w["winT"], wglu=w["wglu"], wbsT=w["wbsT"], wba=w["wba"],
                  wout=w["wout"], d=ssm_d[l][None, :], sink=attn_sink[l], ssm=ssm_p[l])
        h, s2 = mixer_forward("mix", h, mp, tm)
        h, s3 = ffn_forward("ffn2", h, g2n, w["f2_wgT"], w["f2_wuT"], w["f2_wd"], tm)
        saved.append((s1, s2, s3, mp, g1n, g2n))

    dh, loss_acc, dgf = final_loss(h, loss_target[0], final_norm[None, :])
    loss = lax.psum(loss_acc[0, 0], MESH_AXES)

    small = {nm: [None] * DEPTH for nm in SMALL if nm != "final_norm"}

    def start_scatter(tag, grads_d, names_):
        groups = width_groups(names_)
        mine = [jnp.concatenate([lax.dynamic_slice_in_dim(grads_d[nm], me * rows_of(nm), rows_of(nm), axis=0)
                                 for nm in g], axis=0) for g in groups]
        handle, nin = scatter_layer_start(tag + "_start", [[grads_d[nm] for nm in g] for g in groups])
        return dict(tag=tag, handle=handle, nin=nin, groups=groups, mine=mine)

    def finish_scatter(st, after):
        lands = scatter_layer_wait(st["tag"] + "_wait", st["handle"], st["nin"], len(st["groups"]), after)
        out = {}
        for land, mine, g in zip(lands, st["mine"], st["groups"]):
            land = lax.dynamic_update_slice(land, mine[None], (me, 0, 0))
            tot = sum_slots(f"sum_weight_grads_{land.shape[1]}x{land.shape[2]}", land)
            for nm, (off_, r) in zip(g, pieces_for(g)):
                out[nm] = tot[off_:off_ + r]
        return out

    scatters = []
    sent = jnp.zeros((), F32)
    for l in reversed(range(DEPTH)):
        s1, s2, s3, mp, g1n, g2n = saved[l]
        w = full[l]
        dh, dg2, f2g, f2u, f2d = ffn_backward("ffn2", dh, s3, g2n + sent, w["f2_wgT"], w["f2_wuT"], w["f2_wd"], tm)
        st = start_scatter(f"scatter_l{l}_f2", dict(f2_wgT=f2g, f2_wuT=f2u, f2_wd=f2d), PART_F2)
        scatters.append((l, st))
        dh, mg = mixer_backward("mix", dh, s2, dict(mp, d=mp["d"] + st["handle"][3][0, 0]), tm)
        st = start_scatter(f"scatter_l{l}_mix", mg, PART_MIX)
        scatters.append((l, st))
        dh, dg1, f1g, f1u, f1d = ffn_backward("ffn1", dh, s1, g1n + st["handle"][3][0, 0],
                                              w["f1_wgT"], w["f1_wuT"], w["f1_wd"], tm)
        st = start_scatter(f"scatter_l{l}_f1", dict(f1_wgT=f1g, f1_wuT=f1u, f1_wd=f1d), PART_F1)
        scatters.append((l, st))
        sent = st["handle"][3][0, 0]
        small["ffn1_norm"][l] = dg1[0]
        small["mix_norm"][l] = mg["g"][0]
        small["ffn2_norm"][l] = dg2[0]
        small["ssm_d"][l] = mg["d"][0]
        small["attn_sink"][l] = mg["sink"][0, :N_HEADS]
        per_dir = {k: [] for k in ("ssm_lam_re", "ssm_lam_im", "ssm_log_dt", "ssm_b_re", "ssm_b_im",
                                   "ssm_c_re", "ssm_c_im")}
        for dr in range(2):
            gbr, gbi, gcr, gci, s_re, s_im = mg["ssm"][dr]
            a_re, a_im = ssm_p[l][dr]["a_re"], ssm_p[l][dr]["a_im"]
            s_re = s_re.reshape(SGRP, SP)
            s_im = s_im.reshape(SGRP, SP)
            den = a_re * a_re + a_im * a_im
            ga_re = (s_re * a_re - s_im * a_im) / den
            ga_im = (s_re * a_im + s_im * a_re) / den
            glr, gli, gld, gbre, gbim = ssm_vjp[l][dr]((ga_re, ga_im, _unpack_diag(gbr).transpose(0, 2, 1),
                                                        _unpack_diag(gbi).transpose(0, 2, 1)))
            per_dir["ssm_lam_re"].append(glr)
            per_dir["ssm_lam_im"].append(gli)
            per_dir["ssm_log_dt"].append(gld)
            per_dir["ssm_b_re"].append(gbre)
            per_dir["ssm_b_im"].append(gbim)
            per_dir["ssm_c_re"].append(_unpack_diag(gcr))
            per_dir["ssm_c_im"].append(_unpack_diag(gci))
        for k, vlist in per_dir.items():
            small[k][l] = jnp.stack(vlist)

    grad_x = dh[BLK:][None]
    dmeta_part = dh[PAD:BLK]

    small_part = {k: jnp.stack(vv) for k, vv in small.items()}
    small_part["final_norm"] = dgf[0]
    pieces = [small_part[k].reshape(-1) for k in SMALL] + [dmeta_part.reshape(-1)]
    sizes = [p_.shape[0] for p_ in pieces]
    total = sum(sizes)
    rows_s = -(-total // (8 * D)) * 8
    flat = jnp.concatenate(pieces + [jnp.zeros((rows_s * D - total,), F32)]).reshape(rows_s, D)
    ((gathered,),) = all_gather_pieces("gather_small_grads", [(flat, [(0, rows_s)])])
    small_sum = sum_slots("sum_small_grads", gathered.reshape(NDEV, rows_s, D)).reshape(-1)
    grads = {}
    o = 0
    for k, sz in zip(SMALL, sizes[:-1]):
        grads[k] = small_sum[o:o + sz].reshape(weights[k].shape)
        o += sz
    dmeta_full = small_sum[o:o + N_META * D].reshape(N_META, D)
    grads["meta_tokens"] = lax.dynamic_slice_in_dim(dmeta_full, me * (D // NDEV), D // NDEV, axis=1)

    own = [dict() for _ in range(DEPTH)]
    for l, st in scatters:
        own[l].update(finish_scatter(st, dh))

    def stack(fn):
        return jnp.stack([fn(own[l]) for l in range(DEPTH)])

    grads["ffn1_w_gate"] = stack(lambda d: d["f1_wgT"].T)
    grads["ffn1_w_up"] = stack(lambda d: d["f1_wuT"].T)
    grads["ffn1_w_down"] = stack(lambda d: d["f1_wd"])
    grads["w_in"] = stack(lambda d: d["winT"].T)
    grads["ssm_w_glu"] = stack(lambda d: d["wglu"])
    grads["w_branch_ssm"] = stack(lambda d: d["wbsT"].T)
    grads["w_branch_attn"] = stack(lambda d: d["wba"])
    grads["w_out"] = stack(lambda d: d["wout"])
    grads["ffn2_w_gate"] = stack(lambda d: d["f2_wgT"].T)
    grads["ffn2_w_up"] = stack(lambda d: d["f2_wuT"].T)
    grads["ffn2_w_down"] = stack(lambda d: d["f2_wd"])

    deltas, new_m, new_v = {}, {}, {}
    for nm in names:
        deltas[nm], new_m[nm], new_v[nm] = adamw("adamw_" + nm, weights[nm], grads[nm], mom_m[nm], mom_v[nm])

    return (loss, grad_x, *[grads[n] for n in names], *[deltas[n] for n in names],
            *[new_m[n] for n in names], *[new_v[n] for n in names])
```

```python
import functools
import math

import jax
import jax.numpy as jnp
from jax import lax
from jax.experimental import pallas as pl
from jax.experimental.pallas import tpu as pltpu

F32 = jnp.float32
BF16 = jnp.bfloat16

D = 1024
DFF = 2816
N_META = 16
N_HEADS = 16
N_KV = 4
HD = 64
QG = 4
WIN = 128
BLK = 128
PAD = BLK - N_META
SW = 512
SGRP = 32
SCH = 16
SP = 64
NST = SGRP * SP
EPS = 1e-6
NEG = -1e30
SCALE = HD ** -0.5
NDEV = 8
DEPTH = 4
MESH_AXES = ("x", "y", "c")
MESH = pl.DeviceIdType.MESH

ADAM_LR = 0.001
ADAM_B1 = 0.9
ADAM_B2 = 0.999
ADAM_EPS = 1e-08
ADAM_WD = 0.01
ADAM_STEP = 10

VMEM_LIMIT = 56 * 1024 * 1024


def _params(*sem):
    return pltpu.CompilerParams(dimension_semantics=sem, vmem_limit_bytes=VMEM_LIMIT)


def _nn(a, b):
    return lax.dot_general(a, b, (((1,), (0,)), ((), ())), preferred_element_type=F32)


def _nt(a, b):
    return lax.dot_general(a, b, (((1,), (1,)), ((), ())), preferred_element_type=F32)


def _tn(a, b):
    return lax.dot_general(a, b, (((0,), (0,)), ((), ())), preferred_element_type=F32)


def _sig(x):
    return 0.5 * jnp.tanh(0.5 * x) + 0.5


def _rms_fwd(h, g):
    r = lax.rsqrt(jnp.mean(h * h, axis=-1, keepdims=True) + EPS)
    hh = h * r
    return hh, r, hh * g


def _rms_bwd(hh, r, g, dn):
    dhh = dn * g
    dx = r * (dhh - hh * jnp.mean(dhh * hh, axis=-1, keepdims=True))
    return dx, jnp.sum(dn * hh, axis=0, keepdims=True)


def _row_ok(i, tm):
    rows = i * tm + lax.broadcasted_iota(jnp.int32, (tm, 1), 0)
    return rows >= PAD


def _const_spec(shape, single=False):
    nd = len(shape)
    if single:
        return pl.BlockSpec(shape, lambda *_: (0,) * nd, pipeline_mode=pl.Buffered(1))
    return pl.BlockSpec(shape, lambda *_: (0,) * nd)


def rowcall(name, body, rows, consts, outs, accs=(), *, tm):
    nrows = rows[0].shape[0]
    nt = nrows // tm
    assert nt * tm == nrows, (name, nrows, tm)
    nr, nc, no, na = len(rows), len(consts), len(outs), len(accs)
    in_specs = [pl.BlockSpec((tm, r.shape[1]), lambda i: (i, 0)) for r in rows]
    in_specs += [_const_spec(c.shape, single=True) for c in consts]
    out_shape = [jax.ShapeDtypeStruct((nrows, w), dt) for (w, dt) in outs]
    out_specs = [pl.BlockSpec((tm, w), lambda i: (i, 0)) for (w, dt) in outs]
    out_shape += [jax.ShapeDtypeStruct(s, F32) for s in accs]
    out_specs += [_const_spec(s) for s in accs]

    def kern(*refs):
        i = pl.program_id(0)
        row_vals = [r[...] for r in refs[:nr]]
        res = body(i, *row_vals, *refs[nr:nr + nc])
        out_refs = refs[nr + nc:nr + nc + no]
        acc_refs = refs[nr + nc + no:]
        for r, v in zip(out_refs, res[:no]):
            r[...] = v.astype(r.dtype)
        if na:
            @pl.when(i == 0)
            def _():
                for r in acc_refs:
                    r[...] = jnp.zeros_like(r)
            for r, v in zip(acc_refs, res[no:]):
                r[...] += v

    res = pl.pallas_call(
        kern, name=name, grid=(nt,), in_specs=in_specs, out_specs=out_specs, out_shape=out_shape,
        compiler_params=_params("arbitrary"),
    )(*rows, *consts)
    return res


def tn_matmul(name, lhs, rhs, scale=1.0):
    M, K = lhs.shape
    N = rhs.shape[1]
    assert lhs.dtype == BF16 and rhs.dtype == BF16
    nm = 6
    tmw = M // nm
    assert tmw * nm == M and tmw % 16 == 0
    tk = 1408 if (K % 1408 == 0) else K
    nk = K // tk

    def kern(a_ref, b_ref, o_ref, acc):
        m = pl.program_id(1)
        part = _tn(a_ref[...], b_ref[...])

        @pl.when(m == 0)
        def _():
            acc[...] = part

        @pl.when((m > 0) & (m < nm - 1))
        def _():
            acc[...] += part

        @pl.when(m == nm - 1)
        def _():
            o_ref[...] = ((acc[...] + part) * scale).astype(o_ref.dtype)

    return pl.pallas_call(
        kern, name=name, grid=(nk, nm),
        in_specs=[pl.BlockSpec((tmw, tk), lambda k, m: (m, k)), pl.BlockSpec((tmw, N), lambda k, m: (m, 0))],
        out_specs=pl.BlockSpec((tk, N), lambda k, m: (k, 0)),
        out_shape=jax.ShapeDtypeStruct((K, N), BF16),
        scratch_shapes=[pltpu.VMEM((tk, N), F32)],
        compiler_params=_params("arbitrary", "arbitrary"),
    )(lhs, rhs)


def _mesh_pos():
    x, y, c = lax.axis_index("x"), lax.axis_index("y"), lax.axis_index("c")
    return x, y, c


def all_gather_pieces(name, groups):
    ng = len(groups)
    packed = [g[0] for g in groups]
    pieces = [g[1] for g in groups]
    out_shape, out_map = [], []
    for gi, (p, pcs) in enumerate(groups):
        idx = []
        for (off, r) in pcs:
            idx.append(len(out_shape))
            out_shape.append(jax.ShapeDtypeStruct((NDEV * r, p.shape[1]), p.dtype))
        out_map.append(idx)
    nout = len(out_shape)

    def body(*refs):
        p_refs = refs[:ng]
        o_refs = refs[ng:ng + nout]
        send_sems, recv_sems, local_sems = refs[ng + nout:]
        x, y, c = _mesh_pos()
        me = (x, y, c)
        sibling = (x, y, 1 - c)
        chips = [(1 - x, y), (x, 1 - y), (1 - x, 1 - y)]

        def blk(px, py, pc):
            return 4 * px + 2 * py + pc

        def copies(gi, k, origin, to, from_out):
            cps = []
            for (off, r), oi in zip(pieces[gi], out_map[gi]):
                dst = o_refs[oi].at[pl.ds(origin * r, r), :]
                src = dst if from_out else p_refs[gi].at[pl.ds(off, r), :]
                cps.append(pltpu.make_async_remote_copy(
                    src_ref=src, dst_ref=dst, send_sem=send_sems.at[gi, k], recv_sem=recv_sems.at[gi, k],
                    device_id=to, device_id_type=MESH))
            return cps

        def whole(gi, k):
            return pltpu.make_async_remote_copy(
                src_ref=p_refs[gi], dst_ref=p_refs[gi], send_sem=send_sems.at[gi, k],
                recv_sem=recv_sems.at[gi, k], device_id=me, device_id_type=MESH)

        mine = []
        for gi in range(ng):
            for (off, r), oi in zip(pieces[gi], out_map[gi]):
                mine.append(pltpu.make_async_copy(
                    p_refs[gi].at[pl.ds(off, r), :], o_refs[oi].at[pl.ds(blk(*me) * r, r), :],
                    local_sems.at[gi]))
        for cp in mine:
            cp.start()
        for gi in range(ng):
            for cp in copies(gi, 0, blk(*me), sibling, False):
                cp.start()
            for j, chip in enumerate(chips):
                for cp in copies(gi, 1 + j, blk(*me), (*chip, c), False):
                    cp.start()
        for j, chip in enumerate(chips):
            for gi in range(ng):
                whole(gi, 1 + j).wait_recv()
                for cp in copies(gi, 4 + j, blk(*chip, c), sibling, True):
                    cp.start()
        for gi in range(ng):
            whole(gi, 0).wait_recv()
            for j in range(3):
                whole(gi, 4 + j).wait_recv()
        for gi in range(ng):
            for k in range(7):
                whole(gi, k).wait_send()
            pltpu.make_async_copy(p_refs[gi], p_refs[gi], local_sems.at[gi]).wait()

    any_spec = pl.BlockSpec(memory_space=pl.ANY)
    outs = pl.pallas_call(
        body, name=name, out_shape=out_shape,
        in_specs=[any_spec] * ng, out_specs=[any_spec] * nout,
        scratch_shapes=[pltpu.SemaphoreType.DMA((ng, 7)), pltpu.SemaphoreType.DMA((ng, 7)),
                        pltpu.SemaphoreType.DMA((ng,))],
    )(*packed)
    return [[outs[oi] for oi in idx] for idx in out_map]


HBM_SPEC = pl.BlockSpec(memory_space=pltpu.HBM)
SEM_SPEC = pl.BlockSpec(memory_space=pltpu.SEMAPHORE)
DATAFLOW = pltpu.SideEffectType.DATAFLOW_SIDE_EFFECTING


def _peers(x, y, c):
    return [(x, y, 1 - c), (1 - x, y, c), (x, 1 - y, c), (1 - x, 1 - y, c),
            (1 - x, y, 1 - c), (x, 1 - y, 1 - c), (1 - x, 1 - y, 1 - c)]


def exchange_start(name, arrays, ng, plan):
    n = len(arrays)
    ns = ng * 7

    def body(*refs):
        in_refs = refs[:n]
        send_sems, recv_sems = refs[n:n + ns], refs[n + ns:n + 2 * ns]
        token = refs[-1]
        x, y, c = _mesh_pos()
        me_i = 4 * x + 2 * y + c
        for k, peer in enumerate(_peers(x, y, c)):
            p_i = 4 * peer[0] + 2 * peer[1] + peer[2]
            for src, dst, gi in plan(in_refs, me_i, p_i):
                pltpu.make_async_remote_copy(
                    src_ref=src, dst_ref=dst, send_sem=send_sems[gi * 7 + k], recv_sem=recv_sems[gi * 7 + k],
                    device_id=peer, device_id_type=MESH).start()
        token[...] = jnp.zeros_like(token)

    res = pl.pallas_call(
        body, name=name,
        out_shape=(*[pltpu.SemaphoreType.DMA(())] * (2 * ns),
                   *[pltpu.HBM(a.shape, a.dtype) for a in arrays], jax.ShapeDtypeStruct((8, 128), F32)),
        in_specs=[HBM_SPEC] * n,
        out_specs=(*[SEM_SPEC] * (2 * ns), *[HBM_SPEC] * n, pl.BlockSpec(memory_space=pltpu.VMEM)),
        input_output_aliases={i: 2 * ns + i for i in range(n)},
        compiler_params=pltpu.CompilerParams(has_side_effects=DATAFLOW),
    )(*[pltpu.with_memory_space_constraint(a, pltpu.HBM) for a in arrays])
    return list(res[:ns]), list(res[ns:2 * ns]), list(res[2 * ns:2 * ns + n]), res[-1]


def exchange_wait(name, send_sems, recv_sems, arrays, ng, sized, after):
    n = len(arrays)
    ns = ng * 7

    def body(*refs):
        in_refs = refs[:n]
        s_sems, r_sems = refs[n:n + ns], refs[n + ns:n + 2 * ns]
        x, y, c = _mesh_pos()
        for gi in range(ng):
            view = sized(in_refs, gi)
            for k in range(7):
                w = pltpu.make_async_remote_copy(
                    src_ref=view, dst_ref=view, send_sem=s_sems[gi * 7 + k], recv_sem=r_sems[gi * 7 + k],
                    device_id=(x, y, c), device_id_type=MESH)
                w.wait_send()
                w.wait_recv()

    res = pl.pallas_call(
        body, name=name, out_shape=tuple(pltpu.HBM(a.shape, a.dtype) for a in arrays),
        in_specs=[HBM_SPEC] * n + [SEM_SPEC] * (2 * ns) + [pl.BlockSpec(memory_space=pl.ANY)],
        out_specs=tuple([HBM_SPEC] * n), input_output_aliases={i: i for i in range(n)},
        compiler_params=pltpu.CompilerParams(has_side_effects=DATAFLOW),
    )(*arrays, *send_sems, *recv_sems, after)
    return list(res)


def gather_layer_start(name, packed, pieces):
    ng = len(packed)
    dests = [lax.empty((NDEV * r, p.shape[1]), p.dtype) for p, pcs in zip(packed, pieces) for (_, r) in pcs]

    def plan(refs, me_i, p_i):
        out, di = [], ng
        for gi in range(ng):
            for (off, r) in pieces[gi]:
                out.append((refs[gi].at[pl.ds(off, r), :], refs[di].at[pl.ds(me_i * r, r), :], gi))
                di += 1
        return out

    return exchange_start(name, list(packed) + dests, ng, plan)


def gather_layer_wait(name, handle, ng, after):
    send_sems, recv_sems, arrays, _ = handle
    out = exchange_wait(name, send_sems, recv_sems, arrays, ng, lambda refs, gi: refs[gi], after)
    return out[ng:]


def scatter_layer_start(name, groups):
    ng = len(groups)
    flat = [a for arrs in groups for a in arrs]
    offs, lands = [], []
    for arrs in groups:
        o, off = [], 0
        for a in arrs:
            r = a.shape[0] // NDEV
            o.append((off, r))
            off += r
        offs.append(o)
        lands.append(jnp.zeros((NDEV, off, arrs[0].shape[1]), arrs[0].dtype))
    nin = len(flat)

    def plan(refs, me_i, p_i):
        out, ai = [], 0
        for gi in range(ng):
            for (off, r) in offs[gi]:
                out.append((refs[ai].at[pl.ds(p_i * r, r), :], refs[nin + gi].at[me_i, pl.ds(off, r), :], gi))
                ai += 1
        return out

    return exchange_start(name, flat + lands, ng, plan), nin


def scatter_layer_wait(name, handle, nin, ng, after):
    send_sems, recv_sems, arrays, _ = handle
    out = exchange_wait(name, send_sems, recv_sems, arrays, ng, lambda refs, gi: refs[nin + gi].at[0], after)
    return out[nin:]


def _pick_tile(n, cap):
    best = None
    for t in range(8, min(n, cap) + 1, 8):
        if n % t == 0:
            best = t
    return best if best is not None else n


def sum_slots(name, land):
    _, R, W = land.shape
    tr = _pick_tile(R, 512)

    def kern(l_ref, o_ref):
        acc = l_ref[0].astype(F32)
        for s in range(1, NDEV):
            acc = acc + l_ref[s].astype(F32)
        o_ref[...] = acc

    return pl.pallas_call(
        kern, name=name, grid=(R // tr,),
        in_specs=[pl.BlockSpec((NDEV, tr, W), lambda i: (0, i, 0))],
        out_specs=pl.BlockSpec((tr, W), lambda i: (i, 0)),
        out_shape=jax.ShapeDtypeStruct((R, W), F32),
        compiler_params=_params("arbitrary"),
    )(land)


def adamw(name, w, g, m, v):
    shp = w.shape
    C = shp[-1]
    R = max(1, math.prod(shp[:-1]))
    tr = _pick_tile(R, 1024)
    w2, g2, m2, v2 = (a.reshape(R, C) for a in (w, g, m, v))

    def kern(w_ref, g_ref, m_ref, v_ref, d_ref, mo_ref, vo_ref):
        gg = g_ref[...]
        mn = ADAM_B1 * m_ref[...] + (1.0 - ADAM_B1) * gg
        vn = ADAM_B2 * v_ref[...] + (1.0 - ADAM_B2) * jnp.square(gg)
        m_hat = mn / (1.0 - ADAM_B1 ** ADAM_STEP)
        v_hat = vn / (1.0 - ADAM_B2 ** ADAM_STEP)
        d_ref[...] = -ADAM_LR * (m_hat / (jnp.sqrt(v_hat) + ADAM_EPS) + ADAM_WD * w_ref[...])
        mo_ref[...] = mn
        vo_ref[...] = vn

    spec = pl.BlockSpec((tr, C), lambda i: (i, 0))
    d, mo, vo = pl.pallas_call(
        kern, name=name, grid=(R // tr,), in_specs=[spec] * 4, out_specs=[spec] * 3,
        out_shape=[jax.ShapeDtypeStruct((R, C), F32)] * 3, compiler_params=_params("arbitrary"),
    )(w2, g2, m2, v2)
    return d.reshape(shp), mo.reshape(shp), vo.reshape(shp)


def build_h0(x2, blk0):
    L0 = x2.shape[0]
    nb = L0 // BLK + 1

    def kern(x_ref, b_ref, o_ref):
        i = pl.program_id(0)

        @pl.when(i == 0)
        def _():
            o_ref[...] = b_ref[...]

        @pl.when(i > 0)
        def _():
            o_ref[...] = x_ref[...]

    return pl.pallas_call(
        kern, name="build_h0", grid=(nb,),
        in_specs=[pl.BlockSpec((BLK, D), lambda i: (jnp.maximum(i - 1, 0), 0)), _const_spec((BLK, D))],
        out_specs=pl.BlockSpec((BLK, D), lambda i: (i, 0)),
        out_shape=jax.ShapeDtypeStruct((L0 + BLK, D), F32), compiler_params=_params("arbitrary"),
    )(x2, blk0)


def final_loss(h, tgt, gf):
    LP = h.shape[0]
    nb = LP // BLK

    def kern(h_ref, t_ref, g_ref, dh_ref, loss_ref, dg_ref):
        i = pl.program_id(0)

        @pl.when(i == 0)
        def _():
            loss_ref[...] = jnp.zeros_like(loss_ref)
            dg_ref[...] = jnp.zeros_like(dg_ref)

        g = g_ref[...]
        hh, r, yv = _rms_fwd(h_ref[...], g)
        valid = (i > 0).astype(F32)
        err = (yv - t_ref[...]) * valid
        loss_ref[...] += 0.5 * jnp.sum(jnp.sum(err * err, axis=1, keepdims=True), axis=0, keepdims=True) / D
        dy = err / D
        dx, dg = _rms_bwd(hh, r, g, dy)
        dh_ref[...] = dx
        dg_ref[...] += dg

    return pl.pallas_call(
        kern, name="final_loss", grid=(nb,),
        in_specs=[pl.BlockSpec((BLK, D), lambda i: (i, 0)),
                  pl.BlockSpec((BLK, D), lambda i: (jnp.maximum(i - 1, 0), 0)), _const_spec((1, D))],
        out_specs=[pl.BlockSpec((BLK, D), lambda i: (i, 0)), _const_spec((8, 128)), _const_spec((1, D))],
        out_shape=[jax.ShapeDtypeStruct((LP, D), F32), jax.ShapeDtypeStruct((8, 128), F32),
                   jax.ShapeDtypeStruct((1, D), F32)],
        compiler_params=_params("arbitrary"),
    )(h, tgt, gf)


def ffn_forward(tag, h, g, wgT, wuT, wd, tm):
    def f1(i, hv, g_ref, wg_ref, wu_ref):
        _, _, n = _rms_fwd(hv, g_ref[...])
        nb = n.astype(BF16)
        G = _nt(nb, wg_ref[...])
        U = _nt(nb, wu_ref[...])
        A = G * _sig(G) * U
        return nb, G, U, A

    n, G, U, A = rowcall(tag + "_up", f1, [h], [g, wgT, wuT],
                         [(D, BF16), (DFF, BF16), (DFF, BF16), (DFF, BF16)], tm=tm)

    def f2(i, av, hv, wd_ref):
        return (hv + 0.5 * _nn(av, wd_ref[...]),)

    (h2,) = rowcall(tag + "_down", f2, [A, h], [wd], [(D, F32)], tm=tm)
    return h2, (h, n, G, U, A)


def ffn_backward(tag, dh, saved, g, wgT, wuT, wd, tm):
    h, n, G, U, A = saved

    def b1(i, dhv, wd_ref):
        dyb = (0.5 * dhv).astype(BF16)
        return _nt(dyb, wd_ref[...]), dyb

    dA, dyb = rowcall(tag + "_bwd_act", b1, [dh], [wd], [(DFF, BF16), (D, BF16)], tm=tm)

    def b2(i, dAv, Gv, Uv, hv, dhv, g_ref, wg_ref, wu_ref):
        dAf = dAv.astype(F32)
        Gf = Gv.astype(F32)
        sg = _sig(Gf)
        dG = (dAf * Uv.astype(F32) * (sg * (1.0 + Gf * (1.0 - sg)))).astype(BF16)
        dU = (dAf * (Gf * sg)).astype(BF16)
        dn = _nn(dG, wg_ref[...]) + _nn(dU, wu_ref[...])
        gv = g_ref[...]
        hh, r, _ = _rms_fwd(hv, gv)
        dx, dg = _rms_bwd(hh, r, gv, dn)
        dx = jnp.where(_row_ok(i, tm), dx, 0.0)
        return dhv + dx, dG, dU, dg

    dh2, dG, dU, dg = rowcall(tag + "_bwd_in", b2, [dA, G, U, h, dh], [g, wgT, wuT],
                              [(D, F32), (DFF, BF16), (DFF, BF16)], [(1, D)], tm=tm)
    dwd = tn_matmul(tag + "_dwd", A, dyb)
    dwgT = tn_matmul(tag + "_dwg", dG, n)
    dwuT = tn_matmul(tag + "_dwu", dU, n)
    return dh2, dg, dwgT, dwuT, dwd


def _alibi_slope(head):
    return float(2.0 ** (-8.0 * (head + 1) / N_HEADS))


def _att_bias(n, nb):
    qi = lax.broadcasted_iota(jnp.int32, (BLK, 4 * BLK), 0)
    cj = lax.broadcasted_iota(jnp.int32, (BLK, 4 * BLK), 1)
    jb = cj - BLK
    dist = jnp.abs(qi + BLK - jb)
    kpos = (n - 1) * BLK + jb
    band_ok = (dist <= WIN) & (kpos >= BLK) & (kpos < nb * BLK)
    is_meta = cj < BLK
    ok = (is_meta & (cj >= PAD)) | (jnp.logical_not(is_meta) & band_ok)
    distf = jnp.where(is_meta, 0, dist).astype(F32)
    maskadd = jnp.where(ok, 0.0, NEG).astype(F32)
    distf4 = jnp.concatenate([distf] * QG, axis=0)
    mask4 = jnp.concatenate([maskadd] * QG, axis=0)
    return distf4, mask4


def _group_col(vals):
    rg = lax.broadcasted_iota(jnp.int32, (QG * BLK, 1), 0) // BLK
    col = jnp.full((QG * BLK, 1), vals[QG - 1], F32)
    for gq in range(QG - 2, -1, -1):
        col = jnp.where(rg == gq, vals[gq], col)
    return col


def _stack_heads(ref_or_val, kh):
    return jnp.concatenate(
        [ref_or_val[:, (kh * QG + gq) * HD:(kh * QG + gq + 1) * HD] for gq in range(QG)], axis=0)


def _stack_keys(km, kp, kc, kn, kh):
    sl = slice(kh * HD, (kh + 1) * HD)
    return jnp.concatenate([km[:, sl], kp[:, sl], kc[:, sl], kn[:, sl]], axis=0)


LOG2E = 1.4426950408889634
LN2 = 0.6931471805599453
QSCALE = SCALE * LOG2E


def _att_update_bias(bias_ref, n, nb):
    @pl.when((n <= 2) | (n == nb - 1))
    def _():
        distf4, mask4 = _att_bias(n, nb)
        for kh in range(N_KV):
            slope_col = _group_col([_alibi_slope(kh * QG + gq) * LOG2E for gq in range(QG)])
            bias_ref[kh] = mask4 - slope_col * distf4


def _att_exp(qs, kb, bias_ref, kh, sink_ref):
    sink_col = _group_col([sink_ref[kh * QG + gq] for gq in range(QG)]) * LOG2E
    s = _nt(qs, kb) + bias_ref[kh]
    m = jnp.maximum(jnp.max(s, axis=1, keepdims=True), sink_col)
    e = jnp.exp2(s - m)
    es = jnp.exp2(sink_col - m)
    inv = 1.0 / (jnp.sum(e, axis=1, keepdims=True) + es)
    return e, es, inv


def attention_forward(tag, q, k, v, sink):
    LP = q.shape[0]
    nb = LP // BLK

    def kern(sink_ref, q_ref, km_ref, kp_ref, kc_ref, kn_ref, vm_ref, vp_ref, vc_ref, vn_ref, o_ref, bias_ref):
        n = pl.program_id(0)
        _att_update_bias(bias_ref, n, nb)
        qv = q_ref[...]
        km, kp, kc, kn = km_ref[...], kp_ref[...], kc_ref[...], kn_ref[...]
        vm, vp, vc, vn = vm_ref[...], vp_ref[...], vc_ref[...], vn_ref[...]
        for kh in range(N_KV):
            qs = _stack_heads(qv, kh)
            kb = _stack_keys(km, kp, kc, kn, kh)
            vb = _stack_keys(vm, vp, vc, vn, kh)
            e, _, inv = _att_exp(qs, kb, bias_ref, kh, sink_ref)
            o = _nn(e.astype(BF16), vb) * inv
            for gq in range(QG):
                hcol = (kh * QG + gq) * HD
                o_ref[:, hcol:hcol + HD] = o[gq * BLK:(gq + 1) * BLK].astype(o_ref.dtype)

    def kvspec(dn):
        return pl.BlockSpec((BLK, N_KV * HD), lambda n: (jnp.clip(n + dn, 0, nb - 1), 0))

    meta_spec = pl.BlockSpec((BLK, N_KV * HD), lambda n: (0, 0))
    return pl.pallas_call(
        kern, name=tag + "_att_fwd", grid=(nb,),
        in_specs=[pl.BlockSpec(memory_space=pltpu.SMEM), pl.BlockSpec((BLK, D), lambda n: (n, 0)),
                  meta_spec, kvspec(-1), kvspec(0), kvspec(1), meta_spec, kvspec(-1), kvspec(0), kvspec(1)],
        out_specs=pl.BlockSpec((BLK, D), lambda n: (n, 0)),
        out_shape=jax.ShapeDtypeStruct((LP, D), BF16),
        scratch_shapes=[pltpu.VMEM((N_KV, QG * BLK, 4 * BLK), F32)], compiler_params=_params("arbitrary"),
    )(sink, q, k, k, k, k, v, v, v, v)


def attention_backward(tag, q, k, v, do, sink):
    LP = q.shape[0]
    nb = LP // BLK
    KW = N_KV * HD

    def kern(sink_ref, q_ref, do_ref, km_ref, kp_ref, kc_ref, kn_ref, vm_ref, vp_ref, vc_ref, vn_ref,
             dq_ref, dkp_ref, dvp_ref, dkm_ref, dvm_ref, dsink_ref, bias_ref):
        n = pl.program_id(0)

        @pl.when(n == 0)
        def _():
            dkm_ref[...] = jnp.zeros_like(dkm_ref)
            dvm_ref[...] = jnp.zeros_like(dvm_ref)
            dsink_ref[...] = jnp.zeros_like(dsink_ref)

        _att_update_bias(bias_ref, n, nb)
        qv, dov = q_ref[...], do_ref[...]
        km, kp, kc, kn = km_ref[...], kp_ref[...], kc_ref[...], kn_ref[...]
        vm, vp, vc, vn = vm_ref[...], vp_ref[...], vc_ref[...], vn_ref[...]
        lane = lax.broadcasted_iota(jnp.int32, (8, 128), 1)
        dsink = jnp.zeros((8, 128), F32)
        for kh in range(N_KV):
            qs = _stack_heads(qv, kh)
            dos = _stack_heads(dov, kh)
            kb = _stack_keys(km, kp, kc, kn, kh)
            vb = _stack_keys(vm, vp, vc, vn, kh)
            dp = _nt(dos, vb)
            e, es, inv = _att_exp(qs, kb, bias_ref, kh, sink_ref)
            delta = inv * jnp.sum(e * dp, axis=1, keepdims=True)
            ds = (e * ((dp - delta) * inv)).astype(BF16)
            dqs = _nn(ds, kb) * SCALE
            dkt = _tn(qs, ds) * LN2
            dvt = _tn((dos.astype(F32) * inv).astype(BF16), e.astype(BF16))
            dsk = -(es * inv * delta)
            for gq in range(QG):
                hcol = (kh * QG + gq) * HD
                dq_ref[:, hcol:hcol + HD] = dqs[gq * BLK:(gq + 1) * BLK].astype(dq_ref.dtype)
                tot = jnp.sum(dsk[gq * BLK:(gq + 1) * BLK], axis=0, keepdims=True)
                dsink = dsink + jnp.where(lane == kh * QG + gq, tot, 0.0)
            hs = slice(kh * HD, (kh + 1) * HD)
            dkm_ref[hs, :] += dkt[:, 0:BLK]
            dvm_ref[hs, :] += dvt[:, 0:BLK]
            for slot in range(3):
                dkp_ref[0, slot, hs, :] = dkt[:, (slot + 1) * BLK:(slot + 2) * BLK]
                dvp_ref[0, slot, hs, :] = dvt[:, (slot + 1) * BLK:(slot + 2) * BLK]
        dsink_ref[...] += dsink

    def kvspec(dn):
        return pl.BlockSpec((BLK, KW), lambda n: (jnp.clip(n + dn, 0, nb - 1), 0))

    meta_spec = pl.BlockSpec((BLK, KW), lambda n: (0, 0))
    rowspec = pl.BlockSpec((BLK, D), lambda n: (n, 0))
    part_spec = pl.BlockSpec((1, 3, KW, BLK), lambda n: (n, 0, 0, 0))
    dq, dkp, dvp, dkm, dvm, dsink = pl.pallas_call(
        kern, name=tag + "_att_bwd", grid=(nb,),
        in_specs=[pl.BlockSpec(memory_space=pltpu.SMEM), rowspec, rowspec,
                  meta_spec, kvspec(-1), kvspec(0), kvspec(1), meta_spec, kvspec(-1), kvspec(0), kvspec(1)],
        out_specs=[rowspec, part_spec, part_spec, _const_spec((KW, BLK)), _const_spec((KW, BLK)),
                   _const_spec((8, 128))],
        out_shape=[jax.ShapeDtypeStruct((LP, D), BF16), jax.ShapeDtypeStruct((nb, 3, KW, BLK), F32),
                   jax.ShapeDtypeStruct((nb, 3, KW, BLK), F32), jax.ShapeDtypeStruct((KW, BLK), F32),
                   jax.ShapeDtypeStruct((KW, BLK), F32), jax.ShapeDtypeStruct((8, 128), F32)],
        scratch_shapes=[pltpu.VMEM((N_KV, QG * BLK, 4 * BLK), F32)], compiler_params=_params("arbitrary"),
    )(sink, q, do, k, k, k, k, v, v, v, v)

    def comb(a_ref, b_ref, c_ref, m_ref, a2_ref, b2_ref, c2_ref, m2_ref, dk_ref, dv_ref):
        mblk = pl.program_id(0)
        has_prev = (mblk > 0).astype(F32)
        has_next = (mblk < nb - 1).astype(F32)
        is0 = (mblk == 0).astype(F32)
        dkt = a_ref[0, 0] * has_prev + b_ref[0, 0] + c_ref[0, 0] * has_next + m_ref[...] * is0
        dvt = a2_ref[0, 0] * has_prev + b2_ref[0, 0] + c2_ref[0, 0] * has_next + m2_ref[...] * is0
        dk_ref[...] = dkt.T.astype(dk_ref.dtype)
        dv_ref[...] = dvt.T.astype(dv_ref.dtype)

    def pspec(dn, slot):
        return pl.BlockSpec((1, 1, KW, BLK), lambda m: (jnp.clip(m + dn, 0, nb - 1), slot, 0, 0))

    kvout = pl.BlockSpec((BLK, KW), lambda m: (m, 0))
    dk, dv = pl.pallas_call(
        comb, name=tag + "_att_dkv", grid=(nb,),
        in_specs=[pspec(-1, 2), pspec(0, 1), pspec(1, 0), _const_spec((KW, BLK)),
                  pspec(-1, 2), pspec(0, 1), pspec(1, 0), _const_spec((KW, BLK))],
        out_specs=[kvout, kvout],
        out_shape=[jax.ShapeDtypeStruct((LP, KW), BF16)] * 2, compiler_params=_params("arbitrary"),
    )(dkp, dkp, dkp, dkm, dvp, dvp, dvp, dvm)
    return dq, dk, dv, dsink


SCAN_LANES = 1024


def _scan_tile(xr, xi, cr, ci, a8, tab, seg, reverse):
    sub = lax.broadcasted_iota(jnp.int32, (8, SCAN_LANES), 0)
    for c0 in range(0, NST, SCAN_LANES):
        cs = pl.ds(c0, SCAN_LANES)
        ar = a8[0, :, cs]
        ai = a8[1, :, cs]

        def rows(j):
            jj = (seg - 1 - j) if reverse else j
            return pl.ds(jj * 8, 8)

        def step1(j, carry):
            vr, vi = carry
            rs = rows(j)
            nr = ar * vr - ai * vi + xr[rs, cs]
            ni = ar * vi + ai * vr + xi[rs, cs]
            xr[rs, cs] = nr
            xi[rs, cs] = ni
            return nr, ni

        zero = jnp.zeros((8, SCAN_LANES), F32)
        vr, vi = zero, zero
        for j in range(seg):
            vr, vi = step1(j, (vr, vi))
        for t, s in enumerate((1, 2, 4)):
            sh = (8 - s) if reverse else s
            sr = pltpu.roll(vr, sh, 0)
            si = pltpu.roll(vi, sh, 0)
            tr = tab[2 * t, :, cs]
            ti = tab[2 * t + 1, :, cs]
            vr, vi = vr + tr * sr - ti * si, vi + tr * si + ti * sr
        pr = tab[6, :, cs]
        pi = tab[7, :, cs]
        c_r = cr[:, cs]
        c_i = ci[:, cs]
        vr, vi = vr + pr * c_r - pi * c_i, vi + pr * c_i + pi * c_r
        edge = 7 if reverse else 0
        last = 0 if reverse else 7
        sh = 7 if reverse else 1
        in_r = jnp.where(sub == edge, c_r, pltpu.roll(vr, sh, 0))
        in_i = jnp.where(sub == edge, c_i, pltpu.roll(vi, sh, 0))
        cr[:, cs] = jnp.broadcast_to(vr[last:last + 1, :], (8, SCAN_LANES))
        ci[:, cs] = jnp.broadcast_to(vi[last:last + 1, :], (8, SCAN_LANES))

        def step2(j, carry):
            dr, di = carry
            rs = rows(j)
            ndr = ar * dr - ai * di
            ndi = ar * di + ai * dr
            xr[rs, cs] += ndr
            xi[rs, cs] += ndi
            return ndr, ndi

        dr, di = in_r, in_i
        for j in range(seg):
            dr, di = step2(j, (dr, di))


ST_T = 4 * SP * 2
CH_T = 128


def _load_segmented(ref, scr, seg):
    out = []
    for ct in range(4):
        scr[ct] = ref[:, ct * CH_T:(ct + 1) * CH_T]
        out.append(jnp.concatenate([scr[ct, pl.ds(j, 8, stride=seg), :] for j in range(seg)], axis=0))
    return out


def _store_segmented(ref, scr, vals, seg):
    for ct in range(4):
        for j in range(seg):
            scr[ct, pl.ds(j, 8, stride=seg), :] = vals[ct][8 * j:8 * j + 8]
        ref[:, ct * CH_T:(ct + 1) * CH_T] = scr[ct]


def ssm_dir_forward(tag, u, bpr, bpi, cpr, cpi, a8, tab, reverse, tm):
    LP = u.shape[0]
    nt = LP // tm
    seg = tm // 8

    def rix(i):
        return (nt - 1 - i) if reverse else i

    def kern(u_ref, bpr_ref, bpi_ref, cpr_ref, cpi_ref, a8_ref, tab_ref, xre_ref, xim_ref, y_ref,
             xr, xi, ys, cr, ci):
        i = pl.program_id(0)

        @pl.when(i == 0)
        def _():
            cr[...] = jnp.zeros_like(cr)
            ci[...] = jnp.zeros_like(ci)

        ub = _load_segmented(u_ref, ys, seg)
        for ct in range(4):
            uc = ub[ct].astype(BF16)
            xr[:, ct * ST_T:(ct + 1) * ST_T] = _nn(uc, bpr_ref[ct * CH_T:(ct + 1) * CH_T, :])
            xi[:, ct * ST_T:(ct + 1) * ST_T] = _nn(uc, bpi_ref[ct * CH_T:(ct + 1) * CH_T, :])
        _scan_tile(xr, xi, cr, ci, a8_ref, tab_ref, seg, reverse)
        xrb = xr[...].astype(BF16)
        xib = xi[...].astype(BF16)
        xre_ref[...] = xrb
        xim_ref[...] = xib
        yv = []
        for ct in range(4):
            ss = slice(ct * ST_T, (ct + 1) * ST_T)
            yv.append(_nn(xrb[:, ss], cpr_ref[ss, :]) - _nn(xib[:, ss], cpi_ref[ss, :]))
        _store_segmented(y_ref, ys, yv, seg)

    row = lambda w: pl.BlockSpec((tm, w), lambda i: (rix(i), 0))
    return pl.pallas_call(
        kern, name=tag, grid=(nt,),
        in_specs=[row(SW), _const_spec(bpr.shape), _const_spec(bpi.shape), _const_spec(cpr.shape),
                  _const_spec(cpi.shape), _const_spec(a8.shape), _const_spec(tab.shape)],
        out_specs=[row(NST), row(NST), row(SW)],
        out_shape=[jax.ShapeDtypeStruct((LP, NST), BF16), jax.ShapeDtypeStruct((LP, NST), BF16),
                   jax.ShapeDtypeStruct((LP, SW), F32)],
        scratch_shapes=[pltpu.VMEM((tm, NST), F32), pltpu.VMEM((tm, NST), F32), pltpu.VMEM((4, tm, CH_T), F32),
                        pltpu.VMEM((8, NST), F32), pltpu.VMEM((8, NST), F32)],
        compiler_params=_params("arbitrary"),
    )(u, bpr, bpi, cpr, cpi, a8, tab)


def ssm_dir_backward(tag, dy, xre, xim, u, bpr, bpi, cpr, cpi, a8_adj, tab_adj, reverse, tm):
    LP = u.shape[0]
    nt = LP // tm
    seg = tm // 8

    def rix(i):
        return (nt - 1 - i) if reverse else i

    def kern(dy_ref, xre_ref, xim_ref, u_ref, bpr_ref, bpi_ref, cpr_ref, cpi_ref, a8_ref, tab_ref,
             du_ref, gbr_ref, gbi_ref, gcr_ref, gci_ref, sr_ref, si_ref, lr, li, gr, gi, dus, cr, ci):
        i = pl.program_id(0)

        @pl.when(i == 0)
        def _():
            cr[...] = jnp.zeros_like(cr)
            ci[...] = jnp.zeros_like(ci)
            for r in (gbr_ref, gbi_ref, gcr_ref, gci_ref, sr_ref, si_ref):
                r[...] = jnp.zeros_like(r)

        dyb = [v.astype(BF16) for v in _load_segmented(dy_ref, dus, seg)]
        ub = [v.astype(BF16) for v in _load_segmented(u_ref, dus, seg)]
        for ct in range(4):
            ss = slice(ct * ST_T, (ct + 1) * ST_T)
            dc = dyb[ct]
            g_re = _nt(dc, cpr_ref[ss, :])
            g_im = -_nt(dc, cpi_ref[ss, :])
            lr[:, ss] = g_re
            li[:, ss] = g_im
            gr[:, ss] = g_re
            gi[:, ss] = g_im
        _scan_tile(lr, li, cr, ci, a8_ref, tab_ref, seg, reverse)
        lam_r = lr[...]
        lam_i = li[...]
        wr = lam_r - gr[...]
        wi = lam_i - gi[...]
        xr = xre_ref[...].astype(F32)
        xi = xim_ref[...].astype(F32)
        sr_ref[...] += jnp.sum(wr * xr + wi * xi, axis=0, keepdims=True)
        si_ref[...] += jnp.sum(wi * xr - wr * xi, axis=0, keepdims=True)
        lrb = lam_r.astype(BF16)
        lib = lam_i.astype(BF16)
        xrb = xre_ref[...]
        xib = xim_ref[...]
        duv = []
        for ct in range(4):
            ss = slice(ct * ST_T, (ct + 1) * ST_T)
            cs = slice(ct * CH_T, (ct + 1) * CH_T)
            duv.append(_nt(lrb[:, ss], bpr_ref[cs, :]) + _nt(lib[:, ss], bpi_ref[cs, :]))
            gbr_ref[cs, :] += _tn(ub[ct], lrb[:, ss])
            gbi_ref[cs, :] += _tn(ub[ct], lib[:, ss])
            gcr_ref[cs, :] += _tn(dyb[ct], xrb[:, ss])
            gci_ref[cs, :] -= _tn(dyb[ct], xib[:, ss])
        _store_segmented(du_ref, dus, duv, seg)

    row = lambda w: pl.BlockSpec((tm, w), lambda i: (rix(i), 0))
    acc = _const_spec((SW, ST_T))
    vec = _const_spec((1, NST))
    return pl.pallas_call(
        kern, name=tag, grid=(nt,),
        in_specs=[row(SW), row(NST), row(NST), row(SW), _const_spec(bpr.shape), _const_spec(bpi.shape),
                  _const_spec(cpr.shape), _const_spec(cpi.shape), _const_spec(a8_adj.shape),
                  _const_spec(tab_adj.shape)],
        out_specs=[row(SW), acc, acc, acc, acc, vec, vec],
        out_shape=[jax.ShapeDtypeStruct((LP, SW), F32)] + [jax.ShapeDtypeStruct((SW, ST_T), F32)] * 4
        + [jax.ShapeDtypeStruct((1, NST), F32)] * 2,
        scratch_shapes=[pltpu.VMEM((tm, NST), F32)] * 4 + [pltpu.VMEM((4, tm, CH_T), F32)]
        + [pltpu.VMEM((8, NST), F32)] * 2,
        compiler_params=_params("arbitrary"),
    )(dy, xre, xim, u, bpr, bpi, cpr, cpi, a8_adj, tab_adj)


def _ssm_disc(lam_re, lam_im, log_dt, b_re, b_im):
    dt = jnp.exp(log_dt)[:, None]
    mag = jnp.exp(lam_re * dt)
    a_re = mag * jnp.cos(lam_im * dt)
    a_im = mag * jnp.sin(lam_im * dt)
    den = lam_re * lam_re + lam_im * lam_im
    f_re = ((a_re - 1.0) * lam_re + a_im * lam_im) / den
    f_im = (a_im * lam_re - (a_re - 1.0) * lam_im) / den
    bb_re = f_re[:, :, None] * b_re - f_im[:, :, None] * b_im
    bb_im = f_re[:, :, None] * b_im + f_im[:, :, None] * b_re
    return a_re, a_im, bb_re, bb_im


def _scan_tables(lam_re, lam_im, log_dt, conj, reverse, seg):
    dt = jnp.exp(log_dt)[:, None]
    lr = (lam_re * dt).reshape(1, NST)
    li = (lam_im * dt).reshape(1, NST) * (-1.0 if conj else 1.0)
    t = jnp.arange(8, dtype=F32)[:, None]

    def power(kk):
        mag = jnp.exp(kk * lr)
        return mag * jnp.cos(kk * li), mag * jnp.sin(kk * li)

    ones = jnp.ones((8, 1), F32)
    a8 = jnp.stack(power(ones)).astype(F32)
    tabs = []
    for s in (1, 2, 4):
        mask = (t <= 7 - s) if reverse else (t >= s)
        pr, pi = power(float(s * seg) * ones)
        tabs += [jnp.where(mask, pr, 0.0), jnp.where(mask, pi, 0.0)]
    kk = ((8.0 - t) if reverse else (t + 1.0)) * float(seg)
    pr, pi = power(kk)
    tabs += [pr, pi]
    return a8, jnp.stack(tabs).astype(F32)


def _pack_b(bb):
    t = bb.transpose(0, 2, 1).reshape(4, 8, SCH, SP)
    eye = jnp.eye(8, dtype=bb.dtype)
    return jnp.einsum('tgcp,gh->tgchp', t, eye).reshape(SW, ST_T)


def _pack_c(cc):
    t = cc.transpose(0, 2, 1).reshape(4, 8, SP, SCH)
    eye = jnp.eye(8, dtype=cc.dtype)
    return jnp.einsum('tgpc,gh->tgphc', t, eye).reshape(NST, CH_T)


def _unpack_diag(acc):
    t = acc.reshape(4, 8, SCH, 8, SP)
    eye = jnp.eye(8, dtype=acc.dtype)
    return jnp.einsum('tgchp,gh->tgcp', t, eye).reshape(SGRP, SCH, SP)


def _gelu(y):
    k0 = math.sqrt(2.0 / math.pi)
    inner = k0 * (y + 0.044715 * y * y * y)
    th = jnp.tanh(inner)
    z = 0.5 * y * (1.0 + th)
    dz = 0.5 * (1.0 + th) + 0.5 * y * (1.0 - th * th) * k0 * (1.0 + 3.0 * 0.044715 * y * y)
    return z, dz


Q0, K0, V0, U0, GS0, GA0, IN_COLS = 0, 1024, 1280, 1536, 2048, 3072, 4096


def mixer_forward(tag, h, p, tm):
    g, winT, wglu, wbsT, wba, wout = p["g"], p["winT"], p["wglu"], p["wbsT"], p["wba"], p["wout"]

    def proj(i, hv, g_ref, w_ref):
        _, _, n = _rms_fwd(hv, g_ref[...])
        nb = n.astype(BF16)
        return (nb, _nt(nb, w_ref[Q0:K0, :]) * QSCALE, _nt(nb, w_ref[K0:V0, :]), _nt(nb, w_ref[V0:U0, :]),
                _nt(nb, w_ref[U0:GS0, :]), _nt(nb, w_ref[GS0:GA0, :]), _nt(nb, w_ref[GA0:IN_COLS, :]))

    n, q, k, v, u, gs, ga = rowcall(
        tag + "_proj", proj, [h], [g, winT],
        [(D, BF16), (D, BF16), (N_KV * HD, BF16), (N_KV * HD, BF16), (SW, F32), (D, F32), (D, F32)], tm=tm)

    ya = attention_forward(tag, q, k, v, p["sink"])

    states, ydir = [], []
    for dr in range(2):
        s = p["ssm"][dr]
        xre, xim, yd = ssm_dir_forward(f"{tag}_ssm_fwd{dr}", u, s["bpr"], s["bpi"], s["cpr"], s["cpi"],
                                       s["a8"], s["tab"], dr == 1, tm)
        states.append((xre, xim))
        ydir.append(yd)

    def glu(i, y0, y1, uv, d_ref, w_ref):
        ypre = y0 + y1 + d_ref[...] * uv
        z, _ = _gelu(ypre)
        zb = z.astype(BF16)
        t = _nn(zb, w_ref[...])
        return ypre, zb, t, z * _sig(t)

    ypre, zb, t, ys = rowcall(tag + "_glu", glu, [ydir[0], ydir[1], u], [p["d"], wglu],
                              [(SW, F32), (SW, BF16), (SW, F32), (SW, BF16)], tm=tm)

    def merge(i, ysv, yav, gsv, gav, hv, wbs_ref, wba_ref, wout_ref):
        bs = _nt(ysv, wbs_ref[...])
        ba = _nn(yav, wba_ref[...])
        mg = _sig(gsv) * bs + _sig(gav) * ba
        mg = jnp.where(_row_ok(i, tm), mg, 0.0).astype(BF16)
        return bs, ba, mg, hv + _nn(mg, wout_ref[...])

    bs, ba, mg, h2 = rowcall(tag + "_merge", merge, [ys, ya, gs, ga, h], [wbsT, wba, wout],
                             [(D, BF16), (D, BF16), (D, BF16), (D, F32)], tm=tm)
    saved = dict(h=h, n=n, q=q, k=k, v=v, u=u, gs=gs, ga=ga, ya=ya, states=states, ypre=ypre, zb=zb, t=t,
                 ys=ys, bs=bs, ba=ba, mg=mg)
    return h2, saved


def mixer_backward(tag, dh, sv, p, tm):
    g, winT, wglu, wbsT, wba, wout = p["g"], p["winT"], p["wglu"], p["wbsT"], p["wba"], p["wout"]

    def y1(i, dhv, bsv, bav, gsv, gav, ypv, tv, uv, wout_ref, wbs_ref, wba_ref, d_ref, wglu_ref):
        dhb = dhv.astype(BF16)
        dmg = _nt(dhb, wout_ref[...])
        dmg = jnp.where(_row_ok(i, tm), dmg, 0.0)
        sgs = _sig(gsv)
        sga = _sig(gav)
        dbs = (dmg * sgs).astype(BF16)
        dba = (dmg * sga).astype(BF16)
        dgs = dmg * bsv.astype(F32) * sgs * (1.0 - sgs)
        dga = dmg * bav.astype(F32) * sga * (1.0 - sga)
        dys = _nn(dbs, wbs_ref[...])
        dya = _nt(dba, wba_ref[...])
        z, dz_dy = _gelu(ypv)
        st = _sig(tv)
        dt_ = dys * z * st * (1.0 - st)
        dz = dys * st + _nt(dt_.astype(BF16), wglu_ref[...])
        dyp = dz * dz_dy
        return (dbs, dba, dgs, dga, dhb, dya, dyp, dyp * d_ref[...], dt_,
                jnp.sum(dyp * uv, axis=0, keepdims=True))

    dbs, dba, dgs, dga, dhb, dya, dypb, du0, dtb, dd = rowcall(
        tag + "_bwd_merge", y1,
        [dh, sv["bs"], sv["ba"], sv["gs"], sv["ga"], sv["ypre"], sv["t"], sv["u"]],
        [wout, wbsT, wba, p["d"], wglu],
        [(D, BF16)] * 6 + [(SW, F32), (SW, F32), (SW, BF16)], [(1, SW)], tm=tm)
    dwout = tn_matmul(tag + "_dwout", sv["mg"], dhb)
    dwbsT = tn_matmul(tag + "_dwbs", dbs, sv["ys"])
    dwba = tn_matmul(tag + "_dwba", sv["ya"], dba)
    dwglu = tn_matmul(tag + "_dwglu", sv["zb"], dtb)

    du_dirs, ssm_sums = [], []
    for dr in range(2):
        s = p["ssm"][dr]
        xre, xim = sv["states"][dr]
        res = ssm_dir_backward(f"{tag}_ssm_bwd{dr}", dypb, xre, xim, sv["u"], s["bpr"], s["bpi"], s["cpr"],
                               s["cpi"], s["a8_adj"], s["tab_adj"], dr == 0, tm)
        du_dirs.append(res[0])
        ssm_sums.append(res[1:])

    dq, dk, dv, dsink = attention_backward(tag, sv["q"], sv["k"], sv["v"], dya, p["sink"])

    def x1b(i, dqv, dkv, dvv, du0v, du1v, du2v, dgsv, dgav, hv, dhv, g_ref, w_ref):
        dub = (du0v + du1v + du2v).astype(BF16)
        dn = (_nn(dqv, w_ref[Q0:K0, :]) + _nn(dkv, w_ref[K0:V0, :]) + _nn(dvv, w_ref[V0:U0, :])
              + _nn(dub, w_ref[U0:GS0, :]) + _nn(dgsv, w_ref[GS0:GA0, :]) + _nn(dgav, w_ref[GA0:IN_COLS, :]))
        gv = g_ref[...]
        hh, r, _ = _rms_fwd(hv, gv)
        dx, dg = _rms_bwd(hh, r, gv, dn)
        dx = jnp.where(_row_ok(i, tm), dx, 0.0)
        return dhv + dx, dub, dg

    dh2, dub, dg = rowcall(tag + "_bwd_in", x1b,
                           [dq, dk, dv, du0, du_dirs[0], du_dirs[1], dgs, dga, sv["h"], dh], [g, winT],
                           [(D, F32), (SW, BF16)], [(1, D)], tm=tm)
    n = sv["n"]
    dwinT = jnp.concatenate([tn_matmul(f"{tag}_dwin{j}", piece, n)
                             for j, piece in enumerate((dq, dk, dv, dub, dgs, dga))], axis=0)
    grads = dict(g=dg, d=dd, sink=dsink, ssm=ssm_sums, winT=dwinT, wglu=dwglu, wbsT=dwbsT, wba=dwba, wout=dwout)
    return dh2, grads


W1024 = ("f1_wgT", "f1_wuT", "f1_wd", "winT", "wba", "wout", "f2_wgT", "f2_wuT", "f2_wd")
W512 = ("wglu", "wbsT")
PART_F1 = ("f1_wgT", "f1_wuT", "f1_wd")
PART_MIX = ("winT", "wba", "wout", "wglu", "wbsT")
PART_F2 = ("f2_wgT", "f2_wuT", "f2_wd")
PER_LAYER_SMALL = ("ffn1_norm", "mix_norm", "ffn2_norm", "ssm_lam_re", "ssm_lam_im", "ssm_log_dt",
                   "ssm_b_re", "ssm_b_im", "ssm_c_re", "ssm_c_im", "ssm_d", "attn_sink")
SMALL = ("ffn1_norm", "mix_norm", "ffn2_norm", "final_norm", "ssm_lam_re", "ssm_lam_im", "ssm_log_dt",
         "ssm_b_re", "ssm_b_im", "ssm_c_re", "ssm_c_im", "ssm_d", "attn_sink")


def kernel(x, meta_tokens, ffn1_norm, ffn1_w_gate, ffn1_w_up, ffn1_w_down, mix_norm, w_in, ssm_lam_re, ssm_lam_im, ssm_log_dt, ssm_b_re, ssm_b_im, ssm_c_re, ssm_c_im, ssm_d, ssm_w_glu, attn_sink, w_branch_ssm, w_branch_attn, w_out, ffn2_norm, ffn2_w_gate, ffn2_w_up, ffn2_w_down, final_norm, loss_target, m_meta_tokens, m_ffn1_norm, m_ffn1_w_gate, m_ffn1_w_up, m_ffn1_w_down, m_mix_norm, m_w_in, m_ssm_lam_re, m_ssm_lam_im, m_ssm_log_dt, m_ssm_b_re, m_ssm_b_im, m_ssm_c_re, m_ssm_c_im, m_ssm_d, m_ssm_w_glu, m_attn_sink, m_w_branch_ssm, m_w_branch_attn, m_w_out, m_ffn2_norm, m_ffn2_w_gate, m_ffn2_w_up, m_ffn2_w_down, m_final_norm, v_meta_tokens, v_ffn1_norm, v_ffn1_w_gate, v_ffn1_w_up, v_ffn1_w_down, v_mix_norm, v_w_in, v_ssm_lam_re, v_ssm_lam_im, v_ssm_log_dt, v_ssm_b_re, v_ssm_b_im, v_ssm_c_re, v_ssm_c_im, v_ssm_d, v_ssm_w_glu, v_attn_sink, v_w_branch_ssm, v_w_branch_attn, v_w_out, v_ffn2_norm, v_ffn2_w_gate, v_ffn2_w_up, v_ffn2_w_down, v_final_norm):
    weights = dict(meta_tokens=meta_tokens, ffn1_norm=ffn1_norm, ffn1_w_gate=ffn1_w_gate, ffn1_w_up=ffn1_w_up, ffn1_w_down=ffn1_w_down, mix_norm=mix_norm, w_in=w_in, ssm_lam_re=ssm_lam_re, ssm_lam_im=ssm_lam_im, ssm_log_dt=ssm_log_dt, ssm_b_re=ssm_b_re, ssm_b_im=ssm_b_im, ssm_c_re=ssm_c_re, ssm_c_im=ssm_c_im, ssm_d=ssm_d, ssm_w_glu=ssm_w_glu, attn_sink=attn_sink, w_branch_ssm=w_branch_ssm, w_branch_attn=w_branch_attn, w_out=w_out, ffn2_norm=ffn2_norm, ffn2_w_gate=ffn2_w_gate, ffn2_w_up=ffn2_w_up, ffn2_w_down=ffn2_w_down, final_norm=final_norm)
    mom_m = dict(meta_tokens=m_meta_tokens, ffn1_norm=m_ffn1_norm, ffn1_w_gate=m_ffn1_w_gate, ffn1_w_up=m_ffn1_w_up, ffn1_w_down=m_ffn1_w_down, mix_norm=m_mix_norm, w_in=m_w_in, ssm_lam_re=m_ssm_lam_re, ssm_lam_im=m_ssm_lam_im, ssm_log_dt=m_ssm_log_dt, ssm_b_re=m_ssm_b_re, ssm_b_im=m_ssm_b_im, ssm_c_re=m_ssm_c_re, ssm_c_im=m_ssm_c_im, ssm_d=m_ssm_d, ssm_w_glu=m_ssm_w_glu, attn_sink=m_attn_sink, w_branch_ssm=m_w_branch_ssm, w_branch_attn=m_w_branch_attn, w_out=m_w_out, ffn2_norm=m_ffn2_norm, ffn2_w_gate=m_ffn2_w_gate, ffn2_w_up=m_ffn2_w_up, ffn2_w_down=m_ffn2_w_down, final_norm=m_final_norm)
    mom_v = dict(meta_tokens=v_meta_tokens, ffn1_norm=v_ffn1_norm, ffn1_w_gate=v_ffn1_w_gate, ffn1_w_up=v_ffn1_w_up, ffn1_w_down=v_ffn1_w_down, mix_norm=v_mix_norm, w_in=v_w_in, ssm_lam_re=v_ssm_lam_re, ssm_lam_im=v_ssm_lam_im, ssm_log_dt=v_ssm_log_dt, ssm_b_re=v_ssm_b_re, ssm_b_im=v_ssm_b_im, ssm_c_re=v_ssm_c_re, ssm_c_im=v_ssm_c_im, ssm_d=v_ssm_d, ssm_w_glu=v_ssm_w_glu, attn_sink=v_attn_sink, w_branch_ssm=v_w_branch_ssm, w_branch_attn=v_w_branch_attn, w_out=v_w_out, ffn2_norm=v_ffn2_norm, ffn2_w_gate=v_ffn2_w_gate, ffn2_w_up=v_ffn2_w_up, ffn2_w_down=v_ffn2_w_down, final_norm=v_final_norm)
    names = list(weights)

    L0 = x.shape[1]
    LP = L0 + BLK
    tm = 384 if LP % 384 == 0 else BLK
    x_i, y_i, c_i = lax.axis_index("x"), lax.axis_index("y"), lax.axis_index("c")
    me = 4 * x_i + 2 * y_i + c_i

    def canon(l):
        return dict(
            f1_wgT=ffn1_w_gate[l].T, f1_wuT=ffn1_w_up[l].T, f1_wd=ffn1_w_down[l],
            winT=w_in[l].T, wba=w_branch_attn[l], wout=w_out[l],
            f2_wgT=ffn2_w_gate[l].T, f2_wuT=ffn2_w_up[l].T, f2_wd=ffn2_w_down[l],
            wglu=ssm_w_glu[l], wbsT=w_branch_ssm[l].T)

    shards = [{nm: a.astype(BF16) for nm, a in canon(l).items()} for l in range(DEPTH)]

    def rows_of(nm):
        return shards[0][nm].shape[0]

    def width_groups(names_):
        return [g for g in ([nm for nm in names_ if nm in W1024], [nm for nm in names_ if nm in W512]) if g]

    def pieces_for(group):
        out, off = [], 0
        for nm in group:
            out.append((off, rows_of(nm)))
            off += rows_of(nm)
        return out

    def start_gather(tag, l, names_):
        groups = width_groups(names_)
        packed = [jnp.concatenate([shards[l][nm] for nm in g], axis=0) for g in groups]
        return gather_layer_start(tag, packed, [pieces_for(g) for g in groups]), groups

    def finish_gather(tag, l, started_, after):
        handle, groups = started_
        dests = gather_layer_wait(tag, handle, len(groups), after)
        out = {}
        for nm, dest in zip([nm for g in groups for nm in g], dests):
            sh = shards[l][nm]
            out[nm] = lax.dynamic_update_slice(dest, sh, (me * sh.shape[0], 0))
        return out

    g1, gm = all_gather_pieces(
        "gather_weights_first",
        [(jnp.concatenate([shards[0][nm] for nm in PART_F1], axis=0), pieces_for(PART_F1)),
         (meta_tokens, [(0, N_META)])])
    first_weights = dict(zip(PART_F1, g1))
    meta_full = gm[0].reshape(NDEV, N_META, D // NDEV).transpose(1, 0, 2).reshape(N_META, D)
    gather_started = [start_gather("gather_start_l0", 0, PART_MIX + PART_F2)]
    gather_started += [start_gather(f"gather_start_l{l}", l, W1024 + W512) for l in range(1, DEPTH)]
    started = sum(st[0][3][0, 0] for st in gather_started)
    full = [None] * DEPTH

    def disc_all(lre, lim, ldt, bre, bim):
        return _ssm_disc(lre, lim, ldt, bre, bim)

    ssm_p, ssm_vjp = [], []
    for l in range(DEPTH):
        row, vrow = [], []
        for dr in range(2):
            args = (ssm_lam_re[l, dr], ssm_lam_im[l, dr], ssm_log_dt[l, dr], ssm_b_re[l, dr], ssm_b_im[l, dr])
            (a_re, a_im, bb_re, bb_im), vjp = jax.vjp(disc_all, *args)
            a8, tab = _scan_tables(args[0], args[1], args[2], False, dr == 1, tm // 8)
            a8_adj, tab_adj = _scan_tables(args[0], args[1], args[2], True, dr == 0, tm // 8)
            row.append(dict(
                bpr=_pack_b(bb_re).astype(BF16), bpi=_pack_b(bb_im).astype(BF16),
                cpr=_pack_c(ssm_c_re[l, dr]).astype(BF16), cpi=_pack_c(ssm_c_im[l, dr]).astype(BF16),
                a8=a8, tab=tab, a8_adj=a8_adj, tab_adj=tab_adj, a_re=a_re, a_im=a_im))
            vrow.append(vjp)
        ssm_p.append(row)
        ssm_vjp.append(vrow)

    blk0 = jnp.concatenate([jnp.zeros((PAD, D), F32), meta_full.astype(F32)], axis=0)
    h = build_h0(x[0], blk0)
    saved = []
    for l in range(DEPTH):
        w = dict(first_weights) if l == 0 else finish_gather(f"gather_wait_l{l}", l, gather_started[l], h)
        full[l] = w
        g1n, g2n = ffn1_norm[l][None, :], ffn2_norm[l][None, :]
        if l == 0:
            g1n = g1n + started
        h, s1 = ffn_forward("ffn1", h, g1n, w["f1_wgT"], w["f1_wuT"], w["f1_wd"], tm)
        if l == 0:
            w.update(finish_gather("gather_wait_l0", 0, gather_started[0], h))
        mp = dict(g=mix_norm[l][None, :], winT=w["winT"], wglu=w["wglu"], wbsT=w["wbsT"], wba=w["wba"],
                  wout=w["wout"], d=ssm_d[l][None, :], sink=attn_sink[l], ssm=ssm_p[l])
        h, s2 = mixer_forward("mix", h, mp, tm)
        h, s3 = ffn_forward("ffn2", h, g2n, w["f2_wgT"], w["f2_wuT"], w["f2_wd"], tm)
        saved.append((s1, s2, s3, mp, g1n, g2n))

    dh, loss_acc, dgf = final_loss(h, loss_target[0], final_norm[None, :])
    loss = lax.psum(loss_acc[0, 0], MESH_AXES)

    small = {nm: [None] * DEPTH for nm in SMALL if nm != "final_norm"}

    def start_scatter(tag, grads_d, names_):
        groups = width_groups(names_)
        mine = [jnp.concatenate([lax.dynamic_slice_in_dim(grads_d[nm], me * rows_of(nm), rows_of(nm), axis=0)
                                 for nm in g], axis=0) for g in groups]
        handle, nin = scatter_layer_start(tag + "_start", [[grads_d[nm] for nm in g] for g in groups])
        return dict(tag=tag, handle=handle, nin=nin, groups=groups, mine=mine)

    def finish_scatter(st, after):
        lands = scatter_layer_wait(st["tag"] + "_wait", st["handle"], st["nin"], len(st["groups"]), after)
        out = {}
        for land, mine, g in zip(lands, st["mine"], st["groups"]):
            land = lax.dynamic_update_slice(land, mine[None], (me, 0, 0))
            tot = sum_slots(f"sum_weight_grads_{land.shape[1]}x{land.shape[2]}", land)
            for nm, (off_, r) in zip(g, pieces_for(g)):
                out[nm] = tot[off_:off_ + r]
        return out

    scatters = []
    sent = jnp.zeros((), F32)
    for l in reversed(range(DEPTH)):
        s1, s2, s3, mp, g1n, g2n = saved[l]
        w = full[l]
        dh, dg2, f2g, f2u, f2d = ffn_backward("ffn2", dh, s3, g2n + sent, w["f2_wgT"], w["f2_wuT"], w["f2_wd"], tm)
        st = start_scatter(f"scatter_l{l}_f2", dict(f2_wgT=f2g, f2_wuT=f2u, f2_wd=f2d), PART_F2)
        scatters.append((l, st))
        dh, mg = mixer_backward("mix", dh, s2, dict(mp, d=mp["d"] + st["handle"][3][0, 0]), tm)
        st = start_scatter(f"scatter_l{l}_mix", mg, PART_MIX)
        scatters.append((l, st))
        dh, dg1, f1g, f1u, f1d = ffn_backward("ffn1", dh, s1, g1n + st["handle"][3][0, 0],
                                              w["f1_wgT"], w["f1_wuT"], w["f1_wd"], tm)
        st = start_scatter(f"scatter_l{l}_f1", dict(f1_wgT=f1g, f1_wuT=f1u, f1_wd=f1d), PART_F1)
        scatters.append((l, st))
        sent = st["handle"][3][0, 0]
        small["ffn1_norm"][l] = dg1[0]
        small["mix_norm"][l] = mg["g"][0]
        small["ffn2_norm"][l] = dg2[0]
        small["ssm_d"][l] = mg["d"][0]
        small["attn_sink"][l] = mg["sink"][0, :N_HEADS]
        per_dir = {k: [] for k in ("ssm_lam_re", "ssm_lam_im", "ssm_log_dt", "ssm_b_re", "ssm_b_im",
                                   "ssm_c_re", "ssm_c_im")}
        for dr in range(2):
            gbr, gbi, gcr, gci, s_re, s_im = mg["ssm"][dr]
            a_re, a_im = ssm_p[l][dr]["a_re"], ssm_p[l][dr]["a_im"]
            s_re = s_re.reshape(SGRP, SP)
            s_im = s_im.reshape(SGRP, SP)
            den = a_re * a_re + a_im * a_im
            ga_re = (s_re * a_re - s_im * a_im) / den
            ga_im = (s_re * a_im + s_im * a_re) / den
            glr, gli, gld, gbre, gbim = ssm_vjp[l][dr]((ga_re, ga_im, _unpack_diag(gbr).transpose(0, 2, 1),
                                                        _unpack_diag(gbi).transpose(0, 2, 1)))
            per_dir["ssm_lam_re"].append(glr)
            per_dir["ssm_lam_im"].append(gli)
            per_dir["ssm_log_dt"].append(gld)
            per_dir["ssm_b_re"].append(gbre)
            per_dir["ssm_b_im"].append(gbim)
            per_dir["ssm_c_re"].append(_unpack_diag(gcr))
            per_dir["ssm_c_im"].append(_unpack_diag(gci))
        for k, vlist in per_dir.items():
            small[k][l] = jnp.stack(vlist)

    grad_x = dh[BLK:][None]
    dmeta_part = dh[PAD:BLK]

    small_part = {k: jnp.stack(vv) for k, vv in small.items()}
    small_part["final_norm"] = dgf[0]
    pieces = [small_part[k].reshape(-1) for k in SMALL] + [dmeta_part.reshape(-1)]
    sizes = [p_.shape[0] for p_ in pieces]
    total = sum(sizes)
    rows_s = -(-total // (8 * D)) * 8
    flat = jnp.concatenate(pieces + [jnp.zeros((rows_s * D - total,), F32)]).reshape(rows_s, D)
    ((gathered,),) = all_gather_pieces("gather_small_grads", [(flat, [(0, rows_s)])])
    small_sum = sum_slots("sum_small_grads", gathered.reshape(NDEV, rows_s, D)).reshape(-1)
    grads = {}
    o = 0
    for k, sz in zip(SMALL, sizes[:-1]):
        grads[k] = small_sum[o:o + sz].reshape(weights[k].shape)
        o += sz
    dmeta_full = small_sum[o:o + N_META * D].reshape(N_META, D)
    grads["meta_tokens"] = lax.dynamic_slice_in_dim(dmeta_full, me * (D // NDEV), D // NDEV, axis=1)

    own = [dict() for _ in range(DEPTH)]
    for l, st in scatters:
        own[l].update(finish_scatter(st, dh))

    def stack(fn):
        return jnp.stack([fn(own[l]) for l in range(DEPTH)])

    grads["ffn1_w_gate"] = stack(lambda d: d["f1_wgT"].T)
    grads["ffn1_w_up"] = stack(lambda d: d["f1_wuT"].T)
    grads["ffn1_w_down"] = stack(lambda d: d["f1_wd"])
    grads["w_in"] = stack(lambda d: d["winT"].T)
    grads["ssm_w_glu"] = stack(lambda d: d["wglu"])
    grads["w_branch_ssm"] = stack(lambda d: d["wbsT"].T)
    grads["w_branch_attn"] = stack(lambda d: d["wba"])
    grads["w_out"] = stack(lambda d: d["wout"])
    grads["ffn2_w_gate"] = stack(lambda d: d["f2_wgT"].T)
    grads["ffn2_w_up"] = stack(lambda d: d["f2_wuT"].T)
    grads["ffn2_w_down"] = stack(lambda d: d["f2_wd"])

    deltas, new_m, new_v = {}, {}, {}
    for nm in names:
        deltas[nm], new_m[nm], new_v[nm] = adamw("adamw_" + nm, weights[nm], grads[nm], mom_m[nm], mom_v[nm])

    return (loss, grad_x, *[grads[n] for n in names], *[deltas[n] for n in names],
            *[new_m[n] for n in names], *[new_v[n] for n in names])
```

```python
import functools
import math

import jax
import jax.numpy as jnp
from jax import lax
from jax.experimental import pallas as pl
from jax.experimental.pallas import tpu as pltpu

F32 = jnp.float32
BF16 = jnp.bfloat16

D = 1024
DFF = 2816
N_META = 16
N_HEADS = 16
N_KV = 4
HD = 64
QG = 4
WIN = 128
BLK = 128
PAD = BLK - N_META
SW = 512
SGRP = 32
SCH = 16
SP = 64
NST = SGRP * SP
EPS = 1e-6
NEG = -1e30
SCALE = HD ** -0.5
NDEV = 8
DEPTH = 4
MESH_AXES = ("x", "y", "c")
MESH = pl.DeviceIdType.MESH

ADAM_LR = 0.001
ADAM_B1 = 0.9
ADAM_B2 = 0.999
ADAM_EPS = 1e-08
ADAM_WD = 0.01
ADAM_STEP = 10

VMEM_LIMIT = 56 * 1024 * 1024


def _params(*sem):
    return pltpu.CompilerParams(dimension_semantics=sem, vmem_limit_bytes=VMEM_LIMIT)


def _nn(a, b):
    return lax.dot_general(a, b, (((1,), (0,)), ((), ())), preferred_element_type=F32)


def _nt(a, b):
    return lax.dot_general(a, b, (((1,), (1,)), ((), ())), preferred_element_type=F32)


def _tn(a, b):
    return lax.dot_general(a, b, (((0,), (0,)), ((), ())), preferred_element_type=F32)


def _sig(x):
    return 0.5 * jnp.tanh(0.5 * x) + 0.5


def _rms_fwd(h, g):
    r = lax.rsqrt(jnp.mean(h * h, axis=-1, keepdims=True) + EPS)
    hh = h * r
    return hh, r, hh * g


def _rms_bwd(hh, r, g, dn):
    dhh = dn * g
    dx = r * (dhh - hh * jnp.mean(dhh * hh, axis=-1, keepdims=True))
    return dx, jnp.sum(dn * hh, axis=0, keepdims=True)


def _row_ok(i, tm):
    rows = i * tm + lax.broadcasted_iota(jnp.int32, (tm, 1), 0)
    return rows >= PAD


def _const_spec(shape, single=False):
    nd = len(shape)
    if single:
        return pl.BlockSpec(shape, lambda *_: (0,) * nd, pipeline_mode=pl.Buffered(1))
    return pl.BlockSpec(shape, lambda *_: (0,) * nd)


def rowcall(name, body, rows, consts, outs, accs=(), *, tm):
    nrows = rows[0].shape[0]
    nt = nrows // tm
    assert nt * tm == nrows, (name, nrows, tm)
    nr, nc, no, na = len(rows), len(consts), len(outs), len(accs)
    in_specs = [pl.BlockSpec((tm, r.shape[1]), lambda i: (i, 0)) for r in rows]
    in_specs += [_const_spec(c.shape, single=True) for c in consts]
    out_shape = [jax.ShapeDtypeStruct((nrows, w), dt) for (w, dt) in outs]
    out_specs = [pl.BlockSpec((tm, w), lambda i: (i, 0)) for (w, dt) in outs]
    out_shape += [jax.ShapeDtypeStruct(s, F32) for s in accs]
    out_specs += [_const_spec(s) for s in accs]

    def kern(*refs):
        i = pl.program_id(0)
        row_vals = [r[...] for r in refs[:nr]]
        res = body(i, *row_vals, *refs[nr:nr + nc])
        out_refs = refs[nr + nc:nr + nc + no]
        acc_refs = refs[nr + nc + no:]
        for r, v in zip(out_refs, res[:no]):
            r[...] = v.astype(r.dtype)
        if na:
            @pl.when(i == 0)
            def _():
                for r in acc_refs:
                    r[...] = jnp.zeros_like(r)
            for r, v in zip(acc_refs, res[no:]):
                r[...] += v

    res = pl.pallas_call(
        kern, name=name, grid=(nt,), in_specs=in_specs, out_specs=out_specs, out_shape=out_shape,
        compiler_params=_params("arbitrary"),
    )(*rows, *consts)
    return res


def tn_matmul(name, lhs, rhs, scale=1.0):
    M, K = lhs.shape
    N = rhs.shape[1]
    assert lhs.dtype == BF16 and rhs.dtype == BF16
    nm = 6
    tmw = M // nm
    assert tmw * nm == M and tmw % 16 == 0
    tk = 1408 if (K % 1408 == 0) else K
    nk = K // tk

    def kern(a_ref, b_ref, o_ref, acc):
        m = pl.program_id(1)
        part = _tn(a_ref[...], b_ref[...])

        @pl.when(m == 0)
        def _():
            acc[...] = part

        @pl.when((m > 0) & (m < nm - 1))
        def _():
            acc[...] += part

        @pl.when(m == nm - 1)
        def _():
            o_ref[...] = ((acc[...] + part) * scale).astype(o_ref.dtype)

    return pl.pallas_call(
        kern, name=name, grid=(nk, nm),
        in_specs=[pl.BlockSpec((tmw, tk), lambda k, m: (m, k)), pl.BlockSpec((tmw, N), lambda k, m: (m, 0))],
        out_specs=pl.BlockSpec((tk, N), lambda k, m: (k, 0)),
        out_shape=jax.ShapeDtypeStruct((K, N), BF16),
        scratch_shapes=[pltpu.VMEM((tk, N), F32)],
        compiler_params=_params("arbitrary", "arbitrary"),
    )(lhs, rhs)


def _mesh_pos():
    x, y, c = lax.axis_index("x"), lax.axis_index("y"), lax.axis_index("c")
    return x, y, c


def all_gather_pieces(name, groups):
    ng = len(groups)
    packed = [g[0] for g in groups]
    pieces = [g[1] for g in groups]
    out_shape, out_map = [], []
    for gi, (p, pcs) in enumerate(groups):
        idx = []
        for (off, r) in pcs:
            idx.append(len(out_shape))
            out_shape.append(jax.ShapeDtypeStruct((NDEV * r, p.shape[1]), p.dtype))
        out_map.append(idx)
    nout = len(out_shape)

    def body(*refs):
        p_refs = refs[:ng]
        o_refs = refs[ng:ng + nout]
        send_sems, recv_sems, local_sems = refs[ng + nout:]
        x, y, c = _mesh_pos()
        me = (x, y, c)
        sibling = (x, y, 1 - c)
        chips = [(1 - x, y), (x, 1 - y), (1 - x, 1 - y)]

        def blk(px, py, pc):
            return 4 * px + 2 * py + pc

        def copies(gi, k, origin, to, from_out):
            cps = []
            for (off, r), oi in zip(pieces[gi], out_map[gi]):
                dst = o_refs[oi].at[pl.ds(origin * r, r), :]
                src = dst if from_out else p_refs[gi].at[pl.ds(off, r), :]
                cps.append(pltpu.make_async_remote_copy(
                    src_ref=src, dst_ref=dst, send_sem=send_sems.at[gi, k], recv_sem=recv_sems.at[gi, k],
                    device_id=to, device_id_type=MESH))
            return cps

        def whole(gi, k):
            return pltpu.make_async_remote_copy(
                src_ref=p_refs[gi], dst_ref=p_refs[gi], send_sem=send_sems.at[gi, k],
                recv_sem=recv_sems.at[gi, k], device_id=me, device_id_type=MESH)

        mine = []
        for gi in range(ng):
            for (off, r), oi in zip(pieces[gi], out_map[gi]):
                mine.append(pltpu.make_async_copy(
                    p_refs[gi].at[pl.ds(off, r), :], o_refs[oi].at[pl.ds(blk(*me) * r, r), :],
                    local_sems.at[gi]))
        for cp in mine:
            cp.start()
        for gi in range(ng):
            for cp in copies(gi, 0, blk(*me), sibling, False):
                cp.start()
            for j, chip in enumerate(chips):
                for cp in copies(gi, 1 + j, blk(*me), (*chip, c), False):
                    cp.start()
        for j, chip in enumerate(chips):
            for gi in range(ng):
                whole(gi, 1 + j).wait_recv()
                for cp in copies(gi, 4 + j, blk(*chip, c), sibling, True):
                    cp.start()
        for gi in range(ng):
            whole(gi, 0).wait_recv()
            for j in range(3):
                whole(gi, 4 + j).wait_recv()
        for gi in range(ng):
            for k in range(7):
                whole(gi, k).wait_send()
            pltpu.make_async_copy(p_refs[gi], p_refs[gi], local_sems.at[gi]).wait()

    any_spec = pl.BlockSpec(memory_space=pl.ANY)
    outs = pl.pallas_call(
        body, name=name, out_shape=out_shape,
        in_specs=[any_spec] * ng, out_specs=[any_spec] * nout,
        scratch_shapes=[pltpu.SemaphoreType.DMA((ng, 7)), pltpu.SemaphoreType.DMA((ng, 7)),
                        pltpu.SemaphoreType.DMA((ng,))],
    )(*packed)
    return [[outs[oi] for oi in idx] for idx in out_map]


HBM_SPEC = pl.BlockSpec(memory_space=pltpu.HBM)
SEM_SPEC = pl.BlockSpec(memory_space=pltpu.SEMAPHORE)
DATAFLOW = pltpu.SideEffectType.DATAFLOW_SIDE_EFFECTING


def _peers(x, y, c):
    return [(x, y, 1 - c), (1 - x, y, c), (x, 1 - y, c), (1 - x, 1 - y, c),
            (1 - x, y, 1 - c), (x, 1 - y, 1 - c), (1 - x, 1 - y, 1 - c)]


def exchange_start(name, arrays, ng, plan):
    n = len(arrays)
    ns = ng * 7

    def body(*refs):
        in_refs = refs[:n]
        send_sems, recv_sems = refs[n:n + ns], refs[n + ns:n + 2 * ns]
        token = refs[-1]
        x, y, c = _mesh_pos()
        me_i = 4 * x + 2 * y + c
        for k, peer in enumerate(_peers(x, y, c)):
            p_i = 4 * peer[0] + 2 * peer[1] + peer[2]
            for src, dst, gi in plan(in_refs, me_i, p_i):
                pltpu.make_async_remote_copy(
                    src_ref=src, dst_ref=dst, send_sem=send_sems[gi * 7 + k], recv_sem=recv_sems[gi * 7 + k],
                    device_id=peer, device_id_type=MESH).start()
        token[...] = jnp.zeros_like(token)

    res = pl.pallas_call(
        body, name=name,
        out_shape=(*[pltpu.SemaphoreType.DMA(())] * (2 * ns),
                   *[pltpu.HBM(a.shape, a.dtype) for a in arrays], jax.ShapeDtypeStruct((8, 128), F32)),
        in_specs=[HBM_SPEC] * n,
        out_specs=(*[SEM_SPEC] * (2 * ns), *[HBM_SPEC] * n, pl.BlockSpec(memory_space=pltpu.VMEM)),
        input_output_aliases={i: 2 * ns + i for i in range(n)},
        compiler_params=pltpu.CompilerParams(has_side_effects=DATAFLOW),
    )(*[pltpu.with_memory_space_constraint(a, pltpu.HBM) for a in arrays])
    return list(res[:ns]), list(res[ns:2 * ns]), list(res[2 * ns:2 * ns + n]), res[-1]


def exchange_wait(name, send_sems, recv_sems, arrays, ng, sized, after):
    n = len(arrays)
    ns = ng * 7

    def body(*refs):
        in_refs = refs[:n]
        s_sems, r_sems = refs[n:n + ns], refs[n + ns:n + 2 * ns]
        x, y, c = _mesh_pos()
        for gi in range(ng):
            view = sized(in_refs, gi)
            for k in range(7):
                w = pltpu.make_async_remote_copy(
                    src_ref=view, dst_ref=view, send_sem=s_sems[gi * 7 + k], recv_sem=r_sems[gi * 7 + k],
                    device_id=(x, y, c), device_id_type=MESH)
                w.wait_send()
                w.wait_recv()

    res = pl.pallas_call(
        body, name=name, out_shape=tuple(pltpu.HBM(a.shape, a.dtype) for a in arrays),
        in_specs=[HBM_SPEC] * n + [SEM_SPEC] * (2 * ns) + [pl.BlockSpec(memory_space=pl.ANY)],
        out_specs=tuple([HBM_SPEC] * n), input_output_aliases={i: i for i in range(n)},
        compiler_params=pltpu.CompilerParams(has_side_effects=DATAFLOW),
    )(*arrays, *send_sems, *recv_sems, after)
    return list(res)


def gather_layer_start(name, packed, pieces):
    ng = len(packed)
    dests = [lax.empty((NDEV * r, p.shape[1]), p.dtype) for p, pcs in zip(packed, pieces) for (_, r) in pcs]

    def plan(refs, me_i, p_i):
        out, di = [], ng
        for gi in range(ng):
            for (off, r) in pieces[gi]:
                out.append((refs[gi].at[pl.ds(off, r), :], refs[di].at[pl.ds(me_i * r, r), :], gi))
                di += 1
        return out

    return exchange_start(name, list(packed) + dests, ng, plan)


def gather_layer_wait(name, handle, ng, after):
    send_sems, recv_sems, arrays, _ = handle
    out = exchange_wait(name, send_sems, recv_sems, arrays, ng, lambda refs, gi: refs[gi], after)
    return out[ng:]


def scatter_layer_start(name, groups):
    ng = len(groups)
    flat = [a for arrs in groups for a in arrs]
    offs, lands = [], []
    for arrs in groups:
        o, off = [], 0
        for a in arrs:
            r = a.shape[0] // NDEV
            o.append((off, r))
            off += r
        offs.append(o)
        lands.append(jnp.zeros((NDEV, off, arrs[0].shape[1]), arrs[0].dtype))
    nin = len(flat)

    def plan(refs, me_i, p_i):
        out, ai = [], 0
        for gi in range(ng):
            for (off, r) in offs[gi]:
                out.append((refs[ai].at[pl.ds(p_i * r, r), :], refs[nin + gi].at[me_i, pl.ds(off, r), :], gi))
                ai += 1
        return out

    return exchange_start(name, flat + lands, ng, plan), nin


def scatter_layer_wait(name, handle, nin, ng, after):
    send_sems, recv_sems, arrays, _ = handle
    out = exchange_wait(name, send_sems, recv_sems, arrays, ng, lambda refs, gi: refs[nin + gi].at[0], after)
    return out[nin:]


def _pick_tile(n, cap):
    best = None
    for t in range(8, min(n, cap) + 1, 8):
        if n % t == 0:
            best = t
    return best if best is not None else n


def sum_slots(name, land):
    _, R, W = land.shape
    tr = _pick_tile(R, 512)

    def kern(l_ref, o_ref):
        acc = l_ref[0].astype(F32)
        for s in range(1, NDEV):
            acc = acc + l_ref[s].astype(F32)
        o_ref[...] = acc

    return pl.pallas_call(
        kern, name=name, grid=(R // tr,),
        in_specs=[pl.BlockSpec((NDEV, tr, W), lambda i: (0, i, 0))],
        out_specs=pl.BlockSpec((tr, W), lambda i: (i, 0)),
        out_shape=jax.ShapeDtypeStruct((R, W), F32),
        compiler_params=_params("arbitrary"),
    )(land)


def adamw(name, w, g, m, v):
    shp = w.shape
    C = shp[-1]
    R = max(1, math.prod(shp[:-1]))
    tr = _pick_tile(R, 1024)
    w2, g2, m2, v2 = (a.reshape(R, C) for a in (w, g, m, v))

    def kern(w_ref, g_ref, m_ref, v_ref, d_ref, mo_ref, vo_ref):
        gg = g_ref[...]
        mn = ADAM_B1 * m_ref[...] + (1.0 - ADAM_B1) * gg
        vn = ADAM_B2 * v_ref[...] + (1.0 - ADAM_B2) * jnp.square(gg)
        m_hat = mn / (1.0 - ADAM_B1 ** ADAM_STEP)
        v_hat = vn / (1.0 - ADAM_B2 ** ADAM_STEP)
        d_ref[...] = -ADAM_LR * (m_hat / (jnp.sqrt(v_hat) + ADAM_EPS) + ADAM_WD * w_ref[...])
        mo_ref[...] = mn
        vo_ref[...] = vn

    spec = pl.BlockSpec((tr, C), lambda i: (i, 0))
    d, mo, vo = pl.pallas_call(
        kern, name=name, grid=(R // tr,), in_specs=[spec] * 4, out_specs=[spec] * 3,
        out_shape=[jax.ShapeDtypeStruct((R, C), F32)] * 3, compiler_params=_params("arbitrary"),
    )(w2, g2, m2, v2)
    return d.reshape(shp), mo.reshape(shp), vo.reshape(shp)


def build_h0(x2, blk0):
    L0 = x2.shape[0]
    nb = L0 // BLK + 1

    def kern(x_ref, b_ref, o_ref):
        i = pl.program_id(0)

        @pl.when(i == 0)
        def _():
            o_ref[...] = b_ref[...]

        @pl.when(i > 0)
        def _():
            o_ref[...] = x_ref[...]

    return pl.pallas_call(
        kern, name="build_h0", grid=(nb,),
        in_specs=[pl.BlockSpec((BLK, D), lambda i: (jnp.maximum(i - 1, 0), 0)), _const_spec((BLK, D))],
        out_specs=pl.BlockSpec((BLK, D), lambda i: (i, 0)),
        out_shape=jax.ShapeDtypeStruct((L0 + BLK, D), F32), compiler_params=_params("arbitrary"),
    )(x2, blk0)


def final_loss(h, tgt, gf):
    LP = h.shape[0]
    nb = LP // BLK

    def kern(h_ref, t_ref, g_ref, dh_ref, loss_ref, dg_ref):
        i = pl.program_id(0)

        @pl.when(i == 0)
        def _():
            loss_ref[...] = jnp.zeros_like(loss_ref)
            dg_ref[...] = jnp.zeros_like(dg_ref)

        g = g_ref[...]
        hh, r, yv = _rms_fwd(h_ref[...], g)
        valid = (i > 0).astype(F32)
        err = (yv - t_ref[...]) * valid
        loss_ref[...] += 0.5 * jnp.sum(jnp.sum(err * err, axis=1, keepdims=True), axis=0, keepdims=True) / D
        dy = err / D
        dx, dg = _rms_bwd(hh, r, g, dy)
        dh_ref[...] = dx
        dg_ref[...] += dg

    return pl.pallas_call(
        kern, name="final_loss", grid=(nb,),
        in_specs=[pl.BlockSpec((BLK, D), lambda i: (i, 0)),
                  pl.BlockSpec((BLK, D), lambda i: (jnp.maximum(i - 1, 0), 0)), _const_spec((1, D))],
        out_specs=[pl.BlockSpec((BLK, D), lambda i: (i, 0)), _const_spec((8, 128)), _const_spec((1, D))],
        out_shape=[jax.ShapeDtypeStruct((LP, D), F32), jax.ShapeDtypeStruct((8, 128), F32),
                   jax.ShapeDtypeStruct((1, D), F32)],
        compiler_params=_params("arbitrary"),
    )(h, tgt, gf)


def ffn_forward(tag, h, g, wgT, wuT, wd, tm):
    def f1(i, hv, g_ref, wg_ref, wu_ref):
        _, _, n = _rms_fwd(hv, g_ref[...])
        nb = n.astype(BF16)
        G = _nt(nb, wg_ref[...])
        U = _nt(nb, wu_ref[...])
        A = G * _sig(G) * U
        return nb, G, U, A

    n, G, U, A = rowcall(tag + "_up", f1, [h], [g, wgT, wuT],
                         [(D, BF16), (DFF, BF16), (DFF, BF16), (DFF, BF16)], tm=tm)

    def f2(i, av, hv, wd_ref):
        return (hv + 0.5 * _nn(av, wd_ref[...]),)

    (h2,) = rowcall(tag + "_down", f2, [A, h], [wd], [(D, F32)], tm=tm)
    return h2, (h, n, G, U, A)


def ffn_backward(tag, dh, saved, g, wgT, wuT, wd, tm):
    h, n, G, U, A = saved

    def b1(i, dhv, wd_ref):
        dyb = (0.5 * dhv).astype(BF16)
        return _nt(dyb, wd_ref[...]), dyb

    dA, dyb = rowcall(tag + "_bwd_act", b1, [dh], [wd], [(DFF, BF16), (D, BF16)], tm=tm)

    def b2(i, dAv, Gv, Uv, hv, dhv, g_ref, wg_ref, wu_ref):
        dAf = dAv.astype(F32)
        Gf = Gv.astype(F32)
        sg = _sig(Gf)
        dG = (dAf * Uv.astype(F32) * (sg * (1.0 + Gf * (1.0 - sg)))).astype(BF16)
        dU = (dAf * (Gf * sg)).astype(BF16)
        dn = _nn(dG, wg_ref[...]) + _nn(dU, wu_ref[...])
        gv = g_ref[...]
        hh, r, _ = _rms_fwd(hv, gv)
        dx, dg = _rms_bwd(hh, r, gv, dn)
        dx = jnp.where(_row_ok(i, tm), dx, 0.0)
        return dhv + dx, dG, dU, dg

    dh2, dG, dU, dg = rowcall(tag + "_bwd_in", b2, [dA, G, U, h, dh], [g, wgT, wuT],
                              [(D, F32), (DFF, BF16), (DFF, BF16)], [(1, D)], tm=tm)
    dwd = tn_matmul(tag + "_dwd", A, dyb)
    dwgT = tn_matmul(tag + "_dwg", dG, n)
    dwuT = tn_matmul(tag + "_dwu", dU, n)
    return dh2, dg, dwgT, dwuT, dwd


def _alibi_slope(head):
    return float(2.0 ** (-8.0 * (head + 1) / N_HEADS))


def _att_bias(n, nb):
    qi = lax.broadcasted_iota(jnp.int32, (BLK, 4 * BLK), 0)
    cj = lax.broadcasted_iota(jnp.int32, (BLK, 4 * BLK), 1)
    jb = cj - BLK
    dist = jnp.abs(qi + BLK - jb)
    kpos = (n - 1) * BLK + jb
    band_ok = (dist <= WIN) & (kpos >= BLK) & (kpos < nb * BLK)
    is_meta = cj < BLK
    ok = (is_meta & (cj >= PAD)) | (jnp.logical_not(is_meta) & band_ok)
    distf = jnp.where(is_meta, 0, dist).astype(F32)
    maskadd = jnp.where(ok, 0.0, NEG).astype(F32)
    distf4 = jnp.concatenate([distf] * QG, axis=0)
    mask4 = jnp.concatenate([maskadd] * QG, axis=0)
    return distf4, mask4


def _group_col(vals):
    rg = lax.broadcasted_iota(jnp.int32, (QG * BLK, 1), 0) // BLK
    col = jnp.full((QG * BLK, 1), vals[QG - 1], F32)
    for gq in range(QG - 2, -1, -1):
        col = jnp.where(rg == gq, vals[gq], col)
    return col


def _stack_heads(ref_or_val, kh):
    return jnp.concatenate(
        [ref_or_val[:, (kh * QG + gq) * HD:(kh * QG + gq + 1) * HD] for gq in range(QG)], axis=0)


def _stack_keys(km, kp, kc, kn, kh):
    sl = slice(kh * HD, (kh + 1) * HD)
    return jnp.concatenate([km[:, sl], kp[:, sl], kc[:, sl], kn[:, sl]], axis=0)


LOG2E = 1.4426950408889634
LN2 = 0.6931471805599453
QSCALE = SCALE * LOG2E


def _att_update_bias(bias_ref, n, nb):
    @pl.when((n <= 2) | (n == nb - 1))
    def _():
        distf4, mask4 = _att_bias(n, nb)
        for kh in range(N_KV):
            slope_col = _group_col([_alibi_slope(kh * QG + gq) * LOG2E for gq in range(QG)])
            bias_ref[kh] = mask4 - slope_col * distf4


def _att_exp(qs, kb, bias_ref, kh, sink_ref):
    sink_col = _group_col([sink_ref[kh * QG + gq] for gq in range(QG)]) * LOG2E
    s = _nt(qs, kb) + bias_ref[kh]
    m = jnp.maximum(jnp.max(s, axis=1, keepdims=True), sink_col)
    e = jnp.exp2(s - m)
    es = jnp.exp2(sink_col - m)
    inv = 1.0 / (jnp.sum(e, axis=1, keepdims=True) + es)
    return e, es, inv


def attention_forward(tag, q, k, v, sink):
    LP = q.shape[0]
    nb = LP // BLK

    def kern(sink_ref, q_ref, km_ref, kp_ref, kc_ref, kn_ref, vm_ref, vp_ref, vc_ref, vn_ref, o_ref, bias_ref):
        n = pl.program_id(0)
        _att_update_bias(bias_ref, n, nb)
        qv = q_ref[...]
        km, kp, kc, kn = km_ref[...], kp_ref[...], kc_ref[...], kn_ref[...]
        vm, vp, vc, vn = vm_ref[...], vp_ref[...], vc_ref[...], vn_ref[...]
        for kh in range(N_KV):
            qs = _stack_heads(qv, kh)
            kb = _stack_keys(km, kp, kc, kn, kh)
            vb = _stack_keys(vm, vp, vc, vn, kh)
            e, _, inv = _att_exp(qs, kb, bias_ref, kh, sink_ref)
            o = _nn(e.astype(BF16), vb) * inv
            for gq in range(QG):
                hcol = (kh * QG + gq) * HD
                o_ref[:, hcol:hcol + HD] = o[gq * BLK:(gq + 1) * BLK].astype(o_ref.dtype)

    def kvspec(dn):
        return pl.BlockSpec((BLK, N_KV * HD), lambda n: (jnp.clip(n + dn, 0, nb - 1), 0))

    meta_spec = pl.BlockSpec((BLK, N_KV * HD), lambda n: (0, 0))
    return pl.pallas_call(
        kern, name=tag + "_att_fwd", grid=(nb,),
        in_specs=[pl.BlockSpec(memory_space=pltpu.SMEM), pl.BlockSpec((BLK, D), lambda n: (n, 0)),
                  meta_spec, kvspec(-1), kvspec(0), kvspec(1), meta_spec, kvspec(-1), kvspec(0), kvspec(1)],
        out_specs=pl.BlockSpec((BLK, D), lambda n: (n, 0)),
        out_shape=jax.ShapeDtypeStruct((LP, D), BF16),
        scratch_shapes=[pltpu.VMEM((N_KV, QG * BLK, 4 * BLK), F32)], compiler_params=_params("arbitrary"),
    )(sink, q, k, k, k, k, v, v, v, v)


def attention_backward(tag, q, k, v, do, sink):
    LP = q.shape[0]
    nb = LP // BLK
    KW = N_KV * HD

    def kern(sink_ref, q_ref, do_ref, km_ref, kp_ref, kc_ref, kn_ref, vm_ref, vp_ref, vc_ref, vn_ref,
             dq_ref, dk_ref, dv_ref, dkm_ref, dvm_ref, dsink_ref, bias_ref, rk, rv, fk, fv):
        n = pl.program_id(0)

        @pl.when(n == 0)
        def _():
            dkm_ref[...] = jnp.zeros_like(dkm_ref)
            dvm_ref[...] = jnp.zeros_like(dvm_ref)
            dsink_ref[...] = jnp.zeros_like(dsink_ref)
            rk[...] = jnp.zeros_like(rk)
            rv[...] = jnp.zeros_like(rv)

        _att_update_bias(bias_ref, n, nb)

        @pl.when(n < nb)
        def _():
            qv, dov = q_ref[...], do_ref[...]
            km, kp, kc, kn = km_ref[...], kp_ref[...], kc_ref[...], kn_ref[...]
            vm, vp, vc, vn = vm_ref[...], vp_ref[...], vc_ref[...], vn_ref[...]
            lane = lax.broadcasted_iota(jnp.int32, (8, 128), 1)
            dsink = jnp.zeros((8, 128), F32)
            for kh in range(N_KV):
                qs = _stack_heads(qv, kh)
                dos = _stack_heads(dov, kh)
                kb = _stack_keys(km, kp, kc, kn, kh)
                vb = _stack_keys(vm, vp, vc, vn, kh)
                dp = _nt(dos, vb)
                e, es, inv = _att_exp(qs, kb, bias_ref, kh, sink_ref)
                delta = inv * jnp.sum(e * dp, axis=1, keepdims=True)
                dsu = (e * (dp - delta)).astype(BF16)
                dqs = _nn(dsu, kb) * (inv * SCALE)
                dkt = _tn((qs.astype(F32) * (inv * LN2)).astype(BF16), dsu)
                dvt = _tn((dos.astype(F32) * inv).astype(BF16), e.astype(BF16))
                dsk = -(es * inv * delta)
                for gq in range(QG):
                    hcol = (kh * QG + gq) * HD
                    dq_ref[:, hcol:hcol + HD] = dqs[gq * BLK:(gq + 1) * BLK].astype(dq_ref.dtype)
                    tot = jnp.sum(dsk[gq * BLK:(gq + 1) * BLK], axis=0, keepdims=True)
                    dsink = dsink + jnp.where(lane == kh * QG + gq, tot, 0.0)
                hs = slice(kh * HD, (kh + 1) * HD)
                dkm_ref[hs, :] += dkt[:, 0:BLK]
                dvm_ref[hs, :] += dvt[:, 0:BLK]
                for ring, fin, part in ((rk, fk, dkt), (rv, fv, dvt)):
                    fin[hs, :] = ring[0, hs, :] + part[:, BLK:2 * BLK]
                    ring[0, hs, :] = ring[1, hs, :] + part[:, 2 * BLK:3 * BLK]
                    ring[1, hs, :] = part[:, 3 * BLK:4 * BLK]
            dsink_ref[...] += dsink
            dk_ref[...] = fk[...].T.astype(dk_ref.dtype)
            dv_ref[...] = fv[...].T.astype(dv_ref.dtype)

        @pl.when(n == nb)
        def _():
            dk_ref[...] = rk[0].T.astype(dk_ref.dtype)
            dv_ref[...] = rv[0].T.astype(dv_ref.dtype)

    def kvspec(dn):
        return pl.BlockSpec((BLK, KW), lambda n: (jnp.clip(jnp.minimum(n, nb - 1) + dn, 0, nb - 1), 0))

    meta_spec = pl.BlockSpec((BLK, KW), lambda n: (0, 0))
    rowspec = pl.BlockSpec((BLK, D), lambda n: (jnp.minimum(n, nb - 1), 0))
    emit_spec = pl.BlockSpec((BLK, KW), lambda n: (jnp.clip(n - 1, 1, nb - 1), 0))
    dq, dk, dv, dkm, dvm, dsink = pl.pallas_call(
        kern, name=tag + "_att_bwd", grid=(nb + 1,),
        in_specs=[pl.BlockSpec(memory_space=pltpu.SMEM), rowspec, rowspec,
                  meta_spec, kvspec(-1), kvspec(0), kvspec(1), meta_spec, kvspec(-1), kvspec(0), kvspec(1)],
        out_specs=[rowspec, emit_spec, emit_spec, _const_spec((KW, BLK)), _const_spec((KW, BLK)),
                   _const_spec((8, 128))],
        out_shape=[jax.ShapeDtypeStruct((LP, D), BF16), jax.ShapeDtypeStruct((LP, KW), BF16),
                   jax.ShapeDtypeStruct((LP, KW), BF16), jax.ShapeDtypeStruct((KW, BLK), F32),
                   jax.ShapeDtypeStruct((KW, BLK), F32), jax.ShapeDtypeStruct((8, 128), F32)],
        scratch_shapes=[pltpu.VMEM((N_KV, QG * BLK, 4 * BLK), F32), pltpu.VMEM((2, KW, BLK), F32),
                        pltpu.VMEM((2, KW, BLK), F32), pltpu.VMEM((KW, BLK), F32), pltpu.VMEM((KW, BLK), F32)],
        compiler_params=_params("arbitrary"),
    )(sink, q, do, k, k, k, k, v, v, v, v)
    dk = lax.dynamic_update_slice(dk, dkm.T.astype(BF16), (0, 0))
    dv = lax.dynamic_update_slice(dv, dvm.T.astype(BF16), (0, 0))
    return dq, dk, dv, dsink


SCAN_LANES = 1024


def _scan_tile(xr, xi, cr, ci, a8, tab, seg, reverse):
    sub = lax.broadcasted_iota(jnp.int32, (8, SCAN_LANES), 0)
    for c0 in range(0, NST, SCAN_LANES):
        cs = pl.ds(c0, SCAN_LANES)
        ar = a8[0, :, cs]
        ai = a8[1, :, cs]

        def rows(j):
            jj = (seg - 1 - j) if reverse else j
            return pl.ds(jj * 8, 8)

        def step1(j, carry):
            vr, vi = carry
            rs = rows(j)
            nr = ar * vr - ai * vi + xr[rs, cs]
            ni = ar * vi + ai * vr + xi[rs, cs]
            xr[rs, cs] = nr
            xi[rs, cs] = ni
            return nr, ni

        zero = jnp.zeros((8, SCAN_LANES), F32)
        vr, vi = zero, zero
        for j in range(seg):
            vr, vi = step1(j, (vr, vi))
        for t, s in enumerate((1, 2, 4)):
            sh = (8 - s) if reverse else s
            sr = pltpu.roll(vr, sh, 0)
            si = pltpu.roll(vi, sh, 0)
            tr = tab[2 * t, :, cs]
            ti = tab[2 * t + 1, :, cs]
            vr, vi = vr + tr * sr - ti * si, vi + tr * si + ti * sr
        pr = tab[6, :, cs]
        pi = tab[7, :, cs]
        c_r = cr[:, cs]
        c_i = ci[:, cs]
        vr, vi = vr + pr * c_r - pi * c_i, vi + pr * c_i + pi * c_r
        edge = 7 if reverse else 0
        last = 0 if reverse else 7
        sh = 7 if reverse else 1
        in_r = jnp.where(sub == edge, c_r, pltpu.roll(vr, sh, 0))
        in_i = jnp.where(sub == edge, c_i, pltpu.roll(vi, sh, 0))
        cr[:, cs] = jnp.broadcast_to(vr[last:last + 1, :], (8, SCAN_LANES))
        ci[:, cs] = jnp.broadcast_to(vi[last:last + 1, :], (8, SCAN_LANES))

        def step2(j, carry):
            dr, di = carry
            rs = rows(j)
            ndr = ar * dr - ai * di
            ndi = ar * di + ai * dr
            xr[rs, cs] += ndr
            xi[rs, cs] += ndi
            return ndr, ndi

        dr, di = in_r, in_i
        for j in range(seg):
            dr, di = step2(j, (dr, di))


ST_T = 4 * SP * 2
CH_T = 128


def _load_segmented(ref, scr, seg):
    out = []
    for ct in range(4):
        scr[ct] = ref[:, ct * CH_T:(ct + 1) * CH_T]
        out.append(jnp.concatenate([scr[ct, pl.ds(j, 8, stride=seg), :] for j in range(seg)], axis=0))
    return out


def _store_segmented(ref, scr, vals, seg):
    for ct in range(4):
        for j in range(seg):
            scr[ct, pl.ds(j, 8, stride=seg), :] = vals[ct][8 * j:8 * j + 8]
        ref[:, ct * CH_T:(ct + 1) * CH_T] = scr[ct]


def ssm_dir_forward(tag, u, bpr, bpi, cpr, cpi, a8, tab, reverse, tm):
    LP = u.shape[0]
    nt = LP // tm
    seg = tm // 8

    def rix(i):
        return (nt - 1 - i) if reverse else i

    def kern(u_ref, bpr_ref, bpi_ref, cpr_ref, cpi_ref, a8_ref, tab_ref, xre_ref, xim_ref, y_ref,
             xr, xi, ys, cr, ci):
        i = pl.program_id(0)

        @pl.when(i == 0)
        def _():
            cr[...] = jnp.zeros_like(cr)
            ci[...] = jnp.zeros_like(ci)

        ub = _load_segmented(u_ref, ys, seg)
        for ct in range(4):
            uc = ub[ct].astype(BF16)
            xr[:, ct * ST_T:(ct + 1) * ST_T] = _nn(uc, bpr_ref[ct * CH_T:(ct + 1) * CH_T, :])
            xi[:, ct * ST_T:(ct + 1) * ST_T] = _nn(uc, bpi_ref[ct * CH_T:(ct + 1) * CH_T, :])
        _scan_tile(xr, xi, cr, ci, a8_ref, tab_ref, seg, reverse)
        xrb = xr[...].astype(BF16)
        xib = xi[...].astype(BF16)
        xre_ref[...] = xrb
        xim_ref[...] = xib
        yv = []
        for ct in range(4):
            ss = slice(ct * ST_T, (ct + 1) * ST_T)
            yv.append(_nn(xrb[:, ss], cpr_ref[ss, :]) - _nn(xib[:, ss], cpi_ref[ss, :]))
        _store_segmented(y_ref, ys, yv, seg)

    row = lambda w: pl.BlockSpec((tm, w), lambda i: (rix(i), 0))
    return pl.pallas_call(
        kern, name=tag, grid=(nt,),
        in_specs=[row(SW), _const_spec(bpr.shape), _const_spec(bpi.shape), _const_spec(cpr.shape),
                  _const_spec(cpi.shape), _const_spec(a8.shape), _const_spec(tab.shape)],
        out_specs=[row(NST), row(NST), row(SW)],
        out_shape=[jax.ShapeDtypeStruct((LP, NST), BF16), jax.ShapeDtypeStruct((LP, NST), BF16),
                   jax.ShapeDtypeStruct((LP, SW), F32)],
        scratch_shapes=[pltpu.VMEM((tm, NST), F32), pltpu.VMEM((tm, NST), F32), pltpu.VMEM((4, tm, CH_T), F32),
                        pltpu.VMEM((8, NST), F32), pltpu.VMEM((8, NST), F32)],
        compiler_params=_params("arbitrary"),
    )(u, bpr, bpi, cpr, cpi, a8, tab)


def ssm_dir_backward(tag, dy, xre, xim, u, bpr, bpi, cpr, cpi, a8_adj, tab_adj, reverse, tm):
    LP = u.shape[0]
    nt = LP // tm
    seg = tm // 8

    def rix(i):
        return (nt - 1 - i) if reverse else i

    def kern(dy_ref, xre_ref, xim_ref, u_ref, bpr_ref, bpi_ref, cpr_ref, cpi_ref, a8_ref, tab_ref,
             du_ref, gbr_ref, gbi_ref, gcr_ref, gci_ref, sr_ref, si_ref, lr, li, gr, gi, dus, cr, ci):
        i = pl.program_id(0)

        @pl.when(i == 0)
        def _():
            cr[...] = jnp.zeros_like(cr)
            ci[...] = jnp.zeros_like(ci)
            for r in (gbr_ref, gbi_ref, gcr_ref, gci_ref, sr_ref, si_ref):
                r[...] = jnp.zeros_like(r)

        dyb = [v.astype(BF16) for v in _load_segmented(dy_ref, dus, seg)]
        ub = [v.astype(BF16) for v in _load_segmented(u_ref, dus, seg)]
        for ct in range(4):
            ss = slice(ct * ST_T, (ct + 1) * ST_T)
            dc = dyb[ct]
            g_re = _nt(dc, cpr_ref[ss, :])
            g_im = -_nt(dc, cpi_ref[ss, :])
            lr[:, ss] = g_re
            li[:, ss] = g_im
            gr[:, ss] = g_re
            gi[:, ss] = g_im
        _scan_tile(lr, li, cr, ci, a8_ref, tab_ref, seg, reverse)
        lam_r = lr[...]
        lam_i = li[...]
        wr = lam_r - gr[...]
        wi = lam_i - gi[...]
        xr = xre_ref[...].astype(F32)
        xi = xim_ref[...].astype(F32)
        sr_ref[...] += jnp.sum(wr * xr + wi * xi, axis=0, keepdims=True)
        si_ref[...] += jnp.sum(wi * xr - wr * xi, axis=0, keepdims=True)
        lrb = lam_r.astype(BF16)
        lib = lam_i.astype(BF16)
        xrb = xre_ref[...]
        xib = xim_ref[...]
        duv = []
        for ct in range(4):
            ss = slice(ct * ST_T, (ct + 1) * ST_T)
            cs = slice(ct * CH_T, (ct + 1) * CH_T)
            duv.append(_nt(lrb[:, ss], bpr_ref[cs, :]) + _nt(lib[:, ss], bpi_ref[cs, :]))
            gbr_ref[cs, :] += _tn(ub[ct], lrb[:, ss])
            gbi_ref[cs, :] += _tn(ub[ct], lib[:, ss])
            gcr_ref[cs, :] += _tn(dyb[ct], xrb[:, ss])
            gci_ref[cs, :] -= _tn(dyb[ct], xib[:, ss])
        _store_segmented(du_ref, dus, duv, seg)

    row = lambda w: pl.BlockSpec((tm, w), lambda i: (rix(i), 0))
    acc = _const_spec((SW, ST_T))
    vec = _const_spec((1, NST))
    return pl.pallas_call(
        kern, name=tag, grid=(nt,),
        in_specs=[row(SW), row(NST), row(NST), row(SW), _const_spec(bpr.shape), _const_spec(bpi.shape),
                  _const_spec(cpr.shape), _const_spec(cpi.shape), _const_spec(a8_adj.shape),
                  _const_spec(tab_adj.shape)],
        out_specs=[row(SW), acc, acc, acc, acc, vec, vec],
        out_shape=[jax.ShapeDtypeStruct((LP, SW), F32)] + [jax.ShapeDtypeStruct((SW, ST_T), F32)] * 4
        + [jax.ShapeDtypeStruct((1, NST), F32)] * 2,
        scratch_shapes=[pltpu.VMEM((tm, NST), F32)] * 4 + [pltpu.VMEM((4, tm, CH_T), F32)]
        + [pltpu.VMEM((8, NST), F32)] * 2,
        compiler_params=_params("arbitrary"),
    )(dy, xre, xim, u, bpr, bpi, cpr, cpi, a8_adj, tab_adj)


def _ssm_disc(lam_re, lam_im, log_dt, b_re, b_im):
    dt = jnp.exp(log_dt)[:, None]
    mag = jnp.exp(lam_re * dt)
    a_re = mag * jnp.cos(lam_im * dt)
    a_im = mag * jnp.sin(lam_im * dt)
    den = lam_re * lam_re + lam_im * lam_im
    f_re = ((a_re - 1.0) * lam_re + a_im * lam_im) / den
    f_im = (a_im * lam_re - (a_re - 1.0) * lam_im) / den
    bb_re = f_re[:, :, None] * b_re - f_im[:, :, None] * b_im
    bb_im = f_re[:, :, None] * b_im + f_im[:, :, None] * b_re
    return a_re, a_im, bb_re, bb_im


def _scan_tables(lam_re, lam_im, log_dt, conj, reverse, seg):
    dt = jnp.exp(log_dt)[:, None]
    lr = (lam_re * dt).reshape(1, NST)
    li = (lam_im * dt).reshape(1, NST) * (-1.0 if conj else 1.0)
    t = jnp.arange(8, dtype=F32)[:, None]

    def power(kk):
        mag = jnp.exp(kk * lr)
        return mag * jnp.cos(kk * li), mag * jnp.sin(kk * li)

    ones = jnp.ones((8, 1), F32)
    a8 = jnp.stack(power(ones)).astype(F32)
    tabs = []
    for s in (1, 2, 4):
        mask = (t <= 7 - s) if reverse else (t >= s)
        pr, pi = power(float(s * seg) * ones)
        tabs += [jnp.where(mask, pr, 0.0), jnp.where(mask, pi, 0.0)]
    kk = ((8.0 - t) if reverse else (t + 1.0)) * float(seg)
    pr, pi = power(kk)
    tabs += [pr, pi]
    return a8, jnp.stack(tabs).astype(F32)


def _pack_b(bb):
    t = bb.transpose(0, 2, 1).reshape(4, 8, SCH, SP)
    eye = jnp.eye(8, dtype=bb.dtype)
    return jnp.einsum('tgcp,gh->tgchp', t, eye).reshape(SW, ST_T)


def _pack_c(cc):
    t = cc.transpose(0, 2, 1).reshape(4, 8, SP, SCH)
    eye = jnp.eye(8, dtype=cc.dtype)
    return jnp.einsum('tgpc,gh->tgphc', t, eye).reshape(NST, CH_T)


def _unpack_diag(acc):
    t = acc.reshape(4, 8, SCH, 8, SP)
    eye = jnp.eye(8, dtype=acc.dtype)
    return jnp.einsum('tgchp,gh->tgcp', t, eye).reshape(SGRP, SCH, SP)


def _gelu(y):
    k0 = math.sqrt(2.0 / math.pi)
    inner = k0 * (y + 0.044715 * y * y * y)
    th = jnp.tanh(inner)
    z = 0.5 * y * (1.0 + th)
    dz = 0.5 * (1.0 + th) + 0.5 * y * (1.0 - th * th) * k0 * (1.0 + 3.0 * 0.044715 * y * y)
    return z, dz


Q0, K0, V0, U0, GS0, GA0, IN_COLS = 0, 1024, 1280, 1536, 2048, 3072, 4096


def mixer_forward(tag, h, p, tm):
    g, winT, wglu, wbsT, wba, wout = p["g"], p["winT"], p["wglu"], p["wbsT"], p["wba"], p["wout"]

    def proj(i, hv, g_ref, w_ref):
        _, _, n = _rms_fwd(hv, g_ref[...])
        nb = n.astype(BF16)
        return (nb, _nt(nb, w_ref[Q0:K0, :]) * QSCALE, _nt(nb, w_ref[K0:V0, :]), _nt(nb, w_ref[V0:U0, :]),
                _nt(nb, w_ref[U0:GS0, :]), _nt(nb, w_ref[GS0:GA0, :]), _nt(nb, w_ref[GA0:IN_COLS, :]))

    n, q, k, v, u, gs, ga = rowcall(
        tag + "_proj", proj, [h], [g, winT],
        [(D, BF16), (D, BF16), (N_KV * HD, BF16), (N_KV * HD, BF16), (SW, F32), (D, F32), (D, F32)], tm=tm)

    ya = attention_forward(tag, q, k, v, p["sink"])

    states, ydir = [], []
    for dr in range(2):
        s = p["ssm"][dr]
        xre, xim, yd = ssm_dir_forward(f"{tag}_ssm_fwd{dr}", u, s["bpr"], s["bpi"], s["cpr"], s["cpi"],
                                       s["a8"], s["tab"], dr == 1, tm)
        states.append((xre, xim))
        ydir.append(yd)

    def merge(i, y0, y1, uv, yav, gsv, gav, hv, d_ref, wglu_ref, wbs_ref, wba_ref, wout_ref):
        ypre = y0 + y1 + d_ref[...] * uv
        z, _ = _gelu(ypre)
        zb = z.astype(BF16)
        t = _nn(zb, wglu_ref[...])
        ysb = (z * _sig(t)).astype(BF16)
        bs = _nt(ysb, wbs_ref[...])
        ba = _nn(yav, wba_ref[...])
        mg = _sig(gsv) * bs + _sig(gav) * ba
        mg = jnp.where(_row_ok(i, tm), mg, 0.0).astype(BF16)
        return ypre, zb, t, ysb, bs, ba, mg, hv + _nn(mg, wout_ref[...])

    ypre, zb, t, ys, bs, ba, mg, h2 = rowcall(
        tag + "_merge", merge, [ydir[0], ydir[1], u, ya, gs, ga, h], [p["d"], wglu, wbsT, wba, wout],
        [(SW, F32), (SW, BF16), (SW, F32), (SW, BF16), (D, BF16), (D, BF16), (D, BF16), (D, F32)], tm=tm)
    saved = dict(h=h, n=n, q=q, k=k, v=v, u=u, gs=gs, ga=ga, ya=ya, states=states, ypre=ypre, zb=zb, t=t,
                 ys=ys, bs=bs, ba=ba, mg=mg)
    return h2, saved


def mixer_backward(tag, dh, sv, p, tm):
    g, winT, wglu, wbsT, wba, wout = p["g"], p["winT"], p["wglu"], p["wbsT"], p["wba"], p["wout"]

    def y1(i, dhv, bsv, bav, gsv, gav, ypv, tv, uv, wout_ref, wbs_ref, wba_ref, d_ref, wglu_ref):
        dhb = dhv.astype(BF16)
        dmg = _nt(dhb, wout_ref[...])
        dmg = jnp.where(_row_ok(i, tm), dmg, 0.0)
        sgs = _sig(gsv)
        sga = _sig(gav)
        dbs = (dmg * sgs).astype(BF16)
        dba = (dmg * sga).astype(BF16)
        dgs = dmg * bsv.astype(F32) * sgs * (1.0 - sgs)
        dga = dmg * bav.astype(F32) * sga * (1.0 - sga)
        dys = _nn(dbs, wbs_ref[...])
        dya = _nt(dba, wba_ref[...])
        z, dz_dy = _gelu(ypv)
        st = _sig(tv)
        dt_ = dys * z * st * (1.0 - st)
        dz = dys * st + _nt(dt_.astype(BF16), wglu_ref[...])
        dyp = dz * dz_dy
        return (dbs, dba, dgs, dga, dhb, dya, dyp, dyp * d_ref[...], dt_,
                jnp.sum(dyp * uv, axis=0, keepdims=True))

    dbs, dba, dgs, dga, dhb, dya, dypb, du0, dtb, dd = rowcall(
        tag + "_bwd_merge", y1,
        [dh, sv["bs"], sv["ba"], sv["gs"], sv["ga"], sv["ypre"], sv["t"], sv["u"]],
        [wout, wbsT, wba, p["d"], wglu],
        [(D, BF16)] * 6 + [(SW, F32), (SW, F32), (SW, BF16)], [(1, SW)], tm=tm)
    dwout = tn_matmul(tag + "_dwout", sv["mg"], dhb)
    dwbsT = tn_matmul(tag + "_dwbs", dbs, sv["ys"])
    dwba = tn_matmul(tag + "_dwba", sv["ya"], dba)
    dwglu = tn_matmul(tag + "_dwglu", sv["zb"], dtb)

    du_dirs, ssm_sums = [], []
    for dr in range(2):
        s = p["ssm"][dr]
        xre, xim = sv["states"][dr]
        res = ssm_dir_backward(f"{tag}_ssm_bwd{dr}", dypb, xre, xim, sv["u"], s["bpr"], s["bpi"], s["cpr"],
                               s["cpi"], s["a8_adj"], s["tab_adj"], dr == 0, tm)
        du_dirs.append(res[0])
        ssm_sums.append(res[1:])

    dq, dk, dv, dsink = attention_backward(tag, sv["q"], sv["k"], sv["v"], dya, p["sink"])

    def x1b(i, dqv, dkv, dvv, du0v, du1v, du2v, dgsv, dgav, hv, dhv, g_ref, w_ref):
        dub = (du0v + du1v + du2v).astype(BF16)
        dn = (_nn(dqv, w_ref[Q0:K0, :]) + _nn(dkv, w_ref[K0:V0, :]) + _nn(dvv, w_ref[V0:U0, :])
              + _nn(dub, w_ref[U0:GS0, :]) + _nn(dgsv, w_ref[GS0:GA0, :]) + _nn(dgav, w_ref[GA0:IN_COLS, :]))
        gv = g_ref[...]
        hh, r, _ = _rms_fwd(hv, gv)
        dx, dg = _rms_bwd(hh, r, gv, dn)
        dx = jnp.where(_row_ok(i, tm), dx, 0.0)
        return dhv + dx, dub, dg

    dh2, dub, dg = rowcall(tag + "_bwd_in", x1b,
                           [dq, dk, dv, du0, du_dirs[0], du_dirs[1], dgs, dga, sv["h"], dh], [g, winT],
                           [(D, F32), (SW, BF16)], [(1, D)], tm=tm)
    n = sv["n"]
    dwinT = jnp.concatenate([tn_matmul(f"{tag}_dwin{j}", piece, n)
                             for j, piece in enumerate((dq, dk, dv, dub, dgs, dga))], axis=0)
    grads = dict(g=dg, d=dd, sink=dsink, ssm=ssm_sums, winT=dwinT, wglu=dwglu, wbsT=dwbsT, wba=dwba, wout=dwout)
    return dh2, grads


W1024 = ("f1_wgT", "f1_wuT", "f1_wd", "winT", "wba", "wout", "f2_wgT", "f2_wuT", "f2_wd")
W512 = ("wglu", "wbsT")
PART_F1 = ("f1_wgT", "f1_wuT", "f1_wd")
PART_MIX = ("winT", "wba", "wout", "wglu", "wbsT")
PART_F2 = ("f2_wgT", "f2_wuT", "f2_wd")
PER_LAYER_SMALL = ("ffn1_norm", "mix_norm", "ffn2_norm", "ssm_lam_re", "ssm_lam_im", "ssm_log_dt",
                   "ssm_b_re", "ssm_b_im", "ssm_c_re", "ssm_c_im", "ssm_d", "attn_sink")
SMALL = ("ffn1_norm", "mix_norm", "ffn2_norm", "final_norm", "ssm_lam_re", "ssm_lam_im", "ssm_log_dt",
         "ssm_b_re", "ssm_b_im", "ssm_c_re", "ssm_c_im", "ssm_d", "attn_sink")


def kernel(x, meta_tokens, ffn1_norm, ffn1_w_gate, ffn1_w_up, ffn1_w_down, mix_norm, w_in, ssm_lam_re, ssm_lam_im, ssm_log_dt, ssm_b_re, ssm_b_im, ssm_c_re, ssm_c_im, ssm_d, ssm_w_glu, attn_sink, w_branch_ssm, w_branch_attn, w_out, ffn2_norm, ffn2_w_gate, ffn2_w_up, ffn2_w_down, final_norm, loss_target, m_meta_tokens, m_ffn1_norm, m_ffn1_w_gate, m_ffn1_w_up, m_ffn1_w_down, m_mix_norm, m_w_in, m_ssm_lam_re, m_ssm_lam_im, m_ssm_log_dt, m_ssm_b_re, m_ssm_b_im, m_ssm_c_re, m_ssm_c_im, m_ssm_d, m_ssm_w_glu, m_attn_sink, m_w_branch_ssm, m_w_branch_attn, m_w_out, m_ffn2_norm, m_ffn2_w_gate, m_ffn2_w_up, m_ffn2_w_down, m_final_norm, v_meta_tokens, v_ffn1_norm, v_ffn1_w_gate, v_ffn1_w_up, v_ffn1_w_down, v_mix_norm, v_w_in, v_ssm_lam_re, v_ssm_lam_im, v_ssm_log_dt, v_ssm_b_re, v_ssm_b_im, v_ssm_c_re, v_ssm_c_im, v_ssm_d, v_ssm_w_glu, v_attn_sink, v_w_branch_ssm, v_w_branch_attn, v_w_out, v_ffn2_norm, v_ffn2_w_gate, v_ffn2_w_up, v_ffn2_w_down, v_final_norm):
    weights = dict(meta_tokens=meta_tokens, ffn1_norm=ffn1_norm, ffn1_w_gate=ffn1_w_gate, ffn1_w_up=ffn1_w_up, ffn1_w_down=ffn1_w_down, mix_norm=mix_norm, w_in=w_in, ssm_lam_re=ssm_lam_re, ssm_lam_im=ssm_lam_im, ssm_log_dt=ssm_log_dt, ssm_b_re=ssm_b_re, ssm_b_im=ssm_b_im, ssm_c_re=ssm_c_re, ssm_c_im=ssm_c_im, ssm_d=ssm_d, ssm_w_glu=ssm_w_glu, attn_sink=attn_sink, w_branch_ssm=w_branch_ssm, w_branch_attn=w_branch_attn, w_out=w_out, ffn2_norm=ffn2_norm, ffn2_w_gate=ffn2_w_gate, ffn2_w_up=ffn2_w_up, ffn2_w_down=ffn2_w_down, final_norm=final_norm)
    mom_m = dict(meta_tokens=m_meta_tokens, ffn1_norm=m_ffn1_norm, ffn1_w_gate=m_ffn1_w_gate, ffn1_w_up=m_ffn1_w_up, ffn1_w_down=m_ffn1_w_down, mix_norm=m_mix_norm, w_in=m_w_in, ssm_lam_re=m_ssm_lam_re, ssm_lam_im=m_ssm_lam_im, ssm_log_dt=m_ssm_log_dt, ssm_b_re=m_ssm_b_re, ssm_b_im=m_ssm_b_im, ssm_c_re=m_ssm_c_re, ssm_c_im=m_ssm_c_im, ssm_d=m_ssm_d, ssm_w_glu=m_ssm_w_glu, attn_sink=m_attn_sink, w_branch_ssm=m_w_branch_ssm, w_branch_attn=m_w_branch_attn, w_out=m_w_out, ffn2_norm=m_ffn2_norm, ffn2_w_gate=m_ffn2_w_gate, ffn2_w_up=m_ffn2_w_up, ffn2_w_down=m_ffn2_w_down, final_norm=m_final_norm)
    mom_v = dict(meta_tokens=v_meta_tokens, ffn1_norm=v_ffn1_norm, ffn1_w_gate=v_ffn1_w_gate, ffn1_w_up=v_ffn1_w_up, ffn1_w_down=v_ffn1_w_down, mix_norm=v_mix_norm, w_in=v_w_in, ssm_lam_re=v_ssm_lam_re, ssm_lam_im=v_ssm_lam_im, ssm_log_dt=v_ssm_log_dt, ssm_b_re=v_ssm_b_re, ssm_b_im=v_ssm_b_im, ssm_c_re=v_ssm_c_re, ssm_c_im=v_ssm_c_im, ssm_d=v_ssm_d, ssm_w_glu=v_ssm_w_glu, attn_sink=v_attn_sink, w_branch_ssm=v_w_branch_ssm, w_branch_attn=v_w_branch_attn, w_out=v_w_out, ffn2_norm=v_ffn2_norm, ffn2_w_gate=v_ffn2_w_gate, ffn2_w_up=v_ffn2_w_up, ffn2_w_down=v_ffn2_w_down, final_norm=v_final_norm)
    names = list(weights)

    L0 = x.shape[1]
    LP = L0 + BLK
    tm = 384 if LP % 384 == 0 else BLK
    x_i, y_i, c_i = lax.axis_index("x"), lax.axis_index("y"), lax.axis_index("c")
    me = 4 * x_i + 2 * y_i + c_i

    def canon(l):
        return dict(
            f1_wgT=ffn1_w_gate[l].T, f1_wuT=ffn1_w_up[l].T, f1_wd=ffn1_w_down[l],
            winT=w_in[l].T, wba=w_branch_attn[l], wout=w_out[l],
            f2_wgT=ffn2_w_gate[l].T, f2_wuT=ffn2_w_up[l].T, f2_wd=ffn2_w_down[l],
            wglu=ssm_w_glu[l], wbsT=w_branch_ssm[l].T)

    shards = [{nm: a.astype(BF16) for nm, a in canon(l).items()} for l in range(DEPTH)]

    def rows_of(nm):
        return shards[0][nm].shape[0]

    def width_groups(names_):
        return [g for g in ([nm for nm in names_ if nm in W1024], [nm for nm in names_ if nm in W512]) if g]

    def pieces_for(group):
        out, off = [], 0
        for nm in group:
            out.append((off, rows_of(nm)))
            off += rows_of(nm)
        return out

    def start_gather(tag, l, names_):
        groups = width_groups(names_)
        packed = [jnp.concatenate([shards[l][nm] for nm in g], axis=0) for g in groups]
        return gather_layer_start(tag, packed, [pieces_for(g) for g in groups]), groups

    def finish_gather(tag, l, started_, after):
        handle, groups = started_
        dests = gather_layer_wait(tag, handle, len(groups), after)
        out = {}
        for nm, dest in zip([nm for g in groups for nm in g], dests):
            sh = shards[l][nm]
            out[nm] = lax.dynamic_update_slice(dest, sh, (me * sh.shape[0], 0))
        return out

    g1, gm = all_gather_pieces(
        "gather_weights_first",
        [(jnp.concatenate([shards[0][nm] for nm in PART_F1], axis=0), pieces_for(PART_F1)),
         (meta_tokens, [(0, N_META)])])
    first_weights = dict(zip(PART_F1, g1))
    meta_full = gm[0].reshape(NDEV, N_META, D // NDEV).transpose(1, 0, 2).reshape(N_META, D)
    gather_started = [start_gather("gather_start_l0", 0, PART_MIX + PART_F2)]
    gather_started += [start_gather(f"gather_start_l{l}", l, W1024 + W512) for l in range(1, DEPTH)]
    started = sum(st[0][3][0, 0] for st in gather_started)
    full = [None] * DEPTH

    def disc_all(lre, lim, ldt, bre, bim):
        return _ssm_disc(lre, lim, ldt, bre, bim)

    ssm_p, ssm_vjp = [], []
    for l in range(DEPTH):
        row, vrow = [], []
        for dr in range(2):
            args = (ssm_lam_re[l, dr], ssm_lam_im[l, dr], ssm_log_dt[l, dr], ssm_b_re[l, dr], ssm_b_im[l, dr])
            (a_re, a_im, bb_re, bb_im), vjp = jax.vjp(disc_all, *args)
            a8, tab = _scan_tables(args[0], args[1], args[2], False, dr == 1, tm // 8)
            a8_adj, tab_adj = _scan_tables(args[0], args[1], args[2], True, dr == 0, tm // 8)
            row.append(dict(
                bpr=_pack_b(bb_re).astype(BF16), bpi=_pack_b(bb_im).astype(BF16),
                cpr=_pack_c(ssm_c_re[l, dr]).astype(BF16), cpi=_pack_c(ssm_c_im[l, dr]).astype(BF16),
                a8=a8, tab=tab, a8_adj=a8_adj, tab_adj=tab_adj, a_re=a_re, a_im=a_im))
            vrow.append(vjp)
        ssm_p.append(row)
        ssm_vjp.append(vrow)

    blk0 = jnp.concatenate([jnp.zeros((PAD, D), F32), meta_full.astype(F32)], axis=0)
    h = build_h0(x[0], blk0)
    saved = []
    for l in range(DEPTH):
        w = dict(first_weights) if l == 0 else finish_gather(f"gather_wait_l{l}", l, gather_started[l], h)
        full[l] = w
        g1n, g2n = ffn1_norm[l][None, :], ffn2_norm[l][None, :]
        if l == 0:
            g1n = g1n + started
        h, s1 = ffn_forward("ffn1", h, g1n, w["f1_wgT"], w["f1_wuT"], w["f1_wd"], tm)
        if l == 0:
            w.update(finish_gather("gather_wait_l0", 0, gather_started[0], h))
        mp = dict(g=mix_norm[l][None, :], winT=w["winT"], wglu=w["wglu"], wbsT=w["wbsT"], wba=w["wba"],
                  wout=w["wout"], d=ssm_d[l][None, :], sink=attn_sink[l], ssm=ssm_p[l])
        h, s2 = mixer_forward("mix", h, mp, tm)
        h, s3 = ffn_forward("ffn2", h, g2n, w["f2_wgT"], w["f2_wuT"], w["f2_wd"], tm)
        saved.append((s1, s2, s3, mp, g1n, g2n))

    dh, loss_acc, dgf = final_loss(h, loss_target[0], final_norm[None, :])
    loss = lax.psum(loss_acc[0, 0], MESH_AXES)

    small = {nm: [None] * DEPTH for nm in SMALL if nm != "final_norm"}

    def start_scatter(tag, grads_d, names_):
        groups = width_groups(names_)
        mine = [jnp.concatenate([lax.dynamic_slice_in_dim(grads_d[nm], me * rows_of(nm), rows_of(nm), axis=0)
                                 for nm in g], axis=0) for g in groups]
        handle, nin = scatter_layer_start(tag + "_start", [[grads_d[nm] for nm in g] for g in groups])
        return dict(tag=tag, handle=handle, nin=nin, groups=groups, mine=mine)

    def finish_scatter(st, after):
        lands = scatter_layer_wait(st["tag"] + "_wait", st["handle"], st["nin"], len(st["groups"]), after)
        out = {}
        for land, mine, g in zip(lands, st["mine"], st["groups"]):
            land = lax.dynamic_update_slice(land, mine[None], (me, 0, 0))
            tot = sum_slots(f"sum_weight_grads_{land.shape[1]}x{land.shape[2]}", land)
            for nm, (off_, r) in zip(g, pieces_for(g)):
                out[nm] = tot[off_:off_ + r]
        return out

    scatters = []
    small_started = [None] * DEPTH
    small_len = sum(math.prod(weights[k].shape[1:]) for k in PER_LAYER_SMALL) + N_META * D
    small_rows = -(-small_len // (8 * D)) * 8
    sent = jnp.zeros((), F32)
    for l in reversed(range(DEPTH)):
        s1, s2, s3, mp, g1n, g2n = saved[l]
        w = full[l]
        dh, dg2, f2g, f2u, f2d = ffn_backward("ffn2", dh, s3, g2n + sent, w["f2_wgT"], w["f2_wuT"], w["f2_wd"], tm)
        st = start_scatter(f"scatter_l{l}_f2", dict(f2_wgT=f2g, f2_wuT=f2u, f2_wd=f2d), PART_F2)
        scatters.append((l, st))
        dh, mg = mixer_backward("mix", dh, s2, dict(mp, d=mp["d"] + st["handle"][3][0, 0]), tm)
        st = start_scatter(f"scatter_l{l}_mix", mg, PART_MIX)
        scatters.append((l, st))
        dh, dg1, f1g, f1u, f1d = ffn_backward("ffn1", dh, s1, g1n + st["handle"][3][0, 0],
                                              w["f1_wgT"], w["f1_wuT"], w["f1_wd"], tm)
        st = start_scatter(f"scatter_l{l}_f1", dict(f1_wgT=f1g, f1_wuT=f1u, f1_wd=f1d), PART_F1)
        scatters.append((l, st))
        sent = st["handle"][3][0, 0]
        small["ffn1_norm"][l] = dg1[0]
        small["mix_norm"][l] = mg["g"][0]
        small["ffn2_norm"][l] = dg2[0]
        small["ssm_d"][l] = mg["d"][0]
        small["attn_sink"][l] = mg["sink"][0, :N_HEADS]
        per_dir = {k: [] for k in ("ssm_lam_re", "ssm_lam_im", "ssm_log_dt", "ssm_b_re", "ssm_b_im",
                                   "ssm_c_re", "ssm_c_im")}
        for dr in range(2):
            gbr, gbi, gcr, gci, s_re, s_im = mg["ssm"][dr]
            a_re, a_im = ssm_p[l][dr]["a_re"], ssm_p[l][dr]["a_im"]
            s_re = s_re.reshape(SGRP, SP)
            s_im = s_im.reshape(SGRP, SP)
            den = a_re * a_re + a_im * a_im
            ga_re = (s_re * a_re - s_im * a_im) / den
            ga_im = (s_re * a_im + s_im * a_re) / den
            glr, gli, gld, gbre, gbim = ssm_vjp[l][dr]((ga_re, ga_im, _unpack_diag(gbr).transpose(0, 2, 1),
                                                        _unpack_diag(gbi).transpose(0, 2, 1)))
            per_dir["ssm_lam_re"].append(glr)
            per_dir["ssm_lam_im"].append(gli)
            per_dir["ssm_log_dt"].append(gld)
            per_dir["ssm_b_re"].append(gbre)
            per_dir["ssm_b_im"].append(gbim)
            per_dir["ssm_c_re"].append(_unpack_diag(gcr))
            per_dir["ssm_c_im"].append(_unpack_diag(gci))
        for k, vlist in per_dir.items():
            small[k][l] = jnp.stack(vlist)
        vec = [small[k][l].reshape(-1) for k in PER_LAYER_SMALL]
        if l == DEPTH - 1:
            vec.append(dgf[0])
        if l == 0:
            vec.append(dh[PAD:BLK].reshape(-1))
        used = sum(v.shape[0] for v in vec)
        flat = jnp.concatenate(vec + [jnp.zeros((small_rows * D - used,), F32)]).reshape(small_rows, D)
        small_started[l] = (exchange_start(
            f"small_grads_l{l}_start", [flat, jnp.zeros((NDEV, small_rows, D), F32)], 1,
            lambda refs, me_i, p_i: [(refs[0], refs[1].at[me_i], 0)]), flat)
        sent = sent + small_started[l][0][3][0, 0]

    grad_x = dh[BLK:][None]

    grads = {k: [None] * DEPTH for k in PER_LAYER_SMALL}
    for l in reversed(range(DEPTH)):
        (s_sems, r_sems, arrays, _), flat = small_started[l]
        land = exchange_wait(f"small_grads_l{l}_wait", s_sems, r_sems, arrays, 1, lambda refs, gi: refs[0], dh)[1]
        land = lax.dynamic_update_slice(land, flat[None], (me, 0, 0))
        tot = sum_slots("sum_small_grads", land).reshape(-1)
        o = 0
        for k in PER_LAYER_SMALL:
            shp = weights[k].shape[1:]
            sz = math.prod(shp)
            grads[k][l] = tot[o:o + sz].reshape(shp)
            o += sz
        if l == DEPTH - 1:
            final_norm_grad = tot[o:o + D]
        if l == 0:
            dmeta_full = tot[o:o + N_META * D].reshape(N_META, D)
    grads = {k: jnp.stack(vv) for k, vv in grads.items()}
    grads["final_norm"] = final_norm_grad
    grads["meta_tokens"] = lax.dynamic_slice_in_dim(dmeta_full, me * (D // NDEV), D // NDEV, axis=1)

    own = [dict() for _ in range(DEPTH)]
    for l, st in scatters:
        own[l].update(finish_scatter(st, dh))

    def stack(fn):
        return jnp.stack([fn(own[l]) for l in range(DEPTH)])

    grads["ffn1_w_gate"] = stack(lambda d: d["f1_wgT"].T)
    grads["ffn1_w_up"] = stack(lambda d: d["f1_wuT"].T)
    grads["ffn1_w_down"] = stack(lambda d: d["f1_wd"])
    grads["w_in"] = stack(lambda d: d["winT"].T)
    grads["ssm_w_glu"] = stack(lambda d: d["wglu"])
    grads["w_branch_ssm"] = stack(lambda d: d["wbsT"].T)
    grads["w_branch_attn"] = stack(lambda d: d["wba"])
    grads["w_out"] = stack(lambda d: d["wout"])
    grads["ffn2_w_gate"] = stack(lambda d: d["f2_wgT"].T)
    grads["ffn2_w_up"] = stack(lambda d: d["f2_wuT"].T)
    grads["ffn2_w_down"] = stack(lambda d: d["f2_wd"])

    deltas, new_m, new_v = {}, {}, {}
    for nm in names:
        deltas[nm], new_m[nm], new_v[nm] = adamw("adamw_" + nm, weights[nm], grads[nm], mom_m[nm], mom_v[nm])

    return (loss, grad_x, *[grads[n] for n in names], *[deltas[n] for n in names],
            *[new_m[n] for n in names], *[new_v[n] for n in names])
```

```python
import functools
import math

import jax
import jax.numpy as jnp
from jax import lax
from jax.experimental import pallas as pl
from jax.experimental.pallas import tpu as pltpu

F32 = jnp.float32
BF16 = jnp.bfloat16

D = 1024
DFF = 2816
N_META = 16
N_HEADS = 16
N_KV = 4
HD = 64
QG = 4
WIN = 128
BLK = 128
PAD = BLK - N_META
SW = 512
SGRP = 32
SCH = 16
SP = 64
NST = SGRP * SP
EPS = 1e-6
NEG = -1e30
SCALE = HD ** -0.5
NDEV = 8
DEPTH = 4
MESH_AXES = ("x", "y", "c")
MESH = pl.DeviceIdType.MESH

ADAM_LR = 0.001
ADAM_B1 = 0.9
ADAM_B2 = 0.999
ADAM_EPS = 1e-08
ADAM_WD = 0.01
ADAM_STEP = 10

VMEM_LIMIT = 56 * 1024 * 1024


def _params(*sem):
    return pltpu.CompilerParams(dimension_semantics=sem, vmem_limit_bytes=VMEM_LIMIT)


def _nn(a, b):
    return lax.dot_general(a, b, (((1,), (0,)), ((), ())), preferred_element_type=F32)


def _nt(a, b):
    return lax.dot_general(a, b, (((1,), (1,)), ((), ())), preferred_element_type=F32)


def _tn(a, b):
    return lax.dot_general(a, b, (((0,), (0,)), ((), ())), preferred_element_type=F32)


def _sig(x):
    return 0.5 * jnp.tanh(0.5 * x) + 0.5


def _rms_fwd(h, g):
    r = lax.rsqrt(jnp.mean(h * h, axis=-1, keepdims=True) + EPS)
    hh = h * r
    return hh, r, hh * g


def _rms_bwd(hh, r, g, dn):
    dhh = dn * g
    dx = r * (dhh - hh * jnp.mean(dhh * hh, axis=-1, keepdims=True))
    return dx, jnp.sum(dn * hh, axis=0, keepdims=True)


def _row_ok(i, tm):
    rows = i * tm + lax.broadcasted_iota(jnp.int32, (tm, 1), 0)
    return rows >= PAD


def _const_spec(shape, single=False):
    nd = len(shape)
    if single:
        return pl.BlockSpec(shape, lambda *_: (0,) * nd, pipeline_mode=pl.Buffered(1))
    return pl.BlockSpec(shape, lambda *_: (0,) * nd)


def rowcall(name, body, rows, consts, outs, accs=(), *, tm):
    nrows = rows[0].shape[0]
    nt = nrows // tm
    assert nt * tm == nrows, (name, nrows, tm)
    nr, nc, no, na = len(rows), len(consts), len(outs), len(accs)
    in_specs = [pl.BlockSpec((tm, r.shape[1]), lambda i: (i, 0)) for r in rows]
    in_specs += [_const_spec(c.shape, single=True) for c in consts]
    out_shape = [jax.ShapeDtypeStruct((nrows, w), dt) for (w, dt) in outs]
    out_specs = [pl.BlockSpec((tm, w), lambda i: (i, 0)) for (w, dt) in outs]
    out_shape += [jax.ShapeDtypeStruct(s, F32) for s in accs]
    out_specs += [_const_spec(s) for s in accs]

    def kern(*refs):
        i = pl.program_id(0)
        row_vals = [r[...] for r in refs[:nr]]
        res = body(i, *row_vals, *refs[nr:nr + nc])
        out_refs = refs[nr + nc:nr + nc + no]
        acc_refs = refs[nr + nc + no:]
        for r, v in zip(out_refs, res[:no]):
            r[...] = v.astype(r.dtype)
        if na:
            @pl.when(i == 0)
            def _():
                for r in acc_refs:
                    r[...] = jnp.zeros_like(r)
            for r, v in zip(acc_refs, res[no:]):
                r[...] += v

    res = pl.pallas_call(
        kern, name=name, grid=(nt,), in_specs=in_specs, out_specs=out_specs, out_shape=out_shape,
        compiler_params=_params("arbitrary"),
    )(*rows, *consts)
    return res


def tn_matmul(name, lhs, rhs, scale=1.0):
    M, K = lhs.shape
    N = rhs.shape[1]
    assert lhs.dtype == BF16 and rhs.dtype == BF16
    nm = 6
    tmw = M // nm
    assert tmw * nm == M and tmw % 16 == 0
    tk = 1408 if (K % 1408 == 0) else K
    nk = K // tk

    def kern(a_ref, b_ref, o_ref, acc):
        m = pl.program_id(1)
        part = _tn(a_ref[...], b_ref[...])

        @pl.when(m == 0)
        def _():
            acc[...] = part

        @pl.when((m > 0) & (m < nm - 1))
        def _():
            acc[...] += part

        @pl.when(m == nm - 1)
        def _():
            o_ref[...] = ((acc[...] + part) * scale).astype(o_ref.dtype)

    return pl.pallas_call(
        kern, name=name, grid=(nk, nm),
        in_specs=[pl.BlockSpec((tmw, tk), lambda k, m: (m, k)), pl.BlockSpec((tmw, N), lambda k, m: (m, 0))],
        out_specs=pl.BlockSpec((tk, N), lambda k, m: (k, 0)),
        out_shape=jax.ShapeDtypeStruct((K, N), BF16),
        scratch_shapes=[pltpu.VMEM((tk, N), F32)],
        compiler_params=_params("arbitrary", "arbitrary"),
    )(lhs, rhs)


def _mesh_pos():
    x, y, c = lax.axis_index("x"), lax.axis_index("y"), lax.axis_index("c")
    return x, y, c


def all_gather_pieces(name, groups):
    ng = len(groups)
    packed = [g[0] for g in groups]
    pieces = [g[1] for g in groups]
    out_shape, out_map = [], []
    for gi, (p, pcs) in enumerate(groups):
        idx = []
        for (off, r) in pcs:
            idx.append(len(out_shape))
            out_shape.append(jax.ShapeDtypeStruct((NDEV * r, p.shape[1]), p.dtype))
        out_map.append(idx)
    nout = len(out_shape)

    def body(*refs):
        p_refs = refs[:ng]
        o_refs = refs[ng:ng + nout]
        send_sems, recv_sems, local_sems = refs[ng + nout:]
        x, y, c = _mesh_pos()
        me = (x, y, c)
        sibling = (x, y, 1 - c)
        chips = [(1 - x, y), (x, 1 - y), (1 - x, 1 - y)]

        def blk(px, py, pc):
            return 4 * px + 2 * py + pc

        def copies(gi, k, origin, to, from_out):
            cps = []
            for (off, r), oi in zip(pieces[gi], out_map[gi]):
                dst = o_refs[oi].at[pl.ds(origin * r, r), :]
                src = dst if from_out else p_refs[gi].at[pl.ds(off, r), :]
                cps.append(pltpu.make_async_remote_copy(
                    src_ref=src, dst_ref=dst, send_sem=send_sems.at[gi, k], recv_sem=recv_sems.at[gi, k],
                    device_id=to, device_id_type=MESH))
            return cps

        def whole(gi, k):
            return pltpu.make_async_remote_copy(
                src_ref=p_refs[gi], dst_ref=p_refs[gi], send_sem=send_sems.at[gi, k],
                recv_sem=recv_sems.at[gi, k], device_id=me, device_id_type=MESH)

        mine = []
        for gi in range(ng):
            for (off, r), oi in zip(pieces[gi], out_map[gi]):
                mine.append(pltpu.make_async_copy(
                    p_refs[gi].at[pl.ds(off, r), :], o_refs[oi].at[pl.ds(blk(*me) * r, r), :],
                    local_sems.at[gi]))
        for cp in mine:
            cp.start()
        for gi in range(ng):
            for cp in copies(gi, 0, blk(*me), sibling, False):
                cp.start()
            for j, chip in enumerate(chips):
                for cp in copies(gi, 1 + j, blk(*me), (*chip, c), False):
                    cp.start()
        for j, chip in enumerate(chips):
            for gi in range(ng):
                whole(gi, 1 + j).wait_recv()
                for cp in copies(gi, 4 + j, blk(*chip, c), sibling, True):
                    cp.start()
        for gi in range(ng):
            whole(gi, 0).wait_recv()
            for j in range(3):
                whole(gi, 4 + j).wait_recv()
        for gi in range(ng):
            for k in range(7):
                whole(gi, k).wait_send()
            pltpu.make_async_copy(p_refs[gi], p_refs[gi], local_sems.at[gi]).wait()

    any_spec = pl.BlockSpec(memory_space=pl.ANY)
    outs = pl.pallas_call(
        body, name=name, out_shape=out_shape,
        in_specs=[any_spec] * ng, out_specs=[any_spec] * nout,
        scratch_shapes=[pltpu.SemaphoreType.DMA((ng, 7)), pltpu.SemaphoreType.DMA((ng, 7)),
                        pltpu.SemaphoreType.DMA((ng,))],
    )(*packed)
    return [[outs[oi] for oi in idx] for idx in out_map]


HBM_SPEC = pl.BlockSpec(memory_space=pltpu.HBM)
SEM_SPEC = pl.BlockSpec(memory_space=pltpu.SEMAPHORE)
DATAFLOW = pltpu.SideEffectType.DATAFLOW_SIDE_EFFECTING


def _peers(x, y, c):
    return [(x, y, 1 - c), (1 - x, y, c), (x, 1 - y, c), (1 - x, 1 - y, c),
            (1 - x, y, 1 - c), (x, 1 - y, 1 - c), (1 - x, 1 - y, 1 - c)]


def exchange_start(name, arrays, ng, plan):
    n = len(arrays)
    ns = ng * 7

    def body(*refs):
        in_refs = refs[:n]
        send_sems, recv_sems = refs[n:n + ns], refs[n + ns:n + 2 * ns]
        token = refs[-1]
        x, y, c = _mesh_pos()
        me_i = 4 * x + 2 * y + c
        for k, peer in enumerate(_peers(x, y, c)):
            p_i = 4 * peer[0] + 2 * peer[1] + peer[2]
            for src, dst, gi in plan(in_refs, me_i, p_i):
                pltpu.make_async_remote_copy(
                    src_ref=src, dst_ref=dst, send_sem=send_sems[gi * 7 + k], recv_sem=recv_sems[gi * 7 + k],
                    device_id=peer, device_id_type=MESH).start()
        token[...] = jnp.zeros_like(token)

    res = pl.pallas_call(
        body, name=name,
        out_shape=(*[pltpu.SemaphoreType.DMA(())] * (2 * ns),
                   *[pltpu.HBM(a.shape, a.dtype) for a in arrays], jax.ShapeDtypeStruct((8, 128), F32)),
        in_specs=[HBM_SPEC] * n,
        out_specs=(*[SEM_SPEC] * (2 * ns), *[HBM_SPEC] * n, pl.BlockSpec(memory_space=pltpu.VMEM)),
        input_output_aliases={i: 2 * ns + i for i in range(n)},
        compiler_params=pltpu.CompilerParams(has_side_effects=DATAFLOW),
    )(*[pltpu.with_memory_space_constraint(a, pltpu.HBM) for a in arrays])
    return list(res[:ns]), list(res[ns:2 * ns]), list(res[2 * ns:2 * ns + n]), res[-1]


def exchange_wait(name, send_sems, recv_sems, arrays, ng, sized, after):
    n = len(arrays)
    ns = ng * 7

    def body(*refs):
        in_refs = refs[:n]
        s_sems, r_sems = refs[n:n + ns], refs[n + ns:n + 2 * ns]
        x, y, c = _mesh_pos()
        for gi in range(ng):
            view = sized(in_refs, gi)
            for k in range(7):
                w = pltpu.make_async_remote_copy(
                    src_ref=view, dst_ref=view, send_sem=s_sems[gi * 7 + k], recv_sem=r_sems[gi * 7 + k],
                    device_id=(x, y, c), device_id_type=MESH)
                w.wait_send()
                w.wait_recv()

    res = pl.pallas_call(
        body, name=name, out_shape=tuple(pltpu.HBM(a.shape, a.dtype) for a in arrays),
        in_specs=[HBM_SPEC] * n + [SEM_SPEC] * (2 * ns) + [pl.BlockSpec(memory_space=pl.ANY)],
        out_specs=tuple([HBM_SPEC] * n), input_output_aliases={i: i for i in range(n)},
        compiler_params=pltpu.CompilerParams(has_side_effects=DATAFLOW),
    )(*arrays, *send_sems, *recv_sems, after)
    return list(res)


def gather_layer_start(name, packed, pieces):
    ng = len(packed)
    dests = [lax.empty((NDEV * r, p.shape[1]), p.dtype) for p, pcs in zip(packed, pieces) for (_, r) in pcs]

    def plan(refs, me_i, p_i):
        out, di = [], ng
        for gi in range(ng):
            for (off, r) in pieces[gi]:
                out.append((refs[gi].at[pl.ds(off, r), :], refs[di].at[pl.ds(me_i * r, r), :], gi))
                di += 1
        return out

    return exchange_start(name, list(packed) + dests, ng, plan)


def gather_layer_wait(name, handle, ng, after):
    send_sems, recv_sems, arrays, _ = handle
    out = exchange_wait(name, send_sems, recv_sems, arrays, ng, lambda refs, gi: refs[gi], after)
    return out[ng:]


def scatter_layer_start(name, groups):
    ng = len(groups)
    flat = [a for arrs in groups for a in arrs]
    offs, lands = [], []
    for arrs in groups:
        o, off = [], 0
        for a in arrs:
            r = a.shape[0] // NDEV
            o.append((off, r))
            off += r
        offs.append(o)
        lands.append(jnp.zeros((NDEV, off, arrs[0].shape[1]), arrs[0].dtype))
    nin = len(flat)

    def plan(refs, me_i, p_i):
        out, ai = [], 0
        for gi in range(ng):
            for (off, r) in offs[gi]:
                out.append((refs[ai].at[pl.ds(p_i * r, r), :], refs[nin + gi].at[me_i, pl.ds(off, r), :], gi))
                ai += 1
        return out

    return exchange_start(name, flat + lands, ng, plan), nin


def scatter_layer_wait(name, handle, nin, ng, after):
    send_sems, recv_sems, arrays, _ = handle
    out = exchange_wait(name, send_sems, recv_sems, arrays, ng, lambda refs, gi: refs[nin + gi].at[0], after)
    return out[nin:]


def _pick_tile(n, cap):
    best = None
    for t in range(8, min(n, cap) + 1, 8):
        if n % t == 0:
            best = t
    return best if best is not None else n


def sum_slots(name, land):
    _, R, W = land.shape
    tr = _pick_tile(R, 512)

    def kern(l_ref, o_ref):
        acc = l_ref[0].astype(F32)
        for s in range(1, NDEV):
            acc = acc + l_ref[s].astype(F32)
        o_ref[...] = acc

    return pl.pallas_call(
        kern, name=name, grid=(R // tr,),
        in_specs=[pl.BlockSpec((NDEV, tr, W), lambda i: (0, i, 0))],
        out_specs=pl.BlockSpec((tr, W), lambda i: (i, 0)),
        out_shape=jax.ShapeDtypeStruct((R, W), F32),
        compiler_params=_params("arbitrary"),
    )(land)


def adamw(name, w, g, m, v):
    shp = w.shape
    C = shp[-1]
    R = max(1, math.prod(shp[:-1]))
    tr = _pick_tile(R, 1024)
    w2, g2, m2, v2 = (a.reshape(R, C) for a in (w, g, m, v))

    def kern(w_ref, g_ref, m_ref, v_ref, d_ref, mo_ref, vo_ref):
        gg = g_ref[...]
        mn = ADAM_B1 * m_ref[...] + (1.0 - ADAM_B1) * gg
        vn = ADAM_B2 * v_ref[...] + (1.0 - ADAM_B2) * jnp.square(gg)
        m_hat = mn / (1.0 - ADAM_B1 ** ADAM_STEP)
        v_hat = vn / (1.0 - ADAM_B2 ** ADAM_STEP)
        d_ref[...] = -ADAM_LR * (m_hat / (jnp.sqrt(v_hat) + ADAM_EPS) + ADAM_WD * w_ref[...])
        mo_ref[...] = mn
        vo_ref[...] = vn

    spec = pl.BlockSpec((tr, C), lambda i: (i, 0))
    d, mo, vo = pl.pallas_call(
        kern, name=name, grid=(R // tr,), in_specs=[spec] * 4, out_specs=[spec] * 3,
        out_shape=[jax.ShapeDtypeStruct((R, C), F32)] * 3, compiler_params=_params("arbitrary"),
    )(w2, g2, m2, v2)
    return d.reshape(shp), mo.reshape(shp), vo.reshape(shp)


def build_h0(x2, blk0):
    L0 = x2.shape[0]
    nb = L0 // BLK + 1

    def kern(x_ref, b_ref, o_ref):
        i = pl.program_id(0)

        @pl.when(i == 0)
        def _():
            o_ref[...] = b_ref[...]

        @pl.when(i > 0)
        def _():
            o_ref[...] = x_ref[...]

    return pl.pallas_call(
        kern, name="build_h0", grid=(nb,),
        in_specs=[pl.BlockSpec((BLK, D), lambda i: (jnp.maximum(i - 1, 0), 0)), _const_spec((BLK, D))],
        out_specs=pl.BlockSpec((BLK, D), lambda i: (i, 0)),
        out_shape=jax.ShapeDtypeStruct((L0 + BLK, D), F32), compiler_params=_params("arbitrary"),
    )(x2, blk0)


def final_loss(h, tgt, gf):
    LP = h.shape[0]
    nb = LP // BLK

    def kern(h_ref, t_ref, g_ref, dh_ref, loss_ref, dg_ref):
        i = pl.program_id(0)

        @pl.when(i == 0)
        def _():
            loss_ref[...] = jnp.zeros_like(loss_ref)
            dg_ref[...] = jnp.zeros_like(dg_ref)

        g = g_ref[...]
        hh, r, yv = _rms_fwd(h_ref[...], g)
        valid = (i > 0).astype(F32)
        err = (yv - t_ref[...]) * valid
        loss_ref[...] += 0.5 * jnp.sum(jnp.sum(err * err, axis=1, keepdims=True), axis=0, keepdims=True) / D
        dy = err / D
        dx, dg = _rms_bwd(hh, r, g, dy)
        dh_ref[...] = dx
        dg_ref[...] += dg

    return pl.pallas_call(
        kern, name="final_loss", grid=(nb,),
        in_specs=[pl.BlockSpec((BLK, D), lambda i: (i, 0)),
                  pl.BlockSpec((BLK, D), lambda i: (jnp.maximum(i - 1, 0), 0)), _const_spec((1, D))],
        out_specs=[pl.BlockSpec((BLK, D), lambda i: (i, 0)), _const_spec((8, 128)), _const_spec((1, D))],
        out_shape=[jax.ShapeDtypeStruct((LP, D), F32), jax.ShapeDtypeStruct((8, 128), F32),
                   jax.ShapeDtypeStruct((1, D), F32)],
        compiler_params=_params("arbitrary"),
    )(h, tgt, gf)


def _tall_tile(nrows, tm):
    t = nrows // 24
    return t if (t * 24 == nrows and t % 16 == 0 and t > tm) else tm


def ffn_forward(tag, h, g, wgT, wuT, wd, tm):
    def f1(i, hv, g_ref, wg_ref, wu_ref):
        _, _, n = _rms_fwd(hv, g_ref[...])
        nb = n.astype(BF16)
        G = _nt(nb, wg_ref[...])
        U = _nt(nb, wu_ref[...])
        A = G * _sig(G) * U
        return nb, G, U, A

    n, G, U, A = rowcall(tag + "_up", f1, [h], [g, wgT, wuT],
                         [(D, BF16), (DFF, BF16), (DFF, BF16), (DFF, BF16)], tm=tm)

    def f2(i, av, hv, wd_ref):
        return (hv + 0.5 * _nn(av, wd_ref[...]),)

    (h2,) = rowcall(tag + "_down", f2, [A, h], [wd], [(D, F32)], tm=_tall_tile(h.shape[0], tm))
    return h2, (h, n, G, U, A)


def ffn_backward(tag, dh, saved, g, wgT, wuT, wd, tm, emit=None):
    h, n, G, U, A = saved

    def b1(i, dhv, wd_ref):
        dyb = (0.5 * dhv).astype(BF16)
        return _nt(dyb, wd_ref[...]), dyb

    dA, dyb = rowcall(tag + "_bwd_act", b1, [dh], [wd], [(DFF, BF16), (D, BF16)], tm=_tall_tile(h.shape[0], tm))
    dwd = tn_matmul(tag + "_dwd", A, dyb)
    if emit is not None:
        emit("wd", dwd)

    def b2(i, dAv, Gv, Uv, hv, dhv, g_ref, wg_ref, wu_ref):
        dAf = dAv.astype(F32)
        Gf = Gv.astype(F32)
        sg = _sig(Gf)
        dG = (dAf * Uv.astype(F32) * (sg * (1.0 + Gf * (1.0 - sg)))).astype(BF16)
        dU = (dAf * (Gf * sg)).astype(BF16)
        dn = _nn(dG, wg_ref[...]) + _nn(dU, wu_ref[...])
        gv = g_ref[...]
        hh, r, _ = _rms_fwd(hv, gv)
        dx, dg = _rms_bwd(hh, r, gv, dn)
        dx = jnp.where(_row_ok(i, tm), dx, 0.0)
        return dhv + dx, dG, dU, dg

    dh2, dG, dU, dg = rowcall(tag + "_bwd_in", b2, [dA, G, U, h, dh], [g, wgT, wuT],
                              [(D, F32), (DFF, BF16), (DFF, BF16)], [(1, D)], tm=tm)
    dwgT = tn_matmul(tag + "_dwg", dG, n)
    if emit is not None:
        emit("wgT", dwgT)
    dwuT = tn_matmul(tag + "_dwu", dU, n)
    if emit is not None:
        emit("wuT", dwuT)
    return dh2, dg, dwgT, dwuT, dwd


def _alibi_slope(head):
    return float(2.0 ** (-8.0 * (head + 1) / N_HEADS))


def _att_bias(n, nb):
    qi = lax.broadcasted_iota(jnp.int32, (BLK, 4 * BLK), 0)
    cj = lax.broadcasted_iota(jnp.int32, (BLK, 4 * BLK), 1)
    jb = cj - BLK
    dist = jnp.abs(qi + BLK - jb)
    kpos = (n - 1) * BLK + jb
    band_ok = (dist <= WIN) & (kpos >= BLK) & (kpos < nb * BLK)
    is_meta = cj < BLK
    ok = (is_meta & (cj >= PAD)) | (jnp.logical_not(is_meta) & band_ok)
    distf = jnp.where(is_meta, 0, dist).astype(F32)
    maskadd = jnp.where(ok, 0.0, NEG).astype(F32)
    distf4 = jnp.concatenate([distf] * QG, axis=0)
    mask4 = jnp.concatenate([maskadd] * QG, axis=0)
    return distf4, mask4


def _group_col(vals):
    rg = lax.broadcasted_iota(jnp.int32, (QG * BLK, 1), 0) // BLK
    col = jnp.full((QG * BLK, 1), vals[QG - 1], F32)
    for gq in range(QG - 2, -1, -1):
        col = jnp.where(rg == gq, vals[gq], col)
    return col


def _stack_heads(ref_or_val, kh):
    return jnp.concatenate(
        [ref_or_val[:, (kh * QG + gq) * HD:(kh * QG + gq + 1) * HD] for gq in range(QG)], axis=0)


def _stack_keys(km, kp, kc, kn, kh):
    sl = slice(kh * HD, (kh + 1) * HD)
    return jnp.concatenate([km[:, sl], kp[:, sl], kc[:, sl], kn[:, sl]], axis=0)


LOG2E = 1.4426950408889634
LN2 = 0.6931471805599453
QSCALE = SCALE * LOG2E


def _att_update_bias(bias_ref, n, nb):
    @pl.when((n <= 2) | (n == nb - 1))
    def _():
        distf4, mask4 = _att_bias(n, nb)
        for kh in range(N_KV):
            slope_col = _group_col([_alibi_slope(kh * QG + gq) * LOG2E for gq in range(QG)])
            bias_ref[kh] = mask4 - slope_col * distf4


def _att_exp(qs, kb, bias_ref, kh, sink_ref):
    sink_col = _group_col([sink_ref[kh * QG + gq] for gq in range(QG)]) * LOG2E
    s = _nt(qs, kb) + bias_ref[kh]
    m = jnp.maximum(jnp.max(s, axis=1, keepdims=True), sink_col)
    e = jnp.exp2(s - m)
    es = jnp.exp2(sink_col - m)
    inv = 1.0 / (jnp.sum(e, axis=1, keepdims=True) + es)
    return e, es, inv


def attention_forward(tag, q, k, v, sink):
    LP = q.shape[0]
    nb = LP // BLK

    def kern(sink_ref, q_ref, km_ref, kp_ref, kc_ref, kn_ref, vm_ref, vp_ref, vc_ref, vn_ref, o_ref, bias_ref):
        n = pl.program_id(0)
        _att_update_bias(bias_ref, n, nb)
        qv = q_ref[...]
        km, kp, kc, kn = km_ref[...], kp_ref[...], kc_ref[...], kn_ref[...]
        vm, vp, vc, vn = vm_ref[...], vp_ref[...], vc_ref[...], vn_ref[...]
        for kh in range(N_KV):
            qs = _stack_heads(qv, kh)
            kb = _stack_keys(km, kp, kc, kn, kh)
            vb = _stack_keys(vm, vp, vc, vn, kh)
            e, _, inv = _att_exp(qs, kb, bias_ref, kh, sink_ref)
            o = _nn(e.astype(BF16), vb) * inv
            for gq in range(QG):
                hcol = (kh * QG + gq) * HD
                o_ref[:, hcol:hcol + HD] = o[gq * BLK:(gq + 1) * BLK].astype(o_ref.dtype)

    def kvspec(dn):
        return pl.BlockSpec((BLK, N_KV * HD), lambda n: (jnp.clip(n + dn, 0, nb - 1), 0))

    meta_spec = pl.BlockSpec((BLK, N_KV * HD), lambda n: (0, 0))
    return pl.pallas_call(
        kern, name=tag + "_att_fwd", grid=(nb,),
        in_specs=[pl.BlockSpec(memory_space=pltpu.SMEM), pl.BlockSpec((BLK, D), lambda n: (n, 0)),
                  meta_spec, kvspec(-1), kvspec(0), kvspec(1), meta_spec, kvspec(-1), kvspec(0), kvspec(1)],
        out_specs=pl.BlockSpec((BLK, D), lambda n: (n, 0)),
        out_shape=jax.ShapeDtypeStruct((LP, D), BF16),
        scratch_shapes=[pltpu.VMEM((N_KV, QG * BLK, 4 * BLK), F32)], compiler_params=_params("arbitrary"),
    )(sink, q, k, k, k, k, v, v, v, v)


def attention_backward(tag, q, k, v, do, sink):
    LP = q.shape[0]
    nb = LP // BLK
    KW = N_KV * HD

    def kern(sink_ref, q_ref, do_ref, km_ref, kp_ref, kc_ref, kn_ref, vm_ref, vp_ref, vc_ref, vn_ref,
             dq_ref, dk_ref, dv_ref, dkm_ref, dvm_ref, dsink_ref, bias_ref, rk, rv, fk, fv):
        n = pl.program_id(0)

        @pl.when(n == 0)
        def _():
            dkm_ref[...] = jnp.zeros_like(dkm_ref)
            dvm_ref[...] = jnp.zeros_like(dvm_ref)
            dsink_ref[...] = jnp.zeros_like(dsink_ref)
            rk[...] = jnp.zeros_like(rk)
            rv[...] = jnp.zeros_like(rv)

        _att_update_bias(bias_ref, n, nb)

        @pl.when(n < nb)
        def _():
            qv, dov = q_ref[...], do_ref[...]
            km, kp, kc, kn = km_ref[...], kp_ref[...], kc_ref[...], kn_ref[...]
            vm, vp, vc, vn = vm_ref[...], vp_ref[...], vc_ref[...], vn_ref[...]
            lane = lax.broadcasted_iota(jnp.int32, (8, 128), 1)
            dsink = jnp.zeros((8, 128), F32)
            for kh in range(N_KV):
                qs = _stack_heads(qv, kh)
                dos = _stack_heads(dov, kh)
                kb = _stack_keys(km, kp, kc, kn, kh)
                vb = _stack_keys(vm, vp, vc, vn, kh)
                dp = _nt(dos, vb)
                e, es, inv = _att_exp(qs, kb, bias_ref, kh, sink_ref)
                delta = inv * jnp.sum(e * dp, axis=1, keepdims=True)
                dsu = (e * (dp - delta)).astype(BF16)
                dqs = _nn(dsu, kb) * (inv * SCALE)
                dkt = _tn((qs.astype(F32) * (inv * LN2)).astype(BF16), dsu)
                dvt = _tn((dos.astype(F32) * inv).astype(BF16), e.astype(BF16))
                dsk = -(es * inv * delta)
                for gq in range(QG):
                    hcol = (kh * QG + gq) * HD
                    dq_ref[:, hcol:hcol + HD] = dqs[gq * BLK:(gq + 1) * BLK].astype(dq_ref.dtype)
                    tot = jnp.sum(dsk[gq * BLK:(gq + 1) * BLK], axis=0, keepdims=True)
                    dsink = dsink + jnp.where(lane == kh * QG + gq, tot, 0.0)
                hs = slice(kh * HD, (kh + 1) * HD)
                dkm_ref[hs, :] += dkt[:, 0:BLK]
                dvm_ref[hs, :] += dvt[:, 0:BLK]
                for ring, fin, part in ((rk, fk, dkt), (rv, fv, dvt)):
                    fin[hs, :] = ring[0, hs, :] + part[:, BLK:2 * BLK]
                    ring[0, hs, :] = ring[1, hs, :] + part[:, 2 * BLK:3 * BLK]
                    ring[1, hs, :] = part[:, 3 * BLK:4 * BLK]
            dsink_ref[...] += dsink
            dk_ref[...] = fk[...].T.astype(dk_ref.dtype)
            dv_ref[...] = fv[...].T.astype(dv_ref.dtype)

        @pl.when(n == nb)
        def _():
            dk_ref[...] = rk[0].T.astype(dk_ref.dtype)
            dv_ref[...] = rv[0].T.astype(dv_ref.dtype)

    def kvspec(dn):
        return pl.BlockSpec((BLK, KW), lambda n: (jnp.clip(jnp.minimum(n, nb - 1) + dn, 0, nb - 1), 0))

    meta_spec = pl.BlockSpec((BLK, KW), lambda n: (0, 0))
    rowspec = pl.BlockSpec((BLK, D), lambda n: (jnp.minimum(n, nb - 1), 0))
    emit_spec = pl.BlockSpec((BLK, KW), lambda n: (jnp.clip(n - 1, 1, nb - 1), 0))
    dq, dk, dv, dkm, dvm, dsink = pl.pallas_call(
        kern, name=tag + "_att_bwd", grid=(nb + 1,),
        in_specs=[pl.BlockSpec(memory_space=pltpu.SMEM), rowspec, rowspec,
                  meta_spec, kvspec(-1), kvspec(0), kvspec(1), meta_spec, kvspec(-1), kvspec(0), kvspec(1)],
        out_specs=[rowspec, emit_spec, emit_spec, _const_spec((KW, BLK)), _const_spec((KW, BLK)),
                   _const_spec((8, 128))],
        out_shape=[jax.ShapeDtypeStruct((LP, D), BF16), jax.ShapeDtypeStruct((LP, KW), BF16),
                   jax.ShapeDtypeStruct((LP, KW), BF16), jax.ShapeDtypeStruct((KW, BLK), F32),
                   jax.ShapeDtypeStruct((KW, BLK), F32), jax.ShapeDtypeStruct((8, 128), F32)],
        scratch_shapes=[pltpu.VMEM((N_KV, QG * BLK, 4 * BLK), F32), pltpu.VMEM((2, KW, BLK), F32),
                        pltpu.VMEM((2, KW, BLK), F32), pltpu.VMEM((KW, BLK), F32), pltpu.VMEM((KW, BLK), F32)],
        compiler_params=_params("arbitrary"),
    )(sink, q, do, k, k, k, k, v, v, v, v)
    dk = lax.dynamic_update_slice(dk, dkm.T.astype(BF16), (0, 0))
    dv = lax.dynamic_update_slice(dv, dvm.T.astype(BF16), (0, 0))
    return dq, dk, dv, dsink


SCAN_LANES = 1024


def _scan_tile(xr, xi, cr, ci, a8, tab, seg, reverse):
    sub = lax.broadcasted_iota(jnp.int32, (8, SCAN_LANES), 0)
    for c0 in range(0, NST, SCAN_LANES):
        cs = pl.ds(c0, SCAN_LANES)
        ar = a8[0, :, cs]
        ai = a8[1, :, cs]

        def rows(j):
            jj = (seg - 1 - j) if reverse else j
            return pl.ds(jj * 8, 8)

        def step1(j, carry):
            vr, vi = carry
            rs = rows(j)
            nr = ar * vr - ai * vi + xr[rs, cs]
            ni = ar * vi + ai * vr + xi[rs, cs]
            xr[rs, cs] = nr
            xi[rs, cs] = ni
            return nr, ni

        zero = jnp.zeros((8, SCAN_LANES), F32)
        vr, vi = zero, zero
        for j in range(seg):
            vr, vi = step1(j, (vr, vi))
        for t, s in enumerate((1, 2, 4)):
            sh = (8 - s) if reverse else s
            sr = pltpu.roll(vr, sh, 0)
            si = pltpu.roll(vi, sh, 0)
            tr = tab[2 * t, :, cs]
            ti = tab[2 * t + 1, :, cs]
            vr, vi = vr + tr * sr - ti * si, vi + tr * si + ti * sr
        pr = tab[6, :, cs]
        pi = tab[7, :, cs]
        c_r = cr[:, cs]
        c_i = ci[:, cs]
        vr, vi = vr + pr * c_r - pi * c_i, vi + pr * c_i + pi * c_r
        edge = 7 if reverse else 0
        last = 0 if reverse else 7
        sh = 7 if reverse else 1
        in_r = jnp.where(sub == edge, c_r, pltpu.roll(vr, sh, 0))
        in_i = jnp.where(sub == edge, c_i, pltpu.roll(vi, sh, 0))
        cr[:, cs] = jnp.broadcast_to(vr[last:last + 1, :], (8, SCAN_LANES))
        ci[:, cs] = jnp.broadcast_to(vi[last:last + 1, :], (8, SCAN_LANES))

        def step2(j, carry):
            dr, di = carry
            rs = rows(j)
            ndr = ar * dr - ai * di
            ndi = ar * di + ai * dr
            xr[rs, cs] += ndr
            xi[rs, cs] += ndi
            return ndr, ndi

        dr, di = in_r, in_i
        for j in range(seg):
            dr, di = step2(j, (dr, di))


ST_T = 4 * SP * 2
CH_T = 128


def _load_segmented(ref, scr, seg):
    out = []
    for ct in range(4):
        scr[ct] = ref[:, ct * CH_T:(ct + 1) * CH_T]
        out.append(jnp.concatenate([scr[ct, pl.ds(j, 8, stride=seg), :] for j in range(seg)], axis=0))
    return out


def _store_segmented(ref, scr, vals, seg):
    for ct in range(4):
        for j in range(seg):
            scr[ct, pl.ds(j, 8, stride=seg), :] = vals[ct][8 * j:8 * j + 8]
        ref[:, ct * CH_T:(ct + 1) * CH_T] = scr[ct]


def ssm_dir_forward(tag, u, bpr, bpi, cpr, cpi, a8, tab, reverse, tm):
    LP = u.shape[0]
    nt = LP // tm
    seg = tm // 8

    def rix(i):
        return (nt - 1 - i) if reverse else i

    def kern(u_ref, bpr_ref, bpi_ref, cpr_ref, cpi_ref, a8_ref, tab_ref, xre_ref, xim_ref, y_ref,
             xr, xi, ys, cr, ci):
        i = pl.program_id(0)

        @pl.when(i == 0)
        def _():
            cr[...] = jnp.zeros_like(cr)
            ci[...] = jnp.zeros_like(ci)

        ub = _load_segmented(u_ref, ys, seg)
        for ct in range(4):
            uc = ub[ct].astype(BF16)
            xr[:, ct * ST_T:(ct + 1) * ST_T] = _nn(uc, bpr_ref[ct * CH_T:(ct + 1) * CH_T, :])
            xi[:, ct * ST_T:(ct + 1) * ST_T] = _nn(uc, bpi_ref[ct * CH_T:(ct + 1) * CH_T, :])
        _scan_tile(xr, xi, cr, ci, a8_ref, tab_ref, seg, reverse)
        xrb = xr[...].astype(BF16)
        xib = xi[...].astype(BF16)
        xre_ref[...] = xrb
        xim_ref[...] = xib
        yv = []
        for ct in range(4):
            ss = slice(ct * ST_T, (ct + 1) * ST_T)
            yv.append(_nn(xrb[:, ss], cpr_ref[ss, :]) - _nn(xib[:, ss], cpi_ref[ss, :]))
        _store_segmented(y_ref, ys, yv, seg)

    row = lambda w: pl.BlockSpec((tm, w), lambda i: (rix(i), 0))
    return pl.pallas_call(
        kern, name=tag, grid=(nt,),
        in_specs=[row(SW), _const_spec(bpr.shape), _const_spec(bpi.shape), _const_spec(cpr.shape),
                  _const_spec(cpi.shape), _const_spec(a8.shape), _const_spec(tab.shape)],
        out_specs=[row(NST), row(NST), row(SW)],
        out_shape=[jax.ShapeDtypeStruct((LP, NST), BF16), jax.ShapeDtypeStruct((LP, NST), BF16),
                   jax.ShapeDtypeStruct((LP, SW), F32)],
        scratch_shapes=[pltpu.VMEM((tm, NST), F32), pltpu.VMEM((tm, NST), F32), pltpu.VMEM((4, tm, CH_T), F32),
                        pltpu.VMEM((8, NST), F32), pltpu.VMEM((8, NST), F32)],
        compiler_params=_params("arbitrary"),
    )(u, bpr, bpi, cpr, cpi, a8, tab)


def ssm_dir_backward(tag, dy, xre, xim, u, bpr, bpi, cpr, cpi, a8_adj, tab_adj, reverse, tm):
    LP = u.shape[0]
    nt = LP // tm
    seg = tm // 8

    def rix(i):
        return (nt - 1 - i) if reverse else i

    def kern(dy_ref, xre_ref, xim_ref, u_ref, bpr_ref, bpi_ref, cpr_ref, cpi_ref, a8_ref, tab_ref,
             du_ref, gbr_ref, gbi_ref, gcr_ref, gci_ref, sr_ref, si_ref, lr, li, gr, gi, dus, cr, ci):
        i = pl.program_id(0)

        @pl.when(i == 0)
        def _():
            cr[...] = jnp.zeros_like(cr)
            ci[...] = jnp.zeros_like(ci)
            for r in (gbr_ref, gbi_ref, gcr_ref, gci_ref, sr_ref, si_ref):
                r[...] = jnp.zeros_like(r)

        dyb = [v.astype(BF16) for v in _load_segmented(dy_ref, dus, seg)]
        ub = [v.astype(BF16) for v in _load_segmented(u_ref, dus, seg)]
        for ct in range(4):
            ss = slice(ct * ST_T, (ct + 1) * ST_T)
            dc = dyb[ct]
            g_re = _nt(dc, cpr_ref[ss, :])
            g_im = -_nt(dc, cpi_ref[ss, :])
            lr[:, ss] = g_re
            li[:, ss] = g_im
            gr[:, ss] = g_re
            gi[:, ss] = g_im
        _scan_tile(lr, li, cr, ci, a8_ref, tab_ref, seg, reverse)
        lam_r = lr[...]
        lam_i = li[...]
        wr = lam_r - gr[...]
        wi = lam_i - gi[...]
        xr = xre_ref[...].astype(F32)
        xi = xim_ref[...].astype(F32)
        sr_ref[...] += jnp.sum(wr * xr + wi * xi, axis=0, keepdims=True)
        si_ref[...] += jnp.sum(wi * xr - wr * xi, axis=0, keepdims=True)
        lrb = lam_r.astype(BF16)
        lib = lam_i.astype(BF16)
        xrb = xre_ref[...]
        xib = xim_ref[...]
        duv = []
        for ct in range(4):
            ss = slice(ct * ST_T, (ct + 1) * ST_T)
            cs = slice(ct * CH_T, (ct + 1) * CH_T)
            duv.append(_nt(lrb[:, ss], bpr_ref[cs, :]) + _nt(lib[:, ss], bpi_ref[cs, :]))
            gbr_ref[cs, :] += _tn(ub[ct], lrb[:, ss])
            gbi_ref[cs, :] += _tn(ub[ct], lib[:, ss])
            gcr_ref[cs, :] += _tn(dyb[ct], xrb[:, ss])
            gci_ref[cs, :] -= _tn(dyb[ct], xib[:, ss])
        _store_segmented(du_ref, dus, duv, seg)

    row = lambda w: pl.BlockSpec((tm, w), lambda i: (rix(i), 0))
    acc = _const_spec((SW, ST_T))
    vec = _const_spec((1, NST))
    return pl.pallas_call(
        kern, name=tag, grid=(nt,),
        in_specs=[row(SW), row(NST), row(NST), row(SW), _const_spec(bpr.shape), _const_spec(bpi.shape),
                  _const_spec(cpr.shape), _const_spec(cpi.shape), _const_spec(a8_adj.shape),
                  _const_spec(tab_adj.shape)],
        out_specs=[row(SW), acc, acc, acc, acc, vec, vec],
        out_shape=[jax.ShapeDtypeStruct((LP, SW), F32)] + [jax.ShapeDtypeStruct((SW, ST_T), F32)] * 4
        + [jax.ShapeDtypeStruct((1, NST), F32)] * 2,
        scratch_shapes=[pltpu.VMEM((tm, NST), F32)] * 4 + [pltpu.VMEM((4, tm, CH_T), F32)]
        + [pltpu.VMEM((8, NST), F32)] * 2,
        compiler_params=_params("arbitrary"),
    )(dy, xre, xim, u, bpr, bpi, cpr, cpi, a8_adj, tab_adj)


def _ssm_disc(lam_re, lam_im, log_dt, b_re, b_im):
    dt = jnp.exp(log_dt)[:, None]
    mag = jnp.exp(lam_re * dt)
    a_re = mag * jnp.cos(lam_im * dt)
    a_im = mag * jnp.sin(lam_im * dt)
    den = lam_re * lam_re + lam_im * lam_im
    f_re = ((a_re - 1.0) * lam_re + a_im * lam_im) / den
    f_im = (a_im * lam_re - (a_re - 1.0) * lam_im) / den
    bb_re = f_re[:, :, None] * b_re - f_im[:, :, None] * b_im
    bb_im = f_re[:, :, None] * b_im + f_im[:, :, None] * b_re
    return a_re, a_im, bb_re, bb_im


def _scan_tables(lam_re, lam_im, log_dt, conj, reverse, seg):
    dt = jnp.exp(log_dt)[:, None]
    lr = (lam_re * dt).reshape(1, NST)
    li = (lam_im * dt).reshape(1, NST) * (-1.0 if conj else 1.0)
    t = jnp.arange(8, dtype=F32)[:, None]

    def power(kk):
        mag = jnp.exp(kk * lr)
        return mag * jnp.cos(kk * li), mag * jnp.sin(kk * li)

    ones = jnp.ones((8, 1), F32)
    a8 = jnp.stack(power(ones)).astype(F32)
    tabs = []
    for s in (1, 2, 4):
        mask = (t <= 7 - s) if reverse else (t >= s)
        pr, pi = power(float(s * seg) * ones)
        tabs += [jnp.where(mask, pr, 0.0), jnp.where(mask, pi, 0.0)]
    kk = ((8.0 - t) if reverse else (t + 1.0)) * float(seg)
    pr, pi = power(kk)
    tabs += [pr, pi]
    return a8, jnp.stack(tabs).astype(F32)


def _pack_b(bb):
    t = bb.transpose(0, 2, 1).reshape(4, 8, SCH, SP)
    eye = jnp.eye(8, dtype=bb.dtype)
    return jnp.einsum('tgcp,gh->tgchp', t, eye).reshape(SW, ST_T)


def _pack_c(cc):
    t = cc.transpose(0, 2, 1).reshape(4, 8, SP, SCH)
    eye = jnp.eye(8, dtype=cc.dtype)
    return jnp.einsum('tgpc,gh->tgphc', t, eye).reshape(NST, CH_T)


def _unpack_diag(acc):
    t = acc.reshape(4, 8, SCH, 8, SP)
    eye = jnp.eye(8, dtype=acc.dtype)
    return jnp.einsum('tgchp,gh->tgcp', t, eye).reshape(SGRP, SCH, SP)


def _gelu(y):
    k0 = math.sqrt(2.0 / math.pi)
    inner = k0 * (y + 0.044715 * y * y * y)
    th = jnp.tanh(inner)
    z = 0.5 * y * (1.0 + th)
    dz = 0.5 * (1.0 + th) + 0.5 * y * (1.0 - th * th) * k0 * (1.0 + 3.0 * 0.044715 * y * y)
    return z, dz


Q0, K0, V0, U0, GS0, GA0, IN_COLS = 0, 1024, 1280, 1536, 2048, 3072, 4096


def mixer_forward(tag, h, p, tm):
    g, winT, wglu, wbsT, wba, wout = p["g"], p["winT"], p["wglu"], p["wbsT"], p["wba"], p["wout"]

    def proj(i, hv, g_ref, w_ref):
        _, _, n = _rms_fwd(hv, g_ref[...])
        nb = n.astype(BF16)
        return (nb, _nt(nb, w_ref[Q0:K0, :]) * QSCALE, _nt(nb, w_ref[K0:V0, :]), _nt(nb, w_ref[V0:U0, :]),
                _nt(nb, w_ref[U0:GS0, :]), _nt(nb, w_ref[GS0:GA0, :]), _nt(nb, w_ref[GA0:IN_COLS, :]))

    n, q, k, v, u, gs, ga = rowcall(
        tag + "_proj", proj, [h], [g, winT],
        [(D, BF16), (D, BF16), (N_KV * HD, BF16), (N_KV * HD, BF16), (SW, F32), (D, F32), (D, F32)],
        tm=_tall_tile(h.shape[0], tm))

    ya = attention_forward(tag, q, k, v, p["sink"])

    states, ydir = [], []
    for dr in range(2):
        s = p["ssm"][dr]
        xre, xim, yd = ssm_dir_forward(f"{tag}_ssm_fwd{dr}", u, s["bpr"], s["bpi"], s["cpr"], s["cpi"],
                                       s["a8"], s["tab"], dr == 1, tm)
        states.append((xre, xim))
        ydir.append(yd)

    def merge(i, y0, y1, uv, yav, gsv, gav, hv, d_ref, wglu_ref, wbs_ref, wba_ref, wout_ref):
        ypre = y0 + y1 + d_ref[...] * uv
        z, _ = _gelu(ypre)
        zb = z.astype(BF16)
        t = _nn(zb, wglu_ref[...])
        ysb = (z * _sig(t)).astype(BF16)
        bs = _nt(ysb, wbs_ref[...])
        ba = _nn(yav, wba_ref[...])
        mg = _sig(gsv) * bs + _sig(gav) * ba
        mg = jnp.where(_row_ok(i, tm), mg, 0.0).astype(BF16)
        return ypre, zb, t, ysb, bs, ba, mg, hv + _nn(mg, wout_ref[...])

    ypre, zb, t, ys, bs, ba, mg, h2 = rowcall(
        tag + "_merge", merge, [ydir[0], ydir[1], u, ya, gs, ga, h], [p["d"], wglu, wbsT, wba, wout],
        [(SW, F32), (SW, BF16), (SW, F32), (SW, BF16), (D, BF16), (D, BF16), (D, BF16), (D, F32)], tm=tm)
    saved = dict(h=h, n=n, q=q, k=k, v=v, u=u, gs=gs, ga=ga, ya=ya, states=states, ypre=ypre, zb=zb, t=t,
                 ys=ys, bs=bs, ba=ba, mg=mg)
    return h2, saved


def mixer_backward(tag, dh, sv, p, tm):
    g, winT, wglu, wbsT, wba, wout = p["g"], p["winT"], p["wglu"], p["wbsT"], p["wba"], p["wout"]

    def y1(i, dhv, bsv, bav, gsv, gav, ypv, tv, uv, wout_ref, wbs_ref, wba_ref, d_ref, wglu_ref):
        dhb = dhv.astype(BF16)
        dmg = _nt(dhb, wout_ref[...])
        dmg = jnp.where(_row_ok(i, tm), dmg, 0.0)
        sgs = _sig(gsv)
        sga = _sig(gav)
        dbs = (dmg * sgs).astype(BF16)
        dba = (dmg * sga).astype(BF16)
        dgs = dmg * bsv.astype(F32) * sgs * (1.0 - sgs)
        dga = dmg * bav.astype(F32) * sga * (1.0 - sga)
        dys = _nn(dbs, wbs_ref[...])
        dya = _nt(dba, wba_ref[...])
        z, dz_dy = _gelu(ypv)
        st = _sig(tv)
        dt_ = dys * z * st * (1.0 - st)
        dz = dys * st + _nt(dt_.astype(BF16), wglu_ref[...])
        dyp = dz * dz_dy
        return (dbs, dba, dgs, dga, dhb, dya, dyp, dyp * d_ref[...], dt_,
                jnp.sum(dyp * uv, axis=0, keepdims=True))

    dbs, dba, dgs, dga, dhb, dya, dypb, du0, dtb, dd = rowcall(
        tag + "_bwd_merge", y1,
        [dh, sv["bs"], sv["ba"], sv["gs"], sv["ga"], sv["ypre"], sv["t"], sv["u"]],
        [wout, wbsT, wba, p["d"], wglu],
        [(D, BF16)] * 6 + [(SW, F32), (SW, F32), (SW, BF16)], [(1, SW)], tm=tm)
    dwout = tn_matmul(tag + "_dwout", sv["mg"], dhb)
    dwbsT = tn_matmul(tag + "_dwbs", dbs, sv["ys"])
    dwba = tn_matmul(tag + "_dwba", sv["ya"], dba)
    dwglu = tn_matmul(tag + "_dwglu", sv["zb"], dtb)

    du_dirs, ssm_sums = [], []
    for dr in range(2):
        s = p["ssm"][dr]
        xre, xim = sv["states"][dr]
        res = ssm_dir_backward(f"{tag}_ssm_bwd{dr}", dypb, xre, xim, sv["u"], s["bpr"], s["bpi"], s["cpr"],
                               s["cpi"], s["a8_adj"], s["tab_adj"], dr == 0, tm)
        du_dirs.append(res[0])
        ssm_sums.append(res[1:])

    dq, dk, dv, dsink = attention_backward(tag, sv["q"], sv["k"], sv["v"], dya, p["sink"])

    def x1b(i, dqv, dkv, dvv, du0v, du1v, du2v, dgsv, dgav, hv, dhv, g_ref, w_ref):
        dub = (du0v + du1v + du2v).astype(BF16)
        dn = (_nn(dqv, w_ref[Q0:K0, :]) + _nn(dkv, w_ref[K0:V0, :]) + _nn(dvv, w_ref[V0:U0, :])
              + _nn(dub, w_ref[U0:GS0, :]) + _nn(dgsv, w_ref[GS0:GA0, :]) + _nn(dgav, w_ref[GA0:IN_COLS, :]))
        gv = g_ref[...]
        hh, r, _ = _rms_fwd(hv, gv)
        dx, dg = _rms_bwd(hh, r, gv, dn)
        dx = jnp.where(_row_ok(i, tall), dx, 0.0)
        return dhv + dx, dub, dg

    tall = _tall_tile(dh.shape[0], tm)
    dh2, dub, dg = rowcall(tag + "_bwd_in", x1b,
                           [dq, dk, dv, du0, du_dirs[0], du_dirs[1], dgs, dga, sv["h"], dh], [g, winT],
                           [(D, F32), (SW, BF16)], [(1, D)], tm=tall)
    n = sv["n"]
    dwinT = jnp.concatenate([tn_matmul(f"{tag}_dwin{j}", piece, n)
                             for j, piece in enumerate((dq, dk, dv, dub, dgs, dga))], axis=0)
    grads = dict(g=dg, d=dd, sink=dsink, ssm=ssm_sums, winT=dwinT, wglu=dwglu, wbsT=dwbsT, wba=dwba, wout=dwout)
    return dh2, grads


W1024 = ("f1_wgT", "f1_wuT", "f1_wd", "winT", "wba", "wout", "f2_wgT", "f2_wuT", "f2_wd")
W512 = ("wglu", "wbsT")
PART_F1 = ("f1_wgT", "f1_wuT", "f1_wd")
PART_MIX = ("winT", "wba", "wout", "wglu", "wbsT")
PART_F2 = ("f2_wgT", "f2_wuT", "f2_wd")
PER_LAYER_SMALL = ("ffn1_norm", "mix_norm", "ffn2_norm", "ssm_lam_re", "ssm_lam_im", "ssm_log_dt",
                   "ssm_b_re", "ssm_b_im", "ssm_c_re", "ssm_c_im", "ssm_d", "attn_sink")
SMALL = ("ffn1_norm", "mix_norm", "ffn2_norm", "final_norm", "ssm_lam_re", "ssm_lam_im", "ssm_log_dt",
         "ssm_b_re", "ssm_b_im", "ssm_c_re", "ssm_c_im", "ssm_d", "attn_sink")


def kernel(x, meta_tokens, ffn1_norm, ffn1_w_gate, ffn1_w_up, ffn1_w_down, mix_norm, w_in, ssm_lam_re, ssm_lam_im, ssm_log_dt, ssm_b_re, ssm_b_im, ssm_c_re, ssm_c_im, ssm_d, ssm_w_glu, attn_sink, w_branch_ssm, w_branch_attn, w_out, ffn2_norm, ffn2_w_gate, ffn2_w_up, ffn2_w_down, final_norm, loss_target, m_meta_tokens, m_ffn1_norm, m_ffn1_w_gate, m_ffn1_w_up, m_ffn1_w_down, m_mix_norm, m_w_in, m_ssm_lam_re, m_ssm_lam_im, m_ssm_log_dt, m_ssm_b_re, m_ssm_b_im, m_ssm_c_re, m_ssm_c_im, m_ssm_d, m_ssm_w_glu, m_attn_sink, m_w_branch_ssm, m_w_branch_attn, m_w_out, m_ffn2_norm, m_ffn2_w_gate, m_ffn2_w_up, m_ffn2_w_down, m_final_norm, v_meta_tokens, v_ffn1_norm, v_ffn1_w_gate, v_ffn1_w_up, v_ffn1_w_down, v_mix_norm, v_w_in, v_ssm_lam_re, v_ssm_lam_im, v_ssm_log_dt, v_ssm_b_re, v_ssm_b_im, v_ssm_c_re, v_ssm_c_im, v_ssm_d, v_ssm_w_glu, v_attn_sink, v_w_branch_ssm, v_w_branch_attn, v_w_out, v_ffn2_norm, v_ffn2_w_gate, v_ffn2_w_up, v_ffn2_w_down, v_final_norm):
    weights = dict(meta_tokens=meta_tokens, ffn1_norm=ffn1_norm, ffn1_w_gate=ffn1_w_gate, ffn1_w_up=ffn1_w_up, ffn1_w_down=ffn1_w_down, mix_norm=mix_norm, w_in=w_in, ssm_lam_re=ssm_lam_re, ssm_lam_im=ssm_lam_im, ssm_log_dt=ssm_log_dt, ssm_b_re=ssm_b_re, ssm_b_im=ssm_b_im, ssm_c_re=ssm_c_re, ssm_c_im=ssm_c_im, ssm_d=ssm_d, ssm_w_glu=ssm_w_glu, attn_sink=attn_sink, w_branch_ssm=w_branch_ssm, w_branch_attn=w_branch_attn, w_out=w_out, ffn2_norm=ffn2_norm, ffn2_w_gate=ffn2_w_gate, ffn2_w_up=ffn2_w_up, ffn2_w_down=ffn2_w_down, final_norm=final_norm)
    mom_m = dict(meta_tokens=m_meta_tokens, ffn1_norm=m_ffn1_norm, ffn1_w_gate=m_ffn1_w_gate, ffn1_w_up=m_ffn1_w_up, ffn1_w_down=m_ffn1_w_down, mix_norm=m_mix_norm, w_in=m_w_in, ssm_lam_re=m_ssm_lam_re, ssm_lam_im=m_ssm_lam_im, ssm_log_dt=m_ssm_log_dt, ssm_b_re=m_ssm_b_re, ssm_b_im=m_ssm_b_im, ssm_c_re=m_ssm_c_re, ssm_c_im=m_ssm_c_im, ssm_d=m_ssm_d, ssm_w_glu=m_ssm_w_glu, attn_sink=m_attn_sink, w_branch_ssm=m_w_branch_ssm, w_branch_attn=m_w_branch_attn, w_out=m_w_out, ffn2_norm=m_ffn2_norm, ffn2_w_gate=m_ffn2_w_gate, ffn2_w_up=m_ffn2_w_up, ffn2_w_down=m_ffn2_w_down, final_norm=m_final_norm)
    mom_v = dict(meta_tokens=v_meta_tokens, ffn1_norm=v_ffn1_norm, ffn1_w_gate=v_ffn1_w_gate, ffn1_w_up=v_ffn1_w_up, ffn1_w_down=v_ffn1_w_down, mix_norm=v_mix_norm, w_in=v_w_in, ssm_lam_re=v_ssm_lam_re, ssm_lam_im=v_ssm_lam_im, ssm_log_dt=v_ssm_log_dt, ssm_b_re=v_ssm_b_re, ssm_b_im=v_ssm_b_im, ssm_c_re=v_ssm_c_re, ssm_c_im=v_ssm_c_im, ssm_d=v_ssm_d, ssm_w_glu=v_ssm_w_glu, attn_sink=v_attn_sink, w_branch_ssm=v_w_branch_ssm, w_branch_attn=v_w_branch_attn, w_out=v_w_out, ffn2_norm=v_ffn2_norm, ffn2_w_gate=v_ffn2_w_gate, ffn2_w_up=v_ffn2_w_up, ffn2_w_down=v_ffn2_w_down, final_norm=v_final_norm)
    names = list(weights)

    L0 = x.shape[1]
    LP = L0 + BLK
    tm = 384 if LP % 384 == 0 else BLK
    x_i, y_i, c_i = lax.axis_index("x"), lax.axis_index("y"), lax.axis_index("c")
    me = 4 * x_i + 2 * y_i + c_i

    def canon(l):
        return dict(
            f1_wgT=ffn1_w_gate[l].T, f1_wuT=ffn1_w_up[l].T, f1_wd=ffn1_w_down[l],
            winT=w_in[l].T, wba=w_branch_attn[l], wout=w_out[l],
            f2_wgT=ffn2_w_gate[l].T, f2_wuT=ffn2_w_up[l].T, f2_wd=ffn2_w_down[l],
            wglu=ssm_w_glu[l], wbsT=w_branch_ssm[l].T)

    shards = [{nm: a.astype(BF16) for nm, a in canon(l).items()} for l in range(DEPTH)]

    def rows_of(nm):
        return shards[0][nm].shape[0]

    def width_groups(names_):
        return [g for g in ([nm for nm in names_ if nm in W1024], [nm for nm in names_ if nm in W512]) if g]

    def pieces_for(group):
        out, off = [], 0
        for nm in group:
            out.append((off, rows_of(nm)))
            off += rows_of(nm)
        return out

    def start_gather(tag, l, names_):
        groups = width_groups(names_)
        packed = [jnp.concatenate([shards[l][nm] for nm in g], axis=0) for g in groups]
        return gather_layer_start(tag, packed, [pieces_for(g) for g in groups]), groups

    def finish_gather(tag, l, started_, after):
        handle, groups = started_
        dests = gather_layer_wait(tag, handle, len(groups), after)
        out = {}
        for nm, dest in zip([nm for g in groups for nm in g], dests):
            sh = shards[l][nm]
            out[nm] = lax.dynamic_update_slice(dest, sh, (me * sh.shape[0], 0))
        return out

    g1, gm = all_gather_pieces(
        "gather_weights_first",
        [(jnp.concatenate([shards[0][nm] for nm in PART_F1], axis=0), pieces_for(PART_F1)),
         (meta_tokens, [(0, N_META)])])
    first_weights = dict(zip(PART_F1, g1))
    meta_full = gm[0].reshape(NDEV, N_META, D // NDEV).transpose(1, 0, 2).reshape(N_META, D)
    gather_started = [start_gather("gather_start_l0", 0, PART_MIX + PART_F2)]
    gather_started += [start_gather(f"gather_start_l{l}", l, W1024 + W512) for l in range(1, DEPTH)]
    started = sum(st[0][3][0, 0] for st in gather_started)
    full = [None] * DEPTH

    def disc_all(lre, lim, ldt, bre, bim):
        return _ssm_disc(lre, lim, ldt, bre, bim)

    ssm_p, ssm_vjp = [], []
    for l in range(DEPTH):
        row, vrow = [], []
        for dr in range(2):
            args = (ssm_lam_re[l, dr], ssm_lam_im[l, dr], ssm_log_dt[l, dr], ssm_b_re[l, dr], ssm_b_im[l, dr])
            (a_re, a_im, bb_re, bb_im), vjp = jax.vjp(disc_all, *args)
            a8, tab = _scan_tables(args[0], args[1], args[2], False, dr == 1, tm // 8)
            a8_adj, tab_adj = _scan_tables(args[0], args[1], args[2], True, dr == 0, tm // 8)
            row.append(dict(
                bpr=_pack_b(bb_re).astype(BF16), bpi=_pack_b(bb_im).astype(BF16),
                cpr=_pack_c(ssm_c_re[l, dr]).astype(BF16), cpi=_pack_c(ssm_c_im[l, dr]).astype(BF16),
                a8=a8, tab=tab, a8_adj=a8_adj, tab_adj=tab_adj, a_re=a_re, a_im=a_im))
            vrow.append(vjp)
        ssm_p.append(row)
        ssm_vjp.append(vrow)

    blk0 = jnp.concatenate([jnp.zeros((PAD, D), F32), meta_full.astype(F32)], axis=0)
    h = build_h0(x[0], blk0)
    saved = []
    for l in range(DEPTH):
        w = dict(first_weights) if l == 0 else finish_gather(f"gather_wait_l{l}", l, gather_started[l], h)
        full[l] = w
        g1n, g2n = ffn1_norm[l][None, :], ffn2_norm[l][None, :]
        if l == 0:
            g1n = g1n + started
        h, s1 = ffn_forward("ffn1", h, g1n, w["f1_wgT"], w["f1_wuT"], w["f1_wd"], tm)
        if l == 0:
            w.update(finish_gather("gather_wait_l0", 0, gather_started[0], h))
        mp = dict(g=mix_norm[l][None, :], winT=w["winT"], wglu=w["wglu"], wbsT=w["wbsT"], wba=w["wba"],
                  wout=w["wout"], d=ssm_d[l][None, :], sink=attn_sink[l], ssm=ssm_p[l])
        h, s2 = mixer_forward("mix", h, mp, tm)
        h, s3 = ffn_forward("ffn2", h, g2n, w["f2_wgT"], w["f2_wuT"], w["f2_wd"], tm)
        saved.append((s1, s2, s3, mp, g1n, g2n))

    dh, loss_acc, dgf = final_loss(h, loss_target[0], final_norm[None, :])
    loss = lax.psum(loss_acc[0, 0], MESH_AXES)

    small = {nm: [None] * DEPTH for nm in SMALL if nm != "final_norm"}

    def start_scatter(tag, grads_d, names_):
        groups = width_groups(names_)
        mine = [jnp.concatenate([lax.dynamic_slice_in_dim(grads_d[nm], me * rows_of(nm), rows_of(nm), axis=0)
                                 for nm in g], axis=0) for g in groups]
        handle, nin = scatter_layer_start(tag + "_start", [[grads_d[nm] for nm in g] for g in groups])
        return dict(tag=tag, handle=handle, nin=nin, groups=groups, mine=mine)

    def finish_scatter(st, after):
        lands = scatter_layer_wait(st["tag"] + "_wait", st["handle"], st["nin"], len(st["groups"]), after)
        out = {}
        for land, mine, g in zip(lands, st["mine"], st["groups"]):
            land = lax.dynamic_update_slice(land, mine[None], (me, 0, 0))
            tot = sum_slots(f"sum_weight_grads_{land.shape[1]}x{land.shape[2]}", land)
            for nm, (off_, r) in zip(g, pieces_for(g)):
                out[nm] = tot[off_:off_ + r]
        return out

    scatters = []
    small_started = [None] * DEPTH
    small_len = sum(math.prod(weights[k].shape[1:]) for k in PER_LAYER_SMALL) + N_META * D
    small_rows = -(-small_len // (8 * D)) * 8
    sent = jnp.zeros((), F32)
    for l in reversed(range(DEPTH)):
        s1, s2, s3, mp, g1n, g2n = saved[l]
        w = full[l]
        dh, dg2, f2g, f2u, f2d = ffn_backward("ffn2", dh, s3, g2n + sent, w["f2_wgT"], w["f2_wuT"], w["f2_wd"], tm)
        st = start_scatter(f"scatter_l{l}_f2", dict(f2_wgT=f2g, f2_wuT=f2u, f2_wd=f2d), PART_F2)
        scatters.append((l, st))
        dh, mg = mixer_backward("mix", dh, s2, dict(mp, d=mp["d"] + st["handle"][3][0, 0]), tm)
        st = start_scatter(f"scatter_l{l}_mix", mg, PART_MIX)
        scatters.append((l, st))
        if l > 0:
            dh, dg1, f1g, f1u, f1d = ffn_backward("ffn1", dh, s1, g1n + st["handle"][3][0, 0],
                                                  w["f1_wgT"], w["f1_wuT"], w["f1_wd"], tm)
            st = start_scatter(f"scatter_l{l}_f1", dict(f1_wgT=f1g, f1_wuT=f1u, f1_wd=f1d), PART_F1)
            scatters.append((l, st))
            sent = st["handle"][3][0, 0]
        else:
            def emit(nm, arr, l=l):
                scatters.append((l, start_scatter(f"scatter_l{l}_f1_{nm}", {"f1_" + nm: arr}, ("f1_" + nm,))))

            dh, dg1, _, _, _ = ffn_backward("ffn1", dh, s1, g1n + st["handle"][3][0, 0],
                                            w["f1_wgT"], w["f1_wuT"], w["f1_wd"], tm, emit=emit)
        small["ffn1_norm"][l] = dg1[0]
        small["mix_norm"][l] = mg["g"][0]
        small["ffn2_norm"][l] = dg2[0]
        small["ssm_d"][l] = mg["d"][0]
        small["attn_sink"][l] = mg["sink"][0, :N_HEADS]
        per_dir = {k: [] for k in ("ssm_lam_re", "ssm_lam_im", "ssm_log_dt", "ssm_b_re", "ssm_b_im",
                                   "ssm_c_re", "ssm_c_im")}
        for dr in range(2):
            gbr, gbi, gcr, gci, s_re, s_im = mg["ssm"][dr]
            a_re, a_im = ssm_p[l][dr]["a_re"], ssm_p[l][dr]["a_im"]
            s_re = s_re.reshape(SGRP, SP)
            s_im = s_im.reshape(SGRP, SP)
            den = a_re * a_re + a_im * a_im
            ga_re = (s_re * a_re - s_im * a_im) / den
            ga_im = (s_re * a_im + s_im * a_re) / den
            glr, gli, gld, gbre, gbim = ssm_vjp[l][dr]((ga_re, ga_im, _unpack_diag(gbr).transpose(0, 2, 1),
                                                        _unpack_diag(gbi).transpose(0, 2, 1)))
            per_dir["ssm_lam_re"].append(glr)
            per_dir["ssm_lam_im"].append(gli)
            per_dir["ssm_log_dt"].append(gld)
            per_dir["ssm_b_re"].append(gbre)
            per_dir["ssm_b_im"].append(gbim)
            per_dir["ssm_c_re"].append(_unpack_diag(gcr))
            per_dir["ssm_c_im"].append(_unpack_diag(gci))
        for k, vlist in per_dir.items():
            small[k][l] = jnp.stack(vlist)
        vec = [small[k][l].reshape(-1) for k in PER_LAYER_SMALL]
        if l == DEPTH - 1:
            vec.append(dgf[0])
        if l == 0:
            vec.append(dh[PAD:BLK].reshape(-1))
        used = sum(v.shape[0] for v in vec)
        flat = jnp.concatenate(vec + [jnp.zeros((small_rows * D - used,), F32)]).reshape(small_rows, D)
        small_started[l] = (exchange_start(
            f"small_grads_l{l}_start", [flat, jnp.zeros((NDEV, small_rows, D), F32)], 1,
            lambda refs, me_i, p_i: [(refs[0], refs[1].at[me_i], 0)]), flat)
        sent = sent + small_started[l][0][3][0, 0]

    grad_x = dh[BLK:][None]

    grads = {k: [None] * DEPTH for k in PER_LAYER_SMALL}
    for l in reversed(range(DEPTH)):
        (s_sems, r_sems, arrays, _), flat = small_started[l]
        land = exchange_wait(f"small_grads_l{l}_wait", s_sems, r_sems, arrays, 1, lambda refs, gi: refs[0], dh)[1]
        land = lax.dynamic_update_slice(land, flat[None], (me, 0, 0))
        tot = sum_slots("sum_small_grads", land).reshape(-1)
        o = 0
        for k in PER_LAYER_SMALL:
            shp = weights[k].shape[1:]
            sz = math.prod(shp)
            grads[k][l] = tot[o:o + sz].reshape(shp)
            o += sz
        if l == DEPTH - 1:
            final_norm_grad = tot[o:o + D]
        if l == 0:
            dmeta_full = tot[o:o + N_META * D].reshape(N_META, D)
    grads = {k: jnp.stack(vv) for k, vv in grads.items()}
    grads["final_norm"] = final_norm_grad
    grads["meta_tokens"] = lax.dynamic_slice_in_dim(dmeta_full, me * (D // NDEV), D // NDEV, axis=1)

    own = [dict() for _ in range(DEPTH)]
    for l, st in scatters:
        own[l].update(finish_scatter(st, dh))

    def stack(fn):
        return jnp.stack([fn(own[l]) for l in range(DEPTH)])

    grads["ffn1_w_gate"] = stack(lambda d: d["f1_wgT"].T)
    grads["ffn1_w_up"] = stack(lambda d: d["f1_wuT"].T)
    grads["ffn1_w_down"] = stack(lambda d: d["f1_wd"])
    grads["w_in"] = stack(lambda d: d["winT"].T)
    grads["ssm_w_glu"] = stack(lambda d: d["wglu"])
    grads["w_branch_ssm"] = stack(lambda d: d["wbsT"].T)
    grads["w_branch_attn"] = stack(lambda d: d["wba"])
    grads["w_out"] = stack(lambda d: d["wout"])
    grads["ffn2_w_gate"] = stack(lambda d: d["f2_wgT"].T)
    grads["ffn2_w_up"] = stack(lambda d: d["f2_wuT"].T)
    grads["ffn2_w_down"] = stack(lambda d: d["f2_wd"])

    deltas, new_m, new_v = {}, {}, {}
    for nm in names:
        deltas[nm], new_m[nm], new_v[nm] = adamw("adamw_" + nm, weights[nm], grads[nm], mom_m[nm], mom_v[nm])

    return (loss, grad_x, *[grads[n] for n in names], *[deltas[n] for n in names],
            *[new_m[n] for n in names], *[new_v[n] for n in names])
```

```python
import functools
import math

import jax
import jax.numpy as jnp
from jax import lax
from jax.experimental import pallas as pl
from jax.experimental.pallas import tpu as pltpu

F32 = jnp.float32
BF16 = jnp.bfloat16

D = 1024
DFF = 2816
N_META = 16
N_HEADS = 16
N_KV = 4
HD = 64
QG = 4
WIN = 128
BLK = 128
PAD = BLK - N_META
SW = 512
SGRP = 32
SCH = 16
SP = 64
NST = SGRP * SP
EPS = 1e-6
NEG = -1e30
SCALE = HD ** -0.5
NDEV = 8
DEPTH = 4
MESH_AXES = ("x", "y", "c")
MESH = pl.DeviceIdType.MESH

ADAM_LR = 0.001
ADAM_B1 = 0.9
ADAM_B2 = 0.999
ADAM_EPS = 1e-08
ADAM_WD = 0.01
ADAM_STEP = 10

VMEM_LIMIT = 56 * 1024 * 1024


def _params(*sem):
    return pltpu.CompilerParams(dimension_semantics=sem, vmem_limit_bytes=VMEM_LIMIT)


def _nn(a, b):
    return lax.dot_general(a, b, (((1,), (0,)), ((), ())), preferred_element_type=F32)


def _nt(a, b):
    return lax.dot_general(a, b, (((1,), (1,)), ((), ())), preferred_element_type=F32)


def _tn(a, b):
    return lax.dot_general(a, b, (((0,), (0,)), ((), ())), preferred_element_type=F32)


def _sig(x):
    return 0.5 * jnp.tanh(0.5 * x) + 0.5


def _rms_fwd(h, g):
    r = lax.rsqrt(jnp.mean(h * h, axis=-1, keepdims=True) + EPS)
    hh = h * r
    return hh, r, hh * g


def _rms_bwd(hh, r, g, dn):
    dhh = dn * g
    dx = r * (dhh - hh * jnp.mean(dhh * hh, axis=-1, keepdims=True))
    return dx, jnp.sum(dn * hh, axis=0, keepdims=True)


def _row_ok(i, tm):
    rows = i * tm + lax.broadcasted_iota(jnp.int32, (tm, 1), 0)
    return rows >= PAD


def _const_spec(shape, single=False):
    nd = len(shape)
    if single:
        return pl.BlockSpec(shape, lambda *_: (0,) * nd, pipeline_mode=pl.Buffered(1))
    return pl.BlockSpec(shape, lambda *_: (0,) * nd)


def rowcall(name, body, rows, consts, outs, accs=(), *, tm):
    nrows = rows[0].shape[0]
    nt = nrows // tm
    assert nt * tm == nrows, (name, nrows, tm)
    nr, nc, no, na = len(rows), len(consts), len(outs), len(accs)
    in_specs = [pl.BlockSpec((tm, r.shape[1]), lambda i: (i, 0)) for r in rows]
    in_specs += [_const_spec(c.shape, single=True) for c in consts]
    out_shape = [jax.ShapeDtypeStruct((nrows, w), dt) for (w, dt) in outs]
    out_specs = [pl.BlockSpec((tm, w), lambda i: (i, 0)) for (w, dt) in outs]
    out_shape += [jax.ShapeDtypeStruct(s, F32) for s in accs]
    out_specs += [_const_spec(s) for s in accs]

    def kern(*refs):
        i = pl.program_id(0)
        row_vals = [r[...] for r in refs[:nr]]
        res = body(i, *row_vals, *refs[nr:nr + nc])
        out_refs = refs[nr + nc:nr + nc + no]
        acc_refs = refs[nr + nc + no:]
        for r, v in zip(out_refs, res[:no]):
            r[...] = v.astype(r.dtype)
        if na:
            @pl.when(i == 0)
            def _():
                for r in acc_refs:
                    r[...] = jnp.zeros_like(r)
            for r, v in zip(acc_refs, res[no:]):
                r[...] += v

    res = pl.pallas_call(
        kern, name=name, grid=(nt,), in_specs=in_specs, out_specs=out_specs, out_shape=out_shape,
        compiler_params=_params("arbitrary"),
    )(*rows, *consts)
    return res


def tn_matmul(name, lhs, rhs, scale=1.0):
    M, K = lhs.shape
    N = rhs.shape[1]
    assert lhs.dtype == BF16 and rhs.dtype == BF16
    nm = 6
    tmw = M // nm
    assert tmw * nm == M and tmw % 16 == 0
    tk = 1408 if (K % 1408 == 0) else K
    nk = K // tk

    def kern(a_ref, b_ref, o_ref, acc):
        m = pl.program_id(1)
        part = _tn(a_ref[...], b_ref[...])

        @pl.when(m == 0)
        def _():
            acc[...] = part

        @pl.when((m > 0) & (m < nm - 1))
        def _():
            acc[...] += part

        @pl.when(m == nm - 1)
        def _():
            o_ref[...] = ((acc[...] + part) * scale).astype(o_ref.dtype)

    return pl.pallas_call(
        kern, name=name, grid=(nk, nm),
        in_specs=[pl.BlockSpec((tmw, tk), lambda k, m: (m, k)), pl.BlockSpec((tmw, N), lambda k, m: (m, 0))],
        out_specs=pl.BlockSpec((tk, N), lambda k, m: (k, 0)),
        out_shape=jax.ShapeDtypeStruct((K, N), BF16),
        scratch_shapes=[pltpu.VMEM((tk, N), F32)],
        compiler_params=_params("arbitrary", "arbitrary"),
    )(lhs, rhs)


def _mesh_pos():
    x, y, c = lax.axis_index("x"), lax.axis_index("y"), lax.axis_index("c")
    return x, y, c


def all_gather_pieces(name, groups):
    ng = len(groups)
    packed = [g[0] for g in groups]
    pieces = [g[1] for g in groups]
    out_shape, out_map = [], []
    for gi, (p, pcs) in enumerate(groups):
        idx = []
        for (off, r) in pcs:
            idx.append(len(out_shape))
            out_shape.append(jax.ShapeDtypeStruct((NDEV * r, p.shape[1]), p.dtype))
        out_map.append(idx)
    nout = len(out_shape)

    def body(*refs):
        p_refs = refs[:ng]
        o_refs = refs[ng:ng + nout]
        send_sems, recv_sems, local_sems = refs[ng + nout:]
        x, y, c = _mesh_pos()
        me = (x, y, c)
        sibling = (x, y, 1 - c)
        chips = [(1 - x, y), (x, 1 - y), (1 - x, 1 - y)]

        def blk(px, py, pc):
            return 4 * px + 2 * py + pc

        def copies(gi, k, origin, to, from_out):
            cps = []
            for (off, r), oi in zip(pieces[gi], out_map[gi]):
                dst = o_refs[oi].at[pl.ds(origin * r, r), :]
                src = dst if from_out else p_refs[gi].at[pl.ds(off, r), :]
                cps.append(pltpu.make_async_remote_copy(
                    src_ref=src, dst_ref=dst, send_sem=send_sems.at[gi, k], recv_sem=recv_sems.at[gi, k],
                    device_id=to, device_id_type=MESH))
            return cps

        def whole(gi, k):
            return pltpu.make_async_remote_copy(
                src_ref=p_refs[gi], dst_ref=p_refs[gi], send_sem=send_sems.at[gi, k],
                recv_sem=recv_sems.at[gi, k], device_id=me, device_id_type=MESH)

        mine = []
        for gi in range(ng):
            for (off, r), oi in zip(pieces[gi], out_map[gi]):
                mine.append(pltpu.make_async_copy(
                    p_refs[gi].at[pl.ds(off, r), :], o_refs[oi].at[pl.ds(blk(*me) * r, r), :],
                    local_sems.at[gi]))
        for cp in mine:
            cp.start()
        for gi in range(ng):
            for cp in copies(gi, 0, blk(*me), sibling, False):
                cp.start()
            for j, chip in enumerate(chips):
                for cp in copies(gi, 1 + j, blk(*me), (*chip, c), False):
                    cp.start()
        for j, chip in enumerate(chips):
            for gi in range(ng):
                whole(gi, 1 + j).wait_recv()
                for cp in copies(gi, 4 + j, blk(*chip, c), sibling, True):
                    cp.start()
        for gi in range(ng):
            whole(gi, 0).wait_recv()
            for j in range(3):
                whole(gi, 4 + j).wait_recv()
        for gi in range(ng):
            for k in range(7):
                whole(gi, k).wait_send()
            pltpu.make_async_copy(p_refs[gi], p_refs[gi], local_sems.at[gi]).wait()

    any_spec = pl.BlockSpec(memory_space=pl.ANY)
    outs = pl.pallas_call(
        body, name=name, out_shape=out_shape,
        in_specs=[any_spec] * ng, out_specs=[any_spec] * nout,
        scratch_shapes=[pltpu.SemaphoreType.DMA((ng, 7)), pltpu.SemaphoreType.DMA((ng, 7)),
                        pltpu.SemaphoreType.DMA((ng,))],
    )(*packed)
    return [[outs[oi] for oi in idx] for idx in out_map]


HBM_SPEC = pl.BlockSpec(memory_space=pltpu.HBM)
SEM_SPEC = pl.BlockSpec(memory_space=pltpu.SEMAPHORE)
DATAFLOW = pltpu.SideEffectType.DATAFLOW_SIDE_EFFECTING


def _peers(x, y, c):
    return [(x, y, 1 - c), (1 - x, y, c), (x, 1 - y, c), (1 - x, 1 - y, c),
            (1 - x, y, 1 - c), (x, 1 - y, 1 - c), (1 - x, 1 - y, 1 - c)]


def exchange_start(name, arrays, ng, plan):
    n = len(arrays)
    ns = ng * 7

    def body(*refs):
        in_refs = refs[:n]
        send_sems, recv_sems = refs[n:n + ns], refs[n + ns:n + 2 * ns]
        token = refs[-1]
        x, y, c = _mesh_pos()
        me_i = 4 * x + 2 * y + c
        for k, peer in enumerate(_peers(x, y, c)):
            p_i = 4 * peer[0] + 2 * peer[1] + peer[2]
            for src, dst, gi in plan(in_refs, me_i, p_i):
                pltpu.make_async_remote_copy(
                    src_ref=src, dst_ref=dst, send_sem=send_sems[gi * 7 + k], recv_sem=recv_sems[gi * 7 + k],
                    device_id=peer, device_id_type=MESH).start()
        token[...] = jnp.zeros_like(token)

    res = pl.pallas_call(
        body, name=name,
        out_shape=(*[pltpu.SemaphoreType.DMA(())] * (2 * ns),
                   *[pltpu.HBM(a.shape, a.dtype) for a in arrays], jax.ShapeDtypeStruct((8, 128), F32)),
        in_specs=[HBM_SPEC] * n,
        out_specs=(*[SEM_SPEC] * (2 * ns), *[HBM_SPEC] * n, pl.BlockSpec(memory_space=pltpu.VMEM)),
        input_output_aliases={i: 2 * ns + i for i in range(n)},
        compiler_params=pltpu.CompilerParams(has_side_effects=DATAFLOW),
    )(*[pltpu.with_memory_space_constraint(a, pltpu.HBM) for a in arrays])
    return list(res[:ns]), list(res[ns:2 * ns]), list(res[2 * ns:2 * ns + n]), res[-1]


def exchange_wait(name, send_sems, recv_sems, arrays, ng, sized, after):
    n = len(arrays)
    ns = ng * 7

    def body(*refs):
        in_refs = refs[:n]
        s_sems, r_sems = refs[n:n + ns], refs[n + ns:n + 2 * ns]
        x, y, c = _mesh_pos()
        for gi in range(ng):
            view = sized(in_refs, gi)
            for k in range(7):
                w = pltpu.make_async_remote_copy(
                    src_ref=view, dst_ref=view, send_sem=s_sems[gi * 7 + k], recv_sem=r_sems[gi * 7 + k],
                    device_id=(x, y, c), device_id_type=MESH)
                w.wait_send()
                w.wait_recv()

    res = pl.pallas_call(
        body, name=name, out_shape=tuple(pltpu.HBM(a.shape, a.dtype) for a in arrays),
        in_specs=[HBM_SPEC] * n + [SEM_SPEC] * (2 * ns) + [pl.BlockSpec(memory_space=pl.ANY)],
        out_specs=tuple([HBM_SPEC] * n), input_output_aliases={i: i for i in range(n)},
        compiler_params=pltpu.CompilerParams(has_side_effects=DATAFLOW),
    )(*arrays, *send_sems, *recv_sems, after)
    return list(res)


def gather_layer_start(name, packed, pieces):
    ng = len(packed)
    dests = [lax.empty((NDEV * r, p.shape[1]), p.dtype) for p, pcs in zip(packed, pieces) for (_, r) in pcs]

    def plan(refs, me_i, p_i):
        out, di = [], ng
        for gi in range(ng):
            for (off, r) in pieces[gi]:
                out.append((refs[gi].at[pl.ds(off, r), :], refs[di].at[pl.ds(me_i * r, r), :], gi))
                di += 1
        return out

    return exchange_start(name, list(packed) + dests, ng, plan)


def gather_layer_wait(name, handle, ng, after):
    send_sems, recv_sems, arrays, _ = handle
    out = exchange_wait(name, send_sems, recv_sems, arrays, ng, lambda refs, gi: refs[gi], after)
    return out[ng:]


def scatter_layer_start(name, groups):
    ng = len(groups)
    flat = [a for arrs in groups for a in arrs]
    offs, lands = [], []
    for arrs in groups:
        o, off = [], 0
        for a in arrs:
            r = a.shape[0] // NDEV
            o.append((off, r))
            off += r
        offs.append(o)
        lands.append(jnp.zeros((NDEV, off, arrs[0].shape[1]), arrs[0].dtype))
    nin = len(flat)

    def plan(refs, me_i, p_i):
        out, ai = [], 0
        for gi in range(ng):
            for (off, r) in offs[gi]:
                out.append((refs[ai].at[pl.ds(p_i * r, r), :], refs[nin + gi].at[me_i, pl.ds(off, r), :], gi))
                ai += 1
        return out

    return exchange_start(name, flat + lands, ng, plan), nin


def scatter_layer_wait(name, handle, nin, ng, after):
    send_sems, recv_sems, arrays, _ = handle
    out = exchange_wait(name, send_sems, recv_sems, arrays, ng, lambda refs, gi: refs[nin + gi].at[0], after)
    return out[nin:]


def _pick_tile(n, cap):
    best = None
    for t in range(8, min(n, cap) + 1, 8):
        if n % t == 0:
            best = t
    return best if best is not None else n


def sum_slots(name, land):
    _, R, W = land.shape
    tr = _pick_tile(R, 512)

    def kern(l_ref, o_ref):
        acc = l_ref[0].astype(F32)
        for s in range(1, NDEV):
            acc = acc + l_ref[s].astype(F32)
        o_ref[...] = acc

    return pl.pallas_call(
        kern, name=name, grid=(R // tr,),
        in_specs=[pl.BlockSpec((NDEV, tr, W), lambda i: (0, i, 0))],
        out_specs=pl.BlockSpec((tr, W), lambda i: (i, 0)),
        out_shape=jax.ShapeDtypeStruct((R, W), F32),
        compiler_params=_params("arbitrary"),
    )(land)


def adamw(name, w, g, m, v):
    shp = w.shape
    C = shp[-1]
    R = max(1, math.prod(shp[:-1]))
    tr = _pick_tile(R, 1024)
    w2, g2, m2, v2 = (a.reshape(R, C) for a in (w, g, m, v))

    def kern(w_ref, g_ref, m_ref, v_ref, d_ref, mo_ref, vo_ref):
        gg = g_ref[...]
        mn = ADAM_B1 * m_ref[...] + (1.0 - ADAM_B1) * gg
        vn = ADAM_B2 * v_ref[...] + (1.0 - ADAM_B2) * jnp.square(gg)
        m_hat = mn / (1.0 - ADAM_B1 ** ADAM_STEP)
        v_hat = vn / (1.0 - ADAM_B2 ** ADAM_STEP)
        d_ref[...] = -ADAM_LR * (m_hat / (jnp.sqrt(v_hat) + ADAM_EPS) + ADAM_WD * w_ref[...])
        mo_ref[...] = mn
        vo_ref[...] = vn

    spec = pl.BlockSpec((tr, C), lambda i: (i, 0))
    d, mo, vo = pl.pallas_call(
        kern, name=name, grid=(R // tr,), in_specs=[spec] * 4, out_specs=[spec] * 3,
        out_shape=[jax.ShapeDtypeStruct((R, C), F32)] * 3, compiler_params=_params("arbitrary"),
    )(w2, g2, m2, v2)
    return d.reshape(shp), mo.reshape(shp), vo.reshape(shp)


def build_h0(x2, blk0):
    L0 = x2.shape[0]
    nb = L0 // BLK + 1

    def kern(x_ref, b_ref, o_ref):
        i = pl.program_id(0)

        @pl.when(i == 0)
        def _():
            o_ref[...] = b_ref[...]

        @pl.when(i > 0)
        def _():
            o_ref[...] = x_ref[...]

    return pl.pallas_call(
        kern, name="build_h0", grid=(nb,),
        in_specs=[pl.BlockSpec((BLK, D), lambda i: (jnp.maximum(i - 1, 0), 0)), _const_spec((BLK, D))],
        out_specs=pl.BlockSpec((BLK, D), lambda i: (i, 0)),
        out_shape=jax.ShapeDtypeStruct((L0 + BLK, D), F32), compiler_params=_params("arbitrary"),
    )(x2, blk0)


def final_loss(h, tgt, gf):
    LP = h.shape[0]
    nb = LP // BLK

    def kern(h_ref, t_ref, g_ref, dh_ref, loss_ref, dg_ref):
        i = pl.program_id(0)

        @pl.when(i == 0)
        def _():
            loss_ref[...] = jnp.zeros_like(loss_ref)
            dg_ref[...] = jnp.zeros_like(dg_ref)

        g = g_ref[...]
        hh, r, yv = _rms_fwd(h_ref[...], g)
        valid = (i > 0).astype(F32)
        err = (yv - t_ref[...]) * valid
        loss_ref[...] += 0.5 * jnp.sum(jnp.sum(err * err, axis=1, keepdims=True), axis=0, keepdims=True) / D
        dy = err / D
        dx, dg = _rms_bwd(hh, r, g, dy)
        dh_ref[...] = dx
        dg_ref[...] += dg

    return pl.pallas_call(
        kern, name="final_loss", grid=(nb,),
        in_specs=[pl.BlockSpec((BLK, D), lambda i: (i, 0)),
                  pl.BlockSpec((BLK, D), lambda i: (jnp.maximum(i - 1, 0), 0)), _const_spec((1, D))],
        out_specs=[pl.BlockSpec((BLK, D), lambda i: (i, 0)), _const_spec((8, 128)), _const_spec((1, D))],
        out_shape=[jax.ShapeDtypeStruct((LP, D), F32), jax.ShapeDtypeStruct((8, 128), F32),
                   jax.ShapeDtypeStruct((1, D), F32)],
        compiler_params=_params("arbitrary"),
    )(h, tgt, gf)


def _tall_tile(nrows, tm):
    t = nrows // 24
    return t if (t * 24 == nrows and t % 16 == 0 and t > tm) else tm


FF_HALF = DFF // 2


def ffn_forward(tag, h, g, wgT, wuT, wd, tm):
    def fwd(i, hv, g_ref, wg_ref, wu_ref, wd_ref):
        _, _, n = _rms_fwd(hv, g_ref[...])
        nb = n.astype(BF16)
        acc = hv
        Gs, Us, As = [], [], []
        for c in range(2):
            rs = slice(c * FF_HALF, (c + 1) * FF_HALF)
            G = _nt(nb, wg_ref[rs, :])
            U = _nt(nb, wu_ref[rs, :])
            A = (G * _sig(G) * U).astype(BF16)
            acc = acc + 0.5 * _nn(A, wd_ref[rs, :])
            Gs.append(G.astype(BF16))
            Us.append(U.astype(BF16))
            As.append(A)
        return (nb, jnp.concatenate(Gs, axis=1), jnp.concatenate(Us, axis=1), jnp.concatenate(As, axis=1), acc)

    n, G, U, A, h2 = rowcall(tag + "_fwd", fwd, [h], [g, wgT, wuT, wd],
                             [(D, BF16), (DFF, BF16), (DFF, BF16), (DFF, BF16), (D, F32)], tm=tm)
    return h2, (h, n, G, U, A)


def ffn_backward(tag, dh, saved, g, wgT, wuT, wd, tm, emit=None):
    h, n, G, U, A = saved

    def bwd(i, Gv, Uv, hv, dhv, g_ref, wg_ref, wu_ref, wd_ref):
        dyb = (0.5 * dhv).astype(BF16)
        dn = jnp.zeros((tm, D), F32)
        dGs, dUs = [], []
        for c in range(2):
            rs = slice(c * FF_HALF, (c + 1) * FF_HALF)
            dA = _nt(dyb, wd_ref[rs, :])
            Gf = Gv[:, rs].astype(F32)
            sg = _sig(Gf)
            dG = (dA * Uv[:, rs].astype(F32) * (sg * (1.0 + Gf * (1.0 - sg)))).astype(BF16)
            dU = (dA * (Gf * sg)).astype(BF16)
            dn = dn + _nn(dG, wg_ref[rs, :]) + _nn(dU, wu_ref[rs, :])
            dGs.append(dG)
            dUs.append(dU)
        gv = g_ref[...]
        hh, r, _ = _rms_fwd(hv, gv)
        dx, dg = _rms_bwd(hh, r, gv, dn)
        dx = jnp.where(_row_ok(i, tm), dx, 0.0)
        return dhv + dx, jnp.concatenate(dGs, axis=1), jnp.concatenate(dUs, axis=1), dyb, dg

    dh2, dG, dU, dyb, dg = rowcall(tag + "_bwd", bwd, [G, U, h, dh], [g, wgT, wuT, wd],
                                   [(D, F32), (DFF, BF16), (DFF, BF16), (D, BF16)], [(1, D)], tm=tm)
    dwd = tn_matmul(tag + "_dwd", A, dyb)
    if emit is not None:
        emit("wd", dwd)
    dwgT = tn_matmul(tag + "_dwg", dG, n)
    if emit is not None:
        emit("wgT", dwgT)
    dwuT = tn_matmul(tag + "_dwu", dU, n)
    if emit is not None:
        emit("wuT", dwuT)
    return dh2, dg, dwgT, dwuT, dwd


def _alibi_slope(head):
    return float(2.0 ** (-8.0 * (head + 1) / N_HEADS))


def _att_bias(n, nb):
    qi = lax.broadcasted_iota(jnp.int32, (BLK, 4 * BLK), 0)
    cj = lax.broadcasted_iota(jnp.int32, (BLK, 4 * BLK), 1)
    jb = cj - BLK
    dist = jnp.abs(qi + BLK - jb)
    kpos = (n - 1) * BLK + jb
    band_ok = (dist <= WIN) & (kpos >= BLK) & (kpos < nb * BLK)
    is_meta = cj < BLK
    ok = (is_meta & (cj >= PAD)) | (jnp.logical_not(is_meta) & band_ok)
    distf = jnp.where(is_meta, 0, dist).astype(F32)
    maskadd = jnp.where(ok, 0.0, NEG).astype(F32)
    distf4 = jnp.concatenate([distf] * QG, axis=0)
    mask4 = jnp.concatenate([maskadd] * QG, axis=0)
    return distf4, mask4


def _group_col(vals):
    rg = lax.broadcasted_iota(jnp.int32, (QG * BLK, 1), 0) // BLK
    col = jnp.full((QG * BLK, 1), vals[QG - 1], F32)
    for gq in range(QG - 2, -1, -1):
        col = jnp.where(rg == gq, vals[gq], col)
    return col


def _stack_heads(ref_or_val, kh):
    return jnp.concatenate(
        [ref_or_val[:, (kh * QG + gq) * HD:(kh * QG + gq + 1) * HD] for gq in range(QG)], axis=0)


def _stack_keys(km, kp, kc, kn, kh):
    sl = slice(kh * HD, (kh + 1) * HD)
    return jnp.concatenate([km[:, sl], kp[:, sl], kc[:, sl], kn[:, sl]], axis=0)


LOG2E = 1.4426950408889634
LN2 = 0.6931471805599453
QSCALE = SCALE * LOG2E


def _att_update_bias(bias_ref, n, nb):
    @pl.when((n <= 2) | (n == nb - 1))
    def _():
        distf4, mask4 = _att_bias(n, nb)
        for kh in range(N_KV):
            slope_col = _group_col([_alibi_slope(kh * QG + gq) * LOG2E for gq in range(QG)])
            bias_ref[kh] = mask4 - slope_col * distf4


def _att_exp(qs, kb, bias_ref, kh, sink_ref):
    sink_col = _group_col([sink_ref[kh * QG + gq] for gq in range(QG)]) * LOG2E
    s = _nt(qs, kb) + bias_ref[kh]
    m = jnp.maximum(jnp.max(s, axis=1, keepdims=True), sink_col)
    e = jnp.exp2(s - m)
    es = jnp.exp2(sink_col - m)
    inv = 1.0 / (jnp.sum(e, axis=1, keepdims=True) + es)
    return e, es, inv


def attention_forward(tag, q, k, v, sink):
    LP = q.shape[0]
    nb = LP // BLK

    def kern(sink_ref, q_ref, km_ref, kp_ref, kc_ref, kn_ref, vm_ref, vp_ref, vc_ref, vn_ref, o_ref, bias_ref):
        n = pl.program_id(0)
        _att_update_bias(bias_ref, n, nb)
        qv = q_ref[...]
        km, kp, kc, kn = km_ref[...], kp_ref[...], kc_ref[...], kn_ref[...]
        vm, vp, vc, vn = vm_ref[...], vp_ref[...], vc_ref[...], vn_ref[...]
        for kh in range(N_KV):
            qs = _stack_heads(qv, kh)
            kb = _stack_keys(km, kp, kc, kn, kh)
            vb = _stack_keys(vm, vp, vc, vn, kh)
            e, _, inv = _att_exp(qs, kb, bias_ref, kh, sink_ref)
            o = _nn(e.astype(BF16), vb) * inv
            for gq in range(QG):
                hcol = (kh * QG + gq) * HD
                o_ref[:, hcol:hcol + HD] = o[gq * BLK:(gq + 1) * BLK].astype(o_ref.dtype)

    def kvspec(dn):
        return pl.BlockSpec((BLK, N_KV * HD), lambda n: (jnp.clip(n + dn, 0, nb - 1), 0))

    meta_spec = pl.BlockSpec((BLK, N_KV * HD), lambda n: (0, 0))
    return pl.pallas_call(
        kern, name=tag + "_att_fwd", grid=(nb,),
        in_specs=[pl.BlockSpec(memory_space=pltpu.SMEM), pl.BlockSpec((BLK, D), lambda n: (n, 0)),
                  meta_spec, kvspec(-1), kvspec(0), kvspec(1), meta_spec, kvspec(-1), kvspec(0), kvspec(1)],
        out_specs=pl.BlockSpec((BLK, D), lambda n: (n, 0)),
        out_shape=jax.ShapeDtypeStruct((LP, D), BF16),
        scratch_shapes=[pltpu.VMEM((N_KV, QG * BLK, 4 * BLK), F32)], compiler_params=_params("arbitrary"),
    )(sink, q, k, k, k, k, v, v, v, v)


def attention_backward(tag, q, k, v, do, sink):
    LP = q.shape[0]
    nb = LP // BLK
    KW = N_KV * HD

    def kern(sink_ref, q_ref, do_ref, km_ref, kp_ref, kc_ref, kn_ref, vm_ref, vp_ref, vc_ref, vn_ref,
             dq_ref, dk_ref, dv_ref, dkm_ref, dvm_ref, dsink_ref, bias_ref, rk, rv, fk, fv):
        n = pl.program_id(0)

        @pl.when(n == 0)
        def _():
            dkm_ref[...] = jnp.zeros_like(dkm_ref)
            dvm_ref[...] = jnp.zeros_like(dvm_ref)
            dsink_ref[...] = jnp.zeros_like(dsink_ref)
            rk[...] = jnp.zeros_like(rk)
            rv[...] = jnp.zeros_like(rv)

        _att_update_bias(bias_ref, n, nb)

        @pl.when(n < nb)
        def _():
            qv, dov = q_ref[...], do_ref[...]
            km, kp, kc, kn = km_ref[...], kp_ref[...], kc_ref[...], kn_ref[...]
            vm, vp, vc, vn = vm_ref[...], vp_ref[...], vc_ref[...], vn_ref[...]
            lane = lax.broadcasted_iota(jnp.int32, (8, 128), 1)
            dsink = jnp.zeros((8, 128), F32)
            for kh in range(N_KV):
                qs = _stack_heads(qv, kh)
                dos = _stack_heads(dov, kh)
                kb = _stack_keys(km, kp, kc, kn, kh)
                vb = _stack_keys(vm, vp, vc, vn, kh)
                dp = _nt(dos, vb)
                e, es, inv = _att_exp(qs, kb, bias_ref, kh, sink_ref)
                delta = inv * jnp.sum(e * dp, axis=1, keepdims=True)
                dsu = (e * (dp - delta)).astype(BF16)
                dqs = _nn(dsu, kb) * (inv * SCALE)
                dkt = _tn((qs.astype(F32) * (inv * LN2)).astype(BF16), dsu)
                dvt = _tn((dos.astype(F32) * inv).astype(BF16), e.astype(BF16))
                dsk = -(es * inv * delta)
                for gq in range(QG):
                    hcol = (kh * QG + gq) * HD
                    dq_ref[:, hcol:hcol + HD] = dqs[gq * BLK:(gq + 1) * BLK].astype(dq_ref.dtype)
                    tot = jnp.sum(dsk[gq * BLK:(gq + 1) * BLK], axis=0, keepdims=True)
                    dsink = dsink + jnp.where(lane == kh * QG + gq, tot, 0.0)
                hs = slice(kh * HD, (kh + 1) * HD)
                dkm_ref[hs, :] += dkt[:, 0:BLK]
                dvm_ref[hs, :] += dvt[:, 0:BLK]
                for ring, fin, part in ((rk, fk, dkt), (rv, fv, dvt)):
                    fin[hs, :] = ring[0, hs, :] + part[:, BLK:2 * BLK]
                    ring[0, hs, :] = ring[1, hs, :] + part[:, 2 * BLK:3 * BLK]
                    ring[1, hs, :] = part[:, 3 * BLK:4 * BLK]
            dsink_ref[...] += dsink
            dk_ref[...] = fk[...].T.astype(dk_ref.dtype)
            dv_ref[...] = fv[...].T.astype(dv_ref.dtype)

        @pl.when(n == nb)
        def _():
            dk_ref[...] = rk[0].T.astype(dk_ref.dtype)
            dv_ref[...] = rv[0].T.astype(dv_ref.dtype)

    def kvspec(dn):
        return pl.BlockSpec((BLK, KW), lambda n: (jnp.clip(jnp.minimum(n, nb - 1) + dn, 0, nb - 1), 0))

    meta_spec = pl.BlockSpec((BLK, KW), lambda n: (0, 0))
    rowspec = pl.BlockSpec((BLK, D), lambda n: (jnp.minimum(n, nb - 1), 0))
    emit_spec = pl.BlockSpec((BLK, KW), lambda n: (jnp.clip(n - 1, 1, nb - 1), 0))
    dq, dk, dv, dkm, dvm, dsink = pl.pallas_call(
        kern, name=tag + "_att_bwd", grid=(nb + 1,),
        in_specs=[pl.BlockSpec(memory_space=pltpu.SMEM), rowspec, rowspec,
                  meta_spec, kvspec(-1), kvspec(0), kvspec(1), meta_spec, kvspec(-1), kvspec(0), kvspec(1)],
        out_specs=[rowspec, emit_spec, emit_spec, _const_spec((KW, BLK)), _const_spec((KW, BLK)),
                   _const_spec((8, 128))],
        out_shape=[jax.ShapeDtypeStruct((LP, D), BF16), jax.ShapeDtypeStruct((LP, KW), BF16),
                   jax.ShapeDtypeStruct((LP, KW), BF16), jax.ShapeDtypeStruct((KW, BLK), F32),
                   jax.ShapeDtypeStruct((KW, BLK), F32), jax.ShapeDtypeStruct((8, 128), F32)],
        scratch_shapes=[pltpu.VMEM((N_KV, QG * BLK, 4 * BLK), F32), pltpu.VMEM((2, KW, BLK), F32),
                        pltpu.VMEM((2, KW, BLK), F32), pltpu.VMEM((KW, BLK), F32), pltpu.VMEM((KW, BLK), F32)],
        compiler_params=_params("arbitrary"),
    )(sink, q, do, k, k, k, k, v, v, v, v)
    dk = lax.dynamic_update_slice(dk, dkm.T.astype(BF16), (0, 0))
    dv = lax.dynamic_update_slice(dv, dvm.T.astype(BF16), (0, 0))
    return dq, dk, dv, dsink


SCAN_LANES = 1024


def _scan_tile(xr, xi, cr, ci, a8, tab, seg, reverse):
    sub = lax.broadcasted_iota(jnp.int32, (8, SCAN_LANES), 0)
    for c0 in range(0, NST, SCAN_LANES):
        cs = pl.ds(c0, SCAN_LANES)
        ar = a8[0, :, cs]
        ai = a8[1, :, cs]

        def rows(j):
            jj = (seg - 1 - j) if reverse else j
            return pl.ds(jj * 8, 8)

        def step1(j, carry):
            vr, vi = carry
            rs = rows(j)
            nr = ar * vr - ai * vi + xr[rs, cs]
            ni = ar * vi + ai * vr + xi[rs, cs]
            xr[rs, cs] = nr
            xi[rs, cs] = ni
            return nr, ni

        zero = jnp.zeros((8, SCAN_LANES), F32)
        vr, vi = zero, zero
        for j in range(seg):
            vr, vi = step1(j, (vr, vi))
        for t, s in enumerate((1, 2, 4)):
            sh = (8 - s) if reverse else s
            sr = pltpu.roll(vr, sh, 0)
            si = pltpu.roll(vi, sh, 0)
            tr = tab[2 * t, :, cs]
            ti = tab[2 * t + 1, :, cs]
            vr, vi = vr + tr * sr - ti * si, vi + tr * si + ti * sr
        pr = tab[6, :, cs]
        pi = tab[7, :, cs]
        c_r = cr[:, cs]
        c_i = ci[:, cs]
        vr, vi = vr + pr * c_r - pi * c_i, vi + pr * c_i + pi * c_r
        edge = 7 if reverse else 0
        last = 0 if reverse else 7
        sh = 7 if reverse else 1
        in_r = jnp.where(sub == edge, c_r, pltpu.roll(vr, sh, 0))
        in_i = jnp.where(sub == edge, c_i, pltpu.roll(vi, sh, 0))
        cr[:, cs] = jnp.broadcast_to(vr[last:last + 1, :], (8, SCAN_LANES))
        ci[:, cs] = jnp.broadcast_to(vi[last:last + 1, :], (8, SCAN_LANES))

        def step2(j, carry):
            dr, di = carry
            rs = rows(j)
            ndr = ar * dr - ai * di
            ndi = ar * di + ai * dr
            xr[rs, cs] += ndr
            xi[rs, cs] += ndi
            return ndr, ndi

        dr, di = in_r, in_i
        for j in range(seg):
            dr, di = step2(j, (dr, di))


ST_T = 4 * SP * 2
CH_T = 128


def _load_segmented(ref, scr, seg):
    out = []
    for ct in range(4):
        scr[ct] = ref[:, ct * CH_T:(ct + 1) * CH_T]
        out.append(jnp.concatenate([scr[ct, pl.ds(j, 8, stride=seg), :] for j in range(seg)], axis=0))
    return out


def _store_segmented(ref, scr, vals, seg):
    for ct in range(4):
        for j in range(seg):
            scr[ct, pl.ds(j, 8, stride=seg), :] = vals[ct][8 * j:8 * j + 8]
        ref[:, ct * CH_T:(ct + 1) * CH_T] = scr[ct]


def ssm_dir_forward(tag, u, bpr, bpi, cpr, cpi, a8, tab, reverse, tm):
    LP = u.shape[0]
    nt = LP // tm
    seg = tm // 8

    def rix(i):
        return (nt - 1 - i) if reverse else i

    def kern(u_ref, bpr_ref, bpi_ref, cpr_ref, cpi_ref, a8_ref, tab_ref, xre_ref, xim_ref, y_ref,
             xr, xi, ys, cr, ci):
        i = pl.program_id(0)

        @pl.when(i == 0)
        def _():
            cr[...] = jnp.zeros_like(cr)
            ci[...] = jnp.zeros_like(ci)

        ub = _load_segmented(u_ref, ys, seg)
        for ct in range(4):
            uc = ub[ct].astype(BF16)
            xr[:, ct * ST_T:(ct + 1) * ST_T] = _nn(uc, bpr_ref[ct * CH_T:(ct + 1) * CH_T, :])
            xi[:, ct * ST_T:(ct + 1) * ST_T] = _nn(uc, bpi_ref[ct * CH_T:(ct + 1) * CH_T, :])
        _scan_tile(xr, xi, cr, ci, a8_ref, tab_ref, seg, reverse)
        xrb = xr[...].astype(BF16)
        xib = xi[...].astype(BF16)
        xre_ref[...] = xrb
        xim_ref[...] = xib
        yv = []
        for ct in range(4):
            ss = slice(ct * ST_T, (ct + 1) * ST_T)
            yv.append(_nn(xrb[:, ss], cpr_ref[ss, :]) - _nn(xib[:, ss], cpi_ref[ss, :]))
        _store_segmented(y_ref, ys, yv, seg)

    row = lambda w: pl.BlockSpec((tm, w), lambda i: (rix(i), 0))
    return pl.pallas_call(
        kern, name=tag, grid=(nt,),
        in_specs=[row(SW), _const_spec(bpr.shape), _const_spec(bpi.shape), _const_spec(cpr.shape),
                  _const_spec(cpi.shape), _const_spec(a8.shape), _const_spec(tab.shape)],
        out_specs=[row(NST), row(NST), row(SW)],
        out_shape=[jax.ShapeDtypeStruct((LP, NST), BF16), jax.ShapeDtypeStruct((LP, NST), BF16),
                   jax.ShapeDtypeStruct((LP, SW), F32)],
        scratch_shapes=[pltpu.VMEM((tm, NST), F32), pltpu.VMEM((tm, NST), F32), pltpu.VMEM((4, tm, CH_T), F32),
                        pltpu.VMEM((8, NST), F32), pltpu.VMEM((8, NST), F32)],
        compiler_params=_params("arbitrary"),
    )(u, bpr, bpi, cpr, cpi, a8, tab)


def ssm_dir_backward(tag, dy, xre, xim, u, bpr, bpi, cpr, cpi, a8_adj, tab_adj, reverse, tm):
    LP = u.shape[0]
    nt = LP // tm
    seg = tm // 8

    def rix(i):
        return (nt - 1 - i) if reverse else i

    def kern(dy_ref, xre_ref, xim_ref, u_ref, bpr_ref, bpi_ref, cpr_ref, cpi_ref, a8_ref, tab_ref,
             du_ref, gbr_ref, gbi_ref, gcr_ref, gci_ref, sr_ref, si_ref, lr, li, gr, gi, dus, cr, ci):
        i = pl.program_id(0)

        @pl.when(i == 0)
        def _():
            cr[...] = jnp.zeros_like(cr)
            ci[...] = jnp.zeros_like(ci)
            for r in (gbr_ref, gbi_ref, gcr_ref, gci_ref, sr_ref, si_ref):
                r[...] = jnp.zeros_like(r)

        dyb = [v.astype(BF16) for v in _load_segmented(dy_ref, dus, seg)]
        ub = [v.astype(BF16) for v in _load_segmented(u_ref, dus, seg)]
        for ct in range(4):
            ss = slice(ct * ST_T, (ct + 1) * ST_T)
            dc = dyb[ct]
            g_re = _nt(dc, cpr_ref[ss, :])
            g_im = -_nt(dc, cpi_ref[ss, :])
            lr[:, ss] = g_re
            li[:, ss] = g_im
            gr[:, ss] = g_re
            gi[:, ss] = g_im
        _scan_tile(lr, li, cr, ci, a8_ref, tab_ref, seg, reverse)
        lam_r = lr[...]
        lam_i = li[...]
        wr = lam_r - gr[...]
        wi = lam_i - gi[...]
        xr = xre_ref[...].astype(F32)
        xi = xim_ref[...].astype(F32)
        sr_ref[...] += jnp.sum(wr * xr + wi * xi, axis=0, keepdims=True)
        si_ref[...] += jnp.sum(wi * xr - wr * xi, axis=0, keepdims=True)
        lrb = lam_r.astype(BF16)
        lib = lam_i.astype(BF16)
        xrb = xre_ref[...]
        xib = xim_ref[...]
        duv = []
        for ct in range(4):
            ss = slice(ct * ST_T, (ct + 1) * ST_T)
            cs = slice(ct * CH_T, (ct + 1) * CH_T)
            duv.append(_nt(lrb[:, ss], bpr_ref[cs, :]) + _nt(lib[:, ss], bpi_ref[cs, :]))
            gbr_ref[cs, :] += _tn(ub[ct], lrb[:, ss])
            gbi_ref[cs, :] += _tn(ub[ct], lib[:, ss])
            gcr_ref[cs, :] += _tn(dyb[ct], xrb[:, ss])
            gci_ref[cs, :] -= _tn(dyb[ct], xib[:, ss])
        _store_segmented(du_ref, dus, duv, seg)

    row = lambda w: pl.BlockSpec((tm, w), lambda i: (rix(i), 0))
    acc = _const_spec((SW, ST_T))
    vec = _const_spec((1, NST))
    return pl.pallas_call(
        kern, name=tag, grid=(nt,),
        in_specs=[row(SW), row(NST), row(NST), row(SW), _const_spec(bpr.shape), _const_spec(bpi.shape),
                  _const_spec(cpr.shape), _const_spec(cpi.shape), _const_spec(a8_adj.shape),
                  _const_spec(tab_adj.shape)],
        out_specs=[row(SW), acc, acc, acc, acc, vec, vec],
        out_shape=[jax.ShapeDtypeStruct((LP, SW), F32)] + [jax.ShapeDtypeStruct((SW, ST_T), F32)] * 4
        + [jax.ShapeDtypeStruct((1, NST), F32)] * 2,
        scratch_shapes=[pltpu.VMEM((tm, NST), F32)] * 4 + [pltpu.VMEM((4, tm, CH_T), F32)]
        + [pltpu.VMEM((8, NST), F32)] * 2,
        compiler_params=_params("arbitrary"),
    )(dy, xre, xim, u, bpr, bpi, cpr, cpi, a8_adj, tab_adj)


def _ssm_disc(lam_re, lam_im, log_dt, b_re, b_im):
    dt = jnp.exp(log_dt)[:, None]
    mag = jnp.exp(lam_re * dt)
    a_re = mag * jnp.cos(lam_im * dt)
    a_im = mag * jnp.sin(lam_im * dt)
    den = lam_re * lam_re + lam_im * lam_im
    f_re = ((a_re - 1.0) * lam_re + a_im * lam_im) / den
    f_im = (a_im * lam_re - (a_re - 1.0) * lam_im) / den
    bb_re = f_re[:, :, None] * b_re - f_im[:, :, None] * b_im
    bb_im = f_re[:, :, None] * b_im + f_im[:, :, None] * b_re
    return a_re, a_im, bb_re, bb_im


def _scan_tables(lam_re, lam_im, log_dt, conj, reverse, seg):
    dt = jnp.exp(log_dt)[:, None]
    lr = (lam_re * dt).reshape(1, NST)
    li = (lam_im * dt).reshape(1, NST) * (-1.0 if conj else 1.0)
    t = jnp.arange(8, dtype=F32)[:, None]

    def power(kk):
        mag = jnp.exp(kk * lr)
        return mag * jnp.cos(kk * li), mag * jnp.sin(kk * li)

    ones = jnp.ones((8, 1), F32)
    a8 = jnp.stack(power(ones)).astype(F32)
    tabs = []
    for s in (1, 2, 4):
        mask = (t <= 7 - s) if reverse else (t >= s)
        pr, pi = power(float(s * seg) * ones)
        tabs += [jnp.where(mask, pr, 0.0), jnp.where(mask, pi, 0.0)]
    kk = ((8.0 - t) if reverse else (t + 1.0)) * float(seg)
    pr, pi = power(kk)
    tabs += [pr, pi]
    return a8, jnp.stack(tabs).astype(F32)


def _pack_b(bb):
    t = bb.transpose(0, 2, 1).reshape(4, 8, SCH, SP)
    eye = jnp.eye(8, dtype=bb.dtype)
    return jnp.einsum('tgcp,gh->tgchp', t, eye).reshape(SW, ST_T)


def _pack_c(cc):
    t = cc.transpose(0, 2, 1).reshape(4, 8, SP, SCH)
    eye = jnp.eye(8, dtype=cc.dtype)
    return jnp.einsum('tgpc,gh->tgphc', t, eye).reshape(NST, CH_T)


def _unpack_diag(acc):
    t = acc.reshape(4, 8, SCH, 8, SP)
    eye = jnp.eye(8, dtype=acc.dtype)
    return jnp.einsum('tgchp,gh->tgcp', t, eye).reshape(SGRP, SCH, SP)


def _gelu(y):
    k0 = math.sqrt(2.0 / math.pi)
    inner = k0 * (y + 0.044715 * y * y * y)
    th = jnp.tanh(inner)
    z = 0.5 * y * (1.0 + th)
    dz = 0.5 * (1.0 + th) + 0.5 * y * (1.0 - th * th) * k0 * (1.0 + 3.0 * 0.044715 * y * y)
    return z, dz


Q0, K0, V0, U0, GS0, GA0, IN_COLS = 0, 1024, 1280, 1536, 2048, 3072, 4096


def mixer_forward(tag, h, p, tm):
    g, winT, wglu, wbsT, wba, wout = p["g"], p["winT"], p["wglu"], p["wbsT"], p["wba"], p["wout"]

    def proj(i, hv, g_ref, w_ref):
        _, _, n = _rms_fwd(hv, g_ref[...])
        nb = n.astype(BF16)
        return (nb, _nt(nb, w_ref[Q0:K0, :]) * QSCALE, _nt(nb, w_ref[K0:V0, :]), _nt(nb, w_ref[V0:U0, :]),
                _nt(nb, w_ref[U0:GS0, :]), _nt(nb, w_ref[GS0:GA0, :]), _nt(nb, w_ref[GA0:IN_COLS, :]))

    n, q, k, v, u, gs, ga = rowcall(
        tag + "_proj", proj, [h], [g, winT],
        [(D, BF16), (D, BF16), (N_KV * HD, BF16), (N_KV * HD, BF16), (SW, F32), (D, BF16), (D, BF16)],
        tm=_tall_tile(h.shape[0], tm))

    ya = attention_forward(tag, q, k, v, p["sink"])

    states, ydir = [], []
    for dr in range(2):
        s = p["ssm"][dr]
        xre, xim, yd = ssm_dir_forward(f"{tag}_ssm_fwd{dr}", u, s["bpr"], s["bpi"], s["cpr"], s["cpi"],
                                       s["a8"], s["tab"], dr == 1, tm)
        states.append((xre, xim))
        ydir.append(yd)

    def merge(i, y0, y1, uv, yav, gsv, gav, hv, d_ref, wglu_ref, wbs_ref, wba_ref, wout_ref):
        ypre = y0 + y1 + d_ref[...] * uv
        z, _ = _gelu(ypre)
        zb = z.astype(BF16)
        t = _nn(zb, wglu_ref[...])
        ysb = (z * _sig(t)).astype(BF16)
        bs = _nt(ysb, wbs_ref[...])
        ba = _nn(yav, wba_ref[...])
        mg = _sig(gsv.astype(F32)) * bs + _sig(gav.astype(F32)) * ba
        mg = jnp.where(_row_ok(i, tm), mg, 0.0).astype(BF16)
        return ypre, zb, t, ysb, bs, ba, mg, hv + _nn(mg, wout_ref[...])

    ypre, zb, t, ys, bs, ba, mg, h2 = rowcall(
        tag + "_merge", merge, [ydir[0], ydir[1], u, ya, gs, ga, h], [p["d"], wglu, wbsT, wba, wout],
        [(SW, F32), (SW, BF16), (SW, F32), (SW, BF16), (D, BF16), (D, BF16), (D, BF16), (D, F32)], tm=tm)
    saved = dict(h=h, n=n, q=q, k=k, v=v, u=u, gs=gs, ga=ga, ya=ya, states=states, ypre=ypre, zb=zb, t=t,
                 ys=ys, bs=bs, ba=ba, mg=mg)
    return h2, saved


def mixer_backward(tag, dh, sv, p, tm):
    g, winT, wglu, wbsT, wba, wout = p["g"], p["winT"], p["wglu"], p["wbsT"], p["wba"], p["wout"]

    def y1(i, dhv, bsv, bav, gsv, gav, ypv, tv, uv, wout_ref, wbs_ref, wba_ref, d_ref, wglu_ref):
        dhb = dhv.astype(BF16)
        dmg = _nt(dhb, wout_ref[...])
        dmg = jnp.where(_row_ok(i, tm), dmg, 0.0)
        sgs = _sig(gsv.astype(F32))
        sga = _sig(gav.astype(F32))
        dbs = (dmg * sgs).astype(BF16)
        dba = (dmg * sga).astype(BF16)
        dgs = dmg * bsv.astype(F32) * sgs * (1.0 - sgs)
        dga = dmg * bav.astype(F32) * sga * (1.0 - sga)
        dys = _nn(dbs, wbs_ref[...])
        dya = _nt(dba, wba_ref[...])
        z, dz_dy = _gelu(ypv)
        st = _sig(tv)
        dt_ = dys * z * st * (1.0 - st)
        dz = dys * st + _nt(dt_.astype(BF16), wglu_ref[...])
        dyp = dz * dz_dy
        return (dbs, dba, dgs, dga, dhb, dya, dyp, dyp * d_ref[...], dt_,
                jnp.sum(dyp * uv, axis=0, keepdims=True))

    dbs, dba, dgs, dga, dhb, dya, dypb, du0, dtb, dd = rowcall(
        tag + "_bwd_merge", y1,
        [dh, sv["bs"], sv["ba"], sv["gs"], sv["ga"], sv["ypre"], sv["t"], sv["u"]],
        [wout, wbsT, wba, p["d"], wglu],
        [(D, BF16)] * 6 + [(SW, F32), (SW, F32), (SW, BF16)], [(1, SW)], tm=tm)
    dwout = tn_matmul(tag + "_dwout", sv["mg"], dhb)
    dwbsT = tn_matmul(tag + "_dwbs", dbs, sv["ys"])
    dwba = tn_matmul(tag + "_dwba", sv["ya"], dba)
    dwglu = tn_matmul(tag + "_dwglu", sv["zb"], dtb)

    du_dirs, ssm_sums = [], []
    for dr in range(2):
        s = p["ssm"][dr]
        xre, xim = sv["states"][dr]
        res = ssm_dir_backward(f"{tag}_ssm_bwd{dr}", dypb, xre, xim, sv["u"], s["bpr"], s["bpi"], s["cpr"],
                               s["cpi"], s["a8_adj"], s["tab_adj"], dr == 0, tm)
        du_dirs.append(res[0])
        ssm_sums.append(res[1:])

    dq, dk, dv, dsink = attention_backward(tag, sv["q"], sv["k"], sv["v"], dya, p["sink"])

    def x1b(i, dqv, dkv, dvv, du0v, du1v, du2v, dgsv, dgav, hv, dhv, g_ref, w_ref):
        dub = (du0v + du1v + du2v).astype(BF16)
        dn = (_nn(dqv, w_ref[Q0:K0, :]) + _nn(dkv, w_ref[K0:V0, :]) + _nn(dvv, w_ref[V0:U0, :])
              + _nn(dub, w_ref[U0:GS0, :]) + _nn(dgsv, w_ref[GS0:GA0, :]) + _nn(dgav, w_ref[GA0:IN_COLS, :]))
        gv = g_ref[...]
        hh, r, _ = _rms_fwd(hv, gv)
        dx, dg = _rms_bwd(hh, r, gv, dn)
        dx = jnp.where(_row_ok(i, tall), dx, 0.0)
        return dhv + dx, dub, dg

    tall = _tall_tile(dh.shape[0], tm)
    dh2, dub, dg = rowcall(tag + "_bwd_in", x1b,
                           [dq, dk, dv, du0, du_dirs[0], du_dirs[1], dgs, dga, sv["h"], dh], [g, winT],
                           [(D, F32), (SW, BF16)], [(1, D)], tm=tall)
    n = sv["n"]
    dwinT = jnp.concatenate([tn_matmul(f"{tag}_dwin{j}", piece, n)
                             for j, piece in enumerate((dq, dk, dv, dub, dgs, dga))], axis=0)
    grads = dict(g=dg, d=dd, sink=dsink, ssm=ssm_sums, winT=dwinT, wglu=dwglu, wbsT=dwbsT, wba=dwba, wout=dwout)
    return dh2, grads


W1024 = ("f1_wgT", "f1_wuT", "f1_wd", "winT", "wba", "wout", "f2_wgT", "f2_wuT", "f2_wd")
W512 = ("wglu", "wbsT")
PART_F1 = ("f1_wgT", "f1_wuT", "f1_wd")
PART_MIX = ("winT", "wba", "wout", "wglu", "wbsT")
PART_F2 = ("f2_wgT", "f2_wuT", "f2_wd")
PER_LAYER_SMALL = ("ffn1_norm", "mix_norm", "ffn2_norm", "ssm_lam_re", "ssm_lam_im", "ssm_log_dt",
                   "ssm_b_re", "ssm_b_im", "ssm_c_re", "ssm_c_im", "ssm_d", "attn_sink")
SMALL = ("ffn1_norm", "mix_norm", "ffn2_norm", "final_norm", "ssm_lam_re", "ssm_lam_im", "ssm_log_dt",
         "ssm_b_re", "ssm_b_im", "ssm_c_re", "ssm_c_im", "ssm_d", "attn_sink")


def kernel(x, meta_tokens, ffn1_norm, ffn1_w_gate, ffn1_w_up, ffn1_w_down, mix_norm, w_in, ssm_lam_re, ssm_lam_im, ssm_log_dt, ssm_b_re, ssm_b_im, ssm_c_re, ssm_c_im, ssm_d, ssm_w_glu, attn_sink, w_branch_ssm, w_branch_attn, w_out, ffn2_norm, ffn2_w_gate, ffn2_w_up, ffn2_w_down, final_norm, loss_target, m_meta_tokens, m_ffn1_norm, m_ffn1_w_gate, m_ffn1_w_up, m_ffn1_w_down, m_mix_norm, m_w_in, m_ssm_lam_re, m_ssm_lam_im, m_ssm_log_dt, m_ssm_b_re, m_ssm_b_im, m_ssm_c_re, m_ssm_c_im, m_ssm_d, m_ssm_w_glu, m_attn_sink, m_w_branch_ssm, m_w_branch_attn, m_w_out, m_ffn2_norm, m_ffn2_w_gate, m_ffn2_w_up, m_ffn2_w_down, m_final_norm, v_meta_tokens, v_ffn1_norm, v_ffn1_w_gate, v_ffn1_w_up, v_ffn1_w_down, v_mix_norm, v_w_in, v_ssm_lam_re, v_ssm_lam_im, v_ssm_log_dt, v_ssm_b_re, v_ssm_b_im, v_ssm_c_re, v_ssm_c_im, v_ssm_d, v_ssm_w_glu, v_attn_sink, v_w_branch_ssm, v_w_branch_attn, v_w_out, v_ffn2_norm, v_ffn2_w_gate, v_ffn2_w_up, v_ffn2_w_down, v_final_norm):
    weights = dict(meta_tokens=meta_tokens, ffn1_norm=ffn1_norm, ffn1_w_gate=ffn1_w_gate, ffn1_w_up=ffn1_w_up, ffn1_w_down=ffn1_w_down, mix_norm=mix_norm, w_in=w_in, ssm_lam_re=ssm_lam_re, ssm_lam_im=ssm_lam_im, ssm_log_dt=ssm_log_dt, ssm_b_re=ssm_b_re, ssm_b_im=ssm_b_im, ssm_c_re=ssm_c_re, ssm_c_im=ssm_c_im, ssm_d=ssm_d, ssm_w_glu=ssm_w_glu, attn_sink=attn_sink, w_branch_ssm=w_branch_ssm, w_branch_attn=w_branch_attn, w_out=w_out, ffn2_norm=ffn2_norm, ffn2_w_gate=ffn2_w_gate, ffn2_w_up=ffn2_w_up, ffn2_w_down=ffn2_w_down, final_norm=final_norm)
    mom_m = dict(meta_tokens=m_meta_tokens, ffn1_norm=m_ffn1_norm, ffn1_w_gate=m_ffn1_w_gate, ffn1_w_up=m_ffn1_w_up, ffn1_w_down=m_ffn1_w_down, mix_norm=m_mix_norm, w_in=m_w_in, ssm_lam_re=m_ssm_lam_re, ssm_lam_im=m_ssm_lam_im, ssm_log_dt=m_ssm_log_dt, ssm_b_re=m_ssm_b_re, ssm_b_im=m_ssm_b_im, ssm_c_re=m_ssm_c_re, ssm_c_im=m_ssm_c_im, ssm_d=m_ssm_d, ssm_w_glu=m_ssm_w_glu, attn_sink=m_attn_sink, w_branch_ssm=m_w_branch_ssm, w_branch_attn=m_w_branch_attn, w_out=m_w_out, ffn2_norm=m_ffn2_norm, ffn2_w_gate=m_ffn2_w_gate, ffn2_w_up=m_ffn2_w_up, ffn2_w_down=m_ffn2_w_down, final_norm=m_final_norm)
    mom_v = dict(meta_tokens=v_meta_tokens, ffn1_norm=v_ffn1_norm, ffn1_w_gate=v_ffn1_w_gate, ffn1_w_up=v_ffn1_w_up, ffn1_w_down=v_ffn1_w_down, mix_norm=v_mix_norm, w_in=v_w_in, ssm_lam_re=v_ssm_lam_re, ssm_lam_im=v_ssm_lam_im, ssm_log_dt=v_ssm_log_dt, ssm_b_re=v_ssm_b_re, ssm_b_im=v_ssm_b_im, ssm_c_re=v_ssm_c_re, ssm_c_im=v_ssm_c_im, ssm_d=v_ssm_d, ssm_w_glu=v_ssm_w_glu, attn_sink=v_attn_sink, w_branch_ssm=v_w_branch_ssm, w_branch_attn=v_w_branch_attn, w_out=v_w_out, ffn2_norm=v_ffn2_norm, ffn2_w_gate=v_ffn2_w_gate, ffn2_w_up=v_ffn2_w_up, ffn2_w_down=v_ffn2_w_down, final_norm=v_final_norm)
    names = list(weights)

    L0 = x.shape[1]
    LP = L0 + BLK
    tm = 384 if LP % 384 == 0 else BLK
    x_i, y_i, c_i = lax.axis_index("x"), lax.axis_index("y"), lax.axis_index("c")
    me = 4 * x_i + 2 * y_i + c_i

    def canon(l):
        return dict(
            f1_wgT=ffn1_w_gate[l].T, f1_wuT=ffn1_w_up[l].T, f1_wd=ffn1_w_down[l],
            winT=w_in[l].T, wba=w_branch_attn[l], wout=w_out[l],
            f2_wgT=ffn2_w_gate[l].T, f2_wuT=ffn2_w_up[l].T, f2_wd=ffn2_w_down[l],
            wglu=ssm_w_glu[l], wbsT=w_branch_ssm[l].T)

    shards = [{nm: a.astype(BF16) for nm, a in canon(l).items()} for l in range(DEPTH)]

    def rows_of(nm):
        return shards[0][nm].shape[0]

    def width_groups(names_):
        return [g for g in ([nm for nm in names_ if nm in W1024], [nm for nm in names_ if nm in W512]) if g]

    def pieces_for(group):
        out, off = [], 0
        for nm in group:
            out.append((off, rows_of(nm)))
            off += rows_of(nm)
        return out

    def start_gather(tag, l, names_):
        groups = width_groups(names_)
        packed = [jnp.concatenate([shards[l][nm] for nm in g], axis=0) for g in groups]
        return gather_layer_start(tag, packed, [pieces_for(g) for g in groups]), groups

    def finish_gather(tag, l, started_, after):
        handle, groups = started_
        dests = gather_layer_wait(tag, handle, len(groups), after)
        out = {}
        for nm, dest in zip([nm for g in groups for nm in g], dests):
            sh = shards[l][nm]
            out[nm] = lax.dynamic_update_slice(dest, sh, (me * sh.shape[0], 0))
        return out

    g1, gm = all_gather_pieces(
        "gather_weights_first",
        [(jnp.concatenate([shards[0][nm] for nm in PART_F1], axis=0), pieces_for(PART_F1)),
         (meta_tokens, [(0, N_META)])])
    first_weights = dict(zip(PART_F1, g1))
    meta_full = gm[0].reshape(NDEV, N_META, D // NDEV).transpose(1, 0, 2).reshape(N_META, D)
    gather_started = [start_gather("gather_start_l0", 0, PART_MIX + PART_F2)]
    gather_started += [start_gather(f"gather_start_l{l}", l, W1024 + W512) for l in range(1, DEPTH)]
    started = sum(st[0][3][0, 0] for st in gather_started)
    full = [None] * DEPTH

    def disc_all(lre, lim, ldt, bre, bim):
        return _ssm_disc(lre, lim, ldt, bre, bim)

    ssm_p, ssm_vjp = [], []
    for l in range(DEPTH):
        row, vrow = [], []
        for dr in range(2):
            args = (ssm_lam_re[l, dr], ssm_lam_im[l, dr], ssm_log_dt[l, dr], ssm_b_re[l, dr], ssm_b_im[l, dr])
            (a_re, a_im, bb_re, bb_im), vjp = jax.vjp(disc_all, *args)
            a8, tab = _scan_tables(args[0], args[1], args[2], False, dr == 1, tm // 8)
            a8_adj, tab_adj = _scan_tables(args[0], args[1], args[2], True, dr == 0, tm // 8)
            row.append(dict(
                bpr=_pack_b(bb_re).astype(BF16), bpi=_pack_b(bb_im).astype(BF16),
                cpr=_pack_c(ssm_c_re[l, dr]).astype(BF16), cpi=_pack_c(ssm_c_im[l, dr]).astype(BF16),
                a8=a8, tab=tab, a8_adj=a8_adj, tab_adj=tab_adj, a_re=a_re, a_im=a_im))
            vrow.append(vjp)
        ssm_p.append(row)
        ssm_vjp.append(vrow)

    blk0 = jnp.concatenate([jnp.zeros((PAD, D), F32), meta_full.astype(F32)], axis=0)
    h = build_h0(x[0], blk0)
    saved = []
    for l in range(DEPTH):
        w = dict(first_weights) if l == 0 else finish_gather(f"gather_wait_l{l}", l, gather_started[l], h)
        full[l] = w
        g1n, g2n = ffn1_norm[l][None, :], ffn2_norm[l][None, :]
        if l == 0:
            g1n = g1n + started
        h, s1 = ffn_forward("ffn1", h, g1n, w["f1_wgT"], w["f1_wuT"], w["f1_wd"], tm)
        if l == 0:
            w.update(finish_gather("gather_wait_l0", 0, gather_started[0], h))
        mp = dict(g=mix_norm[l][None, :], winT=w["winT"], wglu=w["wglu"], wbsT=w["wbsT"], wba=w["wba"],
                  wout=w["wout"], d=ssm_d[l][None, :], sink=attn_sink[l], ssm=ssm_p[l])
        h, s2 = mixer_forward("mix", h, mp, tm)
        h, s3 = ffn_forward("ffn2", h, g2n, w["f2_wgT"], w["f2_wuT"], w["f2_wd"], tm)
        saved.append((s1, s2, s3, mp, g1n, g2n))

    dh, loss_acc, dgf = final_loss(h, loss_target[0], final_norm[None, :])
    loss = lax.psum(loss_acc[0, 0], MESH_AXES)

    small = {nm: [None] * DEPTH for nm in SMALL if nm != "final_norm"}

    def start_scatter(tag, grads_d, names_):
        groups = width_groups(names_)
        mine = [jnp.concatenate([lax.dynamic_slice_in_dim(grads_d[nm], me * rows_of(nm), rows_of(nm), axis=0)
                                 for nm in g], axis=0) for g in groups]
        handle, nin = scatter_layer_start(tag + "_start", [[grads_d[nm] for nm in g] for g in groups])
        return dict(tag=tag, handle=handle, nin=nin, groups=groups, mine=mine)

    def finish_scatter(st, after):
        lands = scatter_layer_wait(st["tag"] + "_wait", st["handle"], st["nin"], len(st["groups"]), after)
        out = {}
        for land, mine, g in zip(lands, st["mine"], st["groups"]):
            land = lax.dynamic_update_slice(land, mine[None], (me, 0, 0))
            tot = sum_slots(f"sum_weight_grads_{land.shape[1]}x{land.shape[2]}", land)
            for nm, (off_, r) in zip(g, pieces_for(g)):
                out[nm] = tot[off_:off_ + r]
        return out

    scatters = []
    small_started = [None] * DEPTH
    small_len = sum(math.prod(weights[k].shape[1:]) for k in PER_LAYER_SMALL) + N_META * D
    small_rows = -(-small_len // (8 * D)) * 8
    sent = jnp.zeros((), F32)
    for l in reversed(range(DEPTH)):
        s1, s2, s3, mp, g1n, g2n = saved[l]
        w = full[l]
        dh, dg2, f2g, f2u, f2d = ffn_backward("ffn2", dh, s3, g2n + sent, w["f2_wgT"], w["f2_wuT"], w["f2_wd"], tm)
        st = start_scatter(f"scatter_l{l}_f2", dict(f2_wgT=f2g, f2_wuT=f2u, f2_wd=f2d), PART_F2)
        scatters.append((l, st))
        dh, mg = mixer_backward("mix", dh, s2, dict(mp, d=mp["d"] + st["handle"][3][0, 0]), tm)
        st = start_scatter(f"scatter_l{l}_mix", mg, PART_MIX)
        scatters.append((l, st))
        if l > 0:
            dh, dg1, f1g, f1u, f1d = ffn_backward("ffn1", dh, s1, g1n + st["handle"][3][0, 0],
                                                  w["f1_wgT"], w["f1_wuT"], w["f1_wd"], tm)
            st = start_scatter(f"scatter_l{l}_f1", dict(f1_wgT=f1g, f1_wuT=f1u, f1_wd=f1d), PART_F1)
            scatters.append((l, st))
            sent = st["handle"][3][0, 0]
        else:
            def emit(nm, arr, l=l):
                scatters.append((l, start_scatter(f"scatter_l{l}_f1_{nm}", {"f1_" + nm: arr}, ("f1_" + nm,))))

            dh, dg1, _, _, _ = ffn_backward("ffn1", dh, s1, g1n + st["handle"][3][0, 0],
                                            w["f1_wgT"], w["f1_wuT"], w["f1_wd"], tm, emit=emit)
        small["ffn1_norm"][l] = dg1[0]
        small["mix_norm"][l] = mg["g"][0]
        small["ffn2_norm"][l] = dg2[0]
        small["ssm_d"][l] = mg["d"][0]
        small["attn_sink"][l] = mg["sink"][0, :N_HEADS]
        per_dir = {k: [] for k in ("ssm_lam_re", "ssm_lam_im", "ssm_log_dt", "ssm_b_re", "ssm_b_im",
                                   "ssm_c_re", "ssm_c_im")}
        for dr in range(2):
            gbr, gbi, gcr, gci, s_re, s_im = mg["ssm"][dr]
            a_re, a_im = ssm_p[l][dr]["a_re"], ssm_p[l][dr]["a_im"]
            s_re = s_re.reshape(SGRP, SP)
            s_im = s_im.reshape(SGRP, SP)
            den = a_re * a_re + a_im * a_im
            ga_re = (s_re * a_re - s_im * a_im) / den
            ga_im = (s_re * a_im + s_im * a_re) / den
            glr, gli, gld, gbre, gbim = ssm_vjp[l][dr]((ga_re, ga_im, _unpack_diag(gbr).transpose(0, 2, 1),
                                                        _unpack_diag(gbi).transpose(0, 2, 1)))
            per_dir["ssm_lam_re"].append(glr)
            per_dir["ssm_lam_im"].append(gli)
            per_dir["ssm_log_dt"].append(gld)
            per_dir["ssm_b_re"].append(gbre)
            per_dir["ssm_b_im"].append(gbim)
            per_dir["ssm_c_re"].append(_unpack_diag(gcr))
            per_dir["ssm_c_im"].append(_unpack_diag(gci))
        for k, vlist in per_dir.items():
            small[k][l] = jnp.stack(vlist)
        vec = [small[k][l].reshape(-1) for k in PER_LAYER_SMALL]
        if l == DEPTH - 1:
            vec.append(dgf[0])
        if l == 0:
            vec.append(dh[PAD:BLK].reshape(-1))
        used = sum(v.shape[0] for v in vec)
        flat = jnp.concatenate(vec + [jnp.zeros((small_rows * D - used,), F32)]).reshape(small_rows, D)
        small_started[l] = (exchange_start(
            f"small_grads_l{l}_start", [flat, jnp.zeros((NDEV, small_rows, D), F32)], 1,
            lambda refs, me_i, p_i: [(refs[0], refs[1].at[me_i], 0)]), flat)
        sent = sent + small_started[l][0][3][0, 0]

    grad_x = dh[BLK:][None]

    grads = {k: [None] * DEPTH for k in PER_LAYER_SMALL}
    for l in reversed(range(DEPTH)):
        (s_sems, r_sems, arrays, _), flat = small_started[l]
        land = exchange_wait(f"small_grads_l{l}_wait", s_sems, r_sems, arrays, 1, lambda refs, gi: refs[0], dh)[1]
        land = lax.dynamic_update_slice(land, flat[None], (me, 0, 0))
        tot = sum_slots("sum_small_grads", land).reshape(-1)
        o = 0
        for k in PER_LAYER_SMALL:
            shp = weights[k].shape[1:]
            sz = math.prod(shp)
            grads[k][l] = tot[o:o + sz].reshape(shp)
            o += sz
        if l == DEPTH - 1:
            final_norm_grad = tot[o:o + D]
        if l == 0:
            dmeta_full = tot[o:o + N_META * D].reshape(N_META, D)
    grads = {k: jnp.stack(vv) for k, vv in grads.items()}
    grads["final_norm"] = final_norm_grad
    grads["meta_tokens"] = lax.dynamic_slice_in_dim(dmeta_full, me * (D // NDEV), D // NDEV, axis=1)

    own = [dict() for _ in range(DEPTH)]
    for l, st in scatters:
        own[l].update(finish_scatter(st, dh))

    def stack(fn):
        return jnp.stack([fn(own[l]) for l in range(DEPTH)])

    grads["ffn1_w_gate"] = stack(lambda d: d["f1_wgT"].T)
    grads["ffn1_w_up"] = stack(lambda d: d["f1_wuT"].T)
    grads["ffn1_w_down"] = stack(lambda d: d["f1_wd"])
    grads["w_in"] = stack(lambda d: d["winT"].T)
    grads["ssm_w_glu"] = stack(lambda d: d["wglu"])
    grads["w_branch_ssm"] = stack(lambda d: d["wbsT"].T)
    grads["w_branch_attn"] = stack(lambda d: d["wba"])
    grads["w_out"] = stack(lambda d: d["wout"])
    grads["ffn2_w_gate"] = stack(lambda d: d["f2_wgT"].T)
    grads["ffn2_w_up"] = stack(lambda d: d["f2_wuT"].T)
    grads["ffn2_w_down"] = stack(lambda d: d["f2_wd"])

    deltas, new_m, new_v = {}, {}, {}
    for nm in names:
        deltas[nm], new_m[nm], new_v[nm] = adamw("adamw_" + nm, weights[nm], grads[nm], mom_m[nm], mom_v[nm])

    return (loss, grad_x, *[grads[n] for n in names], *[deltas[n] for n in names],
            *[new_m[n] for n in names], *[new_v[n] for n in names])
```

```python
import functools
import math

import jax
import jax.numpy as jnp
from jax import lax
from jax.experimental import pallas as pl
from jax.experimental.pallas import tpu as pltpu

F32 = jnp.float32
BF16 = jnp.bfloat16

D = 1024
DFF = 2816
N_META = 16
N_HEADS = 16
N_KV = 4
HD = 64
QG = 4
WIN = 128
BLK = 128
PAD = BLK - N_META
SW = 512
SGRP = 32
SCH = 16
SP = 64
NST = SGRP * SP
EPS = 1e-6
NEG = -1e30
SCALE = HD ** -0.5
NDEV = 8
DEPTH = 4
MESH_AXES = ("x", "y", "c")
MESH = pl.DeviceIdType.MESH

ADAM_LR = 0.001
ADAM_B1 = 0.9
ADAM_B2 = 0.999
ADAM_EPS = 1e-08
ADAM_WD = 0.01
ADAM_STEP = 10

VMEM_LIMIT = 56 * 1024 * 1024


def _params(*sem):
    return pltpu.CompilerParams(dimension_semantics=sem, vmem_limit_bytes=VMEM_LIMIT)


def _nn(a, b):
    return lax.dot_general(a, b, (((1,), (0,)), ((), ())), preferred_element_type=F32)


def _nt(a, b):
    return lax.dot_general(a, b, (((1,), (1,)), ((), ())), preferred_element_type=F32)


def _tn(a, b):
    return lax.dot_general(a, b, (((0,), (0,)), ((), ())), preferred_element_type=F32)


def _sig(x):
    return 0.5 * jnp.tanh(0.5 * x) + 0.5


def _rms_fwd(h, g):
    r = lax.rsqrt(jnp.mean(h * h, axis=-1, keepdims=True) + EPS)
    hh = h * r
    return hh, r, hh * g


def _rms_bwd(hh, r, g, dn):
    dhh = dn * g
    dx = r * (dhh - hh * jnp.mean(dhh * hh, axis=-1, keepdims=True))
    return dx, jnp.sum(dn * hh, axis=0, keepdims=True)


def _row_ok(i, tm):
    rows = i * tm + lax.broadcasted_iota(jnp.int32, (tm, 1), 0)
    return rows >= PAD


def _const_spec(shape, single=False):
    nd = len(shape)
    if single:
        return pl.BlockSpec(shape, lambda *_: (0,) * nd, pipeline_mode=pl.Buffered(1))
    return pl.BlockSpec(shape, lambda *_: (0,) * nd)


def rowcall(name, body, rows, consts, outs, accs=(), *, tm):
    nrows = rows[0].shape[0]
    nt = nrows // tm
    assert nt * tm == nrows, (name, nrows, tm)
    nr, nc, no, na = len(rows), len(consts), len(outs), len(accs)
    in_specs = [pl.BlockSpec((tm, r.shape[1]), lambda i: (i, 0)) for r in rows]
    in_specs += [_const_spec(c.shape, single=True) for c in consts]
    out_shape = [jax.ShapeDtypeStruct((nrows, w), dt) for (w, dt) in outs]
    out_specs = [pl.BlockSpec((tm, w), lambda i: (i, 0)) for (w, dt) in outs]
    out_shape += [jax.ShapeDtypeStruct(s, F32) for s in accs]
    out_specs += [_const_spec(s) for s in accs]

    def kern(*refs):
        i = pl.program_id(0)
        row_vals = [r[...] for r in refs[:nr]]
        res = body(i, *row_vals, *refs[nr:nr + nc])
        out_refs = refs[nr + nc:nr + nc + no]
        acc_refs = refs[nr + nc + no:]
        for r, v in zip(out_refs, res[:no]):
            r[...] = v.astype(r.dtype)
        if na:
            @pl.when(i == 0)
            def _():
                for r in acc_refs:
                    r[...] = jnp.zeros_like(r)
            for r, v in zip(acc_refs, res[no:]):
                r[...] += v

    res = pl.pallas_call(
        kern, name=name, grid=(nt,), in_specs=in_specs, out_specs=out_specs, out_shape=out_shape,
        compiler_params=_params("arbitrary"),
    )(*rows, *consts)
    return res


def tn_matmul(name, lhs, rhs, scale=1.0):
    M, K = lhs.shape
    N = rhs.shape[1]
    assert lhs.dtype == BF16 and rhs.dtype == BF16
    nm = 6
    tmw = M // nm
    assert tmw * nm == M and tmw % 16 == 0
    tk = 1408 if (K % 1408 == 0) else K
    nk = K // tk

    def kern(a_ref, b_ref, o_ref, acc):
        m = pl.program_id(1)
        part = _tn(a_ref[...], b_ref[...])

        @pl.when(m == 0)
        def _():
            acc[...] = part

        @pl.when((m > 0) & (m < nm - 1))
        def _():
            acc[...] += part

        @pl.when(m == nm - 1)
        def _():
            o_ref[...] = ((acc[...] + part) * scale).astype(o_ref.dtype)

    return pl.pallas_call(
        kern, name=name, grid=(nk, nm),
        in_specs=[pl.BlockSpec((tmw, tk), lambda k, m: (m, k)), pl.BlockSpec((tmw, N), lambda k, m: (m, 0))],
        out_specs=pl.BlockSpec((tk, N), lambda k, m: (k, 0)),
        out_shape=jax.ShapeDtypeStruct((K, N), BF16),
        scratch_shapes=[pltpu.VMEM((tk, N), F32)],
        compiler_params=_params("arbitrary", "arbitrary"),
    )(lhs, rhs)


def _mesh_pos():
    x, y, c = lax.axis_index("x"), lax.axis_index("y"), lax.axis_index("c")
    return x, y, c


def all_gather_pieces(name, groups):
    ng = len(groups)
    packed = [g[0] for g in groups]
    pieces = [g[1] for g in groups]
    out_shape, out_map = [], []
    for gi, (p, pcs) in enumerate(groups):
        idx = []
        for (off, r) in pcs:
            idx.append(len(out_shape))
            out_shape.append(jax.ShapeDtypeStruct((NDEV * r, p.shape[1]), p.dtype))
        out_map.append(idx)
    nout = len(out_shape)

    def body(*refs):
        p_refs = refs[:ng]
        o_refs = refs[ng:ng + nout]
        send_sems, recv_sems, local_sems = refs[ng + nout:]
        x, y, c = _mesh_pos()
        me = (x, y, c)
        sibling = (x, y, 1 - c)
        chips = [(1 - x, y), (x, 1 - y), (1 - x, 1 - y)]

        def blk(px, py, pc):
            return 4 * px + 2 * py + pc

        def copies(gi, k, origin, to, from_out):
            cps = []
            for (off, r), oi in zip(pieces[gi], out_map[gi]):
                dst = o_refs[oi].at[pl.ds(origin * r, r), :]
                src = dst if from_out else p_refs[gi].at[pl.ds(off, r), :]
                cps.append(pltpu.make_async_remote_copy(
                    src_ref=src, dst_ref=dst, send_sem=send_sems.at[gi, k], recv_sem=recv_sems.at[gi, k],
                    device_id=to, device_id_type=MESH))
            return cps

        def whole(gi, k):
            return pltpu.make_async_remote_copy(
                src_ref=p_refs[gi], dst_ref=p_refs[gi], send_sem=send_sems.at[gi, k],
                recv_sem=recv_sems.at[gi, k], device_id=me, device_id_type=MESH)

        mine = []
        for gi in range(ng):
            for (off, r), oi in zip(pieces[gi], out_map[gi]):
                mine.append(pltpu.make_async_copy(
                    p_refs[gi].at[pl.ds(off, r), :], o_refs[oi].at[pl.ds(blk(*me) * r, r), :],
                    local_sems.at[gi]))
        for cp in mine:
            cp.start()
        for gi in range(ng):
            for cp in copies(gi, 0, blk(*me), sibling, False):
                cp.start()
            for j, chip in enumerate(chips):
                for cp in copies(gi, 1 + j, blk(*me), (*chip, c), False):
                    cp.start()
        for j, chip in enumerate(chips):
            for gi in range(ng):
                whole(gi, 1 + j).wait_recv()
                for cp in copies(gi, 4 + j, blk(*chip, c), sibling, True):
                    cp.start()
        for gi in range(ng):
            whole(gi, 0).wait_recv()
            for j in range(3):
                whole(gi, 4 + j).wait_recv()
        for gi in range(ng):
            for k in range(7):
                whole(gi, k).wait_send()
            pltpu.make_async_copy(p_refs[gi], p_refs[gi], local_sems.at[gi]).wait()

    any_spec = pl.BlockSpec(memory_space=pl.ANY)
    outs = pl.pallas_call(
        body, name=name, out_shape=out_shape,
        in_specs=[any_spec] * ng, out_specs=[any_spec] * nout,
        scratch_shapes=[pltpu.SemaphoreType.DMA((ng, 7)), pltpu.SemaphoreType.DMA((ng, 7)),
                        pltpu.SemaphoreType.DMA((ng,))],
    )(*packed)
    return [[outs[oi] for oi in idx] for idx in out_map]


HBM_SPEC = pl.BlockSpec(memory_space=pltpu.HBM)
SEM_SPEC = pl.BlockSpec(memory_space=pltpu.SEMAPHORE)
DATAFLOW = pltpu.SideEffectType.DATAFLOW_SIDE_EFFECTING


def _peers(x, y, c):
    return [(x, y, 1 - c), (1 - x, y, c), (x, 1 - y, c), (1 - x, 1 - y, c),
            (1 - x, y, 1 - c), (x, 1 - y, 1 - c), (1 - x, 1 - y, 1 - c)]


def exchange_start(name, arrays, ng, plan):
    n = len(arrays)
    ns = ng * 7

    def body(*refs):
        in_refs = refs[:n]
        send_sems, recv_sems = refs[n:n + ns], refs[n + ns:n + 2 * ns]
        token = refs[-1]
        x, y, c = _mesh_pos()
        me_i = 4 * x + 2 * y + c
        for k, peer in enumerate(_peers(x, y, c)):
            p_i = 4 * peer[0] + 2 * peer[1] + peer[2]
            for src, dst, gi in plan(in_refs, me_i, p_i):
                pltpu.make_async_remote_copy(
                    src_ref=src, dst_ref=dst, send_sem=send_sems[gi * 7 + k], recv_sem=recv_sems[gi * 7 + k],
                    device_id=peer, device_id_type=MESH).start()
        token[...] = jnp.zeros_like(token)

    res = pl.pallas_call(
        body, name=name,
        out_shape=(*[pltpu.SemaphoreType.DMA(())] * (2 * ns),
                   *[pltpu.HBM(a.shape, a.dtype) for a in arrays], jax.ShapeDtypeStruct((8, 128), F32)),
        in_specs=[HBM_SPEC] * n,
        out_specs=(*[SEM_SPEC] * (2 * ns), *[HBM_SPEC] * n, pl.BlockSpec(memory_space=pltpu.VMEM)),
        input_output_aliases={i: 2 * ns + i for i in range(n)},
        compiler_params=pltpu.CompilerParams(has_side_effects=DATAFLOW),
    )(*[pltpu.with_memory_space_constraint(a, pltpu.HBM) for a in arrays])
    return list(res[:ns]), list(res[ns:2 * ns]), list(res[2 * ns:2 * ns + n]), res[-1]


def exchange_wait(name, send_sems, recv_sems, arrays, ng, sized, after):
    n = len(arrays)
    ns = ng * 7

    def body(*refs):
        in_refs = refs[:n]
        s_sems, r_sems = refs[n:n + ns], refs[n + ns:n + 2 * ns]
        x, y, c = _mesh_pos()
        for gi in range(ng):
            view = sized(in_refs, gi)
            for k in range(7):
                w = pltpu.make_async_remote_copy(
                    src_ref=view, dst_ref=view, send_sem=s_sems[gi * 7 + k], recv_sem=r_sems[gi * 7 + k],
                    device_id=(x, y, c), device_id_type=MESH)
                w.wait_send()
                w.wait_recv()

    res = pl.pallas_call(
        body, name=name, out_shape=tuple(pltpu.HBM(a.shape, a.dtype) for a in arrays),
        in_specs=[HBM_SPEC] * n + [SEM_SPEC] * (2 * ns) + [pl.BlockSpec(memory_space=pl.ANY)],
        out_specs=tuple([HBM_SPEC] * n), input_output_aliases={i: i for i in range(n)},
        compiler_params=pltpu.CompilerParams(has_side_effects=DATAFLOW),
    )(*arrays, *send_sems, *recv_sems, after)
    return list(res)


def gather_layer_start(name, packed, pieces):
    ng = len(packed)
    dests = [lax.empty((NDEV * r, p.shape[1]), p.dtype) for p, pcs in zip(packed, pieces) for (_, r) in pcs]

    def plan(refs, me_i, p_i):
        out, di = [], ng
        for gi in range(ng):
            for (off, r) in pieces[gi]:
                out.append((refs[gi].at[pl.ds(off, r), :], refs[di].at[pl.ds(me_i * r, r), :], gi))
                di += 1
        return out

    return exchange_start(name, list(packed) + dests, ng, plan)


def gather_layer_wait(name, handle, ng, after):
    send_sems, recv_sems, arrays, _ = handle
    out = exchange_wait(name, send_sems, recv_sems, arrays, ng, lambda refs, gi: refs[gi], after)
    return out[ng:]


def scatter_layer_start(name, groups):
    ng = len(groups)
    flat = [a for arrs in groups for a in arrs]
    offs, lands = [], []
    for arrs in groups:
        o, off = [], 0
        for a in arrs:
            r = a.shape[0] // NDEV
            o.append((off, r))
            off += r
        offs.append(o)
        lands.append(jnp.zeros((NDEV, off, arrs[0].shape[1]), arrs[0].dtype))
    nin = len(flat)

    def plan(refs, me_i, p_i):
        out, ai = [], 0
        for gi in range(ng):
            for (off, r) in offs[gi]:
                out.append((refs[ai].at[pl.ds(p_i * r, r), :], refs[nin + gi].at[me_i, pl.ds(off, r), :], gi))
                ai += 1
        return out

    return exchange_start(name, flat + lands, ng, plan), nin


def scatter_layer_wait(name, handle, nin, ng, after):
    send_sems, recv_sems, arrays, _ = handle
    out = exchange_wait(name, send_sems, recv_sems, arrays, ng, lambda refs, gi: refs[nin + gi].at[0], after)
    return out[nin:]


def _pick_tile(n, cap):
    best = None
    for t in range(8, min(n, cap) + 1, 8):
        if n % t == 0:
            best = t
    return best if best is not None else n


def sum_slots(name, land):
    _, R, W = land.shape
    tr = _pick_tile(R, 512)

    def kern(l_ref, o_ref):
        acc = l_ref[0].astype(F32)
        for s in range(1, NDEV):
            acc = acc + l_ref[s].astype(F32)
        o_ref[...] = acc

    return pl.pallas_call(
        kern, name=name, grid=(R // tr,),
        in_specs=[pl.BlockSpec((NDEV, tr, W), lambda i: (0, i, 0))],
        out_specs=pl.BlockSpec((tr, W), lambda i: (i, 0)),
        out_shape=jax.ShapeDtypeStruct((R, W), F32),
        compiler_params=_params("arbitrary"),
    )(land)


def adamw(name, w, g, m, v):
    shp = w.shape
    C = shp[-1]
    R = max(1, math.prod(shp[:-1]))
    tr = _pick_tile(R, 1024)
    w2, g2, m2, v2 = (a.reshape(R, C) for a in (w, g, m, v))

    def kern(w_ref, g_ref, m_ref, v_ref, d_ref, mo_ref, vo_ref):
        gg = g_ref[...]
        mn = ADAM_B1 * m_ref[...] + (1.0 - ADAM_B1) * gg
        vn = ADAM_B2 * v_ref[...] + (1.0 - ADAM_B2) * jnp.square(gg)
        m_hat = mn / (1.0 - ADAM_B1 ** ADAM_STEP)
        v_hat = vn / (1.0 - ADAM_B2 ** ADAM_STEP)
        d_ref[...] = -ADAM_LR * (m_hat / (jnp.sqrt(v_hat) + ADAM_EPS) + ADAM_WD * w_ref[...])
        mo_ref[...] = mn
        vo_ref[...] = vn

    spec = pl.BlockSpec((tr, C), lambda i: (i, 0))
    d, mo, vo = pl.pallas_call(
        kern, name=name, grid=(R // tr,), in_specs=[spec] * 4, out_specs=[spec] * 3,
        out_shape=[jax.ShapeDtypeStruct((R, C), F32)] * 3, compiler_params=_params("arbitrary"),
    )(w2, g2, m2, v2)
    return d.reshape(shp), mo.reshape(shp), vo.reshape(shp)


def build_h0(x2, blk0):
    L0 = x2.shape[0]
    nb = L0 // BLK + 1

    def kern(x_ref, b_ref, o_ref):
        i = pl.program_id(0)

        @pl.when(i == 0)
        def _():
            o_ref[...] = b_ref[...]

        @pl.when(i > 0)
        def _():
            o_ref[...] = x_ref[...]

    return pl.pallas_call(
        kern, name="build_h0", grid=(nb,),
        in_specs=[pl.BlockSpec((BLK, D), lambda i: (jnp.maximum(i - 1, 0), 0)), _const_spec((BLK, D))],
        out_specs=pl.BlockSpec((BLK, D), lambda i: (i, 0)),
        out_shape=jax.ShapeDtypeStruct((L0 + BLK, D), F32), compiler_params=_params("arbitrary"),
    )(x2, blk0)


def final_loss(h, tgt, gf):
    LP = h.shape[0]
    nb = LP // BLK

    def kern(h_ref, t_ref, g_ref, dh_ref, loss_ref, dg_ref):
        i = pl.program_id(0)

        @pl.when(i == 0)
        def _():
            loss_ref[...] = jnp.zeros_like(loss_ref)
            dg_ref[...] = jnp.zeros_like(dg_ref)

        g = g_ref[...]
        hh, r, yv = _rms_fwd(h_ref[...], g)
        valid = (i > 0).astype(F32)
        err = (yv - t_ref[...]) * valid
        loss_ref[...] += 0.5 * jnp.sum(jnp.sum(err * err, axis=1, keepdims=True), axis=0, keepdims=True) / D
        dy = err / D
        dx, dg = _rms_bwd(hh, r, g, dy)
        dh_ref[...] = dx
        dg_ref[...] += dg

    return pl.pallas_call(
        kern, name="final_loss", grid=(nb,),
        in_specs=[pl.BlockSpec((BLK, D), lambda i: (i, 0)),
                  pl.BlockSpec((BLK, D), lambda i: (jnp.maximum(i - 1, 0), 0)), _const_spec((1, D))],
        out_specs=[pl.BlockSpec((BLK, D), lambda i: (i, 0)), _const_spec((8, 128)), _const_spec((1, D))],
        out_shape=[jax.ShapeDtypeStruct((LP, D), F32), jax.ShapeDtypeStruct((8, 128), F32),
                   jax.ShapeDtypeStruct((1, D), F32)],
        compiler_params=_params("arbitrary"),
    )(h, tgt, gf)


def _tall_tile(nrows, tm):
    t = nrows // 24
    return t if (t * 24 == nrows and t % 16 == 0 and t > tm) else tm


def ffn_forward(tag, h, g, wgT, wuT, wd, tm):
    def f1(i, hv, g_ref, wg_ref, wu_ref):
        _, _, n = _rms_fwd(hv, g_ref[...])
        nb = n.astype(BF16)
        G = _nt(nb, wg_ref[...])
        U = _nt(nb, wu_ref[...])
        A = G * _sig(G) * U
        return nb, G, U, A

    n, G, U, A = rowcall(tag + "_up", f1, [h], [g, wgT, wuT],
                         [(D, BF16), (DFF, BF16), (DFF, BF16), (DFF, BF16)], tm=tm)

    def f2(i, av, hv, wd_ref):
        return (hv + 0.5 * _nn(av, wd_ref[...]),)

    (h2,) = rowcall(tag + "_down", f2, [A, h], [wd], [(D, F32)], tm=_tall_tile(h.shape[0], tm))
    return h2, (h, n, G, U, A)


def ffn_backward(tag, dh, saved, g, wgT, wuT, wd, tm, emit=None):
    h, n, G, U, A = saved

    def b1(i, dhv, wd_ref):
        dyb = (0.5 * dhv).astype(BF16)
        return _nt(dyb, wd_ref[...]), dyb

    dA, dyb = rowcall(tag + "_bwd_act", b1, [dh], [wd], [(DFF, BF16), (D, BF16)], tm=_tall_tile(h.shape[0], tm))
    dwd = tn_matmul(tag + "_dwd", A, dyb)
    if emit is not None:
        emit("wd", dwd)

    def b2(i, dAv, Gv, Uv, hv, dhv, g_ref, wg_ref, wu_ref):
        dAf = dAv.astype(F32)
        Gf = Gv.astype(F32)
        sg = _sig(Gf)
        dG = (dAf * Uv.astype(F32) * (sg * (1.0 + Gf * (1.0 - sg)))).astype(BF16)
        dU = (dAf * (Gf * sg)).astype(BF16)
        dn = _nn(dG, wg_ref[...]) + _nn(dU, wu_ref[...])
        gv = g_ref[...]
        hh, r, _ = _rms_fwd(hv, gv)
        dx, dg = _rms_bwd(hh, r, gv, dn)
        dx = jnp.where(_row_ok(i, tm), dx, 0.0)
        return dhv + dx, dG, dU, dg

    dh2, dG, dU, dg = rowcall(tag + "_bwd_in", b2, [dA, G, U, h, dh], [g, wgT, wuT],
                              [(D, F32), (DFF, BF16), (DFF, BF16)], [(1, D)], tm=tm)
    dwgT = tn_matmul(tag + "_dwg", dG, n)
    if emit is not None:
        emit("wgT", dwgT)
    dwuT = tn_matmul(tag + "_dwu", dU, n)
    if emit is not None:
        emit("wuT", dwuT)
    return dh2, dg, dwgT, dwuT, dwd


def _alibi_slope(head):
    return float(2.0 ** (-8.0 * (head + 1) / N_HEADS))


def _att_bias(n, nb):
    qi = lax.broadcasted_iota(jnp.int32, (BLK, 4 * BLK), 0)
    cj = lax.broadcasted_iota(jnp.int32, (BLK, 4 * BLK), 1)
    jb = cj - BLK
    dist = jnp.abs(qi + BLK - jb)
    kpos = (n - 1) * BLK + jb
    band_ok = (dist <= WIN) & (kpos >= BLK) & (kpos < nb * BLK)
    is_meta = cj < BLK
    ok = (is_meta & (cj >= PAD)) | (jnp.logical_not(is_meta) & band_ok)
    distf = jnp.where(is_meta, 0, dist).astype(F32)
    maskadd = jnp.where(ok, 0.0, NEG).astype(F32)
    distf4 = jnp.concatenate([distf] * QG, axis=0)
    mask4 = jnp.concatenate([maskadd] * QG, axis=0)
    return distf4, mask4


def _group_col(vals):
    rg = lax.broadcasted_iota(jnp.int32, (QG * BLK, 1), 0) // BLK
    col = jnp.full((QG * BLK, 1), vals[QG - 1], F32)
    for gq in range(QG - 2, -1, -1):
        col = jnp.where(rg == gq, vals[gq], col)
    return col


def _stack_heads(ref_or_val, kh):
    return jnp.concatenate(
        [ref_or_val[:, (kh * QG + gq) * HD:(kh * QG + gq + 1) * HD] for gq in range(QG)], axis=0)


def _stack_keys(km, kp, kc, kn, kh):
    sl = slice(kh * HD, (kh + 1) * HD)
    return jnp.concatenate([km[:, sl], kp[:, sl], kc[:, sl], kn[:, sl]], axis=0)


LOG2E = 1.4426950408889634
LN2 = 0.6931471805599453
QSCALE = SCALE * LOG2E


def _att_update_bias(bias_ref, n, nb):
    @pl.when((n <= 2) | (n == nb - 1))
    def _():
        distf4, mask4 = _att_bias(n, nb)
        for kh in range(N_KV):
            slope_col = _group_col([_alibi_slope(kh * QG + gq) * LOG2E for gq in range(QG)])
            bias_ref[kh] = mask4 - slope_col * distf4


def _att_exp(qs, kb, bias_ref, kh, sink_ref):
    sink_col = _group_col([sink_ref[kh * QG + gq] for gq in range(QG)]) * LOG2E
    s = _nt(qs, kb) + bias_ref[kh]
    m = jnp.maximum(jnp.max(s, axis=1, keepdims=True), sink_col)
    e = jnp.exp2(s - m)
    es = jnp.exp2(sink_col - m)
    inv = 1.0 / (jnp.sum(e, axis=1, keepdims=True) + es)
    return e, es, inv


def attention_forward(tag, q, k, v, sink):
    LP = q.shape[0]
    nb = LP // BLK

    def kern(sink_ref, q_ref, km_ref, kp_ref, kc_ref, kn_ref, vm_ref, vp_ref, vc_ref, vn_ref, o_ref, bias_ref):
        n = pl.program_id(0)
        _att_update_bias(bias_ref, n, nb)
        qv = q_ref[...]
        km, kp, kc, kn = km_ref[...], kp_ref[...], kc_ref[...], kn_ref[...]
        vm, vp, vc, vn = vm_ref[...], vp_ref[...], vc_ref[...], vn_ref[...]
        for kh in range(N_KV):
            qs = _stack_heads(qv, kh)
            kb = _stack_keys(km, kp, kc, kn, kh)
            vb = _stack_keys(vm, vp, vc, vn, kh)
            e, _, inv = _att_exp(qs, kb, bias_ref, kh, sink_ref)
            o = _nn(e.astype(BF16), vb) * inv
            for gq in range(QG):
                hcol = (kh * QG + gq) * HD
                o_ref[:, hcol:hcol + HD] = o[gq * BLK:(gq + 1) * BLK].astype(o_ref.dtype)

    def kvspec(dn):
        return pl.BlockSpec((BLK, N_KV * HD), lambda n: (jnp.clip(n + dn, 0, nb - 1), 0))

    meta_spec = pl.BlockSpec((BLK, N_KV * HD), lambda n: (0, 0))
    return pl.pallas_call(
        kern, name=tag + "_att_fwd", grid=(nb,),
        in_specs=[pl.BlockSpec(memory_space=pltpu.SMEM), pl.BlockSpec((BLK, D), lambda n: (n, 0)),
                  meta_spec, kvspec(-1), kvspec(0), kvspec(1), meta_spec, kvspec(-1), kvspec(0), kvspec(1)],
        out_specs=pl.BlockSpec((BLK, D), lambda n: (n, 0)),
        out_shape=jax.ShapeDtypeStruct((LP, D), BF16),
        scratch_shapes=[pltpu.VMEM((N_KV, QG * BLK, 4 * BLK), F32)], compiler_params=_params("arbitrary"),
    )(sink, q, k, k, k, k, v, v, v, v)


def attention_backward(tag, q, k, v, do, sink):
    LP = q.shape[0]
    nb = LP // BLK
    KW = N_KV * HD

    def kern(sink_ref, q_ref, do_ref, km_ref, kp_ref, kc_ref, kn_ref, vm_ref, vp_ref, vc_ref, vn_ref,
             dq_ref, dk_ref, dv_ref, dkm_ref, dvm_ref, dsink_ref, bias_ref, rk, rv, fk, fv):
        n = pl.program_id(0)

        @pl.when(n == 0)
        def _():
            dkm_ref[...] = jnp.zeros_like(dkm_ref)
            dvm_ref[...] = jnp.zeros_like(dvm_ref)
            dsink_ref[...] = jnp.zeros_like(dsink_ref)
            rk[...] = jnp.zeros_like(rk)
            rv[...] = jnp.zeros_like(rv)

        _att_update_bias(bias_ref, n, nb)

        @pl.when(n < nb)
        def _():
            qv, dov = q_ref[...], do_ref[...]
            km, kp, kc, kn = km_ref[...], kp_ref[...], kc_ref[...], kn_ref[...]
            vm, vp, vc, vn = vm_ref[...], vp_ref[...], vc_ref[...], vn_ref[...]
            lane = lax.broadcasted_iota(jnp.int32, (8, 128), 1)
            dsink = jnp.zeros((8, 128), F32)
            for kh in range(N_KV):
                qs = _stack_heads(qv, kh)
                dos = _stack_heads(dov, kh)
                kb = _stack_keys(km, kp, kc, kn, kh)
                vb = _stack_keys(vm, vp, vc, vn, kh)
                dp = _nt(dos, vb)
                e, es, inv = _att_exp(qs, kb, bias_ref, kh, sink_ref)
                delta = inv * jnp.sum(e * dp, axis=1, keepdims=True)
                dsu = (e * (dp - delta)).astype(BF16)
                dqs = _nn(dsu, kb) * (inv * SCALE)
                dkt = _tn((qs.astype(F32) * (inv * LN2)).astype(BF16), dsu)
                dvt = _tn((dos.astype(F32) * inv).astype(BF16), e.astype(BF16))
                dsk = -(es * inv * delta)
                for gq in range(QG):
                    hcol = (kh * QG + gq) * HD
                    dq_ref[:, hcol:hcol + HD] = dqs[gq * BLK:(gq + 1) * BLK].astype(dq_ref.dtype)
                    tot = jnp.sum(dsk[gq * BLK:(gq + 1) * BLK], axis=0, keepdims=True)
                    dsink = dsink + jnp.where(lane == kh * QG + gq, tot, 0.0)
                hs = slice(kh * HD, (kh + 1) * HD)
                dkm_ref[hs, :] += dkt[:, 0:BLK]
                dvm_ref[hs, :] += dvt[:, 0:BLK]
                for ring, fin, part in ((rk, fk, dkt), (rv, fv, dvt)):
                    fin[hs, :] = ring[0, hs, :] + part[:, BLK:2 * BLK]
                    ring[0, hs, :] = ring[1, hs, :] + part[:, 2 * BLK:3 * BLK]
                    ring[1, hs, :] = part[:, 3 * BLK:4 * BLK]
            dsink_ref[...] += dsink
            dk_ref[...] = fk[...].T.astype(dk_ref.dtype)
            dv_ref[...] = fv[...].T.astype(dv_ref.dtype)

        @pl.when(n == nb)
        def _():
            dk_ref[...] = rk[0].T.astype(dk_ref.dtype)
            dv_ref[...] = rv[0].T.astype(dv_ref.dtype)

    def kvspec(dn):
        return pl.BlockSpec((BLK, KW), lambda n: (jnp.clip(jnp.minimum(n, nb - 1) + dn, 0, nb - 1), 0))

    meta_spec = pl.BlockSpec((BLK, KW), lambda n: (0, 0))
    rowspec = pl.BlockSpec((BLK, D), lambda n: (jnp.minimum(n, nb - 1), 0))
    emit_spec = pl.BlockSpec((BLK, KW), lambda n: (jnp.clip(n - 1, 1, nb - 1), 0))
    dq, dk, dv, dkm, dvm, dsink = pl.pallas_call(
        kern, name=tag + "_att_bwd", grid=(nb + 1,),
        in_specs=[pl.BlockSpec(memory_space=pltpu.SMEM), rowspec, rowspec,
                  meta_spec, kvspec(-1), kvspec(0), kvspec(1), meta_spec, kvspec(-1), kvspec(0), kvspec(1)],
        out_specs=[rowspec, emit_spec, emit_spec, _const_spec((KW, BLK)), _const_spec((KW, BLK)),
                   _const_spec((8, 128))],
        out_shape=[jax.ShapeDtypeStruct((LP, D), BF16), jax.ShapeDtypeStruct((LP, KW), BF16),
                   jax.ShapeDtypeStruct((LP, KW), BF16), jax.ShapeDtypeStruct((KW, BLK), F32),
                   jax.ShapeDtypeStruct((KW, BLK), F32), jax.ShapeDtypeStruct((8, 128), F32)],
        scratch_shapes=[pltpu.VMEM((N_KV, QG * BLK, 4 * BLK), F32), pltpu.VMEM((2, KW, BLK), F32),
                        pltpu.VMEM((2, KW, BLK), F32), pltpu.VMEM((KW, BLK), F32), pltpu.VMEM((KW, BLK), F32)],
        compiler_params=_params("arbitrary"),
    )(sink, q, do, k, k, k, k, v, v, v, v)
    dk = lax.dynamic_update_slice(dk, dkm.T.astype(BF16), (0, 0))
    dv = lax.dynamic_update_slice(dv, dvm.T.astype(BF16), (0, 0))
    return dq, dk, dv, dsink


SCAN_LANES = 1024


def _scan_tile(xr, xi, cr, ci, a8, tab, seg, reverse):
    sub = lax.broadcasted_iota(jnp.int32, (8, SCAN_LANES), 0)
    for c0 in range(0, NST, SCAN_LANES):
        cs = pl.ds(c0, SCAN_LANES)
        ar = a8[0, :, cs]
        ai = a8[1, :, cs]

        def rows(j):
            jj = (seg - 1 - j) if reverse else j
            return pl.ds(jj * 8, 8)

        def step1(j, carry):
            vr, vi = carry
            rs = rows(j)
            nr = ar * vr - ai * vi + xr[rs, cs]
            ni = ar * vi + ai * vr + xi[rs, cs]
            xr[rs, cs] = nr
            xi[rs, cs] = ni
            return nr, ni

        zero = jnp.zeros((8, SCAN_LANES), F32)
        vr, vi = zero, zero
        for j in range(seg):
            vr, vi = step1(j, (vr, vi))
        for t, s in enumerate((1, 2, 4)):
            sh = (8 - s) if reverse else s
            sr = pltpu.roll(vr, sh, 0)
            si = pltpu.roll(vi, sh, 0)
            tr = tab[2 * t, :, cs]
            ti = tab[2 * t + 1, :, cs]
            vr, vi = vr + tr * sr - ti * si, vi + tr * si + ti * sr
        pr = tab[6, :, cs]
        pi = tab[7, :, cs]
        c_r = cr[:, cs]
        c_i = ci[:, cs]
        vr, vi = vr + pr * c_r - pi * c_i, vi + pr * c_i + pi * c_r
        edge = 7 if reverse else 0
        last = 0 if reverse else 7
        sh = 7 if reverse else 1
        in_r = jnp.where(sub == edge, c_r, pltpu.roll(vr, sh, 0))
        in_i = jnp.where(sub == edge, c_i, pltpu.roll(vi, sh, 0))
        cr[:, cs] = jnp.broadcast_to(vr[last:last + 1, :], (8, SCAN_LANES))
        ci[:, cs] = jnp.broadcast_to(vi[last:last + 1, :], (8, SCAN_LANES))

        def step2(j, carry):
            dr, di = carry
            rs = rows(j)
            ndr = ar * dr - ai * di
            ndi = ar * di + ai * dr
            xr[rs, cs] += ndr
            xi[rs, cs] += ndi
            return ndr, ndi

        dr, di = in_r, in_i
        for j in range(seg):
            dr, di = step2(j, (dr, di))


ST_T = 4 * SP * 2
CH_T = 128


def _load_segmented(ref, scr, seg):
    out = []
    for ct in range(4):
        scr[ct] = ref[:, ct * CH_T:(ct + 1) * CH_T]
        out.append(jnp.concatenate([scr[ct, pl.ds(j, 8, stride=seg), :] for j in range(seg)], axis=0))
    return out


def _store_segmented(ref, scr, vals, seg):
    for ct in range(4):
        for j in range(seg):
            scr[ct, pl.ds(j, 8, stride=seg), :] = vals[ct][8 * j:8 * j + 8]
        ref[:, ct * CH_T:(ct + 1) * CH_T] = scr[ct]


def ssm_dir_forward(tag, u, bpr, bpi, cpr, cpi, a8, tab, reverse, tm):
    LP = u.shape[0]
    nt = LP // tm
    seg = tm // 8

    def rix(i):
        return (nt - 1 - i) if reverse else i

    def kern(u_ref, bpr_ref, bpi_ref, cpr_ref, cpi_ref, a8_ref, tab_ref, xre_ref, xim_ref, y_ref,
             xr, xi, ys, cr, ci):
        i = pl.program_id(0)

        @pl.when(i == 0)
        def _():
            cr[...] = jnp.zeros_like(cr)
            ci[...] = jnp.zeros_like(ci)

        ub = _load_segmented(u_ref, ys, seg)
        for ct in range(4):
            uc = ub[ct].astype(BF16)
            xr[:, ct * ST_T:(ct + 1) * ST_T] = _nn(uc, bpr_ref[ct * CH_T:(ct + 1) * CH_T, :])
            xi[:, ct * ST_T:(ct + 1) * ST_T] = _nn(uc, bpi_ref[ct * CH_T:(ct + 1) * CH_T, :])
        _scan_tile(xr, xi, cr, ci, a8_ref, tab_ref, seg, reverse)
        xrb = xr[...].astype(BF16)
        xib = xi[...].astype(BF16)
        xre_ref[...] = xrb
        xim_ref[...] = xib
        yv = []
        for ct in range(4):
            ss = slice(ct * ST_T, (ct + 1) * ST_T)
            yv.append(_nn(xrb[:, ss], cpr_ref[ss, :]) - _nn(xib[:, ss], cpi_ref[ss, :]))
        _store_segmented(y_ref, ys, yv, seg)

    row = lambda w: pl.BlockSpec((tm, w), lambda i: (rix(i), 0))
    return pl.pallas_call(
        kern, name=tag, grid=(nt,),
        in_specs=[row(SW), _const_spec(bpr.shape), _const_spec(bpi.shape), _const_spec(cpr.shape),
                  _const_spec(cpi.shape), _const_spec(a8.shape), _const_spec(tab.shape)],
        out_specs=[row(NST), row(NST), row(SW)],
        out_shape=[jax.ShapeDtypeStruct((LP, NST), BF16), jax.ShapeDtypeStruct((LP, NST), BF16),
                   jax.ShapeDtypeStruct((LP, SW), F32)],
        scratch_shapes=[pltpu.VMEM((tm, NST), F32), pltpu.VMEM((tm, NST), F32), pltpu.VMEM((4, tm, CH_T), F32),
                        pltpu.VMEM((8, NST), F32), pltpu.VMEM((8, NST), F32)],
        compiler_params=_params("arbitrary"),
    )(u, bpr, bpi, cpr, cpi, a8, tab)


def ssm_dir_backward(tag, dy, xre, xim, u, bpr, bpi, cpr, cpi, a8_adj, tab_adj, reverse, tm):
    LP = u.shape[0]
    nt = LP // tm
    seg = tm // 8

    def rix(i):
        return (nt - 1 - i) if reverse else i

    def kern(dy_ref, xre_ref, xim_ref, u_ref, bpr_ref, bpi_ref, cpr_ref, cpi_ref, a8_ref, tab_ref,
             du_ref, gbr_ref, gbi_ref, gcr_ref, gci_ref, sr_ref, si_ref, lr, li, gr, gi, dus, cr, ci):
        i = pl.program_id(0)

        @pl.when(i == 0)
        def _():
            cr[...] = jnp.zeros_like(cr)
            ci[...] = jnp.zeros_like(ci)
            for r in (gbr_ref, gbi_ref, gcr_ref, gci_ref, sr_ref, si_ref):
                r[...] = jnp.zeros_like(r)

        dyb = [v.astype(BF16) for v in _load_segmented(dy_ref, dus, seg)]
        ub = [v.astype(BF16) for v in _load_segmented(u_ref, dus, seg)]
        for ct in range(4):
            ss = slice(ct * ST_T, (ct + 1) * ST_T)
            dc = dyb[ct]
            g_re = _nt(dc, cpr_ref[ss, :])
            g_im = -_nt(dc, cpi_ref[ss, :])
            lr[:, ss] = g_re
            li[:, ss] = g_im
            gr[:, ss] = g_re
            gi[:, ss] = g_im
        _scan_tile(lr, li, cr, ci, a8_ref, tab_ref, seg, reverse)
        lam_r = lr[...]
        lam_i = li[...]
        wr = lam_r - gr[...]
        wi = lam_i - gi[...]
        xr = xre_ref[...].astype(F32)
        xi = xim_ref[...].astype(F32)
        sr_ref[...] += jnp.sum(wr * xr + wi * xi, axis=0, keepdims=True)
        si_ref[...] += jnp.sum(wi * xr - wr * xi, axis=0, keepdims=True)
        lrb = lam_r.astype(BF16)
        lib = lam_i.astype(BF16)
        xrb = xre_ref[...]
        xib = xim_ref[...]
        duv = []
        for ct in range(4):
            ss = slice(ct * ST_T, (ct + 1) * ST_T)
            cs = slice(ct * CH_T, (ct + 1) * CH_T)
            duv.append(_nt(lrb[:, ss], bpr_ref[cs, :]) + _nt(lib[:, ss], bpi_ref[cs, :]))
            gbr_ref[cs, :] += _tn(ub[ct], lrb[:, ss])
            gbi_ref[cs, :] += _tn(ub[ct], lib[:, ss])
            gcr_ref[cs, :] += _tn(dyb[ct], xrb[:, ss])
            gci_ref[cs, :] -= _tn(dyb[ct], xib[:, ss])
        _store_segmented(du_ref, dus, duv, seg)

    row = lambda w: pl.BlockSpec((tm, w), lambda i: (rix(i), 0))
    acc = _const_spec((SW, ST_T))
    vec = _const_spec((1, NST))
    return pl.pallas_call(
        kern, name=tag, grid=(nt,),
        in_specs=[row(SW), row(NST), row(NST), row(SW), _const_spec(bpr.shape), _const_spec(bpi.shape),
                  _const_spec(cpr.shape), _const_spec(cpi.shape), _const_spec(a8_adj.shape),
                  _const_spec(tab_adj.shape)],
        out_specs=[row(SW), acc, acc, acc, acc, vec, vec],
        out_shape=[jax.ShapeDtypeStruct((LP, SW), F32)] + [jax.ShapeDtypeStruct((SW, ST_T), F32)] * 4
        + [jax.ShapeDtypeStruct((1, NST), F32)] * 2,
        scratch_shapes=[pltpu.VMEM((tm, NST), F32)] * 4 + [pltpu.VMEM((4, tm, CH_T), F32)]
        + [pltpu.VMEM((8, NST), F32)] * 2,
        compiler_params=_params("arbitrary"),
    )(dy, xre, xim, u, bpr, bpi, cpr, cpi, a8_adj, tab_adj)


def _ssm_disc(lam_re, lam_im, log_dt, b_re, b_im):
    dt = jnp.exp(log_dt)[:, None]
    mag = jnp.exp(lam_re * dt)
    a_re = mag * jnp.cos(lam_im * dt)
    a_im = mag * jnp.sin(lam_im * dt)
    den = lam_re * lam_re + lam_im * lam_im
    f_re = ((a_re - 1.0) * lam_re + a_im * lam_im) / den
    f_im = (a_im * lam_re - (a_re - 1.0) * lam_im) / den
    bb_re = f_re[:, :, None] * b_re - f_im[:, :, None] * b_im
    bb_im = f_re[:, :, None] * b_im + f_im[:, :, None] * b_re
    return a_re, a_im, bb_re, bb_im


def _scan_tables(lam_re, lam_im, log_dt, conj, reverse, seg):
    dt = jnp.exp(log_dt)[:, None]
    lr = (lam_re * dt).reshape(1, NST)
    li = (lam_im * dt).reshape(1, NST) * (-1.0 if conj else 1.0)
    t = jnp.arange(8, dtype=F32)[:, None]

    def power(kk):
        mag = jnp.exp(kk * lr)
        return mag * jnp.cos(kk * li), mag * jnp.sin(kk * li)

    ones = jnp.ones((8, 1), F32)
    a8 = jnp.stack(power(ones)).astype(F32)
    tabs = []
    for s in (1, 2, 4):
        mask = (t <= 7 - s) if reverse else (t >= s)
        pr, pi = power(float(s * seg) * ones)
        tabs += [jnp.where(mask, pr, 0.0), jnp.where(mask, pi, 0.0)]
    kk = ((8.0 - t) if reverse else (t + 1.0)) * float(seg)
    pr, pi = power(kk)
    tabs += [pr, pi]
    return a8, jnp.stack(tabs).astype(F32)


def _pack_b(bb):
    t = bb.transpose(0, 2, 1).reshape(4, 8, SCH, SP)
    eye = jnp.eye(8, dtype=bb.dtype)
    return jnp.einsum('tgcp,gh->tgchp', t, eye).reshape(SW, ST_T)


def _pack_c(cc):
    t = cc.transpose(0, 2, 1).reshape(4, 8, SP, SCH)
    eye = jnp.eye(8, dtype=cc.dtype)
    return jnp.einsum('tgpc,gh->tgphc', t, eye).reshape(NST, CH_T)


def _unpack_diag(acc):
    t = acc.reshape(4, 8, SCH, 8, SP)
    eye = jnp.eye(8, dtype=acc.dtype)
    return jnp.einsum('tgchp,gh->tgcp', t, eye).reshape(SGRP, SCH, SP)


def _gelu(y):
    k0 = math.sqrt(2.0 / math.pi)
    inner = k0 * (y + 0.044715 * y * y * y)
    th = jnp.tanh(inner)
    z = 0.5 * y * (1.0 + th)
    dz = 0.5 * (1.0 + th) + 0.5 * y * (1.0 - th * th) * k0 * (1.0 + 3.0 * 0.044715 * y * y)
    return z, dz


Q0, K0, V0, U0, GS0, GA0, IN_COLS = 0, 1024, 1280, 1536, 2048, 3072, 4096


def mixer_forward(tag, h, p, tm):
    g, winT, wglu, wbsT, wba, wout = p["g"], p["winT"], p["wglu"], p["wbsT"], p["wba"], p["wout"]

    def proj(i, hv, g_ref, w_ref):
        _, _, n = _rms_fwd(hv, g_ref[...])
        nb = n.astype(BF16)
        return (nb, _nt(nb, w_ref[Q0:K0, :]) * QSCALE, _nt(nb, w_ref[K0:V0, :]), _nt(nb, w_ref[V0:U0, :]),
                _nt(nb, w_ref[U0:GS0, :]), _nt(nb, w_ref[GS0:GA0, :]), _nt(nb, w_ref[GA0:IN_COLS, :]))

    n, q, k, v, u, gs, ga = rowcall(
        tag + "_proj", proj, [h], [g, winT],
        [(D, BF16), (D, BF16), (N_KV * HD, BF16), (N_KV * HD, BF16), (SW, F32), (D, BF16), (D, BF16)],
        tm=_tall_tile(h.shape[0], tm))

    ya = attention_forward(tag, q, k, v, p["sink"])

    states, ydir = [], []
    for dr in range(2):
        s = p["ssm"][dr]
        xre, xim, yd = ssm_dir_forward(f"{tag}_ssm_fwd{dr}", u, s["bpr"], s["bpi"], s["cpr"], s["cpi"],
                                       s["a8"], s["tab"], dr == 1, tm)
        states.append((xre, xim))
        ydir.append(yd)

    def merge(i, y0, y1, uv, yav, gsv, gav, hv, d_ref, wglu_ref, wbs_ref, wba_ref, wout_ref):
        ypre = y0 + y1 + d_ref[...] * uv
        z, _ = _gelu(ypre)
        zb = z.astype(BF16)
        t = _nn(zb, wglu_ref[...])
        ysb = (z * _sig(t)).astype(BF16)
        bs = _nt(ysb, wbs_ref[...])
        ba = _nn(yav, wba_ref[...])
        mg = _sig(gsv.astype(F32)) * bs + _sig(gav.astype(F32)) * ba
        mg = jnp.where(_row_ok(i, tm), mg, 0.0).astype(BF16)
        return ypre, zb, t, ysb, bs, ba, mg, hv + _nn(mg, wout_ref[...])

    ypre, zb, t, ys, bs, ba, mg, h2 = rowcall(
        tag + "_merge", merge, [ydir[0], ydir[1], u, ya, gs, ga, h], [p["d"], wglu, wbsT, wba, wout],
        [(SW, BF16), (SW, BF16), (SW, BF16), (SW, BF16), (D, BF16), (D, BF16), (D, BF16), (D, F32)], tm=tm)
    saved = dict(h=h, n=n, q=q, k=k, v=v, u=u, gs=gs, ga=ga, ya=ya, states=states, ypre=ypre, zb=zb, t=t,
                 ys=ys, bs=bs, ba=ba, mg=mg)
    return h2, saved


def mixer_backward(tag, dh, sv, p, tm):
    g, winT, wglu, wbsT, wba, wout = p["g"], p["winT"], p["wglu"], p["wbsT"], p["wba"], p["wout"]

    def y1(i, dhv, bsv, bav, gsv, gav, ypv, tv, uv, wout_ref, wbs_ref, wba_ref, d_ref, wglu_ref):
        dhb = dhv.astype(BF16)
        dmg = _nt(dhb, wout_ref[...])
        dmg = jnp.where(_row_ok(i, tm), dmg, 0.0)
        sgs = _sig(gsv.astype(F32))
        sga = _sig(gav.astype(F32))
        dbs = (dmg * sgs).astype(BF16)
        dba = (dmg * sga).astype(BF16)
        dgs = dmg * bsv.astype(F32) * sgs * (1.0 - sgs)
        dga = dmg * bav.astype(F32) * sga * (1.0 - sga)
        dys = _nn(dbs, wbs_ref[...])
        dya = _nt(dba, wba_ref[...])
        z, dz_dy = _gelu(ypv.astype(F32))
        st = _sig(tv.astype(F32))
        dt_ = dys * z * st * (1.0 - st)
        dz = dys * st + _nt(dt_.astype(BF16), wglu_ref[...])
        dyp = dz * dz_dy
        return (dbs, dba, dgs, dga, dhb, dya, dyp, dyp * d_ref[...], dt_,
                jnp.sum(dyp * uv, axis=0, keepdims=True))

    dbs, dba, dgs, dga, dhb, dya, dypb, du0, dtb, dd = rowcall(
        tag + "_bwd_merge", y1,
        [dh, sv["bs"], sv["ba"], sv["gs"], sv["ga"], sv["ypre"], sv["t"], sv["u"]],
        [wout, wbsT, wba, p["d"], wglu],
        [(D, BF16)] * 6 + [(SW, F32), (SW, F32), (SW, BF16)], [(1, SW)], tm=tm)
    dwout = tn_matmul(tag + "_dwout", sv["mg"], dhb)
    dwbsT = tn_matmul(tag + "_dwbs", dbs, sv["ys"])
    dwba = tn_matmul(tag + "_dwba", sv["ya"], dba)
    dwglu = tn_matmul(tag + "_dwglu", sv["zb"], dtb)

    du_dirs, ssm_sums = [], []
    for dr in range(2):
        s = p["ssm"][dr]
        xre, xim = sv["states"][dr]
        res = ssm_dir_backward(f"{tag}_ssm_bwd{dr}", dypb, xre, xim, sv["u"], s["bpr"], s["bpi"], s["cpr"],
                               s["cpi"], s["a8_adj"], s["tab_adj"], dr == 0, tm)
        du_dirs.append(res[0])
        ssm_sums.append(res[1:])

    dq, dk, dv, dsink = attention_backward(tag, sv["q"], sv["k"], sv["v"], dya, p["sink"])

    def x1b(i, dqv, dkv, dvv, du0v, du1v, du2v, dgsv, dgav, hv, dhv, g_ref, w_ref):
        dub = (du0v + du1v + du2v).astype(BF16)
        dn = (_nn(dqv, w_ref[Q0:K0, :]) + _nn(dkv, w_ref[K0:V0, :]) + _nn(dvv, w_ref[V0:U0, :])
              + _nn(dub, w_ref[U0:GS0, :]) + _nn(dgsv, w_ref[GS0:GA0, :]) + _nn(dgav, w_ref[GA0:IN_COLS, :]))
        gv = g_ref[...]
        hh, r, _ = _rms_fwd(hv, gv)
        dx, dg = _rms_bwd(hh, r, gv, dn)
        dx = jnp.where(_row_ok(i, tall), dx, 0.0)
        return dhv + dx, dub, dg

    tall = _tall_tile(dh.shape[0], tm)
    dh2, dub, dg = rowcall(tag + "_bwd_in", x1b,
                           [dq, dk, dv, du0, du_dirs[0], du_dirs[1], dgs, dga, sv["h"], dh], [g, winT],
                           [(D, F32), (SW, BF16)], [(1, D)], tm=tall)
    n = sv["n"]
    dwinT = jnp.concatenate([tn_matmul(f"{tag}_dwin{j}", piece, n)
                             for j, piece in enumerate((dq, dk, dv, dub, dgs, dga))], axis=0)
    grads = dict(g=dg, d=dd, sink=dsink, ssm=ssm_sums, winT=dwinT, wglu=dwglu, wbsT=dwbsT, wba=dwba, wout=dwout)
    return dh2, grads


W1024 = ("f1_wgT", "f1_wuT", "f1_wd", "winT", "wba", "wout", "f2_wgT", "f2_wuT", "f2_wd")
W512 = ("wglu", "wbsT")
PART_F1 = ("f1_wgT", "f1_wuT", "f1_wd")
PART_MIX = ("winT", "wba", "wout", "wglu", "wbsT")
PART_F2 = ("f2_wgT", "f2_wuT", "f2_wd")
PER_LAYER_SMALL = ("ffn1_norm", "mix_norm", "ffn2_norm", "ssm_lam_re", "ssm_lam_im", "ssm_log_dt",
                   "ssm_b_re", "ssm_b_im", "ssm_c_re", "ssm_c_im", "ssm_d", "attn_sink")
SMALL = ("ffn1_norm", "mix_norm", "ffn2_norm", "final_norm", "ssm_lam_re", "ssm_lam_im", "ssm_log_dt",
         "ssm_b_re", "ssm_b_im", "ssm_c_re", "ssm_c_im", "ssm_d", "attn_sink")


def kernel(x, meta_tokens, ffn1_norm, ffn1_w_gate, ffn1_w_up, ffn1_w_down, mix_norm, w_in, ssm_lam_re, ssm_lam_im, ssm_log_dt, ssm_b_re, ssm_b_im, ssm_c_re, ssm_c_im, ssm_d, ssm_w_glu, attn_sink, w_branch_ssm, w_branch_attn, w_out, ffn2_norm, ffn2_w_gate, ffn2_w_up, ffn2_w_down, final_norm, loss_target, m_meta_tokens, m_ffn1_norm, m_ffn1_w_gate, m_ffn1_w_up, m_ffn1_w_down, m_mix_norm, m_w_in, m_ssm_lam_re, m_ssm_lam_im, m_ssm_log_dt, m_ssm_b_re, m_ssm_b_im, m_ssm_c_re, m_ssm_c_im, m_ssm_d, m_ssm_w_glu, m_attn_sink, m_w_branch_ssm, m_w_branch_attn, m_w_out, m_ffn2_norm, m_ffn2_w_gate, m_ffn2_w_up, m_ffn2_w_down, m_final_norm, v_meta_tokens, v_ffn1_norm, v_ffn1_w_gate, v_ffn1_w_up, v_ffn1_w_down, v_mix_norm, v_w_in, v_ssm_lam_re, v_ssm_lam_im, v_ssm_log_dt, v_ssm_b_re, v_ssm_b_im, v_ssm_c_re, v_ssm_c_im, v_ssm_d, v_ssm_w_glu, v_attn_sink, v_w_branch_ssm, v_w_branch_attn, v_w_out, v_ffn2_norm, v_ffn2_w_gate, v_ffn2_w_up, v_ffn2_w_down, v_final_norm):
    weights = dict(meta_tokens=meta_tokens, ffn1_norm=ffn1_norm, ffn1_w_gate=ffn1_w_gate, ffn1_w_up=ffn1_w_up, ffn1_w_down=ffn1_w_down, mix_norm=mix_norm, w_in=w_in, ssm_lam_re=ssm_lam_re, ssm_lam_im=ssm_lam_im, ssm_log_dt=ssm_log_dt, ssm_b_re=ssm_b_re, ssm_b_im=ssm_b_im, ssm_c_re=ssm_c_re, ssm_c_im=ssm_c_im, ssm_d=ssm_d, ssm_w_glu=ssm_w_glu, attn_sink=attn_sink, w_branch_ssm=w_branch_ssm, w_branch_attn=w_branch_attn, w_out=w_out, ffn2_norm=ffn2_norm, ffn2_w_gate=ffn2_w_gate, ffn2_w_up=ffn2_w_up, ffn2_w_down=ffn2_w_down, final_norm=final_norm)
    mom_m = dict(meta_tokens=m_meta_tokens, ffn1_norm=m_ffn1_norm, ffn1_w_gate=m_ffn1_w_gate, ffn1_w_up=m_ffn1_w_up, ffn1_w_down=m_ffn1_w_down, mix_norm=m_mix_norm, w_in=m_w_in, ssm_lam_re=m_ssm_lam_re, ssm_lam_im=m_ssm_lam_im, ssm_log_dt=m_ssm_log_dt, ssm_b_re=m_ssm_b_re, ssm_b_im=m_ssm_b_im, ssm_c_re=m_ssm_c_re, ssm_c_im=m_ssm_c_im, ssm_d=m_ssm_d, ssm_w_glu=m_ssm_w_glu, attn_sink=m_attn_sink, w_branch_ssm=m_w_branch_ssm, w_branch_attn=m_w_branch_attn, w_out=m_w_out, ffn2_norm=m_ffn2_norm, ffn2_w_gate=m_ffn2_w_gate, ffn2_w_up=m_ffn2_w_up, ffn2_w_down=m_ffn2_w_down, final_norm=m_final_norm)
    mom_v = dict(meta_tokens=v_meta_tokens, ffn1_norm=v_ffn1_norm, ffn1_w_gate=v_ffn1_w_gate, ffn1_w_up=v_ffn1_w_up, ffn1_w_down=v_ffn1_w_down, mix_norm=v_mix_norm, w_in=v_w_in, ssm_lam_re=v_ssm_lam_re, ssm_lam_im=v_ssm_lam_im, ssm_log_dt=v_ssm_log_dt, ssm_b_re=v_ssm_b_re, ssm_b_im=v_ssm_b_im, ssm_c_re=v_ssm_c_re, ssm_c_im=v_ssm_c_im, ssm_d=v_ssm_d, ssm_w_glu=v_ssm_w_glu, attn_sink=v_attn_sink, w_branch_ssm=v_w_branch_ssm, w_branch_attn=v_w_branch_attn, w_out=v_w_out, ffn2_norm=v_ffn2_norm, ffn2_w_gate=v_ffn2_w_gate, ffn2_w_up=v_ffn2_w_up, ffn2_w_down=v_ffn2_w_down, final_norm=v_final_norm)
    names = list(weights)

    L0 = x.shape[1]
    LP = L0 + BLK
    tm = 384 if LP % 384 == 0 else BLK
    x_i, y_i, c_i = lax.axis_index("x"), lax.axis_index("y"), lax.axis_index("c")
    me = 4 * x_i + 2 * y_i + c_i

    def canon(l):
        return dict(
            f1_wgT=ffn1_w_gate[l].T, f1_wuT=ffn1_w_up[l].T, f1_wd=ffn1_w_down[l],
            winT=w_in[l].T, wba=w_branch_attn[l], wout=w_out[l],
            f2_wgT=ffn2_w_gate[l].T, f2_wuT=ffn2_w_up[l].T, f2_wd=ffn2_w_down[l],
            wglu=ssm_w_glu[l], wbsT=w_branch_ssm[l].T)

    shards = [{nm: a.astype(BF16) for nm, a in canon(l).items()} for l in range(DEPTH)]

    def rows_of(nm):
        return shards[0][nm].shape[0]

    def width_groups(names_):
        return [g for g in ([nm for nm in names_ if nm in W1024], [nm for nm in names_ if nm in W512]) if g]

    def pieces_for(group):
        out, off = [], 0
        for nm in group:
            out.append((off, rows_of(nm)))
            off += rows_of(nm)
        return out

    def start_gather(tag, l, names_):
        groups = width_groups(names_)
        packed = [jnp.concatenate([shards[l][nm] for nm in g], axis=0) for g in groups]
        return gather_layer_start(tag, packed, [pieces_for(g) for g in groups]), groups

    def finish_gather(tag, l, started_, after):
        handle, groups = started_
        dests = gather_layer_wait(tag, handle, len(groups), after)
        out = {}
        for nm, dest in zip([nm for g in groups for nm in g], dests):
            sh = shards[l][nm]
            out[nm] = lax.dynamic_update_slice(dest, sh, (me * sh.shape[0], 0))
        return out

    g1, gm = all_gather_pieces(
        "gather_weights_first",
        [(jnp.concatenate([shards[0][nm] for nm in PART_F1], axis=0), pieces_for(PART_F1)),
         (meta_tokens, [(0, N_META)])])
    first_weights = dict(zip(PART_F1, g1))
    meta_full = gm[0].reshape(NDEV, N_META, D // NDEV).transpose(1, 0, 2).reshape(N_META, D)
    gather_started = [start_gather("gather_start_l0", 0, PART_MIX + PART_F2)]
    gather_started += [start_gather(f"gather_start_l{l}", l, W1024 + W512) for l in range(1, DEPTH)]
    started = sum(st[0][3][0, 0] for st in gather_started)
    full = [None] * DEPTH

    def disc_all(lre, lim, ldt, bre, bim):
        return _ssm_disc(lre, lim, ldt, bre, bim)

    ssm_p, ssm_vjp = [], []
    for l in range(DEPTH):
        row, vrow = [], []
        for dr in range(2):
            args = (ssm_lam_re[l, dr], ssm_lam_im[l, dr], ssm_log_dt[l, dr], ssm_b_re[l, dr], ssm_b_im[l, dr])
            (a_re, a_im, bb_re, bb_im), vjp = jax.vjp(disc_all, *args)
            a8, tab = _scan_tables(args[0], args[1], args[2], False, dr == 1, tm // 8)
            a8_adj, tab_adj = _scan_tables(args[0], args[1], args[2], True, dr == 0, tm // 8)
            row.append(dict(
                bpr=_pack_b(bb_re).astype(BF16), bpi=_pack_b(bb_im).astype(BF16),
                cpr=_pack_c(ssm_c_re[l, dr]).astype(BF16), cpi=_pack_c(ssm_c_im[l, dr]).astype(BF16),
                a8=a8, tab=tab, a8_adj=a8_adj, tab_adj=tab_adj, a_re=a_re, a_im=a_im))
            vrow.append(vjp)
        ssm_p.append(row)
        ssm_vjp.append(vrow)

    blk0 = jnp.concatenate([jnp.zeros((PAD, D), F32), meta_full.astype(F32)], axis=0)
    h = build_h0(x[0], blk0)
    saved = []
    for l in range(DEPTH):
        w = dict(first_weights) if l == 0 else finish_gather(f"gather_wait_l{l}", l, gather_started[l], h)
        full[l] = w
        g1n, g2n = ffn1_norm[l][None, :], ffn2_norm[l][None, :]
        if l == 0:
            g1n = g1n + started
        h, s1 = ffn_forward("ffn1", h, g1n, w["f1_wgT"], w["f1_wuT"], w["f1_wd"], tm)
        if l == 0:
            w.update(finish_gather("gather_wait_l0", 0, gather_started[0], h))
        mp = dict(g=mix_norm[l][None, :], winT=w["winT"], wglu=w["wglu"], wbsT=w["wbsT"], wba=w["wba"],
                  wout=w["wout"], d=ssm_d[l][None, :], sink=attn_sink[l], ssm=ssm_p[l])
        h, s2 = mixer_forward("mix", h, mp, tm)
        h, s3 = ffn_forward("ffn2", h, g2n, w["f2_wgT"], w["f2_wuT"], w["f2_wd"], tm)
        saved.append((s1, s2, s3, mp, g1n, g2n))

    dh, loss_acc, dgf = final_loss(h, loss_target[0], final_norm[None, :])
    loss = lax.psum(loss_acc[0, 0], MESH_AXES)

    small = {nm: [None] * DEPTH for nm in SMALL if nm != "final_norm"}

    def start_scatter(tag, grads_d, names_):
        groups = width_groups(names_)
        mine = [jnp.concatenate([lax.dynamic_slice_in_dim(grads_d[nm], me * rows_of(nm), rows_of(nm), axis=0)
                                 for nm in g], axis=0) for g in groups]
        handle, nin = scatter_layer_start(tag + "_start", [[grads_d[nm] for nm in g] for g in groups])
        return dict(tag=tag, handle=handle, nin=nin, groups=groups, mine=mine)

    def finish_scatter(st, after):
        lands = scatter_layer_wait(st["tag"] + "_wait", st["handle"], st["nin"], len(st["groups"]), after)
        out = {}
        for land, mine, g in zip(lands, st["mine"], st["groups"]):
            land = lax.dynamic_update_slice(land, mine[None], (me, 0, 0))
            tot = sum_slots(f"sum_weight_grads_{land.shape[1]}x{land.shape[2]}", land)
            for nm, (off_, r) in zip(g, pieces_for(g)):
                out[nm] = tot[off_:off_ + r]
        return out

    scatters = []
    small_started = [None] * DEPTH
    small_len = sum(math.prod(weights[k].shape[1:]) for k in PER_LAYER_SMALL) + N_META * D
    small_rows = -(-small_len // (8 * D)) * 8
    sent = jnp.zeros((), F32)
    for l in reversed(range(DEPTH)):
        s1, s2, s3, mp, g1n, g2n = saved[l]
        w = full[l]
        dh, dg2, f2g, f2u, f2d = ffn_backward("ffn2", dh, s3, g2n + sent, w["f2_wgT"], w["f2_wuT"], w["f2_wd"], tm)
        st = start_scatter(f"scatter_l{l}_f2", dict(f2_wgT=f2g, f2_wuT=f2u, f2_wd=f2d), PART_F2)
        scatters.append((l, st))
        dh, mg = mixer_backward("mix", dh, s2, dict(mp, d=mp["d"] + st["handle"][3][0, 0]), tm)
        st = start_scatter(f"scatter_l{l}_mix", mg, PART_MIX)
        scatters.append((l, st))
        if l > 0:
            dh, dg1, f1g, f1u, f1d = ffn_backward("ffn1", dh, s1, g1n + st["handle"][3][0, 0],
                                                  w["f1_wgT"], w["f1_wuT"], w["f1_wd"], tm)
            st = start_scatter(f"scatter_l{l}_f1", dict(f1_wgT=f1g, f1_wuT=f1u, f1_wd=f1d), PART_F1)
            scatters.append((l, st))
            sent = st["handle"][3][0, 0]
        else:
            def emit(nm, arr, l=l):
                scatters.append((l, start_scatter(f"scatter_l{l}_f1_{nm}", {"f1_" + nm: arr}, ("f1_" + nm,))))

            dh, dg1, _, _, _ = ffn_backward("ffn1", dh, s1, g1n + st["handle"][3][0, 0],
                                            w["f1_wgT"], w["f1_wuT"], w["f1_wd"], tm, emit=emit)
        small["ffn1_norm"][l] = dg1[0]
        small["mix_norm"][l] = mg["g"][0]
        small["ffn2_norm"][l] = dg2[0]
        small["ssm_d"][l] = mg["d"][0]
        small["attn_sink"][l] = mg["sink"][0, :N_HEADS]
        per_dir = {k: [] for k in ("ssm_lam_re", "ssm_lam_im", "ssm_log_dt", "ssm_b_re", "ssm_b_im",
                                   "ssm_c_re", "ssm_c_im")}
        for dr in range(2):
            gbr, gbi, gcr, gci, s_re, s_im = mg["ssm"][dr]
            a_re, a_im = ssm_p[l][dr]["a_re"], ssm_p[l][dr]["a_im"]
            s_re = s_re.reshape(SGRP, SP)
            s_im = s_im.reshape(SGRP, SP)
            den = a_re * a_re + a_im * a_im
            ga_re = (s_re * a_re - s_im * a_im) / den
            ga_im = (s_re * a_im + s_im * a_re) / den
            glr, gli, gld, gbre, gbim = ssm_vjp[l][dr]((ga_re, ga_im, _unpack_diag(gbr).transpose(0, 2, 1),
                                                        _unpack_diag(gbi).transpose(0, 2, 1)))
            per_dir["ssm_lam_re"].append(glr)
            per_dir["ssm_lam_im"].append(gli)
            per_dir["ssm_log_dt"].append(gld)
            per_dir["ssm_b_re"].append(gbre)
            per_dir["ssm_b_im"].append(gbim)
            per_dir["ssm_c_re"].append(_unpack_diag(gcr))
            per_dir["ssm_c_im"].append(_unpack_diag(gci))
        for k, vlist in per_dir.items():
            small[k][l] = jnp.stack(vlist)
        vec = [small[k][l].reshape(-1) for k in PER_LAYER_SMALL]
        if l == DEPTH - 1:
            vec.append(dgf[0])
        if l == 0:
            vec.append(dh[PAD:BLK].reshape(-1))
        used = sum(v.shape[0] for v in vec)
        flat = jnp.concatenate(vec + [jnp.zeros((small_rows * D - used,), F32)]).reshape(small_rows, D)
        small_started[l] = (exchange_start(
            f"small_grads_l{l}_start", [flat, jnp.zeros((NDEV, small_rows, D), F32)], 1,
            lambda refs, me_i, p_i: [(refs[0], refs[1].at[me_i], 0)]), flat)
        sent = sent + small_started[l][0][3][0, 0]

    grad_x = dh[BLK:][None]

    grads = {k: [None] * DEPTH for k in PER_LAYER_SMALL}
    for l in reversed(range(DEPTH)):
        (s_sems, r_sems, arrays, _), flat = small_started[l]
        land = exchange_wait(f"small_grads_l{l}_wait", s_sems, r_sems, arrays, 1, lambda refs, gi: refs[0], dh)[1]
        land = lax.dynamic_update_slice(land, flat[None], (me, 0, 0))
        tot = sum_slots("sum_small_grads", land).reshape(-1)
        o = 0
        for k in PER_LAYER_SMALL:
            shp = weights[k].shape[1:]
            sz = math.prod(shp)
            grads[k][l] = tot[o:o + sz].reshape(shp)
            o += sz
        if l == DEPTH - 1:
            final_norm_grad = tot[o:o + D]
        if l == 0:
            dmeta_full = tot[o:o + N_META * D].reshape(N_META, D)
    grads = {k: jnp.stack(vv) for k, vv in grads.items()}
    grads["final_norm"] = final_norm_grad
    grads["meta_tokens"] = lax.dynamic_slice_in_dim(dmeta_full, me * (D // NDEV), D // NDEV, axis=1)

    own = [dict() for _ in range(DEPTH)]
    for l, st in scatters:
        own[l].update(finish_scatter(st, dh))

    def stack(fn):
        return jnp.stack([fn(own[l]) for l in range(DEPTH)])

    grads["ffn1_w_gate"] = stack(lambda d: d["f1_wgT"].T)
    grads["ffn1_w_up"] = stack(lambda d: d["f1_wuT"].T)
    grads["ffn1_w_down"] = stack(lambda d: d["f1_wd"])
    grads["w_in"] = stack(lambda d: d["winT"].T)
    grads["ssm_w_glu"] = stack(lambda d: d["wglu"])
    grads["w_branch_ssm"] = stack(lambda d: d["wbsT"].T)
    grads["w_branch_attn"] = stack(lambda d: d["wba"])
    grads["w_out"] = stack(lambda d: d["wout"])
    grads["ffn2_w_gate"] = stack(lambda d: d["f2_wgT"].T)
    grads["ffn2_w_up"] = stack(lambda d: d["f2_wuT"].T)
    grads["ffn2_w_down"] = stack(lambda d: d["f2_wd"])

    deltas, new_m, new_v = {}, {}, {}
    for nm in names:
        deltas[nm], new_m[nm], new_v[nm] = adamw("adamw_" + nm, weights[nm], grads[nm], mom_m[nm], mom_v[nm])

    return (loss, grad_x, *[grads[n] for n in names], *[deltas[n] for n in names],
            *[new_m[n] for n in names], *[new_v[n] for n in names])
```

```python
import functools
import math

import jax
import jax.numpy as jnp
from jax import lax
from jax.experimental import pallas as pl
from jax.experimental.pallas import tpu as pltpu

F32 = jnp.float32
BF16 = jnp.bfloat16

D = 1024
DFF = 2816
N_META = 16
N_HEADS = 16
N_KV = 4
HD = 64
QG = 4
WIN = 128
BLK = 128
PAD = BLK - N_META
SW = 512
SGRP = 32
SCH = 16
SP = 64
NST = SGRP * SP
EPS = 1e-6
NEG = -1e30
SCALE = HD ** -0.5
NDEV = 8
DEPTH = 4
MESH_AXES = ("x", "y", "c")
MESH = pl.DeviceIdType.MESH

ADAM_LR = 0.001
ADAM_B1 = 0.9
ADAM_B2 = 0.999
ADAM_EPS = 1e-08
ADAM_WD = 0.01
ADAM_STEP = 10

VMEM_LIMIT = 56 * 1024 * 1024


def _params(*sem):
    return pltpu.CompilerParams(dimension_semantics=sem, vmem_limit_bytes=VMEM_LIMIT)


def _nn(a, b):
    return lax.dot_general(a, b, (((1,), (0,)), ((), ())), preferred_element_type=F32)


def _nt(a, b):
    return lax.dot_general(a, b, (((1,), (1,)), ((), ())), preferred_element_type=F32)


def _tn(a, b):
    return lax.dot_general(a, b, (((0,), (0,)), ((), ())), preferred_element_type=F32)


def _sig(x):
    return 0.5 * jnp.tanh(0.5 * x) + 0.5


def _rms_fwd(h, g):
    r = lax.rsqrt(jnp.mean(h * h, axis=-1, keepdims=True) + EPS)
    hh = h * r
    return hh, r, hh * g


def _rms_bwd(hh, r, g, dn):
    dhh = dn * g
    dx = r * (dhh - hh * jnp.mean(dhh * hh, axis=-1, keepdims=True))
    return dx, jnp.sum(dn * hh, axis=0, keepdims=True)


def _row_ok(i, tm):
    rows = i * tm + lax.broadcasted_iota(jnp.int32, (tm, 1), 0)
    return rows >= PAD


def _const_spec(shape, single=False):
    nd = len(shape)
    if single:
        return pl.BlockSpec(shape, lambda *_: (0,) * nd, pipeline_mode=pl.Buffered(1))
    return pl.BlockSpec(shape, lambda *_: (0,) * nd)


def rowcall(name, body, rows, consts, outs, accs=(), *, tm):
    nrows = rows[0].shape[0]
    nt = nrows // tm
    assert nt * tm == nrows, (name, nrows, tm)
    nr, nc, no, na = len(rows), len(consts), len(outs), len(accs)
    in_specs = [pl.BlockSpec((tm, r.shape[1]), lambda i: (i, 0)) for r in rows]
    in_specs += [_const_spec(c.shape, single=True) for c in consts]
    out_shape = [jax.ShapeDtypeStruct((nrows, w), dt) for (w, dt) in outs]
    out_specs = [pl.BlockSpec((tm, w), lambda i: (i, 0)) for (w, dt) in outs]
    out_shape += [jax.ShapeDtypeStruct(s, F32) for s in accs]
    out_specs += [_const_spec(s) for s in accs]

    def kern(*refs):
        i = pl.program_id(0)
        row_vals = [r[...] for r in refs[:nr]]
        res = body(i, *row_vals, *refs[nr:nr + nc])
        out_refs = refs[nr + nc:nr + nc + no]
        acc_refs = refs[nr + nc + no:]
        for r, v in zip(out_refs, res[:no]):
            r[...] = v.astype(r.dtype)
        if na:
            @pl.when(i == 0)
            def _():
                for r in acc_refs:
                    r[...] = jnp.zeros_like(r)
            for r, v in zip(acc_refs, res[no:]):
                r[...] += v

    res = pl.pallas_call(
        kern, name=name, grid=(nt,), in_specs=in_specs, out_specs=out_specs, out_shape=out_shape,
        compiler_params=_params("arbitrary"),
    )(*rows, *consts)
    return res


def tn_matmul(name, lhs, rhs, scale=1.0):
    M, K = lhs.shape
    N = rhs.shape[1]
    assert lhs.dtype == BF16 and rhs.dtype == BF16
    nm = 6
    tmw = M // nm
    assert tmw * nm == M and tmw % 16 == 0
    tk = 1408 if (K % 1408 == 0) else K
    nk = K // tk

    def kern(a_ref, b_ref, o_ref, acc):
        m = pl.program_id(1)
        part = _tn(a_ref[...], b_ref[...])

        @pl.when(m == 0)
        def _():
            acc[...] = part

        @pl.when((m > 0) & (m < nm - 1))
        def _():
            acc[...] += part

        @pl.when(m == nm - 1)
        def _():
            o_ref[...] = ((acc[...] + part) * scale).astype(o_ref.dtype)

    return pl.pallas_call(
        kern, name=name, grid=(nk, nm),
        in_specs=[pl.BlockSpec((tmw, tk), lambda k, m: (m, k)), pl.BlockSpec((tmw, N), lambda k, m: (m, 0))],
        out_specs=pl.BlockSpec((tk, N), lambda k, m: (k, 0)),
        out_shape=jax.ShapeDtypeStruct((K, N), BF16),
        scratch_shapes=[pltpu.VMEM((tk, N), F32)],
        compiler_params=_params("arbitrary", "arbitrary"),
    )(lhs, rhs)


def _mesh_pos():
    x, y, c = lax.axis_index("x"), lax.axis_index("y"), lax.axis_index("c")
    return x, y, c


def all_gather_pieces(name, groups):
    ng = len(groups)
    packed = [g[0] for g in groups]
    pieces = [g[1] for g in groups]
    out_shape, out_map = [], []
    for gi, (p, pcs) in enumerate(groups):
        idx = []
        for (off, r) in pcs:
            idx.append(len(out_shape))
            out_shape.append(jax.ShapeDtypeStruct((NDEV * r, p.shape[1]), p.dtype))
        out_map.append(idx)
    nout = len(out_shape)

    def body(*refs):
        p_refs = refs[:ng]
        o_refs = refs[ng:ng + nout]
        send_sems, recv_sems, local_sems = refs[ng + nout:]
        x, y, c = _mesh_pos()
        me = (x, y, c)
        sibling = (x, y, 1 - c)
        chips = [(1 - x, y), (x, 1 - y), (1 - x, 1 - y)]

        def blk(px, py, pc):
            return 4 * px + 2 * py + pc

        def copies(gi, k, origin, to, from_out):
            cps = []
            for (off, r), oi in zip(pieces[gi], out_map[gi]):
                dst = o_refs[oi].at[pl.ds(origin * r, r), :]
                src = dst if from_out else p_refs[gi].at[pl.ds(off, r), :]
                cps.append(pltpu.make_async_remote_copy(
                    src_ref=src, dst_ref=dst, send_sem=send_sems.at[gi, k], recv_sem=recv_sems.at[gi, k],
                    device_id=to, device_id_type=MESH))
            return cps

        def whole(gi, k):
            return pltpu.make_async_remote_copy(
                src_ref=p_refs[gi], dst_ref=p_refs[gi], send_sem=send_sems.at[gi, k],
                recv_sem=recv_sems.at[gi, k], device_id=me, device_id_type=MESH)

        mine = []
        for gi in range(ng):
            for (off, r), oi in zip(pieces[gi], out_map[gi]):
                mine.append(pltpu.make_async_copy(
                    p_refs[gi].at[pl.ds(off, r), :], o_refs[oi].at[pl.ds(blk(*me) * r, r), :],
                    local_sems.at[gi]))
        for cp in mine:
            cp.start()
        for gi in range(ng):
            for cp in copies(gi, 0, blk(*me), sibling, False):
                cp.start()
            for j, chip in enumerate(chips):
                for cp in copies(gi, 1 + j, blk(*me), (*chip, c), False):
                    cp.start()
        for j, chip in enumerate(chips):
            for gi in range(ng):
                whole(gi, 1 + j).wait_recv()
                for cp in copies(gi, 4 + j, blk(*chip, c), sibling, True):
                    cp.start()
        for gi in range(ng):
            whole(gi, 0).wait_recv()
            for j in range(3):
                whole(gi, 4 + j).wait_recv()
        for gi in range(ng):
            for k in range(7):
                whole(gi, k).wait_send()
            pltpu.make_async_copy(p_refs[gi], p_refs[gi], local_sems.at[gi]).wait()

    any_spec = pl.BlockSpec(memory_space=pl.ANY)
    outs = pl.pallas_call(
        body, name=name, out_shape=out_shape,
        in_specs=[any_spec] * ng, out_specs=[any_spec] * nout,
        scratch_shapes=[pltpu.SemaphoreType.DMA((ng, 7)), pltpu.SemaphoreType.DMA((ng, 7)),
                        pltpu.SemaphoreType.DMA((ng,))],
    )(*packed)
    return [[outs[oi] for oi in idx] for idx in out_map]


HBM_SPEC = pl.BlockSpec(memory_space=pltpu.HBM)
SEM_SPEC = pl.BlockSpec(memory_space=pltpu.SEMAPHORE)
DATAFLOW = pltpu.SideEffectType.DATAFLOW_SIDE_EFFECTING


def _peers(x, y, c):
    return [(x, y, 1 - c), (1 - x, y, c), (x, 1 - y, c), (1 - x, 1 - y, c),
            (1 - x, y, 1 - c), (x, 1 - y, 1 - c), (1 - x, 1 - y, 1 - c)]


def exchange_start(name, arrays, ng, plan):
    n = len(arrays)
    ns = ng * 7

    def body(*refs):
        in_refs = refs[:n]
        send_sems, recv_sems = refs[n:n + ns], refs[n + ns:n + 2 * ns]
        token = refs[-1]
        x, y, c = _mesh_pos()
        me_i = 4 * x + 2 * y + c
        for k, peer in enumerate(_peers(x, y, c)):
            p_i = 4 * peer[0] + 2 * peer[1] + peer[2]
            for src, dst, gi in plan(in_refs, me_i, p_i):
                pltpu.make_async_remote_copy(
                    src_ref=src, dst_ref=dst, send_sem=send_sems[gi * 7 + k], recv_sem=recv_sems[gi * 7 + k],
                    device_id=peer, device_id_type=MESH).start()
        token[...] = jnp.zeros_like(token)

    res = pl.pallas_call(
        body, name=name,
        out_shape=(*[pltpu.SemaphoreType.DMA(())] * (2 * ns),
                   *[pltpu.HBM(a.shape, a.dtype) for a in arrays], jax.ShapeDtypeStruct((8, 128), F32)),
        in_specs=[HBM_SPEC] * n,
        out_specs=(*[SEM_SPEC] * (2 * ns), *[HBM_SPEC] * n, pl.BlockSpec(memory_space=pltpu.VMEM)),
        input_output_aliases={i: 2 * ns + i for i in range(n)},
        compiler_params=pltpu.CompilerParams(has_side_effects=DATAFLOW),
    )(*[pltpu.with_memory_space_constraint(a, pltpu.HBM) for a in arrays])
    return list(res[:ns]), list(res[ns:2 * ns]), list(res[2 * ns:2 * ns + n]), res[-1]


def exchange_wait(name, send_sems, recv_sems, arrays, ng, sized, after):
    n = len(arrays)
    ns = ng * 7

    def body(*refs):
        in_refs = refs[:n]
        s_sems, r_sems = refs[n:n + ns], refs[n + ns:n + 2 * ns]
        x, y, c = _mesh_pos()
        for gi in range(ng):
            view = sized(in_refs, gi)
            for k in range(7):
                w = pltpu.make_async_remote_copy(
                    src_ref=view, dst_ref=view, send_sem=s_sems[gi * 7 + k], recv_sem=r_sems[gi * 7 + k],
                    device_id=(x, y, c), device_id_type=MESH)
                w.wait_send()
                w.wait_recv()

    res = pl.pallas_call(
        body, name=name, out_shape=tuple(pltpu.HBM(a.shape, a.dtype) for a in arrays),
        in_specs=[HBM_SPEC] * n + [SEM_SPEC] * (2 * ns) + [pl.BlockSpec(memory_space=pl.ANY)],
        out_specs=tuple([HBM_SPEC] * n), input_output_aliases={i: i for i in range(n)},
        compiler_params=pltpu.CompilerParams(has_side_effects=DATAFLOW),
    )(*arrays, *send_sems, *recv_sems, after)
    return list(res)


def gather_layer_start(name, packed, pieces):
    ng = len(packed)
    dests = [lax.empty((NDEV * r, p.shape[1]), p.dtype) for p, pcs in zip(packed, pieces) for (_, r) in pcs]

    def plan(refs, me_i, p_i):
        out, di = [], ng
        for gi in range(ng):
            for (off, r) in pieces[gi]:
                out.append((refs[gi].at[pl.ds(off, r), :], refs[di].at[pl.ds(me_i * r, r), :], gi))
                di += 1
        return out

    return exchange_start(name, list(packed) + dests, ng, plan)


def gather_layer_wait(name, handle, ng, after):
    send_sems, recv_sems, arrays, _ = handle
    out = exchange_wait(name, send_sems, recv_sems, arrays, ng, lambda refs, gi: refs[gi], after)
    return out[ng:]


def scatter_layer_start(name, groups):
    ng = len(groups)
    flat = [a for arrs in groups for a in arrs]
    offs, lands = [], []
    for arrs in groups:
        o, off = [], 0
        for a in arrs:
            r = a.shape[0] // NDEV
            o.append((off, r))
            off += r
        offs.append(o)
        lands.append(jnp.zeros((NDEV, off, arrs[0].shape[1]), arrs[0].dtype))
    nin = len(flat)

    def plan(refs, me_i, p_i):
        out, ai = [], 0
        for gi in range(ng):
            for (off, r) in offs[gi]:
                out.append((refs[ai].at[pl.ds(p_i * r, r), :], refs[nin + gi].at[me_i, pl.ds(off, r), :], gi))
                ai += 1
        return out

    return exchange_start(name, flat + lands, ng, plan), nin


def scatter_layer_wait(name, handle, nin, ng, after):
    send_sems, recv_sems, arrays, _ = handle
    out = exchange_wait(name, send_sems, recv_sems, arrays, ng, lambda refs, gi: refs[nin + gi].at[0], after)
    return out[nin:]


def _pick_tile(n, cap):
    best = None
    for t in range(8, min(n, cap) + 1, 8):
        if n % t == 0:
            best = t
    return best if best is not None else n


def sum_slots(name, land):
    _, R, W = land.shape
    tr = _pick_tile(R, 512)

    def kern(l_ref, o_ref):
        acc = l_ref[0].astype(F32)
        for s in range(1, NDEV):
            acc = acc + l_ref[s].astype(F32)
        o_ref[...] = acc

    return pl.pallas_call(
        kern, name=name, grid=(R // tr,),
        in_specs=[pl.BlockSpec((NDEV, tr, W), lambda i: (0, i, 0))],
        out_specs=pl.BlockSpec((tr, W), lambda i: (i, 0)),
        out_shape=jax.ShapeDtypeStruct((R, W), F32),
        compiler_params=_params("arbitrary"),
    )(land)


def adamw(name, w, g, m, v):
    shp = w.shape
    C = shp[-1]
    R = max(1, math.prod(shp[:-1]))
    tr = _pick_tile(R, 1024)
    w2, g2, m2, v2 = (a.reshape(R, C) for a in (w, g, m, v))

    def kern(w_ref, g_ref, m_ref, v_ref, d_ref, mo_ref, vo_ref):
        gg = g_ref[...]
        mn = ADAM_B1 * m_ref[...] + (1.0 - ADAM_B1) * gg
        vn = ADAM_B2 * v_ref[...] + (1.0 - ADAM_B2) * jnp.square(gg)
        m_hat = mn / (1.0 - ADAM_B1 ** ADAM_STEP)
        v_hat = vn / (1.0 - ADAM_B2 ** ADAM_STEP)
        d_ref[...] = -ADAM_LR * (m_hat / (jnp.sqrt(v_hat) + ADAM_EPS) + ADAM_WD * w_ref[...])
        mo_ref[...] = mn
        vo_ref[...] = vn

    spec = pl.BlockSpec((tr, C), lambda i: (i, 0))
    d, mo, vo = pl.pallas_call(
        kern, name=name, grid=(R // tr,), in_specs=[spec] * 4, out_specs=[spec] * 3,
        out_shape=[jax.ShapeDtypeStruct((R, C), F32)] * 3, compiler_params=_params("arbitrary"),
    )(w2, g2, m2, v2)
    return d.reshape(shp), mo.reshape(shp), vo.reshape(shp)


def build_h0(x2, blk0):
    L0 = x2.shape[0]
    nb = L0 // BLK + 1

    def kern(x_ref, b_ref, o_ref):
        i = pl.program_id(0)

        @pl.when(i == 0)
        def _():
            o_ref[...] = b_ref[...]

        @pl.when(i > 0)
        def _():
            o_ref[...] = x_ref[...]

    return pl.pallas_call(
        kern, name="build_h0", grid=(nb,),
        in_specs=[pl.BlockSpec((BLK, D), lambda i: (jnp.maximum(i - 1, 0), 0)), _const_spec((BLK, D))],
        out_specs=pl.BlockSpec((BLK, D), lambda i: (i, 0)),
        out_shape=jax.ShapeDtypeStruct((L0 + BLK, D), F32), compiler_params=_params("arbitrary"),
    )(x2, blk0)


def final_loss(h, tgt, gf):
    LP = h.shape[0]
    nb = LP // BLK

    def kern(h_ref, t_ref, g_ref, dh_ref, loss_ref, dg_ref):
        i = pl.program_id(0)

        @pl.when(i == 0)
        def _():
            loss_ref[...] = jnp.zeros_like(loss_ref)
            dg_ref[...] = jnp.zeros_like(dg_ref)

        g = g_ref[...]
        hh, r, yv = _rms_fwd(h_ref[...], g)
        valid = (i > 0).astype(F32)
        err = (yv - t_ref[...]) * valid
        loss_ref[...] += 0.5 * jnp.sum(jnp.sum(err * err, axis=1, keepdims=True), axis=0, keepdims=True) / D
        dy = err / D
        dx, dg = _rms_bwd(hh, r, g, dy)
        dh_ref[...] = dx
        dg_ref[...] += dg

    return pl.pallas_call(
        kern, name="final_loss", grid=(nb,),
        in_specs=[pl.BlockSpec((BLK, D), lambda i: (i, 0)),
                  pl.BlockSpec((BLK, D), lambda i: (jnp.maximum(i - 1, 0), 0)), _const_spec((1, D))],
        out_specs=[pl.BlockSpec((BLK, D), lambda i: (i, 0)), _const_spec((8, 128)), _const_spec((1, D))],
        out_shape=[jax.ShapeDtypeStruct((LP, D), F32), jax.ShapeDtypeStruct((8, 128), F32),
                   jax.ShapeDtypeStruct((1, D), F32)],
        compiler_params=_params("arbitrary"),
    )(h, tgt, gf)


def _tall_tile(nrows, tm):
    t = nrows // 24
    return t if (t * 24 == nrows and t % 16 == 0 and t > tm) else tm


def ffn_forward(tag, h, g, wgT, wuT, wd, tm):
    def f1(i, hv, g_ref, wg_ref, wu_ref):
        _, _, n = _rms_fwd(hv, g_ref[...])
        nb = n.astype(BF16)
        G = _nt(nb, wg_ref[...])
        U = _nt(nb, wu_ref[...])
        A = G * _sig(G) * U
        return nb, G, U, A

    n, G, U, A = rowcall(tag + "_up", f1, [h], [g, wgT, wuT],
                         [(D, BF16), (DFF, BF16), (DFF, BF16), (DFF, BF16)], tm=tm)

    def f2(i, av, hv, wd_ref):
        return (hv + 0.5 * _nn(av, wd_ref[...]),)

    (h2,) = rowcall(tag + "_down", f2, [A, h], [wd], [(D, F32)], tm=_tall_tile(h.shape[0], tm))
    return h2, (h, n, G, U, A)


def ffn_backward(tag, dh, saved, g, wgT, wuT, wd, tm, emit=None):
    h, n, G, U, A = saved

    def b1(i, dhv, wd_ref):
        dyb = (0.5 * dhv).astype(BF16)
        return _nt(dyb, wd_ref[...]), dyb

    dA, dyb = rowcall(tag + "_bwd_act", b1, [dh], [wd], [(DFF, BF16), (D, BF16)], tm=_tall_tile(h.shape[0], tm))
    dwd = tn_matmul(tag + "_dwd", A, dyb)
    if emit is not None:
        emit("wd", dwd)

    def b2(i, dAv, Gv, Uv, hv, dhv, g_ref, wg_ref, wu_ref):
        dAf = dAv.astype(F32)
        Gf = Gv.astype(F32)
        sg = _sig(Gf)
        dG = (dAf * Uv.astype(F32) * (sg * (1.0 + Gf * (1.0 - sg)))).astype(BF16)
        dU = (dAf * (Gf * sg)).astype(BF16)
        dn = _nn(dG, wg_ref[...]) + _nn(dU, wu_ref[...])
        gv = g_ref[...]
        hh, r, _ = _rms_fwd(hv, gv)
        dx, dg = _rms_bwd(hh, r, gv, dn)
        dx = jnp.where(_row_ok(i, tm), dx, 0.0)
        return dhv + dx, dG, dU, dg

    dh2, dG, dU, dg = rowcall(tag + "_bwd_in", b2, [dA, G, U, h, dh], [g, wgT, wuT],
                              [(D, F32), (DFF, BF16), (DFF, BF16)], [(1, D)], tm=tm)
    dwgT = tn_matmul(tag + "_dwg", dG, n)
    if emit is not None:
        emit("wgT", dwgT)
    dwuT = tn_matmul(tag + "_dwu", dU, n)
    if emit is not None:
        emit("wuT", dwuT)
    return dh2, dg, dwgT, dwuT, dwd


def _alibi_slope(head):
    return float(2.0 ** (-8.0 * (head + 1) / N_HEADS))


def _att_bias(n, nb):
    qi = lax.broadcasted_iota(jnp.int32, (BLK, 4 * BLK), 0)
    cj = lax.broadcasted_iota(jnp.int32, (BLK, 4 * BLK), 1)
    jb = cj - BLK
    dist = jnp.abs(qi + BLK - jb)
    kpos = (n - 1) * BLK + jb
    band_ok = (dist <= WIN) & (kpos >= BLK) & (kpos < nb * BLK)
    is_meta = cj < BLK
    ok = (is_meta & (cj >= PAD)) | (jnp.logical_not(is_meta) & band_ok)
    distf = jnp.where(is_meta, 0, dist).astype(F32)
    maskadd = jnp.where(ok, 0.0, NEG).astype(F32)
    distf4 = jnp.concatenate([distf] * QG, axis=0)
    mask4 = jnp.concatenate([maskadd] * QG, axis=0)
    return distf4, mask4


def _group_col(vals):
    rg = lax.broadcasted_iota(jnp.int32, (QG * BLK, 1), 0) // BLK
    col = jnp.full((QG * BLK, 1), vals[QG - 1], F32)
    for gq in range(QG - 2, -1, -1):
        col = jnp.where(rg == gq, vals[gq], col)
    return col


def _stack_heads(ref_or_val, kh):
    return jnp.concatenate(
        [ref_or_val[:, (kh * QG + gq) * HD:(kh * QG + gq + 1) * HD] for gq in range(QG)], axis=0)


def _stack_keys(km, kp, kc, kn, kh):
    sl = slice(kh * HD, (kh + 1) * HD)
    return jnp.concatenate([km[:, sl], kp[:, sl], kc[:, sl], kn[:, sl]], axis=0)


LOG2E = 1.4426950408889634
LN2 = 0.6931471805599453
QSCALE = SCALE * LOG2E


def _att_update_bias(bias_ref, n, nb):
    @pl.when((n <= 2) | (n == nb - 1))
    def _():
        distf4, mask4 = _att_bias(n, nb)
        for kh in range(N_KV):
            slope_col = _group_col([_alibi_slope(kh * QG + gq) * LOG2E for gq in range(QG)])
            bias_ref[kh] = mask4 - slope_col * distf4


def _att_exp(qs, kb, bias_ref, kh, sink_ref):
    sink_col = _group_col([sink_ref[kh * QG + gq] for gq in range(QG)]) * LOG2E
    s = _nt(qs, kb) + bias_ref[kh]
    m = jnp.maximum(jnp.max(s, axis=1, keepdims=True), sink_col)
    e = jnp.exp2(s - m)
    es = jnp.exp2(sink_col - m)
    inv = 1.0 / (jnp.sum(e, axis=1, keepdims=True) + es)
    return e, es, inv


def attention_forward(tag, q, k, v, sink):
    LP = q.shape[0]
    nb = LP // BLK

    def kern(sink_ref, q_ref, km_ref, kp_ref, kc_ref, kn_ref, vm_ref, vp_ref, vc_ref, vn_ref, o_ref, bias_ref):
        n = pl.program_id(0)
        _att_update_bias(bias_ref, n, nb)
        qv = q_ref[...]
        km, kp, kc, kn = km_ref[...], kp_ref[...], kc_ref[...], kn_ref[...]
        vm, vp, vc, vn = vm_ref[...], vp_ref[...], vc_ref[...], vn_ref[...]
        for kh in range(N_KV):
            qs = _stack_heads(qv, kh)
            kb = _stack_keys(km, kp, kc, kn, kh)
            vb = _stack_keys(vm, vp, vc, vn, kh)
            e, _, inv = _att_exp(qs, kb, bias_ref, kh, sink_ref)
            o = _nn(e.astype(BF16), vb) * inv
            for gq in range(QG):
                hcol = (kh * QG + gq) * HD
                o_ref[:, hcol:hcol + HD] = o[gq * BLK:(gq + 1) * BLK].astype(o_ref.dtype)

    def kvspec(dn):
        return pl.BlockSpec((BLK, N_KV * HD), lambda n: (jnp.clip(n + dn, 0, nb - 1), 0))

    meta_spec = pl.BlockSpec((BLK, N_KV * HD), lambda n: (0, 0))
    return pl.pallas_call(
        kern, name=tag + "_att_fwd", grid=(nb,),
        in_specs=[pl.BlockSpec(memory_space=pltpu.SMEM), pl.BlockSpec((BLK, D), lambda n: (n, 0)),
                  meta_spec, kvspec(-1), kvspec(0), kvspec(1), meta_spec, kvspec(-1), kvspec(0), kvspec(1)],
        out_specs=pl.BlockSpec((BLK, D), lambda n: (n, 0)),
        out_shape=jax.ShapeDtypeStruct((LP, D), BF16),
        scratch_shapes=[pltpu.VMEM((N_KV, QG * BLK, 4 * BLK), F32)], compiler_params=_params("arbitrary"),
    )(sink, q, k, k, k, k, v, v, v, v)


def attention_backward(tag, q, k, v, do, sink):
    LP = q.shape[0]
    nb = LP // BLK
    KW = N_KV * HD

    def kern(sink_ref, q_ref, do_ref, km_ref, kp_ref, kc_ref, kn_ref, vm_ref, vp_ref, vc_ref, vn_ref,
             dq_ref, dk_ref, dv_ref, dkm_ref, dvm_ref, dsink_ref, bias_ref, rk, rv, fk, fv):
        n = pl.program_id(0)

        @pl.when(n == 0)
        def _():
            dkm_ref[...] = jnp.zeros_like(dkm_ref)
            dvm_ref[...] = jnp.zeros_like(dvm_ref)
            dsink_ref[...] = jnp.zeros_like(dsink_ref)
            rk[...] = jnp.zeros_like(rk)
            rv[...] = jnp.zeros_like(rv)

        _att_update_bias(bias_ref, n, nb)

        @pl.when(n < nb)
        def _():
            qv, dov = q_ref[...], do_ref[...]
            km, kp, kc, kn = km_ref[...], kp_ref[...], kc_ref[...], kn_ref[...]
            vm, vp, vc, vn = vm_ref[...], vp_ref[...], vc_ref[...], vn_ref[...]
            lane = lax.broadcasted_iota(jnp.int32, (8, 128), 1)
            dsink = jnp.zeros((8, 128), F32)
            for kh in range(N_KV):
                qs = _stack_heads(qv, kh)
                dos = _stack_heads(dov, kh)
                kb = _stack_keys(km, kp, kc, kn, kh)
                vb = _stack_keys(vm, vp, vc, vn, kh)
                dp = _nt(dos, vb)
                e, es, inv = _att_exp(qs, kb, bias_ref, kh, sink_ref)
                delta = inv * jnp.sum(e * dp, axis=1, keepdims=True)
                dsu = (e * (dp - delta)).astype(BF16)
                dqs = _nn(dsu, kb) * (inv * SCALE)
                dkt = _tn((qs.astype(F32) * (inv * LN2)).astype(BF16), dsu)
                dvt = _tn((dos.astype(F32) * inv).astype(BF16), e.astype(BF16))
                dsk = -(es * inv * delta)
                for gq in range(QG):
                    hcol = (kh * QG + gq) * HD
                    dq_ref[:, hcol:hcol + HD] = dqs[gq * BLK:(gq + 1) * BLK].astype(dq_ref.dtype)
                    tot = jnp.sum(dsk[gq * BLK:(gq + 1) * BLK], axis=0, keepdims=True)
                    dsink = dsink + jnp.where(lane == kh * QG + gq, tot, 0.0)
                hs = slice(kh * HD, (kh + 1) * HD)
                dkm_ref[hs, :] += dkt[:, 0:BLK]
                dvm_ref[hs, :] += dvt[:, 0:BLK]
                for ring, fin, part in ((rk, fk, dkt), (rv, fv, dvt)):
                    fin[hs, :] = ring[0, hs, :] + part[:, BLK:2 * BLK]
                    ring[0, hs, :] = ring[1, hs, :] + part[:, 2 * BLK:3 * BLK]
                    ring[1, hs, :] = part[:, 3 * BLK:4 * BLK]
            dsink_ref[...] += dsink
            dk_ref[...] = fk[...].T.astype(dk_ref.dtype)
            dv_ref[...] = fv[...].T.astype(dv_ref.dtype)

        @pl.when(n == nb)
        def _():
            dk_ref[...] = rk[0].T.astype(dk_ref.dtype)
            dv_ref[...] = rv[0].T.astype(dv_ref.dtype)

    def kvspec(dn):
        return pl.BlockSpec((BLK, KW), lambda n: (jnp.clip(jnp.minimum(n, nb - 1) + dn, 0, nb - 1), 0))

    meta_spec = pl.BlockSpec((BLK, KW), lambda n: (0, 0))
    rowspec = pl.BlockSpec((BLK, D), lambda n: (jnp.minimum(n, nb - 1), 0))
    emit_spec = pl.BlockSpec((BLK, KW), lambda n: (jnp.clip(n - 1, 1, nb - 1), 0))
    dq, dk, dv, dkm, dvm, dsink = pl.pallas_call(
        kern, name=tag + "_att_bwd", grid=(nb + 1,),
        in_specs=[pl.BlockSpec(memory_space=pltpu.SMEM), rowspec, rowspec,
                  meta_spec, kvspec(-1), kvspec(0), kvspec(1), meta_spec, kvspec(-1), kvspec(0), kvspec(1)],
        out_specs=[rowspec, emit_spec, emit_spec, _const_spec((KW, BLK)), _const_spec((KW, BLK)),
                   _const_spec((8, 128))],
        out_shape=[jax.ShapeDtypeStruct((LP, D), BF16), jax.ShapeDtypeStruct((LP, KW), BF16),
                   jax.ShapeDtypeStruct((LP, KW), BF16), jax.ShapeDtypeStruct((KW, BLK), F32),
                   jax.ShapeDtypeStruct((KW, BLK), F32), jax.ShapeDtypeStruct((8, 128), F32)],
        scratch_shapes=[pltpu.VMEM((N_KV, QG * BLK, 4 * BLK), F32), pltpu.VMEM((2, KW, BLK), F32),
                        pltpu.VMEM((2, KW, BLK), F32), pltpu.VMEM((KW, BLK), F32), pltpu.VMEM((KW, BLK), F32)],
        compiler_params=_params("arbitrary"),
    )(sink, q, do, k, k, k, k, v, v, v, v)
    dk = lax.dynamic_update_slice(dk, dkm.T.astype(BF16), (0, 0))
    dv = lax.dynamic_update_slice(dv, dvm.T.astype(BF16), (0, 0))
    return dq, dk, dv, dsink


SCAN_LANES = 1024


def _scan_tile(xr, xi, cr, ci, a8, tab, seg, reverse):
    sub = lax.broadcasted_iota(jnp.int32, (8, SCAN_LANES), 0)
    for c0 in range(0, NST, SCAN_LANES):
        cs = pl.ds(c0, SCAN_LANES)
        ar = a8[0, :, cs]
        ai = a8[1, :, cs]

        def rows(j):
            jj = (seg - 1 - j) if reverse else j
            return pl.ds(jj * 8, 8)

        def step1(j, carry):
            vr, vi = carry
            rs = rows(j)
            nr = ar * vr - ai * vi + xr[rs, cs]
            ni = ar * vi + ai * vr + xi[rs, cs]
            xr[rs, cs] = nr
            xi[rs, cs] = ni
            return nr, ni

        zero = jnp.zeros((8, SCAN_LANES), F32)
        vr, vi = zero, zero
        for j in range(seg):
            vr, vi = step1(j, (vr, vi))
        for t, s in enumerate((1, 2, 4)):
            sh = (8 - s) if reverse else s
            sr = pltpu.roll(vr, sh, 0)
            si = pltpu.roll(vi, sh, 0)
            tr = tab[2 * t, :, cs]
            ti = tab[2 * t + 1, :, cs]
            vr, vi = vr + tr * sr - ti * si, vi + tr * si + ti * sr
        pr = tab[6, :, cs]
        pi = tab[7, :, cs]
        c_r = cr[:, cs]
        c_i = ci[:, cs]
        vr, vi = vr + pr * c_r - pi * c_i, vi + pr * c_i + pi * c_r
        edge = 7 if reverse else 0
        last = 0 if reverse else 7
        sh = 7 if reverse else 1
        in_r = jnp.where(sub == edge, c_r, pltpu.roll(vr, sh, 0))
        in_i = jnp.where(sub == edge, c_i, pltpu.roll(vi, sh, 0))
        cr[:, cs] = jnp.broadcast_to(vr[last:last + 1, :], (8, SCAN_LANES))
        ci[:, cs] = jnp.broadcast_to(vi[last:last + 1, :], (8, SCAN_LANES))

        def step2(j, carry):
            dr, di = carry
            rs = rows(j)
            ndr = ar * dr - ai * di
            ndi = ar * di + ai * dr
            xr[rs, cs] += ndr
            xi[rs, cs] += ndi
            return ndr, ndi

        dr, di = in_r, in_i
        for j in range(seg):
            dr, di = step2(j, (dr, di))


ST_T = 4 * SP * 2
CH_T = 128


def _load_segmented(ref, scr, seg):
    out = []
    for ct in range(4):
        scr[ct] = ref[:, ct * CH_T:(ct + 1) * CH_T]
        out.append(jnp.concatenate([scr[ct, pl.ds(j, 8, stride=seg), :] for j in range(seg)], axis=0))
    return out


def _store_segmented(ref, scr, vals, seg):
    for ct in range(4):
        for j in range(seg):
            scr[ct, pl.ds(j, 8, stride=seg), :] = vals[ct][8 * j:8 * j + 8]
        ref[:, ct * CH_T:(ct + 1) * CH_T] = scr[ct].astype(ref.dtype)


def ssm_dir_forward(tag, u, bpr, bpi, cpr, cpi, a8, tab, reverse, tm):
    LP = u.shape[0]
    nt = LP // tm
    seg = tm // 8

    def rix(i):
        return (nt - 1 - i) if reverse else i

    def kern(u_ref, bpr_ref, bpi_ref, cpr_ref, cpi_ref, a8_ref, tab_ref, xre_ref, xim_ref, y_ref,
             xr, xi, ys, cr, ci):
        i = pl.program_id(0)

        @pl.when(i == 0)
        def _():
            cr[...] = jnp.zeros_like(cr)
            ci[...] = jnp.zeros_like(ci)

        ub = _load_segmented(u_ref, ys, seg)
        for ct in range(4):
            uc = ub[ct].astype(BF16)
            xr[:, ct * ST_T:(ct + 1) * ST_T] = _nn(uc, bpr_ref[ct * CH_T:(ct + 1) * CH_T, :])
            xi[:, ct * ST_T:(ct + 1) * ST_T] = _nn(uc, bpi_ref[ct * CH_T:(ct + 1) * CH_T, :])
        _scan_tile(xr, xi, cr, ci, a8_ref, tab_ref, seg, reverse)
        xrb = xr[...].astype(BF16)
        xib = xi[...].astype(BF16)
        xre_ref[...] = xrb
        xim_ref[...] = xib
        yv = []
        for ct in range(4):
            ss = slice(ct * ST_T, (ct + 1) * ST_T)
            yv.append(_nn(xrb[:, ss], cpr_ref[ss, :]) - _nn(xib[:, ss], cpi_ref[ss, :]))
        _store_segmented(y_ref, ys, yv, seg)

    row = lambda w: pl.BlockSpec((tm, w), lambda i: (rix(i), 0))
    return pl.pallas_call(
        kern, name=tag, grid=(nt,),
        in_specs=[row(SW), _const_spec(bpr.shape), _const_spec(bpi.shape), _const_spec(cpr.shape),
                  _const_spec(cpi.shape), _const_spec(a8.shape), _const_spec(tab.shape)],
        out_specs=[row(NST), row(NST), row(SW)],
        out_shape=[jax.ShapeDtypeStruct((LP, NST), BF16), jax.ShapeDtypeStruct((LP, NST), BF16),
                   jax.ShapeDtypeStruct((LP, SW), F32)],
        scratch_shapes=[pltpu.VMEM((tm, NST), F32), pltpu.VMEM((tm, NST), F32), pltpu.VMEM((4, tm, CH_T), F32),
                        pltpu.VMEM((8, NST), F32), pltpu.VMEM((8, NST), F32)],
        compiler_params=_params("arbitrary"),
    )(u, bpr, bpi, cpr, cpi, a8, tab)


def ssm_dir_backward(tag, dy, xre, xim, u, bpr, bpi, cpr, cpi, a8_adj, tab_adj, reverse, tm):
    LP = u.shape[0]
    nt = LP // tm
    seg = tm // 8

    def rix(i):
        return (nt - 1 - i) if reverse else i

    def kern(dy_ref, xre_ref, xim_ref, u_ref, bpr_ref, bpi_ref, cpr_ref, cpi_ref, a8_ref, tab_ref,
             du_ref, gbr_ref, gbi_ref, gcr_ref, gci_ref, sr_ref, si_ref, lr, li, gr, gi, dus, cr, ci):
        i = pl.program_id(0)

        @pl.when(i == 0)
        def _():
            cr[...] = jnp.zeros_like(cr)
            ci[...] = jnp.zeros_like(ci)
            for r in (gbr_ref, gbi_ref, gcr_ref, gci_ref, sr_ref, si_ref):
                r[...] = jnp.zeros_like(r)

        dyb = [v.astype(BF16) for v in _load_segmented(dy_ref, dus, seg)]
        ub = [v.astype(BF16) for v in _load_segmented(u_ref, dus, seg)]
        for ct in range(4):
            ss = slice(ct * ST_T, (ct + 1) * ST_T)
            dc = dyb[ct]
            g_re = _nt(dc, cpr_ref[ss, :])
            g_im = -_nt(dc, cpi_ref[ss, :])
            lr[:, ss] = g_re
            li[:, ss] = g_im
            gr[:, ss] = g_re
            gi[:, ss] = g_im
        _scan_tile(lr, li, cr, ci, a8_ref, tab_ref, seg, reverse)
        lam_r = lr[...]
        lam_i = li[...]
        wr = lam_r - gr[...]
        wi = lam_i - gi[...]
        xr = xre_ref[...].astype(F32)
        xi = xim_ref[...].astype(F32)
        sr_ref[...] += jnp.sum(wr * xr + wi * xi, axis=0, keepdims=True)
        si_ref[...] += jnp.sum(wi * xr - wr * xi, axis=0, keepdims=True)
        lrb = lam_r.astype(BF16)
        lib = lam_i.astype(BF16)
        xrb = xre_ref[...]
        xib = xim_ref[...]
        duv = []
        for ct in range(4):
            ss = slice(ct * ST_T, (ct + 1) * ST_T)
            cs = slice(ct * CH_T, (ct + 1) * CH_T)
            duv.append(_nt(lrb[:, ss], bpr_ref[cs, :]) + _nt(lib[:, ss], bpi_ref[cs, :]))
            gbr_ref[cs, :] += _tn(ub[ct], lrb[:, ss])
            gbi_ref[cs, :] += _tn(ub[ct], lib[:, ss])
            gcr_ref[cs, :] += _tn(dyb[ct], xrb[:, ss])
            gci_ref[cs, :] -= _tn(dyb[ct], xib[:, ss])
        _store_segmented(du_ref, dus, duv, seg)

    row = lambda w: pl.BlockSpec((tm, w), lambda i: (rix(i), 0))
    acc = _const_spec((SW, ST_T))
    vec = _const_spec((1, NST))
    return pl.pallas_call(
        kern, name=tag, grid=(nt,),
        in_specs=[row(SW), row(NST), row(NST), row(SW), _const_spec(bpr.shape), _const_spec(bpi.shape),
                  _const_spec(cpr.shape), _const_spec(cpi.shape), _const_spec(a8_adj.shape),
                  _const_spec(tab_adj.shape)],
        out_specs=[row(SW), acc, acc, acc, acc, vec, vec],
        out_shape=[jax.ShapeDtypeStruct((LP, SW), BF16)] + [jax.ShapeDtypeStruct((SW, ST_T), F32)] * 4
        + [jax.ShapeDtypeStruct((1, NST), F32)] * 2,
        scratch_shapes=[pltpu.VMEM((tm, NST), F32)] * 4 + [pltpu.VMEM((4, tm, CH_T), F32)]
        + [pltpu.VMEM((8, NST), F32)] * 2,
        compiler_params=_params("arbitrary"),
    )(dy, xre, xim, u, bpr, bpi, cpr, cpi, a8_adj, tab_adj)


def _ssm_disc(lam_re, lam_im, log_dt, b_re, b_im):
    dt = jnp.exp(log_dt)[:, None]
    mag = jnp.exp(lam_re * dt)
    a_re = mag * jnp.cos(lam_im * dt)
    a_im = mag * jnp.sin(lam_im * dt)
    den = lam_re * lam_re + lam_im * lam_im
    f_re = ((a_re - 1.0) * lam_re + a_im * lam_im) / den
    f_im = (a_im * lam_re - (a_re - 1.0) * lam_im) / den
    bb_re = f_re[:, :, None] * b_re - f_im[:, :, None] * b_im
    bb_im = f_re[:, :, None] * b_im + f_im[:, :, None] * b_re
    return a_re, a_im, bb_re, bb_im


def _scan_tables(lam_re, lam_im, log_dt, conj, reverse, seg):
    dt = jnp.exp(log_dt)[:, None]
    lr = (lam_re * dt).reshape(1, NST)
    li = (lam_im * dt).reshape(1, NST) * (-1.0 if conj else 1.0)
    t = jnp.arange(8, dtype=F32)[:, None]

    def power(kk):
        mag = jnp.exp(kk * lr)
        return mag * jnp.cos(kk * li), mag * jnp.sin(kk * li)

    ones = jnp.ones((8, 1), F32)
    a8 = jnp.stack(power(ones)).astype(F32)
    tabs = []
    for s in (1, 2, 4):
        mask = (t <= 7 - s) if reverse else (t >= s)
        pr, pi = power(float(s * seg) * ones)
        tabs += [jnp.where(mask, pr, 0.0), jnp.where(mask, pi, 0.0)]
    kk = ((8.0 - t) if reverse else (t + 1.0)) * float(seg)
    pr, pi = power(kk)
    tabs += [pr, pi]
    return a8, jnp.stack(tabs).astype(F32)


def _pack_b(bb):
    t = bb.transpose(0, 2, 1).reshape(4, 8, SCH, SP)
    eye = jnp.eye(8, dtype=bb.dtype)
    return jnp.einsum('tgcp,gh->tgchp', t, eye).reshape(SW, ST_T)


def _pack_c(cc):
    t = cc.transpose(0, 2, 1).reshape(4, 8, SP, SCH)
    eye = jnp.eye(8, dtype=cc.dtype)
    return jnp.einsum('tgpc,gh->tgphc', t, eye).reshape(NST, CH_T)


def _unpack_diag(acc):
    t = acc.reshape(4, 8, SCH, 8, SP)
    eye = jnp.eye(8, dtype=acc.dtype)
    return jnp.einsum('tgchp,gh->tgcp', t, eye).reshape(SGRP, SCH, SP)


def _gelu(y):
    k0 = math.sqrt(2.0 / math.pi)
    inner = k0 * (y + 0.044715 * y * y * y)
    th = jnp.tanh(inner)
    z = 0.5 * y * (1.0 + th)
    dz = 0.5 * (1.0 + th) + 0.5 * y * (1.0 - th * th) * k0 * (1.0 + 3.0 * 0.044715 * y * y)
    return z, dz


Q0, K0, V0, U0, GS0, GA0, IN_COLS = 0, 1024, 1280, 1536, 2048, 3072, 4096


def mixer_forward(tag, h, p, tm):
    g, winT, wglu, wbsT, wba, wout = p["g"], p["winT"], p["wglu"], p["wbsT"], p["wba"], p["wout"]

    def proj(i, hv, g_ref, w_ref):
        _, _, n = _rms_fwd(hv, g_ref[...])
        nb = n.astype(BF16)
        return (nb, _nt(nb, w_ref[Q0:K0, :]) * QSCALE, _nt(nb, w_ref[K0:V0, :]), _nt(nb, w_ref[V0:U0, :]),
                _nt(nb, w_ref[U0:GS0, :]), _nt(nb, w_ref[GS0:GA0, :]), _nt(nb, w_ref[GA0:IN_COLS, :]))

    n, q, k, v, u, gs, ga = rowcall(
        tag + "_proj", proj, [h], [g, winT],
        [(D, BF16), (D, BF16), (N_KV * HD, BF16), (N_KV * HD, BF16), (SW, F32), (D, BF16), (D, BF16)],
        tm=_tall_tile(h.shape[0], tm))

    ya = attention_forward(tag, q, k, v, p["sink"])

    states, ydir = [], []
    for dr in range(2):
        s = p["ssm"][dr]
        xre, xim, yd = ssm_dir_forward(f"{tag}_ssm_fwd{dr}", u, s["bpr"], s["bpi"], s["cpr"], s["cpi"],
                                       s["a8"], s["tab"], dr == 1, tm)
        states.append((xre, xim))
        ydir.append(yd)

    def merge(i, y0, y1, uv, yav, gsv, gav, hv, d_ref, wglu_ref, wbs_ref, wba_ref, wout_ref):
        ypre = y0 + y1 + d_ref[...] * uv
        z, _ = _gelu(ypre)
        zb = z.astype(BF16)
        t = _nn(zb, wglu_ref[...])
        ysb = (z * _sig(t)).astype(BF16)
        bs = _nt(ysb, wbs_ref[...])
        ba = _nn(yav, wba_ref[...])
        mg = _sig(gsv.astype(F32)) * bs + _sig(gav.astype(F32)) * ba
        mg = jnp.where(_row_ok(i, tm), mg, 0.0).astype(BF16)
        return ypre, zb, t, ysb, bs, ba, mg, hv + _nn(mg, wout_ref[...])

    ypre, zb, t, ys, bs, ba, mg, h2 = rowcall(
        tag + "_merge", merge, [ydir[0], ydir[1], u, ya, gs, ga, h], [p["d"], wglu, wbsT, wba, wout],
        [(SW, BF16), (SW, BF16), (SW, BF16), (SW, BF16), (D, BF16), (D, BF16), (D, BF16), (D, F32)], tm=tm)
    saved = dict(h=h, n=n, q=q, k=k, v=v, u=u, gs=gs, ga=ga, ya=ya, states=states, ypre=ypre, zb=zb, t=t,
                 ys=ys, bs=bs, ba=ba, mg=mg)
    return h2, saved


def mixer_backward(tag, dh, sv, p, tm):
    g, winT, wglu, wbsT, wba, wout = p["g"], p["winT"], p["wglu"], p["wbsT"], p["wba"], p["wout"]

    def y1(i, dhv, bsv, bav, gsv, gav, ypv, tv, uv, wout_ref, wbs_ref, wba_ref, d_ref, wglu_ref):
        dhb = dhv.astype(BF16)
        dmg = _nt(dhb, wout_ref[...])
        dmg = jnp.where(_row_ok(i, tm), dmg, 0.0)
        sgs = _sig(gsv.astype(F32))
        sga = _sig(gav.astype(F32))
        dbs = (dmg * sgs).astype(BF16)
        dba = (dmg * sga).astype(BF16)
        dgs = dmg * bsv.astype(F32) * sgs * (1.0 - sgs)
        dga = dmg * bav.astype(F32) * sga * (1.0 - sga)
        dys = _nn(dbs, wbs_ref[...])
        dya = _nt(dba, wba_ref[...])
        z, dz_dy = _gelu(ypv.astype(F32))
        st = _sig(tv.astype(F32))
        dt_ = dys * z * st * (1.0 - st)
        dz = dys * st + _nt(dt_.astype(BF16), wglu_ref[...])
        dyp = dz * dz_dy
        return (dbs, dba, dgs, dga, dhb, dya, dyp, dyp * d_ref[...], dt_,
                jnp.sum(dyp * uv, axis=0, keepdims=True))

    dbs, dba, dgs, dga, dhb, dya, dypb, du0, dtb, dd = rowcall(
        tag + "_bwd_merge", y1,
        [dh, sv["bs"], sv["ba"], sv["gs"], sv["ga"], sv["ypre"], sv["t"], sv["u"]],
        [wout, wbsT, wba, p["d"], wglu],
        [(D, BF16)] * 6 + [(SW, F32), (SW, BF16), (SW, BF16)], [(1, SW)], tm=tm)
    dwout = tn_matmul(tag + "_dwout", sv["mg"], dhb)
    dwbsT = tn_matmul(tag + "_dwbs", dbs, sv["ys"])
    dwba = tn_matmul(tag + "_dwba", sv["ya"], dba)
    dwglu = tn_matmul(tag + "_dwglu", sv["zb"], dtb)

    du_dirs, ssm_sums = [], []
    for dr in range(2):
        s = p["ssm"][dr]
        xre, xim = sv["states"][dr]
        res = ssm_dir_backward(f"{tag}_ssm_bwd{dr}", dypb, xre, xim, sv["u"], s["bpr"], s["bpi"], s["cpr"],
                               s["cpi"], s["a8_adj"], s["tab_adj"], dr == 0, tm)
        du_dirs.append(res[0])
        ssm_sums.append(res[1:])

    dq, dk, dv, dsink = attention_backward(tag, sv["q"], sv["k"], sv["v"], dya, p["sink"])

    def x1b(i, dqv, dkv, dvv, du0v, du1v, du2v, dgsv, dgav, hv, dhv, g_ref, w_ref):
        dub = (du0v.astype(F32) + du1v.astype(F32) + du2v.astype(F32)).astype(BF16)
        dn = (_nn(dqv, w_ref[Q0:K0, :]) + _nn(dkv, w_ref[K0:V0, :]) + _nn(dvv, w_ref[V0:U0, :])
              + _nn(dub, w_ref[U0:GS0, :]) + _nn(dgsv, w_ref[GS0:GA0, :]) + _nn(dgav, w_ref[GA0:IN_COLS, :]))
        gv = g_ref[...]
        hh, r, _ = _rms_fwd(hv, gv)
        dx, dg = _rms_bwd(hh, r, gv, dn)
        dx = jnp.where(_row_ok(i, tall), dx, 0.0)
        return dhv + dx, dub, dg

    tall = _tall_tile(dh.shape[0], tm)
    dh2, dub, dg = rowcall(tag + "_bwd_in", x1b,
                           [dq, dk, dv, du0, du_dirs[0], du_dirs[1], dgs, dga, sv["h"], dh], [g, winT],
                           [(D, F32), (SW, BF16)], [(1, D)], tm=tall)
    n = sv["n"]
    dwinT = jnp.concatenate([tn_matmul(f"{tag}_dwin{j}", piece, n)
                             for j, piece in enumerate((dq, dk, dv, dub, dgs, dga))], axis=0)
    grads = dict(g=dg, d=dd, sink=dsink, ssm=ssm_sums, winT=dwinT, wglu=dwglu, wbsT=dwbsT, wba=dwba, wout=dwout)
    return dh2, grads


W1024 = ("f1_wgT", "f1_wuT", "f1_wd", "winT", "wba", "wout", "f2_wgT", "f2_wuT", "f2_wd")
W512 = ("wglu", "wbsT")
PART_F1 = ("f1_wgT", "f1_wuT", "f1_wd")
PART_MIX = ("winT", "wba", "wout", "wglu", "wbsT")
PART_F2 = ("f2_wgT", "f2_wuT", "f2_wd")
PER_LAYER_SMALL = ("ffn1_norm", "mix_norm", "ffn2_norm", "ssm_lam_re", "ssm_lam_im", "ssm_log_dt",
                   "ssm_b_re", "ssm_b_im", "ssm_c_re", "ssm_c_im", "ssm_d", "attn_sink")
SMALL = ("ffn1_norm", "mix_norm", "ffn2_norm", "final_norm", "ssm_lam_re", "ssm_lam_im", "ssm_log_dt",
         "ssm_b_re", "ssm_b_im", "ssm_c_re", "ssm_c_im", "ssm_d", "attn_sink")


def kernel(x, meta_tokens, ffn1_norm, ffn1_w_gate, ffn1_w_up, ffn1_w_down, mix_norm, w_in, ssm_lam_re, ssm_lam_im, ssm_log_dt, ssm_b_re, ssm_b_im, ssm_c_re, ssm_c_im, ssm_d, ssm_w_glu, attn_sink, w_branch_ssm, w_branch_attn, w_out, ffn2_norm, ffn2_w_gate, ffn2_w_up, ffn2_w_down, final_norm, loss_target, m_meta_tokens, m_ffn1_norm, m_ffn1_w_gate, m_ffn1_w_up, m_ffn1_w_down, m_mix_norm, m_w_in, m_ssm_lam_re, m_ssm_lam_im, m_ssm_log_dt, m_ssm_b_re, m_ssm_b_im, m_ssm_c_re, m_ssm_c_im, m_ssm_d, m_ssm_w_glu, m_attn_sink, m_w_branch_ssm, m_w_branch_attn, m_w_out, m_ffn2_norm, m_ffn2_w_gate, m_ffn2_w_up, m_ffn2_w_down, m_final_norm, v_meta_tokens, v_ffn1_norm, v_ffn1_w_gate, v_ffn1_w_up, v_ffn1_w_down, v_mix_norm, v_w_in, v_ssm_lam_re, v_ssm_lam_im, v_ssm_log_dt, v_ssm_b_re, v_ssm_b_im, v_ssm_c_re, v_ssm_c_im, v_ssm_d, v_ssm_w_glu, v_attn_sink, v_w_branch_ssm, v_w_branch_attn, v_w_out, v_ffn2_norm, v_ffn2_w_gate, v_ffn2_w_up, v_ffn2_w_down, v_final_norm):
    weights = dict(meta_tokens=meta_tokens, ffn1_norm=ffn1_norm, ffn1_w_gate=ffn1_w_gate, ffn1_w_up=ffn1_w_up, ffn1_w_down=ffn1_w_down, mix_norm=mix_norm, w_in=w_in, ssm_lam_re=ssm_lam_re, ssm_lam_im=ssm_lam_im, ssm_log_dt=ssm_log_dt, ssm_b_re=ssm_b_re, ssm_b_im=ssm_b_im, ssm_c_re=ssm_c_re, ssm_c_im=ssm_c_im, ssm_d=ssm_d, ssm_w_glu=ssm_w_glu, attn_sink=attn_sink, w_branch_ssm=w_branch_ssm, w_branch_attn=w_branch_attn, w_out=w_out, ffn2_norm=ffn2_norm, ffn2_w_gate=ffn2_w_gate, ffn2_w_up=ffn2_w_up, ffn2_w_down=ffn2_w_down, final_norm=final_norm)
    mom_m = dict(meta_tokens=m_meta_tokens, ffn1_norm=m_ffn1_norm, ffn1_w_gate=m_ffn1_w_gate, ffn1_w_up=m_ffn1_w_up, ffn1_w_down=m_ffn1_w_down, mix_norm=m_mix_norm, w_in=m_w_in, ssm_lam_re=m_ssm_lam_re, ssm_lam_im=m_ssm_lam_im, ssm_log_dt=m_ssm_log_dt, ssm_b_re=m_ssm_b_re, ssm_b_im=m_ssm_b_im, ssm_c_re=m_ssm_c_re, ssm_c_im=m_ssm_c_im, ssm_d=m_ssm_d, ssm_w_glu=m_ssm_w_glu, attn_sink=m_attn_sink, w_branch_ssm=m_w_branch_ssm, w_branch_attn=m_w_branch_attn, w_out=m_w_out, ffn2_norm=m_ffn2_norm, ffn2_w_gate=m_ffn2_w_gate, ffn2_w_up=m_ffn2_w_up, ffn2_w_down=m_ffn2_w_down, final_norm=m_final_norm)
    mom_v = dict(meta_tokens=v_meta_tokens, ffn1_norm=v_ffn1_norm, ffn1_w_gate=v_ffn1_w_gate, ffn1_w_up=v_ffn1_w_up, ffn1_w_down=v_ffn1_w_down, mix_norm=v_mix_norm, w_in=v_w_in, ssm_lam_re=v_ssm_lam_re, ssm_lam_im=v_ssm_lam_im, ssm_log_dt=v_ssm_log_dt, ssm_b_re=v_ssm_b_re, ssm_b_im=v_ssm_b_im, ssm_c_re=v_ssm_c_re, ssm_c_im=v_ssm_c_im, ssm_d=v_ssm_d, ssm_w_glu=v_ssm_w_glu, attn_sink=v_attn_sink, w_branch_ssm=v_w_branch_ssm, w_branch_attn=v_w_branch_attn, w_out=v_w_out, ffn2_norm=v_ffn2_norm, ffn2_w_gate=v_ffn2_w_gate, ffn2_w_up=v_ffn2_w_up, ffn2_w_down=v_ffn2_w_down, final_norm=v_final_norm)
    names = list(weights)

    L0 = x.shape[1]
    LP = L0 + BLK
    tm = 384 if LP % 384 == 0 else BLK
    x_i, y_i, c_i = lax.axis_index("x"), lax.axis_index("y"), lax.axis_index("c")
    me = 4 * x_i + 2 * y_i + c_i

    def canon(l):
        return dict(
            f1_wgT=ffn1_w_gate[l].T, f1_wuT=ffn1_w_up[l].T, f1_wd=ffn1_w_down[l],
            winT=w_in[l].T, wba=w_branch_attn[l], wout=w_out[l],
            f2_wgT=ffn2_w_gate[l].T, f2_wuT=ffn2_w_up[l].T, f2_wd=ffn2_w_down[l],
            wglu=ssm_w_glu[l], wbsT=w_branch_ssm[l].T)

    shards = [{nm: a.astype(BF16) for nm, a in canon(l).items()} for l in range(DEPTH)]

    def rows_of(nm):
        return shards[0][nm].shape[0]

    def width_groups(names_):
        return [g for g in ([nm for nm in names_ if nm in W1024], [nm for nm in names_ if nm in W512]) if g]

    def pieces_for(group):
        out, off = [], 0
        for nm in group:
            out.append((off, rows_of(nm)))
            off += rows_of(nm)
        return out

    def start_gather(tag, l, names_):
        groups = width_groups(names_)
        packed = [jnp.concatenate([shards[l][nm] for nm in g], axis=0) for g in groups]
        return gather_layer_start(tag, packed, [pieces_for(g) for g in groups]), groups

    def finish_gather(tag, l, started_, after):
        handle, groups = started_
        dests = gather_layer_wait(tag, handle, len(groups), after)
        out = {}
        for nm, dest in zip([nm for g in groups for nm in g], dests):
            sh = shards[l][nm]
            out[nm] = lax.dynamic_update_slice(dest, sh, (me * sh.shape[0], 0))
        return out

    g1, gm = all_gather_pieces(
        "gather_weights_first",
        [(jnp.concatenate([shards[0][nm] for nm in PART_F1], axis=0), pieces_for(PART_F1)),
         (meta_tokens, [(0, N_META)])])
    first_weights = dict(zip(PART_F1, g1))
    meta_full = gm[0].reshape(NDEV, N_META, D // NDEV).transpose(1, 0, 2).reshape(N_META, D)
    gather_started = [start_gather("gather_start_l0", 0, PART_MIX + PART_F2)]
    gather_started += [start_gather(f"gather_start_l{l}", l, W1024 + W512) for l in range(1, DEPTH)]
    started = sum(st[0][3][0, 0] for st in gather_started)
    full = [None] * DEPTH

    def disc_all(lre, lim, ldt, bre, bim):
        return _ssm_disc(lre, lim, ldt, bre, bim)

    ssm_p, ssm_vjp = [], []
    for l in range(DEPTH):
        row, vrow = [], []
        for dr in range(2):
            args = (ssm_lam_re[l, dr], ssm_lam_im[l, dr], ssm_log_dt[l, dr], ssm_b_re[l, dr], ssm_b_im[l, dr])
            (a_re, a_im, bb_re, bb_im), vjp = jax.vjp(disc_all, *args)
            a8, tab = _scan_tables(args[0], args[1], args[2], False, dr == 1, tm // 8)
            a8_adj, tab_adj = _scan_tables(args[0], args[1], args[2], True, dr == 0, tm // 8)
            row.append(dict(
                bpr=_pack_b(bb_re).astype(BF16), bpi=_pack_b(bb_im).astype(BF16),
                cpr=_pack_c(ssm_c_re[l, dr]).astype(BF16), cpi=_pack_c(ssm_c_im[l, dr]).astype(BF16),
                a8=a8, tab=tab, a8_adj=a8_adj, tab_adj=tab_adj, a_re=a_re, a_im=a_im))
            vrow.append(vjp)
        ssm_p.append(row)
        ssm_vjp.append(vrow)

    blk0 = jnp.concatenate([jnp.zeros((PAD, D), F32), meta_full.astype(F32)], axis=0)
    h = build_h0(x[0], blk0)
    saved = []
    for l in range(DEPTH):
        w = dict(first_weights) if l == 0 else finish_gather(f"gather_wait_l{l}", l, gather_started[l], h)
        full[l] = w
        g1n, g2n = ffn1_norm[l][None, :], ffn2_norm[l][None, :]
        if l == 0:
            g1n = g1n + started
        h, s1 = ffn_forward("ffn1", h, g1n, w["f1_wgT"], w["f1_wuT"], w["f1_wd"], tm)
        if l == 0:
            w.update(finish_gather("gather_wait_l0", 0, gather_started[0], h))
        mp = dict(g=mix_norm[l][None, :], winT=w["winT"], wglu=w["wglu"], wbsT=w["wbsT"], wba=w["wba"],
                  wout=w["wout"], d=ssm_d[l][None, :], sink=attn_sink[l], ssm=ssm_p[l])
        h, s2 = mixer_forward("mix", h, mp, tm)
        h, s3 = ffn_forward("ffn2", h, g2n, w["f2_wgT"], w["f2_wuT"], w["f2_wd"], tm)
        saved.append((s1, s2, s3, mp, g1n, g2n))

    dh, loss_acc, dgf = final_loss(h, loss_target[0], final_norm[None, :])
    loss = lax.psum(loss_acc[0, 0], MESH_AXES)

    small = {nm: [None] * DEPTH for nm in SMALL if nm != "final_norm"}

    def start_scatter(tag, grads_d, names_):
        groups = width_groups(names_)
        mine = [jnp.concatenate([lax.dynamic_slice_in_dim(grads_d[nm], me * rows_of(nm), rows_of(nm), axis=0)
                                 for nm in g], axis=0) for g in groups]
        handle, nin = scatter_layer_start(tag + "_start", [[grads_d[nm] for nm in g] for g in groups])
        return dict(tag=tag, handle=handle, nin=nin, groups=groups, mine=mine)

    def finish_scatter(st, after):
        lands = scatter_layer_wait(st["tag"] + "_wait", st["handle"], st["nin"], len(st["groups"]), after)
        out = {}
        for land, mine, g in zip(lands, st["mine"], st["groups"]):
            land = lax.dynamic_update_slice(land, mine[None], (me, 0, 0))
            tot = sum_slots(f"sum_weight_grads_{land.shape[1]}x{land.shape[2]}", land)
            for nm, (off_, r) in zip(g, pieces_for(g)):
                out[nm] = tot[off_:off_ + r]
        return out

    scatters = []
    small_started = [None] * DEPTH
    small_len = sum(math.prod(weights[k].shape[1:]) for k in PER_LAYER_SMALL) + N_META * D
    small_rows = -(-small_len // (8 * D)) * 8
    sent = jnp.zeros((), F32)
    for l in reversed(range(DEPTH)):
        s1, s2, s3, mp, g1n, g2n = saved[l]
        w = full[l]
        dh, dg2, f2g, f2u, f2d = ffn_backward("ffn2", dh, s3, g2n + sent, w["f2_wgT"], w["f2_wuT"], w["f2_wd"], tm)
        st = start_scatter(f"scatter_l{l}_f2", dict(f2_wgT=f2g, f2_wuT=f2u, f2_wd=f2d), PART_F2)
        scatters.append((l, st))
        dh, mg = mixer_backward("mix", dh, s2, dict(mp, d=mp["d"] + st["handle"][3][0, 0]), tm)
        st = start_scatter(f"scatter_l{l}_mix", mg, PART_MIX)
        scatters.append((l, st))
        if l > 0:
            dh, dg1, f1g, f1u, f1d = ffn_backward("ffn1", dh, s1, g1n + st["handle"][3][0, 0],
                                                  w["f1_wgT"], w["f1_wuT"], w["f1_wd"], tm)
            st = start_scatter(f"scatter_l{l}_f1", dict(f1_wgT=f1g, f1_wuT=f1u, f1_wd=f1d), PART_F1)
            scatters.append((l, st))
            sent = st["handle"][3][0, 0]
        else:
            def emit(nm, arr, l=l):
                scatters.append((l, start_scatter(f"scatter_l{l}_f1_{nm}", {"f1_" + nm: arr}, ("f1_" + nm,))))

            dh, dg1, _, _, _ = ffn_backward("ffn1", dh, s1, g1n + st["handle"][3][0, 0],
                                            w["f1_wgT"], w["f1_wuT"], w["f1_wd"], tm, emit=emit)
        small["ffn1_norm"][l] = dg1[0]
        small["mix_norm"][l] = mg["g"][0]
        small["ffn2_norm"][l] = dg2[0]
        small["ssm_d"][l] = mg["d"][0]
        small["attn_sink"][l] = mg["sink"][0, :N_HEADS]
        per_dir = {k: [] for k in ("ssm_lam_re", "ssm_lam_im", "ssm_log_dt", "ssm_b_re", "ssm_b_im",
                                   "ssm_c_re", "ssm_c_im")}
        for dr in range(2):
            gbr, gbi, gcr, gci, s_re, s_im = mg["ssm"][dr]
            a_re, a_im = ssm_p[l][dr]["a_re"], ssm_p[l][dr]["a_im"]
            s_re = s_re.reshape(SGRP, SP)
            s_im = s_im.reshape(SGRP, SP)
            den = a_re * a_re + a_im * a_im
            ga_re = (s_re * a_re - s_im * a_im) / den
            ga_im = (s_re * a_im + s_im * a_re) / den
            glr, gli, gld, gbre, gbim = ssm_vjp[l][dr]((ga_re, ga_im, _unpack_diag(gbr).transpose(0, 2, 1),
                                                        _unpack_diag(gbi).transpose(0, 2, 1)))
            per_dir["ssm_lam_re"].append(glr)
            per_dir["ssm_lam_im"].append(gli)
            per_dir["ssm_log_dt"].append(gld)
            per_dir["ssm_b_re"].append(gbre)
            per_dir["ssm_b_im"].append(gbim)
            per_dir["ssm_c_re"].append(_unpack_diag(gcr))
            per_dir["ssm_c_im"].append(_unpack_diag(gci))
        for k, vlist in per_dir.items():
            small[k][l] = jnp.stack(vlist)
        vec = [small[k][l].reshape(-1) for k in PER_LAYER_SMALL]
        if l == DEPTH - 1:
            vec.append(dgf[0])
        if l == 0:
            vec.append(dh[PAD:BLK].reshape(-1))
        used = sum(v.shape[0] for v in vec)
        flat = jnp.concatenate(vec + [jnp.zeros((small_rows * D - used,), F32)]).reshape(small_rows, D)
        small_started[l] = (exchange_start(
            f"small_grads_l{l}_start", [flat, jnp.zeros((NDEV, small_rows, D), F32)], 1,
            lambda refs, me_i, p_i: [(refs[0], refs[1].at[me_i], 0)]), flat)
        sent = sent + small_started[l][0][3][0, 0]

    grad_x = dh[BLK:][None]

    grads = {k: [None] * DEPTH for k in PER_LAYER_SMALL}
    for l in reversed(range(DEPTH)):
        (s_sems, r_sems, arrays, _), flat = small_started[l]
        land = exchange_wait(f"small_grads_l{l}_wait", s_sems, r_sems, arrays, 1, lambda refs, gi: refs[0], dh)[1]
        land = lax.dynamic_update_slice(land, flat[None], (me, 0, 0))
        tot = sum_slots("sum_small_grads", land).reshape(-1)
        o = 0
        for k in PER_LAYER_SMALL:
            shp = weights[k].shape[1:]
            sz = math.prod(shp)
            grads[k][l] = tot[o:o + sz].reshape(shp)
            o += sz
        if l == DEPTH - 1:
            final_norm_grad = tot[o:o + D]
        if l == 0:
            dmeta_full = tot[o:o + N_META * D].reshape(N_META, D)
    grads = {k: jnp.stack(vv) for k, vv in grads.items()}
    grads["final_norm"] = final_norm_grad
    grads["meta_tokens"] = lax.dynamic_slice_in_dim(dmeta_full, me * (D // NDEV), D // NDEV, axis=1)

    own = [dict() for _ in range(DEPTH)]
    for l, st in scatters:
        own[l].update(finish_scatter(st, dh))

    def stack(fn):
        return jnp.stack([fn(own[l]) for l in range(DEPTH)])

    grads["ffn1_w_gate"] = stack(lambda d: d["f1_wgT"].T)
    grads["ffn1_w_up"] = stack(lambda d: d["f1_wuT"].T)
    grads["ffn1_w_down"] = stack(lambda d: d["f1_wd"])
    grads["w_in"] = stack(lambda d: d["winT"].T)
    grads["ssm_w_glu"] = stack(lambda d: d["wglu"])
    grads["w_branch_ssm"] = stack(lambda d: d["wbsT"].T)
    grads["w_branch_attn"] = stack(lambda d: d["wba"])
    grads["w_out"] = stack(lambda d: d["wout"])
    grads["ffn2_w_gate"] = stack(lambda d: d["f2_wgT"].T)
    grads["ffn2_w_up"] = stack(lambda d: d["f2_wuT"].T)
    grads["ffn2_w_down"] = stack(lambda d: d["f2_wd"])

    deltas, new_m, new_v = {}, {}, {}
    for nm in names:
        deltas[nm], new_m[nm], new_v[nm] = adamw("adamw_" + nm, weights[nm], grads[nm], mom_m[nm], mom_v[nm])

    return (loss, grad_x, *[grads[n] for n in names], *[deltas[n] for n in names],
            *[new_m[n] for n in names], *[new_v[n] for n in names])
```

```python
import functools
import math

import jax
import jax.numpy as jnp
from jax import lax
from jax.experimental import pallas as pl
from jax.experimental.pallas import tpu as pltpu

F32 = jnp.float32
BF16 = jnp.bfloat16

D = 1024
DFF = 2816
N_META = 16
N_HEADS = 16
N_KV = 4
HD = 64
QG = 4
WIN = 128
BLK = 128
PAD = BLK - N_META
SW = 512
SGRP = 32
SCH = 16
SP = 64
NST = SGRP * SP
EPS = 1e-6
NEG = -1e30
SCALE = HD ** -0.5
NDEV = 8
DEPTH = 4
MESH_AXES = ("x", "y", "c")
MESH = pl.DeviceIdType.MESH

ADAM_LR = 0.001
ADAM_B1 = 0.9
ADAM_B2 = 0.999
ADAM_EPS = 1e-08
ADAM_WD = 0.01
ADAM_STEP = 10

VMEM_LIMIT = 56 * 1024 * 1024


def _params(*sem):
    return pltpu.CompilerParams(dimension_semantics=sem, vmem_limit_bytes=VMEM_LIMIT)


def _nn(a, b):
    return lax.dot_general(a, b, (((1,), (0,)), ((), ())), preferred_element_type=F32)


def _nt(a, b):
    return lax.dot_general(a, b, (((1,), (1,)), ((), ())), preferred_element_type=F32)


def _tn(a, b):
    return lax.dot_general(a, b, (((0,), (0,)), ((), ())), preferred_element_type=F32)


def _sig(x):
    return 0.5 * jnp.tanh(0.5 * x) + 0.5


def _rms_fwd(h, g):
    r = lax.rsqrt(jnp.mean(h * h, axis=-1, keepdims=True) + EPS)
    hh = h * r
    return hh, r, hh * g


def _rms_bwd(hh, r, g, dn):
    dhh = dn * g
    dx = r * (dhh - hh * jnp.mean(dhh * hh, axis=-1, keepdims=True))
    return dx, jnp.sum(dn * hh, axis=0, keepdims=True)


def _row_ok(i, tm):
    rows = i * tm + lax.broadcasted_iota(jnp.int32, (tm, 1), 0)
    return rows >= PAD


def _const_spec(shape, single=False):
    nd = len(shape)
    if single:
        return pl.BlockSpec(shape, lambda *_: (0,) * nd, pipeline_mode=pl.Buffered(1))
    return pl.BlockSpec(shape, lambda *_: (0,) * nd)


def rowcall(name, body, rows, consts, outs, accs=(), *, tm):
    nrows = rows[0].shape[0]
    nt = nrows // tm
    assert nt * tm == nrows, (name, nrows, tm)
    nr, nc, no, na = len(rows), len(consts), len(outs), len(accs)
    in_specs = [pl.BlockSpec((tm, r.shape[1]), lambda i: (i, 0)) for r in rows]
    in_specs += [_const_spec(c.shape, single=True) for c in consts]
    out_shape = [jax.ShapeDtypeStruct((nrows, w), dt) for (w, dt) in outs]
    out_specs = [pl.BlockSpec((tm, w), lambda i: (i, 0)) for (w, dt) in outs]
    out_shape += [jax.ShapeDtypeStruct(s, F32) for s in accs]
    out_specs += [_const_spec(s) for s in accs]

    def kern(*refs):
        i = pl.program_id(0)
        row_vals = [r[...] for r in refs[:nr]]
        res = body(i, *row_vals, *refs[nr:nr + nc])
        out_refs = refs[nr + nc:nr + nc + no]
        acc_refs = refs[nr + nc + no:]
        for r, v in zip(out_refs, res[:no]):
            r[...] = v.astype(r.dtype)
        if na:
            @pl.when(i == 0)
            def _():
                for r in acc_refs:
                    r[...] = jnp.zeros_like(r)
            for r, v in zip(acc_refs, res[no:]):
                r[...] += v

    res = pl.pallas_call(
        kern, name=name, grid=(nt,), in_specs=in_specs, out_specs=out_specs, out_shape=out_shape,
        compiler_params=_params("arbitrary"),
    )(*rows, *consts)
    return res


def tn_matmul(name, lhs, rhs, scale=1.0):
    M, K = lhs.shape
    N = rhs.shape[1]
    assert lhs.dtype == BF16 and rhs.dtype == BF16
    nm = 6
    tmw = M // nm
    assert tmw * nm == M and tmw % 16 == 0
    tk = 1408 if (K % 1408 == 0) else K
    nk = K // tk

    def kern(a_ref, b_ref, o_ref, acc):
        m = pl.program_id(1)
        part = _tn(a_ref[...], b_ref[...])

        @pl.when(m == 0)
        def _():
            acc[...] = part

        @pl.when((m > 0) & (m < nm - 1))
        def _():
            acc[...] += part

        @pl.when(m == nm - 1)
        def _():
            o_ref[...] = ((acc[...] + part) * scale).astype(o_ref.dtype)

    return pl.pallas_call(
        kern, name=name, grid=(nk, nm),
        in_specs=[pl.BlockSpec((tmw, tk), lambda k, m: (m, k)), pl.BlockSpec((tmw, N), lambda k, m: (m, 0))],
        out_specs=pl.BlockSpec((tk, N), lambda k, m: (k, 0)),
        out_shape=jax.ShapeDtypeStruct((K, N), BF16),
        scratch_shapes=[pltpu.VMEM((tk, N), F32)],
        compiler_params=_params("arbitrary", "arbitrary"),
    )(lhs, rhs)


def _mesh_pos():
    x, y, c = lax.axis_index("x"), lax.axis_index("y"), lax.axis_index("c")
    return x, y, c


def all_gather_pieces(name, groups):
    ng = len(groups)
    packed = [g[0] for g in groups]
    pieces = [g[1] for g in groups]
    out_shape, out_map = [], []
    for gi, (p, pcs) in enumerate(groups):
        idx = []
        for (off, r) in pcs:
            idx.append(len(out_shape))
            out_shape.append(jax.ShapeDtypeStruct((NDEV * r, p.shape[1]), p.dtype))
        out_map.append(idx)
    nout = len(out_shape)

    def body(*refs):
        p_refs = refs[:ng]
        o_refs = refs[ng:ng + nout]
        send_sems, recv_sems, local_sems = refs[ng + nout:]
        x, y, c = _mesh_pos()
        me = (x, y, c)
        sibling = (x, y, 1 - c)
        chips = [(1 - x, y), (x, 1 - y), (1 - x, 1 - y)]

        def blk(px, py, pc):
            return 4 * px + 2 * py + pc

        def copies(gi, k, origin, to, from_out):
            cps = []
            for (off, r), oi in zip(pieces[gi], out_map[gi]):
                dst = o_refs[oi].at[pl.ds(origin * r, r), :]
                src = dst if from_out else p_refs[gi].at[pl.ds(off, r), :]
                cps.append(pltpu.make_async_remote_copy(
                    src_ref=src, dst_ref=dst, send_sem=send_sems.at[gi, k], recv_sem=recv_sems.at[gi, k],
                    device_id=to, device_id_type=MESH))
            return cps

        def whole(gi, k):
            return pltpu.make_async_remote_copy(
                src_ref=p_refs[gi], dst_ref=p_refs[gi], send_sem=send_sems.at[gi, k],
                recv_sem=recv_sems.at[gi, k], device_id=me, device_id_type=MESH)

        mine = []
        for gi in range(ng):
            for (off, r), oi in zip(pieces[gi], out_map[gi]):
                mine.append(pltpu.make_async_copy(
                    p_refs[gi].at[pl.ds(off, r), :], o_refs[oi].at[pl.ds(blk(*me) * r, r), :],
                    local_sems.at[gi]))
        for cp in mine:
            cp.start()
        for gi in range(ng):
            for cp in copies(gi, 0, blk(*me), sibling, False):
                cp.start()
            for j, chip in enumerate(chips):
                for cp in copies(gi, 1 + j, blk(*me), (*chip, c), False):
                    cp.start()
        for j, chip in enumerate(chips):
            for gi in range(ng):
                whole(gi, 1 + j).wait_recv()
                for cp in copies(gi, 4 + j, blk(*chip, c), sibling, True):
                    cp.start()
        for gi in range(ng):
            whole(gi, 0).wait_recv()
            for j in range(3):
                whole(gi, 4 + j).wait_recv()
        for gi in range(ng):
            for k in range(7):
                whole(gi, k).wait_send()
            pltpu.make_async_copy(p_refs[gi], p_refs[gi], local_sems.at[gi]).wait()

    any_spec = pl.BlockSpec(memory_space=pl.ANY)
    outs = pl.pallas_call(
        body, name=name, out_shape=out_shape,
        in_specs=[any_spec] * ng, out_specs=[any_spec] * nout,
        scratch_shapes=[pltpu.SemaphoreType.DMA((ng, 7)), pltpu.SemaphoreType.DMA((ng, 7)),
                        pltpu.SemaphoreType.DMA((ng,))],
    )(*packed)
    return [[outs[oi] for oi in idx] for idx in out_map]


HBM_SPEC = pl.BlockSpec(memory_space=pltpu.HBM)
SEM_SPEC = pl.BlockSpec(memory_space=pltpu.SEMAPHORE)
DATAFLOW = pltpu.SideEffectType.DATAFLOW_SIDE_EFFECTING


def _peers(x, y, c):
    return [(x, y, 1 - c), (1 - x, y, c), (x, 1 - y, c), (1 - x, 1 - y, c),
            (1 - x, y, 1 - c), (x, 1 - y, 1 - c), (1 - x, 1 - y, 1 - c)]


def exchange_start(name, arrays, ng, plan):
    n = len(arrays)
    ns = ng * 7

    def body(*refs):
        in_refs = refs[:n]
        send_sems, recv_sems = refs[n:n + ns], refs[n + ns:n + 2 * ns]
        token = refs[-1]
        x, y, c = _mesh_pos()
        me_i = 4 * x + 2 * y + c
        for k, peer in enumerate(_peers(x, y, c)):
            p_i = 4 * peer[0] + 2 * peer[1] + peer[2]
            for src, dst, gi in plan(in_refs, me_i, p_i):
                pltpu.make_async_remote_copy(
                    src_ref=src, dst_ref=dst, send_sem=send_sems[gi * 7 + k], recv_sem=recv_sems[gi * 7 + k],
                    device_id=peer, device_id_type=MESH).start()
        token[...] = jnp.zeros_like(token)

    res = pl.pallas_call(
        body, name=name,
        out_shape=(*[pltpu.SemaphoreType.DMA(())] * (2 * ns),
                   *[pltpu.HBM(a.shape, a.dtype) for a in arrays], jax.ShapeDtypeStruct((8, 128), F32)),
        in_specs=[HBM_SPEC] * n,
        out_specs=(*[SEM_SPEC] * (2 * ns), *[HBM_SPEC] * n, pl.BlockSpec(memory_space=pltpu.VMEM)),
        input_output_aliases={i: 2 * ns + i for i in range(n)},
        compiler_params=pltpu.CompilerParams(has_side_effects=DATAFLOW),
    )(*[pltpu.with_memory_space_constraint(a, pltpu.HBM) for a in arrays])
    return list(res[:ns]), list(res[ns:2 * ns]), list(res[2 * ns:2 * ns + n]), res[-1]


def exchange_wait(name, send_sems, recv_sems, arrays, ng, sized, after):
    n = len(arrays)
    ns = ng * 7

    def body(*refs):
        in_refs = refs[:n]
        s_sems, r_sems = refs[n:n + ns], refs[n + ns:n + 2 * ns]
        x, y, c = _mesh_pos()
        for gi in range(ng):
            view = sized(in_refs, gi)
            for k in range(7):
                w = pltpu.make_async_remote_copy(
                    src_ref=view, dst_ref=view, send_sem=s_sems[gi * 7 + k], recv_sem=r_sems[gi * 7 + k],
                    device_id=(x, y, c), device_id_type=MESH)
                w.wait_send()
                w.wait_recv()

    res = pl.pallas_call(
        body, name=name, out_shape=tuple(pltpu.HBM(a.shape, a.dtype) for a in arrays),
        in_specs=[HBM_SPEC] * n + [SEM_SPEC] * (2 * ns) + [pl.BlockSpec(memory_space=pl.ANY)],
        out_specs=tuple([HBM_SPEC] * n), input_output_aliases={i: i for i in range(n)},
        compiler_params=pltpu.CompilerParams(has_side_effects=DATAFLOW),
    )(*arrays, *send_sems, *recv_sems, after)
    return list(res)


def gather_layer_start(name, packed, pieces):
    ng = len(packed)
    dests = [lax.empty((NDEV * r, p.shape[1]), p.dtype) for p, pcs in zip(packed, pieces) for (_, r) in pcs]

    def plan(refs, me_i, p_i):
        out, di = [], ng
        for gi in range(ng):
            for (off, r) in pieces[gi]:
                out.append((refs[gi].at[pl.ds(off, r), :], refs[di].at[pl.ds(me_i * r, r), :], gi))
                di += 1
        return out

    return exchange_start(name, list(packed) + dests, ng, plan)


def gather_layer_wait(name, handle, ng, after):
    send_sems, recv_sems, arrays, _ = handle
    out = exchange_wait(name, send_sems, recv_sems, arrays, ng, lambda refs, gi: refs[gi], after)
    return out[ng:]


def scatter_layer_start(name, groups):
    ng = len(groups)
    flat = [a for arrs in groups for a in arrs]
    offs, lands = [], []
    for arrs in groups:
        o, off = [], 0
        for a in arrs:
            r = a.shape[0] // NDEV
            o.append((off, r))
            off += r
        offs.append(o)
        lands.append(lax.empty((NDEV, off, arrs[0].shape[1]), arrs[0].dtype))
    nin = len(flat)

    def plan(refs, me_i, p_i):
        out, ai = [], 0
        for gi in range(ng):
            for (off, r) in offs[gi]:
                out.append((refs[ai].at[pl.ds(p_i * r, r), :], refs[nin + gi].at[me_i, pl.ds(off, r), :], gi))
                ai += 1
        return out

    return exchange_start(name, flat + lands, ng, plan), nin


def scatter_layer_wait(name, handle, nin, ng, after):
    send_sems, recv_sems, arrays, _ = handle
    out = exchange_wait(name, send_sems, recv_sems, arrays, ng, lambda refs, gi: refs[nin + gi].at[0], after)
    return out[nin:]


def _pick_tile(n, cap):
    best = None
    for t in range(8, min(n, cap) + 1, 8):
        if n % t == 0:
            best = t
    return best if best is not None else n


def sum_slots(name, land):
    _, R, W = land.shape
    tr = _pick_tile(R, 512)

    def kern(l_ref, o_ref):
        acc = l_ref[0].astype(F32)
        for s in range(1, NDEV):
            acc = acc + l_ref[s].astype(F32)
        o_ref[...] = acc

    return pl.pallas_call(
        kern, name=name, grid=(R // tr,),
        in_specs=[pl.BlockSpec((NDEV, tr, W), lambda i: (0, i, 0))],
        out_specs=pl.BlockSpec((tr, W), lambda i: (i, 0)),
        out_shape=jax.ShapeDtypeStruct((R, W), F32),
        compiler_params=_params("arbitrary"),
    )(land)


def adamw(name, w, g, m, v):
    shp = w.shape
    C = shp[-1]
    R = max(1, math.prod(shp[:-1]))
    tr = _pick_tile(R, 1024)
    w2, g2, m2, v2 = (a.reshape(R, C) for a in (w, g, m, v))

    def kern(w_ref, g_ref, m_ref, v_ref, d_ref, mo_ref, vo_ref):
        gg = g_ref[...]
        mn = ADAM_B1 * m_ref[...] + (1.0 - ADAM_B1) * gg
        vn = ADAM_B2 * v_ref[...] + (1.0 - ADAM_B2) * jnp.square(gg)
        m_hat = mn / (1.0 - ADAM_B1 ** ADAM_STEP)
        v_hat = vn / (1.0 - ADAM_B2 ** ADAM_STEP)
        d_ref[...] = -ADAM_LR * (m_hat / (jnp.sqrt(v_hat) + ADAM_EPS) + ADAM_WD * w_ref[...])
        mo_ref[...] = mn
        vo_ref[...] = vn

    spec = pl.BlockSpec((tr, C), lambda i: (i, 0))
    d, mo, vo = pl.pallas_call(
        kern, name=name, grid=(R // tr,), in_specs=[spec] * 4, out_specs=[spec] * 3,
        out_shape=[jax.ShapeDtypeStruct((R, C), F32)] * 3, compiler_params=_params("arbitrary"),
    )(w2, g2, m2, v2)
    return d.reshape(shp), mo.reshape(shp), vo.reshape(shp)


def build_h0(x2, blk0):
    L0 = x2.shape[0]
    nb = L0 // BLK + 1

    def kern(x_ref, b_ref, o_ref):
        i = pl.program_id(0)

        @pl.when(i == 0)
        def _():
            o_ref[...] = b_ref[...]

        @pl.when(i > 0)
        def _():
            o_ref[...] = x_ref[...]

    return pl.pallas_call(
        kern, name="build_h0", grid=(nb,),
        in_specs=[pl.BlockSpec((BLK, D), lambda i: (jnp.maximum(i - 1, 0), 0)), _const_spec((BLK, D))],
        out_specs=pl.BlockSpec((BLK, D), lambda i: (i, 0)),
        out_shape=jax.ShapeDtypeStruct((L0 + BLK, D), F32), compiler_params=_params("arbitrary"),
    )(x2, blk0)


def final_loss(h, tgt, gf):
    LP = h.shape[0]
    nb = LP // BLK

    def kern(h_ref, t_ref, g_ref, dh_ref, loss_ref, dg_ref):
        i = pl.program_id(0)

        @pl.when(i == 0)
        def _():
            loss_ref[...] = jnp.zeros_like(loss_ref)
            dg_ref[...] = jnp.zeros_like(dg_ref)

        g = g_ref[...]
        hh, r, yv = _rms_fwd(h_ref[...], g)
        valid = (i > 0).astype(F32)
        err = (yv - t_ref[...]) * valid
        loss_ref[...] += 0.5 * jnp.sum(jnp.sum(err * err, axis=1, keepdims=True), axis=0, keepdims=True) / D
        dy = err / D
        dx, dg = _rms_bwd(hh, r, g, dy)
        dh_ref[...] = dx
        dg_ref[...] += dg

    return pl.pallas_call(
        kern, name="final_loss", grid=(nb,),
        in_specs=[pl.BlockSpec((BLK, D), lambda i: (i, 0)),
                  pl.BlockSpec((BLK, D), lambda i: (jnp.maximum(i - 1, 0), 0)), _const_spec((1, D))],
        out_specs=[pl.BlockSpec((BLK, D), lambda i: (i, 0)), _const_spec((8, 128)), _const_spec((1, D))],
        out_shape=[jax.ShapeDtypeStruct((LP, D), F32), jax.ShapeDtypeStruct((8, 128), F32),
                   jax.ShapeDtypeStruct((1, D), F32)],
        compiler_params=_params("arbitrary"),
    )(h, tgt, gf)


def _tall_tile(nrows, tm):
    t = nrows // 24
    return t if (t * 24 == nrows and t % 16 == 0 and t > tm) else tm


def ffn_forward(tag, h, g, wgT, wuT, wd, tm):
    def f1(i, hv, g_ref, wg_ref, wu_ref):
        _, _, n = _rms_fwd(hv, g_ref[...])
        nb = n.astype(BF16)
        G = _nt(nb, wg_ref[...])
        U = _nt(nb, wu_ref[...])
        A = G * _sig(G) * U
        return nb, G, U, A

    n, G, U, A = rowcall(tag + "_up", f1, [h], [g, wgT, wuT],
                         [(D, BF16), (DFF, BF16), (DFF, BF16), (DFF, BF16)], tm=tm)

    def f2(i, av, hv, wd_ref):
        return (hv + 0.5 * _nn(av, wd_ref[...]),)

    (h2,) = rowcall(tag + "_down", f2, [A, h], [wd], [(D, F32)], tm=_tall_tile(h.shape[0], tm))
    return h2, (h, n, G, U, A)


def ffn_backward(tag, dh, saved, g, wgT, wuT, wd, tm, emit=None):
    h, n, G, U, A = saved

    def b1(i, dhv, wd_ref):
        dyb = (0.5 * dhv).astype(BF16)
        return _nt(dyb, wd_ref[...]), dyb

    dA, dyb = rowcall(tag + "_bwd_act", b1, [dh], [wd], [(DFF, BF16), (D, BF16)], tm=_tall_tile(h.shape[0], tm))
    dwd = tn_matmul(tag + "_dwd", A, dyb)
    if emit is not None:
        emit("wd", dwd)

    def b2(i, dAv, Gv, Uv, hv, dhv, g_ref, wg_ref, wu_ref):
        dAf = dAv.astype(F32)
        Gf = Gv.astype(F32)
        sg = _sig(Gf)
        dG = (dAf * Uv.astype(F32) * (sg * (1.0 + Gf * (1.0 - sg)))).astype(BF16)
        dU = (dAf * (Gf * sg)).astype(BF16)
        dn = _nn(dG, wg_ref[...]) + _nn(dU, wu_ref[...])
        gv = g_ref[...]
        hh, r, _ = _rms_fwd(hv, gv)
        dx, dg = _rms_bwd(hh, r, gv, dn)
        dx = jnp.where(_row_ok(i, tm), dx, 0.0)
        return dhv + dx, dG, dU, dg

    dh2, dG, dU, dg = rowcall(tag + "_bwd_in", b2, [dA, G, U, h, dh], [g, wgT, wuT],
                              [(D, F32), (DFF, BF16), (DFF, BF16)], [(1, D)], tm=tm)
    dwgT = tn_matmul(tag + "_dwg", dG, n)
    if emit is not None:
        emit("wgT", dwgT)
    dwuT = tn_matmul(tag + "_dwu", dU, n)
    if emit is not None:
        emit("wuT", dwuT)
    return dh2, dg, dwgT, dwuT, dwd


def _alibi_slope(head):
    return float(2.0 ** (-8.0 * (head + 1) / N_HEADS))


def _att_bias(n, nb):
    qi = lax.broadcasted_iota(jnp.int32, (BLK, 4 * BLK), 0)
    cj = lax.broadcasted_iota(jnp.int32, (BLK, 4 * BLK), 1)
    jb = cj - BLK
    dist = jnp.abs(qi + BLK - jb)
    kpos = (n - 1) * BLK + jb
    band_ok = (dist <= WIN) & (kpos >= BLK) & (kpos < nb * BLK)
    is_meta = cj < BLK
    ok = (is_meta & (cj >= PAD)) | (jnp.logical_not(is_meta) & band_ok)
    distf = jnp.where(is_meta, 0, dist).astype(F32)
    maskadd = jnp.where(ok, 0.0, NEG).astype(F32)
    distf4 = jnp.concatenate([distf] * QG, axis=0)
    mask4 = jnp.concatenate([maskadd] * QG, axis=0)
    return distf4, mask4


def _group_col(vals):
    rg = lax.broadcasted_iota(jnp.int32, (QG * BLK, 1), 0) // BLK
    col = jnp.full((QG * BLK, 1), vals[QG - 1], F32)
    for gq in range(QG - 2, -1, -1):
        col = jnp.where(rg == gq, vals[gq], col)
    return col


def _stack_heads(ref_or_val, kh):
    return jnp.concatenate(
        [ref_or_val[:, (kh * QG + gq) * HD:(kh * QG + gq + 1) * HD] for gq in range(QG)], axis=0)


def _stack_keys(km, kp, kc, kn, kh):
    sl = slice(kh * HD, (kh + 1) * HD)
    return jnp.concatenate([km[:, sl], kp[:, sl], kc[:, sl], kn[:, sl]], axis=0)


LOG2E = 1.4426950408889634
LN2 = 0.6931471805599453
QSCALE = SCALE * LOG2E


def _att_update_bias(bias_ref, n, nb):
    @pl.when((n <= 2) | (n == nb - 1))
    def _():
        distf4, mask4 = _att_bias(n, nb)
        for kh in range(N_KV):
            slope_col = _group_col([_alibi_slope(kh * QG + gq) * LOG2E for gq in range(QG)])
            bias_ref[kh] = mask4 - slope_col * distf4


def _att_exp(qs, kb, bias_ref, kh, sink_ref):
    sink_col = _group_col([sink_ref[kh * QG + gq] for gq in range(QG)]) * LOG2E
    s = _nt(qs, kb) + bias_ref[kh]
    m = jnp.maximum(jnp.max(s, axis=1, keepdims=True), sink_col)
    e = jnp.exp2(s - m)
    es = jnp.exp2(sink_col - m)
    inv = 1.0 / (jnp.sum(e, axis=1, keepdims=True) + es)
    return e, es, inv


def attention_forward(tag, q, k, v, sink):
    LP = q.shape[0]
    nb = LP // BLK

    def kern(sink_ref, q_ref, km_ref, kp_ref, kc_ref, kn_ref, vm_ref, vp_ref, vc_ref, vn_ref, o_ref, bias_ref):
        n = pl.program_id(0)
        _att_update_bias(bias_ref, n, nb)
        qv = q_ref[...]
        km, kp, kc, kn = km_ref[...], kp_ref[...], kc_ref[...], kn_ref[...]
        vm, vp, vc, vn = vm_ref[...], vp_ref[...], vc_ref[...], vn_ref[...]
        for kh in range(N_KV):
            qs = _stack_heads(qv, kh)
            kb = _stack_keys(km, kp, kc, kn, kh)
            vb = _stack_keys(vm, vp, vc, vn, kh)
            e, _, inv = _att_exp(qs, kb, bias_ref, kh, sink_ref)
            o = _nn(e.astype(BF16), vb) * inv
            for gq in range(QG):
                hcol = (kh * QG + gq) * HD
                o_ref[:, hcol:hcol + HD] = o[gq * BLK:(gq + 1) * BLK].astype(o_ref.dtype)

    def kvspec(dn):
        return pl.BlockSpec((BLK, N_KV * HD), lambda n: (jnp.clip(n + dn, 0, nb - 1), 0))

    meta_spec = pl.BlockSpec((BLK, N_KV * HD), lambda n: (0, 0))
    return pl.pallas_call(
        kern, name=tag + "_att_fwd", grid=(nb,),
        in_specs=[pl.BlockSpec(memory_space=pltpu.SMEM), pl.BlockSpec((BLK, D), lambda n: (n, 0)),
                  meta_spec, kvspec(-1), kvspec(0), kvspec(1), meta_spec, kvspec(-1), kvspec(0), kvspec(1)],
        out_specs=pl.BlockSpec((BLK, D), lambda n: (n, 0)),
        out_shape=jax.ShapeDtypeStruct((LP, D), BF16),
        scratch_shapes=[pltpu.VMEM((N_KV, QG * BLK, 4 * BLK), F32)], compiler_params=_params("arbitrary"),
    )(sink, q, k, k, k, k, v, v, v, v)


def attention_backward(tag, q, k, v, do, sink):
    LP = q.shape[0]
    nb = LP // BLK
    KW = N_KV * HD

    def kern(sink_ref, q_ref, do_ref, km_ref, kp_ref, kc_ref, kn_ref, vm_ref, vp_ref, vc_ref, vn_ref,
             dq_ref, dk_ref, dv_ref, dkm_ref, dvm_ref, dsink_ref, bias_ref, rk, rv, fk, fv):
        n = pl.program_id(0)

        @pl.when(n == 0)
        def _():
            dkm_ref[...] = jnp.zeros_like(dkm_ref)
            dvm_ref[...] = jnp.zeros_like(dvm_ref)
            dsink_ref[...] = jnp.zeros_like(dsink_ref)
            rk[...] = jnp.zeros_like(rk)
            rv[...] = jnp.zeros_like(rv)

        _att_update_bias(bias_ref, n, nb)

        @pl.when(n < nb)
        def _():
            qv, dov = q_ref[...], do_ref[...]
            km, kp, kc, kn = km_ref[...], kp_ref[...], kc_ref[...], kn_ref[...]
            vm, vp, vc, vn = vm_ref[...], vp_ref[...], vc_ref[...], vn_ref[...]
            lane = lax.broadcasted_iota(jnp.int32, (8, 128), 1)
            dsink = jnp.zeros((8, 128), F32)
            for kh in range(N_KV):
                qs = _stack_heads(qv, kh)
                dos = _stack_heads(dov, kh)
                kb = _stack_keys(km, kp, kc, kn, kh)
                vb = _stack_keys(vm, vp, vc, vn, kh)
                dp = _nt(dos, vb)
                e, es, inv = _att_exp(qs, kb, bias_ref, kh, sink_ref)
                delta = inv * jnp.sum(e * dp, axis=1, keepdims=True)
                dsu = (e * (dp - delta)).astype(BF16)
                dqs = _nn(dsu, kb) * (inv * SCALE)
                dkt = _tn((qs.astype(F32) * (inv * LN2)).astype(BF16), dsu)
                dvt = _tn((dos.astype(F32) * inv).astype(BF16), e.astype(BF16))
                dsk = -(es * inv * delta)
                for gq in range(QG):
                    hcol = (kh * QG + gq) * HD
                    dq_ref[:, hcol:hcol + HD] = dqs[gq * BLK:(gq + 1) * BLK].astype(dq_ref.dtype)
                    tot = jnp.sum(dsk[gq * BLK:(gq + 1) * BLK], axis=0, keepdims=True)
                    dsink = dsink + jnp.where(lane == kh * QG + gq, tot, 0.0)
                hs = slice(kh * HD, (kh + 1) * HD)
                dkm_ref[hs, :] += dkt[:, 0:BLK]
                dvm_ref[hs, :] += dvt[:, 0:BLK]
                for ring, fin, part in ((rk, fk, dkt), (rv, fv, dvt)):
                    fin[hs, :] = ring[0, hs, :] + part[:, BLK:2 * BLK]
                    ring[0, hs, :] = ring[1, hs, :] + part[:, 2 * BLK:3 * BLK]
                    ring[1, hs, :] = part[:, 3 * BLK:4 * BLK]
            dsink_ref[...] += dsink
            dk_ref[...] = fk[...].T.astype(dk_ref.dtype)
            dv_ref[...] = fv[...].T.astype(dv_ref.dtype)

        @pl.when(n == nb)
        def _():
            dk_ref[...] = rk[0].T.astype(dk_ref.dtype)
            dv_ref[...] = rv[0].T.astype(dv_ref.dtype)

    def kvspec(dn):
        return pl.BlockSpec((BLK, KW), lambda n: (jnp.clip(jnp.minimum(n, nb - 1) + dn, 0, nb - 1), 0))

    meta_spec = pl.BlockSpec((BLK, KW), lambda n: (0, 0))
    rowspec = pl.BlockSpec((BLK, D), lambda n: (jnp.minimum(n, nb - 1), 0))
    emit_spec = pl.BlockSpec((BLK, KW), lambda n: (jnp.clip(n - 1, 1, nb - 1), 0))
    dq, dk, dv, dkm, dvm, dsink = pl.pallas_call(
        kern, name=tag + "_att_bwd", grid=(nb + 1,),
        in_specs=[pl.BlockSpec(memory_space=pltpu.SMEM), rowspec, rowspec,
                  meta_spec, kvspec(-1), kvspec(0), kvspec(1), meta_spec, kvspec(-1), kvspec(0), kvspec(1)],
        out_specs=[rowspec, emit_spec, emit_spec, _const_spec((KW, BLK)), _const_spec((KW, BLK)),
                   _const_spec((8, 128))],
        out_shape=[jax.ShapeDtypeStruct((LP, D), BF16), jax.ShapeDtypeStruct((LP, KW), BF16),
                   jax.ShapeDtypeStruct((LP, KW), BF16), jax.ShapeDtypeStruct((KW, BLK), F32),
                   jax.ShapeDtypeStruct((KW, BLK), F32), jax.ShapeDtypeStruct((8, 128), F32)],
        scratch_shapes=[pltpu.VMEM((N_KV, QG * BLK, 4 * BLK), F32), pltpu.VMEM((2, KW, BLK), F32),
                        pltpu.VMEM((2, KW, BLK), F32), pltpu.VMEM((KW, BLK), F32), pltpu.VMEM((KW, BLK), F32)],
        compiler_params=_params("arbitrary"),
    )(sink, q, do, k, k, k, k, v, v, v, v)
    dk = lax.dynamic_update_slice(dk, dkm.T.astype(BF16), (0, 0))
    dv = lax.dynamic_update_slice(dv, dvm.T.astype(BF16), (0, 0))
    return dq, dk, dv, dsink


SCAN_LANES = 1024


def _scan_tile(xr, xi, cr, ci, a8, tab, seg, reverse):
    sub = lax.broadcasted_iota(jnp.int32, (8, SCAN_LANES), 0)
    for c0 in range(0, NST, SCAN_LANES):
        cs = pl.ds(c0, SCAN_LANES)
        ar = a8[0, :, cs]
        ai = a8[1, :, cs]

        def rows(j):
            jj = (seg - 1 - j) if reverse else j
            return pl.ds(jj * 8, 8)

        def step1(j, carry):
            vr, vi = carry
            rs = rows(j)
            nr = ar * vr - ai * vi + xr[rs, cs]
            ni = ar * vi + ai * vr + xi[rs, cs]
            xr[rs, cs] = nr
            xi[rs, cs] = ni
            return nr, ni

        zero = jnp.zeros((8, SCAN_LANES), F32)
        vr, vi = zero, zero
        for j in range(seg):
            vr, vi = step1(j, (vr, vi))
        for t, s in enumerate((1, 2, 4)):
            sh = (8 - s) if reverse else s
            sr = pltpu.roll(vr, sh, 0)
            si = pltpu.roll(vi, sh, 0)
            tr = tab[2 * t, :, cs]
            ti = tab[2 * t + 1, :, cs]
            vr, vi = vr + tr * sr - ti * si, vi + tr * si + ti * sr
        pr = tab[6, :, cs]
        pi = tab[7, :, cs]
        c_r = cr[:, cs]
        c_i = ci[:, cs]
        vr, vi = vr + pr * c_r - pi * c_i, vi + pr * c_i + pi * c_r
        edge = 7 if reverse else 0
        last = 0 if reverse else 7
        sh = 7 if reverse else 1
        in_r = jnp.where(sub == edge, c_r, pltpu.roll(vr, sh, 0))
        in_i = jnp.where(sub == edge, c_i, pltpu.roll(vi, sh, 0))
        cr[:, cs] = jnp.broadcast_to(vr[last:last + 1, :], (8, SCAN_LANES))
        ci[:, cs] = jnp.broadcast_to(vi[last:last + 1, :], (8, SCAN_LANES))

        def step2(j, carry):
            dr, di = carry
            rs = rows(j)
            ndr = ar * dr - ai * di
            ndi = ar * di + ai * dr
            xr[rs, cs] += ndr
            xi[rs, cs] += ndi
            return ndr, ndi

        dr, di = in_r, in_i
        for j in range(seg):
            dr, di = step2(j, (dr, di))


ST_T = 4 * SP * 2
CH_T = 128


def _load_segmented(ref, scr, seg):
    out = []
    for ct in range(4):
        scr[ct] = ref[:, ct * CH_T:(ct + 1) * CH_T]
        out.append(jnp.concatenate([scr[ct, pl.ds(j, 8, stride=seg), :] for j in range(seg)], axis=0))
    return out


def _store_segmented(ref, scr, vals, seg):
    for ct in range(4):
        for j in range(seg):
            scr[ct, pl.ds(j, 8, stride=seg), :] = vals[ct][8 * j:8 * j + 8]
        ref[:, ct * CH_T:(ct + 1) * CH_T] = scr[ct].astype(ref.dtype)


def ssm_dir_forward(tag, u, bpr, bpi, cpr, cpi, a8, tab, reverse, tm):
    LP = u.shape[0]
    nt = LP // tm
    seg = tm // 8

    def rix(i):
        return (nt - 1 - i) if reverse else i

    def kern(u_ref, bpr_ref, bpi_ref, cpr_ref, cpi_ref, a8_ref, tab_ref, xre_ref, xim_ref, y_ref,
             xr, xi, ys, cr, ci):
        i = pl.program_id(0)

        @pl.when(i == 0)
        def _():
            cr[...] = jnp.zeros_like(cr)
            ci[...] = jnp.zeros_like(ci)

        ub = _load_segmented(u_ref, ys, seg)
        for ct in range(4):
            uc = ub[ct].astype(BF16)
            xr[:, ct * ST_T:(ct + 1) * ST_T] = _nn(uc, bpr_ref[ct * CH_T:(ct + 1) * CH_T, :])
            xi[:, ct * ST_T:(ct + 1) * ST_T] = _nn(uc, bpi_ref[ct * CH_T:(ct + 1) * CH_T, :])
        _scan_tile(xr, xi, cr, ci, a8_ref, tab_ref, seg, reverse)
        xrb = xr[...].astype(BF16)
        xib = xi[...].astype(BF16)
        xre_ref[...] = xrb
        xim_ref[...] = xib
        yv = []
        for ct in range(4):
            ss = slice(ct * ST_T, (ct + 1) * ST_T)
            yv.append(_nn(xrb[:, ss], cpr_ref[ss, :]) - _nn(xib[:, ss], cpi_ref[ss, :]))
        _store_segmented(y_ref, ys, yv, seg)

    row = lambda w: pl.BlockSpec((tm, w), lambda i: (rix(i), 0))
    return pl.pallas_call(
        kern, name=tag, grid=(nt,),
        in_specs=[row(SW), _const_spec(bpr.shape), _const_spec(bpi.shape), _const_spec(cpr.shape),
                  _const_spec(cpi.shape), _const_spec(a8.shape), _const_spec(tab.shape)],
        out_specs=[row(NST), row(NST), row(SW)],
        out_shape=[jax.ShapeDtypeStruct((LP, NST), BF16), jax.ShapeDtypeStruct((LP, NST), BF16),
                   jax.ShapeDtypeStruct((LP, SW), F32)],
        scratch_shapes=[pltpu.VMEM((tm, NST), F32), pltpu.VMEM((tm, NST), F32), pltpu.VMEM((4, tm, CH_T), F32),
                        pltpu.VMEM((8, NST), F32), pltpu.VMEM((8, NST), F32)],
        compiler_params=_params("arbitrary"),
    )(u, bpr, bpi, cpr, cpi, a8, tab)


def ssm_dir_backward(tag, dy, xre, xim, u, bpr, bpi, cpr, cpi, a8_adj, tab_adj, reverse, tm):
    LP = u.shape[0]
    nt = LP // tm
    seg = tm // 8

    def rix(i):
        return (nt - 1 - i) if reverse else i

    def kern(dy_ref, xre_ref, xim_ref, u_ref, bpr_ref, bpi_ref, cpr_ref, cpi_ref, a8_ref, tab_ref,
             du_ref, gbr_ref, gbi_ref, gcr_ref, gci_ref, sr_ref, si_ref, lr, li, gr, gi, dus, cr, ci):
        i = pl.program_id(0)

        @pl.when(i == 0)
        def _():
            cr[...] = jnp.zeros_like(cr)
            ci[...] = jnp.zeros_like(ci)
            for r in (gbr_ref, gbi_ref, gcr_ref, gci_ref, sr_ref, si_ref):
                r[...] = jnp.zeros_like(r)

        dyb = [v.astype(BF16) for v in _load_segmented(dy_ref, dus, seg)]
        ub = [v.astype(BF16) for v in _load_segmented(u_ref, dus, seg)]
        for ct in range(4):
            ss = slice(ct * ST_T, (ct + 1) * ST_T)
            dc = dyb[ct]
            g_re = _nt(dc, cpr_ref[ss, :])
            g_im = -_nt(dc, cpi_ref[ss, :])
            lr[:, ss] = g_re
            li[:, ss] = g_im
            gr[:, ss] = g_re
            gi[:, ss] = g_im
        _scan_tile(lr, li, cr, ci, a8_ref, tab_ref, seg, reverse)
        lam_r = lr[...]
        lam_i = li[...]
        wr = lam_r - gr[...]
        wi = lam_i - gi[...]
        xr = xre_ref[...].astype(F32)
        xi = xim_ref[...].astype(F32)
        sr_ref[...] += jnp.sum(wr * xr + wi * xi, axis=0, keepdims=True)
        si_ref[...] += jnp.sum(wi * xr - wr * xi, axis=0, keepdims=True)
        lrb = lam_r.astype(BF16)
        lib = lam_i.astype(BF16)
        xrb = xre_ref[...]
        xib = xim_ref[...]
        duv = []
        for ct in range(4):
            ss = slice(ct * ST_T, (ct + 1) * ST_T)
            cs = slice(ct * CH_T, (ct + 1) * CH_T)
            duv.append(_nt(lrb[:, ss], bpr_ref[cs, :]) + _nt(lib[:, ss], bpi_ref[cs, :]))
            gbr_ref[cs, :] += _tn(ub[ct], lrb[:, ss])
            gbi_ref[cs, :] += _tn(ub[ct], lib[:, ss])
            gcr_ref[cs, :] += _tn(dyb[ct], xrb[:, ss])
            gci_ref[cs, :] -= _tn(dyb[ct], xib[:, ss])
        _store_segmented(du_ref, dus, duv, seg)

    row = lambda w: pl.BlockSpec((tm, w), lambda i: (rix(i), 0))
    acc = _const_spec((SW, ST_T))
    vec = _const_spec((1, NST))
    return pl.pallas_call(
        kern, name=tag, grid=(nt,),
        in_specs=[row(SW), row(NST), row(NST), row(SW), _const_spec(bpr.shape), _const_spec(bpi.shape),
                  _const_spec(cpr.shape), _const_spec(cpi.shape), _const_spec(a8_adj.shape),
                  _const_spec(tab_adj.shape)],
        out_specs=[row(SW), acc, acc, acc, acc, vec, vec],
        out_shape=[jax.ShapeDtypeStruct((LP, SW), BF16)] + [jax.ShapeDtypeStruct((SW, ST_T), F32)] * 4
        + [jax.ShapeDtypeStruct((1, NST), F32)] * 2,
        scratch_shapes=[pltpu.VMEM((tm, NST), F32)] * 4 + [pltpu.VMEM((4, tm, CH_T), F32)]
        + [pltpu.VMEM((8, NST), F32)] * 2,
        compiler_params=_params("arbitrary"),
    )(dy, xre, xim, u, bpr, bpi, cpr, cpi, a8_adj, tab_adj)


def _ssm_disc(lam_re, lam_im, log_dt, b_re, b_im):
    dt = jnp.exp(log_dt)[:, None]
    mag = jnp.exp(lam_re * dt)
    a_re = mag * jnp.cos(lam_im * dt)
    a_im = mag * jnp.sin(lam_im * dt)
    den = lam_re * lam_re + lam_im * lam_im
    f_re = ((a_re - 1.0) * lam_re + a_im * lam_im) / den
    f_im = (a_im * lam_re - (a_re - 1.0) * lam_im) / den
    bb_re = f_re[:, :, None] * b_re - f_im[:, :, None] * b_im
    bb_im = f_re[:, :, None] * b_im + f_im[:, :, None] * b_re
    return a_re, a_im, bb_re, bb_im


def _scan_tables(lam_re, lam_im, log_dt, conj, reverse, seg):
    dt = jnp.exp(log_dt)[:, None]
    lr = (lam_re * dt).reshape(1, NST)
    li = (lam_im * dt).reshape(1, NST) * (-1.0 if conj else 1.0)
    t = jnp.arange(8, dtype=F32)[:, None]

    def power(kk):
        mag = jnp.exp(kk * lr)
        return mag * jnp.cos(kk * li), mag * jnp.sin(kk * li)

    ones = jnp.ones((8, 1), F32)
    a8 = jnp.stack(power(ones)).astype(F32)
    tabs = []
    for s in (1, 2, 4):
        mask = (t <= 7 - s) if reverse else (t >= s)
        pr, pi = power(float(s * seg) * ones)
        tabs += [jnp.where(mask, pr, 0.0), jnp.where(mask, pi, 0.0)]
    kk = ((8.0 - t) if reverse else (t + 1.0)) * float(seg)
    pr, pi = power(kk)
    tabs += [pr, pi]
    return a8, jnp.stack(tabs).astype(F32)


def _pack_b(bb):
    t = bb.transpose(0, 2, 1).reshape(4, 8, SCH, SP)
    eye = jnp.eye(8, dtype=bb.dtype)
    return jnp.einsum('tgcp,gh->tgchp', t, eye).reshape(SW, ST_T)


def _pack_c(cc):
    t = cc.transpose(0, 2, 1).reshape(4, 8, SP, SCH)
    eye = jnp.eye(8, dtype=cc.dtype)
    return jnp.einsum('tgpc,gh->tgphc', t, eye).reshape(NST, CH_T)


def _unpack_diag(acc):
    t = acc.reshape(4, 8, SCH, 8, SP)
    eye = jnp.eye(8, dtype=acc.dtype)
    return jnp.einsum('tgchp,gh->tgcp', t, eye).reshape(SGRP, SCH, SP)


def _gelu(y):
    k0 = math.sqrt(2.0 / math.pi)
    inner = k0 * (y + 0.044715 * y * y * y)
    th = jnp.tanh(inner)
    z = 0.5 * y * (1.0 + th)
    dz = 0.5 * (1.0 + th) + 0.5 * y * (1.0 - th * th) * k0 * (1.0 + 3.0 * 0.044715 * y * y)
    return z, dz


Q0, K0, V0, U0, GS0, GA0, IN_COLS = 0, 1024, 1280, 1536, 2048, 3072, 4096


def mixer_forward(tag, h, p, tm):
    g, winT, wglu, wbsT, wba, wout = p["g"], p["winT"], p["wglu"], p["wbsT"], p["wba"], p["wout"]

    def proj(i, hv, g_ref, w_ref):
        _, _, n = _rms_fwd(hv, g_ref[...])
        nb = n.astype(BF16)
        return (nb, _nt(nb, w_ref[Q0:K0, :]) * QSCALE, _nt(nb, w_ref[K0:V0, :]), _nt(nb, w_ref[V0:U0, :]),
                _nt(nb, w_ref[U0:GS0, :]), _nt(nb, w_ref[GS0:GA0, :]), _nt(nb, w_ref[GA0:IN_COLS, :]))

    n, q, k, v, u, gs, ga = rowcall(
        tag + "_proj", proj, [h], [g, winT],
        [(D, BF16), (D, BF16), (N_KV * HD, BF16), (N_KV * HD, BF16), (SW, F32), (D, BF16), (D, BF16)],
        tm=_tall_tile(h.shape[0], tm))

    ya = attention_forward(tag, q, k, v, p["sink"])

    states, ydir = [], []
    for dr in range(2):
        s = p["ssm"][dr]
        xre, xim, yd = ssm_dir_forward(f"{tag}_ssm_fwd{dr}", u, s["bpr"], s["bpi"], s["cpr"], s["cpi"],
                                       s["a8"], s["tab"], dr == 1, tm)
        states.append((xre, xim))
        ydir.append(yd)

    def merge(i, y0, y1, uv, yav, gsv, gav, hv, d_ref, wglu_ref, wbs_ref, wba_ref, wout_ref):
        ypre = y0 + y1 + d_ref[...] * uv
        z, _ = _gelu(ypre)
        zb = z.astype(BF16)
        t = _nn(zb, wglu_ref[...])
        ysb = (z * _sig(t)).astype(BF16)
        bs = _nt(ysb, wbs_ref[...])
        ba = _nn(yav, wba_ref[...])
        mg = _sig(gsv.astype(F32)) * bs + _sig(gav.astype(F32)) * ba
        mg = jnp.where(_row_ok(i, tm), mg, 0.0).astype(BF16)
        return ypre, zb, t, ysb, bs, ba, mg, hv + _nn(mg, wout_ref[...])

    ypre, zb, t, ys, bs, ba, mg, h2 = rowcall(
        tag + "_merge", merge, [ydir[0], ydir[1], u, ya, gs, ga, h], [p["d"], wglu, wbsT, wba, wout],
        [(SW, BF16), (SW, BF16), (SW, BF16), (SW, BF16), (D, BF16), (D, BF16), (D, BF16), (D, F32)], tm=tm)
    saved = dict(h=h, n=n, q=q, k=k, v=v, u=u, gs=gs, ga=ga, ya=ya, states=states, ypre=ypre, zb=zb, t=t,
                 ys=ys, bs=bs, ba=ba, mg=mg)
    return h2, saved


def mixer_backward(tag, dh, sv, p, tm):
    g, winT, wglu, wbsT, wba, wout = p["g"], p["winT"], p["wglu"], p["wbsT"], p["wba"], p["wout"]

    def y1(i, dhv, bsv, bav, gsv, gav, ypv, tv, uv, wout_ref, wbs_ref, wba_ref, d_ref, wglu_ref):
        dhb = dhv.astype(BF16)
        dmg = _nt(dhb, wout_ref[...])
        dmg = jnp.where(_row_ok(i, tm), dmg, 0.0)
        sgs = _sig(gsv.astype(F32))
        sga = _sig(gav.astype(F32))
        dbs = (dmg * sgs).astype(BF16)
        dba = (dmg * sga).astype(BF16)
        dgs = dmg * bsv.astype(F32) * sgs * (1.0 - sgs)
        dga = dmg * bav.astype(F32) * sga * (1.0 - sga)
        dys = _nn(dbs, wbs_ref[...])
        dya = _nt(dba, wba_ref[...])
        z, dz_dy = _gelu(ypv.astype(F32))
        st = _sig(tv.astype(F32))
        dt_ = dys * z * st * (1.0 - st)
        dz = dys * st + _nt(dt_.astype(BF16), wglu_ref[...])
        dyp = dz * dz_dy
        return (dbs, dba, dgs, dga, dhb, dya, dyp, dyp * d_ref[...], dt_,
                jnp.sum(dyp * uv, axis=0, keepdims=True))

    dbs, dba, dgs, dga, dhb, dya, dypb, du0, dtb, dd = rowcall(
        tag + "_bwd_merge", y1,
        [dh, sv["bs"], sv["ba"], sv["gs"], sv["ga"], sv["ypre"], sv["t"], sv["u"]],
        [wout, wbsT, wba, p["d"], wglu],
        [(D, BF16)] * 6 + [(SW, F32), (SW, BF16), (SW, BF16)], [(1, SW)], tm=tm)
    dwout = tn_matmul(tag + "_dwout", sv["mg"], dhb)
    dwbsT = tn_matmul(tag + "_dwbs", dbs, sv["ys"])
    dwba = tn_matmul(tag + "_dwba", sv["ya"], dba)
    dwglu = tn_matmul(tag + "_dwglu", sv["zb"], dtb)

    du_dirs, ssm_sums = [], []
    for dr in range(2):
        s = p["ssm"][dr]
        xre, xim = sv["states"][dr]
        res = ssm_dir_backward(f"{tag}_ssm_bwd{dr}", dypb, xre, xim, sv["u"], s["bpr"], s["bpi"], s["cpr"],
                               s["cpi"], s["a8_adj"], s["tab_adj"], dr == 0, tm)
        du_dirs.append(res[0])
        ssm_sums.append(res[1:])

    dq, dk, dv, dsink = attention_backward(tag, sv["q"], sv["k"], sv["v"], dya, p["sink"])

    def x1b(i, dqv, dkv, dvv, du0v, du1v, du2v, dgsv, dgav, hv, dhv, g_ref, w_ref):
        dub = (du0v.astype(F32) + du1v.astype(F32) + du2v.astype(F32)).astype(BF16)
        dn = (_nn(dqv, w_ref[Q0:K0, :]) + _nn(dkv, w_ref[K0:V0, :]) + _nn(dvv, w_ref[V0:U0, :])
              + _nn(dub, w_ref[U0:GS0, :]) + _nn(dgsv, w_ref[GS0:GA0, :]) + _nn(dgav, w_ref[GA0:IN_COLS, :]))
        gv = g_ref[...]
        hh, r, _ = _rms_fwd(hv, gv)
        dx, dg = _rms_bwd(hh, r, gv, dn)
        dx = jnp.where(_row_ok(i, tall), dx, 0.0)
        return dhv + dx, dub, dg

    tall = _tall_tile(dh.shape[0], tm)
    dh2, dub, dg = rowcall(tag + "_bwd_in", x1b,
                           [dq, dk, dv, du0, du_dirs[0], du_dirs[1], dgs, dga, sv["h"], dh], [g, winT],
                           [(D, F32), (SW, BF16)], [(1, D)], tm=tall)
    n = sv["n"]
    dwinT = jnp.concatenate([tn_matmul(f"{tag}_dwin{j}", piece, n)
                             for j, piece in enumerate((dq, dk, dv, dub, dgs, dga))], axis=0)
    grads = dict(g=dg, d=dd, sink=dsink, ssm=ssm_sums, winT=dwinT, wglu=dwglu, wbsT=dwbsT, wba=dwba, wout=dwout)
    return dh2, grads


W1024 = ("f1_wgT", "f1_wuT", "f1_wd", "winT", "wba", "wout", "f2_wgT", "f2_wuT", "f2_wd")
W512 = ("wglu", "wbsT")
PART_F1 = ("f1_wgT", "f1_wuT", "f1_wd")
PART_MIX = ("winT", "wba", "wout", "wglu", "wbsT")
PART_F2 = ("f2_wgT", "f2_wuT", "f2_wd")
PER_LAYER_SMALL = ("ffn1_norm", "mix_norm", "ffn2_norm", "ssm_lam_re", "ssm_lam_im", "ssm_log_dt",
                   "ssm_b_re", "ssm_b_im", "ssm_c_re", "ssm_c_im", "ssm_d", "attn_sink")
SMALL = ("ffn1_norm", "mix_norm", "ffn2_norm", "final_norm", "ssm_lam_re", "ssm_lam_im", "ssm_log_dt",
         "ssm_b_re", "ssm_b_im", "ssm_c_re", "ssm_c_im", "ssm_d", "attn_sink")


def kernel(x, meta_tokens, ffn1_norm, ffn1_w_gate, ffn1_w_up, ffn1_w_down, mix_norm, w_in, ssm_lam_re, ssm_lam_im, ssm_log_dt, ssm_b_re, ssm_b_im, ssm_c_re, ssm_c_im, ssm_d, ssm_w_glu, attn_sink, w_branch_ssm, w_branch_attn, w_out, ffn2_norm, ffn2_w_gate, ffn2_w_up, ffn2_w_down, final_norm, loss_target, m_meta_tokens, m_ffn1_norm, m_ffn1_w_gate, m_ffn1_w_up, m_ffn1_w_down, m_mix_norm, m_w_in, m_ssm_lam_re, m_ssm_lam_im, m_ssm_log_dt, m_ssm_b_re, m_ssm_b_im, m_ssm_c_re, m_ssm_c_im, m_ssm_d, m_ssm_w_glu, m_attn_sink, m_w_branch_ssm, m_w_branch_attn, m_w_out, m_ffn2_norm, m_ffn2_w_gate, m_ffn2_w_up, m_ffn2_w_down, m_final_norm, v_meta_tokens, v_ffn1_norm, v_ffn1_w_gate, v_ffn1_w_up, v_ffn1_w_down, v_mix_norm, v_w_in, v_ssm_lam_re, v_ssm_lam_im, v_ssm_log_dt, v_ssm_b_re, v_ssm_b_im, v_ssm_c_re, v_ssm_c_im, v_ssm_d, v_ssm_w_glu, v_attn_sink, v_w_branch_ssm, v_w_branch_attn, v_w_out, v_ffn2_norm, v_ffn2_w_gate, v_ffn2_w_up, v_ffn2_w_down, v_final_norm):
    weights = dict(meta_tokens=meta_tokens, ffn1_norm=ffn1_norm, ffn1_w_gate=ffn1_w_gate, ffn1_w_up=ffn1_w_up, ffn1_w_down=ffn1_w_down, mix_norm=mix_norm, w_in=w_in, ssm_lam_re=ssm_lam_re, ssm_lam_im=ssm_lam_im, ssm_log_dt=ssm_log_dt, ssm_b_re=ssm_b_re, ssm_b_im=ssm_b_im, ssm_c_re=ssm_c_re, ssm_c_im=ssm_c_im, ssm_d=ssm_d, ssm_w_glu=ssm_w_glu, attn_sink=attn_sink, w_branch_ssm=w_branch_ssm, w_branch_attn=w_branch_attn, w_out=w_out, ffn2_norm=ffn2_norm, ffn2_w_gate=ffn2_w_gate, ffn2_w_up=ffn2_w_up, ffn2_w_down=ffn2_w_down, final_norm=final_norm)
    mom_m = dict(meta_tokens=m_meta_tokens, ffn1_norm=m_ffn1_norm, ffn1_w_gate=m_ffn1_w_gate, ffn1_w_up=m_ffn1_w_up, ffn1_w_down=m_ffn1_w_down, mix_norm=m_mix_norm, w_in=m_w_in, ssm_lam_re=m_ssm_lam_re, ssm_lam_im=m_ssm_lam_im, ssm_log_dt=m_ssm_log_dt, ssm_b_re=m_ssm_b_re, ssm_b_im=m_ssm_b_im, ssm_c_re=m_ssm_c_re, ssm_c_im=m_ssm_c_im, ssm_d=m_ssm_d, ssm_w_glu=m_ssm_w_glu, attn_sink=m_attn_sink, w_branch_ssm=m_w_branch_ssm, w_branch_attn=m_w_branch_attn, w_out=m_w_out, ffn2_norm=m_ffn2_norm, ffn2_w_gate=m_ffn2_w_gate, ffn2_w_up=m_ffn2_w_up, ffn2_w_down=m_ffn2_w_down, final_norm=m_final_norm)
    mom_v = dict(meta_tokens=v_meta_tokens, ffn1_norm=v_ffn1_norm, ffn1_w_gate=v_ffn1_w_gate, ffn1_w_up=v_ffn1_w_up, ffn1_w_down=v_ffn1_w_down, mix_norm=v_mix_norm, w_in=v_w_in, ssm_lam_re=v_ssm_lam_re, ssm_lam_im=v_ssm_lam_im, ssm_log_dt=v_ssm_log_dt, ssm_b_re=v_ssm_b_re, ssm_b_im=v_ssm_b_im, ssm_c_re=v_ssm_c_re, ssm_c_im=v_ssm_c_im, ssm_d=v_ssm_d, ssm_w_glu=v_ssm_w_glu, attn_sink=v_attn_sink, w_branch_ssm=v_w_branch_ssm, w_branch_attn=v_w_branch_attn, w_out=v_w_out, ffn2_norm=v_ffn2_norm, ffn2_w_gate=v_ffn2_w_gate, ffn2_w_up=v_ffn2_w_up, ffn2_w_down=v_ffn2_w_down, final_norm=v_final_norm)
    names = list(weights)

    L0 = x.shape[1]
    LP = L0 + BLK
    tm = 384 if LP % 384 == 0 else BLK
    x_i, y_i, c_i = lax.axis_index("x"), lax.axis_index("y"), lax.axis_index("c")
    me = 4 * x_i + 2 * y_i + c_i

    def canon(l):
        return dict(
            f1_wgT=ffn1_w_gate[l].T, f1_wuT=ffn1_w_up[l].T, f1_wd=ffn1_w_down[l],
            winT=w_in[l].T, wba=w_branch_attn[l], wout=w_out[l],
            f2_wgT=ffn2_w_gate[l].T, f2_wuT=ffn2_w_up[l].T, f2_wd=ffn2_w_down[l],
            wglu=ssm_w_glu[l], wbsT=w_branch_ssm[l].T)

    shards = [{nm: a.astype(BF16) for nm, a in canon(l).items()} for l in range(DEPTH)]

    def rows_of(nm):
        return shards[0][nm].shape[0]

    def width_groups(names_):
        return [g for g in ([nm for nm in names_ if nm in W1024], [nm for nm in names_ if nm in W512]) if g]

    def pieces_for(group):
        out, off = [], 0
        for nm in group:
            out.append((off, rows_of(nm)))
            off += rows_of(nm)
        return out

    def start_gather(tag, l, names_):
        groups = width_groups(names_)
        packed = [jnp.concatenate([shards[l][nm] for nm in g], axis=0) for g in groups]
        return gather_layer_start(tag, packed, [pieces_for(g) for g in groups]), groups

    def finish_gather(tag, l, started_, after):
        handle, groups = started_
        dests = gather_layer_wait(tag, handle, len(groups), after)
        out = {}
        for nm, dest in zip([nm for g in groups for nm in g], dests):
            sh = shards[l][nm]
            out[nm] = lax.dynamic_update_slice(dest, sh, (me * sh.shape[0], 0))
        return out

    g1, gm = all_gather_pieces(
        "gather_weights_first",
        [(jnp.concatenate([shards[0][nm] for nm in PART_F1], axis=0), pieces_for(PART_F1)),
         (meta_tokens, [(0, N_META)])])
    first_weights = dict(zip(PART_F1, g1))
    meta_full = gm[0].reshape(NDEV, N_META, D // NDEV).transpose(1, 0, 2).reshape(N_META, D)
    gather_started = [start_gather("gather_start_l0", 0, PART_MIX + PART_F2)]
    gather_started += [start_gather(f"gather_start_l{l}", l, W1024 + W512) for l in range(1, DEPTH)]
    started = sum(st[0][3][0, 0] for st in gather_started)
    full = [None] * DEPTH

    def disc_all(lre, lim, ldt, bre, bim):
        return _ssm_disc(lre, lim, ldt, bre, bim)

    ssm_p, ssm_vjp = [], []
    for l in range(DEPTH):
        row, vrow = [], []
        for dr in range(2):
            args = (ssm_lam_re[l, dr], ssm_lam_im[l, dr], ssm_log_dt[l, dr], ssm_b_re[l, dr], ssm_b_im[l, dr])
            (a_re, a_im, bb_re, bb_im), vjp = jax.vjp(disc_all, *args)
            a8, tab = _scan_tables(args[0], args[1], args[2], False, dr == 1, tm // 8)
            a8_adj, tab_adj = _scan_tables(args[0], args[1], args[2], True, dr == 0, tm // 8)
            row.append(dict(
                bpr=_pack_b(bb_re).astype(BF16), bpi=_pack_b(bb_im).astype(BF16),
                cpr=_pack_c(ssm_c_re[l, dr]).astype(BF16), cpi=_pack_c(ssm_c_im[l, dr]).astype(BF16),
                a8=a8, tab=tab, a8_adj=a8_adj, tab_adj=tab_adj, a_re=a_re, a_im=a_im))
            vrow.append(vjp)
        ssm_p.append(row)
        ssm_vjp.append(vrow)

    blk0 = jnp.concatenate([jnp.zeros((PAD, D), F32), meta_full.astype(F32)], axis=0)
    h = build_h0(x[0], blk0)
    saved = []
    for l in range(DEPTH):
        w = dict(first_weights) if l == 0 else finish_gather(f"gather_wait_l{l}", l, gather_started[l], h)
        full[l] = w
        g1n, g2n = ffn1_norm[l][None, :], ffn2_norm[l][None, :]
        if l == 0:
            g1n = g1n + started
        h, s1 = ffn_forward("ffn1", h, g1n, w["f1_wgT"], w["f1_wuT"], w["f1_wd"], tm)
        if l == 0:
            w.update(finish_gather("gather_wait_l0", 0, gather_started[0], h))
        mp = dict(g=mix_norm[l][None, :], winT=w["winT"], wglu=w["wglu"], wbsT=w["wbsT"], wba=w["wba"],
                  wout=w["wout"], d=ssm_d[l][None, :], sink=attn_sink[l], ssm=ssm_p[l])
        h, s2 = mixer_forward("mix", h, mp, tm)
        h, s3 = ffn_forward("ffn2", h, g2n, w["f2_wgT"], w["f2_wuT"], w["f2_wd"], tm)
        saved.append((s1, s2, s3, mp, g1n, g2n))

    dh, loss_acc, dgf = final_loss(h, loss_target[0], final_norm[None, :])
    loss = lax.psum(loss_acc[0, 0], MESH_AXES)

    small = {nm: [None] * DEPTH for nm in SMALL if nm != "final_norm"}

    def start_scatter(tag, grads_d, names_):
        groups = width_groups(names_)
        mine = [jnp.concatenate([lax.dynamic_slice_in_dim(grads_d[nm], me * rows_of(nm), rows_of(nm), axis=0)
                                 for nm in g], axis=0) for g in groups]
        handle, nin = scatter_layer_start(tag + "_start", [[grads_d[nm] for nm in g] for g in groups])
        return dict(tag=tag, handle=handle, nin=nin, groups=groups, mine=mine)

    def finish_scatter(st, after):
        lands = scatter_layer_wait(st["tag"] + "_wait", st["handle"], st["nin"], len(st["groups"]), after)
        out = {}
        for land, mine, g in zip(lands, st["mine"], st["groups"]):
            land = lax.dynamic_update_slice(land, mine[None], (me, 0, 0))
            tot = sum_slots(f"sum_weight_grads_{land.shape[1]}x{land.shape[2]}", land)
            for nm, (off_, r) in zip(g, pieces_for(g)):
                out[nm] = tot[off_:off_ + r]
        return out

    scatters = []
    small_started = [None] * DEPTH
    small_len = sum(math.prod(weights[k].shape[1:]) for k in PER_LAYER_SMALL) + N_META * D
    small_rows = -(-small_len // (8 * D)) * 8
    sent = jnp.zeros((), F32)
    for l in reversed(range(DEPTH)):
        s1, s2, s3, mp, g1n, g2n = saved[l]
        w = full[l]
        dh, dg2, f2g, f2u, f2d = ffn_backward("ffn2", dh, s3, g2n + sent, w["f2_wgT"], w["f2_wuT"], w["f2_wd"], tm)
        st = start_scatter(f"scatter_l{l}_f2", dict(f2_wgT=f2g, f2_wuT=f2u, f2_wd=f2d), PART_F2)
        scatters.append((l, st))
        dh, mg = mixer_backward("mix", dh, s2, dict(mp, d=mp["d"] + st["handle"][3][0, 0]), tm)
        st = start_scatter(f"scatter_l{l}_mix", mg, PART_MIX)
        scatters.append((l, st))
        if l > 0:
            dh, dg1, f1g, f1u, f1d = ffn_backward("ffn1", dh, s1, g1n + st["handle"][3][0, 0],
                                                  w["f1_wgT"], w["f1_wuT"], w["f1_wd"], tm)
            st = start_scatter(f"scatter_l{l}_f1", dict(f1_wgT=f1g, f1_wuT=f1u, f1_wd=f1d), PART_F1)
            scatters.append((l, st))
            sent = st["handle"][3][0, 0]
        else:
            def emit(nm, arr, l=l):
                scatters.append((l, start_scatter(f"scatter_l{l}_f1_{nm}", {"f1_" + nm: arr}, ("f1_" + nm,))))

            dh, dg1, _, _, _ = ffn_backward("ffn1", dh, s1, g1n + st["handle"][3][0, 0],
                                            w["f1_wgT"], w["f1_wuT"], w["f1_wd"], tm, emit=emit)
        small["ffn1_norm"][l] = dg1[0]
        small["mix_norm"][l] = mg["g"][0]
        small["ffn2_norm"][l] = dg2[0]
        small["ssm_d"][l] = mg["d"][0]
        small["attn_sink"][l] = mg["sink"][0, :N_HEADS]
        per_dir = {k: [] for k in ("ssm_lam_re", "ssm_lam_im", "ssm_log_dt", "ssm_b_re", "ssm_b_im",
                                   "ssm_c_re", "ssm_c_im")}
        for dr in range(2):
            gbr, gbi, gcr, gci, s_re, s_im = mg["ssm"][dr]
            a_re, a_im = ssm_p[l][dr]["a_re"], ssm_p[l][dr]["a_im"]
            s_re = s_re.reshape(SGRP, SP)
            s_im = s_im.reshape(SGRP, SP)
            den = a_re * a_re + a_im * a_im
            ga_re = (s_re * a_re - s_im * a_im) / den
            ga_im = (s_re * a_im + s_im * a_re) / den
            glr, gli, gld, gbre, gbim = ssm_vjp[l][dr]((ga_re, ga_im, _unpack_diag(gbr).transpose(0, 2, 1),
                                                        _unpack_diag(gbi).transpose(0, 2, 1)))
            per_dir["ssm_lam_re"].append(glr)
            per_dir["ssm_lam_im"].append(gli)
            per_dir["ssm_log_dt"].append(gld)
            per_dir["ssm_b_re"].append(gbre)
            per_dir["ssm_b_im"].append(gbim)
            per_dir["ssm_c_re"].append(_unpack_diag(gcr))
            per_dir["ssm_c_im"].append(_unpack_diag(gci))
        for k, vlist in per_dir.items():
            small[k][l] = jnp.stack(vlist)
        vec = [small[k][l].reshape(-1) for k in PER_LAYER_SMALL]
        if l == DEPTH - 1:
            vec.append(dgf[0])
        if l == 0:
            vec.append(dh[PAD:BLK].reshape(-1))
        used = sum(v.shape[0] for v in vec)
        flat = jnp.concatenate(vec + [jnp.zeros((small_rows * D - used,), F32)]).reshape(small_rows, D)
        small_started[l] = (exchange_start(
            f"small_grads_l{l}_start", [flat, lax.empty((NDEV, small_rows, D), F32)], 1,
            lambda refs, me_i, p_i: [(refs[0], refs[1].at[me_i], 0)]), flat)
        sent = sent + small_started[l][0][3][0, 0]

    grad_x = dh[BLK:][None]

    grads = {k: [None] * DEPTH for k in PER_LAYER_SMALL}
    for l in reversed(range(DEPTH)):
        (s_sems, r_sems, arrays, _), flat = small_started[l]
        land = exchange_wait(f"small_grads_l{l}_wait", s_sems, r_sems, arrays, 1, lambda refs, gi: refs[0], dh)[1]
        land = lax.dynamic_update_slice(land, flat[None], (me, 0, 0))
        tot = sum_slots("sum_small_grads", land).reshape(-1)
        o = 0
        for k in PER_LAYER_SMALL:
            shp = weights[k].shape[1:]
            sz = math.prod(shp)
            grads[k][l] = tot[o:o + sz].reshape(shp)
            o += sz
        if l == DEPTH - 1:
            final_norm_grad = tot[o:o + D]
        if l == 0:
            dmeta_full = tot[o:o + N_META * D].reshape(N_META, D)
    grads = {k: jnp.stack(vv) for k, vv in grads.items()}
    grads["final_norm"] = final_norm_grad
    grads["meta_tokens"] = lax.dynamic_slice_in_dim(dmeta_full, me * (D // NDEV), D // NDEV, axis=1)

    own = [dict() for _ in range(DEPTH)]
    for l, st in scatters:
        own[l].update(finish_scatter(st, dh))

    def stack(fn):
        return jnp.stack([fn(own[l]) for l in range(DEPTH)])

    grads["ffn1_w_gate"] = stack(lambda d: d["f1_wgT"].T)
    grads["ffn1_w_up"] = stack(lambda d: d["f1_wuT"].T)
    grads["ffn1_w_down"] = stack(lambda d: d["f1_wd"])
    grads["w_in"] = stack(lambda d: d["winT"].T)
    grads["ssm_w_glu"] = stack(lambda d: d["wglu"])
    grads["w_branch_ssm"] = stack(lambda d: d["wbsT"].T)
    grads["w_branch_attn"] = stack(lambda d: d["wba"])
    grads["w_out"] = stack(lambda d: d["wout"])
    grads["ffn2_w_gate"] = stack(lambda d: d["f2_wgT"].T)
    grads["ffn2_w_up"] = stack(lambda d: d["f2_wuT"].T)
    grads["ffn2_w_down"] = stack(lambda d: d["f2_wd"])

    deltas, new_m, new_v = {}, {}, {}
    for nm in names:
        deltas[nm], new_m[nm], new_v[nm] = adamw("adamw_" + nm, weights[nm], grads[nm], mom_m[nm], mom_v[nm])

    return (loss, grad_x, *[grads[n] for n in names], *[deltas[n] for n in names],
            *[new_m[n] for n in names], *[new_v[n] for n in names])
```

```python
import functools
import math

import jax
import jax.numpy as jnp
from jax import lax
from jax.experimental import pallas as pl
from jax.experimental.pallas import tpu as pltpu

F32 = jnp.float32
BF16 = jnp.bfloat16

D = 1024
DFF = 2816
N_META = 16
N_HEADS = 16
N_KV = 4
HD = 64
QG = 4
WIN = 128
BLK = 128
PAD = BLK - N_META
SW = 512
SGRP = 32
SCH = 16
SP = 64
NST = SGRP * SP
EPS = 1e-6
NEG = -1e30
SCALE = HD ** -0.5
NDEV = 8
DEPTH = 4
MESH_AXES = ("x", "y", "c")
MESH = pl.DeviceIdType.MESH

ADAM_LR = 0.001
ADAM_B1 = 0.9
ADAM_B2 = 0.999
ADAM_EPS = 1e-08
ADAM_WD = 0.01
ADAM_STEP = 10

VMEM_LIMIT = 56 * 1024 * 1024


def _params(*sem):
    return pltpu.CompilerParams(dimension_semantics=sem, vmem_limit_bytes=VMEM_LIMIT)


def _nn(a, b):
    return lax.dot_general(a, b, (((1,), (0,)), ((), ())), preferred_element_type=F32)


def _nt(a, b):
    return lax.dot_general(a, b, (((1,), (1,)), ((), ())), preferred_element_type=F32)


def _tn(a, b):
    return lax.dot_general(a, b, (((0,), (0,)), ((), ())), preferred_element_type=F32)


def _sig(x):
    return 0.5 * jnp.tanh(0.5 * x) + 0.5


def _rms_fwd(h, g):
    r = lax.rsqrt(jnp.mean(h * h, axis=-1, keepdims=True) + EPS)
    hh = h * r
    return hh, r, hh * g


def _rms_bwd(hh, r, g, dn):
    dhh = dn * g
    dx = r * (dhh - hh * jnp.mean(dhh * hh, axis=-1, keepdims=True))
    return dx, jnp.sum(dn * hh, axis=0, keepdims=True)


def _row_ok(i, tm):
    rows = i * tm + lax.broadcasted_iota(jnp.int32, (tm, 1), 0)
    return rows >= PAD


def _const_spec(shape, single=False):
    nd = len(shape)
    if single:
        return pl.BlockSpec(shape, lambda *_: (0,) * nd, pipeline_mode=pl.Buffered(1))
    return pl.BlockSpec(shape, lambda *_: (0,) * nd)


def rowcall(name, body, rows, consts, outs, accs=(), *, tm):
    nrows = rows[0].shape[0]
    nt = nrows // tm
    assert nt * tm == nrows, (name, nrows, tm)
    nr, nc, no, na = len(rows), len(consts), len(outs), len(accs)
    in_specs = [pl.BlockSpec((tm, r.shape[1]), lambda i: (i, 0)) for r in rows]
    in_specs += [_const_spec(c.shape, single=True) for c in consts]
    out_shape = [jax.ShapeDtypeStruct((nrows, w), dt) for (w, dt) in outs]
    out_specs = [pl.BlockSpec((tm, w), lambda i: (i, 0)) for (w, dt) in outs]
    out_shape += [jax.ShapeDtypeStruct(s, F32) for s in accs]
    out_specs += [_const_spec(s) for s in accs]

    def kern(*refs):
        i = pl.program_id(0)
        row_vals = [r[...] for r in refs[:nr]]
        res = body(i, *row_vals, *refs[nr:nr + nc])
        out_refs = refs[nr + nc:nr + nc + no]
        acc_refs = refs[nr + nc + no:]
        for r, v in zip(out_refs, res[:no]):
            r[...] = v.astype(r.dtype)
        if na:
            @pl.when(i == 0)
            def _():
                for r in acc_refs:
                    r[...] = jnp.zeros_like(r)
            for r, v in zip(acc_refs, res[no:]):
                r[...] += v

    res = pl.pallas_call(
        kern, name=name, grid=(nt,), in_specs=in_specs, out_specs=out_specs, out_shape=out_shape,
        compiler_params=_params("arbitrary"),
    )(*rows, *consts)
    return res


def tn_matmul(name, lhs, rhs, scale=1.0):
    M, K = lhs.shape
    N = rhs.shape[1]
    assert lhs.dtype == BF16 and rhs.dtype == BF16
    nm = 6
    tmw = M // nm
    assert tmw * nm == M and tmw % 16 == 0
    tk = 1408 if (K % 1408 == 0) else K
    nk = K // tk

    def kern(a_ref, b_ref, o_ref, acc):
        m = pl.program_id(1)
        part = _tn(a_ref[...], b_ref[...])

        @pl.when(m == 0)
        def _():
            acc[...] = part

        @pl.when((m > 0) & (m < nm - 1))
        def _():
            acc[...] += part

        @pl.when(m == nm - 1)
        def _():
            o_ref[...] = ((acc[...] + part) * scale).astype(o_ref.dtype)

    return pl.pallas_call(
        kern, name=name, grid=(nk, nm),
        in_specs=[pl.BlockSpec((tmw, tk), lambda k, m: (m, k)), pl.BlockSpec((tmw, N), lambda k, m: (m, 0))],
        out_specs=pl.BlockSpec((tk, N), lambda k, m: (k, 0)),
        out_shape=jax.ShapeDtypeStruct((K, N), BF16),
        scratch_shapes=[pltpu.VMEM((tk, N), F32)],
        compiler_params=_params("arbitrary", "arbitrary"),
    )(lhs, rhs)


def _mesh_pos():
    x, y, c = lax.axis_index("x"), lax.axis_index("y"), lax.axis_index("c")
    return x, y, c


def all_gather_pieces(name, groups):
    ng = len(groups)
    packed = [g[0] for g in groups]
    pieces = [g[1] for g in groups]
    out_shape, out_map = [], []
    for gi, (p, pcs) in enumerate(groups):
        idx = []
        for (off, r) in pcs:
            idx.append(len(out_shape))
            out_shape.append(jax.ShapeDtypeStruct((NDEV * r, p.shape[1]), p.dtype))
        out_map.append(idx)
    nout = len(out_shape)

    def body(*refs):
        p_refs = refs[:ng]
        o_refs = refs[ng:ng + nout]
        send_sems, recv_sems, local_sems = refs[ng + nout:]
        x, y, c = _mesh_pos()
        me = (x, y, c)
        sibling = (x, y, 1 - c)
        chips = [(1 - x, y), (x, 1 - y), (1 - x, 1 - y)]

        def blk(px, py, pc):
            return 4 * px + 2 * py + pc

        def copies(gi, k, origin, to, from_out):
            cps = []
            for (off, r), oi in zip(pieces[gi], out_map[gi]):
                dst = o_refs[oi].at[pl.ds(origin * r, r), :]
                src = dst if from_out else p_refs[gi].at[pl.ds(off, r), :]
                cps.append(pltpu.make_async_remote_copy(
                    src_ref=src, dst_ref=dst, send_sem=send_sems.at[gi, k], recv_sem=recv_sems.at[gi, k],
                    device_id=to, device_id_type=MESH))
            return cps

        def whole(gi, k):
            return pltpu.make_async_remote_copy(
                src_ref=p_refs[gi], dst_ref=p_refs[gi], send_sem=send_sems.at[gi, k],
                recv_sem=recv_sems.at[gi, k], device_id=me, device_id_type=MESH)

        mine = []
        for gi in range(ng):
            for (off, r), oi in zip(pieces[gi], out_map[gi]):
                mine.append(pltpu.make_async_copy(
                    p_refs[gi].at[pl.ds(off, r), :], o_refs[oi].at[pl.ds(blk(*me) * r, r), :],
                    local_sems.at[gi]))
        for cp in mine:
            cp.start()
        for gi in range(ng):
            for cp in copies(gi, 0, blk(*me), sibling, False):
                cp.start()
            for j, chip in enumerate(chips):
                for cp in copies(gi, 1 + j, blk(*me), (*chip, c), False):
                    cp.start()
        for j, chip in enumerate(chips):
            for gi in range(ng):
                whole(gi, 1 + j).wait_recv()
                for cp in copies(gi, 4 + j, blk(*chip, c), sibling, True):
                    cp.start()
        for gi in range(ng):
            whole(gi, 0).wait_recv()
            for j in range(3):
                whole(gi, 4 + j).wait_recv()
        for gi in range(ng):
            for k in range(7):
                whole(gi, k).wait_send()
            pltpu.make_async_copy(p_refs[gi], p_refs[gi], local_sems.at[gi]).wait()

    any_spec = pl.BlockSpec(memory_space=pl.ANY)
    outs = pl.pallas_call(
        body, name=name, out_shape=out_shape,
        in_specs=[any_spec] * ng, out_specs=[any_spec] * nout,
        scratch_shapes=[pltpu.SemaphoreType.DMA((ng, 7)), pltpu.SemaphoreType.DMA((ng, 7)),
                        pltpu.SemaphoreType.DMA((ng,))],
    )(*packed)
    return [[outs[oi] for oi in idx] for idx in out_map]


HBM_SPEC = pl.BlockSpec(memory_space=pltpu.HBM)
SEM_SPEC = pl.BlockSpec(memory_space=pltpu.SEMAPHORE)
DATAFLOW = pltpu.SideEffectType.DATAFLOW_SIDE_EFFECTING


def _peers(x, y, c):
    return [(x, y, 1 - c), (1 - x, y, c), (x, 1 - y, c), (1 - x, 1 - y, c),
            (1 - x, y, 1 - c), (x, 1 - y, 1 - c), (1 - x, 1 - y, 1 - c)]


def exchange_start(name, arrays, ng, plan):
    n = len(arrays)
    ns = ng * 7

    def body(*refs):
        in_refs = refs[:n]
        send_sems, recv_sems = refs[n:n + ns], refs[n + ns:n + 2 * ns]
        token = refs[-1]
        x, y, c = _mesh_pos()
        me_i = 4 * x + 2 * y + c
        for k, peer in enumerate(_peers(x, y, c)):
            p_i = 4 * peer[0] + 2 * peer[1] + peer[2]
            for src, dst, gi in plan(in_refs, me_i, p_i):
                pltpu.make_async_remote_copy(
                    src_ref=src, dst_ref=dst, send_sem=send_sems[gi * 7 + k], recv_sem=recv_sems[gi * 7 + k],
                    device_id=peer, device_id_type=MESH).start()
        token[...] = jnp.zeros_like(token)

    res = pl.pallas_call(
        body, name=name,
        out_shape=(*[pltpu.SemaphoreType.DMA(())] * (2 * ns),
                   *[pltpu.HBM(a.shape, a.dtype) for a in arrays], jax.ShapeDtypeStruct((8, 128), F32)),
        in_specs=[HBM_SPEC] * n,
        out_specs=(*[SEM_SPEC] * (2 * ns), *[HBM_SPEC] * n, pl.BlockSpec(memory_space=pltpu.VMEM)),
        input_output_aliases={i: 2 * ns + i for i in range(n)},
        compiler_params=pltpu.CompilerParams(has_side_effects=DATAFLOW),
    )(*[pltpu.with_memory_space_constraint(a, pltpu.HBM) for a in arrays])
    return list(res[:ns]), list(res[ns:2 * ns]), list(res[2 * ns:2 * ns + n]), res[-1]


def exchange_wait(name, send_sems, recv_sems, arrays, ng, sized, after):
    n = len(arrays)
    ns = ng * 7

    def body(*refs):
        in_refs = refs[:n]
        s_sems, r_sems = refs[n:n + ns], refs[n + ns:n + 2 * ns]
        x, y, c = _mesh_pos()
        for gi in range(ng):
            view = sized(in_refs, gi)
            for k in range(7):
                w = pltpu.make_async_remote_copy(
                    src_ref=view, dst_ref=view, send_sem=s_sems[gi * 7 + k], recv_sem=r_sems[gi * 7 + k],
                    device_id=(x, y, c), device_id_type=MESH)
                w.wait_send()
                w.wait_recv()

    res = pl.pallas_call(
        body, name=name, out_shape=tuple(pltpu.HBM(a.shape, a.dtype) for a in arrays),
        in_specs=[HBM_SPEC] * n + [SEM_SPEC] * (2 * ns) + [pl.BlockSpec(memory_space=pl.ANY)],
        out_specs=tuple([HBM_SPEC] * n), input_output_aliases={i: i for i in range(n)},
        compiler_params=pltpu.CompilerParams(has_side_effects=DATAFLOW),
    )(*arrays, *send_sems, *recv_sems, after)
    return list(res)


def gather_layer_start(name, packed, pieces):
    ng = len(packed)
    dests = [lax.empty((NDEV * r, p.shape[1]), p.dtype) for p, pcs in zip(packed, pieces) for (_, r) in pcs]

    def plan(refs, me_i, p_i):
        out, di = [], ng
        for gi in range(ng):
            for (off, r) in pieces[gi]:
                out.append((refs[gi].at[pl.ds(off, r), :], refs[di].at[pl.ds(me_i * r, r), :], gi))
                di += 1
        return out

    return exchange_start(name, list(packed) + dests, ng, plan)


def gather_layer_wait(name, handle, ng, after):
    send_sems, recv_sems, arrays, _ = handle
    out = exchange_wait(name, send_sems, recv_sems, arrays, ng, lambda refs, gi: refs[gi], after)
    return out[ng:]


def scatter_layer_start(name, groups):
    ng = len(groups)
    flat = [a for arrs in groups for a in arrs]
    offs, lands = [], []
    for arrs in groups:
        o, off = [], 0
        for a in arrs:
            r = a.shape[0] // NDEV
            o.append((off, r))
            off += r
        offs.append(o)
        lands.append(lax.empty((NDEV, off, arrs[0].shape[1]), arrs[0].dtype))
    nin = len(flat)

    def plan(refs, me_i, p_i):
        out, ai = [], 0
        for gi in range(ng):
            for (off, r) in offs[gi]:
                out.append((refs[ai].at[pl.ds(p_i * r, r), :], refs[nin + gi].at[me_i, pl.ds(off, r), :], gi))
                ai += 1
        return out

    return exchange_start(name, flat + lands, ng, plan), nin


def scatter_layer_wait(name, handle, nin, ng, after):
    send_sems, recv_sems, arrays, _ = handle
    out = exchange_wait(name, send_sems, recv_sems, arrays, ng, lambda refs, gi: refs[nin + gi].at[0], after)
    return out[nin:]


def _pick_tile(n, cap):
    best = None
    for t in range(8, min(n, cap) + 1, 8):
        if n % t == 0:
            best = t
    return best if best is not None else n


def sum_slots(name, land):
    _, R, W = land.shape
    tr = _pick_tile(R, 512)

    def kern(l_ref, o_ref):
        acc = l_ref[0].astype(F32)
        for s in range(1, NDEV):
            acc = acc + l_ref[s].astype(F32)
        o_ref[...] = acc

    return pl.pallas_call(
        kern, name=name, grid=(R // tr,),
        in_specs=[pl.BlockSpec((NDEV, tr, W), lambda i: (0, i, 0))],
        out_specs=pl.BlockSpec((tr, W), lambda i: (i, 0)),
        out_shape=jax.ShapeDtypeStruct((R, W), F32),
        compiler_params=_params("arbitrary"),
    )(land)


def adamw(name, w, g, m, v):
    shp = w.shape
    C = shp[-1]
    R = max(1, math.prod(shp[:-1]))
    tr = _pick_tile(R, 1024)
    w2, g2, m2, v2 = (a.reshape(R, C) for a in (w, g, m, v))

    def kern(w_ref, g_ref, m_ref, v_ref, d_ref, mo_ref, vo_ref):
        gg = g_ref[...]
        mn = ADAM_B1 * m_ref[...] + (1.0 - ADAM_B1) * gg
        vn = ADAM_B2 * v_ref[...] + (1.0 - ADAM_B2) * jnp.square(gg)
        m_hat = mn / (1.0 - ADAM_B1 ** ADAM_STEP)
        v_hat = vn / (1.0 - ADAM_B2 ** ADAM_STEP)
        d_ref[...] = -ADAM_LR * (m_hat / (jnp.sqrt(v_hat) + ADAM_EPS) + ADAM_WD * w_ref[...])
        mo_ref[...] = mn
        vo_ref[...] = vn

    spec = pl.BlockSpec((tr, C), lambda i: (i, 0))
    d, mo, vo = pl.pallas_call(
        kern, name=name, grid=(R // tr,), in_specs=[spec] * 4, out_specs=[spec] * 3,
        out_shape=[jax.ShapeDtypeStruct((R, C), F32)] * 3, compiler_params=_params("arbitrary"),
    )(w2, g2, m2, v2)
    return d.reshape(shp), mo.reshape(shp), vo.reshape(shp)


def build_h0(x2, blk0):
    L0 = x2.shape[0]
    nb = L0 // BLK + 1

    def kern(x_ref, b_ref, o_ref):
        i = pl.program_id(0)

        @pl.when(i == 0)
        def _():
            o_ref[...] = b_ref[...]

        @pl.when(i > 0)
        def _():
            o_ref[...] = x_ref[...]

    return pl.pallas_call(
        kern, name="build_h0", grid=(nb,),
        in_specs=[pl.BlockSpec((BLK, D), lambda i: (jnp.maximum(i - 1, 0), 0)), _const_spec((BLK, D))],
        out_specs=pl.BlockSpec((BLK, D), lambda i: (i, 0)),
        out_shape=jax.ShapeDtypeStruct((L0 + BLK, D), F32), compiler_params=_params("arbitrary"),
    )(x2, blk0)


def final_loss(h, tgt, gf):
    LP = h.shape[0]
    nb = LP // BLK

    def kern(h_ref, t_ref, g_ref, dh_ref, loss_ref, dg_ref):
        i = pl.program_id(0)

        @pl.when(i == 0)
        def _():
            loss_ref[...] = jnp.zeros_like(loss_ref)
            dg_ref[...] = jnp.zeros_like(dg_ref)

        g = g_ref[...]
        hh, r, yv = _rms_fwd(h_ref[...], g)
        valid = (i > 0).astype(F32)
        err = (yv - t_ref[...]) * valid
        loss_ref[...] += 0.5 * jnp.sum(jnp.sum(err * err, axis=1, keepdims=True), axis=0, keepdims=True) / D
        dy = err / D
        dx, dg = _rms_bwd(hh, r, g, dy)
        dh_ref[...] = dx
        dg_ref[...] += dg

    return pl.pallas_call(
        kern, name="final_loss", grid=(nb,),
        in_specs=[pl.BlockSpec((BLK, D), lambda i: (i, 0)),
                  pl.BlockSpec((BLK, D), lambda i: (jnp.maximum(i - 1, 0), 0)), _const_spec((1, D))],
        out_specs=[pl.BlockSpec((BLK, D), lambda i: (i, 0)), _const_spec((8, 128)), _const_spec((1, D))],
        out_shape=[jax.ShapeDtypeStruct((LP, D), F32), jax.ShapeDtypeStruct((8, 128), F32),
                   jax.ShapeDtypeStruct((1, D), F32)],
        compiler_params=_params("arbitrary"),
    )(h, tgt, gf)


def _tall_tile(nrows, tm):
    t = nrows // 24
    return t if (t * 24 == nrows and t % 16 == 0 and t > tm) else tm


def ffn_forward(tag, h, g, wgT, wuT, wd, tm):
    def f1(i, hv, g_ref, wg_ref, wu_ref):
        _, _, n = _rms_fwd(hv, g_ref[...])
        nb = n.astype(BF16)
        G = _nt(nb, wg_ref[...])
        U = _nt(nb, wu_ref[...])
        A = G * _sig(G) * U
        return nb, G, U, A

    n, G, U, A = rowcall(tag + "_up", f1, [h], [g, wgT, wuT],
                         [(D, BF16), (DFF, BF16), (DFF, BF16), (DFF, BF16)], tm=tm)

    def f2(i, av, hv, wd_ref):
        return (hv + 0.5 * _nn(av, wd_ref[...]),)

    (h2,) = rowcall(tag + "_down", f2, [A, h], [wd], [(D, F32)], tm=_tall_tile(h.shape[0], tm))
    return h2, (h, n, G, U, A)


def ffn_backward(tag, dh, saved, g, wgT, wuT, wd, tm, emit=None):
    h, n, G, U, A = saved

    def b1(i, dhv, wd_ref):
        dyb = (0.5 * dhv).astype(BF16)
        return _nt(dyb, wd_ref[...]), dyb

    dA, dyb = rowcall(tag + "_bwd_act", b1, [dh], [wd], [(DFF, BF16), (D, BF16)], tm=_tall_tile(h.shape[0], tm))
    dwd = tn_matmul(tag + "_dwd", A, dyb)
    if emit is not None:
        emit("wd", dwd)

    def b2(i, dAv, Gv, Uv, hv, dhv, g_ref, wg_ref, wu_ref):
        dAf = dAv.astype(F32)
        Gf = Gv.astype(F32)
        sg = _sig(Gf)
        dG = (dAf * Uv.astype(F32) * (sg * (1.0 + Gf * (1.0 - sg)))).astype(BF16)
        dU = (dAf * (Gf * sg)).astype(BF16)
        dn = _nn(dG, wg_ref[...]) + _nn(dU, wu_ref[...])
        gv = g_ref[...]
        hh, r, _ = _rms_fwd(hv, gv)
        dx, dg = _rms_bwd(hh, r, gv, dn)
        dx = jnp.where(_row_ok(i, tm), dx, 0.0)
        return dhv + dx, dG, dU, dg

    dh2, dG, dU, dg = rowcall(tag + "_bwd_in", b2, [dA, G, U, h, dh], [g, wgT, wuT],
                              [(D, F32), (DFF, BF16), (DFF, BF16)], [(1, D)], tm=tm)
    dwgT = tn_matmul(tag + "_dwg", dG, n)
    if emit is not None:
        emit("wgT", dwgT)
    dwuT = tn_matmul(tag + "_dwu", dU, n)
    if emit is not None:
        emit("wuT", dwuT)
    return dh2, dg, dwgT, dwuT, dwd


def _alibi_slope(head):
    return float(2.0 ** (-8.0 * (head + 1) / N_HEADS))


def _att_bias(n, nb):
    qi = lax.broadcasted_iota(jnp.int32, (BLK, 4 * BLK), 0)
    cj = lax.broadcasted_iota(jnp.int32, (BLK, 4 * BLK), 1)
    jb = cj - BLK
    dist = jnp.abs(qi + BLK - jb)
    kpos = (n - 1) * BLK + jb
    band_ok = (dist <= WIN) & (kpos >= BLK) & (kpos < nb * BLK)
    is_meta = cj < BLK
    ok = (is_meta & (cj >= PAD)) | (jnp.logical_not(is_meta) & band_ok)
    distf = jnp.where(is_meta, 0, dist).astype(F32)
    maskadd = jnp.where(ok, 0.0, NEG).astype(F32)
    distf4 = jnp.concatenate([distf] * QG, axis=0)
    mask4 = jnp.concatenate([maskadd] * QG, axis=0)
    return distf4, mask4


def _group_col(vals):
    rg = lax.broadcasted_iota(jnp.int32, (QG * BLK, 1), 0) // BLK
    col = jnp.full((QG * BLK, 1), vals[QG - 1], F32)
    for gq in range(QG - 2, -1, -1):
        col = jnp.where(rg == gq, vals[gq], col)
    return col


def _stack_heads(ref_or_val, kh):
    return jnp.concatenate(
        [ref_or_val[:, (kh * QG + gq) * HD:(kh * QG + gq + 1) * HD] for gq in range(QG)], axis=0)


def _stack_keys(km, kp, kc, kn, kh):
    sl = slice(kh * HD, (kh + 1) * HD)
    return jnp.concatenate([km[:, sl], kp[:, sl], kc[:, sl], kn[:, sl]], axis=0)


LOG2E = 1.4426950408889634
LN2 = 0.6931471805599453
QSCALE = SCALE * LOG2E


def _att_update_bias(bias_ref, n, nb):
    @pl.when((n <= 2) | (n == nb - 1))
    def _():
        distf4, mask4 = _att_bias(n, nb)
        for kh in range(N_KV):
            slope_col = _group_col([_alibi_slope(kh * QG + gq) * LOG2E for gq in range(QG)])
            bias_ref[kh] = mask4 - slope_col * distf4


def _att_exp(qs, kb, bias_ref, kh, sink_ref):
    sink_col = _group_col([sink_ref[kh * QG + gq] for gq in range(QG)]) * LOG2E
    s = _nt(qs, kb) + bias_ref[kh]
    m = jnp.maximum(jnp.max(s, axis=1, keepdims=True), sink_col)
    e = jnp.exp2(s - m)
    es = jnp.exp2(sink_col - m)
    inv = 1.0 / (jnp.sum(e, axis=1, keepdims=True) + es)
    return e, es, inv


def attention_forward(tag, q, k, v, sink):
    LP = q.shape[0]
    nb = LP // BLK

    def kern(sink_ref, q_ref, km_ref, kp_ref, kc_ref, kn_ref, vm_ref, vp_ref, vc_ref, vn_ref, o_ref, bias_ref):
        n = pl.program_id(0)
        _att_update_bias(bias_ref, n, nb)
        qv = q_ref[...]
        km, kp, kc, kn = km_ref[...], kp_ref[...], kc_ref[...], kn_ref[...]
        vm, vp, vc, vn = vm_ref[...], vp_ref[...], vc_ref[...], vn_ref[...]
        for kh in range(N_KV):
            qs = _stack_heads(qv, kh)
            kb = _stack_keys(km, kp, kc, kn, kh)
            vb = _stack_keys(vm, vp, vc, vn, kh)
            e, _, inv = _att_exp(qs, kb, bias_ref, kh, sink_ref)
            o = _nn(e.astype(BF16), vb) * inv
            for gq in range(QG):
                hcol = (kh * QG + gq) * HD
                o_ref[:, hcol:hcol + HD] = o[gq * BLK:(gq + 1) * BLK].astype(o_ref.dtype)

    def kvspec(dn):
        return pl.BlockSpec((BLK, N_KV * HD), lambda n: (jnp.clip(n + dn, 0, nb - 1), 0))

    meta_spec = pl.BlockSpec((BLK, N_KV * HD), lambda n: (0, 0))
    return pl.pallas_call(
        kern, name=tag + "_att_fwd", grid=(nb,),
        in_specs=[pl.BlockSpec(memory_space=pltpu.SMEM), pl.BlockSpec((BLK, D), lambda n: (n, 0)),
                  meta_spec, kvspec(-1), kvspec(0), kvspec(1), meta_spec, kvspec(-1), kvspec(0), kvspec(1)],
        out_specs=pl.BlockSpec((BLK, D), lambda n: (n, 0)),
        out_shape=jax.ShapeDtypeStruct((LP, D), BF16),
        scratch_shapes=[pltpu.VMEM((N_KV, QG * BLK, 4 * BLK), F32)], compiler_params=_params("arbitrary"),
    )(sink, q, k, k, k, k, v, v, v, v)


def attention_backward(tag, q, k, v, do, sink):
    LP = q.shape[0]
    nb = LP // BLK
    KW = N_KV * HD

    def kern(sink_ref, q_ref, do_ref, km_ref, kp_ref, kc_ref, kn_ref, vm_ref, vp_ref, vc_ref, vn_ref,
             dq_ref, dk_ref, dv_ref, dkm_ref, dvm_ref, dsink_ref, bias_ref, rk, rv, fk, fv):
        n = pl.program_id(0)

        @pl.when(n == 0)
        def _():
            dkm_ref[...] = jnp.zeros_like(dkm_ref)
            dvm_ref[...] = jnp.zeros_like(dvm_ref)
            dsink_ref[...] = jnp.zeros_like(dsink_ref)
            rk[...] = jnp.zeros_like(rk)
            rv[...] = jnp.zeros_like(rv)

        _att_update_bias(bias_ref, n, nb)

        @pl.when(n < nb)
        def _():
            qv, dov = q_ref[...], do_ref[...]
            km, kp, kc, kn = km_ref[...], kp_ref[...], kc_ref[...], kn_ref[...]
            vm, vp, vc, vn = vm_ref[...], vp_ref[...], vc_ref[...], vn_ref[...]
            lane = lax.broadcasted_iota(jnp.int32, (8, 128), 1)
            dsink = jnp.zeros((8, 128), F32)
            for kh in range(N_KV):
                qs = _stack_heads(qv, kh)
                dos = _stack_heads(dov, kh)
                kb = _stack_keys(km, kp, kc, kn, kh)
                vb = _stack_keys(vm, vp, vc, vn, kh)
                dp = _nt(dos, vb)
                e, es, inv = _att_exp(qs, kb, bias_ref, kh, sink_ref)
                delta = inv * jnp.sum(e * dp, axis=1, keepdims=True)
                dsu = (e * (dp - delta)).astype(BF16)
                dqs = _nn(dsu, kb) * (inv * SCALE)
                dkt = _tn((qs.astype(F32) * (inv * LN2)).astype(BF16), dsu)
                dvt = _tn((dos.astype(F32) * inv).astype(BF16), e.astype(BF16))
                dsk = -(es * inv * delta)
                for gq in range(QG):
                    hcol = (kh * QG + gq) * HD
                    dq_ref[:, hcol:hcol + HD] = dqs[gq * BLK:(gq + 1) * BLK].astype(dq_ref.dtype)
                    tot = jnp.sum(dsk[gq * BLK:(gq + 1) * BLK], axis=0, keepdims=True)
                    dsink = dsink + jnp.where(lane == kh * QG + gq, tot, 0.0)
                hs = slice(kh * HD, (kh + 1) * HD)
                dkm_ref[hs, :] += dkt[:, 0:BLK]
                dvm_ref[hs, :] += dvt[:, 0:BLK]
                for ring, fin, part in ((rk, fk, dkt), (rv, fv, dvt)):
                    fin[hs, :] = ring[0, hs, :] + part[:, BLK:2 * BLK]
                    ring[0, hs, :] = ring[1, hs, :] + part[:, 2 * BLK:3 * BLK]
                    ring[1, hs, :] = part[:, 3 * BLK:4 * BLK]
            dsink_ref[...] += dsink
            dk_ref[...] = fk[...].T.astype(dk_ref.dtype)
            dv_ref[...] = fv[...].T.astype(dv_ref.dtype)

        @pl.when(n == nb)
        def _():
            dk_ref[...] = rk[0].T.astype(dk_ref.dtype)
            dv_ref[...] = rv[0].T.astype(dv_ref.dtype)

    def kvspec(dn):
        return pl.BlockSpec((BLK, KW), lambda n: (jnp.clip(jnp.minimum(n, nb - 1) + dn, 0, nb - 1), 0))

    meta_spec = pl.BlockSpec((BLK, KW), lambda n: (0, 0))
    rowspec = pl.BlockSpec((BLK, D), lambda n: (jnp.minimum(n, nb - 1), 0))
    emit_spec = pl.BlockSpec((BLK, KW), lambda n: (jnp.clip(n - 1, 1, nb - 1), 0))
    dq, dk, dv, dkm, dvm, dsink = pl.pallas_call(
        kern, name=tag + "_att_bwd", grid=(nb + 1,),
        in_specs=[pl.BlockSpec(memory_space=pltpu.SMEM), rowspec, rowspec,
                  meta_spec, kvspec(-1), kvspec(0), kvspec(1), meta_spec, kvspec(-1), kvspec(0), kvspec(1)],
        out_specs=[rowspec, emit_spec, emit_spec, _const_spec((KW, BLK)), _const_spec((KW, BLK)),
                   _const_spec((8, 128))],
        out_shape=[jax.ShapeDtypeStruct((LP, D), BF16), jax.ShapeDtypeStruct((LP, KW), BF16),
                   jax.ShapeDtypeStruct((LP, KW), BF16), jax.ShapeDtypeStruct((KW, BLK), F32),
                   jax.ShapeDtypeStruct((KW, BLK), F32), jax.ShapeDtypeStruct((8, 128), F32)],
        scratch_shapes=[pltpu.VMEM((N_KV, QG * BLK, 4 * BLK), F32), pltpu.VMEM((2, KW, BLK), F32),
                        pltpu.VMEM((2, KW, BLK), F32), pltpu.VMEM((KW, BLK), F32), pltpu.VMEM((KW, BLK), F32)],
        compiler_params=_params("arbitrary"),
    )(sink, q, do, k, k, k, k, v, v, v, v)
    dk = lax.dynamic_update_slice(dk, dkm.T.astype(BF16), (0, 0))
    dv = lax.dynamic_update_slice(dv, dvm.T.astype(BF16), (0, 0))
    return dq, dk, dv, dsink


SCAN_LANES = 1024


def _scan_tile(xr, xi, cr, ci, a8, tab, seg, reverse):
    sub = lax.broadcasted_iota(jnp.int32, (8, SCAN_LANES), 0)
    for c0 in range(0, NST, SCAN_LANES):
        cs = pl.ds(c0, SCAN_LANES)
        ar = a8[0, :, cs]
        ai = a8[1, :, cs]

        def rows(j):
            jj = (seg - 1 - j) if reverse else j
            return pl.ds(jj * 8, 8)

        def step1(j, carry):
            vr, vi = carry
            rs = rows(j)
            nr = ar * vr - ai * vi + xr[rs, cs]
            ni = ar * vi + ai * vr + xi[rs, cs]
            xr[rs, cs] = nr
            xi[rs, cs] = ni
            return nr, ni

        zero = jnp.zeros((8, SCAN_LANES), F32)
        vr, vi = zero, zero
        for j in range(seg):
            vr, vi = step1(j, (vr, vi))
        for t, s in enumerate((1, 2, 4)):
            sh = (8 - s) if reverse else s
            sr = pltpu.roll(vr, sh, 0)
            si = pltpu.roll(vi, sh, 0)
            tr = tab[2 * t, :, cs]
            ti = tab[2 * t + 1, :, cs]
            vr, vi = vr + tr * sr - ti * si, vi + tr * si + ti * sr
        pr = tab[6, :, cs]
        pi = tab[7, :, cs]
        c_r = cr[:, cs]
        c_i = ci[:, cs]
        vr, vi = vr + pr * c_r - pi * c_i, vi + pr * c_i + pi * c_r
        edge = 7 if reverse else 0
        last = 0 if reverse else 7
        sh = 7 if reverse else 1
        in_r = jnp.where(sub == edge, c_r, pltpu.roll(vr, sh, 0))
        in_i = jnp.where(sub == edge, c_i, pltpu.roll(vi, sh, 0))
        cr[:, cs] = jnp.broadcast_to(vr[last:last + 1, :], (8, SCAN_LANES))
        ci[:, cs] = jnp.broadcast_to(vi[last:last + 1, :], (8, SCAN_LANES))

        def step2(j, carry):
            dr, di = carry
            rs = rows(j)
            ndr = ar * dr - ai * di
            ndi = ar * di + ai * dr
            xr[rs, cs] += ndr
            xi[rs, cs] += ndi
            return ndr, ndi

        dr, di = in_r, in_i
        for j in range(seg):
            dr, di = step2(j, (dr, di))


ST_T = 4 * SP * 2
CH_T = 128


def _load_segmented(ref, scr, seg):
    out = []
    for ct in range(4):
        scr[ct] = ref[:, ct * CH_T:(ct + 1) * CH_T]
        out.append(jnp.concatenate([scr[ct, pl.ds(j, 8, stride=seg), :] for j in range(seg)], axis=0))
    return out


def _store_segmented(ref, scr, vals, seg):
    for ct in range(4):
        for j in range(seg):
            scr[ct, pl.ds(j, 8, stride=seg), :] = vals[ct][8 * j:8 * j + 8]
        ref[:, ct * CH_T:(ct + 1) * CH_T] = scr[ct].astype(ref.dtype)


def ssm_dir_forward(tag, u, bpr, bpi, cpr, cpi, a8, tab, reverse, tm):
    LP = u.shape[0]
    nt = LP // tm
    seg = tm // 8

    def rix(i):
        return (nt - 1 - i) if reverse else i

    def kern(u_ref, bpr_ref, bpi_ref, cpr_ref, cpi_ref, a8_ref, tab_ref, xre_ref, xim_ref, y_ref,
             xr, xi, ys, cr, ci):
        i = pl.program_id(0)

        @pl.when(i == 0)
        def _():
            cr[...] = jnp.zeros_like(cr)
            ci[...] = jnp.zeros_like(ci)

        ub = _load_segmented(u_ref, ys, seg)
        for ct in range(4):
            uc = ub[ct].astype(BF16)
            xr[:, ct * ST_T:(ct + 1) * ST_T] = _nn(uc, bpr_ref[ct * CH_T:(ct + 1) * CH_T, :])
            xi[:, ct * ST_T:(ct + 1) * ST_T] = _nn(uc, bpi_ref[ct * CH_T:(ct + 1) * CH_T, :])
        _scan_tile(xr, xi, cr, ci, a8_ref, tab_ref, seg, reverse)
        xrb = xr[...].astype(BF16)
        xib = xi[...].astype(BF16)
        xre_ref[...] = xrb
        xim_ref[...] = xib
        yv = []
        for ct in range(4):
            ss = slice(ct * ST_T, (ct + 1) * ST_T)
            yv.append(_nn(xrb[:, ss], cpr_ref[ss, :]) - _nn(xib[:, ss], cpi_ref[ss, :]))
        _store_segmented(y_ref, ys, yv, seg)

    row = lambda w: pl.BlockSpec((tm, w), lambda i: (rix(i), 0))
    return pl.pallas_call(
        kern, name=tag, grid=(nt,),
        in_specs=[row(SW), _const_spec(bpr.shape), _const_spec(bpi.shape), _const_spec(cpr.shape),
                  _const_spec(cpi.shape), _const_spec(a8.shape), _const_spec(tab.shape)],
        out_specs=[row(NST), row(NST), row(SW)],
        out_shape=[jax.ShapeDtypeStruct((LP, NST), BF16), jax.ShapeDtypeStruct((LP, NST), BF16),
                   jax.ShapeDtypeStruct((LP, SW), F32)],
        scratch_shapes=[pltpu.VMEM((tm, NST), F32), pltpu.VMEM((tm, NST), F32), pltpu.VMEM((4, tm, CH_T), F32),
                        pltpu.VMEM((8, NST), F32), pltpu.VMEM((8, NST), F32)],
        compiler_params=_params("arbitrary"),
    )(u, bpr, bpi, cpr, cpi, a8, tab)


def ssm_dir_backward(tag, dy, xre, xim, u, bpr, bpi, cpr, cpi, a8_adj, tab_adj, reverse, tm):
    LP = u.shape[0]
    nt = LP // tm
    seg = tm // 8

    def rix(i):
        return (nt - 1 - i) if reverse else i

    def kern(dy_ref, xre_ref, xim_ref, u_ref, bpr_ref, bpi_ref, cpr_ref, cpi_ref, a8_ref, tab_ref,
             du_ref, gbr_ref, gbi_ref, gcr_ref, gci_ref, sr_ref, si_ref, lr, li, gr, gi, dus, cr, ci):
        i = pl.program_id(0)

        @pl.when(i == 0)
        def _():
            cr[...] = jnp.zeros_like(cr)
            ci[...] = jnp.zeros_like(ci)
            for r in (gbr_ref, gbi_ref, gcr_ref, gci_ref, sr_ref, si_ref):
                r[...] = jnp.zeros_like(r)

        dyb = [v.astype(BF16) for v in _load_segmented(dy_ref, dus, seg)]
        ub = [v.astype(BF16) for v in _load_segmented(u_ref, dus, seg)]
        for ct in range(4):
            ss = slice(ct * ST_T, (ct + 1) * ST_T)
            dc = dyb[ct]
            g_re = _nt(dc, cpr_ref[ss, :])
            g_im = -_nt(dc, cpi_ref[ss, :])
            lr[:, ss] = g_re
            li[:, ss] = g_im
            gr[:, ss] = g_re
            gi[:, ss] = g_im
        _scan_tile(lr, li, cr, ci, a8_ref, tab_ref, seg, reverse)
        lam_r = lr[...]
        lam_i = li[...]
        wr = lam_r - gr[...]
        wi = lam_i - gi[...]
        xr = xre_ref[...].astype(F32)
        xi = xim_ref[...].astype(F32)
        sr_ref[...] += jnp.sum(wr * xr + wi * xi, axis=0, keepdims=True)
        si_ref[...] += jnp.sum(wi * xr - wr * xi, axis=0, keepdims=True)
        lrb = lam_r.astype(BF16)
        lib = lam_i.astype(BF16)
        xrb = xre_ref[...]
        xib = xim_ref[...]
        duv = []
        for ct in range(4):
            ss = slice(ct * ST_T, (ct + 1) * ST_T)
            cs = slice(ct * CH_T, (ct + 1) * CH_T)
            duv.append(_nt(lrb[:, ss], bpr_ref[cs, :]) + _nt(lib[:, ss], bpi_ref[cs, :]))
            gbr_ref[cs, :] += _tn(ub[ct], lrb[:, ss])
            gbi_ref[cs, :] += _tn(ub[ct], lib[:, ss])
            gcr_ref[cs, :] += _tn(dyb[ct], xrb[:, ss])
            gci_ref[cs, :] -= _tn(dyb[ct], xib[:, ss])
        _store_segmented(du_ref, dus, duv, seg)

    row = lambda w: pl.BlockSpec((tm, w), lambda i: (rix(i), 0))
    acc = _const_spec((SW, ST_T))
    vec = _const_spec((1, NST))
    return pl.pallas_call(
        kern, name=tag, grid=(nt,),
        in_specs=[row(SW), row(NST), row(NST), row(SW), _const_spec(bpr.shape), _const_spec(bpi.shape),
                  _const_spec(cpr.shape), _const_spec(cpi.shape), _const_spec(a8_adj.shape),
                  _const_spec(tab_adj.shape)],
        out_specs=[row(SW), acc, acc, acc, acc, vec, vec],
        out_shape=[jax.ShapeDtypeStruct((LP, SW), BF16)] + [jax.ShapeDtypeStruct((SW, ST_T), F32)] * 4
        + [jax.ShapeDtypeStruct((1, NST), F32)] * 2,
        scratch_shapes=[pltpu.VMEM((tm, NST), F32)] * 4 + [pltpu.VMEM((4, tm, CH_T), F32)]
        + [pltpu.VMEM((8, NST), F32)] * 2,
        compiler_params=_params("arbitrary"),
    )(dy, xre, xim, u, bpr, bpi, cpr, cpi, a8_adj, tab_adj)


def _ssm_disc(lam_re, lam_im, log_dt, b_re, b_im):
    dt = jnp.exp(log_dt)[:, None]
    mag = jnp.exp(lam_re * dt)
    a_re = mag * jnp.cos(lam_im * dt)
    a_im = mag * jnp.sin(lam_im * dt)
    den = lam_re * lam_re + lam_im * lam_im
    f_re = ((a_re - 1.0) * lam_re + a_im * lam_im) / den
    f_im = (a_im * lam_re - (a_re - 1.0) * lam_im) / den
    bb_re = f_re[:, :, None] * b_re - f_im[:, :, None] * b_im
    bb_im = f_re[:, :, None] * b_im + f_im[:, :, None] * b_re
    return a_re, a_im, bb_re, bb_im


def _scan_tables(lam_re, lam_im, log_dt, conj, reverse, seg):
    dt = jnp.exp(log_dt)[:, None]
    lr = (lam_re * dt).reshape(1, NST)
    li = (lam_im * dt).reshape(1, NST) * (-1.0 if conj else 1.0)
    t = jnp.arange(8, dtype=F32)[:, None]

    def power(kk):
        mag = jnp.exp(kk * lr)
        return mag * jnp.cos(kk * li), mag * jnp.sin(kk * li)

    ones = jnp.ones((8, 1), F32)
    a8 = jnp.stack(power(ones)).astype(F32)
    tabs = []
    for s in (1, 2, 4):
        mask = (t <= 7 - s) if reverse else (t >= s)
        pr, pi = power(float(s * seg) * ones)
        tabs += [jnp.where(mask, pr, 0.0), jnp.where(mask, pi, 0.0)]
    kk = ((8.0 - t) if reverse else (t + 1.0)) * float(seg)
    pr, pi = power(kk)
    tabs += [pr, pi]
    return a8, jnp.stack(tabs).astype(F32)


def _pack_b(bb):
    t = bb.transpose(0, 2, 1).reshape(4, 8, SCH, SP)
    eye = jnp.eye(8, dtype=bb.dtype)
    return jnp.einsum('tgcp,gh->tgchp', t, eye).reshape(SW, ST_T)


def _pack_c(cc):
    t = cc.transpose(0, 2, 1).reshape(4, 8, SP, SCH)
    eye = jnp.eye(8, dtype=cc.dtype)
    return jnp.einsum('tgpc,gh->tgphc', t, eye).reshape(NST, CH_T)


def _unpack_diag(acc):
    t = acc.reshape(4, 8, SCH, 8, SP)
    eye = jnp.eye(8, dtype=acc.dtype)
    return jnp.einsum('tgchp,gh->tgcp', t, eye).reshape(SGRP, SCH, SP)


def _gelu(y):
    k0 = math.sqrt(2.0 / math.pi)
    inner = k0 * (y + 0.044715 * y * y * y)
    th = jnp.tanh(inner)
    z = 0.5 * y * (1.0 + th)
    dz = 0.5 * (1.0 + th) + 0.5 * y * (1.0 - th * th) * k0 * (1.0 + 3.0 * 0.044715 * y * y)
    return z, dz


Q0, K0, V0, U0, GS0, GA0, IN_COLS = 0, 1024, 1280, 1536, 2048, 3072, 4096


def mixer_forward(tag, h, p, tm):
    g, winT, wglu, wbsT, wba, wout = p["g"], p["winT"], p["wglu"], p["wbsT"], p["wba"], p["wout"]

    def proj(i, hv, g_ref, w_ref):
        _, _, n = _rms_fwd(hv, g_ref[...])
        nb = n.astype(BF16)
        return (nb, _nt(nb, w_ref[Q0:K0, :]) * QSCALE, _nt(nb, w_ref[K0:V0, :]), _nt(nb, w_ref[V0:U0, :]),
                _nt(nb, w_ref[U0:GS0, :]), _nt(nb, w_ref[GS0:GA0, :]), _nt(nb, w_ref[GA0:IN_COLS, :]))

    n, q, k, v, u, gs, ga = rowcall(
        tag + "_proj", proj, [h], [g, winT],
        [(D, BF16), (D, BF16), (N_KV * HD, BF16), (N_KV * HD, BF16), (SW, F32), (D, BF16), (D, BF16)],
        tm=_tall_tile(h.shape[0], tm))

    ya = attention_forward(tag, q, k, v, p["sink"])

    states, ydir = [], []
    for dr in range(2):
        s = p["ssm"][dr]
        xre, xim, yd = ssm_dir_forward(f"{tag}_ssm_fwd{dr}", u, s["bpr"], s["bpi"], s["cpr"], s["cpi"],
                                       s["a8"], s["tab"], dr == 1, tm)
        states.append((xre, xim))
        ydir.append(yd)

    def merge(i, y0, y1, uv, yav, gsv, gav, hv, d_ref, wglu_ref, wbs_ref, wba_ref, wout_ref):
        ypre = y0 + y1 + d_ref[...] * uv
        z, _ = _gelu(ypre)
        zb = z.astype(BF16)
        t = _nn(zb, wglu_ref[...])
        ysb = (z * _sig(t)).astype(BF16)
        bs = _nt(ysb, wbs_ref[...])
        ba = _nn(yav, wba_ref[...])
        mg = _sig(gsv.astype(F32)) * bs + _sig(gav.astype(F32)) * ba
        mg = jnp.where(_row_ok(i, tm), mg, 0.0).astype(BF16)
        return ypre, zb, t, ysb, bs, ba, mg, hv + _nn(mg, wout_ref[...])

    ypre, zb, t, ys, bs, ba, mg, h2 = rowcall(
        tag + "_merge", merge, [ydir[0], ydir[1], u, ya, gs, ga, h], [p["d"], wglu, wbsT, wba, wout],
        [(SW, BF16), (SW, BF16), (SW, BF16), (SW, BF16), (D, BF16), (D, BF16), (D, BF16), (D, F32)], tm=tm)
    saved = dict(h=h, n=n, q=q, k=k, v=v, u=u, gs=gs, ga=ga, ya=ya, states=states, ypre=ypre, zb=zb, t=t,
                 ys=ys, bs=bs, ba=ba, mg=mg)
    return h2, saved


def mixer_backward(tag, dh, sv, p, tm):
    g, winT, wglu, wbsT, wba, wout = p["g"], p["winT"], p["wglu"], p["wbsT"], p["wba"], p["wout"]

    def y1(i, dhv, bsv, bav, gsv, gav, ypv, tv, uv, wout_ref, wbs_ref, wba_ref, d_ref, wglu_ref):
        dhb = dhv.astype(BF16)
        dmg = _nt(dhb, wout_ref[...])
        dmg = jnp.where(_row_ok(i, tm), dmg, 0.0)
        sgs = _sig(gsv.astype(F32))
        sga = _sig(gav.astype(F32))
        dbs = (dmg * sgs).astype(BF16)
        dba = (dmg * sga).astype(BF16)
        dgs = dmg * bsv.astype(F32) * sgs * (1.0 - sgs)
        dga = dmg * bav.astype(F32) * sga * (1.0 - sga)
        dys = _nn(dbs, wbs_ref[...])
        dya = _nt(dba, wba_ref[...])
        z, dz_dy = _gelu(ypv.astype(F32))
        st = _sig(tv.astype(F32))
        dt_ = dys * z * st * (1.0 - st)
        dz = dys * st + _nt(dt_.astype(BF16), wglu_ref[...])
        dyp = dz * dz_dy
        return (dbs, dba, dgs, dga, dhb, dya, dyp, dyp * d_ref[...], dt_,
                jnp.sum(dyp * uv, axis=0, keepdims=True))

    dbs, dba, dgs, dga, dhb, dya, dypb, du0, dtb, dd = rowcall(
        tag + "_bwd_merge", y1,
        [dh, sv["bs"], sv["ba"], sv["gs"], sv["ga"], sv["ypre"], sv["t"], sv["u"]],
        [wout, wbsT, wba, p["d"], wglu],
        [(D, BF16)] * 6 + [(SW, F32), (SW, BF16), (SW, BF16)], [(1, SW)], tm=tm)
    dwout = tn_matmul(tag + "_dwout", sv["mg"], dhb)
    dwbsT = tn_matmul(tag + "_dwbs", dbs, sv["ys"])
    dwba = tn_matmul(tag + "_dwba", sv["ya"], dba)
    dwglu = tn_matmul(tag + "_dwglu", sv["zb"], dtb)

    du_dirs, ssm_sums = [], []
    for dr in range(2):
        s = p["ssm"][dr]
        xre, xim = sv["states"][dr]
        res = ssm_dir_backward(f"{tag}_ssm_bwd{dr}", dypb, xre, xim, sv["u"], s["bpr"], s["bpi"], s["cpr"],
                               s["cpi"], s["a8_adj"], s["tab_adj"], dr == 0, tm)
        du_dirs.append(res[0])
        ssm_sums.append(res[1:])

    dq, dk, dv, dsink = attention_backward(tag, sv["q"], sv["k"], sv["v"], dya, p["sink"])

    def x1b(i, dqv, dkv, dvv, du0v, du1v, du2v, dgsv, dgav, hv, dhv, g_ref, w_ref):
        dub = (du0v.astype(F32) + du1v.astype(F32) + du2v.astype(F32)).astype(BF16)
        dn = (_nn(dqv, w_ref[Q0:K0, :]) + _nn(dkv, w_ref[K0:V0, :]) + _nn(dvv, w_ref[V0:U0, :])
              + _nn(dub, w_ref[U0:GS0, :]) + _nn(dgsv, w_ref[GS0:GA0, :]) + _nn(dgav, w_ref[GA0:IN_COLS, :]))
        gv = g_ref[...]
        hh, r, _ = _rms_fwd(hv, gv)
        dx, dg = _rms_bwd(hh, r, gv, dn)
        dx = jnp.where(_row_ok(i, tall), dx, 0.0)
        return dhv + dx, dub, dg

    tall = _tall_tile(dh.shape[0], tm)
    dh2, dub, dg = rowcall(tag + "_bwd_in", x1b,
                           [dq, dk, dv, du0, du_dirs[0], du_dirs[1], dgs, dga, sv["h"], dh], [g, winT],
                           [(D, F32), (SW, BF16)], [(1, D)], tm=tall)
    n = sv["n"]
    dwinT = jnp.concatenate([tn_matmul(f"{tag}_dwin{j}", piece, n)
                             for j, piece in enumerate((dq, dk, dv, dub, dgs, dga))], axis=0)
    grads = dict(g=dg, d=dd, sink=dsink, ssm=ssm_sums, winT=dwinT, wglu=dwglu, wbsT=dwbsT, wba=dwba, wout=dwout)
    return dh2, grads


W1024 = ("f1_wgT", "f1_wuT", "f1_wd", "winT", "wba", "wout", "f2_wgT", "f2_wuT", "f2_wd")
W512 = ("wglu", "wbsT")
PART_F1 = ("f1_wgT", "f1_wuT", "f1_wd")
PART_MIX = ("winT", "wba", "wout", "wglu", "wbsT")
PART_F2 = ("f2_wgT", "f2_wuT", "f2_wd")
PER_LAYER_SMALL = ("ffn1_norm", "mix_norm", "ffn2_norm", "ssm_lam_re", "ssm_lam_im", "ssm_log_dt",
                   "ssm_b_re", "ssm_b_im", "ssm_c_re", "ssm_c_im", "ssm_d", "attn_sink")
EARLY_SMALL = tuple(k for k in PER_LAYER_SMALL if k != "ffn1_norm")
SMALL = ("ffn1_norm", "mix_norm", "ffn2_norm", "final_norm", "ssm_lam_re", "ssm_lam_im", "ssm_log_dt",
         "ssm_b_re", "ssm_b_im", "ssm_c_re", "ssm_c_im", "ssm_d", "attn_sink")


def kernel(x, meta_tokens, ffn1_norm, ffn1_w_gate, ffn1_w_up, ffn1_w_down, mix_norm, w_in, ssm_lam_re, ssm_lam_im, ssm_log_dt, ssm_b_re, ssm_b_im, ssm_c_re, ssm_c_im, ssm_d, ssm_w_glu, attn_sink, w_branch_ssm, w_branch_attn, w_out, ffn2_norm, ffn2_w_gate, ffn2_w_up, ffn2_w_down, final_norm, loss_target, m_meta_tokens, m_ffn1_norm, m_ffn1_w_gate, m_ffn1_w_up, m_ffn1_w_down, m_mix_norm, m_w_in, m_ssm_lam_re, m_ssm_lam_im, m_ssm_log_dt, m_ssm_b_re, m_ssm_b_im, m_ssm_c_re, m_ssm_c_im, m_ssm_d, m_ssm_w_glu, m_attn_sink, m_w_branch_ssm, m_w_branch_attn, m_w_out, m_ffn2_norm, m_ffn2_w_gate, m_ffn2_w_up, m_ffn2_w_down, m_final_norm, v_meta_tokens, v_ffn1_norm, v_ffn1_w_gate, v_ffn1_w_up, v_ffn1_w_down, v_mix_norm, v_w_in, v_ssm_lam_re, v_ssm_lam_im, v_ssm_log_dt, v_ssm_b_re, v_ssm_b_im, v_ssm_c_re, v_ssm_c_im, v_ssm_d, v_ssm_w_glu, v_attn_sink, v_w_branch_ssm, v_w_branch_attn, v_w_out, v_ffn2_norm, v_ffn2_w_gate, v_ffn2_w_up, v_ffn2_w_down, v_final_norm):
    weights = dict(meta_tokens=meta_tokens, ffn1_norm=ffn1_norm, ffn1_w_gate=ffn1_w_gate, ffn1_w_up=ffn1_w_up, ffn1_w_down=ffn1_w_down, mix_norm=mix_norm, w_in=w_in, ssm_lam_re=ssm_lam_re, ssm_lam_im=ssm_lam_im, ssm_log_dt=ssm_log_dt, ssm_b_re=ssm_b_re, ssm_b_im=ssm_b_im, ssm_c_re=ssm_c_re, ssm_c_im=ssm_c_im, ssm_d=ssm_d, ssm_w_glu=ssm_w_glu, attn_sink=attn_sink, w_branch_ssm=w_branch_ssm, w_branch_attn=w_branch_attn, w_out=w_out, ffn2_norm=ffn2_norm, ffn2_w_gate=ffn2_w_gate, ffn2_w_up=ffn2_w_up, ffn2_w_down=ffn2_w_down, final_norm=final_norm)
    mom_m = dict(meta_tokens=m_meta_tokens, ffn1_norm=m_ffn1_norm, ffn1_w_gate=m_ffn1_w_gate, ffn1_w_up=m_ffn1_w_up, ffn1_w_down=m_ffn1_w_down, mix_norm=m_mix_norm, w_in=m_w_in, ssm_lam_re=m_ssm_lam_re, ssm_lam_im=m_ssm_lam_im, ssm_log_dt=m_ssm_log_dt, ssm_b_re=m_ssm_b_re, ssm_b_im=m_ssm_b_im, ssm_c_re=m_ssm_c_re, ssm_c_im=m_ssm_c_im, ssm_d=m_ssm_d, ssm_w_glu=m_ssm_w_glu, attn_sink=m_attn_sink, w_branch_ssm=m_w_branch_ssm, w_branch_attn=m_w_branch_attn, w_out=m_w_out, ffn2_norm=m_ffn2_norm, ffn2_w_gate=m_ffn2_w_gate, ffn2_w_up=m_ffn2_w_up, ffn2_w_down=m_ffn2_w_down, final_norm=m_final_norm)
    mom_v = dict(meta_tokens=v_meta_tokens, ffn1_norm=v_ffn1_norm, ffn1_w_gate=v_ffn1_w_gate, ffn1_w_up=v_ffn1_w_up, ffn1_w_down=v_ffn1_w_down, mix_norm=v_mix_norm, w_in=v_w_in, ssm_lam_re=v_ssm_lam_re, ssm_lam_im=v_ssm_lam_im, ssm_log_dt=v_ssm_log_dt, ssm_b_re=v_ssm_b_re, ssm_b_im=v_ssm_b_im, ssm_c_re=v_ssm_c_re, ssm_c_im=v_ssm_c_im, ssm_d=v_ssm_d, ssm_w_glu=v_ssm_w_glu, attn_sink=v_attn_sink, w_branch_ssm=v_w_branch_ssm, w_branch_attn=v_w_branch_attn, w_out=v_w_out, ffn2_norm=v_ffn2_norm, ffn2_w_gate=v_ffn2_w_gate, ffn2_w_up=v_ffn2_w_up, ffn2_w_down=v_ffn2_w_down, final_norm=v_final_norm)
    names = list(weights)

    L0 = x.shape[1]
    LP = L0 + BLK
    tm = 384 if LP % 384 == 0 else BLK
    x_i, y_i, c_i = lax.axis_index("x"), lax.axis_index("y"), lax.axis_index("c")
    me = 4 * x_i + 2 * y_i + c_i

    def canon(l):
        return dict(
            f1_wgT=ffn1_w_gate[l].T, f1_wuT=ffn1_w_up[l].T, f1_wd=ffn1_w_down[l],
            winT=w_in[l].T, wba=w_branch_attn[l], wout=w_out[l],
            f2_wgT=ffn2_w_gate[l].T, f2_wuT=ffn2_w_up[l].T, f2_wd=ffn2_w_down[l],
            wglu=ssm_w_glu[l], wbsT=w_branch_ssm[l].T)

    shards = [{nm: a.astype(BF16) for nm, a in canon(l).items()} for l in range(DEPTH)]

    def rows_of(nm):
        return shards[0][nm].shape[0]

    def width_groups(names_):
        return [g for g in ([nm for nm in names_ if nm in W1024], [nm for nm in names_ if nm in W512]) if g]

    def pieces_for(group):
        out, off = [], 0
        for nm in group:
            out.append((off, rows_of(nm)))
            off += rows_of(nm)
        return out

    def start_gather(tag, l, names_):
        groups = width_groups(names_)
        packed = [jnp.concatenate([shards[l][nm] for nm in g], axis=0) for g in groups]
        return gather_layer_start(tag, packed, [pieces_for(g) for g in groups]), groups

    def finish_gather(tag, l, started_, after):
        handle, groups = started_
        dests = gather_layer_wait(tag, handle, len(groups), after)
        out = {}
        for nm, dest in zip([nm for g in groups for nm in g], dests):
            sh = shards[l][nm]
            out[nm] = lax.dynamic_update_slice(dest, sh, (me * sh.shape[0], 0))
        return out

    g1, gm = all_gather_pieces(
        "gather_weights_first",
        [(jnp.concatenate([shards[0][nm] for nm in PART_F1], axis=0), pieces_for(PART_F1)),
         (meta_tokens, [(0, N_META)])])
    first_weights = dict(zip(PART_F1, g1))
    meta_full = gm[0].reshape(NDEV, N_META, D // NDEV).transpose(1, 0, 2).reshape(N_META, D)
    gather_started = [start_gather("gather_start_l0", 0, PART_MIX + PART_F2)]
    gather_started += [start_gather(f"gather_start_l{l}", l, W1024 + W512) for l in range(1, DEPTH)]
    started = sum(st[0][3][0, 0] for st in gather_started)
    full = [None] * DEPTH

    def disc_all(lre, lim, ldt, bre, bim):
        return _ssm_disc(lre, lim, ldt, bre, bim)

    ssm_p, ssm_vjp = [], []
    for l in range(DEPTH):
        row, vrow = [], []
        for dr in range(2):
            args = (ssm_lam_re[l, dr], ssm_lam_im[l, dr], ssm_log_dt[l, dr], ssm_b_re[l, dr], ssm_b_im[l, dr])
            (a_re, a_im, bb_re, bb_im), vjp = jax.vjp(disc_all, *args)
            a8, tab = _scan_tables(args[0], args[1], args[2], False, dr == 1, tm // 8)
            a8_adj, tab_adj = _scan_tables(args[0], args[1], args[2], True, dr == 0, tm // 8)
            row.append(dict(
                bpr=_pack_b(bb_re).astype(BF16), bpi=_pack_b(bb_im).astype(BF16),
                cpr=_pack_c(ssm_c_re[l, dr]).astype(BF16), cpi=_pack_c(ssm_c_im[l, dr]).astype(BF16),
                a8=a8, tab=tab, a8_adj=a8_adj, tab_adj=tab_adj, a_re=a_re, a_im=a_im))
            vrow.append(vjp)
        ssm_p.append(row)
        ssm_vjp.append(vrow)

    blk0 = jnp.concatenate([jnp.zeros((PAD, D), F32), meta_full.astype(F32)], axis=0)
    h = build_h0(x[0], blk0)
    saved = []
    for l in range(DEPTH):
        w = dict(first_weights) if l == 0 else finish_gather(f"gather_wait_l{l}", l, gather_started[l], h)
        full[l] = w
        g1n, g2n = ffn1_norm[l][None, :], ffn2_norm[l][None, :]
        if l == 0:
            g1n = g1n + started
        h, s1 = ffn_forward("ffn1", h, g1n, w["f1_wgT"], w["f1_wuT"], w["f1_wd"], tm)
        if l == 0:
            w.update(finish_gather("gather_wait_l0", 0, gather_started[0], h))
        mp = dict(g=mix_norm[l][None, :], winT=w["winT"], wglu=w["wglu"], wbsT=w["wbsT"], wba=w["wba"],
                  wout=w["wout"], d=ssm_d[l][None, :], sink=attn_sink[l], ssm=ssm_p[l])
        h, s2 = mixer_forward("mix", h, mp, tm)
        h, s3 = ffn_forward("ffn2", h, g2n, w["f2_wgT"], w["f2_wuT"], w["f2_wd"], tm)
        saved.append((s1, s2, s3, mp, g1n, g2n))

    dh, loss_acc, dgf = final_loss(h, loss_target[0], final_norm[None, :])
    loss = lax.psum(loss_acc[0, 0], MESH_AXES)

    small = {nm: [None] * DEPTH for nm in SMALL if nm != "final_norm"}

    def start_scatter(tag, grads_d, names_):
        groups = width_groups(names_)
        mine = [jnp.concatenate([lax.dynamic_slice_in_dim(grads_d[nm], me * rows_of(nm), rows_of(nm), axis=0)
                                 for nm in g], axis=0) for g in groups]
        handle, nin = scatter_layer_start(tag + "_start", [[grads_d[nm] for nm in g] for g in groups])
        return dict(tag=tag, handle=handle, nin=nin, groups=groups, mine=mine)

    def finish_scatter(st, after):
        lands = scatter_layer_wait(st["tag"] + "_wait", st["handle"], st["nin"], len(st["groups"]), after)
        out = {}
        for land, mine, g in zip(lands, st["mine"], st["groups"]):
            land = lax.dynamic_update_slice(land, mine[None], (me, 0, 0))
            tot = sum_slots(f"sum_weight_grads_{land.shape[1]}x{land.shape[2]}", land)
            for nm, (off_, r) in zip(g, pieces_for(g)):
                out[nm] = tot[off_:off_ + r]
        return out

    def start_small(tag, vec, rows):
        used = sum(v.shape[0] for v in vec)
        flat = jnp.concatenate(vec + [jnp.zeros((rows * D - used,), F32)]).reshape(rows, D)
        return (exchange_start(tag + "_start", [flat, lax.empty((NDEV, rows, D), F32)], 1,
                               lambda refs, me_i, p_i: [(refs[0], refs[1].at[me_i], 0)]), flat, tag)

    def finish_small(started_, after):
        (s_sems, r_sems, arrays, _), flat, tag = started_
        land = exchange_wait(tag + "_wait", s_sems, r_sems, arrays, 1, lambda refs, gi: refs[0], after)[1]
        land = lax.dynamic_update_slice(land, flat[None], (me, 0, 0))
        return sum_slots(f"sum_small_grads_{land.shape[1]}", land).reshape(-1)

    scatters = []
    small_started = [None] * DEPTH
    small_len = sum(math.prod(weights[k].shape[1:]) for k in PER_LAYER_SMALL) + N_META * D
    small_rows = -(-small_len // (8 * D)) * 8
    sent = jnp.zeros((), F32)
    for l in reversed(range(DEPTH)):
        s1, s2, s3, mp, g1n, g2n = saved[l]
        w = full[l]
        dh, dg2, f2g, f2u, f2d = ffn_backward("ffn2", dh, s3, g2n + sent, w["f2_wgT"], w["f2_wuT"], w["f2_wd"], tm)
        st = start_scatter(f"scatter_l{l}_f2", dict(f2_wgT=f2g, f2_wuT=f2u, f2_wd=f2d), PART_F2)
        scatters.append((l, st))
        dh, mg = mixer_backward("mix", dh, s2, dict(mp, d=mp["d"] + st["handle"][3][0, 0]), tm)
        st = start_scatter(f"scatter_l{l}_mix", mg, PART_MIX)
        scatters.append((l, st))
        small["mix_norm"][l] = mg["g"][0]
        small["ffn2_norm"][l] = dg2[0]
        small["ssm_d"][l] = mg["d"][0]
        small["attn_sink"][l] = mg["sink"][0, :N_HEADS]
        per_dir = {k: [] for k in ("ssm_lam_re", "ssm_lam_im", "ssm_log_dt", "ssm_b_re", "ssm_b_im",
                                   "ssm_c_re", "ssm_c_im")}
        for dr in range(2):
            gbr, gbi, gcr, gci, s_re, s_im = mg["ssm"][dr]
            a_re, a_im = ssm_p[l][dr]["a_re"], ssm_p[l][dr]["a_im"]
            s_re = s_re.reshape(SGRP, SP)
            s_im = s_im.reshape(SGRP, SP)
            den = a_re * a_re + a_im * a_im
            ga_re = (s_re * a_re - s_im * a_im) / den
            ga_im = (s_re * a_im + s_im * a_re) / den
            glr, gli, gld, gbre, gbim = ssm_vjp[l][dr]((ga_re, ga_im, _unpack_diag(gbr).transpose(0, 2, 1),
                                                        _unpack_diag(gbi).transpose(0, 2, 1)))
            per_dir["ssm_lam_re"].append(glr)
            per_dir["ssm_lam_im"].append(gli)
            per_dir["ssm_log_dt"].append(gld)
            per_dir["ssm_b_re"].append(gbre)
            per_dir["ssm_b_im"].append(gbim)
            per_dir["ssm_c_re"].append(_unpack_diag(gcr))
            per_dir["ssm_c_im"].append(_unpack_diag(gci))
        for k, vlist in per_dir.items():
            small[k][l] = jnp.stack(vlist)
        early = None
        if l == 0:
            early = start_small("small_grads_l0a", [small[k][l].reshape(-1) for k in EARLY_SMALL], small_rows)
        tok = st["handle"][3][0, 0] + (early[0][3][0, 0] if early is not None else 0.0)
        if l > 0:
            dh, dg1, f1g, f1u, f1d = ffn_backward("ffn1", dh, s1, g1n + tok, w["f1_wgT"], w["f1_wuT"], w["f1_wd"], tm)
            st = start_scatter(f"scatter_l{l}_f1", dict(f1_wgT=f1g, f1_wuT=f1u, f1_wd=f1d), PART_F1)
            scatters.append((l, st))
            sent = st["handle"][3][0, 0]
        else:
            def emit(nm, arr, l=l):
                scatters.append((l, start_scatter(f"scatter_l{l}_f1_{nm}", {"f1_" + nm: arr}, ("f1_" + nm,))))

            dh, dg1, _, _, _ = ffn_backward("ffn1", dh, s1, g1n + tok, w["f1_wgT"], w["f1_wuT"], w["f1_wd"], tm,
                                            emit=emit)
        small["ffn1_norm"][l] = dg1[0]
        if l == 0:
            small_started[l] = (early, start_small("small_grads_l0b", [dg1[0], dh[PAD:BLK].reshape(-1)], 24))
        else:
            vec = [small[k][l].reshape(-1) for k in PER_LAYER_SMALL]
            if l == DEPTH - 1:
                vec.append(dgf[0])
            small_started[l] = start_small(f"small_grads_l{l}", vec, small_rows)
            sent = sent + small_started[l][0][3][0, 0]

    grad_x = dh[BLK:][None]

    def take(tot, keys, l, dst):
        o = 0
        for k in keys:
            shp = weights[k].shape[1:]
            sz = math.prod(shp)
            dst[k][l] = tot[o:o + sz].reshape(shp)
            o += sz
        return o

    grads = {k: [None] * DEPTH for k in PER_LAYER_SMALL}
    for l in reversed(range(1, DEPTH)):
        tot = finish_small(small_started[l], dh)
        o = take(tot, PER_LAYER_SMALL, l, grads)
        if l == DEPTH - 1:
            final_norm_grad = tot[o:o + D]
    take(finish_small(small_started[0][0], dh), EARLY_SMALL, 0, grads)
    tot = finish_small(small_started[0][1], dh)
    grads["ffn1_norm"][0] = tot[0:D]
    dmeta_full = tot[D:D + N_META * D].reshape(N_META, D)
    grads = {k: jnp.stack(vv) for k, vv in grads.items()}
    grads["final_norm"] = final_norm_grad
    grads["meta_tokens"] = lax.dynamic_slice_in_dim(dmeta_full, me * (D // NDEV), D // NDEV, axis=1)

    own = [dict() for _ in range(DEPTH)]
    for l, st in scatters:
        own[l].update(finish_scatter(st, dh))

    def stack(fn):
        return jnp.stack([fn(own[l]) for l in range(DEPTH)])

    grads["ffn1_w_gate"] = stack(lambda d: d["f1_wgT"].T)
    grads["ffn1_w_up"] = stack(lambda d: d["f1_wuT"].T)
    grads["ffn1_w_down"] = stack(lambda d: d["f1_wd"])
    grads["w_in"] = stack(lambda d: d["winT"].T)
    grads["ssm_w_glu"] = stack(lambda d: d["wglu"])
    grads["w_branch_ssm"] = stack(lambda d: d["wbsT"].T)
    grads["w_branch_attn"] = stack(lambda d: d["wba"])
    grads["w_out"] = stack(lambda d: d["wout"])
    grads["ffn2_w_gate"] = stack(lambda d: d["f2_wgT"].T)
    grads["ffn2_w_up"] = stack(lambda d: d["f2_wuT"].T)
    grads["ffn2_w_down"] = stack(lambda d: d["f2_wd"])

    deltas, new_m, new_v = {}, {}, {}
    for nm in names:
        deltas[nm], new_m[nm], new_v[nm] = adamw("adamw_" + nm, weights[nm], grads[nm], mom_m[nm], mom_v[nm])

    return (loss, grad_x, *[grads[n] for n in names], *[deltas[n] for n in names],
            *[new_m[n] for n in names], *[new_v[n] for n in names])
```

```python
import functools
import math

import jax
import jax.numpy as jnp
from jax import lax
from jax.experimental import pallas as pl
from jax.experimental.pallas import tpu as pltpu

F32 = jnp.float32
BF16 = jnp.bfloat16

D = 1024
DFF = 2816
N_META = 16
N_HEADS = 16
N_KV = 4
HD = 64
QG = 4
WIN = 128
BLK = 128
PAD = BLK - N_META
SW = 512
SGRP = 32
SCH = 16
SP = 64
NST = SGRP * SP
EPS = 1e-6
NEG = -1e30
SCALE = HD ** -0.5
NDEV = 8
DEPTH = 4
MESH_AXES = ("x", "y", "c")
MESH = pl.DeviceIdType.MESH

ADAM_LR = 0.001
ADAM_B1 = 0.9
ADAM_B2 = 0.999
ADAM_EPS = 1e-08
ADAM_WD = 0.01
ADAM_STEP = 10

VMEM_LIMIT = 56 * 1024 * 1024


def _params(*sem):
    return pltpu.CompilerParams(dimension_semantics=sem, vmem_limit_bytes=VMEM_LIMIT)


def _nn(a, b):
    return lax.dot_general(a, b, (((1,), (0,)), ((), ())), preferred_element_type=F32)


def _nt(a, b):
    return lax.dot_general(a, b, (((1,), (1,)), ((), ())), preferred_element_type=F32)


def _tn(a, b):
    return lax.dot_general(a, b, (((0,), (0,)), ((), ())), preferred_element_type=F32)


def _sig(x):
    return 0.5 * jnp.tanh(0.5 * x) + 0.5


def _rms_fwd(h, g):
    r = lax.rsqrt(jnp.mean(h * h, axis=-1, keepdims=True) + EPS)
    hh = h * r
    return hh, r, hh * g


def _rms_bwd(hh, r, g, dn):
    dhh = dn * g
    dx = r * (dhh - hh * jnp.mean(dhh * hh, axis=-1, keepdims=True))
    return dx, jnp.sum(dn * hh, axis=0, keepdims=True)


def _row_ok(i, tm):
    rows = i * tm + lax.broadcasted_iota(jnp.int32, (tm, 1), 0)
    return rows >= PAD


def _const_spec(shape, single=False):
    nd = len(shape)
    if single:
        return pl.BlockSpec(shape, lambda *_: (0,) * nd, pipeline_mode=pl.Buffered(1))
    return pl.BlockSpec(shape, lambda *_: (0,) * nd)


def rowcall(name, body, rows, consts, outs, accs=(), *, tm):
    nrows = rows[0].shape[0]
    nt = nrows // tm
    assert nt * tm == nrows, (name, nrows, tm)
    nr, nc, no, na = len(rows), len(consts), len(outs), len(accs)
    in_specs = [pl.BlockSpec((tm, r.shape[1]), lambda i: (i, 0)) for r in rows]
    in_specs += [_const_spec(c.shape, single=True) for c in consts]
    out_shape = [jax.ShapeDtypeStruct((nrows, w), dt) for (w, dt) in outs]
    out_specs = [pl.BlockSpec((tm, w), lambda i: (i, 0)) for (w, dt) in outs]
    out_shape += [jax.ShapeDtypeStruct(s, F32) for s in accs]
    out_specs += [_const_spec(s) for s in accs]

    def kern(*refs):
        i = pl.program_id(0)
        row_vals = [r[...] for r in refs[:nr]]
        res = body(i, *row_vals, *refs[nr:nr + nc])
        out_refs = refs[nr + nc:nr + nc + no]
        acc_refs = refs[nr + nc + no:]
        for r, v in zip(out_refs, res[:no]):
            r[...] = v.astype(r.dtype)
        if na:
            @pl.when(i == 0)
            def _():
                for r in acc_refs:
                    r[...] = jnp.zeros_like(r)
            for r, v in zip(acc_refs, res[no:]):
                r[...] += v

    res = pl.pallas_call(
        kern, name=name, grid=(nt,), in_specs=in_specs, out_specs=out_specs, out_shape=out_shape,
        compiler_params=_params("arbitrary"),
    )(*rows, *consts)
    return res


def tn_matmul(name, lhs, rhs, scale=1.0):
    M, K = lhs.shape
    N = rhs.shape[1]
    assert lhs.dtype == BF16 and rhs.dtype == BF16
    nm = 6
    tmw = M // nm
    assert tmw * nm == M and tmw % 16 == 0
    tk = 1408 if (K % 1408 == 0) else K
    nk = K // tk

    def kern(a_ref, b_ref, o_ref, acc):
        m = pl.program_id(1)
        part = _tn(a_ref[...], b_ref[...])

        @pl.when(m == 0)
        def _():
            acc[...] = part

        @pl.when((m > 0) & (m < nm - 1))
        def _():
            acc[...] += part

        @pl.when(m == nm - 1)
        def _():
            o_ref[...] = ((acc[...] + part) * scale).astype(o_ref.dtype)

    return pl.pallas_call(
        kern, name=name, grid=(nk, nm),
        in_specs=[pl.BlockSpec((tmw, tk), lambda k, m: (m, k)), pl.BlockSpec((tmw, N), lambda k, m: (m, 0))],
        out_specs=pl.BlockSpec((tk, N), lambda k, m: (k, 0)),
        out_shape=jax.ShapeDtypeStruct((K, N), BF16),
        scratch_shapes=[pltpu.VMEM((tk, N), F32)],
        compiler_params=_params("arbitrary", "arbitrary"),
    )(lhs, rhs)


def _mesh_pos():
    x, y, c = lax.axis_index("x"), lax.axis_index("y"), lax.axis_index("c")
    return x, y, c


def all_gather_pieces(name, groups):
    ng = len(groups)
    packed = [g[0] for g in groups]
    pieces = [g[1] for g in groups]
    out_shape, out_map = [], []
    for gi, (p, pcs) in enumerate(groups):
        idx = []
        for (off, r) in pcs:
            idx.append(len(out_shape))
            out_shape.append(jax.ShapeDtypeStruct((NDEV * r, p.shape[1]), p.dtype))
        out_map.append(idx)
    nout = len(out_shape)

    def body(*refs):
        p_refs = refs[:ng]
        o_refs = refs[ng:ng + nout]
        send_sems, recv_sems, local_sems = refs[ng + nout:]
        x, y, c = _mesh_pos()
        me = (x, y, c)
        sibling = (x, y, 1 - c)
        chips = [(1 - x, y), (x, 1 - y), (1 - x, 1 - y)]

        def blk(px, py, pc):
            return 4 * px + 2 * py + pc

        def copies(gi, k, origin, to, from_out):
            cps = []
            for (off, r), oi in zip(pieces[gi], out_map[gi]):
                dst = o_refs[oi].at[pl.ds(origin * r, r), :]
                src = dst if from_out else p_refs[gi].at[pl.ds(off, r), :]
                cps.append(pltpu.make_async_remote_copy(
                    src_ref=src, dst_ref=dst, send_sem=send_sems.at[gi, k], recv_sem=recv_sems.at[gi, k],
                    device_id=to, device_id_type=MESH))
            return cps

        def whole(gi, k):
            return pltpu.make_async_remote_copy(
                src_ref=p_refs[gi], dst_ref=p_refs[gi], send_sem=send_sems.at[gi, k],
                recv_sem=recv_sems.at[gi, k], device_id=me, device_id_type=MESH)

        mine = []
        for gi in range(ng):
            for (off, r), oi in zip(pieces[gi], out_map[gi]):
                mine.append(pltpu.make_async_copy(
                    p_refs[gi].at[pl.ds(off, r), :], o_refs[oi].at[pl.ds(blk(*me) * r, r), :],
                    local_sems.at[gi]))
        for cp in mine:
            cp.start()
        for gi in range(ng):
            for cp in copies(gi, 0, blk(*me), sibling, False):
                cp.start()
            for j, chip in enumerate(chips):
                for cp in copies(gi, 1 + j, blk(*me), (*chip, c), False):
                    cp.start()
        for j, chip in enumerate(chips):
            for gi in range(ng):
                whole(gi, 1 + j).wait_recv()
                for cp in copies(gi, 4 + j, blk(*chip, c), sibling, True):
                    cp.start()
        for gi in range(ng):
            whole(gi, 0).wait_recv()
            for j in range(3):
                whole(gi, 4 + j).wait_recv()
        for gi in range(ng):
            for k in range(7):
                whole(gi, k).wait_send()
            pltpu.make_async_copy(p_refs[gi], p_refs[gi], local_sems.at[gi]).wait()

    any_spec = pl.BlockSpec(memory_space=pl.ANY)
    outs = pl.pallas_call(
        body, name=name, out_shape=out_shape,
        in_specs=[any_spec] * ng, out_specs=[any_spec] * nout,
        scratch_shapes=[pltpu.SemaphoreType.DMA((ng, 7)), pltpu.SemaphoreType.DMA((ng, 7)),
                        pltpu.SemaphoreType.DMA((ng,))],
    )(*packed)
    return [[outs[oi] for oi in idx] for idx in out_map]


HBM_SPEC = pl.BlockSpec(memory_space=pltpu.HBM)
SEM_SPEC = pl.BlockSpec(memory_space=pltpu.SEMAPHORE)
DATAFLOW = pltpu.SideEffectType.DATAFLOW_SIDE_EFFECTING


def _peers(x, y, c):
    return [(x, y, 1 - c), (1 - x, y, c), (x, 1 - y, c), (1 - x, 1 - y, c),
            (1 - x, y, 1 - c), (x, 1 - y, 1 - c), (1 - x, 1 - y, 1 - c)]


def exchange_start(name, arrays, ng, plan):
    n = len(arrays)
    ns = ng * 7

    def body(*refs):
        in_refs = refs[:n]
        send_sems, recv_sems = refs[n:n + ns], refs[n + ns:n + 2 * ns]
        token = refs[-1]
        x, y, c = _mesh_pos()
        me_i = 4 * x + 2 * y + c
        for k, peer in enumerate(_peers(x, y, c)):
            p_i = 4 * peer[0] + 2 * peer[1] + peer[2]
            for src, dst, gi in plan(in_refs, me_i, p_i):
                pltpu.make_async_remote_copy(
                    src_ref=src, dst_ref=dst, send_sem=send_sems[gi * 7 + k], recv_sem=recv_sems[gi * 7 + k],
                    device_id=peer, device_id_type=MESH).start()
        token[...] = jnp.zeros_like(token)

    res = pl.pallas_call(
        body, name=name,
        out_shape=(*[pltpu.SemaphoreType.DMA(())] * (2 * ns),
                   *[pltpu.HBM(a.shape, a.dtype) for a in arrays], jax.ShapeDtypeStruct((8, 128), F32)),
        in_specs=[HBM_SPEC] * n,
        out_specs=(*[SEM_SPEC] * (2 * ns), *[HBM_SPEC] * n, pl.BlockSpec(memory_space=pltpu.VMEM)),
        input_output_aliases={i: 2 * ns + i for i in range(n)},
        compiler_params=pltpu.CompilerParams(has_side_effects=DATAFLOW),
    )(*[pltpu.with_memory_space_constraint(a, pltpu.HBM) for a in arrays])
    return list(res[:ns]), list(res[ns:2 * ns]), list(res[2 * ns:2 * ns + n]), res[-1]


def exchange_wait(name, send_sems, recv_sems, arrays, ng, sized, after):
    n = len(arrays)
    ns = ng * 7

    def body(*refs):
        in_refs = refs[:n]
        s_sems, r_sems = refs[n:n + ns], refs[n + ns:n + 2 * ns]
        x, y, c = _mesh_pos()
        for gi in range(ng):
            view = sized(in_refs, gi)
            for k in range(7):
                w = pltpu.make_async_remote_copy(
                    src_ref=view, dst_ref=view, send_sem=s_sems[gi * 7 + k], recv_sem=r_sems[gi * 7 + k],
                    device_id=(x, y, c), device_id_type=MESH)
                w.wait_send()
                w.wait_recv()

    res = pl.pallas_call(
        body, name=name, out_shape=tuple(pltpu.HBM(a.shape, a.dtype) for a in arrays),
        in_specs=[HBM_SPEC] * n + [SEM_SPEC] * (2 * ns) + [pl.BlockSpec(memory_space=pl.ANY)],
        out_specs=tuple([HBM_SPEC] * n), input_output_aliases={i: i for i in range(n)},
        compiler_params=pltpu.CompilerParams(has_side_effects=DATAFLOW),
    )(*arrays, *send_sems, *recv_sems, after)
    return list(res)


def gather_layer_start(name, packed, pieces):
    ng = len(packed)
    dests = [lax.empty((NDEV * r, p.shape[1]), p.dtype) for p, pcs in zip(packed, pieces) for (_, r) in pcs]

    def plan(refs, me_i, p_i):
        out, di = [], ng
        for gi in range(ng):
            for (off, r) in pieces[gi]:
                out.append((refs[gi].at[pl.ds(off, r), :], refs[di].at[pl.ds(me_i * r, r), :], gi))
                di += 1
        return out

    return exchange_start(name, list(packed) + dests, ng, plan)


def gather_layer_wait(name, handle, ng, after):
    send_sems, recv_sems, arrays, _ = handle
    out = exchange_wait(name, send_sems, recv_sems, arrays, ng, lambda refs, gi: refs[gi], after)
    return out[ng:]


def scatter_layer_start(name, groups):
    ng = len(groups)
    flat = [a for arrs in groups for a in arrs]
    offs, lands = [], []
    for arrs in groups:
        o, off = [], 0
        for a in arrs:
            r = a.shape[0] // NDEV
            o.append((off, r))
            off += r
        offs.append(o)
        lands.append(lax.empty((NDEV, off, arrs[0].shape[1]), arrs[0].dtype))
    nin = len(flat)

    def plan(refs, me_i, p_i):
        out, ai = [], 0
        for gi in range(ng):
            for (off, r) in offs[gi]:
                out.append((refs[ai].at[pl.ds(p_i * r, r), :], refs[nin + gi].at[me_i, pl.ds(off, r), :], gi))
                ai += 1
        return out

    return exchange_start(name, flat + lands, ng, plan), nin


def scatter_layer_wait(name, handle, nin, ng, after):
    send_sems, recv_sems, arrays, _ = handle
    out = exchange_wait(name, send_sems, recv_sems, arrays, ng, lambda refs, gi: refs[nin + gi].at[0], after)
    return out[nin:]


def _pick_tile(n, cap):
    best = None
    for t in range(8, min(n, cap) + 1, 8):
        if n % t == 0:
            best = t
    return best if best is not None else n


def sum_slots(name, land):
    _, R, W = land.shape
    tr = _pick_tile(R, 512)

    def kern(l_ref, o_ref):
        acc = l_ref[0].astype(F32)
        for s in range(1, NDEV):
            acc = acc + l_ref[s].astype(F32)
        o_ref[...] = acc

    return pl.pallas_call(
        kern, name=name, grid=(R // tr,),
        in_specs=[pl.BlockSpec((NDEV, tr, W), lambda i: (0, i, 0))],
        out_specs=pl.BlockSpec((tr, W), lambda i: (i, 0)),
        out_shape=jax.ShapeDtypeStruct((R, W), F32),
        compiler_params=_params("arbitrary"),
    )(land)


def adamw(name, w, g, m, v):
    shp = w.shape
    C = shp[-1]
    R = max(1, math.prod(shp[:-1]))
    tr = _pick_tile(R, 1024)
    w2, g2, m2, v2 = (a.reshape(R, C) for a in (w, g, m, v))

    def kern(w_ref, g_ref, m_ref, v_ref, d_ref, mo_ref, vo_ref):
        gg = g_ref[...]
        mn = ADAM_B1 * m_ref[...] + (1.0 - ADAM_B1) * gg
        vn = ADAM_B2 * v_ref[...] + (1.0 - ADAM_B2) * jnp.square(gg)
        m_hat = mn / (1.0 - ADAM_B1 ** ADAM_STEP)
        v_hat = vn / (1.0 - ADAM_B2 ** ADAM_STEP)
        d_ref[...] = -ADAM_LR * (m_hat / (jnp.sqrt(v_hat) + ADAM_EPS) + ADAM_WD * w_ref[...])
        mo_ref[...] = mn
        vo_ref[...] = vn

    spec = pl.BlockSpec((tr, C), lambda i: (i, 0))
    d, mo, vo = pl.pallas_call(
        kern, name=name, grid=(R // tr,), in_specs=[spec] * 4, out_specs=[spec] * 3,
        out_shape=[jax.ShapeDtypeStruct((R, C), F32)] * 3, compiler_params=_params("arbitrary"),
    )(w2, g2, m2, v2)
    return d.reshape(shp), mo.reshape(shp), vo.reshape(shp)


def build_h0(x2, blk0):
    L0 = x2.shape[0]
    nb = L0 // BLK + 1

    def kern(x_ref, b_ref, o_ref):
        i = pl.program_id(0)

        @pl.when(i == 0)
        def _():
            o_ref[...] = b_ref[...]

        @pl.when(i > 0)
        def _():
            o_ref[...] = x_ref[...]

    return pl.pallas_call(
        kern, name="build_h0", grid=(nb,),
        in_specs=[pl.BlockSpec((BLK, D), lambda i: (jnp.maximum(i - 1, 0), 0)), _const_spec((BLK, D))],
        out_specs=pl.BlockSpec((BLK, D), lambda i: (i, 0)),
        out_shape=jax.ShapeDtypeStruct((L0 + BLK, D), F32), compiler_params=_params("arbitrary"),
    )(x2, blk0)


def final_loss(h, tgt, gf):
    LP = h.shape[0]
    nb = LP // BLK

    def kern(h_ref, t_ref, g_ref, dh_ref, loss_ref, dg_ref):
        i = pl.program_id(0)

        @pl.when(i == 0)
        def _():
            loss_ref[...] = jnp.zeros_like(loss_ref)
            dg_ref[...] = jnp.zeros_like(dg_ref)

        g = g_ref[...]
        hh, r, yv = _rms_fwd(h_ref[...], g)
        valid = (i > 0).astype(F32)
        err = (yv - t_ref[...]) * valid
        loss_ref[...] += 0.5 * jnp.sum(jnp.sum(err * err, axis=1, keepdims=True), axis=0, keepdims=True) / D
        dy = err / D
        dx, dg = _rms_bwd(hh, r, g, dy)
        dh_ref[...] = dx
        dg_ref[...] += dg

    return pl.pallas_call(
        kern, name="final_loss", grid=(nb,),
        in_specs=[pl.BlockSpec((BLK, D), lambda i: (i, 0)),
                  pl.BlockSpec((BLK, D), lambda i: (jnp.maximum(i - 1, 0), 0)), _const_spec((1, D))],
        out_specs=[pl.BlockSpec((BLK, D), lambda i: (i, 0)), _const_spec((8, 128)), _const_spec((1, D))],
        out_shape=[jax.ShapeDtypeStruct((LP, D), F32), jax.ShapeDtypeStruct((8, 128), F32),
                   jax.ShapeDtypeStruct((1, D), F32)],
        compiler_params=_params("arbitrary"),
    )(h, tgt, gf)


def _tall_tile(nrows, tm):
    t = nrows // 24
    return t if (t * 24 == nrows and t % 16 == 0 and t > tm) else tm


def ffn_forward(tag, h, g, wgT, wuT, wd, tm):
    def f1(i, hv, g_ref, wg_ref, wu_ref):
        _, _, n = _rms_fwd(hv, g_ref[...])
        nb = n.astype(BF16)
        G = _nt(nb, wg_ref[...])
        U = _nt(nb, wu_ref[...])
        A = G * _sig(G) * U
        return nb, G, U, A

    n, G, U, A = rowcall(tag + "_up", f1, [h], [g, wgT, wuT],
                         [(D, BF16), (DFF, BF16), (DFF, BF16), (DFF, BF16)], tm=tm)

    def f2(i, av, hv, wd_ref):
        return (hv + 0.5 * _nn(av, wd_ref[...]),)

    (h2,) = rowcall(tag + "_down", f2, [A, h], [wd], [(D, F32)], tm=_tall_tile(h.shape[0], tm))
    return h2, (h, n, G, U, A)


def _bwd_act_ring(name, dh, wd, tm):
    M = dh.shape[0]
    nt = M // tm
    assert nt * tm == M

    def kern(dh_hbm, wd_ref, da_ref, dyb_ref, buf, sems):
        i = pl.program_id(0)

        def copy(step, slot):
            rows = pl.ds(pl.multiple_of(step * tm, 8), tm)
            return pltpu.make_async_copy(dh_hbm.at[rows, :], buf.at[slot], sems.at[slot])

        @pl.when(i == 0)
        def _():
            copy(0, 0).start()
            if nt > 1:
                copy(1, 1).start()

        @pl.when(i + 2 < nt)
        def _():
            copy(i + 2, (i + 2) % 3).start()

        slot = i % 3
        copy(i, slot).wait()
        dyb = (0.5 * buf[slot]).astype(BF16)
        da_ref[...] = _nt(dyb, wd_ref[...]).astype(da_ref.dtype)
        dyb_ref[...] = dyb

    return pl.pallas_call(
        kern, name=name, grid=(nt,),
        in_specs=[pl.BlockSpec(memory_space=pl.ANY), _const_spec(wd.shape, single=True)],
        out_specs=[pl.BlockSpec((tm, DFF), lambda i: (i, 0)), pl.BlockSpec((tm, D), lambda i: (i, 0))],
        out_shape=[jax.ShapeDtypeStruct((M, DFF), BF16), jax.ShapeDtypeStruct((M, D), BF16)],
        scratch_shapes=[pltpu.VMEM((3, tm, D), F32), pltpu.SemaphoreType.DMA((3,))],
        compiler_params=_params("arbitrary"),
    )(dh, wd)


def ffn_backward(tag, dh, saved, g, wgT, wuT, wd, tm, emit=None):
    h, n, G, U, A = saved

    def b1(i, dhv, wd_ref):
        dyb = (0.5 * dhv).astype(BF16)
        return _nt(dyb, wd_ref[...]), dyb

    dA, dyb = _bwd_act_ring(tag + "_bwd_act", dh, wd, _tall_tile(h.shape[0], tm))
    dwd = tn_matmul(tag + "_dwd", A, dyb)
    if emit is not None:
        emit("wd", dwd)

    def b2(i, dAv, Gv, Uv, hv, dhv, g_ref, wg_ref, wu_ref):
        dAf = dAv.astype(F32)
        Gf = Gv.astype(F32)
        sg = _sig(Gf)
        dG = (dAf * Uv.astype(F32) * (sg * (1.0 + Gf * (1.0 - sg)))).astype(BF16)
        dU = (dAf * (Gf * sg)).astype(BF16)
        dn = _nn(dG, wg_ref[...]) + _nn(dU, wu_ref[...])
        gv = g_ref[...]
        hh, r, _ = _rms_fwd(hv, gv)
        dx, dg = _rms_bwd(hh, r, gv, dn)
        dx = jnp.where(_row_ok(i, tm), dx, 0.0)
        return dhv + dx, dG, dU, dg

    dh2, dG, dU, dg = rowcall(tag + "_bwd_in", b2, [dA, G, U, h, dh], [g, wgT, wuT],
                              [(D, F32), (DFF, BF16), (DFF, BF16)], [(1, D)], tm=tm)
    dwgT = tn_matmul(tag + "_dwg", dG, n)
    if emit is not None:
        emit("wgT", dwgT)
    dwuT = tn_matmul(tag + "_dwu", dU, n)
    if emit is not None:
        emit("wuT", dwuT)
    return dh2, dg, dwgT, dwuT, dwd


def _alibi_slope(head):
    return float(2.0 ** (-8.0 * (head + 1) / N_HEADS))


def _att_bias(n, nb):
    qi = lax.broadcasted_iota(jnp.int32, (BLK, 4 * BLK), 0)
    cj = lax.broadcasted_iota(jnp.int32, (BLK, 4 * BLK), 1)
    jb = cj - BLK
    dist = jnp.abs(qi + BLK - jb)
    kpos = (n - 1) * BLK + jb
    band_ok = (dist <= WIN) & (kpos >= BLK) & (kpos < nb * BLK)
    is_meta = cj < BLK
    ok = (is_meta & (cj >= PAD)) | (jnp.logical_not(is_meta) & band_ok)
    distf = jnp.where(is_meta, 0, dist).astype(F32)
    maskadd = jnp.where(ok, 0.0, NEG).astype(F32)
    distf4 = jnp.concatenate([distf] * QG, axis=0)
    mask4 = jnp.concatenate([maskadd] * QG, axis=0)
    return distf4, mask4


def _group_col(vals):
    rg = lax.broadcasted_iota(jnp.int32, (QG * BLK, 1), 0) // BLK
    col = jnp.full((QG * BLK, 1), vals[QG - 1], F32)
    for gq in range(QG - 2, -1, -1):
        col = jnp.where(rg == gq, vals[gq], col)
    return col


def _stack_heads(ref_or_val, kh):
    return jnp.concatenate(
        [ref_or_val[:, (kh * QG + gq) * HD:(kh * QG + gq + 1) * HD] for gq in range(QG)], axis=0)


def _stack_keys(km, kp, kc, kn, kh):
    sl = slice(kh * HD, (kh + 1) * HD)
    return jnp.concatenate([km[:, sl], kp[:, sl], kc[:, sl], kn[:, sl]], axis=0)


LOG2E = 1.4426950408889634
LN2 = 0.6931471805599453
QSCALE = SCALE * LOG2E


def _att_update_bias(bias_ref, n, nb):
    @pl.when((n <= 2) | (n == nb - 1))
    def _():
        distf4, mask4 = _att_bias(n, nb)
        for kh in range(N_KV):
            slope_col = _group_col([_alibi_slope(kh * QG + gq) * LOG2E for gq in range(QG)])
            bias_ref[kh] = mask4 - slope_col * distf4


def _att_exp(qs, kb, bias_ref, kh, sink_ref):
    sink_col = _group_col([sink_ref[kh * QG + gq] for gq in range(QG)]) * LOG2E
    s = _nt(qs, kb) + bias_ref[kh]
    m = jnp.maximum(jnp.max(s, axis=1, keepdims=True), sink_col)
    e = jnp.exp2(s - m)
    es = jnp.exp2(sink_col - m)
    inv = 1.0 / (jnp.sum(e, axis=1, keepdims=True) + es)
    return e, es, inv


def attention_forward(tag, q, k, v, sink):
    LP = q.shape[0]
    nb = LP // BLK

    def kern(sink_ref, q_ref, km_ref, kp_ref, kc_ref, kn_ref, vm_ref, vp_ref, vc_ref, vn_ref, o_ref, bias_ref):
        n = pl.program_id(0)
        _att_update_bias(bias_ref, n, nb)
        qv = q_ref[...]
        km, kp, kc, kn = km_ref[...], kp_ref[...], kc_ref[...], kn_ref[...]
        vm, vp, vc, vn = vm_ref[...], vp_ref[...], vc_ref[...], vn_ref[...]
        for kh in range(N_KV):
            qs = _stack_heads(qv, kh)
            kb = _stack_keys(km, kp, kc, kn, kh)
            vb = _stack_keys(vm, vp, vc, vn, kh)
            e, _, inv = _att_exp(qs, kb, bias_ref, kh, sink_ref)
            o = _nn(e.astype(BF16), vb) * inv
            for gq in range(QG):
                hcol = (kh * QG + gq) * HD
                o_ref[:, hcol:hcol + HD] = o[gq * BLK:(gq + 1) * BLK].astype(o_ref.dtype)

    def kvspec(dn):
        return pl.BlockSpec((BLK, N_KV * HD), lambda n: (jnp.clip(n + dn, 0, nb - 1), 0))

    meta_spec = pl.BlockSpec((BLK, N_KV * HD), lambda n: (0, 0))
    return pl.pallas_call(
        kern, name=tag + "_att_fwd", grid=(nb,),
        in_specs=[pl.BlockSpec(memory_space=pltpu.SMEM), pl.BlockSpec((BLK, D), lambda n: (n, 0)),
                  meta_spec, kvspec(-1), kvspec(0), kvspec(1), meta_spec, kvspec(-1), kvspec(0), kvspec(1)],
        out_specs=pl.BlockSpec((BLK, D), lambda n: (n, 0)),
        out_shape=jax.ShapeDtypeStruct((LP, D), BF16),
        scratch_shapes=[pltpu.VMEM((N_KV, QG * BLK, 4 * BLK), F32)], compiler_params=_params("arbitrary"),
    )(sink, q, k, k, k, k, v, v, v, v)


def attention_backward(tag, q, k, v, do, sink):
    LP = q.shape[0]
    nb = LP // BLK
    KW = N_KV * HD

    def kern(sink_ref, q_ref, do_ref, km_ref, kp_ref, kc_ref, kn_ref, vm_ref, vp_ref, vc_ref, vn_ref,
             dq_ref, dk_ref, dv_ref, dkm_ref, dvm_ref, dsink_ref, bias_ref, rk, rv, fk, fv):
        n = pl.program_id(0)

        @pl.when(n == 0)
        def _():
            dkm_ref[...] = jnp.zeros_like(dkm_ref)
            dvm_ref[...] = jnp.zeros_like(dvm_ref)
            dsink_ref[...] = jnp.zeros_like(dsink_ref)
            rk[...] = jnp.zeros_like(rk)
            rv[...] = jnp.zeros_like(rv)

        _att_update_bias(bias_ref, n, nb)

        @pl.when(n < nb)
        def _():
            qv, dov = q_ref[...], do_ref[...]
            km, kp, kc, kn = km_ref[...], kp_ref[...], kc_ref[...], kn_ref[...]
            vm, vp, vc, vn = vm_ref[...], vp_ref[...], vc_ref[...], vn_ref[...]
            lane = lax.broadcasted_iota(jnp.int32, (8, 128), 1)
            dsink = jnp.zeros((8, 128), F32)
            for kh in range(N_KV):
                qs = _stack_heads(qv, kh)
                dos = _stack_heads(dov, kh)
                kb = _stack_keys(km, kp, kc, kn, kh)
                vb = _stack_keys(vm, vp, vc, vn, kh)
                dp = _nt(dos, vb)
                e, es, inv = _att_exp(qs, kb, bias_ref, kh, sink_ref)
                delta = inv * jnp.sum(e * dp, axis=1, keepdims=True)
                dsu = (e * (dp - delta)).astype(BF16)
                dqs = _nn(dsu, kb) * (inv * SCALE)
                dkt = _tn((qs.astype(F32) * (inv * LN2)).astype(BF16), dsu)
                dvt = _tn((dos.astype(F32) * inv).astype(BF16), e.astype(BF16))
                dsk = -(es * inv * delta)
                for gq in range(QG):
                    hcol = (kh * QG + gq) * HD
                    dq_ref[:, hcol:hcol + HD] = dqs[gq * BLK:(gq + 1) * BLK].astype(dq_ref.dtype)
                    tot = jnp.sum(dsk[gq * BLK:(gq + 1) * BLK], axis=0, keepdims=True)
                    dsink = dsink + jnp.where(lane == kh * QG + gq, tot, 0.0)
                hs = slice(kh * HD, (kh + 1) * HD)
                dkm_ref[hs, :] += dkt[:, 0:BLK]
                dvm_ref[hs, :] += dvt[:, 0:BLK]
                for ring, fin, part in ((rk, fk, dkt), (rv, fv, dvt)):
                    fin[hs, :] = ring[0, hs, :] + part[:, BLK:2 * BLK]
                    ring[0, hs, :] = ring[1, hs, :] + part[:, 2 * BLK:3 * BLK]
                    ring[1, hs, :] = part[:, 3 * BLK:4 * BLK]
            dsink_ref[...] += dsink
            dk_ref[...] = fk[...].T.astype(dk_ref.dtype)
            dv_ref[...] = fv[...].T.astype(dv_ref.dtype)

        @pl.when(n == nb)
        def _():
            dk_ref[...] = rk[0].T.astype(dk_ref.dtype)
            dv_ref[...] = rv[0].T.astype(dv_ref.dtype)

    def kvspec(dn):
        return pl.BlockSpec((BLK, KW), lambda n: (jnp.clip(jnp.minimum(n, nb - 1) + dn, 0, nb - 1), 0))

    meta_spec = pl.BlockSpec((BLK, KW), lambda n: (0, 0))
    rowspec = pl.BlockSpec((BLK, D), lambda n: (jnp.minimum(n, nb - 1), 0))
    emit_spec = pl.BlockSpec((BLK, KW), lambda n: (jnp.clip(n - 1, 1, nb - 1), 0))
    dq, dk, dv, dkm, dvm, dsink = pl.pallas_call(
        kern, name=tag + "_att_bwd", grid=(nb + 1,),
        in_specs=[pl.BlockSpec(memory_space=pltpu.SMEM), rowspec, rowspec,
                  meta_spec, kvspec(-1), kvspec(0), kvspec(1), meta_spec, kvspec(-1), kvspec(0), kvspec(1)],
        out_specs=[rowspec, emit_spec, emit_spec, _const_spec((KW, BLK)), _const_spec((KW, BLK)),
                   _const_spec((8, 128))],
        out_shape=[jax.ShapeDtypeStruct((LP, D), BF16), jax.ShapeDtypeStruct((LP, KW), BF16),
                   jax.ShapeDtypeStruct((LP, KW), BF16), jax.ShapeDtypeStruct((KW, BLK), F32),
                   jax.ShapeDtypeStruct((KW, BLK), F32), jax.ShapeDtypeStruct((8, 128), F32)],
        scratch_shapes=[pltpu.VMEM((N_KV, QG * BLK, 4 * BLK), F32), pltpu.VMEM((2, KW, BLK), F32),
                        pltpu.VMEM((2, KW, BLK), F32), pltpu.VMEM((KW, BLK), F32), pltpu.VMEM((KW, BLK), F32)],
        compiler_params=_params("arbitrary"),
    )(sink, q, do, k, k, k, k, v, v, v, v)
    dk = lax.dynamic_update_slice(dk, dkm.T.astype(BF16), (0, 0))
    dv = lax.dynamic_update_slice(dv, dvm.T.astype(BF16), (0, 0))
    return dq, dk, dv, dsink


SCAN_LANES = 1024


def _scan_tile(xr, xi, cr, ci, a8, tab, seg, reverse):
    sub = lax.broadcasted_iota(jnp.int32, (8, SCAN_LANES), 0)
    for c0 in range(0, NST, SCAN_LANES):
        cs = pl.ds(c0, SCAN_LANES)
        ar = a8[0, :, cs]
        ai = a8[1, :, cs]

        def rows(j):
            jj = (seg - 1 - j) if reverse else j
            return pl.ds(jj * 8, 8)

        def step1(j, carry):
            vr, vi = carry
            rs = rows(j)
            nr = ar * vr - ai * vi + xr[rs, cs]
            ni = ar * vi + ai * vr + xi[rs, cs]
            xr[rs, cs] = nr
            xi[rs, cs] = ni
            return nr, ni

        zero = jnp.zeros((8, SCAN_LANES), F32)
        vr, vi = zero, zero
        for j in range(seg):
            vr, vi = step1(j, (vr, vi))
        for t, s in enumerate((1, 2, 4)):
            sh = (8 - s) if reverse else s
            sr = pltpu.roll(vr, sh, 0)
            si = pltpu.roll(vi, sh, 0)
            tr = tab[2 * t, :, cs]
            ti = tab[2 * t + 1, :, cs]
            vr, vi = vr + tr * sr - ti * si, vi + tr * si + ti * sr
        pr = tab[6, :, cs]
        pi = tab[7, :, cs]
        c_r = cr[:, cs]
        c_i = ci[:, cs]
        vr, vi = vr + pr * c_r - pi * c_i, vi + pr * c_i + pi * c_r
        edge = 7 if reverse else 0
        last = 0 if reverse else 7
        sh = 7 if reverse else 1
        in_r = jnp.where(sub == edge, c_r, pltpu.roll(vr, sh, 0))
        in_i = jnp.where(sub == edge, c_i, pltpu.roll(vi, sh, 0))
        cr[:, cs] = jnp.broadcast_to(vr[last:last + 1, :], (8, SCAN_LANES))
        ci[:, cs] = jnp.broadcast_to(vi[last:last + 1, :], (8, SCAN_LANES))

        def step2(j, carry):
            dr, di = carry
            rs = rows(j)
            ndr = ar * dr - ai * di
            ndi = ar * di + ai * dr
            xr[rs, cs] += ndr
            xi[rs, cs] += ndi
            return ndr, ndi

        dr, di = in_r, in_i
        for j in range(seg):
            dr, di = step2(j, (dr, di))


ST_T = 4 * SP * 2
CH_T = 128


def _load_segmented(ref, scr, seg):
    out = []
    for ct in range(4):
        scr[ct] = ref[:, ct * CH_T:(ct + 1) * CH_T]
        out.append(jnp.concatenate([scr[ct, pl.ds(j, 8, stride=seg), :] for j in range(seg)], axis=0))
    return out


def _store_segmented(ref, scr, vals, seg):
    for ct in range(4):
        for j in range(seg):
            scr[ct, pl.ds(j, 8, stride=seg), :] = vals[ct][8 * j:8 * j + 8]
        ref[:, ct * CH_T:(ct + 1) * CH_T] = scr[ct].astype(ref.dtype)


def ssm_dir_forward(tag, u, bpr, bpi, cpr, cpi, a8, tab, reverse, tm):
    LP = u.shape[0]
    nt = LP // tm
    seg = tm // 8

    def rix(i):
        return (nt - 1 - i) if reverse else i

    def kern(u_ref, bpr_ref, bpi_ref, cpr_ref, cpi_ref, a8_ref, tab_ref, xre_ref, xim_ref, y_ref,
             xr, xi, ys, cr, ci):
        i = pl.program_id(0)

        @pl.when(i == 0)
        def _():
            cr[...] = jnp.zeros_like(cr)
            ci[...] = jnp.zeros_like(ci)

        ub = _load_segmented(u_ref, ys, seg)
        for ct in range(4):
            uc = ub[ct].astype(BF16)
            xr[:, ct * ST_T:(ct + 1) * ST_T] = _nn(uc, bpr_ref[ct * CH_T:(ct + 1) * CH_T, :])
            xi[:, ct * ST_T:(ct + 1) * ST_T] = _nn(uc, bpi_ref[ct * CH_T:(ct + 1) * CH_T, :])
        _scan_tile(xr, xi, cr, ci, a8_ref, tab_ref, seg, reverse)
        xrb = xr[...].astype(BF16)
        xib = xi[...].astype(BF16)
        xre_ref[...] = xrb
        xim_ref[...] = xib
        yv = []
        for ct in range(4):
            ss = slice(ct * ST_T, (ct + 1) * ST_T)
            yv.append(_nn(xrb[:, ss], cpr_ref[ss, :]) - _nn(xib[:, ss], cpi_ref[ss, :]))
        _store_segmented(y_ref, ys, yv, seg)

    row = lambda w: pl.BlockSpec((tm, w), lambda i: (rix(i), 0))
    return pl.pallas_call(
        kern, name=tag, grid=(nt,),
        in_specs=[row(SW), _const_spec(bpr.shape), _const_spec(bpi.shape), _const_spec(cpr.shape),
                  _const_spec(cpi.shape), _const_spec(a8.shape), _const_spec(tab.shape)],
        out_specs=[row(NST), row(NST), row(SW)],
        out_shape=[jax.ShapeDtypeStruct((LP, NST), BF16), jax.ShapeDtypeStruct((LP, NST), BF16),
                   jax.ShapeDtypeStruct((LP, SW), F32)],
        scratch_shapes=[pltpu.VMEM((tm, NST), F32), pltpu.VMEM((tm, NST), F32), pltpu.VMEM((4, tm, CH_T), F32),
                        pltpu.VMEM((8, NST), F32), pltpu.VMEM((8, NST), F32)],
        compiler_params=_params("arbitrary"),
    )(u, bpr, bpi, cpr, cpi, a8, tab)


def ssm_dir_backward(tag, dy, xre, xim, u, bpr, bpi, cpr, cpi, a8_adj, tab_adj, reverse, tm):
    LP = u.shape[0]
    nt = LP // tm
    seg = tm // 8

    def rix(i):
        return (nt - 1 - i) if reverse else i

    def kern(dy_ref, xre_ref, xim_ref, u_ref, bpr_ref, bpi_ref, cpr_ref, cpi_ref, a8_ref, tab_ref,
             du_ref, gbr_ref, gbi_ref, gcr_ref, gci_ref, sr_ref, si_ref, lr, li, gr, gi, dus, cr, ci):
        i = pl.program_id(0)

        @pl.when(i == 0)
        def _():
            cr[...] = jnp.zeros_like(cr)
            ci[...] = jnp.zeros_like(ci)
            for r in (gbr_ref, gbi_ref, gcr_ref, gci_ref, sr_ref, si_ref):
                r[...] = jnp.zeros_like(r)

        dyb = [v.astype(BF16) for v in _load_segmented(dy_ref, dus, seg)]
        ub = [v.astype(BF16) for v in _load_segmented(u_ref, dus, seg)]
        for ct in range(4):
            ss = slice(ct * ST_T, (ct + 1) * ST_T)
            dc = dyb[ct]
            g_re = _nt(dc, cpr_ref[ss, :])
            g_im = -_nt(dc, cpi_ref[ss, :])
            lr[:, ss] = g_re
            li[:, ss] = g_im
            gr[:, ss] = g_re
            gi[:, ss] = g_im
        _scan_tile(lr, li, cr, ci, a8_ref, tab_ref, seg, reverse)
        lam_r = lr[...]
        lam_i = li[...]
        wr = lam_r - gr[...]
        wi = lam_i - gi[...]
        xr = xre_ref[...].astype(F32)
        xi = xim_ref[...].astype(F32)
        sr_ref[...] += jnp.sum(wr * xr + wi * xi, axis=0, keepdims=True)
        si_ref[...] += jnp.sum(wi * xr - wr * xi, axis=0, keepdims=True)
        lrb = lam_r.astype(BF16)
        lib = lam_i.astype(BF16)
        xrb = xre_ref[...]
        xib = xim_ref[...]
        duv = []
        for ct in range(4):
            ss = slice(ct * ST_T, (ct + 1) * ST_T)
            cs = slice(ct * CH_T, (ct + 1) * CH_T)
            duv.append(_nt(lrb[:, ss], bpr_ref[cs, :]) + _nt(lib[:, ss], bpi_ref[cs, :]))
            gbr_ref[cs, :] += _tn(ub[ct], lrb[:, ss])
            gbi_ref[cs, :] += _tn(ub[ct], lib[:, ss])
            gcr_ref[cs, :] += _tn(dyb[ct], xrb[:, ss])
            gci_ref[cs, :] -= _tn(dyb[ct], xib[:, ss])
        _store_segmented(du_ref, dus, duv, seg)

    row = lambda w: pl.BlockSpec((tm, w), lambda i: (rix(i), 0))
    acc = _const_spec((SW, ST_T))
    vec = _const_spec((1, NST))
    return pl.pallas_call(
        kern, name=tag, grid=(nt,),
        in_specs=[row(SW), row(NST), row(NST), row(SW), _const_spec(bpr.shape), _const_spec(bpi.shape),
                  _const_spec(cpr.shape), _const_spec(cpi.shape), _const_spec(a8_adj.shape),
                  _const_spec(tab_adj.shape)],
        out_specs=[row(SW), acc, acc, acc, acc, vec, vec],
        out_shape=[jax.ShapeDtypeStruct((LP, SW), BF16)] + [jax.ShapeDtypeStruct((SW, ST_T), F32)] * 4
        + [jax.ShapeDtypeStruct((1, NST), F32)] * 2,
        scratch_shapes=[pltpu.VMEM((tm, NST), F32)] * 4 + [pltpu.VMEM((4, tm, CH_T), F32)]
        + [pltpu.VMEM((8, NST), F32)] * 2,
        compiler_params=_params("arbitrary"),
    )(dy, xre, xim, u, bpr, bpi, cpr, cpi, a8_adj, tab_adj)


def _ssm_disc(lam_re, lam_im, log_dt, b_re, b_im):
    dt = jnp.exp(log_dt)[:, None]
    mag = jnp.exp(lam_re * dt)
    a_re = mag * jnp.cos(lam_im * dt)
    a_im = mag * jnp.sin(lam_im * dt)
    den = lam_re * lam_re + lam_im * lam_im
    f_re = ((a_re - 1.0) * lam_re + a_im * lam_im) / den
    f_im = (a_im * lam_re - (a_re - 1.0) * lam_im) / den
    bb_re = f_re[:, :, None] * b_re - f_im[:, :, None] * b_im
    bb_im = f_re[:, :, None] * b_im + f_im[:, :, None] * b_re
    return a_re, a_im, bb_re, bb_im


def _scan_tables(lam_re, lam_im, log_dt, conj, reverse, seg):
    dt = jnp.exp(log_dt)[:, None]
    lr = (lam_re * dt).reshape(1, NST)
    li = (lam_im * dt).reshape(1, NST) * (-1.0 if conj else 1.0)
    t = jnp.arange(8, dtype=F32)[:, None]

    def power(kk):
        mag = jnp.exp(kk * lr)
        return mag * jnp.cos(kk * li), mag * jnp.sin(kk * li)

    ones = jnp.ones((8, 1), F32)
    a8 = jnp.stack(power(ones)).astype(F32)
    tabs = []
    for s in (1, 2, 4):
        mask = (t <= 7 - s) if reverse else (t >= s)
        pr, pi = power(float(s * seg) * ones)
        tabs += [jnp.where(mask, pr, 0.0), jnp.where(mask, pi, 0.0)]
    kk = ((8.0 - t) if reverse else (t + 1.0)) * float(seg)
    pr, pi = power(kk)
    tabs += [pr, pi]
    return a8, jnp.stack(tabs).astype(F32)


def _pack_b(bb):
    t = bb.transpose(0, 2, 1).reshape(4, 8, SCH, SP)
    eye = jnp.eye(8, dtype=bb.dtype)
    return jnp.einsum('tgcp,gh->tgchp', t, eye).reshape(SW, ST_T)


def _pack_c(cc):
    t = cc.transpose(0, 2, 1).reshape(4, 8, SP, SCH)
    eye = jnp.eye(8, dtype=cc.dtype)
    return jnp.einsum('tgpc,gh->tgphc', t, eye).reshape(NST, CH_T)


def _unpack_diag(acc):
    t = acc.reshape(4, 8, SCH, 8, SP)
    eye = jnp.eye(8, dtype=acc.dtype)
    return jnp.einsum('tgchp,gh->tgcp', t, eye).reshape(SGRP, SCH, SP)


def _gelu(y):
    k0 = math.sqrt(2.0 / math.pi)
    inner = k0 * (y + 0.044715 * y * y * y)
    th = jnp.tanh(inner)
    z = 0.5 * y * (1.0 + th)
    dz = 0.5 * (1.0 + th) + 0.5 * y * (1.0 - th * th) * k0 * (1.0 + 3.0 * 0.044715 * y * y)
    return z, dz


Q0, K0, V0, U0, GS0, GA0, IN_COLS = 0, 1024, 1280, 1536, 2048, 3072, 4096


def mixer_forward(tag, h, p, tm):
    g, winT, wglu, wbsT, wba, wout = p["g"], p["winT"], p["wglu"], p["wbsT"], p["wba"], p["wout"]

    def proj(i, hv, g_ref, w_ref):
        _, _, n = _rms_fwd(hv, g_ref[...])
        nb = n.astype(BF16)
        return (nb, _nt(nb, w_ref[Q0:K0, :]) * QSCALE, _nt(nb, w_ref[K0:V0, :]), _nt(nb, w_ref[V0:U0, :]),
                _nt(nb, w_ref[U0:GS0, :]), _nt(nb, w_ref[GS0:GA0, :]), _nt(nb, w_ref[GA0:IN_COLS, :]))

    n, q, k, v, u, gs, ga = rowcall(
        tag + "_proj", proj, [h], [g, winT],
        [(D, BF16), (D, BF16), (N_KV * HD, BF16), (N_KV * HD, BF16), (SW, F32), (D, BF16), (D, BF16)],
        tm=_tall_tile(h.shape[0], tm))

    ya = attention_forward(tag, q, k, v, p["sink"])

    states, ydir = [], []
    for dr in range(2):
        s = p["ssm"][dr]
        xre, xim, yd = ssm_dir_forward(f"{tag}_ssm_fwd{dr}", u, s["bpr"], s["bpi"], s["cpr"], s["cpi"],
                                       s["a8"], s["tab"], dr == 1, tm)
        states.append((xre, xim))
        ydir.append(yd)

    def merge(i, y0, y1, uv, yav, gsv, gav, hv, d_ref, wglu_ref, wbs_ref, wba_ref, wout_ref):
        ypre = y0 + y1 + d_ref[...] * uv
        z, _ = _gelu(ypre)
        zb = z.astype(BF16)
        t = _nn(zb, wglu_ref[...])
        ysb = (z * _sig(t)).astype(BF16)
        bs = _nt(ysb, wbs_ref[...])
        ba = _nn(yav, wba_ref[...])
        mg = _sig(gsv.astype(F32)) * bs + _sig(gav.astype(F32)) * ba
        mg = jnp.where(_row_ok(i, tm), mg, 0.0).astype(BF16)
        return ypre, zb, t, ysb, bs, ba, mg, hv + _nn(mg, wout_ref[...])

    ypre, zb, t, ys, bs, ba, mg, h2 = rowcall(
        tag + "_merge", merge, [ydir[0], ydir[1], u, ya, gs, ga, h], [p["d"], wglu, wbsT, wba, wout],
        [(SW, BF16), (SW, BF16), (SW, BF16), (SW, BF16), (D, BF16), (D, BF16), (D, BF16), (D, F32)], tm=tm)
    saved = dict(h=h, n=n, q=q, k=k, v=v, u=u, gs=gs, ga=ga, ya=ya, states=states, ypre=ypre, zb=zb, t=t,
                 ys=ys, bs=bs, ba=ba, mg=mg)
    return h2, saved


def mixer_backward(tag, dh, sv, p, tm):
    g, winT, wglu, wbsT, wba, wout = p["g"], p["winT"], p["wglu"], p["wbsT"], p["wba"], p["wout"]

    def y1(i, dhv, bsv, bav, gsv, gav, ypv, tv, uv, wout_ref, wbs_ref, wba_ref, d_ref, wglu_ref):
        dhb = dhv.astype(BF16)
        dmg = _nt(dhb, wout_ref[...])
        dmg = jnp.where(_row_ok(i, tm), dmg, 0.0)
        sgs = _sig(gsv.astype(F32))
        sga = _sig(gav.astype(F32))
        dbs = (dmg * sgs).astype(BF16)
        dba = (dmg * sga).astype(BF16)
        dgs = dmg * bsv.astype(F32) * sgs * (1.0 - sgs)
        dga = dmg * bav.astype(F32) * sga * (1.0 - sga)
        dys = _nn(dbs, wbs_ref[...])
        dya = _nt(dba, wba_ref[...])
        z, dz_dy = _gelu(ypv.astype(F32))
        st = _sig(tv.astype(F32))
        dt_ = dys * z * st * (1.0 - st)
        dz = dys * st + _nt(dt_.astype(BF16), wglu_ref[...])
        dyp = dz * dz_dy
        return (dbs, dba, dgs, dga, dhb, dya, dyp, dyp * d_ref[...], dt_,
                jnp.sum(dyp * uv, axis=0, keepdims=True))

    dbs, dba, dgs, dga, dhb, dya, dypb, du0, dtb, dd = rowcall(
        tag + "_bwd_merge", y1,
        [dh, sv["bs"], sv["ba"], sv["gs"], sv["ga"], sv["ypre"], sv["t"], sv["u"]],
        [wout, wbsT, wba, p["d"], wglu],
        [(D, BF16)] * 6 + [(SW, F32), (SW, BF16), (SW, BF16)], [(1, SW)], tm=tm)
    dwout = tn_matmul(tag + "_dwout", sv["mg"], dhb)
    dwbsT = tn_matmul(tag + "_dwbs", dbs, sv["ys"])
    dwba = tn_matmul(tag + "_dwba", sv["ya"], dba)
    dwglu = tn_matmul(tag + "_dwglu", sv["zb"], dtb)

    du_dirs, ssm_sums = [], []
    for dr in range(2):
        s = p["ssm"][dr]
        xre, xim = sv["states"][dr]
        res = ssm_dir_backward(f"{tag}_ssm_bwd{dr}", dypb, xre, xim, sv["u"], s["bpr"], s["bpi"], s["cpr"],
                               s["cpi"], s["a8_adj"], s["tab_adj"], dr == 0, tm)
        du_dirs.append(res[0])
        ssm_sums.append(res[1:])

    dq, dk, dv, dsink = attention_backward(tag, sv["q"], sv["k"], sv["v"], dya, p["sink"])

    def x1b(i, dqv, dkv, dvv, du0v, du1v, du2v, dgsv, dgav, hv, dhv, g_ref, w_ref):
        dub = (du0v.astype(F32) + du1v.astype(F32) + du2v.astype(F32)).astype(BF16)
        dn = (_nn(dqv, w_ref[Q0:K0, :]) + _nn(dkv, w_ref[K0:V0, :]) + _nn(dvv, w_ref[V0:U0, :])
              + _nn(dub, w_ref[U0:GS0, :]) + _nn(dgsv, w_ref[GS0:GA0, :]) + _nn(dgav, w_ref[GA0:IN_COLS, :]))
        gv = g_ref[...]
        hh, r, _ = _rms_fwd(hv, gv)
        dx, dg = _rms_bwd(hh, r, gv, dn)
        dx = jnp.where(_row_ok(i, tall), dx, 0.0)
        return dhv + dx, dub, dg

    tall = _tall_tile(dh.shape[0], tm)
    dh2, dub, dg = rowcall(tag + "_bwd_in", x1b,
                           [dq, dk, dv, du0, du_dirs[0], du_dirs[1], dgs, dga, sv["h"], dh], [g, winT],
                           [(D, F32), (SW, BF16)], [(1, D)], tm=tall)
    n = sv["n"]
    dwinT = jnp.concatenate([tn_matmul(f"{tag}_dwin{j}", piece, n)
                             for j, piece in enumerate((dq, dk, dv, dub, dgs, dga))], axis=0)
    grads = dict(g=dg, d=dd, sink=dsink, ssm=ssm_sums, winT=dwinT, wglu=dwglu, wbsT=dwbsT, wba=dwba, wout=dwout)
    return dh2, grads


W1024 = ("f1_wgT", "f1_wuT", "f1_wd", "winT", "wba", "wout", "f2_wgT", "f2_wuT", "f2_wd")
W512 = ("wglu", "wbsT")
PART_F1 = ("f1_wgT", "f1_wuT", "f1_wd")
PART_MIX = ("winT", "wba", "wout", "wglu", "wbsT")
PART_F2 = ("f2_wgT", "f2_wuT", "f2_wd")
PER_LAYER_SMALL = ("ffn1_norm", "mix_norm", "ffn2_norm", "ssm_lam_re", "ssm_lam_im", "ssm_log_dt",
                   "ssm_b_re", "ssm_b_im", "ssm_c_re", "ssm_c_im", "ssm_d", "attn_sink")
EARLY_SMALL = tuple(k for k in PER_LAYER_SMALL if k != "ffn1_norm")
SMALL = ("ffn1_norm", "mix_norm", "ffn2_norm", "final_norm", "ssm_lam_re", "ssm_lam_im", "ssm_log_dt",
         "ssm_b_re", "ssm_b_im", "ssm_c_re", "ssm_c_im", "ssm_d", "attn_sink")


def kernel(x, meta_tokens, ffn1_norm, ffn1_w_gate, ffn1_w_up, ffn1_w_down, mix_norm, w_in, ssm_lam_re, ssm_lam_im, ssm_log_dt, ssm_b_re, ssm_b_im, ssm_c_re, ssm_c_im, ssm_d, ssm_w_glu, attn_sink, w_branch_ssm, w_branch_attn, w_out, ffn2_norm, ffn2_w_gate, ffn2_w_up, ffn2_w_down, final_norm, loss_target, m_meta_tokens, m_ffn1_norm, m_ffn1_w_gate, m_ffn1_w_up, m_ffn1_w_down, m_mix_norm, m_w_in, m_ssm_lam_re, m_ssm_lam_im, m_ssm_log_dt, m_ssm_b_re, m_ssm_b_im, m_ssm_c_re, m_ssm_c_im, m_ssm_d, m_ssm_w_glu, m_attn_sink, m_w_branch_ssm, m_w_branch_attn, m_w_out, m_ffn2_norm, m_ffn2_w_gate, m_ffn2_w_up, m_ffn2_w_down, m_final_norm, v_meta_tokens, v_ffn1_norm, v_ffn1_w_gate, v_ffn1_w_up, v_ffn1_w_down, v_mix_norm, v_w_in, v_ssm_lam_re, v_ssm_lam_im, v_ssm_log_dt, v_ssm_b_re, v_ssm_b_im, v_ssm_c_re, v_ssm_c_im, v_ssm_d, v_ssm_w_glu, v_attn_sink, v_w_branch_ssm, v_w_branch_attn, v_w_out, v_ffn2_norm, v_ffn2_w_gate, v_ffn2_w_up, v_ffn2_w_down, v_final_norm):
    weights = dict(meta_tokens=meta_tokens, ffn1_norm=ffn1_norm, ffn1_w_gate=ffn1_w_gate, ffn1_w_up=ffn1_w_up, ffn1_w_down=ffn1_w_down, mix_norm=mix_norm, w_in=w_in, ssm_lam_re=ssm_lam_re, ssm_lam_im=ssm_lam_im, ssm_log_dt=ssm_log_dt, ssm_b_re=ssm_b_re, ssm_b_im=ssm_b_im, ssm_c_re=ssm_c_re, ssm_c_im=ssm_c_im, ssm_d=ssm_d, ssm_w_glu=ssm_w_glu, attn_sink=attn_sink, w_branch_ssm=w_branch_ssm, w_branch_attn=w_branch_attn, w_out=w_out, ffn2_norm=ffn2_norm, ffn2_w_gate=ffn2_w_gate, ffn2_w_up=ffn2_w_up, ffn2_w_down=ffn2_w_down, final_norm=final_norm)
    mom_m = dict(meta_tokens=m_meta_tokens, ffn1_norm=m_ffn1_norm, ffn1_w_gate=m_ffn1_w_gate, ffn1_w_up=m_ffn1_w_up, ffn1_w_down=m_ffn1_w_down, mix_norm=m_mix_norm, w_in=m_w_in, ssm_lam_re=m_ssm_lam_re, ssm_lam_im=m_ssm_lam_im, ssm_log_dt=m_ssm_log_dt, ssm_b_re=m_ssm_b_re, ssm_b_im=m_ssm_b_im, ssm_c_re=m_ssm_c_re, ssm_c_im=m_ssm_c_im, ssm_d=m_ssm_d, ssm_w_glu=m_ssm_w_glu, attn_sink=m_attn_sink, w_branch_ssm=m_w_branch_ssm, w_branch_attn=m_w_branch_attn, w_out=m_w_out, ffn2_norm=m_ffn2_norm, ffn2_w_gate=m_ffn2_w_gate, ffn2_w_up=m_ffn2_w_up, ffn2_w_down=m_ffn2_w_down, final_norm=m_final_norm)
    mom_v = dict(meta_tokens=v_meta_tokens, ffn1_norm=v_ffn1_norm, ffn1_w_gate=v_ffn1_w_gate, ffn1_w_up=v_ffn1_w_up, ffn1_w_down=v_ffn1_w_down, mix_norm=v_mix_norm, w_in=v_w_in, ssm_lam_re=v_ssm_lam_re, ssm_lam_im=v_ssm_lam_im, ssm_log_dt=v_ssm_log_dt, ssm_b_re=v_ssm_b_re, ssm_b_im=v_ssm_b_im, ssm_c_re=v_ssm_c_re, ssm_c_im=v_ssm_c_im, ssm_d=v_ssm_d, ssm_w_glu=v_ssm_w_glu, attn_sink=v_attn_sink, w_branch_ssm=v_w_branch_ssm, w_branch_attn=v_w_branch_attn, w_out=v_w_out, ffn2_norm=v_ffn2_norm, ffn2_w_gate=v_ffn2_w_gate, ffn2_w_up=v_ffn2_w_up, ffn2_w_down=v_ffn2_w_down, final_norm=v_final_norm)
    names = list(weights)

    L0 = x.shape[1]
    LP = L0 + BLK
    tm = 384 if LP % 384 == 0 else BLK
    x_i, y_i, c_i = lax.axis_index("x"), lax.axis_index("y"), lax.axis_index("c")
    me = 4 * x_i + 2 * y_i + c_i

    def canon(l):
        return dict(
            f1_wgT=ffn1_w_gate[l].T, f1_wuT=ffn1_w_up[l].T, f1_wd=ffn1_w_down[l],
            winT=w_in[l].T, wba=w_branch_attn[l], wout=w_out[l],
            f2_wgT=ffn2_w_gate[l].T, f2_wuT=ffn2_w_up[l].T, f2_wd=ffn2_w_down[l],
            wglu=ssm_w_glu[l], wbsT=w_branch_ssm[l].T)

    shards = [{nm: a.astype(BF16) for nm, a in canon(l).items()} for l in range(DEPTH)]

    def rows_of(nm):
        return shards[0][nm].shape[0]

    def width_groups(names_):
        return [g for g in ([nm for nm in names_ if nm in W1024], [nm for nm in names_ if nm in W512]) if g]

    def pieces_for(group):
        out, off = [], 0
        for nm in group:
            out.append((off, rows_of(nm)))
            off += rows_of(nm)
        return out

    def start_gather(tag, l, names_):
        groups = width_groups(names_)
        packed = [jnp.concatenate([shards[l][nm] for nm in g], axis=0) for g in groups]
        return gather_layer_start(tag, packed, [pieces_for(g) for g in groups]), groups

    def finish_gather(tag, l, started_, after):
        handle, groups = started_
        dests = gather_layer_wait(tag, handle, len(groups), after)
        out = {}
        for nm, dest in zip([nm for g in groups for nm in g], dests):
            sh = shards[l][nm]
            out[nm] = lax.dynamic_update_slice(dest, sh, (me * sh.shape[0], 0))
        return out

    g1, gm = all_gather_pieces(
        "gather_weights_first",
        [(jnp.concatenate([shards[0][nm] for nm in PART_F1], axis=0), pieces_for(PART_F1)),
         (meta_tokens, [(0, N_META)])])
    first_weights = dict(zip(PART_F1, g1))
    meta_full = gm[0].reshape(NDEV, N_META, D // NDEV).transpose(1, 0, 2).reshape(N_META, D)
    gather_started = [start_gather("gather_start_l0", 0, PART_MIX + PART_F2)]
    gather_started += [start_gather(f"gather_start_l{l}", l, W1024 + W512) for l in range(1, DEPTH)]
    started = sum(st[0][3][0, 0] for st in gather_started)
    full = [None] * DEPTH

    def disc_all(lre, lim, ldt, bre, bim):
        return _ssm_disc(lre, lim, ldt, bre, bim)

    ssm_p, ssm_vjp = [], []
    for l in range(DEPTH):
        row, vrow = [], []
        for dr in range(2):
            args = (ssm_lam_re[l, dr], ssm_lam_im[l, dr], ssm_log_dt[l, dr], ssm_b_re[l, dr], ssm_b_im[l, dr])
            (a_re, a_im, bb_re, bb_im), vjp = jax.vjp(disc_all, *args)
            a8, tab = _scan_tables(args[0], args[1], args[2], False, dr == 1, tm // 8)
            a8_adj, tab_adj = _scan_tables(args[0], args[1], args[2], True, dr == 0, tm // 8)
            row.append(dict(
                bpr=_pack_b(bb_re).astype(BF16), bpi=_pack_b(bb_im).astype(BF16),
                cpr=_pack_c(ssm_c_re[l, dr]).astype(BF16), cpi=_pack_c(ssm_c_im[l, dr]).astype(BF16),
                a8=a8, tab=tab, a8_adj=a8_adj, tab_adj=tab_adj, a_re=a_re, a_im=a_im))
            vrow.append(vjp)
        ssm_p.append(row)
        ssm_vjp.append(vrow)

    blk0 = jnp.concatenate([jnp.zeros((PAD, D), F32), meta_full.astype(F32)], axis=0)
    h = build_h0(x[0], blk0)
    saved = []
    for l in range(DEPTH):
        w = dict(first_weights) if l == 0 else finish_gather(f"gather_wait_l{l}", l, gather_started[l], h)
        full[l] = w
        g1n, g2n = ffn1_norm[l][None, :], ffn2_norm[l][None, :]
        if l == 0:
            g1n = g1n + started
        h, s1 = ffn_forward("ffn1", h, g1n, w["f1_wgT"], w["f1_wuT"], w["f1_wd"], tm)
        if l == 0:
            w.update(finish_gather("gather_wait_l0", 0, gather_started[0], h))
        mp = dict(g=mix_norm[l][None, :], winT=w["winT"], wglu=w["wglu"], wbsT=w["wbsT"], wba=w["wba"],
                  wout=w["wout"], d=ssm_d[l][None, :], sink=attn_sink[l], ssm=ssm_p[l])
        h, s2 = mixer_forward("mix", h, mp, tm)
        h, s3 = ffn_forward("ffn2", h, g2n, w["f2_wgT"], w["f2_wuT"], w["f2_wd"], tm)
        saved.append((s1, s2, s3, mp, g1n, g2n))

    dh, loss_acc, dgf = final_loss(h, loss_target[0], final_norm[None, :])
    loss = lax.psum(loss_acc[0, 0], MESH_AXES)

    small = {nm: [None] * DEPTH for nm in SMALL if nm != "final_norm"}

    def start_scatter(tag, grads_d, names_):
        groups = width_groups(names_)
        mine = [jnp.concatenate([lax.dynamic_slice_in_dim(grads_d[nm], me * rows_of(nm), rows_of(nm), axis=0)
                                 for nm in g], axis=0) for g in groups]
        handle, nin = scatter_layer_start(tag + "_start", [[grads_d[nm] for nm in g] for g in groups])
        return dict(tag=tag, handle=handle, nin=nin, groups=groups, mine=mine)

    def finish_scatter(st, after):
        lands = scatter_layer_wait(st["tag"] + "_wait", st["handle"], st["nin"], len(st["groups"]), after)
        out = {}
        for land, mine, g in zip(lands, st["mine"], st["groups"]):
            land = lax.dynamic_update_slice(land, mine[None], (me, 0, 0))
            tot = sum_slots(f"sum_weight_grads_{land.shape[1]}x{land.shape[2]}", land)
            for nm, (off_, r) in zip(g, pieces_for(g)):
                out[nm] = tot[off_:off_ + r]
        return out

    def start_small(tag, vec, rows):
        used = sum(v.shape[0] for v in vec)
        flat = jnp.concatenate(vec + [jnp.zeros((rows * D - used,), F32)]).reshape(rows, D)
        return (exchange_start(tag + "_start", [flat, lax.empty((NDEV, rows, D), F32)], 1,
                               lambda refs, me_i, p_i: [(refs[0], refs[1].at[me_i], 0)]), flat, tag)

    def finish_small(started_, after):
        (s_sems, r_sems, arrays, _), flat, tag = started_
        land = exchange_wait(tag + "_wait", s_sems, r_sems, arrays, 1, lambda refs, gi: refs[0], after)[1]
        land = lax.dynamic_update_slice(land, flat[None], (me, 0, 0))
        return sum_slots(f"sum_small_grads_{land.shape[1]}", land).reshape(-1)

    scatters = []
    small_started = [None] * DEPTH
    small_len = sum(math.prod(weights[k].shape[1:]) for k in PER_LAYER_SMALL) + N_META * D
    small_rows = -(-small_len // (8 * D)) * 8
    sent = jnp.zeros((), F32)
    for l in reversed(range(DEPTH)):
        s1, s2, s3, mp, g1n, g2n = saved[l]
        w = full[l]
        dh, dg2, f2g, f2u, f2d = ffn_backward("ffn2", dh, s3, g2n + sent, w["f2_wgT"], w["f2_wuT"], w["f2_wd"], tm)
        st = start_scatter(f"scatter_l{l}_f2", dict(f2_wgT=f2g, f2_wuT=f2u, f2_wd=f2d), PART_F2)
        scatters.append((l, st))
        dh, mg = mixer_backward("mix", dh, s2, dict(mp, d=mp["d"] + st["handle"][3][0, 0]), tm)
        st = start_scatter(f"scatter_l{l}_mix", mg, PART_MIX)
        scatters.append((l, st))
        small["mix_norm"][l] = mg["g"][0]
        small["ffn2_norm"][l] = dg2[0]
        small["ssm_d"][l] = mg["d"][0]
        small["attn_sink"][l] = mg["sink"][0, :N_HEADS]
        per_dir = {k: [] for k in ("ssm_lam_re", "ssm_lam_im", "ssm_log_dt", "ssm_b_re", "ssm_b_im",
                                   "ssm_c_re", "ssm_c_im")}
        for dr in range(2):
            gbr, gbi, gcr, gci, s_re, s_im = mg["ssm"][dr]
            a_re, a_im = ssm_p[l][dr]["a_re"], ssm_p[l][dr]["a_im"]
            s_re = s_re.reshape(SGRP, SP)
            s_im = s_im.reshape(SGRP, SP)
            den = a_re * a_re + a_im * a_im
            ga_re = (s_re * a_re - s_im * a_im) / den
            ga_im = (s_re * a_im + s_im * a_re) / den
            glr, gli, gld, gbre, gbim = ssm_vjp[l][dr]((ga_re, ga_im, _unpack_diag(gbr).transpose(0, 2, 1),
                                                        _unpack_diag(gbi).transpose(0, 2, 1)))
            per_dir["ssm_lam_re"].append(glr)
            per_dir["ssm_lam_im"].append(gli)
            per_dir["ssm_log_dt"].append(gld)
            per_dir["ssm_b_re"].append(gbre)
            per_dir["ssm_b_im"].append(gbim)
            per_dir["ssm_c_re"].append(_unpack_diag(gcr))
            per_dir["ssm_c_im"].append(_unpack_diag(gci))
        for k, vlist in per_dir.items():
            small[k][l] = jnp.stack(vlist)
        early = None
        if l == 0:
            early = start_small("small_grads_l0a", [small[k][l].reshape(-1) for k in EARLY_SMALL], small_rows)
        tok = st["handle"][3][0, 0] + (early[0][3][0, 0] if early is not None else 0.0)
        if l > 0:
            dh, dg1, f1g, f1u, f1d = ffn_backward("ffn1", dh, s1, g1n + tok, w["f1_wgT"], w["f1_wuT"], w["f1_wd"], tm)
            st = start_scatter(f"scatter_l{l}_f1", dict(f1_wgT=f1g, f1_wuT=f1u, f1_wd=f1d), PART_F1)
            scatters.append((l, st))
            sent = st["handle"][3][0, 0]
        else:
            def emit(nm, arr, l=l):
                scatters.append((l, start_scatter(f"scatter_l{l}_f1_{nm}", {"f1_" + nm: arr}, ("f1_" + nm,))))

            dh, dg1, _, _, _ = ffn_backward("ffn1", dh, s1, g1n + tok, w["f1_wgT"], w["f1_wuT"], w["f1_wd"], tm,
                                            emit=emit)
        small["ffn1_norm"][l] = dg1[0]
        if l == 0:
            small_started[l] = (early, start_small("small_grads_l0b", [dg1[0], dh[PAD:BLK].reshape(-1)], 24))
        else:
            vec = [small[k][l].reshape(-1) for k in PER_LAYER_SMALL]
            if l == DEPTH - 1:
                vec.append(dgf[0])
            small_started[l] = start_small(f"small_grads_l{l}", vec, small_rows)
            sent = sent + small_started[l][0][3][0, 0]

    grad_x = dh[BLK:][None]

    def take(tot, keys, l, dst):
        o = 0
        for k in keys:
            shp = weights[k].shape[1:]
            sz = math.prod(shp)
            dst[k][l] = tot[o:o + sz].reshape(shp)
            o += sz
        return o

    grads = {k: [None] * DEPTH for k in PER_LAYER_SMALL}
    for l in reversed(range(1, DEPTH)):
        tot = finish_small(small_started[l], dh)
        o = take(tot, PER_LAYER_SMALL, l, grads)
        if l == DEPTH - 1:
            final_norm_grad = tot[o:o + D]
    take(finish_small(small_started[0][0], dh), EARLY_SMALL, 0, grads)
    tot = finish_small(small_started[0][1], dh)
    grads["ffn1_norm"][0] = tot[0:D]
    dmeta_full = tot[D:D + N_META * D].reshape(N_META, D)
    grads = {k: jnp.stack(vv) for k, vv in grads.items()}
    grads["final_norm"] = final_norm_grad
    grads["meta_tokens"] = lax.dynamic_slice_in_dim(dmeta_full, me * (D // NDEV), D // NDEV, axis=1)

    own = [dict() for _ in range(DEPTH)]
    for l, st in scatters:
        own[l].update(finish_scatter(st, dh))

    def stack(fn):
        return jnp.stack([fn(own[l]) for l in range(DEPTH)])

    grads["ffn1_w_gate"] = stack(lambda d: d["f1_wgT"].T)
    grads["ffn1_w_up"] = stack(lambda d: d["f1_wuT"].T)
    grads["ffn1_w_down"] = stack(lambda d: d["f1_wd"])
    grads["w_in"] = stack(lambda d: d["winT"].T)
    grads["ssm_w_glu"] = stack(lambda d: d["wglu"])
    grads["w_branch_ssm"] = stack(lambda d: d["wbsT"].T)
    grads["w_branch_attn"] = stack(lambda d: d["wba"])
    grads["w_out"] = stack(lambda d: d["wout"])
    grads["ffn2_w_gate"] = stack(lambda d: d["f2_wgT"].T)
    grads["ffn2_w_up"] = stack(lambda d: d["f2_wuT"].T)
    grads["ffn2_w_down"] = stack(lambda d: d["f2_wd"])

    deltas, new_m, new_v = {}, {}, {}
    for nm in names:
        deltas[nm], new_m[nm], new_v[nm] = adamw("adamw_" + nm, weights[nm], grads[nm], mom_m[nm], mom_v[nm])

    return (loss, grad_x, *[grads[n] for n in names], *[deltas[n] for n in names],
            *[new_m[n] for n in names], *[new_v[n] for n in names])
```
